```python
import math
import jax, jax.numpy as jnp
from jax import lax
import numpy as np

D_MODEL = 1024
BATCH = 16
SEQ = 4096
DEPTH = 2

N_MIXERS = 2
N_META = 16
N_HEADS = 16
N_KV_HEADS = 4
HEAD_DIM = 64
GQA_GROUP = N_HEADS // N_KV_HEADS
WINDOW = 128
BLOCK = 128
ALIBI_MAX_EXP = 8.0
QKV_DIM = (N_HEADS + 2 * N_KV_HEADS) * HEAD_DIM
SSM_GROUP = 16
SSM_N_GROUPS = D_MODEL // SSM_GROUP
SSM_STATE = 64
DT_MIN = 1e-3
DT_MAX = 1e-1
LAMBDA_RE_MAX = -1e-4
D_FF = 4 * D_MODEL
RMS_EPS = 1e-6
NEG_INF = -1e30
N_ATTN_LAYERS = (DEPTH + 1) // 2
N_SSM_LAYERS = DEPTH // 2

kernel_name = "hybrid_swa_sink_alibi_s5_sqrelu_meta"


def rms_norm(x, w):
    xf = x.astype(jnp.float32)
    y = xf * lax.rsqrt(jnp.mean(xf * xf, axis=-1, keepdims=True) + RMS_EPS)
    return (y * w.astype(jnp.float32)).astype(x.dtype)


def alibi_slopes():
    h = jnp.arange(1, N_HEADS + 1, dtype=jnp.float32)
    return jnp.exp2(-ALIBI_MAX_EXP * h / N_HEADS)


def swa_sink_attention(h, norm_w, w_qkv, sinks, w_o):
    b, l, _ = h.shape
    pad = (-N_META) % BLOCK
    lp = l + pad
    nb = lp // BLOCK
    hn = rms_norm(h, norm_w)
    qkv = hn @ w_qkv
    q, k, v = jnp.split(qkv, [N_HEADS * HEAD_DIM, (N_HEADS + N_KV_HEADS) * HEAD_DIM], axis=-1)
    q = q.reshape(b, l, N_KV_HEADS, GQA_GROUP, HEAD_DIM) * (HEAD_DIM ** -0.5)
    k = k.reshape(b, l, N_KV_HEADS, HEAD_DIM)
    v = v.reshape(b, l, N_KV_HEADS, HEAD_DIM)
    k_meta, v_meta = k[:, :N_META], v[:, :N_META]

    def front_pad(t):
        return jnp.pad(t, [(0, 0), (pad, 0)] + [(0, 0)] * (t.ndim - 2))

    qb = front_pad(q).reshape(b, nb, BLOCK, N_KV_HEADS, GQA_GROUP, HEAD_DIM)
    kb = front_pad(k).reshape(b, nb, BLOCK, N_KV_HEADS, HEAD_DIM)
    vb = front_pad(v).reshape(b, nb, BLOCK, N_KV_HEADS, HEAD_DIM)

    def band(t):
        prev = jnp.pad(t, [(0, 0), (1, 0)] + [(0, 0)] * (t.ndim - 2))[:, :-1]
        return jnp.concatenate([prev, t], axis=2)

    k_band, v_band = band(kb), band(vb)

    blk = jnp.arange(nb)[:, None, None]
    qi = jnp.arange(BLOCK)[None, :, None]
    kj = jnp.arange(2 * BLOCK)[None, None, :]
    q_pos = blk * BLOCK + qi - pad
    k_pos = (blk - 1) * BLOCK + kj - pad
    dist = q_pos - k_pos
    band_ok = (dist >= 0) & (dist < WINDOW) & (k_pos >= N_META)
    meta_ok = jnp.arange(N_META)[None, None, :] <= q_pos

    slopes = alibi_slopes().reshape(N_KV_HEADS, GQA_GROUP)[:, :, None, None]
    alibi = -slopes * dist[:, None, None].astype(jnp.float32)

    s_band = jnp.einsum('bnqkgd,bnskd->bnkgqs', qb, k_band).astype(jnp.float32)
    s_band = jnp.where(band_ok[:, None, None], s_band + alibi, NEG_INF)
    s_meta = jnp.einsum('bnqkgd,bmkd->bnkgqm', qb, k_meta).astype(jnp.float32)
    s_meta = jnp.where(meta_ok[:, None, None], s_meta, NEG_INF)
    sink = jnp.broadcast_to(
        sinks.astype(jnp.float32).reshape(N_KV_HEADS, GQA_GROUP)[None, None, :, :, None, None],
        (b, nb, N_KV_HEADS, GQA_GROUP, BLOCK, 1))
    probs = jax.nn.softmax(jnp.concatenate([s_meta, s_band, sink], axis=-1), axis=-1)
    p_meta = probs[..., :N_META].astype(v.dtype)
    p_band = probs[..., N_META:N_META + 2 * BLOCK].astype(v.dtype)
    out = (jnp.einsum('bnkgqs,bnskd->bnqkgd', p_band, v_band)
           + jnp.einsum('bnkgqm,bmkd->bnqkgd', p_meta, v_meta))
    out = out.reshape(b, lp, N_HEADS * HEAD_DIM)[:, pad:]
    return out @ w_o


def _linear_recurrence_combine(left, right):
    a_i, b_i = left
    a_j, b_j = right
    return (a_j * a_i, a_j * b_i + b_j)


def s5_mixer(h, norm_w, lam_re, lam_im, log_dt, b_re, b_im, c_re, c_im, d_skip, w_glu):
    b, l, _ = h.shape
    u = rms_norm(h, norm_w).astype(jnp.float32).reshape(b, l, SSM_N_GROUPS, SSM_GROUP)
    lam = lax.complex(jnp.minimum(lam_re.astype(jnp.float32), LAMBDA_RE_MAX),
                      lam_im.astype(jnp.float32))
    dt = jnp.exp(log_dt.astype(jnp.float32))[:, None]
    a_bar = jnp.exp(lam * dt)
    b_c = lax.complex(b_re.astype(jnp.float32), b_im.astype(jnp.float32))
    c_c = lax.complex(c_re.astype(jnp.float32), c_im.astype(jnp.float32))
    b_bar = ((a_bar - 1.0) / lam)[:, :, None] * b_c
    bu = jnp.einsum('blgc,gpc->blgp', u.astype(jnp.complex64), b_bar)
    a_seq = jnp.broadcast_to(a_bar[None, None], (1, l, SSM_N_GROUPS, SSM_STATE))
    _, states = lax.associative_scan(_linear_recurrence_combine, (a_seq, bu), axis=1)
    y = jnp.real(jnp.einsum('blgp,gcp->blgc', states, c_c)) \
        + d_skip.astype(jnp.float32).reshape(SSM_N_GROUPS, SSM_GROUP) * u
    y = jax.nn.gelu(y.reshape(b, l, D_MODEL)).astype(h.dtype)
    val, gate = jnp.split(y @ w_glu, 2, axis=-1)
    return val * jax.nn.sigmoid(gate)


def sq_relu_mlp(h, norm_w, w_up, w_down):
    a = jax.nn.relu(rms_norm(h, norm_w) @ w_up)
    return (a * a) @ w_down


def _fwd_setup_inputs(seed: int = 0) -> dict:
    key = jax.random.key(seed)
    ks = jax.random.split(key, 24)
    f32 = jnp.float32
    na, ns, G, P, C = N_ATTN_LAYERS, N_SSM_LAYERS, SSM_N_GROUPS, SSM_STATE, SSM_GROUP
    x = jax.random.normal(ks[0], (BATCH, SEQ, D_MODEL), f32)
    meta_tokens = jax.random.normal(ks[1], (N_META, D_MODEL), f32)
    attn_norm_w = 1.0 + 0.02 * jax.random.normal(ks[2], (na, D_MODEL), f32)
    attn_w_qkv = jax.random.normal(ks[3], (na, D_MODEL, QKV_DIM), f32) * D_MODEL ** -0.5
    attn_sinks = 0.5 * jax.random.normal(ks[4], (na, N_HEADS), f32)
    attn_w_o = jax.random.normal(ks[5], (na, N_HEADS * HEAD_DIM, D_MODEL), f32) * (N_HEADS * HEAD_DIM) ** -0.5
    ssm_norm_w = 1.0 + 0.02 * jax.random.normal(ks[6], (ns, D_MODEL), f32)
    ssm_lambda_re = -0.5 + 0.01 * jax.random.normal(ks[7], (ns, G, P), f32)
    ssm_lambda_im = jnp.broadcast_to(jnp.pi * jnp.arange(P, dtype=f32), (ns, G, P)) \
        + 0.01 * jax.random.normal(ks[8], (ns, G, P), f32)
    ssm_log_dt = jax.random.uniform(ks[9], (ns, G), f32, math.log(DT_MIN), math.log(DT_MAX))
    ssm_b_re = jax.random.normal(ks[10], (ns, G, P, C), f32) * (2.0 * C) ** -0.5
    ssm_b_im = jax.random.normal(ks[11], (ns, G, P, C), f32) * (2.0 * C) ** -0.5
    ssm_c_re = jax.random.normal(ks[12], (ns, G, C, P), f32) * (2.0 * P) ** -0.5
    ssm_c_im = jax.random.normal(ks[13], (ns, G, C, P), f32) * (2.0 * P) ** -0.5
    ssm_d = jax.random.normal(ks[14], (ns, D_MODEL), f32)
    ssm_w_glu = jax.random.normal(ks[15], (ns, D_MODEL, 2 * D_MODEL), f32) * D_MODEL ** -0.5
    mlp_norm_w = 1.0 + 0.02 * jax.random.normal(ks[16], (DEPTH, D_MODEL), f32)
    mlp_w_up = jax.random.normal(ks[17], (DEPTH, D_MODEL, D_FF), f32) * D_MODEL ** -0.5
    mlp_w_down = jax.random.normal(ks[18], (DEPTH, D_FF, D_MODEL), f32) * D_FF ** -0.5
    final_norm_w = 1.0 + 0.02 * jax.random.normal(ks[19], (D_MODEL,), f32)
    return {"x": x, "meta_tokens": meta_tokens,
            "attn_norm_w": attn_norm_w, "attn_w_qkv": attn_w_qkv,
            "attn_sinks": attn_sinks, "attn_w_o": attn_w_o,
            "ssm_norm_w": ssm_norm_w, "ssm_lambda_re": ssm_lambda_re,
            "ssm_lambda_im": ssm_lambda_im, "ssm_log_dt": ssm_log_dt,
            "ssm_b_re": ssm_b_re, "ssm_b_im": ssm_b_im,
            "ssm_c_re": ssm_c_re, "ssm_c_im": ssm_c_im,
            "ssm_d": ssm_d, "ssm_w_glu": ssm_w_glu,
            "mlp_norm_w": mlp_norm_w, "mlp_w_up": mlp_w_up, "mlp_w_down": mlp_w_down,
            "final_norm_w": final_norm_w}


def _fwd_reference(x, meta_tokens, attn_norm_w, attn_w_qkv, attn_sinks, attn_w_o,
              ssm_norm_w, ssm_lambda_re, ssm_lambda_im, ssm_log_dt,
              ssm_b_re, ssm_b_im, ssm_c_re, ssm_c_im, ssm_d, ssm_w_glu,
              mlp_norm_w, mlp_w_up, mlp_w_down, final_norm_w):
    b = x.shape[0]
    meta = jnp.broadcast_to(meta_tokens.astype(x.dtype)[None], (b, N_META, D_MODEL))
    h = jnp.concatenate([meta, x], axis=1)
    for i in range(DEPTH):
        j = i // N_MIXERS
        if i % N_MIXERS == 0:
            h = h + swa_sink_attention(h, attn_norm_w[j], attn_w_qkv[j], attn_sinks[j], attn_w_o[j])
        else:
            h = h + s5_mixer(h, ssm_norm_w[j], ssm_lambda_re[j], ssm_lambda_im[j], ssm_log_dt[j],
                             ssm_b_re[j], ssm_b_im[j], ssm_c_re[j], ssm_c_im[j], ssm_d[j], ssm_w_glu[j])
        h = h + sq_relu_mlp(h, mlp_norm_w[i], mlp_w_up[i], mlp_w_down[i])
    return rms_norm(h[:, N_META:], final_norm_w)


import jax as _jax
import jax.numpy as _jnp

TWIN_FORMAT = 'train_step'
FWD_PARAMS = ['x', 'meta_tokens', 'attn_norm_w', 'attn_w_qkv', 'attn_sinks', 'attn_w_o', 'ssm_norm_w', 'ssm_lambda_re', 'ssm_lambda_im', 'ssm_log_dt', 'ssm_b_re', 'ssm_b_im', 'ssm_c_re', 'ssm_c_im', 'ssm_d', 'ssm_w_glu', 'mlp_norm_w', 'mlp_w_up', 'mlp_w_down', 'final_norm_w']
TWIN_WEIGHTS = ['meta_tokens', 'attn_norm_w', 'attn_w_qkv', 'attn_sinks', 'attn_w_o', 'ssm_norm_w', 'ssm_lambda_re', 'ssm_lambda_im', 'ssm_log_dt', 'ssm_b_re', 'ssm_b_im', 'ssm_c_re', 'ssm_c_im', 'ssm_d', 'ssm_w_glu', 'mlp_norm_w', 'mlp_w_up', 'mlp_w_down', 'final_norm_w']
TWIN_DIFF_INPUT = 'x'
TWIN_INPUTS = ['x', 'meta_tokens', 'attn_norm_w', 'attn_w_qkv', 'attn_sinks', 'attn_w_o', 'ssm_norm_w', 'ssm_lambda_re', 'ssm_lambda_im', 'ssm_log_dt', 'ssm_b_re', 'ssm_b_im', 'ssm_c_re', 'ssm_c_im', 'ssm_d', 'ssm_w_glu', 'mlp_norm_w', 'mlp_w_up', 'mlp_w_down', 'final_norm_w', 'loss_target', 'm_meta_tokens', 'm_attn_norm_w', 'm_attn_w_qkv', 'm_attn_sinks', 'm_attn_w_o', 'm_ssm_norm_w', 'm_ssm_lambda_re', 'm_ssm_lambda_im', 'm_ssm_log_dt', 'm_ssm_b_re', 'm_ssm_b_im', 'm_ssm_c_re', 'm_ssm_c_im', 'm_ssm_d', 'm_ssm_w_glu', 'm_mlp_norm_w', 'm_mlp_w_up', 'm_mlp_w_down', 'm_final_norm_w', 'v_meta_tokens', 'v_attn_norm_w', 'v_attn_w_qkv', 'v_attn_sinks', 'v_attn_w_o', 'v_ssm_norm_w', 'v_ssm_lambda_re', 'v_ssm_lambda_im', 'v_ssm_log_dt', 'v_ssm_b_re', 'v_ssm_b_im', 'v_ssm_c_re', 'v_ssm_c_im', 'v_ssm_d', 'v_ssm_w_glu', 'v_mlp_norm_w', 'v_mlp_w_up', 'v_mlp_w_down', 'v_final_norm_w']
TWIN_OUTPUTS = ['loss', 'grad_x', 'grad_meta_tokens', 'grad_attn_norm_w', 'grad_attn_w_qkv', 'grad_attn_sinks', 'grad_attn_w_o', 'grad_ssm_norm_w', 'grad_ssm_lambda_re', 'grad_ssm_lambda_im', 'grad_ssm_log_dt', 'grad_ssm_b_re', 'grad_ssm_b_im', 'grad_ssm_c_re', 'grad_ssm_c_im', 'grad_ssm_d', 'grad_ssm_w_glu', 'grad_mlp_norm_w', 'grad_mlp_w_up', 'grad_mlp_w_down', 'grad_final_norm_w', 'delta_meta_tokens', 'delta_attn_norm_w', 'delta_attn_w_qkv', 'delta_attn_sinks', 'delta_attn_w_o', 'delta_ssm_norm_w', 'delta_ssm_lambda_re', 'delta_ssm_lambda_im', 'delta_ssm_log_dt', 'delta_ssm_b_re', 'delta_ssm_b_im', 'delta_ssm_c_re', 'delta_ssm_c_im', 'delta_ssm_d', 'delta_ssm_w_glu', 'delta_mlp_norm_w', 'delta_mlp_w_up', 'delta_mlp_w_down', 'delta_final_norm_w', 'new_m_meta_tokens', 'new_m_attn_norm_w', 'new_m_attn_w_qkv', 'new_m_attn_sinks', 'new_m_attn_w_o', 'new_m_ssm_norm_w', 'new_m_ssm_lambda_re', 'new_m_ssm_lambda_im', 'new_m_ssm_log_dt', 'new_m_ssm_b_re', 'new_m_ssm_b_im', 'new_m_ssm_c_re', 'new_m_ssm_c_im', 'new_m_ssm_d', 'new_m_ssm_w_glu', 'new_m_mlp_norm_w', 'new_m_mlp_w_up', 'new_m_mlp_w_down', 'new_m_final_norm_w', 'new_v_meta_tokens', 'new_v_attn_norm_w', 'new_v_attn_w_qkv', 'new_v_attn_sinks', 'new_v_attn_w_o', 'new_v_ssm_norm_w', 'new_v_ssm_lambda_re', 'new_v_ssm_lambda_im', 'new_v_ssm_log_dt', 'new_v_ssm_b_re', 'new_v_ssm_b_im', 'new_v_ssm_c_re', 'new_v_ssm_c_im', 'new_v_ssm_d', 'new_v_ssm_w_glu', 'new_v_mlp_norm_w', 'new_v_mlp_w_up', 'new_v_mlp_w_down', 'new_v_final_norm_w']
TWIN_LEAF_KINDS = {'loss': 'loss', 'grad_x': 'grad_x', 'grad_meta_tokens': 'grad_w', 'grad_attn_norm_w': 'grad_w', 'grad_attn_w_qkv': 'grad_w', 'grad_attn_sinks': 'grad_w', 'grad_attn_w_o': 'grad_w', 'grad_ssm_norm_w': 'grad_w', 'grad_ssm_lambda_re': 'grad_w', 'grad_ssm_lambda_im': 'grad_w', 'grad_ssm_log_dt': 'grad_w', 'grad_ssm_b_re': 'grad_w', 'grad_ssm_b_im': 'grad_w', 'grad_ssm_c_re': 'grad_w', 'grad_ssm_c_im': 'grad_w', 'grad_ssm_d': 'grad_w', 'grad_ssm_w_glu': 'grad_w', 'grad_mlp_norm_w': 'grad_w', 'grad_mlp_w_up': 'grad_w', 'grad_mlp_w_down': 'grad_w', 'grad_final_norm_w': 'grad_w', 'delta_meta_tokens': 'delta_w', 'delta_attn_norm_w': 'delta_w', 'delta_attn_w_qkv': 'delta_w', 'delta_attn_sinks': 'delta_w', 'delta_attn_w_o': 'delta_w', 'delta_ssm_norm_w': 'delta_w', 'delta_ssm_lambda_re': 'delta_w', 'delta_ssm_lambda_im': 'delta_w', 'delta_ssm_log_dt': 'delta_w', 'delta_ssm_b_re': 'delta_w', 'delta_ssm_b_im': 'delta_w', 'delta_ssm_c_re': 'delta_w', 'delta_ssm_c_im': 'delta_w', 'delta_ssm_d': 'delta_w', 'delta_ssm_w_glu': 'delta_w', 'delta_mlp_norm_w': 'delta_w', 'delta_mlp_w_up': 'delta_w', 'delta_mlp_w_down': 'delta_w', 'delta_final_norm_w': 'delta_w', 'new_m_meta_tokens': 'new_m', 'new_m_attn_norm_w': 'new_m', 'new_m_attn_w_qkv': 'new_m', 'new_m_attn_sinks': 'new_m', 'new_m_attn_w_o': 'new_m', 'new_m_ssm_norm_w': 'new_m', 'new_m_ssm_lambda_re': 'new_m', 'new_m_ssm_lambda_im': 'new_m', 'new_m_ssm_log_dt': 'new_m', 'new_m_ssm_b_re': 'new_m', 'new_m_ssm_b_im': 'new_m', 'new_m_ssm_c_re': 'new_m', 'new_m_ssm_c_im': 'new_m', 'new_m_ssm_d': 'new_m', 'new_m_ssm_w_glu': 'new_m', 'new_m_mlp_norm_w': 'new_m', 'new_m_mlp_w_up': 'new_m', 'new_m_mlp_w_down': 'new_m', 'new_m_final_norm_w': 'new_m', 'new_v_meta_tokens': 'new_v', 'new_v_attn_norm_w': 'new_v', 'new_v_attn_w_qkv': 'new_v', 'new_v_attn_sinks': 'new_v', 'new_v_attn_w_o': 'new_v', 'new_v_ssm_norm_w': 'new_v', 'new_v_ssm_lambda_re': 'new_v', 'new_v_ssm_lambda_im': 'new_v', 'new_v_ssm_log_dt': 'new_v', 'new_v_ssm_b_re': 'new_v', 'new_v_ssm_b_im': 'new_v', 'new_v_ssm_c_re': 'new_v', 'new_v_ssm_c_im': 'new_v', 'new_v_ssm_d': 'new_v', 'new_v_ssm_w_glu': 'new_v', 'new_v_mlp_norm_w': 'new_v', 'new_v_mlp_w_up': 'new_v', 'new_v_mlp_w_down': 'new_v', 'new_v_final_norm_w': 'new_v'}


def _forward(args):
    return _fwd_reference(*[args[k] for k in FWD_PARAMS])


def _output_shape():
    out = _jax.eval_shape(lambda: _forward(_fwd_setup_inputs(0)))
    return out.shape, out.dtype

N_MICROBATCH = 1
ADAM_LR = 0.001
ADAM_B1 = 0.9
ADAM_B2 = 0.999
ADAM_EPS = 1e-08
ADAM_WD = 0.01
ADAM_STEP = 10
PER_EXAMPLE_BATCH_AXIS = {'x': 0, 'loss_target': 0}
SHARED_INPUTS = []
_WEIGHT_DTYPES = {'meta_tokens': _jnp.float32, 'attn_norm_w': _jnp.float32, 'attn_w_qkv': _jnp.float32, 'attn_sinks': _jnp.float32, 'attn_w_o': _jnp.float32, 'ssm_norm_w': _jnp.float32, 'ssm_lambda_re': _jnp.float32, 'ssm_lambda_im': _jnp.float32, 'ssm_log_dt': _jnp.float32, 'ssm_b_re': _jnp.float32, 'ssm_b_im': _jnp.float32, 'ssm_c_re': _jnp.float32, 'ssm_c_im': _jnp.float32, 'ssm_d': _jnp.float32, 'ssm_w_glu': _jnp.float32, 'mlp_norm_w': _jnp.float32, 'mlp_w_up': _jnp.float32, 'mlp_w_down': _jnp.float32, 'final_norm_w': _jnp.float32}
MOMENT_SCALE = {'meta_tokens': 2.076644e-02, 'attn_norm_w': 1.117448e-01, 'attn_w_qkv': 9.084477e-02, 'attn_sinks': 1.397671e-02, 'attn_w_o': 7.148083e-02, 'ssm_norm_w': 7.847305e-02, 'ssm_lambda_re': 5.120136e-03, 'ssm_lambda_im': 5.348140e-03, 'ssm_log_dt': 1.974452e+00, 'ssm_b_re': 2.881991e-03, 'ssm_b_im': 2.960715e-03, 'ssm_c_re': 6.460407e-03, 'ssm_c_im': 6.084705e-03, 'ssm_d': 7.529937e-02, 'ssm_w_glu': 5.103191e-02, 'mlp_norm_w': 2.123308e-01, 'mlp_w_up': 1.046841e-01, 'mlp_w_down': 2.074500e-01, 'final_norm_w': 6.525926e+01}


def _to_microbatches(a, axis):
    t = _jnp.moveaxis(a, axis, 0)
    t = t.reshape((N_MICROBATCH, t.shape[0] // N_MICROBATCH) + t.shape[1:])
    return _jnp.moveaxis(t, 1, axis + 1)


def setup_inputs(seed: int = 0) -> dict:
    inp = _fwd_setup_inputs(seed)
    key = _jax.random.fold_in(_jax.random.key(seed), 7919)
    shape, _ = _output_shape()
    out = dict(inp)
    out["loss_target"] = _jax.random.normal(_jax.random.fold_in(key, 0), shape, _jnp.float32)
    for i, name in enumerate(TWIN_WEIGHTS):
        w = inp[name].astype(_jnp.float32)
        if MOMENT_SCALE is None:
            s = _jnp.sqrt(_jnp.mean(_jnp.square(w)) + 1e-30)
        else:
            s = MOMENT_SCALE[name]
        km, kv = _jax.random.split(_jax.random.fold_in(key, i + 1))
        out[name] = w
        out["m_" + name] = s * _jax.random.normal(km, w.shape, _jnp.float32)
        out["v_" + name] = (s * s) * _jax.random.uniform(kv, w.shape, _jnp.float32, 0.5, 1.5)
    if N_MICROBATCH > 1:
        for name, axis in PER_EXAMPLE_BATCH_AXIS.items():
            out[name] = _to_microbatches(out[name], axis)
    return {'x': out['x'], 'meta_tokens': out['meta_tokens'], 'attn_norm_w': out['attn_norm_w'], 'attn_w_qkv': out['attn_w_qkv'], 'attn_sinks': out['attn_sinks'], 'attn_w_o': out['attn_w_o'], 'ssm_norm_w': out['ssm_norm_w'], 'ssm_lambda_re': out['ssm_lambda_re'], 'ssm_lambda_im': out['ssm_lambda_im'], 'ssm_log_dt': out['ssm_log_dt'], 'ssm_b_re': out['ssm_b_re'], 'ssm_b_im': out['ssm_b_im'], 'ssm_c_re': out['ssm_c_re'], 'ssm_c_im': out['ssm_c_im'], 'ssm_d': out['ssm_d'], 'ssm_w_glu': out['ssm_w_glu'], 'mlp_norm_w': out['mlp_norm_w'], 'mlp_w_up': out['mlp_w_up'], 'mlp_w_down': out['mlp_w_down'], 'final_norm_w': out['final_norm_w'], 'loss_target': out['loss_target'], 'm_meta_tokens': out['m_meta_tokens'], 'm_attn_norm_w': out['m_attn_norm_w'], 'm_attn_w_qkv': out['m_attn_w_qkv'], 'm_attn_sinks': out['m_attn_sinks'], 'm_attn_w_o': out['m_attn_w_o'], 'm_ssm_norm_w': out['m_ssm_norm_w'], 'm_ssm_lambda_re': out['m_ssm_lambda_re'], 'm_ssm_lambda_im': out['m_ssm_lambda_im'], 'm_ssm_log_dt': out['m_ssm_log_dt'], 'm_ssm_b_re': out['m_ssm_b_re'], 'm_ssm_b_im': out['m_ssm_b_im'], 'm_ssm_c_re': out['m_ssm_c_re'], 'm_ssm_c_im': out['m_ssm_c_im'], 'm_ssm_d': out['m_ssm_d'], 'm_ssm_w_glu': out['m_ssm_w_glu'], 'm_mlp_norm_w': out['m_mlp_norm_w'], 'm_mlp_w_up': out['m_mlp_w_up'], 'm_mlp_w_down': out['m_mlp_w_down'], 'm_final_norm_w': out['m_final_norm_w'], 'v_meta_tokens': out['v_meta_tokens'], 'v_attn_norm_w': out['v_attn_norm_w'], 'v_attn_w_qkv': out['v_attn_w_qkv'], 'v_attn_sinks': out['v_attn_sinks'], 'v_attn_w_o': out['v_attn_w_o'], 'v_ssm_norm_w': out['v_ssm_norm_w'], 'v_ssm_lambda_re': out['v_ssm_lambda_re'], 'v_ssm_lambda_im': out['v_ssm_lambda_im'], 'v_ssm_log_dt': out['v_ssm_log_dt'], 'v_ssm_b_re': out['v_ssm_b_re'], 'v_ssm_b_im': out['v_ssm_b_im'], 'v_ssm_c_re': out['v_ssm_c_re'], 'v_ssm_c_im': out['v_ssm_c_im'], 'v_ssm_d': out['v_ssm_d'], 'v_ssm_w_glu': out['v_ssm_w_glu'], 'v_mlp_norm_w': out['v_mlp_norm_w'], 'v_mlp_w_up': out['v_mlp_w_up'], 'v_mlp_w_down': out['v_mlp_w_down'], 'v_final_norm_w': out['v_final_norm_w']}


def _loss(weights, diff, rest, loss_target):
    with _jax.named_scope("forward"):
        args = {**rest, TWIN_DIFF_INPUT: diff, **{k: w.astype(_WEIGHT_DTYPES[k]) for k, w in weights.items()}}
        y = _forward(args)
    with _jax.named_scope("loss_head"):
        err = _jnp.square(y.astype(_jnp.float32) - loss_target)
        return 0.5 * _jnp.sum(_jnp.mean(err, axis=-1)) if err.ndim else 0.5 * err


def _adamw(w, g, m, v):
    m = ADAM_B1 * m + (1.0 - ADAM_B1) * g
    v = ADAM_B2 * v + (1.0 - ADAM_B2) * _jnp.square(g)
    m_hat = m / (1.0 - ADAM_B1 ** ADAM_STEP)
    v_hat = v / (1.0 - ADAM_B2 ** ADAM_STEP)
    delta = -ADAM_LR * (m_hat / (_jnp.sqrt(v_hat) + ADAM_EPS) + ADAM_WD * w)
    return delta, m, v


def reference(x, meta_tokens, attn_norm_w, attn_w_qkv, attn_sinks, attn_w_o, ssm_norm_w, ssm_lambda_re, ssm_lambda_im, ssm_log_dt, ssm_b_re, ssm_b_im, ssm_c_re, ssm_c_im, ssm_d, ssm_w_glu, mlp_norm_w, mlp_w_up, mlp_w_down, final_norm_w, loss_target, m_meta_tokens, m_attn_norm_w, m_attn_w_qkv, m_attn_sinks, m_attn_w_o, m_ssm_norm_w, m_ssm_lambda_re, m_ssm_lambda_im, m_ssm_log_dt, m_ssm_b_re, m_ssm_b_im, m_ssm_c_re, m_ssm_c_im, m_ssm_d, m_ssm_w_glu, m_mlp_norm_w, m_mlp_w_up, m_mlp_w_down, m_final_norm_w, v_meta_tokens, v_attn_norm_w, v_attn_w_qkv, v_attn_sinks, v_attn_w_o, v_ssm_norm_w, v_ssm_lambda_re, v_ssm_lambda_im, v_ssm_log_dt, v_ssm_b_re, v_ssm_b_im, v_ssm_c_re, v_ssm_c_im, v_ssm_d, v_ssm_w_glu, v_mlp_norm_w, v_mlp_w_up, v_mlp_w_down, v_final_norm_w):
    given = dict(x=x, meta_tokens=meta_tokens, attn_norm_w=attn_norm_w, attn_w_qkv=attn_w_qkv, attn_sinks=attn_sinks, attn_w_o=attn_w_o, ssm_norm_w=ssm_norm_w, ssm_lambda_re=ssm_lambda_re, ssm_lambda_im=ssm_lambda_im, ssm_log_dt=ssm_log_dt, ssm_b_re=ssm_b_re, ssm_b_im=ssm_b_im, ssm_c_re=ssm_c_re, ssm_c_im=ssm_c_im, ssm_d=ssm_d, ssm_w_glu=ssm_w_glu, mlp_norm_w=mlp_norm_w, mlp_w_up=mlp_w_up, mlp_w_down=mlp_w_down, final_norm_w=final_norm_w, loss_target=loss_target, m_meta_tokens=m_meta_tokens, m_attn_norm_w=m_attn_norm_w, m_attn_w_qkv=m_attn_w_qkv, m_attn_sinks=m_attn_sinks, m_attn_w_o=m_attn_w_o, m_ssm_norm_w=m_ssm_norm_w, m_ssm_lambda_re=m_ssm_lambda_re, m_ssm_lambda_im=m_ssm_lambda_im, m_ssm_log_dt=m_ssm_log_dt, m_ssm_b_re=m_ssm_b_re, m_ssm_b_im=m_ssm_b_im, m_ssm_c_re=m_ssm_c_re, m_ssm_c_im=m_ssm_c_im, m_ssm_d=m_ssm_d, m_ssm_w_glu=m_ssm_w_glu, m_mlp_norm_w=m_mlp_norm_w, m_mlp_w_up=m_mlp_w_up, m_mlp_w_down=m_mlp_w_down, m_final_norm_w=m_final_norm_w, v_meta_tokens=v_meta_tokens, v_attn_norm_w=v_attn_norm_w, v_attn_w_qkv=v_attn_w_qkv, v_attn_sinks=v_attn_sinks, v_attn_w_o=v_attn_w_o, v_ssm_norm_w=v_ssm_norm_w, v_ssm_lambda_re=v_ssm_lambda_re, v_ssm_lambda_im=v_ssm_lambda_im, v_ssm_log_dt=v_ssm_log_dt, v_ssm_b_re=v_ssm_b_re, v_ssm_b_im=v_ssm_b_im, v_ssm_c_re=v_ssm_c_re, v_ssm_c_im=v_ssm_c_im, v_ssm_d=v_ssm_d, v_ssm_w_glu=v_ssm_w_glu, v_mlp_norm_w=v_mlp_norm_w, v_mlp_w_up=v_mlp_w_up, v_mlp_w_down=v_mlp_w_down, v_final_norm_w=v_final_norm_w)
    weights = {n: given[n] for n in TWIN_WEIGHTS}
    shared = {n: given[n] for n in SHARED_INPUTS}
    per_example = {n: given[n] for n in ['x']}
    grad_fn = _jax.value_and_grad(_loss, argnums=(0, 1))

    def one_microbatch(ex, loss_target):
        ex = dict(ex)
        diff = ex.pop(TWIN_DIFF_INPUT)
        return grad_fn(weights, diff, {**shared, **ex}, loss_target)

    if N_MICROBATCH == 1:
        loss, (grad_w, grad_x) = one_microbatch(per_example, given["loss_target"])
    else:
        def body(carry, xs):
            loss_sum, grad_sum = carry
            l_k, (gw_k, gx_k) = one_microbatch(xs[0], xs[1])
            with _jax.named_scope("update"):
                return (loss_sum + l_k, _jax.tree.map(_jnp.add, grad_sum, gw_k)), gx_k

        init = (_jnp.zeros((), _jnp.float32), _jax.tree.map(_jnp.zeros_like, weights))
        (loss, grad_w), grad_x = _jax.lax.scan(body, init, (per_example, given["loss_target"]))
    with _jax.named_scope("update"):
        delta_w, new_m, new_v = {}, {}, {}
        for n in TWIN_WEIGHTS:
            delta_w[n], new_m[n], new_v[n] = _adamw(weights[n], grad_w[n], given["m_" + n], given["v_" + n])
    return (loss, grad_x, *[grad_w[n] for n in TWIN_WEIGHTS], *[delta_w[n] for n in TWIN_WEIGHTS],
            *[new_m[n] for n in TWIN_WEIGHTS], *[new_v[n] for n in TWIN_WEIGHTS])
```

```python
import functools
import math

import jax
import jax.numpy as jnp
from jax import lax
from jax.experimental import pallas as pl
from jax.experimental.pallas import tpu as pltpu

F32 = jnp.float32
BF16 = jnp.bfloat16
SDS = jax.ShapeDtypeStruct

D_MODEL = 1024
N_HEADS = 16
N_KV = 4
HEAD_DIM = 64
BLOCK = 128
N_META = 16
PAD = BLOCK - N_META
QKV_DIM = (N_HEADS + 2 * N_KV) * HEAD_DIM
KV_DIM = 2 * N_KV * HEAD_DIM
D_FF = 4 * D_MODEL
N_CHIPS = 4
N_DEV = 8
SSM_GROUP = 16
SSM_NG = D_MODEL // SSM_GROUP
SSM_STATE = 64
N_PAIR = SSM_NG // 2
PAIRS_PER_CHUNK = 4
RMS_EPS = 1e-6
NEG_INF = -1e30
LAMBDA_RE_MAX = -1e-4
ADAM_LR, ADAM_B1, ADAM_B2, ADAM_EPS, ADAM_WD, ADAM_STEP = 0.001, 0.9, 0.999, 1e-08, 0.01, 10

TM = 384
VMEM_LIMIT = 56 * 1024 * 1024


def _params(n_grid):
    return pltpu.CompilerParams(dimension_semantics=("arbitrary",) * n_grid, vmem_limit_bytes=VMEM_LIMIT)


def _rms(h, w):
    r = lax.rsqrt(jnp.mean(h * h, axis=-1, keepdims=True) + RMS_EPS)
    return h * r * w


def _rms_bwd(dhn, h, w):
    r = lax.rsqrt(jnp.mean(h * h, axis=-1, keepdims=True) + RMS_EPS)
    g = dhn * w
    proj = jnp.sum(g * h, axis=-1, keepdims=True) * (1.0 / D_MODEL)
    return r * g - h * (r * r * r) * proj, dhn * h * r


def _fold8(t):
    return jnp.sum(t.reshape(t.shape[0] // 8, 8, t.shape[1]), axis=0)


def _gelu(y):
    return 0.5 * y * (1.0 + jnp.tanh(0.7978845608028654 * (y + 0.044715 * y * y * y)))


def _gelu_grad(y):
    t = jnp.tanh(0.7978845608028654 * (y + 0.044715 * y * y * y))
    return 0.5 * (1.0 + t) + 0.5 * y * (1.0 - t * t) * 0.7978845608028654 * (1.0 + 3.0 * 0.044715 * y * y)


def _rms_mm_cols(h, wn, w4, layer, name):
    n_rows = h.shape[0]
    n_sh, _, k, n = w4.shape

    def body(h_ref, wn_ref, w_ref, o_ref, hn_ref, hn_s):
        @pl.when(pl.program_id(1) == 0)
        def _():
            hn = _rms(h_ref[...], wn_ref[...]).astype(BF16)
            hn_s[...] = hn
            hn_ref[...] = hn

        o_ref[...] = jnp.dot(hn_s[...], w_ref[...], preferred_element_type=F32).astype(o_ref.dtype)

    return pl.pallas_call(
        body, name=name, grid=(n_rows // TM, n_sh),
        in_specs=[pl.BlockSpec((TM, k), lambda i, s: (i, 0)),
                  pl.BlockSpec((1, k), lambda i, s: (0, 0)),
                  pl.BlockSpec((None, None, k, n), lambda i, s: (s, layer, 0, 0))],
        out_specs=[pl.BlockSpec((TM, n), lambda i, s: (i, s)),
                   pl.BlockSpec((TM, k), lambda i, s: (i, 0))],
        out_shape=[SDS((n_rows, n_sh * n), BF16), SDS((n_rows, k), BF16)],
        scratch_shapes=[pltpu.VMEM((TM, k), BF16)],
        compiler_params=_params(2),
    )(h, wn, w4)


def _mm_cols(x, w4, layer, trans_w, name, mul2relu=None):
    n_rows, kx = x.shape
    n_sh, _, k, n = w4.shape
    n_out = k if trans_w else n
    dims = (((1,), (1,)), ((), ())) if trans_w else (((1,), (0,)), ((), ()))

    def body(*refs):
        if mul2relu is None:
            x_ref, w_ref, o_ref = refs
        else:
            x_ref, w_ref, a_ref, o_ref = refs
        acc = lax.dot_general(x_ref[...].astype(BF16), w_ref[...], dims, preferred_element_type=F32)
        if mul2relu is not None:
            acc = acc * (2.0 * jnp.maximum(a_ref[...].astype(F32), 0.0))
        o_ref[...] = acc.astype(o_ref.dtype)

    in_specs = [pl.BlockSpec((TM, kx), lambda i, s: (i, 0)),
                pl.BlockSpec((None, None, k, n), lambda i, s: (s, layer, 0, 0))]
    args = [x, w4]
    if mul2relu is not None:
        in_specs.append(pl.BlockSpec((TM, n_out), lambda i, s: (i, s)))
        args.append(mul2relu)
    return pl.pallas_call(
        body, name=name, grid=(n_rows // TM, n_sh),
        in_specs=in_specs,
        out_specs=pl.BlockSpec((TM, n_out), lambda i, s: (i, s)),
        out_shape=SDS((n_rows, n_sh * n_out), BF16),
        compiler_params=_params(2),
    )(*args)


def _mm_acc(x, w4, layer, trans_w, name, res=None, rms_bwd=None, out_dtype=F32):
    n_rows = x.shape[0]
    n_sh, _, k, n = w4.shape
    kx, n_out = (n, k) if trans_w else (k, n)
    dims = (((1,), (1,)), ((), ())) if trans_w else (((1,), (0,)), ((), ()))
    n_i = n_rows // TM

    def body(*refs):
        i, s = pl.program_id(0), pl.program_id(1)
        if rms_bwd is not None:
            x_ref, w_ref, h_ref, wn_ref, dres_ref, o_ref, dw_ref, acc = refs
        elif res is not None:
            x_ref, w_ref, res_ref, o_ref, acc = refs
        else:
            x_ref, w_ref, o_ref, acc = refs

        @pl.when(s == 0)
        def _():
            acc[...] = jnp.zeros_like(acc)

        acc[...] += lax.dot_general(x_ref[...].astype(BF16), w_ref[...], dims, preferred_element_type=F32)

        @pl.when(s == n_sh - 1)
        def _():
            if rms_bwd is not None:
                dh, dw_rows = _rms_bwd(acc[...], h_ref[...], wn_ref[...])
                o_ref[...] = (dres_ref[...] + dh).astype(o_ref.dtype)

                @pl.when(i == 0)
                def _():
                    dw_ref[...] = jnp.zeros_like(dw_ref)

                dw_ref[...] += _fold8(dw_rows)
            elif res is not None:
                o_ref[...] = (res_ref[...] + acc[...]).astype(o_ref.dtype)
            else:
                o_ref[...] = acc[...].astype(o_ref.dtype)

    row = lambda i, s: (i, 0)
    in_specs = [pl.BlockSpec((TM, kx), lambda i, s: (i, s)),
                pl.BlockSpec((None, None, k, n), lambda i, s: (s, layer, 0, 0))]
    args = [x, w4]
    out_specs = pl.BlockSpec((TM, n_out), row)
    out_shape = SDS((n_rows, n_out), out_dtype)
    if rms_bwd is not None:
        h, wn, dres = rms_bwd
        in_specs += [pl.BlockSpec((TM, n_out), row), pl.BlockSpec((1, n_out), lambda i, s: (0, 0)),
                     pl.BlockSpec((TM, n_out), row)]
        args += [h, wn, dres]
        out_specs = [out_specs, pl.BlockSpec((8, n_out), lambda i, s: (0, 0))]
        out_shape = [out_shape, SDS((8, n_out), F32)]
    elif res is not None:
        in_specs.append(pl.BlockSpec((TM, n_out), row))
        args.append(res)
    return pl.pallas_call(
        body, name=name, grid=(n_i, n_sh), in_specs=in_specs, out_specs=out_specs, out_shape=out_shape,
        scratch_shapes=[pltpu.VMEM((TM, n_out), F32)], compiler_params=_params(2),
    )(*args)


def _mm_tn(a, b, n_sh, a_sharded, name, relu2_a=False):
    n_rows = a.shape[0]
    ka = a.shape[1] // n_sh if a_sharded else a.shape[1]
    nb = b.shape[1] if a_sharded else b.shape[1] // n_sh
    n_i = n_rows // TM

    def body(a_ref, b_ref, o_ref, acc):
        i = pl.program_id(1)

        @pl.when(i == 0)
        def _():
            acc[...] = jnp.zeros_like(acc)

        at = a_ref[...]
        if relu2_a:
            at = jnp.maximum(at.astype(F32), 0.0)
            at = at * at
        acc[...] += lax.dot_general(at.astype(BF16), b_ref[...].astype(BF16), (((0,), (0,)), ((), ())),
                                    preferred_element_type=F32)

        @pl.when(i == n_i - 1)
        def _():
            o_ref[...] = acc[...].astype(o_ref.dtype)

    a_spec = pl.BlockSpec((TM, ka), (lambda s, i: (i, s)) if a_sharded else (lambda s, i: (i, 0)))
    b_spec = pl.BlockSpec((TM, nb), (lambda s, i: (i, 0)) if a_sharded else (lambda s, i: (i, s)))
    return pl.pallas_call(
        body, name=name, grid=(n_sh, n_i), in_specs=[a_spec, b_spec],
        out_specs=pl.BlockSpec((None, ka, nb), lambda s, i: (s, 0, 0)),
        out_shape=SDS((n_sh, ka, nb), BF16),
        scratch_shapes=[pltpu.VMEM((ka, nb), F32)], compiler_params=_params(2),
    )(a, b)


def _mlp_fwd(h, wn, w_up4, w_down4, layer, name):
    n_rows = h.shape[0]
    n_sh = w_up4.shape[0]
    f_sh = D_FF // n_sh

    def body(h_ref, wn_ref, wu_ref, wd_ref, o_ref, a_ref, hn_ref, hn_s, acc):
        s = pl.program_id(1)

        @pl.when(s == 0)
        def _():
            hn = _rms(h_ref[...], wn_ref[...]).astype(BF16)
            hn_s[...] = hn
            hn_ref[...] = hn
            acc[...] = jnp.zeros_like(acc)

        a = jnp.dot(hn_s[...], wu_ref[...], preferred_element_type=F32)
        a_ref[...] = a.astype(BF16)
        act = jnp.maximum(a, 0.0)
        acc[...] += jnp.dot((act * act).astype(BF16), wd_ref[...], preferred_element_type=F32)

        @pl.when(s == n_sh - 1)
        def _():
            o_ref[...] = h_ref[...] + acc[...]

    row = lambda i, s: (i, 0)
    return pl.pallas_call(
        body, name=name, grid=(n_rows // TM, n_sh),
        in_specs=[pl.BlockSpec((TM, D_MODEL), row), pl.BlockSpec((1, D_MODEL), lambda i, s: (0, 0)),
                  pl.BlockSpec((None, None, D_MODEL, f_sh), lambda i, s: (s, layer, 0, 0)),
                  pl.BlockSpec((None, None, f_sh, D_MODEL), lambda i, s: (s, layer, 0, 0))],
        out_specs=[pl.BlockSpec((TM, D_MODEL), row), pl.BlockSpec((TM, f_sh), lambda i, s: (i, s)),
                   pl.BlockSpec((TM, D_MODEL), row)],
        out_shape=[SDS((n_rows, D_MODEL), F32), SDS((n_rows, D_FF), BF16), SDS((n_rows, D_MODEL), BF16)],
        scratch_shapes=[pltpu.VMEM((TM, D_MODEL), BF16), pltpu.VMEM((TM, D_MODEL), F32)],
        compiler_params=_params(2),
    )(h, wn, w_up4, w_down4)


def _attn_masks(n):
    qi = lax.broadcasted_iota(jnp.int32, (BLOCK, 3 * BLOCK), 0)
    col = lax.broadcasted_iota(jnp.int32, (BLOCK, 3 * BLOCK), 1)
    kj = col - BLOCK
    dist = BLOCK + qi - kj
    kmin = jnp.where(n == 0, 2 * BLOCK, jnp.where(n == 1, BLOCK, 0))
    band_ok = (col >= BLOCK) & (dist >= 0) & (dist < BLOCK) & (kj >= kmin)
    q_pos = n * BLOCK + qi - PAD
    meta_ok = (col >= PAD) & (col < BLOCK) & (col - PAD <= q_pos)
    distf = jnp.where(col >= BLOCK, dist, 0).astype(F32)
    return band_ok | meta_ok, distf


def _alibi_slope(h):
    return float(2.0 ** (-8.0 * (h + 1) / N_HEADS))


def _attn_fwd(qkv, sinks, n_ex, nb):
    n_rows = qkv.shape[0]
    kvb = N_HEADS * HEAD_DIM // KV_DIM

    def body(sink_ref, q_ref, kvm_ref, kvp_ref, kvc_ref, o_ref, lse_ref, k_s, v_s):
        n = pl.program_id(1)
        ok, distf = _attn_masks(n)
        for part, ref in enumerate((kvm_ref, kvp_ref, kvc_ref)):
            k_s[part * BLOCK:(part + 1) * BLOCK, :] = ref[:, 0:N_KV * HEAD_DIM]
            v_s[part * BLOCK:(part + 1) * BLOCK, :] = ref[:, N_KV * HEAD_DIM:KV_DIM]
        for h in range(N_HEADS):
            kv = h // (N_HEADS // N_KV)
            qh = q_ref[:, h * HEAD_DIM:(h + 1) * HEAD_DIM]
            kh = k_s[:, kv * HEAD_DIM:(kv + 1) * HEAD_DIM]
            vh = v_s[:, kv * HEAD_DIM:(kv + 1) * HEAD_DIM]
            s = lax.dot_general(qh, kh, (((1,), (1,)), ((), ())), preferred_element_type=F32) * (HEAD_DIM ** -0.5)
            s = jnp.where(ok, s - _alibi_slope(h) * distf, NEG_INF)
            sink = sink_ref[0, h]
            m = jnp.maximum(jnp.max(s, axis=-1, keepdims=True), sink)
            e = jnp.exp(s - m)
            l = jnp.sum(e, axis=-1, keepdims=True) + jnp.exp(sink - m)
            o = jnp.dot(e.astype(BF16), vh, preferred_element_type=F32) / l
            o_ref[:, h * HEAD_DIM:(h + 1) * HEAD_DIM] = o.astype(BF16)
            lse_ref[:, h:h + 1] = m + jnp.log(l)

    return pl.pallas_call(
        body, name="attn_fwd", grid=(n_ex, nb),
        in_specs=[pl.BlockSpec(memory_space=pltpu.SMEM),
                  pl.BlockSpec((BLOCK, N_HEADS * HEAD_DIM), lambda b, n: (b * nb + n, 0)),
                  pl.BlockSpec((BLOCK, KV_DIM), lambda b, n: (b * nb, kvb)),
                  pl.BlockSpec((BLOCK, KV_DIM), lambda b, n: (b * nb + jnp.maximum(n - 1, 0), kvb)),
                  pl.BlockSpec((BLOCK, KV_DIM), lambda b, n: (b * nb + n, kvb))],
        out_specs=[pl.BlockSpec((BLOCK, N_HEADS * HEAD_DIM), lambda b, n: (b * nb + n, 0)),
                   pl.BlockSpec((BLOCK, N_HEADS), lambda b, n: (b * nb + n, 0))],
        out_shape=[SDS((n_rows, N_HEADS * HEAD_DIM), BF16), SDS((n_rows, N_HEADS), F32)],
        scratch_shapes=[pltpu.VMEM((3 * BLOCK, N_KV * HEAD_DIM), BF16), pltpu.VMEM((3 * BLOCK, N_KV * HEAD_DIM), BF16)],
        compiler_params=_params(2),
    )(sinks, qkv, qkv, qkv, qkv)


def _attn_bwd(qkv, sinks, o, lse, do, n_ex, nb):
    n_rows = qkv.shape[0]
    kvb = N_HEADS * HEAD_DIM // KV_DIM
    scale = HEAD_DIM ** -0.5
    nq = lambda r: nb - 1 - r

    def body(sink_ref, q_ref, kvm_ref, kvp_ref, kvc_ref, o_ref, lse_ref, do_ref, dqkv_ref, dsink_ref,
             k_s, v_s, dkv_s, carry_s, meta_s):
        b, r = pl.program_id(0), pl.program_id(1)
        n = nq(r)
        ok, distf = _attn_masks(n)

        @pl.when((b == 0) & (r == 0))
        def _():
            dsink_ref[...] = jnp.zeros_like(dsink_ref)

        @pl.when(r == 0)
        def _():
            carry_s[...] = jnp.zeros_like(carry_s)
            meta_s[...] = jnp.zeros_like(meta_s)

        for part, ref in enumerate((kvm_ref, kvp_ref, kvc_ref)):
            k_s[part * BLOCK:(part + 1) * BLOCK, :] = ref[:, 0:N_KV * HEAD_DIM]
            v_s[part * BLOCK:(part + 1) * BLOCK, :] = ref[:, N_KV * HEAD_DIM:KV_DIM]
        dkv_s[...] = jnp.zeros_like(dkv_s)
        for h in range(N_HEADS):
            kv = h // (N_HEADS // N_KV)
            cols = slice(h * HEAD_DIM, (h + 1) * HEAD_DIM)
            kcols = slice(kv * HEAD_DIM, (kv + 1) * HEAD_DIM)
            vcols = slice(N_KV * HEAD_DIM + kv * HEAD_DIM, N_KV * HEAD_DIM + (kv + 1) * HEAD_DIM)
            qh = q_ref[:, cols]
            kh = k_s[:, kcols]
            vh = v_s[:, kcols]
            doh = do_ref[:, cols]
            s = lax.dot_general(qh, kh, (((1,), (1,)), ((), ())), preferred_element_type=F32) * scale
            s = jnp.where(ok, s - _alibi_slope(h) * distf, NEG_INF)
            lse_h = lse_ref[:, h:h + 1]
            p = jnp.exp(s - lse_h)
            dp = lax.dot_general(doh, vh, (((1,), (1,)), ((), ())), preferred_element_type=F32)
            delta = jnp.sum(doh.astype(F32) * o_ref[:, cols].astype(F32), axis=-1, keepdims=True)
            ds = (p * (dp - delta)).astype(BF16)
            dsink_ref[:, h:h + 1] += -jnp.exp(sink_ref[0, h] - lse_h) * delta
            dq = jnp.dot(ds, kh, preferred_element_type=F32) * scale
            dqkv_ref[:, cols] = dq.astype(BF16)
            dkv_s[:, kcols] += lax.dot_general(ds, qh, (((0,), (0,)), ((), ())), preferred_element_type=F32) * scale
            dkv_s[:, vcols] += lax.dot_general(p.astype(BF16), doh, (((0,), (0,)), ((), ())), preferred_element_type=F32)

        meta_s[...] += dkv_s[0:BLOCK, :]
        cur = dkv_s[2 * BLOCK:3 * BLOCK, :] + carry_s[...]
        carry_s[...] = dkv_s[BLOCK:2 * BLOCK, :]

        @pl.when(n > 0)
        def _():
            dqkv_ref[:, N_HEADS * HEAD_DIM:QKV_DIM] = cur.astype(BF16)

        @pl.when(n == 0)
        def _():
            dqkv_ref[:, N_HEADS * HEAD_DIM:QKV_DIM] = (cur + meta_s[...]).astype(BF16)

    blk = lambda b, r: (b * nb + nq(r), 0)
    return pl.pallas_call(
        body, name="attn_bwd", grid=(n_ex, nb),
        in_specs=[pl.BlockSpec(memory_space=pltpu.SMEM),
                  pl.BlockSpec((BLOCK, N_HEADS * HEAD_DIM), blk),
                  pl.BlockSpec((BLOCK, KV_DIM), lambda b, r: (b * nb, kvb)),
                  pl.BlockSpec((BLOCK, KV_DIM), lambda b, r: (b * nb + jnp.maximum(nq(r) - 1, 0), kvb)),
                  pl.BlockSpec((BLOCK, KV_DIM), lambda b, r: (b * nb + nq(r), kvb)),
                  pl.BlockSpec((BLOCK, N_HEADS * HEAD_DIM), blk),
                  pl.BlockSpec((BLOCK, N_HEADS), blk),
                  pl.BlockSpec((BLOCK, N_HEADS * HEAD_DIM), blk)],
        out_specs=[pl.BlockSpec((BLOCK, QKV_DIM), blk),
                   pl.BlockSpec((BLOCK, N_HEADS), lambda b, r: (0, 0))],
        out_shape=[SDS((n_rows, QKV_DIM), BF16), SDS((BLOCK, N_HEADS), F32)],
        scratch_shapes=[pltpu.VMEM((3 * BLOCK, N_KV * HEAD_DIM), BF16), pltpu.VMEM((3 * BLOCK, N_KV * HEAD_DIM), BF16),
                        pltpu.VMEM((3 * BLOCK, KV_DIM), F32), pltpu.VMEM((BLOCK, KV_DIM), F32),
                        pltpu.VMEM((BLOCK, KV_DIM), F32)],
        compiler_params=_params(2),
    )(sinks, qkv, qkv, qkv, qkv, o, lse, do)


def _cmul_add(xr, xi, mr, mi, sr, si):
    return xr + mr * sr - mi * si, xi + mr * si + mi * sr


def _scan_tiles(buf, tab_ref, carry_s, n_groups, reverse):
    shifts = (7, 6, 4) if reverse else (1, 2, 4)

    def group(gi, carry):
        g = (n_groups - 1 - gi) if reverse else gi
        row = pl.multiple_of(g * 8, 8)
        out = []
        for j in range(PAIRS_PER_CHUNK):
            xr = buf[j, pl.ds(row, 8), 0:128]
            xi = buf[j, pl.ds(row, 8), 128:256]
            for lvl, sh in enumerate(shifts):
                xr, xi = _cmul_add(xr, xi, tab_ref[j, 2 * lvl], tab_ref[j, 2 * lvl + 1],
                                   pltpu.roll(xr, sh, 0), pltpu.roll(xi, sh, 0))
            xr, xi = _cmul_add(xr, xi, tab_ref[j, 6], tab_ref[j, 7], carry[2 * j], carry[2 * j + 1])
            buf[j, pl.ds(row, 8), 0:128] = xr
            buf[j, pl.ds(row, 8), 128:256] = xi
            edge = slice(0, 1) if reverse else slice(7, 8)
            out += [jnp.broadcast_to(xr[edge], (8, 128)), jnp.broadcast_to(xi[edge], (8, 128))]
        return tuple(out)

    carry0 = tuple(carry_s[k] for k in range(2 * PAIRS_PER_CHUNK))
    carry = lax.fori_loop(0, n_groups, group, carry0)
    for k in range(2 * PAIRS_PER_CHUNK):
        carry_s[k] = carry[k]


def _ssm_fwd(u, b_pad, c_pad, tab, d_skip, n_ex, lp):
    n_rows = u.shape[0]
    n_t = lp // TM
    n_chunk = D_MODEL // 128

    def body(u_ref, bp_ref, cp_ref, tab_ref, d_ref, yg_ref, y_ref, xs_ref, buf, carry_s):
        @pl.when(pl.program_id(2) == 0)
        def _():
            carry_s[...] = jnp.zeros_like(carry_s)

        ub = u_ref[...]
        u16 = ub.astype(BF16)
        for j in range(PAIRS_PER_CHUNK):
            buf[j] = jnp.dot(u16, bp_ref[j], preferred_element_type=F32)
        _scan_tiles(buf, tab_ref, carry_s, TM // 8, reverse=False)
        y = d_ref[...] * ub
        for j in range(PAIRS_PER_CHUNK):
            xb = buf[j].astype(BF16)
            xs_ref[j] = xb
            y = y + jnp.dot(xb, cp_ref[j], preferred_element_type=F32)
        y_ref[...] = y
        yg_ref[...] = _gelu(y).astype(BF16)

    rows = lambda b, q, t: (b * n_t + t, q)
    return pl.pallas_call(
        body, name="ssm_fwd", grid=(n_ex, n_chunk, n_t),
        in_specs=[pl.BlockSpec((TM, 128), rows),
                  pl.BlockSpec((PAIRS_PER_CHUNK, 128, 256), lambda b, q, t: (q, 0, 0)),
                  pl.BlockSpec((PAIRS_PER_CHUNK, 256, 128), lambda b, q, t: (q, 0, 0)),
                  pl.BlockSpec((PAIRS_PER_CHUNK, 8, 8, 128), lambda b, q, t: (q, 0, 0, 0)),
                  pl.BlockSpec((1, 128), lambda b, q, t: (0, q))],
        out_specs=[pl.BlockSpec((TM, 128), rows), pl.BlockSpec((TM, 128), rows),
                   pl.BlockSpec((PAIRS_PER_CHUNK, TM, 256), lambda b, q, t: (q, b * n_t + t, 0))],
        out_shape=[SDS((n_rows, D_MODEL), BF16), SDS((n_rows, D_MODEL), F32), SDS((N_PAIR, n_rows, 256), BF16)],
        scratch_shapes=[pltpu.VMEM((PAIRS_PER_CHUNK, TM, 256), F32), pltpu.VMEM((2 * PAIRS_PER_CHUNK, 8, 128), F32)],
        compiler_params=_params(3),
    )(u, b_pad, c_pad, tab, d_skip)


def _ssm_bwd(dyg, y, u, xs, ct_pad, bt_pad, tab_rev, d_skip, n_ex, lp):
    n_rows = u.shape[0]
    n_t = lp // TM
    n_chunk = D_MODEL // 128
    tile = lambda q, b, t: (b * n_t + (n_t - 1 - t), q)

    def body(dyg_ref, y_ref, u_ref, xs_ref, xp_ref, ct_ref, bt_ref, tab_ref, d_ref,
             du_ref, db_ref, dc_ref, da_ref, dd_ref, buf, xf, carry_s):
        b, t = pl.program_id(1), pl.program_id(2)

        @pl.when((b == 0) & (t == 0))
        def _():
            db_ref[...] = jnp.zeros_like(db_ref)
            dc_ref[...] = jnp.zeros_like(dc_ref)
            da_ref[...] = jnp.zeros_like(da_ref)
            dd_ref[...] = jnp.zeros_like(dd_ref)

        @pl.when(t == 0)
        def _():
            carry_s[...] = jnp.zeros_like(carry_s)

        ub = u_ref[...]
        dy = dyg_ref[...].astype(F32) * _gelu_grad(y_ref[...])
        dd_ref[...] += _fold8(dy * ub)
        dy16 = dy.astype(BF16)
        first_tile = t == n_t - 1
        for j in range(PAIRS_PER_CHUNK):
            buf[j] = jnp.dot(dy16, ct_ref[j], preferred_element_type=F32)
            dc_ref[j] += lax.dot_general(dy16, xs_ref[j], (((0,), (0,)), ((), ())), preferred_element_type=F32)
            xf[j, 16:16 + TM, :] = xs_ref[j].astype(F32)
            xf[j, 0:16, :] = jnp.where(first_tile, 0.0, xp_ref[j].astype(F32))
        _scan_tiles(buf, tab_ref, carry_s, TM // 8, reverse=True)
        du = d_ref[...] * dy
        u16 = ub.astype(BF16)
        for j in range(PAIRS_PER_CHUNK):
            g = buf[j]
            g16 = g.astype(BF16)
            du = du + jnp.dot(g16, bt_ref[j], preferred_element_type=F32)
            db_ref[j] += lax.dot_general(u16, g16, (((0,), (0,)), ((), ())), preferred_element_type=F32)
            xprev = pltpu.roll(xf[j], 1, 0)[16:16 + TM, :]
            gr, gi = g[:, 0:128], g[:, 128:256]
            pr, pi = xprev[:, 0:128], xprev[:, 128:256]
            da_ref[j, 0] += _fold8(gr * pr + gi * pi)
            da_ref[j, 1] += _fold8(gi * pr - gr * pi)
        du_ref[...] = du

    prev16 = lambda q, b, t: (q, jnp.maximum((b * n_t + (n_t - 1 - t)) * (TM // 16) - 1, 0), 0)
    return pl.pallas_call(
        body, name="ssm_bwd", grid=(n_chunk, n_ex, n_t),
        in_specs=[pl.BlockSpec((TM, 128), tile), pl.BlockSpec((TM, 128), tile), pl.BlockSpec((TM, 128), tile),
                  pl.BlockSpec((PAIRS_PER_CHUNK, TM, 256), lambda q, b, t: (q, b * n_t + (n_t - 1 - t), 0)),
                  pl.BlockSpec((PAIRS_PER_CHUNK, 16, 256), prev16),
                  pl.BlockSpec((PAIRS_PER_CHUNK, 128, 256), lambda q, b, t: (q, 0, 0)),
                  pl.BlockSpec((PAIRS_PER_CHUNK, 256, 128), lambda q, b, t: (q, 0, 0)),
                  pl.BlockSpec((PAIRS_PER_CHUNK, 8, 8, 128), lambda q, b, t: (q, 0, 0, 0)),
                  pl.BlockSpec((1, 128), lambda q, b, t: (0, q))],
        out_specs=[pl.BlockSpec((TM, 128), tile),
                   pl.BlockSpec((PAIRS_PER_CHUNK, 128, 256), lambda q, b, t: (q, 0, 0)),
                   pl.BlockSpec((PAIRS_PER_CHUNK, 128, 256), lambda q, b, t: (q, 0, 0)),
                   pl.BlockSpec((PAIRS_PER_CHUNK, 2, 8, 128), lambda q, b, t: (q, 0, 0, 0)),
                   pl.BlockSpec((8, 128), lambda q, b, t: (0, q))],
        out_shape=[SDS((n_rows, D_MODEL), F32), SDS((N_PAIR, 128, 256), F32), SDS((N_PAIR, 128, 256), F32),
                   SDS((N_PAIR, 2, 8, 128), F32), SDS((8, D_MODEL), F32)],
        scratch_shapes=[pltpu.VMEM((PAIRS_PER_CHUNK, TM, 256), F32), pltpu.VMEM((PAIRS_PER_CHUNK, TM + 16, 256), F32),
                        pltpu.VMEM((2 * PAIRS_PER_CHUNK, 8, 128), F32)],
        compiler_params=_params(3),
    )(dyg, y, u, xs, xs, ct_pad, bt_pad, tab_rev, d_skip)


def _rms_fwd(h, wn, name):
    n_rows = h.shape[0]

    def body(h_ref, wn_ref, o_ref):
        o_ref[...] = _rms(h_ref[...], wn_ref[...])

    return pl.pallas_call(
        body, name=name, grid=(n_rows // TM,),
        in_specs=[pl.BlockSpec((TM, D_MODEL), lambda i: (i, 0)), pl.BlockSpec((1, D_MODEL), lambda i: (0, 0))],
        out_specs=pl.BlockSpec((TM, D_MODEL), lambda i: (i, 0)),
        out_shape=SDS((n_rows, D_MODEL), F32), compiler_params=_params(1),
    )(h, wn)


def _rms_bwd_call(dhn, h, wn, dres, name):
    n_rows = h.shape[0]

    def body(dhn_ref, h_ref, wn_ref, dres_ref, o_ref, dw_ref):
        @pl.when(pl.program_id(0) == 0)
        def _():
            dw_ref[...] = jnp.zeros_like(dw_ref)

        dh, dw_rows = _rms_bwd(dhn_ref[...], h_ref[...], wn_ref[...])
        o_ref[...] = dres_ref[...] + dh
        dw_ref[...] += _fold8(dw_rows)

    row = lambda i: (i, 0)
    return pl.pallas_call(
        body, name=name, grid=(n_rows // TM,),
        in_specs=[pl.BlockSpec((TM, D_MODEL), row), pl.BlockSpec((TM, D_MODEL), row),
                  pl.BlockSpec((1, D_MODEL), lambda i: (0, 0)), pl.BlockSpec((TM, D_MODEL), row)],
        out_specs=[pl.BlockSpec((TM, D_MODEL), row), pl.BlockSpec((8, D_MODEL), lambda i: (0, 0))],
        out_shape=[SDS((n_rows, D_MODEL), F32), SDS((8, D_MODEL), F32)], compiler_params=_params(1),
    )(dhn, h, wn, dres)


def _glu_fwd(h, z):
    n_rows = h.shape[0]

    def body(h_ref, val_ref, gate_ref, o_ref):
        o_ref[...] = h_ref[...] + val_ref[...].astype(F32) * jax.nn.sigmoid(gate_ref[...].astype(F32))

    row = lambda i: (i, 0)
    return pl.pallas_call(
        body, name="glu_fwd", grid=(n_rows // TM,),
        in_specs=[pl.BlockSpec((TM, D_MODEL), row), pl.BlockSpec((TM, D_MODEL), row),
                  pl.BlockSpec((TM, D_MODEL), lambda i: (i, 1))],
        out_specs=pl.BlockSpec((TM, D_MODEL), row),
        out_shape=SDS((n_rows, D_MODEL), F32), compiler_params=_params(1),
    )(h, z, z)


def _glu_bwd(dh, z):
    n_rows = dh.shape[0]

    def body(dh_ref, val_ref, gate_ref, dval_ref, dgate_ref):
        sg = jax.nn.sigmoid(gate_ref[...].astype(F32))
        d = dh_ref[...]
        dval_ref[...] = (d * sg).astype(BF16)
        dgate_ref[...] = (d * val_ref[...].astype(F32) * sg * (1.0 - sg)).astype(BF16)

    row = lambda i: (i, 0)
    dval, dgate = pl.pallas_call(
        body, name="glu_bwd", grid=(n_rows // TM,),
        in_specs=[pl.BlockSpec((TM, D_MODEL), row), pl.BlockSpec((TM, D_MODEL), row),
                  pl.BlockSpec((TM, D_MODEL), lambda i: (i, 1))],
        out_specs=[pl.BlockSpec((TM, D_MODEL), row), pl.BlockSpec((TM, D_MODEL), row)],
        out_shape=[SDS((n_rows, D_MODEL), BF16), SDS((n_rows, D_MODEL), BF16)], compiler_params=_params(1),
    )(dh, z, z)
    return dval, dgate


def _loss_head(h, wn, target, n_ex, nb):
    n_rows = h.shape[0]

    def body(h_ref, wn_ref, t_ref, dh_ref, loss_ref, dw_ref):
        b, n = pl.program_id(0), pl.program_id(1)

        @pl.when((b == 0) & (n == 0))
        def _():
            loss_ref[...] = jnp.zeros_like(loss_ref)
            dw_ref[...] = jnp.zeros_like(dw_ref)

        @pl.when(n == 0)
        def _():
            dh_ref[...] = jnp.zeros_like(dh_ref)

        @pl.when(n > 0)
        def _():
            hh = h_ref[...]
            diff = _rms(hh, wn_ref[...]) - t_ref[...]
            loss_ref[...] += 0.5 * jnp.sum(diff * diff) * (1.0 / D_MODEL)
            dh, dw_rows = _rms_bwd(diff * (1.0 / D_MODEL), hh, wn_ref[...])
            dh_ref[...] = dh
            dw_ref[...] += _fold8(dw_rows)

    return pl.pallas_call(
        body, name="loss_head", grid=(n_ex, nb),
        in_specs=[pl.BlockSpec((BLOCK, D_MODEL), lambda b, n: (b * nb + n, 0)),
                  pl.BlockSpec((1, D_MODEL), lambda b, n: (0, 0)),
                  pl.BlockSpec((BLOCK, D_MODEL), lambda b, n: (b * (nb - 1) + jnp.maximum(n - 1, 0), 0))],
        out_specs=[pl.BlockSpec((BLOCK, D_MODEL), lambda b, n: (b * nb + n, 0)),
                   pl.BlockSpec((8, 128), lambda b, n: (0, 0)),
                   pl.BlockSpec((8, D_MODEL), lambda b, n: (0, 0))],
        out_shape=[SDS((n_rows, D_MODEL), F32), SDS((8, 128), F32), SDS((8, D_MODEL), F32)],
        compiler_params=_params(2),
    )(h, wn, target)


def _adamw(pieces, w, m, v, name):
    rows, cols = w.shape
    rb = rows
    for cand in (256, 136, 128, 64, 32, 16, 8):
        if rows % cand == 0 and rows > cand:
            rb = cand
            break
    c1 = 1.0 / (1.0 - ADAM_B1 ** ADAM_STEP)
    c2 = 1.0 / (1.0 - ADAM_B2 ** ADAM_STEP)

    def body(p_ref, w_ref, m_ref, v_ref, g_out, d_out, m_out, v_out):
        g = p_ref[0].astype(F32)
        for k in range(1, N_DEV):
            g = g + p_ref[k].astype(F32)
        m_new = ADAM_B1 * m_ref[...] + (1.0 - ADAM_B1) * g
        v_new = ADAM_B2 * v_ref[...] + (1.0 - ADAM_B2) * (g * g)
        g_out[...] = g
        m_out[...] = m_new
        v_out[...] = v_new
        d_out[...] = -ADAM_LR * ((m_new * c1) / (jnp.sqrt(v_new * c2) + ADAM_EPS) + ADAM_WD * w_ref[...])

    blk = pl.BlockSpec((rb, cols), lambda i: (i, 0))
    return pl.pallas_call(
        body, name=name, grid=(rows // rb,),
        in_specs=[pl.BlockSpec((N_DEV, rb, cols), lambda i: (0, i, 0)), blk, blk, blk],
        out_specs=[blk, blk, blk, blk],
        out_shape=[SDS((rows, cols), F32)] * 4, compiler_params=_params(1),
    )(pieces, w, m, v)


def _all_gather_chips(arrays):
    n_arr = len(arrays)

    def body(*refs):
        ins, outs = refs[:n_arr], refs[n_arr:2 * n_arr]
        send_sems, recv_sems, local_sems = refs[2 * n_arr:]
        x, y, c = lax.axis_index("x"), lax.axis_index("y"), lax.axis_index("c")
        mine = 2 * x + y
        chips = [(1 - x, y), (x, 1 - y), (1 - x, 1 - y)]
        copies = []
        for a in range(n_arr):
            loc = pltpu.make_async_copy(ins[a], outs[a].at[mine], local_sems.at[a])
            loc.start()
            copies.append(loc)
            for k, (px, py) in enumerate(chips):
                cp = pltpu.make_async_remote_copy(
                    src_ref=ins[a], dst_ref=outs[a].at[mine], send_sem=send_sems.at[a, k], recv_sem=recv_sems.at[a, k],
                    device_id=(px, py, c), device_id_type=pl.DeviceIdType.MESH)
                cp.start()
                copies.append(cp)
        for a in range(n_arr):
            for k, (px, py) in enumerate(chips):
                pltpu.make_async_remote_copy(
                    src_ref=ins[a], dst_ref=outs[a].at[2 * px + py], send_sem=send_sems.at[a, k],
                    recv_sem=recv_sems.at[a, k], device_id=(px, py, c), device_id_type=pl.DeviceIdType.MESH).wait_recv()
        for idx, cp in enumerate(copies):
            if idx % 4 == 0:
                cp.wait()
            else:
                cp.wait_send()

    return pl.pallas_call(
        body, name="gather_weights",
        in_specs=[pl.BlockSpec(memory_space=pl.ANY)] * n_arr,
        out_specs=[pl.BlockSpec(memory_space=pl.ANY)] * n_arr,
        out_shape=[SDS((N_CHIPS,) + a.shape, a.dtype) for a in arrays],
        scratch_shapes=[pltpu.SemaphoreType.DMA((n_arr, 3)), pltpu.SemaphoreType.DMA((n_arr, 3)),
                        pltpu.SemaphoreType.DMA((n_arr,))],
        compiler_params=pltpu.CompilerParams(has_side_effects=True),
    )(*arrays)


def _exchange_grads(scatter, whole):
    arrays = list(scatter) + list(whole)
    n_arr, n_sc = len(arrays), len(scatter)

    def body(*refs):
        ins, outs = refs[:n_arr], refs[n_arr:2 * n_arr]
        send_sems, recv_sems, local_sems = refs[2 * n_arr:]
        x, y, c = lax.axis_index("x"), lax.axis_index("y"), lax.axis_index("c")
        me = 4 * x + 2 * y + c
        peers = [(x ^ (k >> 2), y ^ ((k >> 1) & 1), c ^ (k & 1)) for k in range(1, N_DEV)]

        def piece(a, px, py):
            return ins[a].at[2 * px + py] if a < n_sc else ins[a]

        started = []
        for a in range(n_arr):
            loc = pltpu.make_async_copy(piece(a, x, y), outs[a].at[me], local_sems.at[a])
            loc.start()
            started.append((loc, True))
            for k, (px, py, pc) in enumerate(peers):
                cp = pltpu.make_async_remote_copy(
                    src_ref=piece(a, px, py), dst_ref=outs[a].at[me], send_sem=send_sems.at[a, k],
                    recv_sem=recv_sems.at[a, k], device_id=(px, py, pc), device_id_type=pl.DeviceIdType.MESH)
                cp.start()
                started.append((cp, False))
        for a in range(n_arr):
            for k, (px, py, pc) in enumerate(peers):
                pltpu.make_async_remote_copy(
                    src_ref=piece(a, x, y), dst_ref=outs[a].at[4 * px + 2 * py + pc], send_sem=send_sems.at[a, k],
                    recv_sem=recv_sems.at[a, k], device_id=(px, py, pc), device_id_type=pl.DeviceIdType.MESH).wait_recv()
        for cp, is_local in started:
            if is_local:
                cp.wait()
            else:
                cp.wait_send()

    out_shape = [SDS((N_DEV,) + (a.shape[1:] if i < n_sc else a.shape), a.dtype) for i, a in enumerate(arrays)]
    return pl.pallas_call(
        body, name="exchange_grads",
        in_specs=[pl.BlockSpec(memory_space=pl.ANY)] * n_arr,
        out_specs=[pl.BlockSpec(memory_space=pl.ANY)] * n_arr,
        out_shape=out_shape,
        scratch_shapes=[pltpu.SemaphoreType.DMA((n_arr, N_DEV - 1)), pltpu.SemaphoreType.DMA((n_arr, N_DEV - 1)),
                        pltpu.SemaphoreType.DMA((n_arr,))],
        compiler_params=pltpu.CompilerParams(has_side_effects=True),
    )(*arrays)


def _ssm_discretize(lam_re, lam_im, log_dt, b_re, b_im):
    lr = jnp.minimum(lam_re, LAMBDA_RE_MAX)
    li = lam_im
    dt = jnp.exp(log_dt)[:, None]
    mag = jnp.exp(lr * dt)
    ar, ai = mag * jnp.cos(li * dt), mag * jnp.sin(li * dt)
    den = lr * lr + li * li
    nr, ni = ar - 1.0, ai
    gr, gi = (nr * lr + ni * li) / den, (ni * lr - nr * li) / den
    bbr = gr[:, :, None] * b_re - gi[:, :, None] * b_im
    bbi = gr[:, :, None] * b_im + gi[:, :, None] * b_re
    return ar, ai, bbr, bbi


def _pair_lanes(t):
    return t.reshape(N_PAIR, 2 * SSM_STATE)


def _chan_state_blocks(t_gcp):
    t = t_gcp.reshape(N_PAIR, 2, SSM_GROUP, SSM_STATE)
    eye2 = jnp.eye(2, dtype=t.dtype)
    blk = jnp.einsum("rgcp,gh->rgchp", t, eye2).reshape(N_PAIR, 2 * SSM_GROUP, 2 * SSM_STATE)
    place = jax.nn.one_hot(jnp.arange(N_PAIR) % PAIRS_PER_CHUNK, PAIRS_PER_CHUNK, dtype=t.dtype)
    return jnp.einsum("rcl,rj->rjcl", blk, place).reshape(N_PAIR, 128, 2 * SSM_STATE)


def _chan_state_unblock(t):
    t = t.reshape(N_PAIR, PAIRS_PER_CHUNK, 2, SSM_GROUP, 2, SSM_STATE)
    place = jax.nn.one_hot(jnp.arange(N_PAIR) % PAIRS_PER_CHUNK, PAIRS_PER_CHUNK, dtype=t.dtype)
    t = jnp.einsum("rjgchp,rj->rgchp", t, place)
    t = jnp.einsum("rgchp,gh->rgcp", t, jnp.eye(2, dtype=t.dtype))
    return t.reshape(SSM_NG, SSM_GROUP, SSM_STATE)


def _scan_tables(ar, ai, reverse):
    ar, ai = _pair_lanes(ar), _pair_lanes(ai)
    if reverse:
        ai = -ai
    pows = [(ar, ai)]
    for _ in range(7):
        pr, pi = pows[-1]
        pows.append((pr * ar - pi * ai, pr * ai + pi * ar))
    rows = jnp.arange(8)[None, :, None]
    tiles = []
    for k in (1, 2, 4):
        keep = (rows <= 7 - k) if reverse else (rows >= k)
        for part in pows[k - 1]:
            tiles.append(jnp.where(keep, part[:, None, :], 0.0))
    order = list(range(7, -1, -1)) if reverse else list(range(8))
    for comp in (0, 1):
        tiles.append(jnp.stack([pows[e][comp] for e in order], axis=1))
    return jnp.stack(tiles, axis=1)


def _local_step(x, target, w):
    n_ex, seq, _ = x.shape
    lp = seq + BLOCK
    nb = lp // BLOCK
    n_rows = n_ex * lp
    g = {}

    head = jnp.concatenate([jnp.zeros((PAD, D_MODEL), F32), w["meta_tokens"]], axis=0)
    h0 = jnp.concatenate([jnp.broadcast_to(head[None], (n_ex, BLOCK, D_MODEL)), x], axis=1).reshape(n_rows, D_MODEL)

    qkv, hn_a = _rms_mm_cols(h0, w["attn_norm_w"], w["attn_w_qkv"], 0, "qkv_fwd")
    att, lse = _attn_fwd(qkv, w["attn_sinks"], n_ex, nb)
    h1 = _mm_acc(att, w["attn_w_o"], 0, False, "attn_out_fwd", res=h0)
    h2, a0, hn_m0 = _mlp_fwd(h1, w["mlp_norm_w"][0:1], w["mlp_w_up"], w["mlp_w_down"], 0, "mlp0_fwd")

    ar, ai, bbr, bbi = _ssm_discretize(w["ssm_lambda_re"], w["ssm_lambda_im"], w["ssm_log_dt"], w["ssm_b_re"], w["ssm_b_im"])
    b_blk = jnp.concatenate([_chan_state_blocks(jnp.swapaxes(bbr, 1, 2)), _chan_state_blocks(jnp.swapaxes(bbi, 1, 2))], axis=2)
    c_blk = jnp.concatenate([_chan_state_blocks(w["ssm_c_re"]), -_chan_state_blocks(w["ssm_c_im"])], axis=2)
    b_pad, bt_pad = b_blk.astype(BF16), jnp.swapaxes(b_blk, 1, 2).astype(BF16)
    ct_pad, c_pad = c_blk.astype(BF16), jnp.swapaxes(c_blk, 1, 2).astype(BF16)
    u = _rms_fwd(h2, w["ssm_norm_w"], "ssm_norm_fwd")
    yg, y, xs = _ssm_fwd(u, b_pad, c_pad, _scan_tables(ar, ai, False), w["ssm_d"], n_ex, lp)
    z = _mm_cols(yg, w["ssm_w_glu"], 0, False, "glu_mm_fwd")
    h3 = _glu_fwd(h2, z)
    h4, a1, hn_m1 = _mlp_fwd(h3, w["mlp_norm_w"][1:2], w["mlp_w_up"], w["mlp_w_down"], 1, "mlp1_fwd")

    dh4, loss_tile, g["final_norm_w"] = _loss_head(h4, w["final_norm_w"], target.reshape(n_ex * seq, D_MODEL), n_ex, nb)

    def mlp_bwd(dh_out, h_in, a, hn, layer, tag):
        da = _mm_cols(dh_out, w["mlp_w_down"], layer, True, tag + "_dact", mul2relu=a)
        dh_in, dnorm = _mm_acc(da, w["mlp_w_up"], layer, True, tag + "_dx",
                               rms_bwd=(h_in, w["mlp_norm_w"][layer:layer + 1], dh_out))
        dw_down = _mm_tn(a, dh_out, N_CHIPS, True, tag + "_dwdown", relu2_a=True)
        dw_up = _mm_tn(hn, da, N_CHIPS, False, tag + "_dwup")
        return dh_in, dnorm, dw_up, dw_down

    dh3, dnorm_m1, dwu1, dwd1 = mlp_bwd(dh4, h3, a1, hn_m1, 1, "mlp1")
    dval, dgate = _glu_bwd(dh3, z)
    dz = jnp.concatenate([dval, dgate], axis=1)
    dyg = _mm_acc(dz, w["ssm_w_glu"], 0, True, "glu_mm_dx", out_dtype=BF16)
    g["ssm_w_glu"] = _mm_tn(yg, dz, N_CHIPS, False, "glu_mm_dw")
    du, db_blk, dc_blk, da_t, dd_t = _ssm_bwd(dyg, y, u, xs, ct_pad, bt_pad, _scan_tables(ar, ai, True), w["ssm_d"], n_ex, lp)
    dh2, dnorm_s = _rms_bwd_call(du, h2, w["ssm_norm_w"], dh3, "ssm_norm_bwd")
    dh1, dnorm_m0, dwu0, dwd0 = mlp_bwd(dh2, h1, a0, hn_m0, 0, "mlp0")
    datt = _mm_cols(dh1, w["attn_w_o"], 0, True, "attn_out_dx")
    g["attn_w_o"] = _mm_tn(att, dh1, N_CHIPS, True, "attn_out_dw")
    dqkv, dsink_rows = _attn_bwd(qkv, w["attn_sinks"], att, lse, datt, n_ex, nb)
    dh0, dnorm_a = _mm_acc(dqkv, w["attn_w_qkv"], 0, True, "qkv_dx", rms_bwd=(h0, w["attn_norm_w"], dh1))
    g["attn_w_qkv"] = _mm_tn(hn_a, dqkv, N_CHIPS, False, "qkv_dw")

    g["mlp_w_up"] = jnp.stack([dwu0, dwu1], axis=1)
    g["mlp_w_down"] = jnp.stack([dwd0, dwd1], axis=1)
    g["mlp_norm_w"] = jnp.stack([jnp.sum(dnorm_m0, axis=0), jnp.sum(dnorm_m1, axis=0)])
    g["final_norm_w"] = jnp.sum(g["final_norm_w"], axis=0)
    g["attn_norm_w"] = jnp.sum(dnorm_a, axis=0)[None]
    g["ssm_norm_w"] = jnp.sum(dnorm_s, axis=0)[None]
    g["attn_sinks"] = jnp.sum(dsink_rows, axis=0)[None]
    g["ssm_d"] = jnp.sum(dd_t, axis=0)[None]
    g["ssm_c_re"] = _chan_state_unblock(dc_blk[:, :, 0:128])
    g["ssm_c_im"] = -_chan_state_unblock(dc_blk[:, :, 128:256])
    g_bbr = jnp.swapaxes(_chan_state_unblock(db_blk[:, :, 0:128]), 1, 2)
    g_bbi = jnp.swapaxes(_chan_state_unblock(db_blk[:, :, 128:256]), 1, 2)
    g_a = jnp.sum(da_t, axis=2).reshape(N_PAIR, 2, 2, SSM_STATE)
    g_ar, g_ai = g_a[:, 0].reshape(SSM_NG, SSM_STATE), g_a[:, 1].reshape(SSM_NG, SSM_STATE)
    _, vjp = jax.vjp(_ssm_discretize, w["ssm_lambda_re"], w["ssm_lambda_im"], w["ssm_log_dt"], w["ssm_b_re"], w["ssm_b_im"])
    g["ssm_lambda_re"], g["ssm_lambda_im"], g["ssm_log_dt"], g["ssm_b_re"], g["ssm_b_im"] = vjp((g_ar, g_ai, g_bbr, g_bbi))

    dh0 = dh0.reshape(n_ex, lp, D_MODEL)
    g["meta_tokens"] = jnp.sum(dh0[:, PAD:BLOCK], axis=0)
    return loss_tile, dh0[:, BLOCK:], g


_SHARDED_SMALL = ("meta_tokens", "ssm_norm_w", "ssm_d")
_REPLICATED = ("attn_norm_w", "attn_sinks", "ssm_lambda_re", "ssm_lambda_im", "ssm_log_dt", "ssm_b_re", "ssm_b_im",
               "ssm_c_re", "ssm_c_im", "mlp_norm_w", "final_norm_w")
_BIG = ("attn_w_qkv", "attn_w_o", "ssm_w_glu", "mlp_w_up", "mlp_w_down")


def _pack(parts, cols):
    flat = jnp.concatenate([p.reshape(-1) for p in parts])
    rows = -(-flat.shape[0] // (8 * cols)) * 8
    return jnp.pad(flat, (0, rows * cols - flat.shape[0])).reshape(rows, cols)


def _unpack(packed, like):
    flat = packed.reshape(-1)
    out, at = [], 0
    for p in like:
        out.append(flat[at:at + p.size].reshape(p.shape))
        at += p.size
    return out


def kernel(x, meta_tokens, attn_norm_w, attn_w_qkv, attn_sinks, attn_w_o, ssm_norm_w, ssm_lambda_re, ssm_lambda_im, ssm_log_dt, ssm_b_re, ssm_b_im, ssm_c_re, ssm_c_im, ssm_d, ssm_w_glu, mlp_norm_w, mlp_w_up, mlp_w_down, final_norm_w, loss_target, m_meta_tokens, m_attn_norm_w, m_attn_w_qkv, m_attn_sinks, m_attn_w_o, m_ssm_norm_w, m_ssm_lambda_re, m_ssm_lambda_im, m_ssm_log_dt, m_ssm_b_re, m_ssm_b_im, m_ssm_c_re, m_ssm_c_im, m_ssm_d, m_ssm_w_glu, m_mlp_norm_w, m_mlp_w_up, m_mlp_w_down, m_final_norm_w, v_meta_tokens, v_attn_norm_w, v_attn_w_qkv, v_attn_sinks, v_attn_w_o, v_ssm_norm_w, v_ssm_lambda_re, v_ssm_lambda_im, v_ssm_log_dt, v_ssm_b_re, v_ssm_b_im, v_ssm_c_re, v_ssm_c_im, v_ssm_d, v_ssm_w_glu, v_mlp_norm_w, v_mlp_w_up, v_mlp_w_down, v_final_norm_w):
    names = ("meta_tokens", "attn_norm_w", "attn_w_qkv", "attn_sinks", "attn_w_o", "ssm_norm_w", "ssm_lambda_re",
             "ssm_lambda_im", "ssm_log_dt", "ssm_b_re", "ssm_b_im", "ssm_c_re", "ssm_c_im", "ssm_d", "ssm_w_glu",
             "mlp_norm_w", "mlp_w_up", "mlp_w_down", "final_norm_w")
    wts = dict(zip(names, (meta_tokens, attn_norm_w, attn_w_qkv, attn_sinks, attn_w_o, ssm_norm_w, ssm_lambda_re,
                           ssm_lambda_im, ssm_log_dt, ssm_b_re, ssm_b_im, ssm_c_re, ssm_c_im, ssm_d, ssm_w_glu,
                           mlp_norm_w, mlp_w_up, mlp_w_down, final_norm_w)))
    mom = dict(zip(names, (m_meta_tokens, m_attn_norm_w, m_attn_w_qkv, m_attn_sinks, m_attn_w_o, m_ssm_norm_w,
                           m_ssm_lambda_re, m_ssm_lambda_im, m_ssm_log_dt, m_ssm_b_re, m_ssm_b_im, m_ssm_c_re,
                           m_ssm_c_im, m_ssm_d, m_ssm_w_glu, m_mlp_norm_w, m_mlp_w_up, m_mlp_w_down, m_final_norm_w)))
    var = dict(zip(names, (v_meta_tokens, v_attn_norm_w, v_attn_w_qkv, v_attn_sinks, v_attn_w_o, v_ssm_norm_w,
                           v_ssm_lambda_re, v_ssm_lambda_im, v_ssm_log_dt, v_ssm_b_re, v_ssm_b_im, v_ssm_c_re,
                           v_ssm_c_im, v_ssm_d, v_ssm_w_glu, v_mlp_norm_w, v_mlp_w_up, v_mlp_w_down, v_final_norm_w)))

    small_mine = _pack([wts[n] for n in _SHARDED_SMALL], 128)
    gathered = _all_gather_chips([wts[n].astype(BF16) for n in _BIG] + [small_mine])
    full = {n: wts[n] for n in _REPLICATED}
    full["attn_norm_w"] = attn_norm_w
    full["final_norm_w"] = final_norm_w[None]
    for n in ("ssm_lambda_re", "ssm_lambda_im", "ssm_log_dt", "ssm_b_re", "ssm_b_im", "ssm_c_re", "ssm_c_im"):
        full[n] = wts[n][0]
    full["attn_w_qkv"], full["attn_w_o"], full["ssm_w_glu"] = gathered[0], gathered[1], gathered[2]
    full["mlp_w_up"], full["mlp_w_down"] = gathered[3], gathered[4]
    smalls = [_unpack(gathered[5][s], [wts[n] for n in _SHARDED_SMALL]) for s in range(N_CHIPS)]
    full["meta_tokens"] = jnp.concatenate([smalls[s][0] for s in range(N_CHIPS)], axis=1)
    full["ssm_norm_w"] = jnp.concatenate([smalls[s][1] for s in range(N_CHIPS)], axis=1)
    full["ssm_d"] = jnp.concatenate([smalls[s][2] for s in range(N_CHIPS)], axis=1)

    loss_tile, grad_x, g = _local_step(x, loss_target, full)
    loss = lax.psum(loss_tile[0, 0], ("x", "y", "c"))

    def shard_cols(t, width):
        return jnp.swapaxes(t.reshape(t.shape[0], N_CHIPS, width), 0, 1)

    g_small4 = [_pack([shard_cols(g["meta_tokens"], 256)[s], shard_cols(g["ssm_norm_w"], 256)[s],
                       shard_cols(g["ssm_d"], 256)[s]], 128) for s in range(N_CHIPS)]
    like_rep = [wts[n] for n in _REPLICATED]
    g_rep = _pack([g[n].reshape(wts[n].shape) for n in _REPLICATED], D_MODEL)
    recv = _exchange_grads([g[n] for n in _BIG] + [jnp.stack(g_small4)], [g_rep])

    out = {}

    def update(tag, pieces, w2, m2, v2):
        return _adamw(pieces, w2, m2, v2, "adamw_" + tag)

    for i, n in enumerate(_BIG):
        shp = wts[n].shape
        r2 = (math.prod(shp[:-1]), shp[-1])
        res = update(n, recv[i].reshape((N_DEV,) + r2), wts[n].reshape(r2), mom[n].reshape(r2), var[n].reshape(r2))
        out[n] = [t.reshape(shp) for t in res]
    like_small = [wts[n] for n in _SHARDED_SMALL]
    res = update("small", recv[5], small_mine, _pack([mom[n] for n in _SHARDED_SMALL], 128),
                 _pack([var[n] for n in _SHARDED_SMALL], 128))
    for k, n in enumerate(_SHARDED_SMALL):
        out[n] = [_unpack(t, like_small)[k] for t in res]
    res = update("replicated", recv[6], _pack(like_rep, D_MODEL), _pack([mom[n] for n in _REPLICATED], D_MODEL),
                 _pack([var[n] for n in _REPLICATED], D_MODEL))
    for k, n in enumerate(_REPLICATED):
        out[n] = [_unpack(t, like_rep)[k] for t in res]

    return (loss, grad_x, *[out[n][0] for n in names], *[out[n][1] for n in names],
            *[out[n][2] for n in names], *[out[n][3] for n in names])
```

```python
import functools
import math

import jax
import jax.numpy as jnp
from jax import lax
from jax.experimental import pallas as pl
from jax.experimental.pallas import tpu as pltpu

F32 = jnp.float32
BF16 = jnp.bfloat16
SDS = jax.ShapeDtypeStruct

D_MODEL = 1024
N_HEADS = 16
N_KV = 4
HEAD_DIM = 64
BLOCK = 128
N_META = 16
PAD = BLOCK - N_META
QKV_DIM = (N_HEADS + 2 * N_KV) * HEAD_DIM
KV_DIM = 2 * N_KV * HEAD_DIM
D_FF = 4 * D_MODEL
N_CHIPS = 4
N_DEV = 8
SSM_GROUP = 16
SSM_NG = D_MODEL // SSM_GROUP
SSM_STATE = 64
N_PAIR = SSM_NG // 2
PAIRS_PER_CHUNK = 4
RMS_EPS = 1e-6
NEG_INF = -1e30
LAMBDA_RE_MAX = -1e-4
ADAM_LR, ADAM_B1, ADAM_B2, ADAM_EPS, ADAM_WD, ADAM_STEP = 0.001, 0.9, 0.999, 1e-08, 0.01, 10

TM = 384
VMEM_LIMIT = 56 * 1024 * 1024


def _params(n_grid):
    return pltpu.CompilerParams(dimension_semantics=("arbitrary",) * n_grid, vmem_limit_bytes=VMEM_LIMIT)


def _rms(h, w):
    r = lax.rsqrt(jnp.mean(h * h, axis=-1, keepdims=True) + RMS_EPS)
    return h * r * w


def _rms_bwd(dhn, h, w):
    r = lax.rsqrt(jnp.mean(h * h, axis=-1, keepdims=True) + RMS_EPS)
    g = dhn * w
    proj = jnp.sum(g * h, axis=-1, keepdims=True) * (1.0 / D_MODEL)
    return r * g - h * (r * r * r) * proj, dhn * h * r


def _fold8(t):
    return jnp.sum(t.reshape(t.shape[0] // 8, 8, t.shape[1]), axis=0)


def _gelu(y):
    return 0.5 * y * (1.0 + jnp.tanh(0.7978845608028654 * (y + 0.044715 * y * y * y)))


def _gelu_grad(y):
    t = jnp.tanh(0.7978845608028654 * (y + 0.044715 * y * y * y))
    return 0.5 * (1.0 + t) + 0.5 * y * (1.0 - t * t) * 0.7978845608028654 * (1.0 + 3.0 * 0.044715 * y * y)


def _rms_mm_cols(h, wn, w4, layer, name):
    n_rows = h.shape[0]
    n_sh, _, k, n = w4.shape

    def body(h_ref, wn_ref, w_ref, o_ref, hn_ref, hn_s):
        @pl.when(pl.program_id(1) == 0)
        def _():
            hn = _rms(h_ref[...], wn_ref[...]).astype(BF16)
            hn_s[...] = hn
            hn_ref[...] = hn

        o_ref[...] = jnp.dot(hn_s[...], w_ref[...], preferred_element_type=F32).astype(o_ref.dtype)

    return pl.pallas_call(
        body, name=name, grid=(n_rows // TM, n_sh),
        in_specs=[pl.BlockSpec((TM, k), lambda i, s: (i, 0)),
                  pl.BlockSpec((1, k), lambda i, s: (0, 0)),
                  pl.BlockSpec((None, None, k, n), lambda i, s: (s, layer, 0, 0))],
        out_specs=[pl.BlockSpec((TM, n), lambda i, s: (i, s)),
                   pl.BlockSpec((TM, k), lambda i, s: (i, 0))],
        out_shape=[SDS((n_rows, n_sh * n), BF16), SDS((n_rows, k), BF16)],
        scratch_shapes=[pltpu.VMEM((TM, k), BF16)],
        compiler_params=_params(2),
    )(h, wn, w4)


def _mm_cols(x, w4, layer, trans_w, name, mul2relu=None):
    n_rows, kx = x.shape
    n_sh, _, k, n = w4.shape
    n_out = k if trans_w else n
    dims = (((1,), (1,)), ((), ())) if trans_w else (((1,), (0,)), ((), ()))

    def body(*refs):
        if mul2relu is None:
            x_ref, w_ref, o_ref = refs
        else:
            x_ref, w_ref, a_ref, o_ref = refs
        acc = lax.dot_general(x_ref[...].astype(BF16), w_ref[...], dims, preferred_element_type=F32)
        if mul2relu is not None:
            acc = acc * (2.0 * jnp.maximum(a_ref[...].astype(F32), 0.0))
        o_ref[...] = acc.astype(o_ref.dtype)

    in_specs = [pl.BlockSpec((TM, kx), lambda i, s: (i, 0)),
                pl.BlockSpec((None, None, k, n), lambda i, s: (s, layer, 0, 0))]
    args = [x, w4]
    if mul2relu is not None:
        in_specs.append(pl.BlockSpec((TM, n_out), lambda i, s: (i, s)))
        args.append(mul2relu)
    return pl.pallas_call(
        body, name=name, grid=(n_rows // TM, n_sh),
        in_specs=in_specs,
        out_specs=pl.BlockSpec((TM, n_out), lambda i, s: (i, s)),
        out_shape=SDS((n_rows, n_sh * n_out), BF16),
        compiler_params=_params(2),
    )(*args)


def _mm_acc(x, w4, layer, trans_w, name, res=None, rms_bwd=None, out_dtype=F32):
    n_rows = x.shape[0]
    n_sh, _, k, n = w4.shape
    kx, n_out = (n, k) if trans_w else (k, n)
    dims = (((1,), (1,)), ((), ())) if trans_w else (((1,), (0,)), ((), ()))
    n_i = n_rows // TM

    def body(*refs):
        i, s = pl.program_id(0), pl.program_id(1)
        if rms_bwd is not None:
            x_ref, w_ref, h_ref, wn_ref, dres_ref, o_ref, dw_ref, acc = refs
        elif res is not None:
            x_ref, w_ref, res_ref, o_ref, acc = refs
        else:
            x_ref, w_ref, o_ref, acc = refs

        @pl.when(s == 0)
        def _():
            acc[...] = jnp.zeros_like(acc)

        acc[...] += lax.dot_general(x_ref[...].astype(BF16), w_ref[...], dims, preferred_element_type=F32)

        @pl.when(s == n_sh - 1)
        def _():
            if rms_bwd is not None:
                dh, dw_rows = _rms_bwd(acc[...], h_ref[...], wn_ref[...])
                o_ref[...] = (dres_ref[...] + dh).astype(o_ref.dtype)

                @pl.when(i == 0)
                def _():
                    dw_ref[...] = jnp.zeros_like(dw_ref)

                dw_ref[...] += _fold8(dw_rows)
            elif res is not None:
                o_ref[...] = (res_ref[...] + acc[...]).astype(o_ref.dtype)
            else:
                o_ref[...] = acc[...].astype(o_ref.dtype)

    row = lambda i, s: (i, 0)
    in_specs = [pl.BlockSpec((TM, kx), lambda i, s: (i, s)),
                pl.BlockSpec((None, None, k, n), lambda i, s: (s, layer, 0, 0))]
    args = [x, w4]
    out_specs = pl.BlockSpec((TM, n_out), row)
    out_shape = SDS((n_rows, n_out), out_dtype)
    if rms_bwd is not None:
        h, wn, dres = rms_bwd
        in_specs += [pl.BlockSpec((TM, n_out), row), pl.BlockSpec((1, n_out), lambda i, s: (0, 0)),
                     pl.BlockSpec((TM, n_out), row)]
        args += [h, wn, dres]
        out_specs = [out_specs, pl.BlockSpec((8, n_out), lambda i, s: (0, 0))]
        out_shape = [out_shape, SDS((8, n_out), F32)]
    elif res is not None:
        in_specs.append(pl.BlockSpec((TM, n_out), row))
        args.append(res)
    return pl.pallas_call(
        body, name=name, grid=(n_i, n_sh), in_specs=in_specs, out_specs=out_specs, out_shape=out_shape,
        scratch_shapes=[pltpu.VMEM((TM, n_out), F32)], compiler_params=_params(2),
    )(*args)


def _mm_tn(a, b, n_sh, a_sharded, name, relu2_a=False):
    n_rows = a.shape[0]
    ka = a.shape[1] // n_sh if a_sharded else a.shape[1]
    nb = b.shape[1] if a_sharded else b.shape[1] // n_sh
    n_i = n_rows // TM

    def body(a_ref, b_ref, o_ref, acc):
        i = pl.program_id(1)

        @pl.when(i == 0)
        def _():
            acc[...] = jnp.zeros_like(acc)

        at = a_ref[...]
        if relu2_a:
            at = jnp.maximum(at.astype(F32), 0.0)
            at = at * at
        acc[...] += lax.dot_general(at.astype(BF16), b_ref[...].astype(BF16), (((0,), (0,)), ((), ())),
                                    preferred_element_type=F32)

        @pl.when(i == n_i - 1)
        def _():
            o_ref[...] = acc[...].astype(o_ref.dtype)

    a_spec = pl.BlockSpec((TM, ka), (lambda s, i: (i, s)) if a_sharded else (lambda s, i: (i, 0)))
    b_spec = pl.BlockSpec((TM, nb), (lambda s, i: (i, 0)) if a_sharded else (lambda s, i: (i, s)))
    return pl.pallas_call(
        body, name=name, grid=(n_sh, n_i), in_specs=[a_spec, b_spec],
        out_specs=pl.BlockSpec((None, ka, nb), lambda s, i: (s, 0, 0)),
        out_shape=SDS((n_sh, ka, nb), BF16),
        scratch_shapes=[pltpu.VMEM((ka, nb), F32)], compiler_params=_params(2),
    )(a, b)


def _mlp_fwd(h, wn, w_up4, w_down4, layer, name):
    n_rows = h.shape[0]
    n_sh = w_up4.shape[0]
    f_sh = D_FF // n_sh

    def body(h_ref, wn_ref, wu_ref, wd_ref, o_ref, a_ref, hn_ref, hn_s, acc):
        s = pl.program_id(1)

        @pl.when(s == 0)
        def _():
            hn = _rms(h_ref[...], wn_ref[...]).astype(BF16)
            hn_s[...] = hn
            hn_ref[...] = hn
            acc[...] = jnp.zeros_like(acc)

        a = jnp.dot(hn_s[...], wu_ref[...], preferred_element_type=F32)
        a_ref[...] = a.astype(BF16)
        act = jnp.maximum(a, 0.0)
        acc[...] += jnp.dot((act * act).astype(BF16), wd_ref[...], preferred_element_type=F32)

        @pl.when(s == n_sh - 1)
        def _():
            o_ref[...] = h_ref[...] + acc[...]

    row = lambda i, s: (i, 0)
    return pl.pallas_call(
        body, name=name, grid=(n_rows // TM, n_sh),
        in_specs=[pl.BlockSpec((TM, D_MODEL), row), pl.BlockSpec((1, D_MODEL), lambda i, s: (0, 0)),
                  pl.BlockSpec((None, None, D_MODEL, f_sh), lambda i, s: (s, layer, 0, 0)),
                  pl.BlockSpec((None, None, f_sh, D_MODEL), lambda i, s: (s, layer, 0, 0))],
        out_specs=[pl.BlockSpec((TM, D_MODEL), row), pl.BlockSpec((TM, f_sh), lambda i, s: (i, s)),
                   pl.BlockSpec((TM, D_MODEL), row)],
        out_shape=[SDS((n_rows, D_MODEL), F32), SDS((n_rows, D_FF), BF16), SDS((n_rows, D_MODEL), BF16)],
        scratch_shapes=[pltpu.VMEM((TM, D_MODEL), BF16), pltpu.VMEM((TM, D_MODEL), F32)],
        compiler_params=_params(2),
    )(h, wn, w_up4, w_down4)


def _attn_masks(n):
    qi = lax.broadcasted_iota(jnp.int32, (BLOCK, 3 * BLOCK), 0)
    col = lax.broadcasted_iota(jnp.int32, (BLOCK, 3 * BLOCK), 1)
    kj = col - BLOCK
    dist = BLOCK + qi - kj
    kmin = jnp.where(n == 0, 2 * BLOCK, jnp.where(n == 1, BLOCK, 0))
    band_ok = (col >= BLOCK) & (dist >= 0) & (dist < BLOCK) & (kj >= kmin)
    q_pos = n * BLOCK + qi - PAD
    meta_ok = (col >= PAD) & (col < BLOCK) & (col - PAD <= q_pos)
    distf = jnp.where(col >= BLOCK, dist, 0).astype(F32)
    return band_ok | meta_ok, distf


def _alibi_slope(h):
    return float(2.0 ** (-8.0 * (h + 1) / N_HEADS))


def _attn_fwd(qkv, sinks, n_ex, nb):
    n_rows = qkv.shape[0]
    kvb = N_HEADS * HEAD_DIM // KV_DIM

    def body(sink_ref, q_ref, kvm_ref, kvp_ref, kvc_ref, o_ref, lse_ref, k_s, v_s):
        n = pl.program_id(1)
        ok, distf = _attn_masks(n)
        for part, ref in enumerate((kvm_ref, kvp_ref, kvc_ref)):
            k_s[part * BLOCK:(part + 1) * BLOCK, :] = ref[:, 0:N_KV * HEAD_DIM]
            v_s[part * BLOCK:(part + 1) * BLOCK, :] = ref[:, N_KV * HEAD_DIM:KV_DIM]
        for h in range(N_HEADS):
            kv = h // (N_HEADS // N_KV)
            qh = q_ref[:, h * HEAD_DIM:(h + 1) * HEAD_DIM]
            kh = k_s[:, kv * HEAD_DIM:(kv + 1) * HEAD_DIM]
            vh = v_s[:, kv * HEAD_DIM:(kv + 1) * HEAD_DIM]
            s = lax.dot_general(qh, kh, (((1,), (1,)), ((), ())), preferred_element_type=F32) * (HEAD_DIM ** -0.5)
            s = jnp.where(ok, s - _alibi_slope(h) * distf, NEG_INF)
            sink = sink_ref[0, h]
            m = jnp.maximum(jnp.max(s, axis=-1, keepdims=True), sink)
            e = jnp.exp(s - m)
            l = jnp.sum(e, axis=-1, keepdims=True) + jnp.exp(sink - m)
            o = jnp.dot(e.astype(BF16), vh, preferred_element_type=F32) / l
            o_ref[:, h * HEAD_DIM:(h + 1) * HEAD_DIM] = o.astype(BF16)
            lse_ref[:, h:h + 1] = m + jnp.log(l)

    return pl.pallas_call(
        body, name="attn_fwd", grid=(n_ex, nb),
        in_specs=[pl.BlockSpec(memory_space=pltpu.SMEM),
                  pl.BlockSpec((BLOCK, N_HEADS * HEAD_DIM), lambda b, n: (b * nb + n, 0)),
                  pl.BlockSpec((BLOCK, KV_DIM), lambda b, n: (b * nb, kvb)),
                  pl.BlockSpec((BLOCK, KV_DIM), lambda b, n: (b * nb + jnp.maximum(n - 1, 0), kvb)),
                  pl.BlockSpec((BLOCK, KV_DIM), lambda b, n: (b * nb + n, kvb))],
        out_specs=[pl.BlockSpec((BLOCK, N_HEADS * HEAD_DIM), lambda b, n: (b * nb + n, 0)),
                   pl.BlockSpec((BLOCK, N_HEADS), lambda b, n: (b * nb + n, 0))],
        out_shape=[SDS((n_rows, N_HEADS * HEAD_DIM), BF16), SDS((n_rows, N_HEADS), F32)],
        scratch_shapes=[pltpu.VMEM((3 * BLOCK, N_KV * HEAD_DIM), BF16), pltpu.VMEM((3 * BLOCK, N_KV * HEAD_DIM), BF16)],
        compiler_params=_params(2),
    )(sinks, qkv, qkv, qkv, qkv)


def _attn_bwd(qkv, sinks, o, lse, do, n_ex, nb):
    n_rows = qkv.shape[0]
    kvb = N_HEADS * HEAD_DIM // KV_DIM
    scale = HEAD_DIM ** -0.5
    nq = lambda r: nb - 1 - r

    def body(sink_ref, q_ref, kvm_ref, kvp_ref, kvc_ref, o_ref, lse_ref, do_ref, dqkv_ref, dsink_ref,
             k_s, v_s, dkv_s, carry_s, meta_s):
        b, r = pl.program_id(0), pl.program_id(1)
        n = nq(r)
        ok, distf = _attn_masks(n)

        @pl.when((b == 0) & (r == 0))
        def _():
            dsink_ref[...] = jnp.zeros_like(dsink_ref)

        @pl.when(r == 0)
        def _():
            carry_s[...] = jnp.zeros_like(carry_s)
            meta_s[...] = jnp.zeros_like(meta_s)

        for part, ref in enumerate((kvm_ref, kvp_ref, kvc_ref)):
            k_s[part * BLOCK:(part + 1) * BLOCK, :] = ref[:, 0:N_KV * HEAD_DIM]
            v_s[part * BLOCK:(part + 1) * BLOCK, :] = ref[:, N_KV * HEAD_DIM:KV_DIM]
        dkv_s[...] = jnp.zeros_like(dkv_s)
        for h in range(N_HEADS):
            kv = h // (N_HEADS // N_KV)
            cols = slice(h * HEAD_DIM, (h + 1) * HEAD_DIM)
            kcols = slice(kv * HEAD_DIM, (kv + 1) * HEAD_DIM)
            vcols = slice(N_KV * HEAD_DIM + kv * HEAD_DIM, N_KV * HEAD_DIM + (kv + 1) * HEAD_DIM)
            qh = q_ref[:, cols]
            kh = k_s[:, kcols]
            vh = v_s[:, kcols]
            doh = do_ref[:, cols]
            s = lax.dot_general(qh, kh, (((1,), (1,)), ((), ())), preferred_element_type=F32) * scale
            s = jnp.where(ok, s - _alibi_slope(h) * distf, NEG_INF)
            lse_h = lse_ref[:, h:h + 1]
            p = jnp.exp(s - lse_h)
            dp = lax.dot_general(doh, vh, (((1,), (1,)), ((), ())), preferred_element_type=F32)
            delta = jnp.sum(doh.astype(F32) * o_ref[:, cols].astype(F32), axis=-1, keepdims=True)
            ds = (p * (dp - delta)).astype(BF16)
            dsink_ref[:, h:h + 1] += -jnp.exp(sink_ref[0, h] - lse_h) * delta
            dq = jnp.dot(ds, kh, preferred_element_type=F32) * scale
            dqkv_ref[:, cols] = dq.astype(BF16)
            dkv_s[:, kcols] += lax.dot_general(ds, qh, (((0,), (0,)), ((), ())), preferred_element_type=F32) * scale
            dkv_s[:, vcols] += lax.dot_general(p.astype(BF16), doh, (((0,), (0,)), ((), ())), preferred_element_type=F32)

        meta_s[...] += dkv_s[0:BLOCK, :]
        cur = dkv_s[2 * BLOCK:3 * BLOCK, :] + carry_s[...]
        carry_s[...] = dkv_s[BLOCK:2 * BLOCK, :]

        @pl.when(n > 0)
        def _():
            dqkv_ref[:, N_HEADS * HEAD_DIM:QKV_DIM] = cur.astype(BF16)

        @pl.when(n == 0)
        def _():
            dqkv_ref[:, N_HEADS * HEAD_DIM:QKV_DIM] = (cur + meta_s[...]).astype(BF16)

    blk = lambda b, r: (b * nb + nq(r), 0)
    return pl.pallas_call(
        body, name="attn_bwd", grid=(n_ex, nb),
        in_specs=[pl.BlockSpec(memory_space=pltpu.SMEM),
                  pl.BlockSpec((BLOCK, N_HEADS * HEAD_DIM), blk),
                  pl.BlockSpec((BLOCK, KV_DIM), lambda b, r: (b * nb, kvb)),
                  pl.BlockSpec((BLOCK, KV_DIM), lambda b, r: (b * nb + jnp.maximum(nq(r) - 1, 0), kvb)),
                  pl.BlockSpec((BLOCK, KV_DIM), lambda b, r: (b * nb + nq(r), kvb)),
                  pl.BlockSpec((BLOCK, N_HEADS * HEAD_DIM), blk),
                  pl.BlockSpec((BLOCK, N_HEADS), blk),
                  pl.BlockSpec((BLOCK, N_HEADS * HEAD_DIM), blk)],
        out_specs=[pl.BlockSpec((BLOCK, QKV_DIM), blk),
                   pl.BlockSpec((BLOCK, N_HEADS), lambda b, r: (0, 0))],
        out_shape=[SDS((n_rows, QKV_DIM), BF16), SDS((BLOCK, N_HEADS), F32)],
        scratch_shapes=[pltpu.VMEM((3 * BLOCK, N_KV * HEAD_DIM), BF16), pltpu.VMEM((3 * BLOCK, N_KV * HEAD_DIM), BF16),
                        pltpu.VMEM((3 * BLOCK, KV_DIM), F32), pltpu.VMEM((BLOCK, KV_DIM), F32),
                        pltpu.VMEM((BLOCK, KV_DIM), F32)],
        compiler_params=_params(2),
    )(sinks, qkv, qkv, qkv, qkv, o, lse, do)


def _cmul_add(xr, xi, mr, mi, sr, si):
    return xr + mr * sr - mi * si, xi + mr * si + mi * sr


def _scan_tiles(buf, tab_ref, carry_s, n_groups, reverse):
    shifts = (7, 6, 4) if reverse else (1, 2, 4)

    def group(gi, carry):
        g = (n_groups - 1 - gi) if reverse else gi
        row = pl.multiple_of(g * 8, 8)
        out = []
        for j in range(PAIRS_PER_CHUNK):
            xr = buf[j, pl.ds(row, 8), 0:128]
            xi = buf[j, pl.ds(row, 8), 128:256]
            for lvl, sh in enumerate(shifts):
                xr, xi = _cmul_add(xr, xi, tab_ref[j, 2 * lvl], tab_ref[j, 2 * lvl + 1],
                                   pltpu.roll(xr, sh, 0), pltpu.roll(xi, sh, 0))
            xr, xi = _cmul_add(xr, xi, tab_ref[j, 6], tab_ref[j, 7], carry[2 * j], carry[2 * j + 1])
            buf[j, pl.ds(row, 8), 0:128] = xr
            buf[j, pl.ds(row, 8), 128:256] = xi
            edge = slice(0, 1) if reverse else slice(7, 8)
            out += [jnp.broadcast_to(xr[edge], (8, 128)), jnp.broadcast_to(xi[edge], (8, 128))]
        return tuple(out)

    carry0 = tuple(carry_s[k] for k in range(2 * PAIRS_PER_CHUNK))
    carry = lax.fori_loop(0, n_groups, group, carry0)
    for k in range(2 * PAIRS_PER_CHUNK):
        carry_s[k] = carry[k]


def _ssm_fwd(u, b_pad, c_pad, tab, d_skip, n_ex, lp):
    n_rows = u.shape[0]
    n_t = lp // TM
    n_chunk = D_MODEL // 128

    def body(u_ref, bp_ref, cp_ref, tab_ref, d_ref, yg_ref, y_ref, xs_ref, buf, carry_s):
        @pl.when(pl.program_id(2) == 0)
        def _():
            carry_s[...] = jnp.zeros_like(carry_s)

        ub = u_ref[...]
        u16 = ub.astype(BF16)
        for j in range(PAIRS_PER_CHUNK):
            buf[j] = jnp.dot(u16, bp_ref[j], preferred_element_type=F32)
        _scan_tiles(buf, tab_ref, carry_s, TM // 8, reverse=False)
        y = d_ref[...] * ub
        for j in range(PAIRS_PER_CHUNK):
            xb = buf[j].astype(BF16)
            xs_ref[j] = xb
            y = y + jnp.dot(xb, cp_ref[j], preferred_element_type=F32)
        y_ref[...] = y
        yg_ref[...] = _gelu(y).astype(BF16)

    rows = lambda b, q, t: (b * n_t + t, q)
    return pl.pallas_call(
        body, name="ssm_fwd", grid=(n_ex, n_chunk, n_t),
        in_specs=[pl.BlockSpec((TM, 128), rows),
                  pl.BlockSpec((PAIRS_PER_CHUNK, 128, 256), lambda b, q, t: (q, 0, 0)),
                  pl.BlockSpec((PAIRS_PER_CHUNK, 256, 128), lambda b, q, t: (q, 0, 0)),
                  pl.BlockSpec((PAIRS_PER_CHUNK, 8, 8, 128), lambda b, q, t: (q, 0, 0, 0)),
                  pl.BlockSpec((1, 128), lambda b, q, t: (0, q))],
        out_specs=[pl.BlockSpec((TM, 128), rows), pl.BlockSpec((TM, 128), rows),
                   pl.BlockSpec((PAIRS_PER_CHUNK, TM, 256), lambda b, q, t: (q, b * n_t + t, 0))],
        out_shape=[SDS((n_rows, D_MODEL), BF16), SDS((n_rows, D_MODEL), F32), SDS((N_PAIR, n_rows, 256), BF16)],
        scratch_shapes=[pltpu.VMEM((PAIRS_PER_CHUNK, TM, 256), F32), pltpu.VMEM((2 * PAIRS_PER_CHUNK, 8, 128), F32)],
        compiler_params=_params(3),
    )(u, b_pad, c_pad, tab, d_skip)


def _ssm_bwd(dyg, y, u, xs, ct_pad, bt_pad, tab_rev, d_skip, n_ex, lp):
    n_rows = u.shape[0]
    n_t = lp // TM
    n_chunk = D_MODEL // 128
    tile = lambda q, b, t: (b * n_t + (n_t - 1 - t), q)

    def body(dyg_ref, y_ref, u_ref, xs_ref, xp_ref, ct_ref, bt_ref, tab_ref, d_ref,
             du_ref, db_ref, dc_ref, da_ref, dd_ref, buf, xf, carry_s):
        b, t = pl.program_id(1), pl.program_id(2)

        @pl.when((b == 0) & (t == 0))
        def _():
            db_ref[...] = jnp.zeros_like(db_ref)
            dc_ref[...] = jnp.zeros_like(dc_ref)
            da_ref[...] = jnp.zeros_like(da_ref)
            dd_ref[...] = jnp.zeros_like(dd_ref)

        @pl.when(t == 0)
        def _():
            carry_s[...] = jnp.zeros_like(carry_s)

        ub = u_ref[...]
        dy = dyg_ref[...].astype(F32) * _gelu_grad(y_ref[...])
        dd_ref[...] += _fold8(dy * ub)
        dy16 = dy.astype(BF16)
        first_tile = t == n_t - 1
        for j in range(PAIRS_PER_CHUNK):
            buf[j] = jnp.dot(dy16, ct_ref[j], preferred_element_type=F32)
            dc_ref[j] += lax.dot_general(dy16, xs_ref[j], (((0,), (0,)), ((), ())), preferred_element_type=F32)
            xf[j, 16:16 + TM, :] = xs_ref[j].astype(F32)
            xf[j, 0:16, :] = jnp.where(first_tile, 0.0, xp_ref[j].astype(F32))
        _scan_tiles(buf, tab_ref, carry_s, TM // 8, reverse=True)
        du = d_ref[...] * dy
        u16 = ub.astype(BF16)
        for j in range(PAIRS_PER_CHUNK):
            g = buf[j]
            g16 = g.astype(BF16)
            du = du + jnp.dot(g16, bt_ref[j], preferred_element_type=F32)
            db_ref[j] += lax.dot_general(u16, g16, (((0,), (0,)), ((), ())), preferred_element_type=F32)
            xprev = pltpu.roll(xf[j], 1, 0)[16:16 + TM, :]
            gr, gi = g[:, 0:128], g[:, 128:256]
            pr, pi = xprev[:, 0:128], xprev[:, 128:256]
            da_ref[j, 0] += _fold8(gr * pr + gi * pi)
            da_ref[j, 1] += _fold8(gi * pr - gr * pi)
        du_ref[...] = du

    prev16 = lambda q, b, t: (q, jnp.maximum((b * n_t + (n_t - 1 - t)) * (TM // 16) - 1, 0), 0)
    return pl.pallas_call(
        body, name="ssm_bwd", grid=(n_chunk, n_ex, n_t),
        in_specs=[pl.BlockSpec((TM, 128), tile), pl.BlockSpec((TM, 128), tile), pl.BlockSpec((TM, 128), tile),
                  pl.BlockSpec((PAIRS_PER_CHUNK, TM, 256), lambda q, b, t: (q, b * n_t + (n_t - 1 - t), 0)),
                  pl.BlockSpec((PAIRS_PER_CHUNK, 16, 256), prev16),
                  pl.BlockSpec((PAIRS_PER_CHUNK, 128, 256), lambda q, b, t: (q, 0, 0)),
                  pl.BlockSpec((PAIRS_PER_CHUNK, 256, 128), lambda q, b, t: (q, 0, 0)),
                  pl.BlockSpec((PAIRS_PER_CHUNK, 8, 8, 128), lambda q, b, t: (q, 0, 0, 0)),
                  pl.BlockSpec((1, 128), lambda q, b, t: (0, q))],
        out_specs=[pl.BlockSpec((TM, 128), tile),
                   pl.BlockSpec((PAIRS_PER_CHUNK, 128, 256), lambda q, b, t: (q, 0, 0)),
                   pl.BlockSpec((PAIRS_PER_CHUNK, 128, 256), lambda q, b, t: (q, 0, 0)),
                   pl.BlockSpec((PAIRS_PER_CHUNK, 2, 8, 128), lambda q, b, t: (q, 0, 0, 0)),
                   pl.BlockSpec((8, 128), lambda q, b, t: (0, q))],
        out_shape=[SDS((n_rows, D_MODEL), F32), SDS((N_PAIR, 128, 256), F32), SDS((N_PAIR, 128, 256), F32),
                   SDS((N_PAIR, 2, 8, 128), F32), SDS((8, D_MODEL), F32)],
        scratch_shapes=[pltpu.VMEM((PAIRS_PER_CHUNK, TM, 256), F32), pltpu.VMEM((PAIRS_PER_CHUNK, TM + 16, 256), F32),
                        pltpu.VMEM((2 * PAIRS_PER_CHUNK, 8, 128), F32)],
        compiler_params=_params(3),
    )(dyg, y, u, xs, xs, ct_pad, bt_pad, tab_rev, d_skip)


def _rms_fwd(h, wn, name):
    n_rows = h.shape[0]

    def body(h_ref, wn_ref, o_ref):
        o_ref[...] = _rms(h_ref[...], wn_ref[...])

    return pl.pallas_call(
        body, name=name, grid=(n_rows // TM,),
        in_specs=[pl.BlockSpec((TM, D_MODEL), lambda i: (i, 0)), pl.BlockSpec((1, D_MODEL), lambda i: (0, 0))],
        out_specs=pl.BlockSpec((TM, D_MODEL), lambda i: (i, 0)),
        out_shape=SDS((n_rows, D_MODEL), F32), compiler_params=_params(1),
    )(h, wn)


def _rms_bwd_call(dhn, h, wn, dres, name):
    n_rows = h.shape[0]

    def body(dhn_ref, h_ref, wn_ref, dres_ref, o_ref, dw_ref):
        @pl.when(pl.program_id(0) == 0)
        def _():
            dw_ref[...] = jnp.zeros_like(dw_ref)

        dh, dw_rows = _rms_bwd(dhn_ref[...], h_ref[...], wn_ref[...])
        o_ref[...] = dres_ref[...] + dh
        dw_ref[...] += _fold8(dw_rows)

    row = lambda i: (i, 0)
    return pl.pallas_call(
        body, name=name, grid=(n_rows // TM,),
        in_specs=[pl.BlockSpec((TM, D_MODEL), row), pl.BlockSpec((TM, D_MODEL), row),
                  pl.BlockSpec((1, D_MODEL), lambda i: (0, 0)), pl.BlockSpec((TM, D_MODEL), row)],
        out_specs=[pl.BlockSpec((TM, D_MODEL), row), pl.BlockSpec((8, D_MODEL), lambda i: (0, 0))],
        out_shape=[SDS((n_rows, D_MODEL), F32), SDS((8, D_MODEL), F32)], compiler_params=_params(1),
    )(dhn, h, wn, dres)


def _glu_fwd(h, z):
    n_rows = h.shape[0]

    def body(h_ref, val_ref, gate_ref, o_ref):
        o_ref[...] = h_ref[...] + val_ref[...].astype(F32) * jax.nn.sigmoid(gate_ref[...].astype(F32))

    row = lambda i: (i, 0)
    return pl.pallas_call(
        body, name="glu_fwd", grid=(n_rows // TM,),
        in_specs=[pl.BlockSpec((TM, D_MODEL), row), pl.BlockSpec((TM, D_MODEL), row),
                  pl.BlockSpec((TM, D_MODEL), lambda i: (i, 1))],
        out_specs=pl.BlockSpec((TM, D_MODEL), row),
        out_shape=SDS((n_rows, D_MODEL), F32), compiler_params=_params(1),
    )(h, z, z)


def _glu_bwd(dh, z):
    n_rows = dh.shape[0]

    def body(dh_ref, val_ref, gate_ref, dval_ref, dgate_ref):
        sg = jax.nn.sigmoid(gate_ref[...].astype(F32))
        d = dh_ref[...]
        dval_ref[...] = (d * sg).astype(BF16)
        dgate_ref[...] = (d * val_ref[...].astype(F32) * sg * (1.0 - sg)).astype(BF16)

    row = lambda i: (i, 0)
    dval, dgate = pl.pallas_call(
        body, name="glu_bwd", grid=(n_rows // TM,),
        in_specs=[pl.BlockSpec((TM, D_MODEL), row), pl.BlockSpec((TM, D_MODEL), row),
                  pl.BlockSpec((TM, D_MODEL), lambda i: (i, 1))],
        out_specs=[pl.BlockSpec((TM, D_MODEL), row), pl.BlockSpec((TM, D_MODEL), row)],
        out_shape=[SDS((n_rows, D_MODEL), BF16), SDS((n_rows, D_MODEL), BF16)], compiler_params=_params(1),
    )(dh, z, z)
    return dval, dgate


def _loss_head(h, wn, target, n_ex, nb):
    n_rows = h.shape[0]

    def body(h_ref, wn_ref, t_ref, dh_ref, loss_ref, dw_ref):
        b, n = pl.program_id(0), pl.program_id(1)

        @pl.when((b == 0) & (n == 0))
        def _():
            loss_ref[...] = jnp.zeros_like(loss_ref)
            dw_ref[...] = jnp.zeros_like(dw_ref)

        @pl.when(n == 0)
        def _():
            dh_ref[...] = jnp.zeros_like(dh_ref)

        @pl.when(n > 0)
        def _():
            hh = h_ref[...]
            diff = _rms(hh, wn_ref[...]) - t_ref[...]
            loss_ref[...] += 0.5 * jnp.sum(diff * diff) * (1.0 / D_MODEL)
            dh, dw_rows = _rms_bwd(diff * (1.0 / D_MODEL), hh, wn_ref[...])
            dh_ref[...] = dh
            dw_ref[...] += _fold8(dw_rows)

    return pl.pallas_call(
        body, name="loss_head", grid=(n_ex, nb),
        in_specs=[pl.BlockSpec((BLOCK, D_MODEL), lambda b, n: (b * nb + n, 0)),
                  pl.BlockSpec((1, D_MODEL), lambda b, n: (0, 0)),
                  pl.BlockSpec((BLOCK, D_MODEL), lambda b, n: (b * (nb - 1) + jnp.maximum(n - 1, 0), 0))],
        out_specs=[pl.BlockSpec((BLOCK, D_MODEL), lambda b, n: (b * nb + n, 0)),
                   pl.BlockSpec((8, 128), lambda b, n: (0, 0)),
                   pl.BlockSpec((8, D_MODEL), lambda b, n: (0, 0))],
        out_shape=[SDS((n_rows, D_MODEL), F32), SDS((8, 128), F32), SDS((8, D_MODEL), F32)],
        compiler_params=_params(2),
    )(h, wn, target)


def _adamw(pieces, w, m, v, name):
    n_layers = len(pieces)
    rows, cols = pieces[0].shape[1:]
    rb = rows
    for cand in (256, 136, 128, 64, 32, 16, 8):
        if rows % cand == 0 and rows > cand:
            rb = cand
            break
    n_blk = rows // rb
    c1 = 1.0 / (1.0 - ADAM_B1 ** ADAM_STEP)
    c2 = 1.0 / (1.0 - ADAM_B2 ** ADAM_STEP)

    def body(*refs):
        p_refs = refs[:n_layers]
        w_ref, m_ref, v_ref, g_out, d_out, m_out, v_out = refs[n_layers:]
        layer = pl.program_id(0)
        g = None
        for l, p_ref in enumerate(p_refs):
            gl = p_ref[0].astype(F32)
            for k in range(1, N_DEV):
                gl = gl + p_ref[k].astype(F32)
            g = gl if g is None else jnp.where(layer == l, gl, g)
        m_new = ADAM_B1 * m_ref[...] + (1.0 - ADAM_B1) * g
        v_new = ADAM_B2 * v_ref[...] + (1.0 - ADAM_B2) * (g * g)
        g_out[...] = g
        m_out[...] = m_new
        v_out[...] = v_new
        d_out[...] = -ADAM_LR * ((m_new * c1) / (jnp.sqrt(v_new * c2) + ADAM_EPS) + ADAM_WD * w_ref[...])

    def piece_spec(l):
        return pl.BlockSpec((N_DEV, rb, cols), lambda ly, i: (0, jnp.where(ly == l, i, 0), 0))

    blk = pl.BlockSpec((rb, cols), lambda ly, i: (ly * n_blk + i, 0))
    return pl.pallas_call(
        body, name=name, grid=(n_layers, n_blk),
        in_specs=[piece_spec(l) for l in range(n_layers)] + [blk, blk, blk],
        out_specs=[blk, blk, blk, blk],
        out_shape=[SDS((n_layers * rows, cols), F32)] * 4, compiler_params=_params(2),
    )(*pieces, w, m, v)


_HBM = pl.BlockSpec(memory_space=pltpu.HBM)
_SEM = pl.BlockSpec(memory_space=pltpu.SEMAPHORE)
_EFFECT = pltpu.SideEffectType.DATAFLOW_SIDE_EFFECTING
N_GATHER_PEERS = N_CHIPS - 1
N_EXCHANGE_PEERS = N_DEV - 1


def _gather_copies(srcs, lands, send_sems, recv_sems):
    x, y, c = lax.axis_index("x"), lax.axis_index("y"), lax.axis_index("c")
    mine = 2 * x + y
    chips = [(1 - x, y), (x, 1 - y), (1 - x, 1 - y)]
    out, inc = [], []
    for a in range(len(srcs)):
        for k, (px, py) in enumerate(chips):
            j = a * N_GATHER_PEERS + k
            sems = dict(send_sem=send_sems.at[j], recv_sem=recv_sems.at[j], device_id=(px, py, c),
                        device_id_type=pl.DeviceIdType.MESH)
            out.append(pltpu.make_async_remote_copy(src_ref=srcs[a], dst_ref=lands[a].at[mine], **sems))
            inc.append(pltpu.make_async_remote_copy(src_ref=srcs[a], dst_ref=lands[a].at[2 * px + py], **sems))
    return out, inc


def _exchange_copies(n_scatter):
    def copies(srcs, lands, send_sems, recv_sems):
        x, y, c = lax.axis_index("x"), lax.axis_index("y"), lax.axis_index("c")
        me = 4 * x + 2 * y + c
        peers = [(x ^ (k >> 2), y ^ ((k >> 1) & 1), c ^ (k & 1)) for k in range(1, N_DEV)]
        out, inc = [], []
        for a in range(len(srcs)):
            for k, (px, py, pc) in enumerate(peers):
                j = a * N_EXCHANGE_PEERS + k
                sems = dict(send_sem=send_sems.at[j], recv_sem=recv_sems.at[j], device_id=(px, py, pc),
                            device_id_type=pl.DeviceIdType.MESH)
                theirs = srcs[a].at[2 * px + py] if a < n_scatter else srcs[a]
                mine = srcs[a].at[2 * x + y] if a < n_scatter else srcs[a]
                out.append(pltpu.make_async_remote_copy(src_ref=theirs, dst_ref=lands[a].at[me], **sems))
                inc.append(pltpu.make_async_remote_copy(src_ref=mine, dst_ref=lands[a].at[4 * px + 2 * py + pc], **sems))
        return out, inc

    return copies


def _split_start(groups, copies_fn, n_peers, name):
    sizes = [len(srcs) for srcs, _ in groups]
    flat = [a for srcs, lands in groups for a in list(srcs) + list(lands)]
    n_flat, n_grp = len(flat), len(groups)

    def body(*refs):
        sems = refs[2 * n_flat:2 * n_flat + 2 * n_grp]
        token = refs[-1]
        at = 0
        for gi, n in enumerate(sizes):
            out, _ = copies_fn(refs[at:at + n], refs[at + n:at + 2 * n], sems[2 * gi], sems[2 * gi + 1])
            for cp in out:
                cp.start()
            at += 2 * n
        token[...] = jnp.zeros_like(token)

    sem_shapes = []
    for n in sizes:
        sem_shapes += [pltpu.SemaphoreType.DMA((n * n_peers,)), pltpu.SemaphoreType.DMA((n * n_peers,))]
    res = pl.pallas_call(
        body, name=name,
        out_shape=(*[pltpu.HBM(a.shape, a.dtype) for a in flat], *sem_shapes, SDS((8, 128), F32)),
        in_specs=[_HBM] * n_flat,
        out_specs=(*[_HBM] * n_flat, *[_SEM] * (2 * n_grp), pl.BlockSpec(memory_space=pltpu.VMEM)),
        input_output_aliases={i: i for i in range(n_flat)},
        compiler_params=pltpu.CompilerParams(has_side_effects=_EFFECT),
    )(*[pltpu.with_memory_space_constraint(a, pltpu.HBM) for a in flat])
    handles, at = [], 0
    for gi, n in enumerate(sizes):
        handles.append((res[n_flat + 2 * gi], res[n_flat + 2 * gi + 1], list(res[at:at + n]), list(res[at + n:at + 2 * n])))
        at += 2 * n
    return handles, res[-1]


def _split_wait(handle, after, copies_fn, name):
    send_sems, recv_sems, srcs, lands = handle
    n = len(srcs)

    def body(*refs):
        out, inc = copies_fn(refs[:n], refs[n:2 * n], refs[2 * n], refs[2 * n + 1])
        for cp in out:
            cp.wait_send()
        for cp in inc:
            cp.wait_recv()

    flat = list(srcs) + list(lands)
    res = pl.pallas_call(
        body, name=name,
        out_shape=tuple(pltpu.HBM(a.shape, a.dtype) for a in flat),
        in_specs=[_HBM] * (2 * n) + [_SEM, _SEM, pl.BlockSpec(memory_space=pl.ANY)],
        out_specs=tuple([_HBM] * (2 * n)),
        input_output_aliases={i: i for i in range(2 * n)},
        compiler_params=pltpu.CompilerParams(has_side_effects=_EFFECT),
    )(*flat, send_sems, recv_sems, after)
    return list(res[n:])


def _landing(own, slot, n_slots):
    return lax.dynamic_update_index_in_dim(lax.empty((n_slots,) + own.shape, own.dtype), own, slot, 0)


def _ssm_discretize(lam_re, lam_im, log_dt, b_re, b_im):
    lr = jnp.minimum(lam_re, LAMBDA_RE_MAX)
    li = lam_im
    dt = jnp.exp(log_dt)[:, None]
    mag = jnp.exp(lr * dt)
    ar, ai = mag * jnp.cos(li * dt), mag * jnp.sin(li * dt)
    den = lr * lr + li * li
    nr, ni = ar - 1.0, ai
    gr, gi = (nr * lr + ni * li) / den, (ni * lr - nr * li) / den
    bbr = gr[:, :, None] * b_re - gi[:, :, None] * b_im
    bbi = gr[:, :, None] * b_im + gi[:, :, None] * b_re
    return ar, ai, bbr, bbi


def _pair_lanes(t):
    return t.reshape(N_PAIR, 2 * SSM_STATE)


def _chan_state_blocks(t_gcp):
    t = t_gcp.reshape(N_PAIR, 2, SSM_GROUP, SSM_STATE)
    eye2 = jnp.eye(2, dtype=t.dtype)
    blk = jnp.einsum("rgcp,gh->rgchp", t, eye2).reshape(N_PAIR, 2 * SSM_GROUP, 2 * SSM_STATE)
    place = jax.nn.one_hot(jnp.arange(N_PAIR) % PAIRS_PER_CHUNK, PAIRS_PER_CHUNK, dtype=t.dtype)
    return jnp.einsum("rcl,rj->rjcl", blk, place).reshape(N_PAIR, 128, 2 * SSM_STATE)


def _chan_state_unblock(t):
    t = t.reshape(N_PAIR, PAIRS_PER_CHUNK, 2, SSM_GROUP, 2, SSM_STATE)
    place = jax.nn.one_hot(jnp.arange(N_PAIR) % PAIRS_PER_CHUNK, PAIRS_PER_CHUNK, dtype=t.dtype)
    t = jnp.einsum("rjgchp,rj->rgchp", t, place)
    t = jnp.einsum("rgchp,gh->rgcp", t, jnp.eye(2, dtype=t.dtype))
    return t.reshape(SSM_NG, SSM_GROUP, SSM_STATE)


def _scan_tables(ar, ai, reverse):
    ar, ai = _pair_lanes(ar), _pair_lanes(ai)
    if reverse:
        ai = -ai
    pows = [(ar, ai)]
    for _ in range(7):
        pr, pi = pows[-1]
        pows.append((pr * ar - pi * ai, pr * ai + pi * ar))
    rows = jnp.arange(8)[None, :, None]
    tiles = []
    for k in (1, 2, 4):
        keep = (rows <= 7 - k) if reverse else (rows >= k)
        for part in pows[k - 1]:
            tiles.append(jnp.where(keep, part[:, None, :], 0.0))
    order = list(range(7, -1, -1)) if reverse else list(range(8))
    for comp in (0, 1):
        tiles.append(jnp.stack([pows[e][comp] for e in order], axis=1))
    return jnp.stack(tiles, axis=1)


def _local_step(x, target, w, late_weights, on_grads):
    n_ex, seq, _ = x.shape
    lp = seq + BLOCK
    nb = lp // BLOCK
    n_rows = n_ex * lp
    g = {}

    head = jnp.concatenate([jnp.zeros((PAD, D_MODEL), F32), w["meta_tokens"]], axis=0)
    h0 = jnp.concatenate([jnp.broadcast_to(head[None], (n_ex, BLOCK, D_MODEL)), x], axis=1).reshape(n_rows, D_MODEL)

    qkv, hn_a = _rms_mm_cols(h0, w["attn_norm_w"], w["attn_w_qkv"], 0, "qkv_fwd")
    att, lse = _attn_fwd(qkv, w["attn_sinks"], n_ex, nb)
    h1 = _mm_acc(att, w["attn_w_o"], 0, False, "attn_out_fwd", res=h0)
    w = {**w, **late_weights(att)}
    h2, a0, hn_m0 = _mlp_fwd(h1, w["mlp_norm_w"][0:1], w["mlp_w_up"], w["mlp_w_down"], 0, "mlp0_fwd")

    ar, ai, bbr, bbi = _ssm_discretize(w["ssm_lambda_re"], w["ssm_lambda_im"], w["ssm_log_dt"], w["ssm_b_re"], w["ssm_b_im"])
    b_blk = jnp.concatenate([_chan_state_blocks(jnp.swapaxes(bbr, 1, 2)), _chan_state_blocks(jnp.swapaxes(bbi, 1, 2))], axis=2)
    c_blk = jnp.concatenate([_chan_state_blocks(w["ssm_c_re"]), -_chan_state_blocks(w["ssm_c_im"])], axis=2)
    b_pad, bt_pad = b_blk.astype(BF16), jnp.swapaxes(b_blk, 1, 2).astype(BF16)
    ct_pad, c_pad = c_blk.astype(BF16), jnp.swapaxes(c_blk, 1, 2).astype(BF16)
    u = _rms_fwd(h2, w["ssm_norm_w"], "ssm_norm_fwd")
    yg, y, xs = _ssm_fwd(u, b_pad, c_pad, _scan_tables(ar, ai, False), w["ssm_d"], n_ex, lp)
    z = _mm_cols(yg, w["ssm_w_glu"], 0, False, "glu_mm_fwd")
    h3 = _glu_fwd(h2, z)
    h4, a1, hn_m1 = _mlp_fwd(h3, w["mlp_norm_w"][1:2], w["mlp_w_up"], w["mlp_w_down"], 1, "mlp1_fwd")

    dh4, loss_tile, dnorm_f = _loss_head(h4, w["final_norm_w"], target.reshape(n_ex * seq, D_MODEL), n_ex, nb)

    def mlp_bwd(dh_out, h_in, a, hn, layer, tag, norm_w):
        da = _mm_cols(dh_out, w["mlp_w_down"], layer, True, tag + "_dact", mul2relu=a)
        dh_in, dnorm = _mm_acc(da, w["mlp_w_up"], layer, True, tag + "_dx", rms_bwd=(h_in, norm_w, dh_out))
        dw_down = _mm_tn(a, dh_out, N_CHIPS, True, tag + "_dwdown", relu2_a=True)
        dw_up = _mm_tn(hn, da, N_CHIPS, False, tag + "_dwup")
        return dh_in, dnorm, dw_up, dw_down

    dh3, dnorm_m1, dwu1, dwd1 = mlp_bwd(dh4, h3, a1, hn_m1, 1, "mlp1", w["mlp_norm_w"][1:2])
    tok = on_grads("mlp1", {"mlp_w_up": dwu1, "mlp_w_down": dwd1})
    dval, dgate = _glu_bwd(dh3, z)
    dz = jnp.concatenate([dval, dgate], axis=1)
    dyg = _mm_acc(dz, w["ssm_w_glu"], 0, True, "glu_mm_dx", out_dtype=BF16)
    g["ssm_w_glu"] = _mm_tn(yg, dz, N_CHIPS, False, "glu_mm_dw")
    du, db_blk, dc_blk, da_t, dd_t = _ssm_bwd(dyg, y, u, xs, ct_pad, bt_pad, _scan_tables(ar, ai, True),
                                              w["ssm_d"] + tok, n_ex, lp)
    dh2, dnorm_s = _rms_bwd_call(du, h2, w["ssm_norm_w"], dh3, "ssm_norm_bwd")
    g["ssm_c_re"] = _chan_state_unblock(dc_blk[:, :, 0:128])
    g["ssm_c_im"] = -_chan_state_unblock(dc_blk[:, :, 128:256])
    g_bbr = jnp.swapaxes(_chan_state_unblock(db_blk[:, :, 0:128]), 1, 2)
    g_bbi = jnp.swapaxes(_chan_state_unblock(db_blk[:, :, 128:256]), 1, 2)
    g_a = jnp.sum(da_t, axis=2).reshape(N_PAIR, 2, 2, SSM_STATE)
    g_ar, g_ai = g_a[:, 0].reshape(SSM_NG, SSM_STATE), g_a[:, 1].reshape(SSM_NG, SSM_STATE)
    _, vjp = jax.vjp(_ssm_discretize, w["ssm_lambda_re"], w["ssm_lambda_im"], w["ssm_log_dt"], w["ssm_b_re"], w["ssm_b_im"])
    g["ssm_lambda_re"], g["ssm_lambda_im"], g["ssm_log_dt"], g["ssm_b_re"], g["ssm_b_im"] = vjp((g_ar, g_ai, g_bbr, g_bbi))
    tok = on_grads("ssm", g)
    g = {}
    dh1, dnorm_m0, dwu0, dwd0 = mlp_bwd(dh2, h1, a0, hn_m0, 0, "mlp0", w["mlp_norm_w"][0:1] + tok)
    datt = _mm_cols(dh1, w["attn_w_o"], 0, True, "attn_out_dx")
    dw_o = _mm_tn(att, dh1, N_CHIPS, True, "attn_out_dw")
    tok = on_grads("mlp0", {"mlp_w_up": dwu0, "mlp_w_down": dwd0, "attn_w_o": dw_o})
    dqkv, dsink_rows = _attn_bwd(qkv, w["attn_sinks"] + tok, att, lse, datt, n_ex, nb)
    tok = on_grads("qkv", {"attn_w_qkv": _mm_tn(hn_a, dqkv, N_CHIPS, False, "qkv_dw")})
    dh0, dnorm_a = _mm_acc(dqkv, w["attn_w_qkv"], 0, True, "qkv_dx", rms_bwd=(h0, w["attn_norm_w"] + tok, dh1))

    dh0 = dh0.reshape(n_ex, lp, D_MODEL)
    on_grads("rest", {
        "mlp_norm_w": jnp.stack([jnp.sum(dnorm_m0, axis=0), jnp.sum(dnorm_m1, axis=0)]),
        "final_norm_w": jnp.sum(dnorm_f, axis=0),
        "attn_norm_w": jnp.sum(dnorm_a, axis=0)[None],
        "ssm_norm_w": jnp.sum(dnorm_s, axis=0)[None],
        "attn_sinks": jnp.sum(dsink_rows, axis=0)[None],
        "ssm_d": jnp.sum(dd_t, axis=0)[None],
        "meta_tokens": jnp.sum(dh0[:, PAD:BLOCK], axis=0)})
    return loss_tile, dh0[:, BLOCK:]


_SHARDED_SMALL = ("meta_tokens", "ssm_norm_w", "ssm_d")
_REP_SSM = ("ssm_lambda_re", "ssm_lambda_im", "ssm_log_dt", "ssm_b_re", "ssm_b_im", "ssm_c_re", "ssm_c_im")
_REP_MISC = ("attn_norm_w", "attn_sinks", "mlp_norm_w", "final_norm_w")
_BIG = ("attn_w_qkv", "attn_w_o", "ssm_w_glu", "mlp_w_up", "mlp_w_down")


def _pack(parts, cols):
    flat = jnp.concatenate([p.reshape(-1) for p in parts])
    rows = -(-flat.shape[0] // (8 * cols)) * 8
    return jnp.pad(flat, (0, rows * cols - flat.shape[0])).reshape(rows, cols)


def _unpack(packed, like):
    flat = packed.reshape(-1)
    out, at = [], 0
    for p in like:
        out.append(flat[at:at + p.size].reshape(p.shape))
        at += p.size
    return out


def kernel(x, meta_tokens, attn_norm_w, attn_w_qkv, attn_sinks, attn_w_o, ssm_norm_w, ssm_lambda_re, ssm_lambda_im, ssm_log_dt, ssm_b_re, ssm_b_im, ssm_c_re, ssm_c_im, ssm_d, ssm_w_glu, mlp_norm_w, mlp_w_up, mlp_w_down, final_norm_w, loss_target, m_meta_tokens, m_attn_norm_w, m_attn_w_qkv, m_attn_sinks, m_attn_w_o, m_ssm_norm_w, m_ssm_lambda_re, m_ssm_lambda_im, m_ssm_log_dt, m_ssm_b_re, m_ssm_b_im, m_ssm_c_re, m_ssm_c_im, m_ssm_d, m_ssm_w_glu, m_mlp_norm_w, m_mlp_w_up, m_mlp_w_down, m_final_norm_w, v_meta_tokens, v_attn_norm_w, v_attn_w_qkv, v_attn_sinks, v_attn_w_o, v_ssm_norm_w, v_ssm_lambda_re, v_ssm_lambda_im, v_ssm_log_dt, v_ssm_b_re, v_ssm_b_im, v_ssm_c_re, v_ssm_c_im, v_ssm_d, v_ssm_w_glu, v_mlp_norm_w, v_mlp_w_up, v_mlp_w_down, v_final_norm_w):
    names = ("meta_tokens", "attn_norm_w", "attn_w_qkv", "attn_sinks", "attn_w_o", "ssm_norm_w", "ssm_lambda_re",
             "ssm_lambda_im", "ssm_log_dt", "ssm_b_re", "ssm_b_im", "ssm_c_re", "ssm_c_im", "ssm_d", "ssm_w_glu",
             "mlp_norm_w", "mlp_w_up", "mlp_w_down", "final_norm_w")
    wts = dict(zip(names, (meta_tokens, attn_norm_w, attn_w_qkv, attn_sinks, attn_w_o, ssm_norm_w, ssm_lambda_re,
                           ssm_lambda_im, ssm_log_dt, ssm_b_re, ssm_b_im, ssm_c_re, ssm_c_im, ssm_d, ssm_w_glu,
                           mlp_norm_w, mlp_w_up, mlp_w_down, final_norm_w)))
    mom = dict(zip(names, (m_meta_tokens, m_attn_norm_w, m_attn_w_qkv, m_attn_sinks, m_attn_w_o, m_ssm_norm_w,
                           m_ssm_lambda_re, m_ssm_lambda_im, m_ssm_log_dt, m_ssm_b_re, m_ssm_b_im, m_ssm_c_re,
                           m_ssm_c_im, m_ssm_d, m_ssm_w_glu, m_mlp_norm_w, m_mlp_w_up, m_mlp_w_down, m_final_norm_w)))
    var = dict(zip(names, (v_meta_tokens, v_attn_norm_w, v_attn_w_qkv, v_attn_sinks, v_attn_w_o, v_ssm_norm_w,
                           v_ssm_lambda_re, v_ssm_lambda_im, v_ssm_log_dt, v_ssm_b_re, v_ssm_b_im, v_ssm_c_re,
                           v_ssm_c_im, v_ssm_d, v_ssm_w_glu, v_mlp_norm_w, v_mlp_w_up, v_mlp_w_down, v_final_norm_w)))

    my_chip = 2 * lax.axis_index("x") + lax.axis_index("y")
    my_dev = 2 * my_chip + lax.axis_index("c")
    small_mine = _pack([wts[n] for n in _SHARDED_SMALL], 128)
    first = [attn_w_qkv.astype(BF16), attn_w_o.astype(BF16), small_mine]
    later = [ssm_w_glu.astype(BF16), mlp_w_up.astype(BF16), mlp_w_down.astype(BF16)]
    handles, _ = _split_start([(srcs, [_landing(a, my_chip, N_CHIPS) for a in srcs]) for srcs in (first, later)],
                              _gather_copies, N_GATHER_PEERS, "gather_start")
    got = _split_wait(handles[0], small_mine, _gather_copies, "gather_wait_first")
    full = {n: wts[n] for n in _REP_MISC}
    full["final_norm_w"] = final_norm_w[None]
    for n in _REP_SSM:
        full[n] = wts[n][0]
    full["attn_w_qkv"], full["attn_w_o"] = got[0], got[1]
    smalls = [_unpack(got[2][s], [wts[n] for n in _SHARDED_SMALL]) for s in range(N_CHIPS)]
    for k, n in enumerate(_SHARDED_SMALL):
        full[n] = jnp.concatenate([smalls[s][k] for s in range(N_CHIPS)], axis=1)

    def late_weights(after):
        glu, up, down = _split_wait(handles[1], after, _gather_copies, "gather_wait_later")
        return {"ssm_w_glu": glu, "mlp_w_up": up, "mlp_w_down": down}

    def shard_cols(t):
        return jnp.swapaxes(t.reshape(t.shape[0], N_CHIPS, t.shape[1] // N_CHIPS), 0, 1)

    pending = {}

    def on_grads(tag, g):
        scatter = [g[n] for n in _BIG if n in g]
        whole = []
        if tag == "ssm":
            whole = [_pack([g[n] for n in _REP_SSM], D_MODEL)]
        if tag == "rest":
            parts = [shard_cols(g[n]) for n in _SHARDED_SMALL]
            scatter = [jnp.stack([_pack([p[s] for p in parts], 128) for s in range(N_CHIPS)])]
            whole = [_pack([g[n] for n in _REP_MISC], D_MODEL)]
        srcs = scatter + whole
        lands = [_landing(lax.dynamic_index_in_dim(a, my_chip, 0, keepdims=False), my_dev, N_DEV) for a in scatter]
        lands += [_landing(a, my_dev, N_DEV) for a in whole]
        hs, token = _split_start([(srcs, lands)], _exchange_copies(len(scatter)), N_EXCHANGE_PEERS, "exchange_start_" + tag)
        pending[tag] = (hs[0], len(scatter))
        return token[0, 0]

    loss_tile, grad_x = _local_step(x, loss_target, full, late_weights, on_grads)
    loss = lax.psum(loss_tile[0, 0], ("x", "y", "c"))

    recv = {}
    for tag, (handle, n_scatter) in pending.items():
        recv[tag] = _split_wait(handle, grad_x, _exchange_copies(n_scatter), "exchange_wait_" + tag)

    out = {}

    def update(tag, pieces, w2, m2, v2):
        return _adamw(pieces, w2, m2, v2, "adamw_" + tag)

    def update_weight(n, pieces):
        shp = wts[n].shape
        r2 = (math.prod(shp[:-1]), shp[-1])
        res = update(n, pieces, wts[n].reshape(r2), mom[n].reshape(r2), var[n].reshape(r2))
        out[n] = [t.reshape(shp) for t in res]

    update_weight("mlp_w_up", [recv["mlp0"][1], recv["mlp1"][0]])
    update_weight("mlp_w_down", [recv["mlp0"][2], recv["mlp1"][1]])
    update_weight("attn_w_o", [recv["mlp0"][0]])
    update_weight("ssm_w_glu", [recv["ssm"][0]])
    update_weight("attn_w_qkv", [recv["qkv"][0]])
    for tag, group, pieces, cols in (("small", _SHARDED_SMALL, recv["rest"][0], 128),
                                     ("rep_ssm", _REP_SSM, recv["ssm"][1], D_MODEL),
                                     ("rep_misc", _REP_MISC, recv["rest"][1], D_MODEL)):
        like = [wts[n] for n in group]
        res = update(tag, [pieces], _pack(like, cols), _pack([mom[n] for n in group], cols),
                     _pack([var[n] for n in group], cols))
        for k, n in enumerate(group):
            out[n] = [_unpack(t, like)[k] for t in res]

    return (loss, grad_x, *[out[n][0] for n in names], *[out[n][1] for n in names],
            *[out[n][2] for n in names], *[out[n][3] for n in names])
```

```python
import functools
import math

import jax
import jax.numpy as jnp
from jax import lax
from jax.experimental import pallas as pl
from jax.experimental.pallas import tpu as pltpu

F32 = jnp.float32
BF16 = jnp.bfloat16
SDS = jax.ShapeDtypeStruct

D_MODEL = 1024
N_HEADS = 16
N_KV = 4
GQA = N_HEADS // N_KV
HEAD_DIM = 64
BLOCK = 128
N_META = 16
PAD = BLOCK - N_META
QKV_DIM = (N_HEADS + 2 * N_KV) * HEAD_DIM
KV_DIM = 2 * N_KV * HEAD_DIM
D_FF = 4 * D_MODEL
N_CHIPS = 4
N_DEV = 8
SSM_GROUP = 16
SSM_NG = D_MODEL // SSM_GROUP
SSM_STATE = 64
N_PAIR = SSM_NG // 2
PAIRS_PER_CHUNK = 4
RMS_EPS = 1e-6
NEG_INF = -1e30
LAMBDA_RE_MAX = -1e-4
ADAM_LR, ADAM_B1, ADAM_B2, ADAM_EPS, ADAM_WD, ADAM_STEP = 0.001, 0.9, 0.999, 1e-08, 0.01, 10

TM = 384
MM_TILES = (768, 384)
TN_TILES = (1408, 768, 384)
VMEM_LIMIT = 56 * 1024 * 1024


def _params(n_grid):
    return pltpu.CompilerParams(dimension_semantics=("arbitrary",) * n_grid, vmem_limit_bytes=VMEM_LIMIT)


def _row_tile(n_rows, tiles):
    return next(t for t in tiles if n_rows % t == 0)


def _rms(h, w):
    r = lax.rsqrt(jnp.mean(h * h, axis=-1, keepdims=True) + RMS_EPS)
    return h * r * w


def _rms_bwd(dhn, h, w):
    r = lax.rsqrt(jnp.mean(h * h, axis=-1, keepdims=True) + RMS_EPS)
    g = dhn * w
    proj = jnp.sum(g * h, axis=-1, keepdims=True) * (1.0 / D_MODEL)
    return r * g - h * (r * r * r) * proj, dhn * h * r


def _fold8(t):
    return jnp.sum(t.reshape(t.shape[0] // 8, 8, t.shape[1]), axis=0)


def _gelu(y):
    return 0.5 * y * (1.0 + jnp.tanh(0.7978845608028654 * (y + 0.044715 * y * y * y)))


def _gelu_grad(y):
    t = jnp.tanh(0.7978845608028654 * (y + 0.044715 * y * y * y))
    return 0.5 * (1.0 + t) + 0.5 * y * (1.0 - t * t) * 0.7978845608028654 * (1.0 + 3.0 * 0.044715 * y * y)


def _rms_mm_cols(h, wn, w4, layer, name):
    n_rows = h.shape[0]
    n_sh, _, k, n = w4.shape
    TM = _row_tile(n_rows, MM_TILES)

    def body(h_ref, wn_ref, w_ref, o_ref, hn_ref, hn_s):
        @pl.when(pl.program_id(1) == 0)
        def _():
            hn = _rms(h_ref[...], wn_ref[...]).astype(BF16)
            hn_s[...] = hn
            hn_ref[...] = hn

        o_ref[...] = jnp.dot(hn_s[...], w_ref[...], preferred_element_type=F32).astype(o_ref.dtype)

    return pl.pallas_call(
        body, name=name, grid=(n_rows // TM, n_sh),
        in_specs=[pl.BlockSpec((TM, k), lambda i, s: (i, 0)),
                  pl.BlockSpec((1, k), lambda i, s: (0, 0)),
                  pl.BlockSpec((None, None, k, n), lambda i, s: (s, layer, 0, 0))],
        out_specs=[pl.BlockSpec((TM, n), lambda i, s: (i, s)),
                   pl.BlockSpec((TM, k), lambda i, s: (i, 0))],
        out_shape=[SDS((n_rows, n_sh * n), BF16), SDS((n_rows, k), BF16)],
        scratch_shapes=[pltpu.VMEM((TM, k), BF16)],
        compiler_params=_params(2),
    )(h, wn, w4)


def _mm_cols(x, w4, layer, trans_w, name, mul2relu=None):
    n_rows, kx = x.shape
    TM = _row_tile(n_rows, MM_TILES)
    n_sh, _, k, n = w4.shape
    n_out = k if trans_w else n
    dims = (((1,), (1,)), ((), ())) if trans_w else (((1,), (0,)), ((), ()))

    def body(*refs):
        if mul2relu is None:
            x_ref, w_ref, o_ref = refs
        else:
            x_ref, w_ref, a_ref, o_ref = refs
        acc = lax.dot_general(x_ref[...].astype(BF16), w_ref[...], dims, preferred_element_type=F32)
        if mul2relu is not None:
            acc = acc * (2.0 * jnp.maximum(a_ref[...].astype(F32), 0.0))
        o_ref[...] = acc.astype(o_ref.dtype)

    in_specs = [pl.BlockSpec((TM, kx), lambda i, s: (i, 0)),
                pl.BlockSpec((None, None, k, n), lambda i, s: (s, layer, 0, 0))]
    args = [x, w4]
    if mul2relu is not None:
        in_specs.append(pl.BlockSpec((TM, n_out), lambda i, s: (i, s)))
        args.append(mul2relu)
    return pl.pallas_call(
        body, name=name, grid=(n_rows // TM, n_sh),
        in_specs=in_specs,
        out_specs=pl.BlockSpec((TM, n_out), lambda i, s: (i, s)),
        out_shape=SDS((n_rows, n_sh * n_out), BF16),
        compiler_params=_params(2),
    )(*args)


def _mm_acc(x, w4, layer, trans_w, name, res=None, rms_bwd=None, out_dtype=F32):
    n_rows = x.shape[0]
    TM = _row_tile(n_rows, MM_TILES)
    n_sh, _, k, n = w4.shape
    kx, n_out = (n, k) if trans_w else (k, n)
    dims = (((1,), (1,)), ((), ())) if trans_w else (((1,), (0,)), ((), ()))
    n_i = n_rows // TM

    def body(*refs):
        i, s = pl.program_id(0), pl.program_id(1)
        if rms_bwd is not None:
            x_ref, w_ref, h_ref, wn_ref, dres_ref, o_ref, dw_ref, acc = refs
        elif res is not None:
            x_ref, w_ref, res_ref, o_ref, acc = refs
        else:
            x_ref, w_ref, o_ref, acc = refs

        @pl.when(s == 0)
        def _():
            acc[...] = jnp.zeros_like(acc)

        acc[...] += lax.dot_general(x_ref[...].astype(BF16), w_ref[...], dims, preferred_element_type=F32)

        @pl.when(s == n_sh - 1)
        def _():
            if rms_bwd is not None:
                dh, dw_rows = _rms_bwd(acc[...], h_ref[...], wn_ref[...])
                o_ref[...] = (dres_ref[...] + dh).astype(o_ref.dtype)

                @pl.when(i == 0)
                def _():
                    dw_ref[...] = jnp.zeros_like(dw_ref)

                dw_ref[...] += _fold8(dw_rows)
            elif res is not None:
                o_ref[...] = (res_ref[...] + acc[...]).astype(o_ref.dtype)
            else:
                o_ref[...] = acc[...].astype(o_ref.dtype)

    row = lambda i, s: (i, 0)
    in_specs = [pl.BlockSpec((TM, kx), lambda i, s: (i, s)),
                pl.BlockSpec((None, None, k, n), lambda i, s: (s, layer, 0, 0))]
    args = [x, w4]
    out_specs = pl.BlockSpec((TM, n_out), row)
    out_shape = SDS((n_rows, n_out), out_dtype)
    if rms_bwd is not None:
        h, wn, dres = rms_bwd
        in_specs += [pl.BlockSpec((TM, n_out), row), pl.BlockSpec((1, n_out), lambda i, s: (0, 0)),
                     pl.BlockSpec((TM, n_out), row)]
        args += [h, wn, dres]
        out_specs = [out_specs, pl.BlockSpec((8, n_out), lambda i, s: (0, 0))]
        out_shape = [out_shape, SDS((8, n_out), F32)]
    elif res is not None:
        in_specs.append(pl.BlockSpec((TM, n_out), row))
        args.append(res)
    return pl.pallas_call(
        body, name=name, grid=(n_i, n_sh), in_specs=in_specs, out_specs=out_specs, out_shape=out_shape,
        scratch_shapes=[pltpu.VMEM((TM, n_out), F32)], compiler_params=_params(2),
    )(*args)


def _mm_tn(a, b, n_sh, a_sharded, name, relu2_a=False):
    n_rows = a.shape[0]
    TM = _row_tile(n_rows, TN_TILES)
    ka = a.shape[1] // n_sh if a_sharded else a.shape[1]
    nb = b.shape[1] if a_sharded else b.shape[1] // n_sh
    n_i = n_rows // TM

    def body(a_ref, b_ref, o_ref, acc):
        i = pl.program_id(1)

        @pl.when(i == 0)
        def _():
            acc[...] = jnp.zeros_like(acc)

        at = a_ref[...]
        if relu2_a:
            at = jnp.maximum(at.astype(F32), 0.0)
            at = at * at
        acc[...] += lax.dot_general(at.astype(BF16), b_ref[...].astype(BF16), (((0,), (0,)), ((), ())),
                                    preferred_element_type=F32)

        @pl.when(i == n_i - 1)
        def _():
            o_ref[...] = acc[...].astype(o_ref.dtype)

    a_spec = pl.BlockSpec((TM, ka), (lambda s, i: (i, s)) if a_sharded else (lambda s, i: (i, 0)))
    b_spec = pl.BlockSpec((TM, nb), (lambda s, i: (i, 0)) if a_sharded else (lambda s, i: (i, s)))
    return pl.pallas_call(
        body, name=name, grid=(n_sh, n_i), in_specs=[a_spec, b_spec],
        out_specs=pl.BlockSpec((None, ka, nb), lambda s, i: (s, 0, 0)),
        out_shape=SDS((n_sh, ka, nb), BF16),
        scratch_shapes=[pltpu.VMEM((ka, nb), F32)], compiler_params=_params(2),
    )(a, b)


def _mlp_fwd(h, wn, w_up4, w_down4, layer, name):
    n_rows = h.shape[0]
    TM = _row_tile(n_rows, MM_TILES)
    n_sh = w_up4.shape[0]
    f_sh = D_FF // n_sh

    def body(h_ref, wn_ref, wu_ref, wd_ref, o_ref, a_ref, hn_ref, hn_s, acc):
        s = pl.program_id(1)

        @pl.when(s == 0)
        def _():
            hn = _rms(h_ref[...], wn_ref[...]).astype(BF16)
            hn_s[...] = hn
            hn_ref[...] = hn
            acc[...] = jnp.zeros_like(acc)

        a = jnp.dot(hn_s[...], wu_ref[...], preferred_element_type=F32)
        a_ref[...] = a.astype(BF16)
        act = jnp.maximum(a, 0.0)
        acc[...] += jnp.dot((act * act).astype(BF16), wd_ref[...], preferred_element_type=F32)

        @pl.when(s == n_sh - 1)
        def _():
            o_ref[...] = h_ref[...] + acc[...]

    row = lambda i, s: (i, 0)
    return pl.pallas_call(
        body, name=name, grid=(n_rows // TM, n_sh),
        in_specs=[pl.BlockSpec((TM, D_MODEL), row), pl.BlockSpec((1, D_MODEL), lambda i, s: (0, 0)),
                  pl.BlockSpec((None, None, D_MODEL, f_sh), lambda i, s: (s, layer, 0, 0)),
                  pl.BlockSpec((None, None, f_sh, D_MODEL), lambda i, s: (s, layer, 0, 0))],
        out_specs=[pl.BlockSpec((TM, D_MODEL), row), pl.BlockSpec((TM, f_sh), lambda i, s: (i, s)),
                   pl.BlockSpec((TM, D_MODEL), row)],
        out_shape=[SDS((n_rows, D_MODEL), F32), SDS((n_rows, D_FF), BF16), SDS((n_rows, D_MODEL), BF16)],
        scratch_shapes=[pltpu.VMEM((TM, D_MODEL), BF16), pltpu.VMEM((TM, D_MODEL), F32)],
        compiler_params=_params(2),
    )(h, wn, w_up4, w_down4)


def _attn_masks(n):
    qi = lax.broadcasted_iota(jnp.int32, (BLOCK, 3 * BLOCK), 0)
    col = lax.broadcasted_iota(jnp.int32, (BLOCK, 3 * BLOCK), 1)
    kj = col - BLOCK
    dist = BLOCK + qi - kj
    kmin = jnp.where(n == 0, 2 * BLOCK, jnp.where(n == 1, BLOCK, 0))
    band_ok = (col >= BLOCK) & (dist >= 0) & (dist < BLOCK) & (kj >= kmin)
    q_pos = n * BLOCK + qi - PAD
    meta_ok = (col >= PAD) & (col < BLOCK) & (col - PAD <= q_pos)
    distf = jnp.where(col >= BLOCK, dist, 0).astype(F32)
    return band_ok | meta_ok, distf


def _alibi_slope(h):
    return float(2.0 ** (-8.0 * (h + 1) / N_HEADS))


def _attn_fwd(qkv, sinks, n_ex, nb):
    n_rows = qkv.shape[0]
    kvb = N_HEADS * HEAD_DIM // KV_DIM

    def body(sink_ref, q_ref, kvm_ref, kvp_ref, kvc_ref, o_ref, lse_ref, k_s, v_s, q_s):
        n = pl.program_id(1)
        ok, distf = _attn_masks(n)
        v_s[...] = jnp.ones_like(v_s)
        for part, ref in enumerate((kvm_ref, kvp_ref, kvc_ref)):
            rows = slice(part * BLOCK, (part + 1) * BLOCK)
            k_s[rows, :] = ref[:, 0:N_KV * HEAD_DIM]
            for kv in range(N_KV):
                v_s[rows, kv * 2 * HEAD_DIM:kv * 2 * HEAD_DIM + HEAD_DIM] = \
                    ref[:, (N_KV + kv) * HEAD_DIM:(N_KV + kv + 1) * HEAD_DIM]
        for kv in range(N_KV):
            for g in range(GQA):
                h = kv * GQA + g
                q_s[kv, g * BLOCK:(g + 1) * BLOCK, :] = q_ref[:, h * HEAD_DIM:(h + 1) * HEAD_DIM]
            s4 = lax.dot_general(q_s[kv], k_s[:, kv * HEAD_DIM:(kv + 1) * HEAD_DIM], (((1,), (1,)), ((), ())),
                                 preferred_element_type=F32) * (HEAD_DIM ** -0.5)
            es, ms, sink_es = [], [], []
            for g in range(GQA):
                h = kv * GQA + g
                s = jnp.where(ok, s4[g * BLOCK:(g + 1) * BLOCK] - _alibi_slope(h) * distf, NEG_INF)
                sink = sink_ref[0, h]
                m = jnp.maximum(jnp.max(s, axis=-1, keepdims=True), sink)
                es.append(jnp.exp(s - m).astype(BF16))
                ms.append(m)
                sink_es.append(jnp.exp(sink - m))
            pv = jnp.dot(jnp.concatenate(es, axis=0), v_s[:, kv * 2 * HEAD_DIM:(kv + 1) * 2 * HEAD_DIM],
                         preferred_element_type=F32)
            for g in range(GQA):
                h = kv * GQA + g
                pg = pv[g * BLOCK:(g + 1) * BLOCK]
                l = pg[:, HEAD_DIM:HEAD_DIM + 1] + sink_es[g]
                o_ref[:, h * HEAD_DIM:(h + 1) * HEAD_DIM] = (pg[:, 0:HEAD_DIM] / l).astype(BF16)
                lse_ref[:, h:h + 1] = ms[g] + jnp.log(l)

    return pl.pallas_call(
        body, name="attn_fwd", grid=(n_ex, nb),
        in_specs=[pl.BlockSpec(memory_space=pltpu.SMEM),
                  pl.BlockSpec((BLOCK, N_HEADS * HEAD_DIM), lambda b, n: (b * nb + n, 0)),
                  pl.BlockSpec((BLOCK, KV_DIM), lambda b, n: (b * nb, kvb)),
                  pl.BlockSpec((BLOCK, KV_DIM), lambda b, n: (b * nb + jnp.maximum(n - 1, 0), kvb)),
                  pl.BlockSpec((BLOCK, KV_DIM), lambda b, n: (b * nb + n, kvb))],
        out_specs=[pl.BlockSpec((BLOCK, N_HEADS * HEAD_DIM), lambda b, n: (b * nb + n, 0)),
                   pl.BlockSpec((BLOCK, N_HEADS), lambda b, n: (b * nb + n, 0))],
        out_shape=[SDS((n_rows, N_HEADS * HEAD_DIM), BF16), SDS((n_rows, N_HEADS), F32)],
        scratch_shapes=[pltpu.VMEM((3 * BLOCK, N_KV * HEAD_DIM), BF16), pltpu.VMEM((3 * BLOCK, 2 * N_KV * HEAD_DIM), BF16),
                        pltpu.VMEM((N_KV, GQA * BLOCK, HEAD_DIM), BF16)],
        compiler_params=_params(2),
    )(sinks, qkv, qkv, qkv, qkv)


def _attn_bwd(qkv, sinks, o, lse, do, n_ex, nb):
    n_rows = qkv.shape[0]
    kvb = N_HEADS * HEAD_DIM // KV_DIM
    scale = HEAD_DIM ** -0.5
    nq = lambda r: nb - 1 - r

    def body(sink_ref, q_ref, kvm_ref, kvp_ref, kvc_ref, o_ref, lse_ref, do_ref, dqkv_ref, dsink_ref,
             k_s, v_s, dkv_s, carry_s, meta_s, q_s, do_s):
        b, r = pl.program_id(0), pl.program_id(1)
        n = nq(r)
        ok, distf = _attn_masks(n)

        @pl.when((b == 0) & (r == 0))
        def _():
            dsink_ref[...] = jnp.zeros_like(dsink_ref)

        @pl.when(r == 0)
        def _():
            carry_s[...] = jnp.zeros_like(carry_s)
            meta_s[...] = jnp.zeros_like(meta_s)

        for part, ref in enumerate((kvm_ref, kvp_ref, kvc_ref)):
            k_s[part * BLOCK:(part + 1) * BLOCK, :] = ref[:, 0:N_KV * HEAD_DIM]
            v_s[part * BLOCK:(part + 1) * BLOCK, :] = ref[:, N_KV * HEAD_DIM:KV_DIM]
        nt = (((1,), (1,)), ((), ()))
        tn = (((0,), (0,)), ((), ()))
        for kv in range(N_KV):
            kcols = slice(kv * HEAD_DIM, (kv + 1) * HEAD_DIM)
            vcols = slice(N_KV * HEAD_DIM + kv * HEAD_DIM, N_KV * HEAD_DIM + (kv + 1) * HEAD_DIM)
            for g in range(GQA):
                cols = slice((kv * GQA + g) * HEAD_DIM, (kv * GQA + g + 1) * HEAD_DIM)
                q_s[kv, g * BLOCK:(g + 1) * BLOCK, :] = q_ref[:, cols]
                do_s[kv, g * BLOCK:(g + 1) * BLOCK, :] = do_ref[:, cols]
            kh, vh = k_s[:, kcols], v_s[:, kcols]
            s4 = lax.dot_general(q_s[kv], kh, nt, preferred_element_type=F32) * scale
            dp4 = lax.dot_general(do_s[kv], vh, nt, preferred_element_type=F32)
            ps, dss = [], []
            for g in range(GQA):
                h = kv * GQA + g
                cols = slice(h * HEAD_DIM, (h + 1) * HEAD_DIM)
                rows = slice(g * BLOCK, (g + 1) * BLOCK)
                s = jnp.where(ok, s4[rows] - _alibi_slope(h) * distf, NEG_INF)
                lse_h = lse_ref[:, h:h + 1]
                p = jnp.exp(s - lse_h)
                delta = jnp.sum(do_ref[:, cols].astype(F32) * o_ref[:, cols].astype(F32), axis=-1, keepdims=True)
                dsink_ref[:, h:h + 1] += -jnp.exp(sink_ref[0, h] - lse_h) * delta
                ps.append(p.astype(BF16))
                dss.append((p * (dp4[rows] - delta)).astype(BF16))
            p4, ds4 = jnp.concatenate(ps, axis=0), jnp.concatenate(dss, axis=0)
            dq4 = jnp.dot(ds4, kh, preferred_element_type=F32) * scale
            for g in range(GQA):
                cols = slice((kv * GQA + g) * HEAD_DIM, (kv * GQA + g + 1) * HEAD_DIM)
                dqkv_ref[:, cols] = dq4[g * BLOCK:(g + 1) * BLOCK].astype(BF16)
            dkv_s[:, kcols] = lax.dot_general(ds4, q_s[kv], tn, preferred_element_type=F32) * scale
            dkv_s[:, vcols] = lax.dot_general(p4, do_s[kv], tn, preferred_element_type=F32)

        meta_s[...] += dkv_s[0:BLOCK, :]
        cur = dkv_s[2 * BLOCK:3 * BLOCK, :] + carry_s[...]
        carry_s[...] = dkv_s[BLOCK:2 * BLOCK, :]

        @pl.when(n > 0)
        def _():
            dqkv_ref[:, N_HEADS * HEAD_DIM:QKV_DIM] = cur.astype(BF16)

        @pl.when(n == 0)
        def _():
            dqkv_ref[:, N_HEADS * HEAD_DIM:QKV_DIM] = (cur + meta_s[...]).astype(BF16)

    blk = lambda b, r: (b * nb + nq(r), 0)
    return pl.pallas_call(
        body, name="attn_bwd", grid=(n_ex, nb),
        in_specs=[pl.BlockSpec(memory_space=pltpu.SMEM),
                  pl.BlockSpec((BLOCK, N_HEADS * HEAD_DIM), blk),
                  pl.BlockSpec((BLOCK, KV_DIM), lambda b, r: (b * nb, kvb)),
                  pl.BlockSpec((BLOCK, KV_DIM), lambda b, r: (b * nb + jnp.maximum(nq(r) - 1, 0), kvb)),
                  pl.BlockSpec((BLOCK, KV_DIM), lambda b, r: (b * nb + nq(r), kvb)),
                  pl.BlockSpec((BLOCK, N_HEADS * HEAD_DIM), blk),
                  pl.BlockSpec((BLOCK, N_HEADS), blk),
                  pl.BlockSpec((BLOCK, N_HEADS * HEAD_DIM), blk)],
        out_specs=[pl.BlockSpec((BLOCK, QKV_DIM), blk),
                   pl.BlockSpec((BLOCK, N_HEADS), lambda b, r: (0, 0))],
        out_shape=[SDS((n_rows, QKV_DIM), BF16), SDS((BLOCK, N_HEADS), F32)],
        scratch_shapes=[pltpu.VMEM((3 * BLOCK, N_KV * HEAD_DIM), BF16), pltpu.VMEM((3 * BLOCK, N_KV * HEAD_DIM), BF16),
                        pltpu.VMEM((3 * BLOCK, KV_DIM), F32), pltpu.VMEM((BLOCK, KV_DIM), F32),
                        pltpu.VMEM((BLOCK, KV_DIM), F32), pltpu.VMEM((N_KV, GQA * BLOCK, HEAD_DIM), BF16),
                        pltpu.VMEM((N_KV, GQA * BLOCK, HEAD_DIM), BF16)],
        compiler_params=_params(2),
    )(sinks, qkv, qkv, qkv, qkv, o, lse, do)


def _cmul_add(xr, xi, mr, mi, sr, si):
    return xr + mr * sr - mi * si, xi + mr * si + mi * sr


def _scan_tiles(buf, tab_ref, carry_s, n_groups, reverse):
    shifts = (7, 6, 4) if reverse else (1, 2, 4)

    def group(gi, carry):
        g = (n_groups - 1 - gi) if reverse else gi
        row = pl.multiple_of(g * 8, 8)
        out = []
        for j in range(PAIRS_PER_CHUNK):
            xr = buf[j, pl.ds(row, 8), 0:128]
            xi = buf[j, pl.ds(row, 8), 128:256]
            for lvl, sh in enumerate(shifts):
                xr, xi = _cmul_add(xr, xi, tab_ref[j, 2 * lvl], tab_ref[j, 2 * lvl + 1],
                                   pltpu.roll(xr, sh, 0), pltpu.roll(xi, sh, 0))
            xr, xi = _cmul_add(xr, xi, tab_ref[j, 6], tab_ref[j, 7], carry[2 * j], carry[2 * j + 1])
            buf[j, pl.ds(row, 8), 0:128] = xr
            buf[j, pl.ds(row, 8), 128:256] = xi
            edge = slice(0, 1) if reverse else slice(7, 8)
            out += [jnp.broadcast_to(xr[edge], (8, 128)), jnp.broadcast_to(xi[edge], (8, 128))]
        return tuple(out)

    carry0 = tuple(carry_s[k] for k in range(2 * PAIRS_PER_CHUNK))
    carry = lax.fori_loop(0, n_groups, group, carry0)
    for k in range(2 * PAIRS_PER_CHUNK):
        carry_s[k] = carry[k]


def _ssm_fwd(u, b_pad, c_pad, tab, d_skip, n_ex, lp):
    n_rows = u.shape[0]
    n_t = lp // TM
    n_chunk = D_MODEL // 128

    def body(u_ref, bp_ref, cp_ref, tab_ref, d_ref, yg_ref, y_ref, xs_ref, buf, carry_s):
        @pl.when(pl.program_id(2) == 0)
        def _():
            carry_s[...] = jnp.zeros_like(carry_s)

        ub = u_ref[...]
        u16 = ub.astype(BF16)
        for j in range(PAIRS_PER_CHUNK):
            buf[j] = jnp.dot(u16, bp_ref[j], preferred_element_type=F32)
        _scan_tiles(buf, tab_ref, carry_s, TM // 8, reverse=False)
        y = d_ref[...] * ub
        for j in range(PAIRS_PER_CHUNK):
            xb = buf[j].astype(BF16)
            xs_ref[j] = xb
            y = y + jnp.dot(xb, cp_ref[j], preferred_element_type=F32)
        y_ref[...] = y
        yg_ref[...] = _gelu(y).astype(BF16)

    rows = lambda b, q, t: (b * n_t + t, q)
    return pl.pallas_call(
        body, name="ssm_fwd", grid=(n_ex, n_chunk, n_t),
        in_specs=[pl.BlockSpec((TM, 128), rows),
                  pl.BlockSpec((PAIRS_PER_CHUNK, 128, 256), lambda b, q, t: (q, 0, 0)),
                  pl.BlockSpec((PAIRS_PER_CHUNK, 256, 128), lambda b, q, t: (q, 0, 0)),
                  pl.BlockSpec((PAIRS_PER_CHUNK, 8, 8, 128), lambda b, q, t: (q, 0, 0, 0)),
                  pl.BlockSpec((1, 128), lambda b, q, t: (0, q))],
        out_specs=[pl.BlockSpec((TM, 128), rows), pl.BlockSpec((TM, 128), rows),
                   pl.BlockSpec((PAIRS_PER_CHUNK, TM, 256), lambda b, q, t: (q, b * n_t + t, 0))],
        out_shape=[SDS((n_rows, D_MODEL), BF16), SDS((n_rows, D_MODEL), F32), SDS((N_PAIR, n_rows, 256), BF16)],
        scratch_shapes=[pltpu.VMEM((PAIRS_PER_CHUNK, TM, 256), F32), pltpu.VMEM((2 * PAIRS_PER_CHUNK, 8, 128), F32)],
        compiler_params=_params(3),
    )(u, b_pad, c_pad, tab, d_skip)


def _ssm_bwd(dyg, y, u, xs, ct_pad, bt_pad, tab_rev, d_skip, n_ex, lp):
    n_rows = u.shape[0]
    n_t = lp // TM
    n_chunk = D_MODEL // 128
    tile = lambda q, b, t: (b * n_t + (n_t - 1 - t), q)

    def body(dyg_ref, y_ref, u_ref, xs_ref, xp_ref, ct_ref, bt_ref, tab_ref, d_ref,
             du_ref, db_ref, dc_ref, da_ref, dd_ref, buf, xf, carry_s):
        b, t = pl.program_id(1), pl.program_id(2)

        @pl.when((b == 0) & (t == 0))
        def _():
            db_ref[...] = jnp.zeros_like(db_ref)
            dc_ref[...] = jnp.zeros_like(dc_ref)
            da_ref[...] = jnp.zeros_like(da_ref)
            dd_ref[...] = jnp.zeros_like(dd_ref)

        @pl.when(t == 0)
        def _():
            carry_s[...] = jnp.zeros_like(carry_s)

        ub = u_ref[...]
        dy = dyg_ref[...].astype(F32) * _gelu_grad(y_ref[...])
        dd_ref[...] += _fold8(dy * ub)
        dy16 = dy.astype(BF16)
        first_tile = t == n_t - 1
        for j in range(PAIRS_PER_CHUNK):
            buf[j] = jnp.dot(dy16, ct_ref[j], preferred_element_type=F32)
            dc_ref[j] += lax.dot_general(dy16, xs_ref[j], (((0,), (0,)), ((), ())), preferred_element_type=F32)
            xf[j, 16:16 + TM, :] = xs_ref[j].astype(F32)
            xf[j, 0:16, :] = jnp.where(first_tile, 0.0, xp_ref[j].astype(F32))
        _scan_tiles(buf, tab_ref, carry_s, TM // 8, reverse=True)
        du = d_ref[...] * dy
        u16 = ub.astype(BF16)
        for j in range(PAIRS_PER_CHUNK):
            g = buf[j]
            g16 = g.astype(BF16)
            du = du + jnp.dot(g16, bt_ref[j], preferred_element_type=F32)
            db_ref[j] += lax.dot_general(u16, g16, (((0,), (0,)), ((), ())), preferred_element_type=F32)
            xprev = pltpu.roll(xf[j], 1, 0)[16:16 + TM, :]
            gr, gi = g[:, 0:128], g[:, 128:256]
            pr, pi = xprev[:, 0:128], xprev[:, 128:256]
            da_ref[j, 0] += _fold8(gr * pr + gi * pi)
            da_ref[j, 1] += _fold8(gi * pr - gr * pi)
        du_ref[...] = du

    prev16 = lambda q, b, t: (q, jnp.maximum((b * n_t + (n_t - 1 - t)) * (TM // 16) - 1, 0), 0)
    return pl.pallas_call(
        body, name="ssm_bwd", grid=(n_chunk, n_ex, n_t),
        in_specs=[pl.BlockSpec((TM, 128), tile), pl.BlockSpec((TM, 128), tile), pl.BlockSpec((TM, 128), tile),
                  pl.BlockSpec((PAIRS_PER_CHUNK, TM, 256), lambda q, b, t: (q, b * n_t + (n_t - 1 - t), 0)),
                  pl.BlockSpec((PAIRS_PER_CHUNK, 16, 256), prev16),
                  pl.BlockSpec((PAIRS_PER_CHUNK, 128, 256), lambda q, b, t: (q, 0, 0)),
                  pl.BlockSpec((PAIRS_PER_CHUNK, 256, 128), lambda q, b, t: (q, 0, 0)),
                  pl.BlockSpec((PAIRS_PER_CHUNK, 8, 8, 128), lambda q, b, t: (q, 0, 0, 0)),
                  pl.BlockSpec((1, 128), lambda q, b, t: (0, q))],
        out_specs=[pl.BlockSpec((TM, 128), tile),
                   pl.BlockSpec((PAIRS_PER_CHUNK, 128, 256), lambda q, b, t: (q, 0, 0)),
                   pl.BlockSpec((PAIRS_PER_CHUNK, 128, 256), lambda q, b, t: (q, 0, 0)),
                   pl.BlockSpec((PAIRS_PER_CHUNK, 2, 8, 128), lambda q, b, t: (q, 0, 0, 0)),
                   pl.BlockSpec((8, 128), lambda q, b, t: (0, q))],
        out_shape=[SDS((n_rows, D_MODEL), F32), SDS((N_PAIR, 128, 256), F32), SDS((N_PAIR, 128, 256), F32),
                   SDS((N_PAIR, 2, 8, 128), F32), SDS((8, D_MODEL), F32)],
        scratch_shapes=[pltpu.VMEM((PAIRS_PER_CHUNK, TM, 256), F32), pltpu.VMEM((PAIRS_PER_CHUNK, TM + 16, 256), F32),
                        pltpu.VMEM((2 * PAIRS_PER_CHUNK, 8, 128), F32)],
        compiler_params=_params(3),
    )(dyg, y, u, xs, xs, ct_pad, bt_pad, tab_rev, d_skip)


def _rms_fwd(h, wn, name):
    n_rows = h.shape[0]

    def body(h_ref, wn_ref, o_ref):
        o_ref[...] = _rms(h_ref[...], wn_ref[...])

    return pl.pallas_call(
        body, name=name, grid=(n_rows // TM,),
        in_specs=[pl.BlockSpec((TM, D_MODEL), lambda i: (i, 0)), pl.BlockSpec((1, D_MODEL), lambda i: (0, 0))],
        out_specs=pl.BlockSpec((TM, D_MODEL), lambda i: (i, 0)),
        out_shape=SDS((n_rows, D_MODEL), F32), compiler_params=_params(1),
    )(h, wn)


def _rms_bwd_call(dhn, h, wn, dres, name):
    n_rows = h.shape[0]

    def body(dhn_ref, h_ref, wn_ref, dres_ref, o_ref, dw_ref):
        @pl.when(pl.program_id(0) == 0)
        def _():
            dw_ref[...] = jnp.zeros_like(dw_ref)

        dh, dw_rows = _rms_bwd(dhn_ref[...], h_ref[...], wn_ref[...])
        o_ref[...] = dres_ref[...] + dh
        dw_ref[...] += _fold8(dw_rows)

    row = lambda i: (i, 0)
    return pl.pallas_call(
        body, name=name, grid=(n_rows // TM,),
        in_specs=[pl.BlockSpec((TM, D_MODEL), row), pl.BlockSpec((TM, D_MODEL), row),
                  pl.BlockSpec((1, D_MODEL), lambda i: (0, 0)), pl.BlockSpec((TM, D_MODEL), row)],
        out_specs=[pl.BlockSpec((TM, D_MODEL), row), pl.BlockSpec((8, D_MODEL), lambda i: (0, 0))],
        out_shape=[SDS((n_rows, D_MODEL), F32), SDS((8, D_MODEL), F32)], compiler_params=_params(1),
    )(dhn, h, wn, dres)


def _glu_fwd(h, z):
    n_rows = h.shape[0]

    def body(h_ref, val_ref, gate_ref, o_ref):
        o_ref[...] = h_ref[...] + val_ref[...].astype(F32) * jax.nn.sigmoid(gate_ref[...].astype(F32))

    row = lambda i: (i, 0)
    return pl.pallas_call(
        body, name="glu_fwd", grid=(n_rows // TM,),
        in_specs=[pl.BlockSpec((TM, D_MODEL), row), pl.BlockSpec((TM, D_MODEL), row),
                  pl.BlockSpec((TM, D_MODEL), lambda i: (i, 1))],
        out_specs=pl.BlockSpec((TM, D_MODEL), row),
        out_shape=SDS((n_rows, D_MODEL), F32), compiler_params=_params(1),
    )(h, z, z)


def _glu_bwd(dh, z):
    n_rows = dh.shape[0]

    def body(dh_ref, val_ref, gate_ref, dz_ref):
        sg = jax.nn.sigmoid(gate_ref[...].astype(F32))
        d = dh_ref[...]
        dz_ref[:, 0:D_MODEL] = (d * sg).astype(BF16)
        dz_ref[:, D_MODEL:2 * D_MODEL] = (d * val_ref[...].astype(F32) * sg * (1.0 - sg)).astype(BF16)

    row = lambda i: (i, 0)
    return pl.pallas_call(
        body, name="glu_bwd", grid=(n_rows // TM,),
        in_specs=[pl.BlockSpec((TM, D_MODEL), row), pl.BlockSpec((TM, D_MODEL), row),
                  pl.BlockSpec((TM, D_MODEL), lambda i: (i, 1))],
        out_specs=pl.BlockSpec((TM, 2 * D_MODEL), row),
        out_shape=SDS((n_rows, 2 * D_MODEL), BF16), compiler_params=_params(1),
    )(dh, z, z)


def _loss_head(h, wn, target, n_ex, nb):
    n_rows = h.shape[0]

    def body(h_ref, wn_ref, t_ref, dh_ref, loss_ref, dw_ref):
        b, n = pl.program_id(0), pl.program_id(1)

        @pl.when((b == 0) & (n == 0))
        def _():
            loss_ref[...] = jnp.zeros_like(loss_ref)
            dw_ref[...] = jnp.zeros_like(dw_ref)

        @pl.when(n == 0)
        def _():
            dh_ref[...] = jnp.zeros_like(dh_ref)

        @pl.when(n > 0)
        def _():
            hh = h_ref[...]
            diff = _rms(hh, wn_ref[...]) - t_ref[...]
            loss_ref[...] += 0.5 * jnp.sum(diff * diff) * (1.0 / D_MODEL)
            dh, dw_rows = _rms_bwd(diff * (1.0 / D_MODEL), hh, wn_ref[...])
            dh_ref[...] = dh
            dw_ref[...] += _fold8(dw_rows)

    return pl.pallas_call(
        body, name="loss_head", grid=(n_ex, nb),
        in_specs=[pl.BlockSpec((BLOCK, D_MODEL), lambda b, n: (b * nb + n, 0)),
                  pl.BlockSpec((1, D_MODEL), lambda b, n: (0, 0)),
                  pl.BlockSpec((BLOCK, D_MODEL), lambda b, n: (b * (nb - 1) + jnp.maximum(n - 1, 0), 0))],
        out_specs=[pl.BlockSpec((BLOCK, D_MODEL), lambda b, n: (b * nb + n, 0)),
                   pl.BlockSpec((8, 128), lambda b, n: (0, 0)),
                   pl.BlockSpec((8, D_MODEL), lambda b, n: (0, 0))],
        out_shape=[SDS((n_rows, D_MODEL), F32), SDS((8, 128), F32), SDS((8, D_MODEL), F32)],
        compiler_params=_params(2),
    )(h, wn, target)


def _adamw(pieces, w, m, v, name):
    n_layers = len(pieces)
    rows, cols = pieces[0].shape[1:]
    rb = rows
    for cand in (256, 136, 128, 64, 32, 16, 8):
        if rows % cand == 0 and rows > cand:
            rb = cand
            break
    n_blk = rows // rb
    c1 = 1.0 / (1.0 - ADAM_B1 ** ADAM_STEP)
    c2 = 1.0 / (1.0 - ADAM_B2 ** ADAM_STEP)

    def body(*refs):
        p_refs = refs[:n_layers]
        w_ref, m_ref, v_ref, g_out, d_out, m_out, v_out = refs[n_layers:]
        layer = pl.program_id(0)
        g = None
        for l, p_ref in enumerate(p_refs):
            gl = p_ref[0].astype(F32)
            for k in range(1, N_DEV):
                gl = gl + p_ref[k].astype(F32)
            g = gl if g is None else jnp.where(layer == l, gl, g)
        m_new = ADAM_B1 * m_ref[...] + (1.0 - ADAM_B1) * g
        v_new = ADAM_B2 * v_ref[...] + (1.0 - ADAM_B2) * (g * g)
        g_out[...] = g
        m_out[...] = m_new
        v_out[...] = v_new
        d_out[...] = -ADAM_LR * ((m_new * c1) / (jnp.sqrt(v_new * c2) + ADAM_EPS) + ADAM_WD * w_ref[...])

    def piece_spec(l):
        return pl.BlockSpec((N_DEV, rb, cols), lambda ly, i: (0, jnp.where(ly == l, i, 0), 0))

    blk = pl.BlockSpec((rb, cols), lambda ly, i: (ly * n_blk + i, 0))
    return pl.pallas_call(
        body, name=name, grid=(n_layers, n_blk),
        in_specs=[piece_spec(l) for l in range(n_layers)] + [blk, blk, blk],
        out_specs=[blk, blk, blk, blk],
        out_shape=[SDS((n_layers * rows, cols), F32)] * 4, compiler_params=_params(2),
    )(*pieces, w, m, v)


_HBM = pl.BlockSpec(memory_space=pltpu.HBM)
_SEM = pl.BlockSpec(memory_space=pltpu.SEMAPHORE)
_EFFECT = pltpu.SideEffectType.DATAFLOW_SIDE_EFFECTING
N_GATHER_PEERS = N_CHIPS - 1
N_EXCHANGE_PEERS = N_DEV - 1


def _gather_copies(srcs, lands, send_sems, recv_sems):
    x, y, c = lax.axis_index("x"), lax.axis_index("y"), lax.axis_index("c")
    mine = 2 * x + y
    chips = [(1 - x, y), (x, 1 - y), (1 - x, 1 - y)]
    out, inc = [], []
    for a in range(len(srcs)):
        for k, (px, py) in enumerate(chips):
            j = a * N_GATHER_PEERS + k
            sems = dict(send_sem=send_sems.at[j], recv_sem=recv_sems.at[j], device_id=(px, py, c),
                        device_id_type=pl.DeviceIdType.MESH)
            out.append(pltpu.make_async_remote_copy(src_ref=srcs[a], dst_ref=lands[a].at[mine], **sems))
            inc.append(pltpu.make_async_remote_copy(src_ref=srcs[a], dst_ref=lands[a].at[2 * px + py], **sems))
    return out, inc


def _exchange_copies(n_scatter):
    def copies(srcs, lands, send_sems, recv_sems):
        x, y, c = lax.axis_index("x"), lax.axis_index("y"), lax.axis_index("c")
        me = 4 * x + 2 * y + c
        peers = [(x ^ (k >> 2), y ^ ((k >> 1) & 1), c ^ (k & 1)) for k in range(1, N_DEV)]
        out, inc = [], []
        for a in range(len(srcs)):
            for k, (px, py, pc) in enumerate(peers):
                j = a * N_EXCHANGE_PEERS + k
                sems = dict(send_sem=send_sems.at[j], recv_sem=recv_sems.at[j], device_id=(px, py, pc),
                            device_id_type=pl.DeviceIdType.MESH)
                theirs = srcs[a].at[2 * px + py] if a < n_scatter else srcs[a]
                mine = srcs[a].at[2 * x + y] if a < n_scatter else srcs[a]
                out.append(pltpu.make_async_remote_copy(src_ref=theirs, dst_ref=lands[a].at[me], **sems))
                inc.append(pltpu.make_async_remote_copy(src_ref=mine, dst_ref=lands[a].at[4 * px + 2 * py + pc], **sems))
        return out, inc

    return copies


def _split_start(groups, copies_fn, n_peers, name):
    sizes = [len(srcs) for srcs, _ in groups]
    flat = [a for srcs, lands in groups for a in list(srcs) + list(lands)]
    n_flat, n_grp = len(flat), len(groups)

    def body(*refs):
        sems = refs[2 * n_flat:2 * n_flat + 2 * n_grp]
        token = refs[-1]
        at = 0
        for gi, n in enumerate(sizes):
            out, _ = copies_fn(refs[at:at + n], refs[at + n:at + 2 * n], sems[2 * gi], sems[2 * gi + 1])
            for cp in out:
                cp.start()
            at += 2 * n
        token[...] = jnp.zeros_like(token)

    sem_shapes = []
    for n in sizes:
        sem_shapes += [pltpu.SemaphoreType.DMA((n * n_peers,)), pltpu.SemaphoreType.DMA((n * n_peers,))]
    res = pl.pallas_call(
        body, name=name,
        out_shape=(*[pltpu.HBM(a.shape, a.dtype) for a in flat], *sem_shapes, SDS((8, 128), F32)),
        in_specs=[_HBM] * n_flat,
        out_specs=(*[_HBM] * n_flat, *[_SEM] * (2 * n_grp), pl.BlockSpec(memory_space=pltpu.VMEM)),
        input_output_aliases={i: i for i in range(n_flat)},
        compiler_params=pltpu.CompilerParams(has_side_effects=_EFFECT),
    )(*[pltpu.with_memory_space_constraint(a, pltpu.HBM) for a in flat])
    handles, at = [], 0
    for gi, n in enumerate(sizes):
        handles.append((res[n_flat + 2 * gi], res[n_flat + 2 * gi + 1], list(res[at:at + n]), list(res[at + n:at + 2 * n])))
        at += 2 * n
    return handles, res[-1]


def _split_wait(handle, after, copies_fn, name):
    send_sems, recv_sems, srcs, lands = handle
    n = len(srcs)

    def body(*refs):
        out, inc = copies_fn(refs[:n], refs[n:2 * n], refs[2 * n], refs[2 * n + 1])
        for cp in out:
            cp.wait_send()
        for cp in inc:
            cp.wait_recv()

    flat = list(srcs) + list(lands)
    res = pl.pallas_call(
        body, name=name,
        out_shape=tuple(pltpu.HBM(a.shape, a.dtype) for a in flat),
        in_specs=[_HBM] * (2 * n) + [_SEM, _SEM, pl.BlockSpec(memory_space=pl.ANY)],
        out_specs=tuple([_HBM] * (2 * n)),
        input_output_aliases={i: i for i in range(2 * n)},
        compiler_params=pltpu.CompilerParams(has_side_effects=_EFFECT),
    )(*flat, send_sems, recv_sems, after)
    return list(res[n:])


def _landing(own, slot, n_slots):
    return lax.dynamic_update_index_in_dim(lax.empty((n_slots,) + own.shape, own.dtype), own, slot, 0)


def _ssm_discretize(lam_re, lam_im, log_dt, b_re, b_im):
    lr = jnp.minimum(lam_re, LAMBDA_RE_MAX)
    li = lam_im
    dt = jnp.exp(log_dt)[:, None]
    mag = jnp.exp(lr * dt)
    ar, ai = mag * jnp.cos(li * dt), mag * jnp.sin(li * dt)
    den = lr * lr + li * li
    nr, ni = ar - 1.0, ai
    gr, gi = (nr * lr + ni * li) / den, (ni * lr - nr * li) / den
    bbr = gr[:, :, None] * b_re - gi[:, :, None] * b_im
    bbi = gr[:, :, None] * b_im + gi[:, :, None] * b_re
    return ar, ai, bbr, bbi


def _pair_lanes(t):
    return t.reshape(N_PAIR, 2 * SSM_STATE)


def _chan_state_blocks(t_gcp):
    t = t_gcp.reshape(N_PAIR, 2, SSM_GROUP, SSM_STATE)
    eye2 = jnp.eye(2, dtype=t.dtype)
    blk = jnp.einsum("rgcp,gh->rgchp", t, eye2).reshape(N_PAIR, 2 * SSM_GROUP, 2 * SSM_STATE)
    place = jax.nn.one_hot(jnp.arange(N_PAIR) % PAIRS_PER_CHUNK, PAIRS_PER_CHUNK, dtype=t.dtype)
    return jnp.einsum("rcl,rj->rjcl", blk, place).reshape(N_PAIR, 128, 2 * SSM_STATE)


def _chan_state_unblock(t):
    t = t.reshape(N_PAIR, PAIRS_PER_CHUNK, 2, SSM_GROUP, 2, SSM_STATE)
    place = jax.nn.one_hot(jnp.arange(N_PAIR) % PAIRS_PER_CHUNK, PAIRS_PER_CHUNK, dtype=t.dtype)
    t = jnp.einsum("rjgchp,rj->rgchp", t, place)
    t = jnp.einsum("rgchp,gh->rgcp", t, jnp.eye(2, dtype=t.dtype))
    return t.reshape(SSM_NG, SSM_GROUP, SSM_STATE)


def _scan_tables(ar, ai, reverse):
    ar, ai = _pair_lanes(ar), _pair_lanes(ai)
    if reverse:
        ai = -ai
    pows = [(ar, ai)]
    for _ in range(7):
        pr, pi = pows[-1]
        pows.append((pr * ar - pi * ai, pr * ai + pi * ar))
    rows = jnp.arange(8)[None, :, None]
    tiles = []
    for k in (1, 2, 4):
        keep = (rows <= 7 - k) if reverse else (rows >= k)
        for part in pows[k - 1]:
            tiles.append(jnp.where(keep, part[:, None, :], 0.0))
    order = list(range(7, -1, -1)) if reverse else list(range(8))
    for comp in (0, 1):
        tiles.append(jnp.stack([pows[e][comp] for e in order], axis=1))
    return jnp.stack(tiles, axis=1)


def _local_step(x, target, w, late_weights, on_grads):
    n_ex, seq, _ = x.shape
    lp = seq + BLOCK
    nb = lp // BLOCK
    n_rows = n_ex * lp
    g = {}

    head = jnp.concatenate([jnp.zeros((PAD, D_MODEL), F32), w["meta_tokens"]], axis=0)
    h0 = jnp.concatenate([jnp.broadcast_to(head[None], (n_ex, BLOCK, D_MODEL)), x], axis=1).reshape(n_rows, D_MODEL)

    qkv, hn_a = _rms_mm_cols(h0, w["attn_norm_w"], w["attn_w_qkv"], 0, "qkv_fwd")
    att, lse = _attn_fwd(qkv, w["attn_sinks"], n_ex, nb)
    h1 = _mm_acc(att, w["attn_w_o"], 0, False, "attn_out_fwd", res=h0)
    w = {**w, **late_weights(att)}
    h2, a0, hn_m0 = _mlp_fwd(h1, w["mlp_norm_w"][0:1], w["mlp_w_up"], w["mlp_w_down"], 0, "mlp0_fwd")

    ar, ai, bbr, bbi = _ssm_discretize(w["ssm_lambda_re"], w["ssm_lambda_im"], w["ssm_log_dt"], w["ssm_b_re"], w["ssm_b_im"])
    b_blk = jnp.concatenate([_chan_state_blocks(jnp.swapaxes(bbr, 1, 2)), _chan_state_blocks(jnp.swapaxes(bbi, 1, 2))], axis=2)
    c_blk = jnp.concatenate([_chan_state_blocks(w["ssm_c_re"]), -_chan_state_blocks(w["ssm_c_im"])], axis=2)
    b_pad, bt_pad = b_blk.astype(BF16), jnp.swapaxes(b_blk, 1, 2).astype(BF16)
    ct_pad, c_pad = c_blk.astype(BF16), jnp.swapaxes(c_blk, 1, 2).astype(BF16)
    u = _rms_fwd(h2, w["ssm_norm_w"], "ssm_norm_fwd")
    yg, y, xs = _ssm_fwd(u, b_pad, c_pad, _scan_tables(ar, ai, False), w["ssm_d"], n_ex, lp)
    z = _mm_cols(yg, w["ssm_w_glu"], 0, False, "glu_mm_fwd")
    h3 = _glu_fwd(h2, z)
    h4, a1, hn_m1 = _mlp_fwd(h3, w["mlp_norm_w"][1:2], w["mlp_w_up"], w["mlp_w_down"], 1, "mlp1_fwd")

    dh4, loss_tile, dnorm_f = _loss_head(h4, w["final_norm_w"], target.reshape(n_ex * seq, D_MODEL), n_ex, nb)

    def mlp_bwd(dh_out, h_in, a, hn, layer, tag, norm_w):
        da = _mm_cols(dh_out, w["mlp_w_down"], layer, True, tag + "_dact", mul2relu=a)
        dh_in, dnorm = _mm_acc(da, w["mlp_w_up"], layer, True, tag + "_dx", rms_bwd=(h_in, norm_w, dh_out))
        dw_down = _mm_tn(a, dh_out, N_CHIPS, True, tag + "_dwdown", relu2_a=True)
        dw_up = _mm_tn(hn, da, N_CHIPS, False, tag + "_dwup")
        return dh_in, dnorm, dw_up, dw_down

    dh3, dnorm_m1, dwu1, dwd1 = mlp_bwd(dh4, h3, a1, hn_m1, 1, "mlp1", w["mlp_norm_w"][1:2])
    tok = on_grads("mlp1", {"mlp_w_up": dwu1, "mlp_w_down": dwd1})
    dz = _glu_bwd(dh3, z)
    dyg = _mm_acc(dz, w["ssm_w_glu"], 0, True, "glu_mm_dx", out_dtype=BF16)
    g["ssm_w_glu"] = _mm_tn(yg, dz, N_CHIPS, False, "glu_mm_dw")
    du, db_blk, dc_blk, da_t, dd_t = _ssm_bwd(dyg, y, u, xs, ct_pad, bt_pad, _scan_tables(ar, ai, True),
                                              w["ssm_d"] + tok, n_ex, lp)
    dh2, dnorm_s = _rms_bwd_call(du, h2, w["ssm_norm_w"], dh3, "ssm_norm_bwd")
    g["ssm_c_re"] = _chan_state_unblock(dc_blk[:, :, 0:128])
    g["ssm_c_im"] = -_chan_state_unblock(dc_blk[:, :, 128:256])
    g_bbr = jnp.swapaxes(_chan_state_unblock(db_blk[:, :, 0:128]), 1, 2)
    g_bbi = jnp.swapaxes(_chan_state_unblock(db_blk[:, :, 128:256]), 1, 2)
    g_a = jnp.sum(da_t, axis=2).reshape(N_PAIR, 2, 2, SSM_STATE)
    g_ar, g_ai = g_a[:, 0].reshape(SSM_NG, SSM_STATE), g_a[:, 1].reshape(SSM_NG, SSM_STATE)
    _, vjp = jax.vjp(_ssm_discretize, w["ssm_lambda_re"], w["ssm_lambda_im"], w["ssm_log_dt"], w["ssm_b_re"], w["ssm_b_im"])
    g["ssm_lambda_re"], g["ssm_lambda_im"], g["ssm_log_dt"], g["ssm_b_re"], g["ssm_b_im"] = vjp((g_ar, g_ai, g_bbr, g_bbi))
    tok = on_grads("ssm", g)
    g = {}
    dh1, dnorm_m0, dwu0, dwd0 = mlp_bwd(dh2, h1, a0, hn_m0, 0, "mlp0", w["mlp_norm_w"][0:1] + tok)
    datt = _mm_cols(dh1, w["attn_w_o"], 0, True, "attn_out_dx")
    dw_o = _mm_tn(att, dh1, N_CHIPS, True, "attn_out_dw")
    tok = on_grads("mlp0", {"mlp_w_up": dwu0, "mlp_w_down": dwd0, "attn_w_o": dw_o})
    dqkv, dsink_rows = _attn_bwd(qkv, w["attn_sinks"] + tok, att, lse, datt, n_ex, nb)
    tok = on_grads("qkv", {"attn_w_qkv": _mm_tn(hn_a, dqkv, N_CHIPS, False, "qkv_dw")})
    dh0, dnorm_a = _mm_acc(dqkv, w["attn_w_qkv"], 0, True, "qkv_dx", rms_bwd=(h0, w["attn_norm_w"] + tok, dh1))

    dh0 = dh0.reshape(n_ex, lp, D_MODEL)
    on_grads("rest", {
        "mlp_norm_w": jnp.stack([jnp.sum(dnorm_m0, axis=0), jnp.sum(dnorm_m1, axis=0)]),
        "final_norm_w": jnp.sum(dnorm_f, axis=0),
        "attn_norm_w": jnp.sum(dnorm_a, axis=0)[None],
        "ssm_norm_w": jnp.sum(dnorm_s, axis=0)[None],
        "attn_sinks": jnp.sum(dsink_rows, axis=0)[None],
        "ssm_d": jnp.sum(dd_t, axis=0)[None],
        "meta_tokens": jnp.sum(dh0[:, PAD:BLOCK], axis=0)})
    return loss_tile, dh0[:, BLOCK:]


_SHARDED_SMALL = ("meta_tokens", "ssm_norm_w", "ssm_d")
_REP_SSM = ("ssm_lambda_re", "ssm_lambda_im", "ssm_log_dt", "ssm_b_re", "ssm_b_im", "ssm_c_re", "ssm_c_im")
_REP_MISC = ("attn_norm_w", "attn_sinks", "mlp_norm_w", "final_norm_w")
_BIG = ("attn_w_qkv", "attn_w_o", "ssm_w_glu", "mlp_w_up", "mlp_w_down")


def _pack(parts, cols):
    flat = jnp.concatenate([p.reshape(-1) for p in parts])
    rows = -(-flat.shape[0] // (8 * cols)) * 8
    return jnp.pad(flat, (0, rows * cols - flat.shape[0])).reshape(rows, cols)


def _unpack(packed, like):
    flat = packed.reshape(-1)
    out, at = [], 0
    for p in like:
        out.append(flat[at:at + p.size].reshape(p.shape))
        at += p.size
    return out


def kernel(x, meta_tokens, attn_norm_w, attn_w_qkv, attn_sinks, attn_w_o, ssm_norm_w, ssm_lambda_re, ssm_lambda_im, ssm_log_dt, ssm_b_re, ssm_b_im, ssm_c_re, ssm_c_im, ssm_d, ssm_w_glu, mlp_norm_w, mlp_w_up, mlp_w_down, final_norm_w, loss_target, m_meta_tokens, m_attn_norm_w, m_attn_w_qkv, m_attn_sinks, m_attn_w_o, m_ssm_norm_w, m_ssm_lambda_re, m_ssm_lambda_im, m_ssm_log_dt, m_ssm_b_re, m_ssm_b_im, m_ssm_c_re, m_ssm_c_im, m_ssm_d, m_ssm_w_glu, m_mlp_norm_w, m_mlp_w_up, m_mlp_w_down, m_final_norm_w, v_meta_tokens, v_attn_norm_w, v_attn_w_qkv, v_attn_sinks, v_attn_w_o, v_ssm_norm_w, v_ssm_lambda_re, v_ssm_lambda_im, v_ssm_log_dt, v_ssm_b_re, v_ssm_b_im, v_ssm_c_re, v_ssm_c_im, v_ssm_d, v_ssm_w_glu, v_mlp_norm_w, v_mlp_w_up, v_mlp_w_down, v_final_norm_w):
    names = ("meta_tokens", "attn_norm_w", "attn_w_qkv", "attn_sinks", "attn_w_o", "ssm_norm_w", "ssm_lambda_re",
             "ssm_lambda_im", "ssm_log_dt", "ssm_b_re", "ssm_b_im", "ssm_c_re", "ssm_c_im", "ssm_d", "ssm_w_glu",
             "mlp_norm_w", "mlp_w_up", "mlp_w_down", "final_norm_w")
    wts = dict(zip(names, (meta_tokens, attn_norm_w, attn_w_qkv, attn_sinks, attn_w_o, ssm_norm_w, ssm_lambda_re,
                           ssm_lambda_im, ssm_log_dt, ssm_b_re, ssm_b_im, ssm_c_re, ssm_c_im, ssm_d, ssm_w_glu,
                           mlp_norm_w, mlp_w_up, mlp_w_down, final_norm_w)))
    mom = dict(zip(names, (m_meta_tokens, m_attn_norm_w, m_attn_w_qkv, m_attn_sinks, m_attn_w_o, m_ssm_norm_w,
                           m_ssm_lambda_re, m_ssm_lambda_im, m_ssm_log_dt, m_ssm_b_re, m_ssm_b_im, m_ssm_c_re,
                           m_ssm_c_im, m_ssm_d, m_ssm_w_glu, m_mlp_norm_w, m_mlp_w_up, m_mlp_w_down, m_final_norm_w)))
    var = dict(zip(names, (v_meta_tokens, v_attn_norm_w, v_attn_w_qkv, v_attn_sinks, v_attn_w_o, v_ssm_norm_w,
                           v_ssm_lambda_re, v_ssm_lambda_im, v_ssm_log_dt, v_ssm_b_re, v_ssm_b_im, v_ssm_c_re,
                           v_ssm_c_im, v_ssm_d, v_ssm_w_glu, v_mlp_norm_w, v_mlp_w_up, v_mlp_w_down, v_final_norm_w)))

    my_chip = 2 * lax.axis_index("x") + lax.axis_index("y")
    my_dev = 2 * my_chip + lax.axis_index("c")
    small_mine = _pack([wts[n] for n in _SHARDED_SMALL], 128)
    first = [attn_w_qkv.astype(BF16), attn_w_o.astype(BF16), small_mine]
    later = [ssm_w_glu.astype(BF16), mlp_w_up.astype(BF16), mlp_w_down.astype(BF16)]
    handles, _ = _split_start([(srcs, [_landing(a, my_chip, N_CHIPS) for a in srcs]) for srcs in (first, later)],
                              _gather_copies, N_GATHER_PEERS, "gather_start")
    got = _split_wait(handles[0], small_mine, _gather_copies, "gather_wait_first")
    full = {n: wts[n] for n in _REP_MISC}
    full["final_norm_w"] = final_norm_w[None]
    for n in _REP_SSM:
        full[n] = wts[n][0]
    full["attn_w_qkv"], full["attn_w_o"] = got[0], got[1]
    smalls = [_unpack(got[2][s], [wts[n] for n in _SHARDED_SMALL]) for s in range(N_CHIPS)]
    for k, n in enumerate(_SHARDED_SMALL):
        full[n] = jnp.concatenate([smalls[s][k] for s in range(N_CHIPS)], axis=1)

    def late_weights(after):
        glu, up, down = _split_wait(handles[1], after, _gather_copies, "gather_wait_later")
        return {"ssm_w_glu": glu, "mlp_w_up": up, "mlp_w_down": down}

    def shard_cols(t):
        return jnp.swapaxes(t.reshape(t.shape[0], N_CHIPS, t.shape[1] // N_CHIPS), 0, 1)

    pending = {}

    def on_grads(tag, g):
        scatter = [g[n] for n in _BIG if n in g]
        whole = []
        if tag == "ssm":
            whole = [_pack([g[n] for n in _REP_SSM], D_MODEL)]
        if tag == "rest":
            parts = [shard_cols(g[n]) for n in _SHARDED_SMALL]
            scatter = [jnp.stack([_pack([p[s] for p in parts], 128) for s in range(N_CHIPS)])]
            whole = [_pack([g[n] for n in _REP_MISC], D_MODEL)]
        srcs = scatter + whole
        lands = [_landing(lax.dynamic_index_in_dim(a, my_chip, 0, keepdims=False), my_dev, N_DEV) for a in scatter]
        lands += [_landing(a, my_dev, N_DEV) for a in whole]
        hs, token = _split_start([(srcs, lands)], _exchange_copies(len(scatter)), N_EXCHANGE_PEERS, "exchange_start_" + tag)
        pending[tag] = (hs[0], len(scatter))
        return token[0, 0]

    loss_tile, grad_x = _local_step(x, loss_target, full, late_weights, on_grads)
    loss = lax.psum(loss_tile[0, 0], ("x", "y", "c"))

    recv = {}
    for tag, (handle, n_scatter) in pending.items():
        recv[tag] = _split_wait(handle, grad_x, _exchange_copies(n_scatter), "exchange_wait_" + tag)

    out = {}

    def update(tag, pieces, w2, m2, v2):
        return _adamw(pieces, w2, m2, v2, "adamw_" + tag)

    def update_weight(n, pieces):
        shp = wts[n].shape
        r2 = (math.prod(shp[:-1]), shp[-1])
        res = update(n, pieces, wts[n].reshape(r2), mom[n].reshape(r2), var[n].reshape(r2))
        out[n] = [t.reshape(shp) for t in res]

    update_weight("mlp_w_up", [recv["mlp0"][1], recv["mlp1"][0]])
    update_weight("mlp_w_down", [recv["mlp0"][2], recv["mlp1"][1]])
    update_weight("attn_w_o", [recv["mlp0"][0]])
    update_weight("ssm_w_glu", [recv["ssm"][0]])
    update_weight("attn_w_qkv", [recv["qkv"][0]])
    for tag, group, pieces, cols in (("small", _SHARDED_SMALL, recv["rest"][0], 128),
                                     ("rep_ssm", _REP_SSM, recv["ssm"][1], D_MODEL),
                                     ("rep_misc", _REP_MISC, recv["rest"][1], D_MODEL)):
        like = [wts[n] for n in group]
        res = update(tag, [pieces], _pack(like, cols), _pack([mom[n] for n in group], cols),
                     _pack([var[n] for n in group], cols))
        for k, n in enumerate(group):
            out[n] = [_unpack(t, like)[k] for t in res]

    return (loss, grad_x, *[out[n][0] for n in names], *[out[n][1] for n in names],
            *[out[n][2] for n in names], *[out[n][3] for n in names])
```

```python
import functools
import math

import jax
import jax.numpy as jnp
from jax import lax
from jax.experimental import pallas as pl
from jax.experimental.pallas import tpu as pltpu

F32 = jnp.float32
BF16 = jnp.bfloat16
SDS = jax.ShapeDtypeStruct

D_MODEL = 1024
N_HEADS = 16
N_KV = 4
GQA = N_HEADS // N_KV
HEAD_DIM = 64
BLOCK = 128
N_META = 16
PAD = BLOCK - N_META
QKV_DIM = (N_HEADS + 2 * N_KV) * HEAD_DIM
KV_DIM = 2 * N_KV * HEAD_DIM
D_FF = 4 * D_MODEL
N_CHIPS = 4
N_DEV = 8
SSM_GROUP = 16
SSM_NG = D_MODEL // SSM_GROUP
SSM_STATE = 64
N_PAIR = SSM_NG // 2
PAIRS_PER_CHUNK = 4
RMS_EPS = 1e-6
NEG_INF = -1e30
LAMBDA_RE_MAX = -1e-4
ADAM_LR, ADAM_B1, ADAM_B2, ADAM_EPS, ADAM_WD, ADAM_STEP = 0.001, 0.9, 0.999, 1e-08, 0.01, 10

TM = 384
MM_TILES = (768, 384)
MLP_BWD_TILES = (768, 384)
TN_TILES = (1408, 768, 384)
VMEM_LIMIT = 56 * 1024 * 1024


def _params(n_grid):
    return pltpu.CompilerParams(dimension_semantics=("arbitrary",) * n_grid, vmem_limit_bytes=VMEM_LIMIT)


def _row_tile(n_rows, tiles):
    return next(t for t in tiles if n_rows % t == 0)


def _rms(h, w):
    r = lax.rsqrt(jnp.mean(h * h, axis=-1, keepdims=True) + RMS_EPS)
    return h * r * w


def _rms_bwd(dhn, h, w):
    r = lax.rsqrt(jnp.mean(h * h, axis=-1, keepdims=True) + RMS_EPS)
    g = dhn * w
    proj = jnp.sum(g * h, axis=-1, keepdims=True) * (1.0 / D_MODEL)
    return r * g - h * (r * r * r) * proj, dhn * h * r


def _fold8(t):
    return jnp.sum(t.reshape(t.shape[0] // 8, 8, t.shape[1]), axis=0)


def _gelu(y):
    return 0.5 * y * (1.0 + jnp.tanh(0.7978845608028654 * (y + 0.044715 * y * y * y)))


def _gelu_grad(y):
    t = jnp.tanh(0.7978845608028654 * (y + 0.044715 * y * y * y))
    return 0.5 * (1.0 + t) + 0.5 * y * (1.0 - t * t) * 0.7978845608028654 * (1.0 + 3.0 * 0.044715 * y * y)


def _rms_mm_cols(h, wn, w4, layer, name):
    n_rows = h.shape[0]
    n_sh, _, k, n = w4.shape
    TM = _row_tile(n_rows, MM_TILES)

    def body(h_ref, wn_ref, w_ref, o_ref, hn_ref, hn_s):
        @pl.when(pl.program_id(1) == 0)
        def _():
            hn = _rms(h_ref[...], wn_ref[...]).astype(BF16)
            hn_s[...] = hn
            hn_ref[...] = hn

        o_ref[...] = jnp.dot(hn_s[...], w_ref[...], preferred_element_type=F32).astype(o_ref.dtype)

    return pl.pallas_call(
        body, name=name, grid=(n_rows // TM, n_sh),
        in_specs=[pl.BlockSpec((TM, k), lambda i, s: (i, 0)),
                  pl.BlockSpec((1, k), lambda i, s: (0, 0)),
                  pl.BlockSpec((None, None, k, n), lambda i, s: (s, layer, 0, 0))],
        out_specs=[pl.BlockSpec((TM, n), lambda i, s: (i, s)),
                   pl.BlockSpec((TM, k), lambda i, s: (i, 0))],
        out_shape=[SDS((n_rows, n_sh * n), BF16), SDS((n_rows, k), BF16)],
        scratch_shapes=[pltpu.VMEM((TM, k), BF16)],
        compiler_params=_params(2),
    )(h, wn, w4)


def _mm_cols(x, w4, layer, trans_w, name, mul2relu=None):
    n_rows, kx = x.shape
    TM = _row_tile(n_rows, MM_TILES)
    n_sh, _, k, n = w4.shape
    n_out = k if trans_w else n
    dims = (((1,), (1,)), ((), ())) if trans_w else (((1,), (0,)), ((), ()))

    def body(*refs):
        if mul2relu is None:
            x_ref, w_ref, o_ref = refs
        else:
            x_ref, w_ref, a_ref, o_ref = refs
        acc = lax.dot_general(x_ref[...].astype(BF16), w_ref[...], dims, preferred_element_type=F32)
        if mul2relu is not None:
            acc = acc * (2.0 * jnp.maximum(a_ref[...].astype(F32), 0.0))
        o_ref[...] = acc.astype(o_ref.dtype)

    in_specs = [pl.BlockSpec((TM, kx), lambda i, s: (i, 0)),
                pl.BlockSpec((None, None, k, n), lambda i, s: (s, layer, 0, 0))]
    args = [x, w4]
    if mul2relu is not None:
        in_specs.append(pl.BlockSpec((TM, n_out), lambda i, s: (i, s)))
        args.append(mul2relu)
    return pl.pallas_call(
        body, name=name, grid=(n_rows // TM, n_sh),
        in_specs=in_specs,
        out_specs=pl.BlockSpec((TM, n_out), lambda i, s: (i, s)),
        out_shape=SDS((n_rows, n_sh * n_out), BF16),
        compiler_params=_params(2),
    )(*args)


def _mm_acc(x, w4, layer, trans_w, name, res=None, rms_bwd=None, out_dtype=F32):
    n_rows = x.shape[0]
    TM = _row_tile(n_rows, MM_TILES)
    n_sh, _, k, n = w4.shape
    kx, n_out = (n, k) if trans_w else (k, n)
    dims = (((1,), (1,)), ((), ())) if trans_w else (((1,), (0,)), ((), ()))
    n_i = n_rows // TM

    def body(*refs):
        i, s = pl.program_id(0), pl.program_id(1)
        if rms_bwd is not None:
            x_ref, w_ref, h_ref, wn_ref, dres_ref, o_ref, dw_ref, acc = refs
        elif res is not None:
            x_ref, w_ref, res_ref, o_ref, acc = refs
        else:
            x_ref, w_ref, o_ref, acc = refs

        @pl.when(s == 0)
        def _():
            acc[...] = jnp.zeros_like(acc)

        acc[...] += lax.dot_general(x_ref[...].astype(BF16), w_ref[...], dims, preferred_element_type=F32)

        @pl.when(s == n_sh - 1)
        def _():
            if rms_bwd is not None:
                dh, dw_rows = _rms_bwd(acc[...], h_ref[...], wn_ref[...])
                o_ref[...] = (dres_ref[...] + dh).astype(o_ref.dtype)

                @pl.when(i == 0)
                def _():
                    dw_ref[...] = jnp.zeros_like(dw_ref)

                dw_ref[...] += _fold8(dw_rows)
            elif res is not None:
                o_ref[...] = (res_ref[...] + acc[...]).astype(o_ref.dtype)
            else:
                o_ref[...] = acc[...].astype(o_ref.dtype)

    row = lambda i, s: (i, 0)
    in_specs = [pl.BlockSpec((TM, kx), lambda i, s: (i, s)),
                pl.BlockSpec((None, None, k, n), lambda i, s: (s, layer, 0, 0))]
    args = [x, w4]
    out_specs = pl.BlockSpec((TM, n_out), row)
    out_shape = SDS((n_rows, n_out), out_dtype)
    if rms_bwd is not None:
        h, wn, dres = rms_bwd
        in_specs += [pl.BlockSpec((TM, n_out), row), pl.BlockSpec((1, n_out), lambda i, s: (0, 0)),
                     pl.BlockSpec((TM, n_out), row)]
        args += [h, wn, dres]
        out_specs = [out_specs, pl.BlockSpec((8, n_out), lambda i, s: (0, 0))]
        out_shape = [out_shape, SDS((8, n_out), F32)]
    elif res is not None:
        in_specs.append(pl.BlockSpec((TM, n_out), row))
        args.append(res)
    return pl.pallas_call(
        body, name=name, grid=(n_i, n_sh), in_specs=in_specs, out_specs=out_specs, out_shape=out_shape,
        scratch_shapes=[pltpu.VMEM((TM, n_out), F32)], compiler_params=_params(2),
    )(*args)


def _mm_tn(a, b, n_sh, a_sharded, name, relu2_a=False):
    n_rows = a.shape[0]
    TM = _row_tile(n_rows, TN_TILES)
    ka = a.shape[1] // n_sh if a_sharded else a.shape[1]
    nb = b.shape[1] if a_sharded else b.shape[1] // n_sh
    n_i = n_rows // TM

    def body(a_ref, b_ref, o_ref, acc):
        i = pl.program_id(1)

        @pl.when(i == 0)
        def _():
            acc[...] = jnp.zeros_like(acc)

        at = a_ref[...]
        if relu2_a:
            at = jnp.maximum(at.astype(F32), 0.0)
            at = at * at
        acc[...] += lax.dot_general(at.astype(BF16), b_ref[...].astype(BF16), (((0,), (0,)), ((), ())),
                                    preferred_element_type=F32)

        @pl.when(i == n_i - 1)
        def _():
            o_ref[...] = acc[...].astype(o_ref.dtype)

    a_spec = pl.BlockSpec((TM, ka), (lambda s, i: (i, s)) if a_sharded else (lambda s, i: (i, 0)))
    b_spec = pl.BlockSpec((TM, nb), (lambda s, i: (i, 0)) if a_sharded else (lambda s, i: (i, s)))
    return pl.pallas_call(
        body, name=name, grid=(n_sh, n_i), in_specs=[a_spec, b_spec],
        out_specs=pl.BlockSpec((None, ka, nb), lambda s, i: (s, 0, 0)),
        out_shape=SDS((n_sh, ka, nb), BF16),
        scratch_shapes=[pltpu.VMEM((ka, nb), F32)], compiler_params=_params(2),
    )(a, b)


def _mlp_fwd(h, wn, w_up4, w_down4, layer, name):
    n_rows = h.shape[0]
    TM = _row_tile(n_rows, MM_TILES)
    n_sh = w_up4.shape[0]
    f_sh = D_FF // n_sh

    def body(h_ref, wn_ref, wu_ref, wd_ref, o_ref, a_ref, hn_ref, hn_s, acc):
        s = pl.program_id(1)

        @pl.when(s == 0)
        def _():
            hn = _rms(h_ref[...], wn_ref[...]).astype(BF16)
            hn_s[...] = hn
            hn_ref[...] = hn
            acc[...] = jnp.zeros_like(acc)

        a = jnp.dot(hn_s[...], wu_ref[...], preferred_element_type=F32)
        a_ref[...] = a.astype(BF16)
        act = jnp.maximum(a, 0.0)
        acc[...] += jnp.dot((act * act).astype(BF16), wd_ref[...], preferred_element_type=F32)

        @pl.when(s == n_sh - 1)
        def _():
            o_ref[...] = h_ref[...] + acc[...]

    row = lambda i, s: (i, 0)
    return pl.pallas_call(
        body, name=name, grid=(n_rows // TM, n_sh),
        in_specs=[pl.BlockSpec((TM, D_MODEL), row), pl.BlockSpec((1, D_MODEL), lambda i, s: (0, 0)),
                  pl.BlockSpec((None, None, D_MODEL, f_sh), lambda i, s: (s, layer, 0, 0)),
                  pl.BlockSpec((None, None, f_sh, D_MODEL), lambda i, s: (s, layer, 0, 0))],
        out_specs=[pl.BlockSpec((TM, D_MODEL), row), pl.BlockSpec((TM, f_sh), lambda i, s: (i, s)),
                   pl.BlockSpec((TM, D_MODEL), row)],
        out_shape=[SDS((n_rows, D_MODEL), F32), SDS((n_rows, D_FF), BF16), SDS((n_rows, D_MODEL), BF16)],
        scratch_shapes=[pltpu.VMEM((TM, D_MODEL), BF16), pltpu.VMEM((TM, D_MODEL), F32)],
        compiler_params=_params(2),
    )(h, wn, w_up4, w_down4)


def _mlp_bwd_shard(s, dh, a, hn, dhn_prev, h, wn, w_up4, w_down4, dw_up_buf, dw_down_buf, name):
    n_rows = dh.shape[0]
    n_sh = w_up4.shape[0]
    f_sh = D_FF // n_sh
    tm = _row_tile(n_rows, MLP_BWD_TILES)
    n_i = n_rows // tm
    last = h is not None
    nt = (((1,), (1,)), ((), ()))
    tn = (((0,), (0,)), ((), ()))

    def body(*refs):
        refs = list(refs)
        dh_ref, a_ref, hn_ref, wu_ref, wd_ref = refs[:5]
        at = 5
        prev_ref = None
        if dhn_prev is not None:
            prev_ref = refs[at]
            at += 1
        if last:
            h_ref, wn_ref = refs[at:at + 2]
            at += 2
        if dw_up_buf is not None:
            at += 2
        o_ref, dwu_ref, dwd_ref = refs[at:at + 3]
        at += 3
        if last:
            dnorm_ref = refs[at]
            at += 1
        acc_u, acc_d = refs[at:at + 2]
        i = pl.program_id(0)

        @pl.when(i == 0)
        def _():
            acc_u[...] = jnp.zeros_like(acc_u)
            acc_d[...] = jnp.zeros_like(acc_d)
            if last:
                dnorm_ref[...] = jnp.zeros_like(dnorm_ref)

        dh16 = dh_ref[...].astype(BF16)
        r = jnp.maximum(a_ref[...].astype(F32), 0.0)
        dact = lax.dot_general(dh16, wd_ref[...], nt, preferred_element_type=F32)
        da16 = (dact * (2.0 * r)).astype(BF16)
        acc_d[...] += lax.dot_general((r * r).astype(BF16), dh16, tn, preferred_element_type=F32)
        acc_u[...] += lax.dot_general(hn_ref[...], da16, tn, preferred_element_type=F32)
        dhn = lax.dot_general(da16, wu_ref[...], nt, preferred_element_type=F32)
        if prev_ref is not None:
            dhn = dhn + prev_ref[...]
        if last:
            d_rms, dw_rows = _rms_bwd(dhn, h_ref[...], wn_ref[...])
            o_ref[...] = dh_ref[...] + d_rms
            dnorm_ref[...] += _fold8(dw_rows)
        else:
            o_ref[...] = dhn

        @pl.when(i == n_i - 1)
        def _():
            dwu_ref[...] = acc_u[...].astype(BF16)
            dwd_ref[...] = acc_d[...].astype(BF16)

    row = lambda i: (i, 0)
    tile = pl.BlockSpec((tm, D_MODEL), row)
    in_specs = [tile, pl.BlockSpec((tm, f_sh), lambda i: (i, s)), tile,
                pl.BlockSpec((None, None, D_MODEL, f_sh), lambda i: (s, 0, 0, 0)),
                pl.BlockSpec((None, None, f_sh, D_MODEL), lambda i: (s, 0, 0, 0))]
    args = [dh, a, hn, w_up4, w_down4]
    if dhn_prev is not None:
        in_specs.append(tile)
        args.append(dhn_prev)
    if last:
        in_specs += [tile, pl.BlockSpec((1, D_MODEL), lambda i: (0, 0))]
        args += [h, wn]
    aliases = {}
    if dw_up_buf is not None:
        aliases = {len(args): 1, len(args) + 1: 2}
        in_specs += [pl.BlockSpec(memory_space=pl.ANY)] * 2
        args += [dw_up_buf, dw_down_buf]
    out_specs = [tile, pl.BlockSpec((None, D_MODEL, f_sh), lambda i: (s, 0, 0)),
                 pl.BlockSpec((None, f_sh, D_MODEL), lambda i: (s, 0, 0))]
    out_shape = [SDS((n_rows, D_MODEL), F32), SDS((n_sh, D_MODEL, f_sh), BF16), SDS((n_sh, f_sh, D_MODEL), BF16)]
    if last:
        out_specs.append(pl.BlockSpec((8, D_MODEL), lambda i: (0, 0)))
        out_shape.append(SDS((8, D_MODEL), F32))
    return pl.pallas_call(
        body, name=name, grid=(n_i,), in_specs=in_specs, out_specs=out_specs, out_shape=out_shape,
        input_output_aliases=aliases,
        scratch_shapes=[pltpu.VMEM((D_MODEL, f_sh), F32), pltpu.VMEM((f_sh, D_MODEL), F32)],
        compiler_params=_params(1),
    )(*args)


def _attn_masks(n):
    qi = lax.broadcasted_iota(jnp.int32, (BLOCK, 3 * BLOCK), 0)
    col = lax.broadcasted_iota(jnp.int32, (BLOCK, 3 * BLOCK), 1)
    kj = col - BLOCK
    dist = BLOCK + qi - kj
    kmin = jnp.where(n == 0, 2 * BLOCK, jnp.where(n == 1, BLOCK, 0))
    band_ok = (col >= BLOCK) & (dist >= 0) & (dist < BLOCK) & (kj >= kmin)
    q_pos = n * BLOCK + qi - PAD
    meta_ok = (col >= PAD) & (col < BLOCK) & (col - PAD <= q_pos)
    distf = jnp.where(col >= BLOCK, dist, 0).astype(F32)
    return band_ok | meta_ok, distf


def _alibi_slope(h):
    return float(2.0 ** (-8.0 * (h + 1) / N_HEADS))


def _attn_fwd(qkv, sinks, n_ex, nb):
    n_rows = qkv.shape[0]
    kvb = N_HEADS * HEAD_DIM // KV_DIM

    def body(sink_ref, q_ref, kvm_ref, kvp_ref, kvc_ref, o_ref, lse_ref, k_s, v_s, q_s):
        n = pl.program_id(1)
        ok, distf = _attn_masks(n)
        v_s[...] = jnp.ones_like(v_s)
        for part, ref in enumerate((kvm_ref, kvp_ref, kvc_ref)):
            rows = slice(part * BLOCK, (part + 1) * BLOCK)
            k_s[rows, :] = ref[:, 0:N_KV * HEAD_DIM]
            for kv in range(N_KV):
                v_s[rows, kv * 2 * HEAD_DIM:kv * 2 * HEAD_DIM + HEAD_DIM] = \
                    ref[:, (N_KV + kv) * HEAD_DIM:(N_KV + kv + 1) * HEAD_DIM]
        for kv in range(N_KV):
            for g in range(GQA):
                h = kv * GQA + g
                q_s[kv, g * BLOCK:(g + 1) * BLOCK, :] = q_ref[:, h * HEAD_DIM:(h + 1) * HEAD_DIM]
            s4 = lax.dot_general(q_s[kv], k_s[:, kv * HEAD_DIM:(kv + 1) * HEAD_DIM], (((1,), (1,)), ((), ())),
                                 preferred_element_type=F32) * (HEAD_DIM ** -0.5)
            es, ms, sink_es = [], [], []
            for g in range(GQA):
                h = kv * GQA + g
                s = jnp.where(ok, s4[g * BLOCK:(g + 1) * BLOCK] - _alibi_slope(h) * distf, NEG_INF)
                sink = sink_ref[0, h]
                m = jnp.maximum(jnp.max(s, axis=-1, keepdims=True), sink)
                es.append(jnp.exp(s - m).astype(BF16))
                ms.append(m)
                sink_es.append(jnp.exp(sink - m))
            pv = jnp.dot(jnp.concatenate(es, axis=0), v_s[:, kv * 2 * HEAD_DIM:(kv + 1) * 2 * HEAD_DIM],
                         preferred_element_type=F32)
            for g in range(GQA):
                h = kv * GQA + g
                pg = pv[g * BLOCK:(g + 1) * BLOCK]
                l = pg[:, HEAD_DIM:HEAD_DIM + 1] + sink_es[g]
                o_ref[:, h * HEAD_DIM:(h + 1) * HEAD_DIM] = (pg[:, 0:HEAD_DIM] / l).astype(BF16)
                lse_ref[:, h:h + 1] = ms[g] + jnp.log(l)

    return pl.pallas_call(
        body, name="attn_fwd", grid=(n_ex, nb),
        in_specs=[pl.BlockSpec(memory_space=pltpu.SMEM),
                  pl.BlockSpec((BLOCK, N_HEADS * HEAD_DIM), lambda b, n: (b * nb + n, 0)),
                  pl.BlockSpec((BLOCK, KV_DIM), lambda b, n: (b * nb, kvb)),
                  pl.BlockSpec((BLOCK, KV_DIM), lambda b, n: (b * nb + jnp.maximum(n - 1, 0), kvb)),
                  pl.BlockSpec((BLOCK, KV_DIM), lambda b, n: (b * nb + n, kvb))],
        out_specs=[pl.BlockSpec((BLOCK, N_HEADS * HEAD_DIM), lambda b, n: (b * nb + n, 0)),
                   pl.BlockSpec((BLOCK, N_HEADS), lambda b, n: (b * nb + n, 0))],
        out_shape=[SDS((n_rows, N_HEADS * HEAD_DIM), BF16), SDS((n_rows, N_HEADS), F32)],
        scratch_shapes=[pltpu.VMEM((3 * BLOCK, N_KV * HEAD_DIM), BF16), pltpu.VMEM((3 * BLOCK, 2 * N_KV * HEAD_DIM), BF16),
                        pltpu.VMEM((N_KV, GQA * BLOCK, HEAD_DIM), BF16)],
        compiler_params=_params(2),
    )(sinks, qkv, qkv, qkv, qkv)


def _attn_bwd(qkv, sinks, o, lse, do, n_ex, nb):
    n_rows = qkv.shape[0]
    kvb = N_HEADS * HEAD_DIM // KV_DIM
    scale = HEAD_DIM ** -0.5
    nq = lambda r: nb - 1 - r

    def body(sink_ref, q_ref, kvm_ref, kvp_ref, kvc_ref, o_ref, lse_ref, do_ref, dqkv_ref, dsink_ref,
             k_s, v_s, dkv_s, carry_s, meta_s, q_s, do_s):
        b, r = pl.program_id(0), pl.program_id(1)
        n = nq(r)
        ok, distf = _attn_masks(n)

        @pl.when((b == 0) & (r == 0))
        def _():
            dsink_ref[...] = jnp.zeros_like(dsink_ref)

        @pl.when(r == 0)
        def _():
            carry_s[...] = jnp.zeros_like(carry_s)
            meta_s[...] = jnp.zeros_like(meta_s)

        for part, ref in enumerate((kvm_ref, kvp_ref, kvc_ref)):
            k_s[part * BLOCK:(part + 1) * BLOCK, :] = ref[:, 0:N_KV * HEAD_DIM]
            v_s[part * BLOCK:(part + 1) * BLOCK, :] = ref[:, N_KV * HEAD_DIM:KV_DIM]
        nt = (((1,), (1,)), ((), ()))
        tn = (((0,), (0,)), ((), ()))
        for kv in range(N_KV):
            kcols = slice(kv * HEAD_DIM, (kv + 1) * HEAD_DIM)
            vcols = slice(N_KV * HEAD_DIM + kv * HEAD_DIM, N_KV * HEAD_DIM + (kv + 1) * HEAD_DIM)
            for g in range(GQA):
                cols = slice((kv * GQA + g) * HEAD_DIM, (kv * GQA + g + 1) * HEAD_DIM)
                q_s[kv, g * BLOCK:(g + 1) * BLOCK, :] = q_ref[:, cols]
                do_s[kv, g * BLOCK:(g + 1) * BLOCK, :] = do_ref[:, cols]
            kh, vh = k_s[:, kcols], v_s[:, kcols]
            s4 = lax.dot_general(q_s[kv], kh, nt, preferred_element_type=F32) * scale
            dp4 = lax.dot_general(do_s[kv], vh, nt, preferred_element_type=F32)
            ps, dss = [], []
            for g in range(GQA):
                h = kv * GQA + g
                cols = slice(h * HEAD_DIM, (h + 1) * HEAD_DIM)
                rows = slice(g * BLOCK, (g + 1) * BLOCK)
                s = jnp.where(ok, s4[rows] - _alibi_slope(h) * distf, NEG_INF)
                lse_h = lse_ref[:, h:h + 1]
                p = jnp.exp(s - lse_h)
                delta = jnp.sum(do_ref[:, cols].astype(F32) * o_ref[:, cols].astype(F32), axis=-1, keepdims=True)
                dsink_ref[:, h:h + 1] += -jnp.exp(sink_ref[0, h] - lse_h) * delta
                ps.append(p.astype(BF16))
                dss.append((p * (dp4[rows] - delta)).astype(BF16))
            p4, ds4 = jnp.concatenate(ps, axis=0), jnp.concatenate(dss, axis=0)
            dq4 = jnp.dot(ds4, kh, preferred_element_type=F32) * scale
            for g in range(GQA):
                cols = slice((kv * GQA + g) * HEAD_DIM, (kv * GQA + g + 1) * HEAD_DIM)
                dqkv_ref[:, cols] = dq4[g * BLOCK:(g + 1) * BLOCK].astype(BF16)
            dkv_s[:, kcols] = lax.dot_general(ds4, q_s[kv], tn, preferred_element_type=F32) * scale
            dkv_s[:, vcols] = lax.dot_general(p4, do_s[kv], tn, preferred_element_type=F32)

        meta_s[...] += dkv_s[0:BLOCK, :]
        cur = dkv_s[2 * BLOCK:3 * BLOCK, :] + carry_s[...]
        carry_s[...] = dkv_s[BLOCK:2 * BLOCK, :]

        @pl.when(n > 0)
        def _():
            dqkv_ref[:, N_HEADS * HEAD_DIM:QKV_DIM] = cur.astype(BF16)

        @pl.when(n == 0)
        def _():
            dqkv_ref[:, N_HEADS * HEAD_DIM:QKV_DIM] = (cur + meta_s[...]).astype(BF16)

    blk = lambda b, r: (b * nb + nq(r), 0)
    return pl.pallas_call(
        body, name="attn_bwd", grid=(n_ex, nb),
        in_specs=[pl.BlockSpec(memory_space=pltpu.SMEM),
                  pl.BlockSpec((BLOCK, N_HEADS * HEAD_DIM), blk),
                  pl.BlockSpec((BLOCK, KV_DIM), lambda b, r: (b * nb, kvb)),
                  pl.BlockSpec((BLOCK, KV_DIM), lambda b, r: (b * nb + jnp.maximum(nq(r) - 1, 0), kvb)),
                  pl.BlockSpec((BLOCK, KV_DIM), lambda b, r: (b * nb + nq(r), kvb)),
                  pl.BlockSpec((BLOCK, N_HEADS * HEAD_DIM), blk),
                  pl.BlockSpec((BLOCK, N_HEADS), blk),
                  pl.BlockSpec((BLOCK, N_HEADS * HEAD_DIM), blk)],
        out_specs=[pl.BlockSpec((BLOCK, QKV_DIM), blk),
                   pl.BlockSpec((BLOCK, N_HEADS), lambda b, r: (0, 0))],
        out_shape=[SDS((n_rows, QKV_DIM), BF16), SDS((BLOCK, N_HEADS), F32)],
        scratch_shapes=[pltpu.VMEM((3 * BLOCK, N_KV * HEAD_DIM), BF16), pltpu.VMEM((3 * BLOCK, N_KV * HEAD_DIM), BF16),
                        pltpu.VMEM((3 * BLOCK, KV_DIM), F32), pltpu.VMEM((BLOCK, KV_DIM), F32),
                        pltpu.VMEM((BLOCK, KV_DIM), F32), pltpu.VMEM((N_KV, GQA * BLOCK, HEAD_DIM), BF16),
                        pltpu.VMEM((N_KV, GQA * BLOCK, HEAD_DIM), BF16)],
        compiler_params=_params(2),
    )(sinks, qkv, qkv, qkv, qkv, o, lse, do)


def _cmul_add(xr, xi, mr, mi, sr, si):
    return xr + mr * sr - mi * si, xi + mr * si + mi * sr


def _scan_tiles(buf, tab_ref, carry_s, n_groups, reverse):
    shifts = (7, 6, 4) if reverse else (1, 2, 4)

    def group(gi, carry):
        g = (n_groups - 1 - gi) if reverse else gi
        row = pl.multiple_of(g * 8, 8)
        out = []
        for j in range(PAIRS_PER_CHUNK):
            xr = buf[j, pl.ds(row, 8), 0:128]
            xi = buf[j, pl.ds(row, 8), 128:256]
            for lvl, sh in enumerate(shifts):
                xr, xi = _cmul_add(xr, xi, tab_ref[j, 2 * lvl], tab_ref[j, 2 * lvl + 1],
                                   pltpu.roll(xr, sh, 0), pltpu.roll(xi, sh, 0))
            xr, xi = _cmul_add(xr, xi, tab_ref[j, 6], tab_ref[j, 7], carry[2 * j], carry[2 * j + 1])
            buf[j, pl.ds(row, 8), 0:128] = xr
            buf[j, pl.ds(row, 8), 128:256] = xi
            edge = slice(0, 1) if reverse else slice(7, 8)
            out += [jnp.broadcast_to(xr[edge], (8, 128)), jnp.broadcast_to(xi[edge], (8, 128))]
        return tuple(out)

    carry0 = tuple(carry_s[k] for k in range(2 * PAIRS_PER_CHUNK))
    carry = lax.fori_loop(0, n_groups, group, carry0)
    for k in range(2 * PAIRS_PER_CHUNK):
        carry_s[k] = carry[k]


def _ssm_fwd(u, b_pad, c_pad, tab, d_skip, n_ex, lp):
    n_rows = u.shape[0]
    n_t = lp // TM
    n_chunk = D_MODEL // 128

    def body(u_ref, bp_ref, cp_ref, tab_ref, d_ref, yg_ref, y_ref, xs_ref, buf, carry_s):
        @pl.when(pl.program_id(2) == 0)
        def _():
            carry_s[...] = jnp.zeros_like(carry_s)

        ub = u_ref[...]
        u16 = ub.astype(BF16)
        for j in range(PAIRS_PER_CHUNK):
            buf[j] = jnp.dot(u16, bp_ref[j], preferred_element_type=F32)
        _scan_tiles(buf, tab_ref, carry_s, TM // 8, reverse=False)
        y = d_ref[...] * ub
        for j in range(PAIRS_PER_CHUNK):
            xb = buf[j].astype(BF16)
            xs_ref[j] = xb
            y = y + jnp.dot(xb, cp_ref[j], preferred_element_type=F32)
        y_ref[...] = y
        yg_ref[...] = _gelu(y).astype(BF16)

    rows = lambda b, q, t: (b * n_t + t, q)
    return pl.pallas_call(
        body, name="ssm_fwd", grid=(n_ex, n_chunk, n_t),
        in_specs=[pl.BlockSpec((TM, 128), rows),
                  pl.BlockSpec((PAIRS_PER_CHUNK, 128, 256), lambda b, q, t: (q, 0, 0)),
                  pl.BlockSpec((PAIRS_PER_CHUNK, 256, 128), lambda b, q, t: (q, 0, 0)),
                  pl.BlockSpec((PAIRS_PER_CHUNK, 8, 8, 128), lambda b, q, t: (q, 0, 0, 0)),
                  pl.BlockSpec((1, 128), lambda b, q, t: (0, q))],
        out_specs=[pl.BlockSpec((TM, 128), rows), pl.BlockSpec((TM, 128), rows),
                   pl.BlockSpec((PAIRS_PER_CHUNK, TM, 256), lambda b, q, t: (q, b * n_t + t, 0))],
        out_shape=[SDS((n_rows, D_MODEL), BF16), SDS((n_rows, D_MODEL), F32), SDS((N_PAIR, n_rows, 256), BF16)],
        scratch_shapes=[pltpu.VMEM((PAIRS_PER_CHUNK, TM, 256), F32), pltpu.VMEM((2 * PAIRS_PER_CHUNK, 8, 128), F32)],
        compiler_params=_params(3),
    )(u, b_pad, c_pad, tab, d_skip)


def _ssm_bwd(dyg, y, u, xs, ct_pad, bt_pad, tab_rev, d_skip, n_ex, lp):
    n_rows = u.shape[0]
    n_t = lp // TM
    n_chunk = D_MODEL // 128
    tile = lambda q, b, t: (b * n_t + (n_t - 1 - t), q)

    def body(dyg_ref, y_ref, u_ref, xs_ref, xp_ref, ct_ref, bt_ref, tab_ref, d_ref,
             du_ref, db_ref, dc_ref, da_ref, dd_ref, buf, xf, carry_s):
        b, t = pl.program_id(1), pl.program_id(2)

        @pl.when((b == 0) & (t == 0))
        def _():
            db_ref[...] = jnp.zeros_like(db_ref)
            dc_ref[...] = jnp.zeros_like(dc_ref)
            da_ref[...] = jnp.zeros_like(da_ref)
            dd_ref[...] = jnp.zeros_like(dd_ref)

        @pl.when(t == 0)
        def _():
            carry_s[...] = jnp.zeros_like(carry_s)

        ub = u_ref[...]
        dy = dyg_ref[...].astype(F32) * _gelu_grad(y_ref[...])
        dd_ref[...] += _fold8(dy * ub)
        dy16 = dy.astype(BF16)
        first_tile = t == n_t - 1
        for j in range(PAIRS_PER_CHUNK):
            buf[j] = jnp.dot(dy16, ct_ref[j], preferred_element_type=F32)
            dc_ref[j] += lax.dot_general(dy16, xs_ref[j], (((0,), (0,)), ((), ())), preferred_element_type=F32)
            xf[j, 16:16 + TM, :] = xs_ref[j].astype(F32)
            xf[j, 0:16, :] = jnp.where(first_tile, 0.0, xp_ref[j].astype(F32))
        _scan_tiles(buf, tab_ref, carry_s, TM // 8, reverse=True)
        du = d_ref[...] * dy
        u16 = ub.astype(BF16)
        for j in range(PAIRS_PER_CHUNK):
            g = buf[j]
            g16 = g.astype(BF16)
            du = du + jnp.dot(g16, bt_ref[j], preferred_element_type=F32)
            db_ref[j] += lax.dot_general(u16, g16, (((0,), (0,)), ((), ())), preferred_element_type=F32)
            xprev = pltpu.roll(xf[j], 1, 0)[16:16 + TM, :]
            gr, gi = g[:, 0:128], g[:, 128:256]
            pr, pi = xprev[:, 0:128], xprev[:, 128:256]
            da_ref[j, 0] += _fold8(gr * pr + gi * pi)
            da_ref[j, 1] += _fold8(gi * pr - gr * pi)
        du_ref[...] = du

    prev16 = lambda q, b, t: (q, jnp.maximum((b * n_t + (n_t - 1 - t)) * (TM // 16) - 1, 0), 0)
    return pl.pallas_call(
        body, name="ssm_bwd", grid=(n_chunk, n_ex, n_t),
        in_specs=[pl.BlockSpec((TM, 128), tile), pl.BlockSpec((TM, 128), tile), pl.BlockSpec((TM, 128), tile),
                  pl.BlockSpec((PAIRS_PER_CHUNK, TM, 256), lambda q, b, t: (q, b * n_t + (n_t - 1 - t), 0)),
                  pl.BlockSpec((PAIRS_PER_CHUNK, 16, 256), prev16),
                  pl.BlockSpec((PAIRS_PER_CHUNK, 128, 256), lambda q, b, t: (q, 0, 0)),
                  pl.BlockSpec((PAIRS_PER_CHUNK, 256, 128), lambda q, b, t: (q, 0, 0)),
                  pl.BlockSpec((PAIRS_PER_CHUNK, 8, 8, 128), lambda q, b, t: (q, 0, 0, 0)),
                  pl.BlockSpec((1, 128), lambda q, b, t: (0, q))],
        out_specs=[pl.BlockSpec((TM, 128), tile),
                   pl.BlockSpec((PAIRS_PER_CHUNK, 128, 256), lambda q, b, t: (q, 0, 0)),
                   pl.BlockSpec((PAIRS_PER_CHUNK, 128, 256), lambda q, b, t: (q, 0, 0)),
                   pl.BlockSpec((PAIRS_PER_CHUNK, 2, 8, 128), lambda q, b, t: (q, 0, 0, 0)),
                   pl.BlockSpec((8, 128), lambda q, b, t: (0, q))],
        out_shape=[SDS((n_rows, D_MODEL), F32), SDS((N_PAIR, 128, 256), F32), SDS((N_PAIR, 128, 256), F32),
                   SDS((N_PAIR, 2, 8, 128), F32), SDS((8, D_MODEL), F32)],
        scratch_shapes=[pltpu.VMEM((PAIRS_PER_CHUNK, TM, 256), F32), pltpu.VMEM((PAIRS_PER_CHUNK, TM + 16, 256), F32),
                        pltpu.VMEM((2 * PAIRS_PER_CHUNK, 8, 128), F32)],
        compiler_params=_params(3),
    )(dyg, y, u, xs, xs, ct_pad, bt_pad, tab_rev, d_skip)


def _rms_fwd(h, wn, name):
    n_rows = h.shape[0]

    def body(h_ref, wn_ref, o_ref):
        o_ref[...] = _rms(h_ref[...], wn_ref[...])

    return pl.pallas_call(
        body, name=name, grid=(n_rows // TM,),
        in_specs=[pl.BlockSpec((TM, D_MODEL), lambda i: (i, 0)), pl.BlockSpec((1, D_MODEL), lambda i: (0, 0))],
        out_specs=pl.BlockSpec((TM, D_MODEL), lambda i: (i, 0)),
        out_shape=SDS((n_rows, D_MODEL), F32), compiler_params=_params(1),
    )(h, wn)


def _rms_bwd_call(dhn, h, wn, dres, name):
    n_rows = h.shape[0]

    def body(dhn_ref, h_ref, wn_ref, dres_ref, o_ref, dw_ref):
        @pl.when(pl.program_id(0) == 0)
        def _():
            dw_ref[...] = jnp.zeros_like(dw_ref)

        dh, dw_rows = _rms_bwd(dhn_ref[...], h_ref[...], wn_ref[...])
        o_ref[...] = dres_ref[...] + dh
        dw_ref[...] += _fold8(dw_rows)

    row = lambda i: (i, 0)
    return pl.pallas_call(
        body, name=name, grid=(n_rows // TM,),
        in_specs=[pl.BlockSpec((TM, D_MODEL), row), pl.BlockSpec((TM, D_MODEL), row),
                  pl.BlockSpec((1, D_MODEL), lambda i: (0, 0)), pl.BlockSpec((TM, D_MODEL), row)],
        out_specs=[pl.BlockSpec((TM, D_MODEL), row), pl.BlockSpec((8, D_MODEL), lambda i: (0, 0))],
        out_shape=[SDS((n_rows, D_MODEL), F32), SDS((8, D_MODEL), F32)], compiler_params=_params(1),
    )(dhn, h, wn, dres)


def _glu_fwd(h, z):
    n_rows = h.shape[0]

    def body(h_ref, val_ref, gate_ref, o_ref):
        o_ref[...] = h_ref[...] + val_ref[...].astype(F32) * jax.nn.sigmoid(gate_ref[...].astype(F32))

    row = lambda i: (i, 0)
    return pl.pallas_call(
        body, name="glu_fwd", grid=(n_rows // TM,),
        in_specs=[pl.BlockSpec((TM, D_MODEL), row), pl.BlockSpec((TM, D_MODEL), row),
                  pl.BlockSpec((TM, D_MODEL), lambda i: (i, 1))],
        out_specs=pl.BlockSpec((TM, D_MODEL), row),
        out_shape=SDS((n_rows, D_MODEL), F32), compiler_params=_params(1),
    )(h, z, z)


def _glu_bwd(dh, z):
    n_rows = dh.shape[0]

    def body(dh_ref, val_ref, gate_ref, dz_ref):
        sg = jax.nn.sigmoid(gate_ref[...].astype(F32))
        d = dh_ref[...]
        dz_ref[:, 0:D_MODEL] = (d * sg).astype(BF16)
        dz_ref[:, D_MODEL:2 * D_MODEL] = (d * val_ref[...].astype(F32) * sg * (1.0 - sg)).astype(BF16)

    row = lambda i: (i, 0)
    return pl.pallas_call(
        body, name="glu_bwd", grid=(n_rows // TM,),
        in_specs=[pl.BlockSpec((TM, D_MODEL), row), pl.BlockSpec((TM, D_MODEL), row),
                  pl.BlockSpec((TM, D_MODEL), lambda i: (i, 1))],
        out_specs=pl.BlockSpec((TM, 2 * D_MODEL), row),
        out_shape=SDS((n_rows, 2 * D_MODEL), BF16), compiler_params=_params(1),
    )(dh, z, z)


def _loss_head(h, wn, target, n_ex, nb):
    n_rows = h.shape[0]

    def body(h_ref, wn_ref, t_ref, dh_ref, loss_ref, dw_ref):
        b, n = pl.program_id(0), pl.program_id(1)

        @pl.when((b == 0) & (n == 0))
        def _():
            loss_ref[...] = jnp.zeros_like(loss_ref)
            dw_ref[...] = jnp.zeros_like(dw_ref)

        @pl.when(n == 0)
        def _():
            dh_ref[...] = jnp.zeros_like(dh_ref)

        @pl.when(n > 0)
        def _():
            hh = h_ref[...]
            diff = _rms(hh, wn_ref[...]) - t_ref[...]
            loss_ref[...] += 0.5 * jnp.sum(diff * diff) * (1.0 / D_MODEL)
            dh, dw_rows = _rms_bwd(diff * (1.0 / D_MODEL), hh, wn_ref[...])
            dh_ref[...] = dh
            dw_ref[...] += _fold8(dw_rows)

    return pl.pallas_call(
        body, name="loss_head", grid=(n_ex, nb),
        in_specs=[pl.BlockSpec((BLOCK, D_MODEL), lambda b, n: (b * nb + n, 0)),
                  pl.BlockSpec((1, D_MODEL), lambda b, n: (0, 0)),
                  pl.BlockSpec((BLOCK, D_MODEL), lambda b, n: (b * (nb - 1) + jnp.maximum(n - 1, 0), 0))],
        out_specs=[pl.BlockSpec((BLOCK, D_MODEL), lambda b, n: (b * nb + n, 0)),
                   pl.BlockSpec((8, 128), lambda b, n: (0, 0)),
                   pl.BlockSpec((8, D_MODEL), lambda b, n: (0, 0))],
        out_shape=[SDS((n_rows, D_MODEL), F32), SDS((8, 128), F32), SDS((8, D_MODEL), F32)],
        compiler_params=_params(2),
    )(h, wn, target)


def _adamw(pieces, w, m, v, name):
    n_layers = len(pieces)
    rows, cols = pieces[0].shape[1:]
    rb = rows
    for cand in (256, 136, 128, 64, 32, 16, 8):
        if rows % cand == 0 and rows > cand:
            rb = cand
            break
    n_blk = rows // rb
    c1 = 1.0 / (1.0 - ADAM_B1 ** ADAM_STEP)
    c2 = 1.0 / (1.0 - ADAM_B2 ** ADAM_STEP)

    def body(*refs):
        p_refs = refs[:n_layers]
        w_ref, m_ref, v_ref, g_out, d_out, m_out, v_out = refs[n_layers:]
        layer = pl.program_id(0)
        g = None
        for l, p_ref in enumerate(p_refs):
            gl = p_ref[0].astype(F32)
            for k in range(1, N_DEV):
                gl = gl + p_ref[k].astype(F32)
            g = gl if g is None else jnp.where(layer == l, gl, g)
        m_new = ADAM_B1 * m_ref[...] + (1.0 - ADAM_B1) * g
        v_new = ADAM_B2 * v_ref[...] + (1.0 - ADAM_B2) * (g * g)
        g_out[...] = g
        m_out[...] = m_new
        v_out[...] = v_new
        d_out[...] = -ADAM_LR * ((m_new * c1) / (jnp.sqrt(v_new * c2) + ADAM_EPS) + ADAM_WD * w_ref[...])

    def piece_spec(l):
        return pl.BlockSpec((N_DEV, rb, cols), lambda ly, i: (0, jnp.where(ly == l, i, 0), 0))

    blk = pl.BlockSpec((rb, cols), lambda ly, i: (ly * n_blk + i, 0))
    return pl.pallas_call(
        body, name=name, grid=(n_layers, n_blk),
        in_specs=[piece_spec(l) for l in range(n_layers)] + [blk, blk, blk],
        out_specs=[blk, blk, blk, blk],
        out_shape=[SDS((n_layers * rows, cols), F32)] * 4, compiler_params=_params(2),
    )(*pieces, w, m, v)


_HBM = pl.BlockSpec(memory_space=pltpu.HBM)
_SEM = pl.BlockSpec(memory_space=pltpu.SEMAPHORE)
_EFFECT = pltpu.SideEffectType.DATAFLOW_SIDE_EFFECTING
N_GATHER_PEERS = N_CHIPS - 1
N_EXCHANGE_PEERS = N_DEV - 1


def _gather_copies(srcs, lands, send_sems, recv_sems):
    x, y, c = lax.axis_index("x"), lax.axis_index("y"), lax.axis_index("c")
    mine = 2 * x + y
    chips = [(1 - x, y), (x, 1 - y), (1 - x, 1 - y)]
    out, inc = [], []
    for a in range(len(srcs)):
        for k, (px, py) in enumerate(chips):
            j = a * N_GATHER_PEERS + k
            sems = dict(send_sem=send_sems.at[j], recv_sem=recv_sems.at[j], device_id=(px, py, c),
                        device_id_type=pl.DeviceIdType.MESH)
            out.append(pltpu.make_async_remote_copy(src_ref=srcs[a], dst_ref=lands[a].at[mine], **sems))
            inc.append(pltpu.make_async_remote_copy(src_ref=srcs[a], dst_ref=lands[a].at[2 * px + py], **sems))
    return out, inc


def _exchange_copies(n_scatter):
    def copies(srcs, lands, send_sems, recv_sems):
        x, y, c = lax.axis_index("x"), lax.axis_index("y"), lax.axis_index("c")
        me = 4 * x + 2 * y + c
        peers = [(x ^ (k >> 2), y ^ ((k >> 1) & 1), c ^ (k & 1)) for k in range(1, N_DEV)]
        out, inc = [], []
        for a in range(len(srcs)):
            for k, (px, py, pc) in enumerate(peers):
                j = a * N_EXCHANGE_PEERS + k
                sems = dict(send_sem=send_sems.at[j], recv_sem=recv_sems.at[j], device_id=(px, py, pc),
                            device_id_type=pl.DeviceIdType.MESH)
                theirs = srcs[a].at[2 * px + py] if a < n_scatter else srcs[a]
                mine = srcs[a].at[2 * x + y] if a < n_scatter else srcs[a]
                out.append(pltpu.make_async_remote_copy(src_ref=theirs, dst_ref=lands[a].at[me], **sems))
                inc.append(pltpu.make_async_remote_copy(src_ref=mine, dst_ref=lands[a].at[4 * px + 2 * py + pc], **sems))
        return out, inc

    return copies


def _split_start(groups, copies_fn, n_peers, name):
    sizes = [len(srcs) for srcs, _ in groups]
    flat = [a for srcs, lands in groups for a in list(srcs) + list(lands)]
    n_flat, n_grp = len(flat), len(groups)

    def body(*refs):
        sems = refs[2 * n_flat:2 * n_flat + 2 * n_grp]
        token = refs[-1]
        at = 0
        for gi, n in enumerate(sizes):
            out, _ = copies_fn(refs[at:at + n], refs[at + n:at + 2 * n], sems[2 * gi], sems[2 * gi + 1])
            for cp in out:
                cp.start()
            at += 2 * n
        token[...] = jnp.zeros_like(token)

    sem_shapes = []
    for n in sizes:
        sem_shapes += [pltpu.SemaphoreType.DMA((n * n_peers,)), pltpu.SemaphoreType.DMA((n * n_peers,))]
    res = pl.pallas_call(
        body, name=name,
        out_shape=(*[pltpu.HBM(a.shape, a.dtype) for a in flat], *sem_shapes, SDS((8, 128), F32)),
        in_specs=[_HBM] * n_flat,
        out_specs=(*[_HBM] * n_flat, *[_SEM] * (2 * n_grp), pl.BlockSpec(memory_space=pltpu.VMEM)),
        input_output_aliases={i: i for i in range(n_flat)},
        compiler_params=pltpu.CompilerParams(has_side_effects=_EFFECT),
    )(*[pltpu.with_memory_space_constraint(a, pltpu.HBM) for a in flat])
    handles, at = [], 0
    for gi, n in enumerate(sizes):
        handles.append((res[n_flat + 2 * gi], res[n_flat + 2 * gi + 1], list(res[at:at + n]), list(res[at + n:at + 2 * n])))
        at += 2 * n
    return handles, res[-1]


def _split_wait(handle, after, copies_fn, name):
    send_sems, recv_sems, srcs, lands = handle
    n = len(srcs)

    def body(*refs):
        out, inc = copies_fn(refs[:n], refs[n:2 * n], refs[2 * n], refs[2 * n + 1])
        for cp in out:
            cp.wait_send()
        for cp in inc:
            cp.wait_recv()

    flat = list(srcs) + list(lands)
    res = pl.pallas_call(
        body, name=name,
        out_shape=tuple(pltpu.HBM(a.shape, a.dtype) for a in flat),
        in_specs=[_HBM] * (2 * n) + [_SEM, _SEM, pl.BlockSpec(memory_space=pl.ANY)],
        out_specs=tuple([_HBM] * (2 * n)),
        input_output_aliases={i: i for i in range(2 * n)},
        compiler_params=pltpu.CompilerParams(has_side_effects=_EFFECT),
    )(*flat, send_sems, recv_sems, after)
    return list(res[n:])


def _landing(own, slot, n_slots):
    return lax.dynamic_update_index_in_dim(lax.empty((n_slots,) + own.shape, own.dtype), own, slot, 0)


def _ssm_discretize(lam_re, lam_im, log_dt, b_re, b_im):
    lr = jnp.minimum(lam_re, LAMBDA_RE_MAX)
    li = lam_im
    dt = jnp.exp(log_dt)[:, None]
    mag = jnp.exp(lr * dt)
    ar, ai = mag * jnp.cos(li * dt), mag * jnp.sin(li * dt)
    den = lr * lr + li * li
    nr, ni = ar - 1.0, ai
    gr, gi = (nr * lr + ni * li) / den, (ni * lr - nr * li) / den
    bbr = gr[:, :, None] * b_re - gi[:, :, None] * b_im
    bbi = gr[:, :, None] * b_im + gi[:, :, None] * b_re
    return ar, ai, bbr, bbi


def _pair_lanes(t):
    return t.reshape(N_PAIR, 2 * SSM_STATE)


def _chan_state_blocks(t_gcp):
    t = t_gcp.reshape(N_PAIR, 2, SSM_GROUP, SSM_STATE)
    eye2 = jnp.eye(2, dtype=t.dtype)
    blk = jnp.einsum("rgcp,gh->rgchp", t, eye2).reshape(N_PAIR, 2 * SSM_GROUP, 2 * SSM_STATE)
    place = jax.nn.one_hot(jnp.arange(N_PAIR) % PAIRS_PER_CHUNK, PAIRS_PER_CHUNK, dtype=t.dtype)
    return jnp.einsum("rcl,rj->rjcl", blk, place).reshape(N_PAIR, 128, 2 * SSM_STATE)


def _chan_state_unblock(t):
    t = t.reshape(N_PAIR, PAIRS_PER_CHUNK, 2, SSM_GROUP, 2, SSM_STATE)
    place = jax.nn.one_hot(jnp.arange(N_PAIR) % PAIRS_PER_CHUNK, PAIRS_PER_CHUNK, dtype=t.dtype)
    t = jnp.einsum("rjgchp,rj->rgchp", t, place)
    t = jnp.einsum("rgchp,gh->rgcp", t, jnp.eye(2, dtype=t.dtype))
    return t.reshape(SSM_NG, SSM_GROUP, SSM_STATE)


def _scan_tables(ar, ai, reverse):
    ar, ai = _pair_lanes(ar), _pair_lanes(ai)
    if reverse:
        ai = -ai
    pows = [(ar, ai)]
    for _ in range(7):
        pr, pi = pows[-1]
        pows.append((pr * ar - pi * ai, pr * ai + pi * ar))
    rows = jnp.arange(8)[None, :, None]
    tiles = []
    for k in (1, 2, 4):
        keep = (rows <= 7 - k) if reverse else (rows >= k)
        for part in pows[k - 1]:
            tiles.append(jnp.where(keep, part[:, None, :], 0.0))
    order = list(range(7, -1, -1)) if reverse else list(range(8))
    for comp in (0, 1):
        tiles.append(jnp.stack([pows[e][comp] for e in order], axis=1))
    return jnp.stack(tiles, axis=1)


def _local_step(x, target, w, late_weights, on_grads):
    n_ex, seq, _ = x.shape
    lp = seq + BLOCK
    nb = lp // BLOCK
    n_rows = n_ex * lp
    g = {}

    head = jnp.concatenate([jnp.zeros((PAD, D_MODEL), F32), w["meta_tokens"]], axis=0)
    h0 = jnp.concatenate([jnp.broadcast_to(head[None], (n_ex, BLOCK, D_MODEL)), x], axis=1).reshape(n_rows, D_MODEL)

    qkv, hn_a = _rms_mm_cols(h0, w["attn_norm_w"], w["attn_w_qkv"], 0, "qkv_fwd")
    att, lse = _attn_fwd(qkv, w["attn_sinks"], n_ex, nb)
    h1 = _mm_acc(att, w["attn_w_o"], 0, False, "attn_out_fwd", res=h0)
    w = {**w, **late_weights(0, att)}
    h2, a0, hn_m0 = _mlp_fwd(h1, w["mlp_norm_w"][0:1], w["mlp_w_up"][0], w["mlp_w_down"][0], 0, "mlp0_fwd")
    late = late_weights(1, h2)
    w["ssm_w_glu"] = late["ssm_w_glu"]
    w["mlp_w_up"], w["mlp_w_down"] = w["mlp_w_up"] + late["mlp_w_up"], w["mlp_w_down"] + late["mlp_w_down"]

    ar, ai, bbr, bbi = _ssm_discretize(w["ssm_lambda_re"], w["ssm_lambda_im"], w["ssm_log_dt"], w["ssm_b_re"], w["ssm_b_im"])
    b_blk = jnp.concatenate([_chan_state_blocks(jnp.swapaxes(bbr, 1, 2)), _chan_state_blocks(jnp.swapaxes(bbi, 1, 2))], axis=2)
    c_blk = jnp.concatenate([_chan_state_blocks(w["ssm_c_re"]), -_chan_state_blocks(w["ssm_c_im"])], axis=2)
    b_pad, bt_pad = b_blk.astype(BF16), jnp.swapaxes(b_blk, 1, 2).astype(BF16)
    ct_pad, c_pad = c_blk.astype(BF16), jnp.swapaxes(c_blk, 1, 2).astype(BF16)
    u = _rms_fwd(h2, w["ssm_norm_w"], "ssm_norm_fwd")
    yg, y, xs = _ssm_fwd(u, b_pad, c_pad, _scan_tables(ar, ai, False), w["ssm_d"], n_ex, lp)
    z = _mm_cols(yg, w["ssm_w_glu"], 0, False, "glu_mm_fwd")
    h3 = _glu_fwd(h2, z)
    h4, a1, hn_m1 = _mlp_fwd(h3, w["mlp_norm_w"][1:2], w["mlp_w_up"][1], w["mlp_w_down"][1], 0, "mlp1_fwd")

    dh4, loss_tile, dnorm_f = _loss_head(h4, w["final_norm_w"], target.reshape(n_ex * seq, D_MODEL), n_ex, nb)

    def mlp_bwd(dh_out, h_in, a, hn, layer, tag, norm_w):
        dhn, dw_up, dw_down = None, None, None
        for s in range(N_CHIPS):
            final = s == N_CHIPS - 1
            res = _mlp_bwd_shard(s, dh_out, a, hn, dhn, h_in if final else None, norm_w,
                                 w["mlp_w_up"][layer], w["mlp_w_down"][layer], dw_up, dw_down, f"{tag}_bwd{s}")
            dhn, dw_up, dw_down = res[:3]
        return dhn, res[3], dw_up, dw_down

    dh3, dnorm_m1, dwu1, dwd1 = mlp_bwd(dh4, h3, a1, hn_m1, 1, "mlp1", w["mlp_norm_w"][1:2])
    tok = on_grads("mlp1", {"mlp_w_up": dwu1, "mlp_w_down": dwd1})
    dz = _glu_bwd(dh3, z)
    dyg = _mm_acc(dz, w["ssm_w_glu"], 0, True, "glu_mm_dx", out_dtype=BF16)
    g["ssm_w_glu"] = _mm_tn(yg, dz, N_CHIPS, False, "glu_mm_dw")
    du, db_blk, dc_blk, da_t, dd_t = _ssm_bwd(dyg, y, u, xs, ct_pad, bt_pad, _scan_tables(ar, ai, True),
                                              w["ssm_d"] + tok, n_ex, lp)
    dh2, dnorm_s = _rms_bwd_call(du, h2, w["ssm_norm_w"], dh3, "ssm_norm_bwd")
    g["ssm_c_re"] = _chan_state_unblock(dc_blk[:, :, 0:128])
    g["ssm_c_im"] = -_chan_state_unblock(dc_blk[:, :, 128:256])
    g_bbr = jnp.swapaxes(_chan_state_unblock(db_blk[:, :, 0:128]), 1, 2)
    g_bbi = jnp.swapaxes(_chan_state_unblock(db_blk[:, :, 128:256]), 1, 2)
    g_a = jnp.sum(da_t, axis=2).reshape(N_PAIR, 2, 2, SSM_STATE)
    g_ar, g_ai = g_a[:, 0].reshape(SSM_NG, SSM_STATE), g_a[:, 1].reshape(SSM_NG, SSM_STATE)
    _, vjp = jax.vjp(_ssm_discretize, w["ssm_lambda_re"], w["ssm_lambda_im"], w["ssm_log_dt"], w["ssm_b_re"], w["ssm_b_im"])
    g["ssm_lambda_re"], g["ssm_lambda_im"], g["ssm_log_dt"], g["ssm_b_re"], g["ssm_b_im"] = vjp((g_ar, g_ai, g_bbr, g_bbi))
    tok = on_grads("ssm", g)
    g = {}
    dh1, dnorm_m0, dwu0, dwd0 = mlp_bwd(dh2, h1, a0, hn_m0, 0, "mlp0", w["mlp_norm_w"][0:1] + tok)
    datt = _mm_cols(dh1, w["attn_w_o"], 0, True, "attn_out_dx")
    dw_o = _mm_tn(att, dh1, N_CHIPS, True, "attn_out_dw")
    tok = on_grads("mlp0", {"mlp_w_up": dwu0, "mlp_w_down": dwd0, "attn_w_o": dw_o})
    dqkv, dsink_rows = _attn_bwd(qkv, w["attn_sinks"] + tok, att, lse, datt, n_ex, nb)
    tok = on_grads("qkv", {"attn_w_qkv": _mm_tn(hn_a, dqkv, N_CHIPS, False, "qkv_dw")})
    dh0, dnorm_a = _mm_acc(dqkv, w["attn_w_qkv"], 0, True, "qkv_dx", rms_bwd=(h0, w["attn_norm_w"] + tok, dh1))

    dh0 = dh0.reshape(n_ex, lp, D_MODEL)
    on_grads("rest", {
        "mlp_norm_w": jnp.stack([jnp.sum(dnorm_m0, axis=0), jnp.sum(dnorm_m1, axis=0)]),
        "final_norm_w": jnp.sum(dnorm_f, axis=0),
        "attn_norm_w": jnp.sum(dnorm_a, axis=0)[None],
        "ssm_norm_w": jnp.sum(dnorm_s, axis=0)[None],
        "attn_sinks": jnp.sum(dsink_rows, axis=0)[None],
        "ssm_d": jnp.sum(dd_t, axis=0)[None],
        "meta_tokens": jnp.sum(dh0[:, PAD:BLOCK], axis=0)})
    return loss_tile, dh0[:, BLOCK:]


_SHARDED_SMALL = ("meta_tokens", "ssm_norm_w", "ssm_d")
_REP_SSM = ("ssm_lambda_re", "ssm_lambda_im", "ssm_log_dt", "ssm_b_re", "ssm_b_im", "ssm_c_re", "ssm_c_im")
_REP_MISC = ("attn_norm_w", "attn_sinks", "mlp_norm_w", "final_norm_w")
_BIG = ("attn_w_qkv", "attn_w_o", "ssm_w_glu", "mlp_w_up", "mlp_w_down")


def _pack(parts, cols):
    flat = jnp.concatenate([p.reshape(-1) for p in parts])
    rows = -(-flat.shape[0] // (8 * cols)) * 8
    return jnp.pad(flat, (0, rows * cols - flat.shape[0])).reshape(rows, cols)


def _unpack(packed, like):
    flat = packed.reshape(-1)
    out, at = [], 0
    for p in like:
        out.append(flat[at:at + p.size].reshape(p.shape))
        at += p.size
    return out


def kernel(x, meta_tokens, attn_norm_w, attn_w_qkv, attn_sinks, attn_w_o, ssm_norm_w, ssm_lambda_re, ssm_lambda_im, ssm_log_dt, ssm_b_re, ssm_b_im, ssm_c_re, ssm_c_im, ssm_d, ssm_w_glu, mlp_norm_w, mlp_w_up, mlp_w_down, final_norm_w, loss_target, m_meta_tokens, m_attn_norm_w, m_attn_w_qkv, m_attn_sinks, m_attn_w_o, m_ssm_norm_w, m_ssm_lambda_re, m_ssm_lambda_im, m_ssm_log_dt, m_ssm_b_re, m_ssm_b_im, m_ssm_c_re, m_ssm_c_im, m_ssm_d, m_ssm_w_glu, m_mlp_norm_w, m_mlp_w_up, m_mlp_w_down, m_final_norm_w, v_meta_tokens, v_attn_norm_w, v_attn_w_qkv, v_attn_sinks, v_attn_w_o, v_ssm_norm_w, v_ssm_lambda_re, v_ssm_lambda_im, v_ssm_log_dt, v_ssm_b_re, v_ssm_b_im, v_ssm_c_re, v_ssm_c_im, v_ssm_d, v_ssm_w_glu, v_mlp_norm_w, v_mlp_w_up, v_mlp_w_down, v_final_norm_w):
    names = ("meta_tokens", "attn_norm_w", "attn_w_qkv", "attn_sinks", "attn_w_o", "ssm_norm_w", "ssm_lambda_re",
             "ssm_lambda_im", "ssm_log_dt", "ssm_b_re", "ssm_b_im", "ssm_c_re", "ssm_c_im", "ssm_d", "ssm_w_glu",
             "mlp_norm_w", "mlp_w_up", "mlp_w_down", "final_norm_w")
    wts = dict(zip(names, (meta_tokens, attn_norm_w, attn_w_qkv, attn_sinks, attn_w_o, ssm_norm_w, ssm_lambda_re,
                           ssm_lambda_im, ssm_log_dt, ssm_b_re, ssm_b_im, ssm_c_re, ssm_c_im, ssm_d, ssm_w_glu,
                           mlp_norm_w, mlp_w_up, mlp_w_down, final_norm_w)))
    mom = dict(zip(names, (m_meta_tokens, m_attn_norm_w, m_attn_w_qkv, m_attn_sinks, m_attn_w_o, m_ssm_norm_w,
                           m_ssm_lambda_re, m_ssm_lambda_im, m_ssm_log_dt, m_ssm_b_re, m_ssm_b_im, m_ssm_c_re,
                           m_ssm_c_im, m_ssm_d, m_ssm_w_glu, m_mlp_norm_w, m_mlp_w_up, m_mlp_w_down, m_final_norm_w)))
    var = dict(zip(names, (v_meta_tokens, v_attn_norm_w, v_attn_w_qkv, v_attn_sinks, v_attn_w_o, v_ssm_norm_w,
                           v_ssm_lambda_re, v_ssm_lambda_im, v_ssm_log_dt, v_ssm_b_re, v_ssm_b_im, v_ssm_c_re,
                           v_ssm_c_im, v_ssm_d, v_ssm_w_glu, v_mlp_norm_w, v_mlp_w_up, v_mlp_w_down, v_final_norm_w)))

    my_chip = 2 * lax.axis_index("x") + lax.axis_index("y")
    my_dev = 2 * my_chip + lax.axis_index("c")
    small_mine = _pack([wts[n] for n in _SHARDED_SMALL], 128)
    first = [attn_w_qkv.astype(BF16), attn_w_o.astype(BF16), small_mine]
    up16, down16 = mlp_w_up.astype(BF16), mlp_w_down.astype(BF16)
    mlp0 = [up16[0:1], down16[0:1]]
    rest = [ssm_w_glu.astype(BF16), up16[1:2], down16[1:2]]
    handles, _ = _split_start([(srcs, [_landing(a, my_chip, N_CHIPS) for a in srcs]) for srcs in (first, mlp0, rest)],
                              _gather_copies, N_GATHER_PEERS, "gather_start")
    got = _split_wait(handles[0], small_mine, _gather_copies, "gather_wait_first")
    full = {n: wts[n] for n in _REP_MISC}
    full["final_norm_w"] = final_norm_w[None]
    for n in _REP_SSM:
        full[n] = wts[n][0]
    full["attn_w_qkv"], full["attn_w_o"] = got[0], got[1]
    smalls = [_unpack(got[2][s], [wts[n] for n in _SHARDED_SMALL]) for s in range(N_CHIPS)]
    for k, n in enumerate(_SHARDED_SMALL):
        full[n] = jnp.concatenate([smalls[s][k] for s in range(N_CHIPS)], axis=1)

    def late_weights(stage, after):
        if stage == 0:
            up, down = _split_wait(handles[1], after, _gather_copies, "gather_wait_mlp0")
            return {"mlp_w_up": [up], "mlp_w_down": [down]}
        glu, up, down = _split_wait(handles[2], after, _gather_copies, "gather_wait_rest")
        return {"ssm_w_glu": glu, "mlp_w_up": [up], "mlp_w_down": [down]}

    def shard_cols(t):
        return jnp.swapaxes(t.reshape(t.shape[0], N_CHIPS, t.shape[1] // N_CHIPS), 0, 1)

    pending = {}

    def on_grads(tag, g):
        scatter = [g[n] for n in _BIG if n in g]
        whole = []
        if tag == "ssm":
            whole = [_pack([g[n] for n in _REP_SSM], D_MODEL)]
        if tag == "rest":
            parts = [shard_cols(g[n]) for n in _SHARDED_SMALL]
            scatter = [jnp.stack([_pack([p[s] for p in parts], 128) for s in range(N_CHIPS)])]
            whole = [_pack([g[n] for n in _REP_MISC], D_MODEL)]
        srcs = scatter + whole
        lands = [_landing(lax.dynamic_index_in_dim(a, my_chip, 0, keepdims=False), my_dev, N_DEV) for a in scatter]
        lands += [_landing(a, my_dev, N_DEV) for a in whole]
        hs, token = _split_start([(srcs, lands)], _exchange_copies(len(scatter)), N_EXCHANGE_PEERS, "exchange_start_" + tag)
        pending[tag] = (hs[0], len(scatter))
        return token[0, 0]

    loss_tile, grad_x = _local_step(x, loss_target, full, late_weights, on_grads)
    loss = lax.psum(loss_tile[0, 0], ("x", "y", "c"))

    recv = {}
    for tag, (handle, n_scatter) in pending.items():
        recv[tag] = _split_wait(handle, grad_x, _exchange_copies(n_scatter), "exchange_wait_" + tag)

    out = {}

    def update(tag, pieces, w2, m2, v2):
        return _adamw(pieces, w2, m2, v2, "adamw_" + tag)

    def update_weight(n, pieces):
        shp = wts[n].shape
        r2 = (math.prod(shp[:-1]), shp[-1])
        res = update(n, pieces, wts[n].reshape(r2), mom[n].reshape(r2), var[n].reshape(r2))
        out[n] = [t.reshape(shp) for t in res]

    update_weight("mlp_w_up", [recv["mlp0"][1], recv["mlp1"][0]])
    update_weight("mlp_w_down", [recv["mlp0"][2], recv["mlp1"][1]])
    update_weight("attn_w_o", [recv["mlp0"][0]])
    update_weight("ssm_w_glu", [recv["ssm"][0]])
    update_weight("attn_w_qkv", [recv["qkv"][0]])
    for tag, group, pieces, cols in (("small", _SHARDED_SMALL, recv["rest"][0], 128),
                                     ("rep_ssm", _REP_SSM, recv["ssm"][1], D_MODEL),
                                     ("rep_misc", _REP_MISC, recv["rest"][1], D_MODEL)):
        like = [wts[n] for n in group]
        res = update(tag, [pieces], _pack(like, cols), _pack([mom[n] for n in group], cols),
                     _pack([var[n] for n in group], cols))
        for k, n in enumerate(group):
            out[n] = [_unpack(t, like)[k] for t in res]

    return (loss, grad_x, *[out[n][0] for n in names], *[out[n][1] for n in names],
            *[out[n][2] for n in names], *[out[n][3] for n in names])
```

```python
import functools
import math

import jax
import jax.numpy as jnp
from jax import lax
from jax.experimental import pallas as pl
from jax.experimental.pallas import tpu as pltpu

F32 = jnp.float32
BF16 = jnp.bfloat16
SDS = jax.ShapeDtypeStruct

D_MODEL = 1024
N_HEADS = 16
N_KV = 4
GQA = N_HEADS // N_KV
HEAD_DIM = 64
BLOCK = 128
N_META = 16
PAD = BLOCK - N_META
QKV_DIM = (N_HEADS + 2 * N_KV) * HEAD_DIM
KV_DIM = 2 * N_KV * HEAD_DIM
D_FF = 4 * D_MODEL
N_CHIPS = 4
N_DEV = 8
SSM_GROUP = 16
SSM_NG = D_MODEL // SSM_GROUP
SSM_STATE = 64
N_PAIR = SSM_NG // 2
PAIRS_PER_CHUNK = 4
RMS_EPS = 1e-6
NEG_INF = -1e30
LAMBDA_RE_MAX = -1e-4
ADAM_LR, ADAM_B1, ADAM_B2, ADAM_EPS, ADAM_WD, ADAM_STEP = 0.001, 0.9, 0.999, 1e-08, 0.01, 10

TM = 384
MM_TILES = (768, 384)
MLP_BWD_TILES = (768, 384)
TN_TILES = (1408, 768, 384)
VMEM_LIMIT = 56 * 1024 * 1024


def _params(n_grid):
    return pltpu.CompilerParams(dimension_semantics=("arbitrary",) * n_grid, vmem_limit_bytes=VMEM_LIMIT)


def _row_tile(n_rows, tiles):
    return next(t for t in tiles if n_rows % t == 0)


def _rms(h, w):
    r = lax.rsqrt(jnp.mean(h * h, axis=-1, keepdims=True) + RMS_EPS)
    return h * r * w


def _rms_bwd(dhn, h, w):
    r = lax.rsqrt(jnp.mean(h * h, axis=-1, keepdims=True) + RMS_EPS)
    g = dhn * w
    proj = jnp.sum(g * h, axis=-1, keepdims=True) * (1.0 / D_MODEL)
    return r * g - h * (r * r * r) * proj, dhn * h * r


def _fold8(t):
    return jnp.sum(t.reshape(t.shape[0] // 8, 8, t.shape[1]), axis=0)


def _gelu(y):
    return 0.5 * y * (1.0 + jnp.tanh(0.7978845608028654 * (y + 0.044715 * y * y * y)))


def _gelu_grad(y):
    t = jnp.tanh(0.7978845608028654 * (y + 0.044715 * y * y * y))
    return 0.5 * (1.0 + t) + 0.5 * y * (1.0 - t * t) * 0.7978845608028654 * (1.0 + 3.0 * 0.044715 * y * y)


def _w4_spec(w4):
    n_sh, _, k, n = w4.shape
    return pl.BlockSpec((n_sh, None, k, n), lambda i: (0, 0, 0, 0))


def _rms_mm_cols(h, wn, w4, name):
    n_rows = h.shape[0]
    n_sh, _, k, n = w4.shape
    tm = _row_tile(n_rows, MM_TILES)

    def body(h_ref, wn_ref, w_ref, o_ref, hn_ref):
        hn = _rms(h_ref[...], wn_ref[...]).astype(BF16)
        hn_ref[...] = hn
        for s in range(n_sh):
            o_ref[:, s * n:(s + 1) * n] = jnp.dot(hn, w_ref[s], preferred_element_type=F32).astype(o_ref.dtype)

    return pl.pallas_call(
        body, name=name, grid=(n_rows // tm,),
        in_specs=[pl.BlockSpec((tm, k), lambda i: (i, 0)), pl.BlockSpec((1, k), lambda i: (0, 0)), _w4_spec(w4)],
        out_specs=[pl.BlockSpec((tm, n_sh * n), lambda i: (i, 0)), pl.BlockSpec((tm, k), lambda i: (i, 0))],
        out_shape=[SDS((n_rows, n_sh * n), BF16), SDS((n_rows, k), BF16)],
        compiler_params=_params(1),
    )(h, wn, w4)


def _mm_cols(x, w4, trans_w, name):
    n_rows, kx = x.shape
    tm = _row_tile(n_rows, MM_TILES)
    n_sh, _, k, n = w4.shape
    n_out = k if trans_w else n
    dims = (((1,), (1,)), ((), ())) if trans_w else (((1,), (0,)), ((), ()))

    def body(x_ref, w_ref, o_ref):
        x16 = x_ref[...].astype(BF16)
        for s in range(n_sh):
            o_ref[:, s * n_out:(s + 1) * n_out] = lax.dot_general(
                x16, w_ref[s], dims, preferred_element_type=F32).astype(o_ref.dtype)

    return pl.pallas_call(
        body, name=name, grid=(n_rows // tm,),
        in_specs=[pl.BlockSpec((tm, kx), lambda i: (i, 0)), _w4_spec(w4)],
        out_specs=pl.BlockSpec((tm, n_sh * n_out), lambda i: (i, 0)),
        out_shape=SDS((n_rows, n_sh * n_out), BF16),
        compiler_params=_params(1),
    )(x, w4)


def _mm_acc(x, w4, trans_w, name, res=None, rms_bwd=None, out_dtype=F32):
    n_rows = x.shape[0]
    tm = _row_tile(n_rows, MM_TILES)
    n_sh, _, k, n = w4.shape
    kx, n_out = (n, k) if trans_w else (k, n)
    dims = (((1,), (1,)), ((), ())) if trans_w else (((1,), (0,)), ((), ()))

    def body(*refs):
        if rms_bwd is not None:
            x_ref, w_ref, h_ref, wn_ref, dres_ref, o_ref, dw_ref = refs
        elif res is not None:
            x_ref, w_ref, res_ref, o_ref = refs
        else:
            x_ref, w_ref, o_ref = refs
        acc = None
        for s in range(n_sh):
            part = lax.dot_general(x_ref[:, s * kx:(s + 1) * kx].astype(BF16), w_ref[s], dims, preferred_element_type=F32)
            acc = part if acc is None else acc + part
        if rms_bwd is not None:
            dh, dw_rows = _rms_bwd(acc, h_ref[...], wn_ref[...])
            o_ref[...] = (dres_ref[...] + dh).astype(o_ref.dtype)

            @pl.when(pl.program_id(0) == 0)
            def _():
                dw_ref[...] = jnp.zeros_like(dw_ref)

            dw_ref[...] += _fold8(dw_rows)
        elif res is not None:
            o_ref[...] = (res_ref[...] + acc).astype(o_ref.dtype)
        else:
            o_ref[...] = acc.astype(o_ref.dtype)

    row = lambda i: (i, 0)
    in_specs = [pl.BlockSpec((tm, n_sh * kx), row), _w4_spec(w4)]
    args = [x, w4]
    out_specs = pl.BlockSpec((tm, n_out), row)
    out_shape = SDS((n_rows, n_out), out_dtype)
    if rms_bwd is not None:
        h, wn, dres = rms_bwd
        in_specs += [pl.BlockSpec((tm, n_out), row), pl.BlockSpec((1, n_out), lambda i: (0, 0)),
                     pl.BlockSpec((tm, n_out), row)]
        args += [h, wn, dres]
        out_specs = [out_specs, pl.BlockSpec((8, n_out), lambda i: (0, 0))]
        out_shape = [out_shape, SDS((8, n_out), F32)]
    elif res is not None:
        in_specs.append(pl.BlockSpec((tm, n_out), row))
        args.append(res)
    return pl.pallas_call(
        body, name=name, grid=(n_rows // tm,), in_specs=in_specs, out_specs=out_specs, out_shape=out_shape,
        compiler_params=_params(1),
    )(*args)


def _mm_tn(a, b, n_sh, a_sharded, name):
    n_rows = a.shape[0]
    tm = _row_tile(n_rows, TN_TILES)
    ka = a.shape[1] // n_sh if a_sharded else a.shape[1]
    nb = b.shape[1] if a_sharded else b.shape[1] // n_sh
    n_i = n_rows // tm

    def body(a_ref, b_ref, o_ref, acc):
        i = pl.program_id(0)

        @pl.when(i == 0)
        def _():
            acc[...] = jnp.zeros_like(acc)

        for s in range(n_sh):
            a_s = a_ref[:, s * ka:(s + 1) * ka] if a_sharded else a_ref[...]
            b_s = b_ref[...] if a_sharded else b_ref[:, s * nb:(s + 1) * nb]
            acc[s] += lax.dot_general(a_s.astype(BF16), b_s.astype(BF16), (((0,), (0,)), ((), ())),
                                      preferred_element_type=F32)

        @pl.when(i == n_i - 1)
        def _():
            o_ref[...] = acc[...].astype(o_ref.dtype)

    return pl.pallas_call(
        body, name=name, grid=(n_i,),
        in_specs=[pl.BlockSpec((tm, a.shape[1]), lambda i: (i, 0)), pl.BlockSpec((tm, b.shape[1]), lambda i: (i, 0))],
        out_specs=pl.BlockSpec((n_sh, ka, nb), lambda i: (0, 0, 0)),
        out_shape=SDS((n_sh, ka, nb), BF16),
        scratch_shapes=[pltpu.VMEM((n_sh, ka, nb), F32)], compiler_params=_params(1),
    )(a, b)


def _mlp_fwd(h, wn, w_up4, w_down4, layer, name):
    n_rows = h.shape[0]
    TM = _row_tile(n_rows, MM_TILES)
    n_sh = w_up4.shape[0]
    f_sh = D_FF // n_sh

    def body(h_ref, wn_ref, wu_ref, wd_ref, o_ref, a_ref, hn_ref, hn_s, acc):
        s = pl.program_id(1)

        @pl.when(s == 0)
        def _():
            hn = _rms(h_ref[...], wn_ref[...]).astype(BF16)
            hn_s[...] = hn
            hn_ref[...] = hn
            acc[...] = jnp.zeros_like(acc)

        a = jnp.dot(hn_s[...], wu_ref[...], preferred_element_type=F32)
        a_ref[...] = a.astype(BF16)
        act = jnp.maximum(a, 0.0)
        acc[...] += jnp.dot((act * act).astype(BF16), wd_ref[...], preferred_element_type=F32)

        @pl.when(s == n_sh - 1)
        def _():
            o_ref[...] = h_ref[...] + acc[...]

    row = lambda i, s: (i, 0)
    return pl.pallas_call(
        body, name=name, grid=(n_rows // TM, n_sh),
        in_specs=[pl.BlockSpec((TM, D_MODEL), row), pl.BlockSpec((1, D_MODEL), lambda i, s: (0, 0)),
                  pl.BlockSpec((None, None, D_MODEL, f_sh), lambda i, s: (s, layer, 0, 0)),
                  pl.BlockSpec((None, None, f_sh, D_MODEL), lambda i, s: (s, layer, 0, 0))],
        out_specs=[pl.BlockSpec((TM, D_MODEL), row), pl.BlockSpec((TM, f_sh), lambda i, s: (i, s)),
                   pl.BlockSpec((TM, D_MODEL), row)],
        out_shape=[SDS((n_rows, D_MODEL), F32), SDS((n_rows, D_FF), BF16), SDS((n_rows, D_MODEL), BF16)],
        scratch_shapes=[pltpu.VMEM((TM, D_MODEL), BF16), pltpu.VMEM((TM, D_MODEL), F32)],
        compiler_params=_params(2),
    )(h, wn, w_up4, w_down4)


def _mlp_bwd_shard(s, dh, a, hn, dhn_prev, h, wn, w_up4, w_down4, dw_up_buf, dw_down_buf, name):
    n_rows = dh.shape[0]
    n_sh = w_up4.shape[0]
    f_sh = D_FF // n_sh
    tm = _row_tile(n_rows, MLP_BWD_TILES)
    n_i = n_rows // tm
    last = h is not None
    nt = (((1,), (1,)), ((), ()))
    tn = (((0,), (0,)), ((), ()))

    def body(*refs):
        refs = list(refs)
        dh_ref, a_ref, hn_ref, wu_ref, wd_ref = refs[:5]
        at = 5
        prev_ref = None
        if dhn_prev is not None:
            prev_ref = refs[at]
            at += 1
        if last:
            h_ref, wn_ref = refs[at:at + 2]
            at += 2
        if dw_up_buf is not None:
            at += 2
        o_ref, dwu_ref, dwd_ref = refs[at:at + 3]
        at += 3
        if last:
            dnorm_ref = refs[at]
            at += 1
        acc_u, acc_d = refs[at:at + 2]
        i = pl.program_id(0)

        @pl.when(i == 0)
        def _():
            acc_u[...] = jnp.zeros_like(acc_u)
            acc_d[...] = jnp.zeros_like(acc_d)
            if last:
                dnorm_ref[...] = jnp.zeros_like(dnorm_ref)

        dh16 = dh_ref[...].astype(BF16)
        r = jnp.maximum(a_ref[...].astype(F32), 0.0)
        dact = lax.dot_general(dh16, wd_ref[...], nt, preferred_element_type=F32)
        da16 = (dact * (2.0 * r)).astype(BF16)
        acc_d[...] += lax.dot_general((r * r).astype(BF16), dh16, tn, preferred_element_type=F32)
        acc_u[...] += lax.dot_general(hn_ref[...], da16, tn, preferred_element_type=F32)
        dhn = lax.dot_general(da16, wu_ref[...], nt, preferred_element_type=F32)
        if prev_ref is not None:
            dhn = dhn + prev_ref[...]
        if last:
            d_rms, dw_rows = _rms_bwd(dhn, h_ref[...], wn_ref[...])
            o_ref[...] = dh_ref[...] + d_rms
            dnorm_ref[...] += _fold8(dw_rows)
        else:
            o_ref[...] = dhn

        @pl.when(i == n_i - 1)
        def _():
            dwu_ref[...] = acc_u[...].astype(BF16)
            dwd_ref[...] = acc_d[...].astype(BF16)

    row = lambda i: (i, 0)
    tile = pl.BlockSpec((tm, D_MODEL), row)
    in_specs = [tile, pl.BlockSpec((tm, f_sh), lambda i: (i, s)), tile,
                pl.BlockSpec((None, None, D_MODEL, f_sh), lambda i: (s, 0, 0, 0)),
                pl.BlockSpec((None, None, f_sh, D_MODEL), lambda i: (s, 0, 0, 0))]
    args = [dh, a, hn, w_up4, w_down4]
    if dhn_prev is not None:
        in_specs.append(tile)
        args.append(dhn_prev)
    if last:
        in_specs += [tile, pl.BlockSpec((1, D_MODEL), lambda i: (0, 0))]
        args += [h, wn]
    aliases = {}
    if dw_up_buf is not None:
        aliases = {len(args): 1, len(args) + 1: 2}
        in_specs += [pl.BlockSpec(memory_space=pl.ANY)] * 2
        args += [dw_up_buf, dw_down_buf]
    out_specs = [tile, pl.BlockSpec((None, D_MODEL, f_sh), lambda i: (s, 0, 0)),
                 pl.BlockSpec((None, f_sh, D_MODEL), lambda i: (s, 0, 0))]
    out_shape = [SDS((n_rows, D_MODEL), F32), SDS((n_sh, D_MODEL, f_sh), BF16), SDS((n_sh, f_sh, D_MODEL), BF16)]
    if last:
        out_specs.append(pl.BlockSpec((8, D_MODEL), lambda i: (0, 0)))
        out_shape.append(SDS((8, D_MODEL), F32))
    return pl.pallas_call(
        body, name=name, grid=(n_i,), in_specs=in_specs, out_specs=out_specs, out_shape=out_shape,
        input_output_aliases=aliases,
        scratch_shapes=[pltpu.VMEM((D_MODEL, f_sh), F32), pltpu.VMEM((f_sh, D_MODEL), F32)],
        compiler_params=_params(1),
    )(*args)


def _attn_masks(n):
    qi = lax.broadcasted_iota(jnp.int32, (BLOCK, 3 * BLOCK), 0)
    col = lax.broadcasted_iota(jnp.int32, (BLOCK, 3 * BLOCK), 1)
    kj = col - BLOCK
    dist = BLOCK + qi - kj
    kmin = jnp.where(n == 0, 2 * BLOCK, jnp.where(n == 1, BLOCK, 0))
    band_ok = (col >= BLOCK) & (dist >= 0) & (dist < BLOCK) & (kj >= kmin)
    q_pos = n * BLOCK + qi - PAD
    meta_ok = (col >= PAD) & (col < BLOCK) & (col - PAD <= q_pos)
    distf = jnp.where(col >= BLOCK, dist, 0).astype(F32)
    return band_ok | meta_ok, distf


def _alibi_slope(h):
    return float(2.0 ** (-8.0 * (h + 1) / N_HEADS))


def _attn_fwd(qkv, sinks, n_ex, nb):
    n_rows = qkv.shape[0]
    kvb = N_HEADS * HEAD_DIM // KV_DIM

    def body(sink_ref, q_ref, kvm_ref, kvp_ref, kvc_ref, o_ref, lse_ref, k_s, v_s, q_s):
        n = pl.program_id(1)
        ok, distf = _attn_masks(n)
        v_s[...] = jnp.ones_like(v_s)
        for part, ref in enumerate((kvm_ref, kvp_ref, kvc_ref)):
            rows = slice(part * BLOCK, (part + 1) * BLOCK)
            k_s[rows, :] = ref[:, 0:N_KV * HEAD_DIM]
            for kv in range(N_KV):
                v_s[rows, kv * 2 * HEAD_DIM:kv * 2 * HEAD_DIM + HEAD_DIM] = \
                    ref[:, (N_KV + kv) * HEAD_DIM:(N_KV + kv + 1) * HEAD_DIM]
        for kv in range(N_KV):
            for g in range(GQA):
                h = kv * GQA + g
                q_s[kv, g * BLOCK:(g + 1) * BLOCK, :] = q_ref[:, h * HEAD_DIM:(h + 1) * HEAD_DIM]
            s4 = lax.dot_general(q_s[kv], k_s[:, kv * HEAD_DIM:(kv + 1) * HEAD_DIM], (((1,), (1,)), ((), ())),
                                 preferred_element_type=F32) * (HEAD_DIM ** -0.5)
            es, ms, sink_es = [], [], []
            for g in range(GQA):
                h = kv * GQA + g
                s = jnp.where(ok, s4[g * BLOCK:(g + 1) * BLOCK] - _alibi_slope(h) * distf, NEG_INF)
                sink = sink_ref[0, h]
                m = jnp.maximum(jnp.max(s, axis=-1, keepdims=True), sink)
                es.append(jnp.exp(s - m).astype(BF16))
                ms.append(m)
                sink_es.append(jnp.exp(sink - m))
            pv = jnp.dot(jnp.concatenate(es, axis=0), v_s[:, kv * 2 * HEAD_DIM:(kv + 1) * 2 * HEAD_DIM],
                         preferred_element_type=F32)
            for g in range(GQA):
                h = kv * GQA + g
                pg = pv[g * BLOCK:(g + 1) * BLOCK]
                l = pg[:, HEAD_DIM:HEAD_DIM + 1] + sink_es[g]
                o_ref[:, h * HEAD_DIM:(h + 1) * HEAD_DIM] = (pg[:, 0:HEAD_DIM] / l).astype(BF16)
                lse_ref[:, h:h + 1] = ms[g] + jnp.log(l)

    return pl.pallas_call(
        body, name="attn_fwd", grid=(n_ex, nb),
        in_specs=[pl.BlockSpec(memory_space=pltpu.SMEM),
                  pl.BlockSpec((BLOCK, N_HEADS * HEAD_DIM), lambda b, n: (b * nb + n, 0)),
                  pl.BlockSpec((BLOCK, KV_DIM), lambda b, n: (b * nb, kvb)),
                  pl.BlockSpec((BLOCK, KV_DIM), lambda b, n: (b * nb + jnp.maximum(n - 1, 0), kvb)),
                  pl.BlockSpec((BLOCK, KV_DIM), lambda b, n: (b * nb + n, kvb))],
        out_specs=[pl.BlockSpec((BLOCK, N_HEADS * HEAD_DIM), lambda b, n: (b * nb + n, 0)),
                   pl.BlockSpec((BLOCK, N_HEADS), lambda b, n: (b * nb + n, 0))],
        out_shape=[SDS((n_rows, N_HEADS * HEAD_DIM), BF16), SDS((n_rows, N_HEADS), F32)],
        scratch_shapes=[pltpu.VMEM((3 * BLOCK, N_KV * HEAD_DIM), BF16), pltpu.VMEM((3 * BLOCK, 2 * N_KV * HEAD_DIM), BF16),
                        pltpu.VMEM((N_KV, GQA * BLOCK, HEAD_DIM), BF16)],
        compiler_params=_params(2),
    )(sinks, qkv, qkv, qkv, qkv)


def _attn_bwd(qkv, sinks, o, lse, do, n_ex, nb):
    n_rows = qkv.shape[0]
    kvb = N_HEADS * HEAD_DIM // KV_DIM
    scale = HEAD_DIM ** -0.5
    nq = lambda r: nb - 1 - r

    def body(sink_ref, q_ref, kvm_ref, kvp_ref, kvc_ref, o_ref, lse_ref, do_ref, dqkv_ref, dsink_ref,
             k_s, v_s, dkv_s, carry_s, meta_s, q_s, do_s):
        b, r = pl.program_id(0), pl.program_id(1)
        n = nq(r)
        ok, distf = _attn_masks(n)

        @pl.when((b == 0) & (r == 0))
        def _():
            dsink_ref[...] = jnp.zeros_like(dsink_ref)

        @pl.when(r == 0)
        def _():
            carry_s[...] = jnp.zeros_like(carry_s)
            meta_s[...] = jnp.zeros_like(meta_s)

        for part, ref in enumerate((kvm_ref, kvp_ref, kvc_ref)):
            k_s[part * BLOCK:(part + 1) * BLOCK, :] = ref[:, 0:N_KV * HEAD_DIM]
            v_s[part * BLOCK:(part + 1) * BLOCK, :] = ref[:, N_KV * HEAD_DIM:KV_DIM]
        nt = (((1,), (1,)), ((), ()))
        tn = (((0,), (0,)), ((), ()))
        for kv in range(N_KV):
            kcols = slice(kv * HEAD_DIM, (kv + 1) * HEAD_DIM)
            vcols = slice(N_KV * HEAD_DIM + kv * HEAD_DIM, N_KV * HEAD_DIM + (kv + 1) * HEAD_DIM)
            for g in range(GQA):
                cols = slice((kv * GQA + g) * HEAD_DIM, (kv * GQA + g + 1) * HEAD_DIM)
                q_s[kv, g * BLOCK:(g + 1) * BLOCK, :] = q_ref[:, cols]
                do_s[kv, g * BLOCK:(g + 1) * BLOCK, :] = do_ref[:, cols]
            kh, vh = k_s[:, kcols], v_s[:, kcols]
            s4 = lax.dot_general(q_s[kv], kh, nt, preferred_element_type=F32) * scale
            dp4 = lax.dot_general(do_s[kv], vh, nt, preferred_element_type=F32)
            ps, dss = [], []
            for g in range(GQA):
                h = kv * GQA + g
                cols = slice(h * HEAD_DIM, (h + 1) * HEAD_DIM)
                rows = slice(g * BLOCK, (g + 1) * BLOCK)
                s = jnp.where(ok, s4[rows] - _alibi_slope(h) * distf, NEG_INF)
                lse_h = lse_ref[:, h:h + 1]
                p = jnp.exp(s - lse_h)
                delta = jnp.sum(do_ref[:, cols].astype(F32) * o_ref[:, cols].astype(F32), axis=-1, keepdims=True)
                dsink_ref[:, h:h + 1] += -jnp.exp(sink_ref[0, h] - lse_h) * delta
                ps.append(p.astype(BF16))
                dss.append((p * (dp4[rows] - delta)).astype(BF16))
            p4, ds4 = jnp.concatenate(ps, axis=0), jnp.concatenate(dss, axis=0)
            dq4 = jnp.dot(ds4, kh, preferred_element_type=F32) * scale
            for g in range(GQA):
                cols = slice((kv * GQA + g) * HEAD_DIM, (kv * GQA + g + 1) * HEAD_DIM)
                dqkv_ref[:, cols] = dq4[g * BLOCK:(g + 1) * BLOCK].astype(BF16)
            dkv_s[:, kcols] = lax.dot_general(ds4, q_s[kv], tn, preferred_element_type=F32) * scale
            dkv_s[:, vcols] = lax.dot_general(p4, do_s[kv], tn, preferred_element_type=F32)

        meta_s[...] += dkv_s[0:BLOCK, :]
        cur = dkv_s[2 * BLOCK:3 * BLOCK, :] + carry_s[...]
        carry_s[...] = dkv_s[BLOCK:2 * BLOCK, :]

        @pl.when(n > 0)
        def _():
            dqkv_ref[:, N_HEADS * HEAD_DIM:QKV_DIM] = cur.astype(BF16)

        @pl.when(n == 0)
        def _():
            dqkv_ref[:, N_HEADS * HEAD_DIM:QKV_DIM] = (cur + meta_s[...]).astype(BF16)

    blk = lambda b, r: (b * nb + nq(r), 0)
    return pl.pallas_call(
        body, name="attn_bwd", grid=(n_ex, nb),
        in_specs=[pl.BlockSpec(memory_space=pltpu.SMEM),
                  pl.BlockSpec((BLOCK, N_HEADS * HEAD_DIM), blk),
                  pl.BlockSpec((BLOCK, KV_DIM), lambda b, r: (b * nb, kvb)),
                  pl.BlockSpec((BLOCK, KV_DIM), lambda b, r: (b * nb + jnp.maximum(nq(r) - 1, 0), kvb)),
                  pl.BlockSpec((BLOCK, KV_DIM), lambda b, r: (b * nb + nq(r), kvb)),
                  pl.BlockSpec((BLOCK, N_HEADS * HEAD_DIM), blk),
                  pl.BlockSpec((BLOCK, N_HEADS), blk),
                  pl.BlockSpec((BLOCK, N_HEADS * HEAD_DIM), blk)],
        out_specs=[pl.BlockSpec((BLOCK, QKV_DIM), blk),
                   pl.BlockSpec((BLOCK, N_HEADS), lambda b, r: (0, 0))],
        out_shape=[SDS((n_rows, QKV_DIM), BF16), SDS((BLOCK, N_HEADS), F32)],
        scratch_shapes=[pltpu.VMEM((3 * BLOCK, N_KV * HEAD_DIM), BF16), pltpu.VMEM((3 * BLOCK, N_KV * HEAD_DIM), BF16),
                        pltpu.VMEM((3 * BLOCK, KV_DIM), F32), pltpu.VMEM((BLOCK, KV_DIM), F32),
                        pltpu.VMEM((BLOCK, KV_DIM), F32), pltpu.VMEM((N_KV, GQA * BLOCK, HEAD_DIM), BF16),
                        pltpu.VMEM((N_KV, GQA * BLOCK, HEAD_DIM), BF16)],
        compiler_params=_params(2),
    )(sinks, qkv, qkv, qkv, qkv, o, lse, do)


def _cmul_add(xr, xi, mr, mi, sr, si):
    return xr + mr * sr - mi * si, xi + mr * si + mi * sr


def _scan_tiles(buf, tab_ref, carry_s, n_groups, reverse):
    shifts = (7, 6, 4) if reverse else (1, 2, 4)

    def group(gi, carry):
        g = (n_groups - 1 - gi) if reverse else gi
        row = pl.multiple_of(g * 8, 8)
        out = []
        for j in range(PAIRS_PER_CHUNK):
            xr = buf[j, pl.ds(row, 8), 0:128]
            xi = buf[j, pl.ds(row, 8), 128:256]
            for lvl, sh in enumerate(shifts):
                xr, xi = _cmul_add(xr, xi, tab_ref[j, 2 * lvl], tab_ref[j, 2 * lvl + 1],
                                   pltpu.roll(xr, sh, 0), pltpu.roll(xi, sh, 0))
            xr, xi = _cmul_add(xr, xi, tab_ref[j, 6], tab_ref[j, 7], carry[2 * j], carry[2 * j + 1])
            buf[j, pl.ds(row, 8), 0:128] = xr
            buf[j, pl.ds(row, 8), 128:256] = xi
            edge = slice(0, 1) if reverse else slice(7, 8)
            out += [jnp.broadcast_to(xr[edge], (8, 128)), jnp.broadcast_to(xi[edge], (8, 128))]
        return tuple(out)

    carry0 = tuple(carry_s[k] for k in range(2 * PAIRS_PER_CHUNK))
    carry = lax.fori_loop(0, n_groups, group, carry0)
    for k in range(2 * PAIRS_PER_CHUNK):
        carry_s[k] = carry[k]


def _ssm_fwd(u, b_pad, c_pad, tab, d_skip, n_ex, lp):
    n_rows = u.shape[0]
    n_t = lp // TM
    n_chunk = D_MODEL // 128

    def body(u_ref, bp_ref, cp_ref, tab_ref, d_ref, yg_ref, y_ref, xs_ref, buf, carry_s):
        @pl.when(pl.program_id(2) == 0)
        def _():
            carry_s[...] = jnp.zeros_like(carry_s)

        ub = u_ref[...]
        u16 = ub.astype(BF16)
        for j in range(PAIRS_PER_CHUNK):
            buf[j] = jnp.dot(u16, bp_ref[j], preferred_element_type=F32)
        _scan_tiles(buf, tab_ref, carry_s, TM // 8, reverse=False)
        y = d_ref[...] * ub
        for j in range(PAIRS_PER_CHUNK):
            xb = buf[j].astype(BF16)
            xs_ref[j] = xb
            y = y + jnp.dot(xb, cp_ref[j], preferred_element_type=F32)
        y_ref[...] = y
        yg_ref[...] = _gelu(y).astype(BF16)

    rows = lambda b, q, t: (b * n_t + t, q)
    return pl.pallas_call(
        body, name="ssm_fwd", grid=(n_ex, n_chunk, n_t),
        in_specs=[pl.BlockSpec((TM, 128), rows),
                  pl.BlockSpec((PAIRS_PER_CHUNK, 128, 256), lambda b, q, t: (q, 0, 0)),
                  pl.BlockSpec((PAIRS_PER_CHUNK, 256, 128), lambda b, q, t: (q, 0, 0)),
                  pl.BlockSpec((PAIRS_PER_CHUNK, 8, 8, 128), lambda b, q, t: (q, 0, 0, 0)),
                  pl.BlockSpec((1, 128), lambda b, q, t: (0, q))],
        out_specs=[pl.BlockSpec((TM, 128), rows), pl.BlockSpec((TM, 128), rows),
                   pl.BlockSpec((PAIRS_PER_CHUNK, TM, 256), lambda b, q, t: (q, b * n_t + t, 0))],
        out_shape=[SDS((n_rows, D_MODEL), BF16), SDS((n_rows, D_MODEL), F32), SDS((N_PAIR, n_rows, 256), BF16)],
        scratch_shapes=[pltpu.VMEM((PAIRS_PER_CHUNK, TM, 256), F32), pltpu.VMEM((2 * PAIRS_PER_CHUNK, 8, 128), F32)],
        compiler_params=_params(3),
    )(u, b_pad, c_pad, tab, d_skip)


def _ssm_bwd(dyg, y, u, xs, ct_pad, bt_pad, tab_rev, d_skip, n_ex, lp):
    n_rows = u.shape[0]
    n_t = lp // TM
    n_chunk = D_MODEL // 128
    tile = lambda q, b, t: (b * n_t + (n_t - 1 - t), q)

    def body(dyg_ref, y_ref, u_ref, xs_ref, xp_ref, ct_ref, bt_ref, tab_ref, d_ref,
             du_ref, db_ref, dc_ref, da_ref, dd_ref, buf, xf, carry_s):
        b, t = pl.program_id(1), pl.program_id(2)

        @pl.when((b == 0) & (t == 0))
        def _():
            db_ref[...] = jnp.zeros_like(db_ref)
            dc_ref[...] = jnp.zeros_like(dc_ref)
            da_ref[...] = jnp.zeros_like(da_ref)
            dd_ref[...] = jnp.zeros_like(dd_ref)

        @pl.when(t == 0)
        def _():
            carry_s[...] = jnp.zeros_like(carry_s)

        ub = u_ref[...]
        dy = dyg_ref[...].astype(F32) * _gelu_grad(y_ref[...])
        dd_ref[...] += _fold8(dy * ub)
        dy16 = dy.astype(BF16)
        first_tile = t == n_t - 1
        for j in range(PAIRS_PER_CHUNK):
            buf[j] = jnp.dot(dy16, ct_ref[j], preferred_element_type=F32)
            dc_ref[j] += lax.dot_general(dy16, xs_ref[j], (((0,), (0,)), ((), ())), preferred_element_type=F32)
            xf[j, 16:16 + TM, :] = xs_ref[j].astype(F32)
            xf[j, 0:16, :] = jnp.where(first_tile, 0.0, xp_ref[j].astype(F32))
        _scan_tiles(buf, tab_ref, carry_s, TM // 8, reverse=True)
        du = d_ref[...] * dy
        u16 = ub.astype(BF16)
        for j in range(PAIRS_PER_CHUNK):
            g = buf[j]
            g16 = g.astype(BF16)
            du = du + jnp.dot(g16, bt_ref[j], preferred_element_type=F32)
            db_ref[j] += lax.dot_general(u16, g16, (((0,), (0,)), ((), ())), preferred_element_type=F32)
            xprev = pltpu.roll(xf[j], 1, 0)[16:16 + TM, :]
            gr, gi = g[:, 0:128], g[:, 128:256]
            pr, pi = xprev[:, 0:128], xprev[:, 128:256]
            da_ref[j, 0] += _fold8(gr * pr + gi * pi)
            da_ref[j, 1] += _fold8(gi * pr - gr * pi)
        du_ref[...] = du

    prev16 = lambda q, b, t: (q, jnp.maximum((b * n_t + (n_t - 1 - t)) * (TM // 16) - 1, 0), 0)
    return pl.pallas_call(
        body, name="ssm_bwd", grid=(n_chunk, n_ex, n_t),
        in_specs=[pl.BlockSpec((TM, 128), tile), pl.BlockSpec((TM, 128), tile), pl.BlockSpec((TM, 128), tile),
                  pl.BlockSpec((PAIRS_PER_CHUNK, TM, 256), lambda q, b, t: (q, b * n_t + (n_t - 1 - t), 0)),
                  pl.BlockSpec((PAIRS_PER_CHUNK, 16, 256), prev16),
                  pl.BlockSpec((PAIRS_PER_CHUNK, 128, 256), lambda q, b, t: (q, 0, 0)),
                  pl.BlockSpec((PAIRS_PER_CHUNK, 256, 128), lambda q, b, t: (q, 0, 0)),
                  pl.BlockSpec((PAIRS_PER_CHUNK, 8, 8, 128), lambda q, b, t: (q, 0, 0, 0)),
                  pl.BlockSpec((1, 128), lambda q, b, t: (0, q))],
        out_specs=[pl.BlockSpec((TM, 128), tile),
                   pl.BlockSpec((PAIRS_PER_CHUNK, 128, 256), lambda q, b, t: (q, 0, 0)),
                   pl.BlockSpec((PAIRS_PER_CHUNK, 128, 256), lambda q, b, t: (q, 0, 0)),
                   pl.BlockSpec((PAIRS_PER_CHUNK, 2, 8, 128), lambda q, b, t: (q, 0, 0, 0)),
                   pl.BlockSpec((8, 128), lambda q, b, t: (0, q))],
        out_shape=[SDS((n_rows, D_MODEL), F32), SDS((N_PAIR, 128, 256), F32), SDS((N_PAIR, 128, 256), F32),
                   SDS((N_PAIR, 2, 8, 128), F32), SDS((8, D_MODEL), F32)],
        scratch_shapes=[pltpu.VMEM((PAIRS_PER_CHUNK, TM, 256), F32), pltpu.VMEM((PAIRS_PER_CHUNK, TM + 16, 256), F32),
                        pltpu.VMEM((2 * PAIRS_PER_CHUNK, 8, 128), F32)],
        compiler_params=_params(3),
    )(dyg, y, u, xs, xs, ct_pad, bt_pad, tab_rev, d_skip)


def _rms_fwd(h, wn, name):
    n_rows = h.shape[0]

    def body(h_ref, wn_ref, o_ref):
        o_ref[...] = _rms(h_ref[...], wn_ref[...])

    return pl.pallas_call(
        body, name=name, grid=(n_rows // TM,),
        in_specs=[pl.BlockSpec((TM, D_MODEL), lambda i: (i, 0)), pl.BlockSpec((1, D_MODEL), lambda i: (0, 0))],
        out_specs=pl.BlockSpec((TM, D_MODEL), lambda i: (i, 0)),
        out_shape=SDS((n_rows, D_MODEL), F32), compiler_params=_params(1),
    )(h, wn)


def _rms_bwd_call(dhn, h, wn, dres, name):
    n_rows = h.shape[0]

    def body(dhn_ref, h_ref, wn_ref, dres_ref, o_ref, dw_ref):
        @pl.when(pl.program_id(0) == 0)
        def _():
            dw_ref[...] = jnp.zeros_like(dw_ref)

        dh, dw_rows = _rms_bwd(dhn_ref[...], h_ref[...], wn_ref[...])
        o_ref[...] = dres_ref[...] + dh
        dw_ref[...] += _fold8(dw_rows)

    row = lambda i: (i, 0)
    return pl.pallas_call(
        body, name=name, grid=(n_rows // TM,),
        in_specs=[pl.BlockSpec((TM, D_MODEL), row), pl.BlockSpec((TM, D_MODEL), row),
                  pl.BlockSpec((1, D_MODEL), lambda i: (0, 0)), pl.BlockSpec((TM, D_MODEL), row)],
        out_specs=[pl.BlockSpec((TM, D_MODEL), row), pl.BlockSpec((8, D_MODEL), lambda i: (0, 0))],
        out_shape=[SDS((n_rows, D_MODEL), F32), SDS((8, D_MODEL), F32)], compiler_params=_params(1),
    )(dhn, h, wn, dres)


def _glu_fwd(h, z):
    n_rows = h.shape[0]

    def body(h_ref, val_ref, gate_ref, o_ref):
        o_ref[...] = h_ref[...] + val_ref[...].astype(F32) * jax.nn.sigmoid(gate_ref[...].astype(F32))

    row = lambda i: (i, 0)
    return pl.pallas_call(
        body, name="glu_fwd", grid=(n_rows // TM,),
        in_specs=[pl.BlockSpec((TM, D_MODEL), row), pl.BlockSpec((TM, D_MODEL), row),
                  pl.BlockSpec((TM, D_MODEL), lambda i: (i, 1))],
        out_specs=pl.BlockSpec((TM, D_MODEL), row),
        out_shape=SDS((n_rows, D_MODEL), F32), compiler_params=_params(1),
    )(h, z, z)


def _glu_bwd(dh, z):
    n_rows = dh.shape[0]

    def body(dh_ref, val_ref, gate_ref, dz_ref):
        sg = jax.nn.sigmoid(gate_ref[...].astype(F32))
        d = dh_ref[...]
        dz_ref[:, 0:D_MODEL] = (d * sg).astype(BF16)
        dz_ref[:, D_MODEL:2 * D_MODEL] = (d * val_ref[...].astype(F32) * sg * (1.0 - sg)).astype(BF16)

    row = lambda i: (i, 0)
    return pl.pallas_call(
        body, name="glu_bwd", grid=(n_rows // TM,),
        in_specs=[pl.BlockSpec((TM, D_MODEL), row), pl.BlockSpec((TM, D_MODEL), row),
                  pl.BlockSpec((TM, D_MODEL), lambda i: (i, 1))],
        out_specs=pl.BlockSpec((TM, 2 * D_MODEL), row),
        out_shape=SDS((n_rows, 2 * D_MODEL), BF16), compiler_params=_params(1),
    )(dh, z, z)


def _loss_head(h, wn, target, n_ex, nb):
    n_rows = h.shape[0]

    def body(h_ref, wn_ref, t_ref, dh_ref, loss_ref, dw_ref):
        b, n = pl.program_id(0), pl.program_id(1)

        @pl.when((b == 0) & (n == 0))
        def _():
            loss_ref[...] = jnp.zeros_like(loss_ref)
            dw_ref[...] = jnp.zeros_like(dw_ref)

        @pl.when(n == 0)
        def _():
            dh_ref[...] = jnp.zeros_like(dh_ref)

        @pl.when(n > 0)
        def _():
            hh = h_ref[...]
            diff = _rms(hh, wn_ref[...]) - t_ref[...]
            loss_ref[...] += 0.5 * jnp.sum(diff * diff) * (1.0 / D_MODEL)
            dh, dw_rows = _rms_bwd(diff * (1.0 / D_MODEL), hh, wn_ref[...])
            dh_ref[...] = dh
            dw_ref[...] += _fold8(dw_rows)

    return pl.pallas_call(
        body, name="loss_head", grid=(n_ex, nb),
        in_specs=[pl.BlockSpec((BLOCK, D_MODEL), lambda b, n: (b * nb + n, 0)),
                  pl.BlockSpec((1, D_MODEL), lambda b, n: (0, 0)),
                  pl.BlockSpec((BLOCK, D_MODEL), lambda b, n: (b * (nb - 1) + jnp.maximum(n - 1, 0), 0))],
        out_specs=[pl.BlockSpec((BLOCK, D_MODEL), lambda b, n: (b * nb + n, 0)),
                   pl.BlockSpec((8, 128), lambda b, n: (0, 0)),
                   pl.BlockSpec((8, D_MODEL), lambda b, n: (0, 0))],
        out_shape=[SDS((n_rows, D_MODEL), F32), SDS((8, 128), F32), SDS((8, D_MODEL), F32)],
        compiler_params=_params(2),
    )(h, wn, target)


def _adamw(pieces, w, m, v, name):
    n_layers = len(pieces)
    rows, cols = pieces[0].shape[1:]
    rb = rows
    for cand in (256, 136, 128, 64, 32, 16, 8):
        if rows % cand == 0 and rows > cand:
            rb = cand
            break
    n_blk = rows // rb
    c1 = 1.0 / (1.0 - ADAM_B1 ** ADAM_STEP)
    c2 = 1.0 / (1.0 - ADAM_B2 ** ADAM_STEP)

    def body(*refs):
        p_refs = refs[:n_layers]
        w_ref, m_ref, v_ref, g_out, d_out, m_out, v_out = refs[n_layers:]
        layer = pl.program_id(0)
        g = None
        for l, p_ref in enumerate(p_refs):
            gl = p_ref[0].astype(F32)
            for k in range(1, N_DEV):
                gl = gl + p_ref[k].astype(F32)
            g = gl if g is None else jnp.where(layer == l, gl, g)
        m_new = ADAM_B1 * m_ref[...] + (1.0 - ADAM_B1) * g
        v_new = ADAM_B2 * v_ref[...] + (1.0 - ADAM_B2) * (g * g)
        g_out[...] = g
        m_out[...] = m_new
        v_out[...] = v_new
        d_out[...] = -ADAM_LR * ((m_new * c1) / (jnp.sqrt(v_new * c2) + ADAM_EPS) + ADAM_WD * w_ref[...])

    def piece_spec(l):
        return pl.BlockSpec((N_DEV, rb, cols), lambda ly, i: (0, jnp.where(ly == l, i, 0), 0))

    blk = pl.BlockSpec((rb, cols), lambda ly, i: (ly * n_blk + i, 0))
    return pl.pallas_call(
        body, name=name, grid=(n_layers, n_blk),
        in_specs=[piece_spec(l) for l in range(n_layers)] + [blk, blk, blk],
        out_specs=[blk, blk, blk, blk],
        out_shape=[SDS((n_layers * rows, cols), F32)] * 4, compiler_params=_params(2),
    )(*pieces, w, m, v)


_HBM = pl.BlockSpec(memory_space=pltpu.HBM)
_SEM = pl.BlockSpec(memory_space=pltpu.SEMAPHORE)
_EFFECT = pltpu.SideEffectType.DATAFLOW_SIDE_EFFECTING
N_GATHER_PEERS = N_CHIPS - 1
N_EXCHANGE_PEERS = N_DEV - 1


def _gather_copies(srcs, lands, send_sems, recv_sems):
    x, y, c = lax.axis_index("x"), lax.axis_index("y"), lax.axis_index("c")
    mine = 2 * x + y
    chips = [(1 - x, y), (x, 1 - y), (1 - x, 1 - y)]
    out, inc = [], []
    for a in range(len(srcs)):
        for k, (px, py) in enumerate(chips):
            j = a * N_GATHER_PEERS + k
            sems = dict(send_sem=send_sems.at[j], recv_sem=recv_sems.at[j], device_id=(px, py, c),
                        device_id_type=pl.DeviceIdType.MESH)
            out.append(pltpu.make_async_remote_copy(src_ref=srcs[a], dst_ref=lands[a].at[mine], **sems))
            inc.append(pltpu.make_async_remote_copy(src_ref=srcs[a], dst_ref=lands[a].at[2 * px + py], **sems))
    return out, inc


def _exchange_copies(n_scatter):
    def copies(srcs, lands, send_sems, recv_sems):
        x, y, c = lax.axis_index("x"), lax.axis_index("y"), lax.axis_index("c")
        me = 4 * x + 2 * y + c
        peers = [(x ^ (k >> 2), y ^ ((k >> 1) & 1), c ^ (k & 1)) for k in range(1, N_DEV)]
        out, inc = [], []
        for a in range(len(srcs)):
            for k, (px, py, pc) in enumerate(peers):
                j = a * N_EXCHANGE_PEERS + k
                sems = dict(send_sem=send_sems.at[j], recv_sem=recv_sems.at[j], device_id=(px, py, pc),
                            device_id_type=pl.DeviceIdType.MESH)
                theirs = srcs[a].at[2 * px + py] if a < n_scatter else srcs[a]
                mine = srcs[a].at[2 * x + y] if a < n_scatter else srcs[a]
                out.append(pltpu.make_async_remote_copy(src_ref=theirs, dst_ref=lands[a].at[me], **sems))
                inc.append(pltpu.make_async_remote_copy(src_ref=mine, dst_ref=lands[a].at[4 * px + 2 * py + pc], **sems))
        return out, inc

    return copies


def _split_start(groups, copies_fn, n_peers, name):
    sizes = [len(srcs) for srcs, _ in groups]
    flat = [a for srcs, lands in groups for a in list(srcs) + list(lands)]
    n_flat, n_grp = len(flat), len(groups)

    def body(*refs):
        sems = refs[2 * n_flat:2 * n_flat + 2 * n_grp]
        token = refs[-1]
        at = 0
        for gi, n in enumerate(sizes):
            out, _ = copies_fn(refs[at:at + n], refs[at + n:at + 2 * n], sems[2 * gi], sems[2 * gi + 1])
            for cp in out:
                cp.start()
            at += 2 * n
        token[...] = jnp.zeros_like(token)

    sem_shapes = []
    for n in sizes:
        sem_shapes += [pltpu.SemaphoreType.DMA((n * n_peers,)), pltpu.SemaphoreType.DMA((n * n_peers,))]
    res = pl.pallas_call(
        body, name=name,
        out_shape=(*[pltpu.HBM(a.shape, a.dtype) for a in flat], *sem_shapes, SDS((8, 128), F32)),
        in_specs=[_HBM] * n_flat,
        out_specs=(*[_HBM] * n_flat, *[_SEM] * (2 * n_grp), pl.BlockSpec(memory_space=pltpu.VMEM)),
        input_output_aliases={i: i for i in range(n_flat)},
        compiler_params=pltpu.CompilerParams(has_side_effects=_EFFECT),
    )(*[pltpu.with_memory_space_constraint(a, pltpu.HBM) for a in flat])
    handles, at = [], 0
    for gi, n in enumerate(sizes):
        handles.append((res[n_flat + 2 * gi], res[n_flat + 2 * gi + 1], list(res[at:at + n]), list(res[at + n:at + 2 * n])))
        at += 2 * n
    return handles, res[-1]


def _split_wait(handle, after, copies_fn, name):
    send_sems, recv_sems, srcs, lands = handle
    n = len(srcs)
    after = list(after) if isinstance(after, (list, tuple)) else [after]

    def body(*refs):
        out, inc = copies_fn(refs[:n], refs[n:2 * n], refs[2 * n], refs[2 * n + 1])
        for cp in out:
            cp.wait_send()
        for cp in inc:
            cp.wait_recv()

    flat = list(srcs) + list(lands)
    res = pl.pallas_call(
        body, name=name,
        out_shape=tuple(pltpu.HBM(a.shape, a.dtype) for a in flat),
        in_specs=[_HBM] * (2 * n) + [_SEM, _SEM] + [pl.BlockSpec(memory_space=pl.ANY)] * len(after),
        out_specs=tuple([_HBM] * (2 * n)),
        input_output_aliases={i: i for i in range(2 * n)},
        compiler_params=pltpu.CompilerParams(has_side_effects=_EFFECT),
    )(*flat, send_sems, recv_sems, *after)
    return list(res[n:])


def _landing(own, slot, n_slots):
    return lax.dynamic_update_index_in_dim(lax.empty((n_slots,) + own.shape, own.dtype), own, slot, 0)


def _ssm_discretize(lam_re, lam_im, log_dt, b_re, b_im):
    lr = jnp.minimum(lam_re, LAMBDA_RE_MAX)
    li = lam_im
    dt = jnp.exp(log_dt)[:, None]
    mag = jnp.exp(lr * dt)
    ar, ai = mag * jnp.cos(li * dt), mag * jnp.sin(li * dt)
    den = lr * lr + li * li
    nr, ni = ar - 1.0, ai
    gr, gi = (nr * lr + ni * li) / den, (ni * lr - nr * li) / den
    bbr = gr[:, :, None] * b_re - gi[:, :, None] * b_im
    bbi = gr[:, :, None] * b_im + gi[:, :, None] * b_re
    return ar, ai, bbr, bbi


def _pair_lanes(t):
    return t.reshape(N_PAIR, 2 * SSM_STATE)


def _chan_state_blocks(t_gcp):
    t = t_gcp.reshape(N_PAIR, 2, SSM_GROUP, SSM_STATE)
    eye2 = jnp.eye(2, dtype=t.dtype)
    blk = jnp.einsum("rgcp,gh->rgchp", t, eye2).reshape(N_PAIR, 2 * SSM_GROUP, 2 * SSM_STATE)
    place = jax.nn.one_hot(jnp.arange(N_PAIR) % PAIRS_PER_CHUNK, PAIRS_PER_CHUNK, dtype=t.dtype)
    return jnp.einsum("rcl,rj->rjcl", blk, place).reshape(N_PAIR, 128, 2 * SSM_STATE)


def _chan_state_unblock(t):
    t = t.reshape(N_PAIR, PAIRS_PER_CHUNK, 2, SSM_GROUP, 2, SSM_STATE)
    place = jax.nn.one_hot(jnp.arange(N_PAIR) % PAIRS_PER_CHUNK, PAIRS_PER_CHUNK, dtype=t.dtype)
    t = jnp.einsum("rjgchp,rj->rgchp", t, place)
    t = jnp.einsum("rgchp,gh->rgcp", t, jnp.eye(2, dtype=t.dtype))
    return t.reshape(SSM_NG, SSM_GROUP, SSM_STATE)


def _scan_tables(ar, ai, reverse):
    ar, ai = _pair_lanes(ar), _pair_lanes(ai)
    if reverse:
        ai = -ai
    pows = [(ar, ai)]
    for _ in range(7):
        pr, pi = pows[-1]
        pows.append((pr * ar - pi * ai, pr * ai + pi * ar))
    rows = jnp.arange(8)[None, :, None]
    tiles = []
    for k in (1, 2, 4):
        keep = (rows <= 7 - k) if reverse else (rows >= k)
        for part in pows[k - 1]:
            tiles.append(jnp.where(keep, part[:, None, :], 0.0))
    order = list(range(7, -1, -1)) if reverse else list(range(8))
    for comp in (0, 1):
        tiles.append(jnp.stack([pows[e][comp] for e in order], axis=1))
    return jnp.stack(tiles, axis=1)


def _ssm_operands(w):
    ar, ai, bbr, bbi = _ssm_discretize(w["ssm_lambda_re"], w["ssm_lambda_im"], w["ssm_log_dt"], w["ssm_b_re"], w["ssm_b_im"])
    b_blk = jnp.concatenate([_chan_state_blocks(jnp.swapaxes(bbr, 1, 2)), _chan_state_blocks(jnp.swapaxes(bbi, 1, 2))], axis=2)
    c_blk = jnp.concatenate([_chan_state_blocks(w["ssm_c_re"]), -_chan_state_blocks(w["ssm_c_im"])], axis=2)
    return (b_blk.astype(BF16), jnp.swapaxes(b_blk, 1, 2).astype(BF16), c_blk.astype(BF16),
            jnp.swapaxes(c_blk, 1, 2).astype(BF16), _scan_tables(ar, ai, False), _scan_tables(ar, ai, True))


def _local_step(x, target, w, late_weights, on_grads):
    n_ex, seq, _ = x.shape
    lp = seq + BLOCK
    nb = lp // BLOCK
    n_rows = n_ex * lp
    g = {}

    head = jnp.concatenate([jnp.zeros((PAD, D_MODEL), F32), w["meta_tokens"]], axis=0)
    h0 = jnp.concatenate([jnp.broadcast_to(head[None], (n_ex, BLOCK, D_MODEL)), x], axis=1).reshape(n_rows, D_MODEL)

    qkv, hn_a = _rms_mm_cols(h0, w["attn_norm_w"], w["attn_w_qkv"], "qkv_fwd")
    att, lse = _attn_fwd(qkv, w["attn_sinks"], n_ex, nb)
    h1 = _mm_acc(att, w["attn_w_o"], False, "attn_out_fwd", res=h0)
    w = {**w, **late_weights(0, att)}
    h2, a0, hn_m0 = _mlp_fwd(h1, w["mlp_norm_w"][0:1], w["mlp_w_up"][0], w["mlp_w_down"][0], 0, "mlp0_fwd")
    late = late_weights(1, h2)
    w["ssm_w_glu"] = late["ssm_w_glu"]
    w["mlp_w_up"], w["mlp_w_down"] = w["mlp_w_up"] + late["mlp_w_up"], w["mlp_w_down"] + late["mlp_w_down"]

    b_pad, bt_pad, ct_pad, c_pad, tab_fwd, tab_rev = w["ssm_operands"] if "ssm_operands" in w else _ssm_operands(w)
    u = _rms_fwd(h2, w["ssm_norm_w"], "ssm_norm_fwd")
    yg, y, xs = _ssm_fwd(u, b_pad, c_pad, tab_fwd, w["ssm_d"], n_ex, lp)
    z = _mm_cols(yg, w["ssm_w_glu"], False, "glu_mm_fwd")
    h3 = _glu_fwd(h2, z)
    h4, a1, hn_m1 = _mlp_fwd(h3, w["mlp_norm_w"][1:2], w["mlp_w_up"][1], w["mlp_w_down"][1], 0, "mlp1_fwd")

    dh4, loss_tile, dnorm_f = _loss_head(h4, w["final_norm_w"], target.reshape(n_ex * seq, D_MODEL), n_ex, nb)

    def mlp_bwd(dh_out, h_in, a, hn, layer, tag, norm_w):
        dhn, dw_up, dw_down = None, None, None
        for s in range(N_CHIPS):
            final = s == N_CHIPS - 1
            res = _mlp_bwd_shard(s, dh_out, a, hn, dhn, h_in if final else None, norm_w,
                                 w["mlp_w_up"][layer], w["mlp_w_down"][layer], dw_up, dw_down, f"{tag}_bwd{s}")
            dhn, dw_up, dw_down = res[:3]
        return dhn, res[3], dw_up, dw_down

    dh3, dnorm_m1, dwu1, dwd1 = mlp_bwd(dh4, h3, a1, hn_m1, 1, "mlp1", w["mlp_norm_w"][1:2])
    tok = on_grads("mlp1", {"mlp_w_up": dwu1, "mlp_w_down": dwd1})
    dz = _glu_bwd(dh3, z)
    dyg = _mm_acc(dz, w["ssm_w_glu"], True, "glu_mm_dx", out_dtype=BF16)
    g["ssm_w_glu"] = _mm_tn(yg, dz, N_CHIPS, False, "glu_mm_dw")
    du, db_blk, dc_blk, da_t, dd_t = _ssm_bwd(dyg, y, u, xs, ct_pad, bt_pad, tab_rev, w["ssm_d"] + tok, n_ex, lp)
    dh2, dnorm_s = _rms_bwd_call(du, h2, w["ssm_norm_w"], dh3, "ssm_norm_bwd")
    g["ssm_c_re"] = _chan_state_unblock(dc_blk[:, :, 0:128])
    g["ssm_c_im"] = -_chan_state_unblock(dc_blk[:, :, 128:256])
    g_bbr = jnp.swapaxes(_chan_state_unblock(db_blk[:, :, 0:128]), 1, 2)
    g_bbi = jnp.swapaxes(_chan_state_unblock(db_blk[:, :, 128:256]), 1, 2)
    g_a = jnp.sum(da_t, axis=2).reshape(N_PAIR, 2, 2, SSM_STATE)
    g_ar, g_ai = g_a[:, 0].reshape(SSM_NG, SSM_STATE), g_a[:, 1].reshape(SSM_NG, SSM_STATE)
    _, vjp = jax.vjp(_ssm_discretize, w["ssm_lambda_re"], w["ssm_lambda_im"], w["ssm_log_dt"], w["ssm_b_re"], w["ssm_b_im"])
    g["ssm_lambda_re"], g["ssm_lambda_im"], g["ssm_log_dt"], g["ssm_b_re"], g["ssm_b_im"] = vjp((g_ar, g_ai, g_bbr, g_bbi))
    tok = on_grads("ssm", g)
    g = {}
    dh1, dnorm_m0, dwu0, dwd0 = mlp_bwd(dh2, h1, a0, hn_m0, 0, "mlp0", w["mlp_norm_w"][0:1] + tok)
    datt = _mm_cols(dh1, w["attn_w_o"], True, "attn_out_dx")
    dw_o = _mm_tn(att, dh1, N_CHIPS, True, "attn_out_dw")
    tok = on_grads("mlp0", {"mlp_w_up": dwu0, "mlp_w_down": dwd0, "attn_w_o": dw_o})
    dqkv, dsink_rows = _attn_bwd(qkv, w["attn_sinks"] + tok, att, lse, datt, n_ex, nb)
    tok = on_grads("qkv", {"attn_w_qkv": _mm_tn(hn_a, dqkv, N_CHIPS, False, "qkv_dw")})
    dh0, dnorm_a = _mm_acc(dqkv, w["attn_w_qkv"], True, "qkv_dx", rms_bwd=(h0, w["attn_norm_w"] + tok, dh1))

    dh0 = dh0.reshape(n_ex, lp, D_MODEL)
    on_grads("rest", {
        "mlp_norm_w": jnp.stack([jnp.sum(dnorm_m0, axis=0), jnp.sum(dnorm_m1, axis=0)]),
        "final_norm_w": jnp.sum(dnorm_f, axis=0),
        "attn_norm_w": jnp.sum(dnorm_a, axis=0)[None],
        "ssm_norm_w": jnp.sum(dnorm_s, axis=0)[None],
        "attn_sinks": jnp.sum(dsink_rows, axis=0)[None],
        "ssm_d": jnp.sum(dd_t, axis=0)[None],
        "meta_tokens": jnp.sum(dh0[:, PAD:BLOCK], axis=0)})
    return loss_tile, dh0[:, BLOCK:]


_SHARDED_SMALL = ("meta_tokens", "ssm_norm_w", "ssm_d")
_REP_SSM = ("ssm_lambda_re", "ssm_lambda_im", "ssm_log_dt", "ssm_b_re", "ssm_b_im", "ssm_c_re", "ssm_c_im")
_REP_MISC = ("attn_norm_w", "attn_sinks", "mlp_norm_w", "final_norm_w")
_BIG = ("attn_w_qkv", "attn_w_o", "ssm_w_glu", "mlp_w_up", "mlp_w_down")


def _pack(parts, cols):
    flat = jnp.concatenate([p.reshape(-1) for p in parts])
    rows = -(-flat.shape[0] // (8 * cols)) * 8
    return jnp.pad(flat, (0, rows * cols - flat.shape[0])).reshape(rows, cols)


def _unpack(packed, like):
    flat = packed.reshape(-1)
    out, at = [], 0
    for p in like:
        out.append(flat[at:at + p.size].reshape(p.shape))
        at += p.size
    return out


def kernel(x, meta_tokens, attn_norm_w, attn_w_qkv, attn_sinks, attn_w_o, ssm_norm_w, ssm_lambda_re, ssm_lambda_im, ssm_log_dt, ssm_b_re, ssm_b_im, ssm_c_re, ssm_c_im, ssm_d, ssm_w_glu, mlp_norm_w, mlp_w_up, mlp_w_down, final_norm_w, loss_target, m_meta_tokens, m_attn_norm_w, m_attn_w_qkv, m_attn_sinks, m_attn_w_o, m_ssm_norm_w, m_ssm_lambda_re, m_ssm_lambda_im, m_ssm_log_dt, m_ssm_b_re, m_ssm_b_im, m_ssm_c_re, m_ssm_c_im, m_ssm_d, m_ssm_w_glu, m_mlp_norm_w, m_mlp_w_up, m_mlp_w_down, m_final_norm_w, v_meta_tokens, v_attn_norm_w, v_attn_w_qkv, v_attn_sinks, v_attn_w_o, v_ssm_norm_w, v_ssm_lambda_re, v_ssm_lambda_im, v_ssm_log_dt, v_ssm_b_re, v_ssm_b_im, v_ssm_c_re, v_ssm_c_im, v_ssm_d, v_ssm_w_glu, v_mlp_norm_w, v_mlp_w_up, v_mlp_w_down, v_final_norm_w):
    names = ("meta_tokens", "attn_norm_w", "attn_w_qkv", "attn_sinks", "attn_w_o", "ssm_norm_w", "ssm_lambda_re",
             "ssm_lambda_im", "ssm_log_dt", "ssm_b_re", "ssm_b_im", "ssm_c_re", "ssm_c_im", "ssm_d", "ssm_w_glu",
             "mlp_norm_w", "mlp_w_up", "mlp_w_down", "final_norm_w")
    wts = dict(zip(names, (meta_tokens, attn_norm_w, attn_w_qkv, attn_sinks, attn_w_o, ssm_norm_w, ssm_lambda_re,
                           ssm_lambda_im, ssm_log_dt, ssm_b_re, ssm_b_im, ssm_c_re, ssm_c_im, ssm_d, ssm_w_glu,
                           mlp_norm_w, mlp_w_up, mlp_w_down, final_norm_w)))
    mom = dict(zip(names, (m_meta_tokens, m_attn_norm_w, m_attn_w_qkv, m_attn_sinks, m_attn_w_o, m_ssm_norm_w,
                           m_ssm_lambda_re, m_ssm_lambda_im, m_ssm_log_dt, m_ssm_b_re, m_ssm_b_im, m_ssm_c_re,
                           m_ssm_c_im, m_ssm_d, m_ssm_w_glu, m_mlp_norm_w, m_mlp_w_up, m_mlp_w_down, m_final_norm_w)))
    var = dict(zip(names, (v_meta_tokens, v_attn_norm_w, v_attn_w_qkv, v_attn_sinks, v_attn_w_o, v_ssm_norm_w,
                           v_ssm_lambda_re, v_ssm_lambda_im, v_ssm_log_dt, v_ssm_b_re, v_ssm_b_im, v_ssm_c_re,
                           v_ssm_c_im, v_ssm_d, v_ssm_w_glu, v_mlp_norm_w, v_mlp_w_up, v_mlp_w_down, v_final_norm_w)))

    my_chip = 2 * lax.axis_index("x") + lax.axis_index("y")
    my_dev = 2 * my_chip + lax.axis_index("c")
    small_mine = _pack([wts[n] for n in _SHARDED_SMALL], 128)
    first = [attn_w_qkv.astype(BF16), attn_w_o.astype(BF16), small_mine]
    up16, down16 = mlp_w_up.astype(BF16), mlp_w_down.astype(BF16)
    mlp0 = [up16[0:1], down16[0:1]]
    rest = [ssm_w_glu.astype(BF16), up16[1:2], down16[1:2]]
    handles, _ = _split_start([(srcs, [_landing(a, my_chip, N_CHIPS) for a in srcs]) for srcs in (first, mlp0, rest)],
                              _gather_copies, N_GATHER_PEERS, "gather_start")
    full = {n: wts[n] for n in _REP_MISC}
    full["final_norm_w"] = final_norm_w[None]
    for n in _REP_SSM:
        full[n] = wts[n][0]
    full["ssm_operands"] = _ssm_operands(full)
    got = _split_wait(handles[0], full["ssm_operands"], _gather_copies, "gather_wait_first")
    full["attn_w_qkv"], full["attn_w_o"] = got[0], got[1]
    smalls = [_unpack(got[2][s], [wts[n] for n in _SHARDED_SMALL]) for s in range(N_CHIPS)]
    for k, n in enumerate(_SHARDED_SMALL):
        full[n] = jnp.concatenate([smalls[s][k] for s in range(N_CHIPS)], axis=1)

    def late_weights(stage, after):
        if stage == 0:
            up, down = _split_wait(handles[1], after, _gather_copies, "gather_wait_mlp0")
            return {"mlp_w_up": [up], "mlp_w_down": [down]}
        glu, up, down = _split_wait(handles[2], after, _gather_copies, "gather_wait_rest")
        return {"ssm_w_glu": glu, "mlp_w_up": [up], "mlp_w_down": [down]}

    def shard_cols(t):
        return jnp.swapaxes(t.reshape(t.shape[0], N_CHIPS, t.shape[1] // N_CHIPS), 0, 1)

    pending = {}

    def on_grads(tag, g):
        scatter = [g[n] for n in _BIG if n in g]
        whole = []
        if tag == "ssm":
            whole = [_pack([g[n] for n in _REP_SSM], D_MODEL)]
        if tag == "rest":
            parts = [shard_cols(g[n]) for n in _SHARDED_SMALL]
            scatter = [jnp.stack([_pack([p[s] for p in parts], 128) for s in range(N_CHIPS)])]
            whole = [_pack([g[n] for n in _REP_MISC], D_MODEL)]
        srcs = scatter + whole
        lands = [_landing(lax.dynamic_index_in_dim(a, my_chip, 0, keepdims=False), my_dev, N_DEV) for a in scatter]
        lands += [_landing(a, my_dev, N_DEV) for a in whole]
        hs, token = _split_start([(srcs, lands)], _exchange_copies(len(scatter)), N_EXCHANGE_PEERS, "exchange_start_" + tag)
        pending[tag] = (hs[0], len(scatter))
        return token[0, 0]

    loss_tile, grad_x = _local_step(x, loss_target, full, late_weights, on_grads)
    loss = lax.psum(loss_tile[0, 0], ("x", "y", "c"))

    recv = {}
    for tag, (handle, n_scatter) in pending.items():
        recv[tag] = _split_wait(handle, grad_x, _exchange_copies(n_scatter), "exchange_wait_" + tag)

    out = {}

    def update(tag, pieces, w2, m2, v2):
        return _adamw(pieces, w2, m2, v2, "adamw_" + tag)

    def update_weight(n, pieces):
        shp = wts[n].shape
        r2 = (math.prod(shp[:-1]), shp[-1])
        res = update(n, pieces, wts[n].reshape(r2), mom[n].reshape(r2), var[n].reshape(r2))
        out[n] = [t.reshape(shp) for t in res]

    update_weight("mlp_w_up", [recv["mlp0"][1], recv["mlp1"][0]])
    update_weight("mlp_w_down", [recv["mlp0"][2], recv["mlp1"][1]])
    update_weight("attn_w_o", [recv["mlp0"][0]])
    update_weight("ssm_w_glu", [recv["ssm"][0]])
    update_weight("attn_w_qkv", [recv["qkv"][0]])
    for tag, group, pieces, cols in (("small", _SHARDED_SMALL, recv["rest"][0], 128),
                                     ("rep_ssm", _REP_SSM, recv["ssm"][1], D_MODEL),
                                     ("rep_misc", _REP_MISC, recv["rest"][1], D_MODEL)):
        like = [wts[n] for n in group]
        res = update(tag, [pieces], _pack(like, cols), _pack([mom[n] for n in group], cols),
                     _pack([var[n] for n in group], cols))
        for k, n in enumerate(group):
            out[n] = [_unpack(t, like)[k] for t in res]

    return (loss, grad_x, *[out[n][0] for n in names], *[out[n][1] for n in names],
            *[out[n][2] for n in names], *[out[n][3] for n in names])
```

```python
import functools
import math

import jax
import jax.numpy as jnp
from jax import lax
from jax.experimental import pallas as pl
from jax.experimental.pallas import tpu as pltpu

F32 = jnp.float32
BF16 = jnp.bfloat16
SDS = jax.ShapeDtypeStruct

D_MODEL = 1024
N_HEADS = 16
N_KV = 4
GQA = N_HEADS // N_KV
HEAD_DIM = 64
BLOCK = 128
N_META = 16
PAD = BLOCK - N_META
QKV_DIM = (N_HEADS + 2 * N_KV) * HEAD_DIM
KV_DIM = 2 * N_KV * HEAD_DIM
D_FF = 4 * D_MODEL
N_CHIPS = 4
N_DEV = 8
SSM_GROUP = 16
SSM_NG = D_MODEL // SSM_GROUP
SSM_STATE = 64
N_PAIR = SSM_NG // 2
PAIRS_PER_CHUNK = 4
RMS_EPS = 1e-6
NEG_INF = -1e30
LAMBDA_RE_MAX = -1e-4
ADAM_LR, ADAM_B1, ADAM_B2, ADAM_EPS, ADAM_WD, ADAM_STEP = 0.001, 0.9, 0.999, 1e-08, 0.01, 10

TM = 384
MM_TILES = (768, 384)
MLP_BWD_TILES = (768, 384)
TN_TILES = (1408, 768, 384)
VMEM_LIMIT = 56 * 1024 * 1024


def _params(n_grid):
    return pltpu.CompilerParams(dimension_semantics=("arbitrary",) * n_grid, vmem_limit_bytes=VMEM_LIMIT)


def _row_tile(n_rows, tiles):
    return next(t for t in tiles if n_rows % t == 0)


def _rms(h, w):
    r = lax.rsqrt(jnp.mean(h * h, axis=-1, keepdims=True) + RMS_EPS)
    return h * r * w


def _rms_bwd(dhn, h, w):
    r = lax.rsqrt(jnp.mean(h * h, axis=-1, keepdims=True) + RMS_EPS)
    g = dhn * w
    proj = jnp.sum(g * h, axis=-1, keepdims=True) * (1.0 / D_MODEL)
    return r * g - h * (r * r * r) * proj, dhn * h * r


def _fold8(t):
    return jnp.sum(t.reshape(t.shape[0] // 8, 8, t.shape[1]), axis=0)


def _gelu(y):
    return 0.5 * y * (1.0 + jnp.tanh(0.7978845608028654 * (y + 0.044715 * y * y * y)))


def _gelu_grad(y):
    t = jnp.tanh(0.7978845608028654 * (y + 0.044715 * y * y * y))
    return 0.5 * (1.0 + t) + 0.5 * y * (1.0 - t * t) * 0.7978845608028654 * (1.0 + 3.0 * 0.044715 * y * y)


def _w4_spec(w4):
    n_sh, _, k, n = w4.shape
    return pl.BlockSpec((n_sh, None, k, n), lambda i: (0, 0, 0, 0))


def _rms_mm_cols(h, wn, w4, name):
    n_rows = h.shape[0]
    n_sh, _, k, n = w4.shape
    tm = _row_tile(n_rows, MM_TILES)

    def body(h_ref, wn_ref, w_ref, o_ref, hn_ref):
        hn = _rms(h_ref[...], wn_ref[...]).astype(BF16)
        hn_ref[...] = hn
        for s in range(n_sh):
            o_ref[:, s * n:(s + 1) * n] = jnp.dot(hn, w_ref[s], preferred_element_type=F32).astype(o_ref.dtype)

    return pl.pallas_call(
        body, name=name, grid=(n_rows // tm,),
        in_specs=[pl.BlockSpec((tm, k), lambda i: (i, 0)), pl.BlockSpec((1, k), lambda i: (0, 0)), _w4_spec(w4)],
        out_specs=[pl.BlockSpec((tm, n_sh * n), lambda i: (i, 0)), pl.BlockSpec((tm, k), lambda i: (i, 0))],
        out_shape=[SDS((n_rows, n_sh * n), BF16), SDS((n_rows, k), BF16)],
        compiler_params=_params(1),
    )(h, wn, w4)


def _mm_cols(x, w4, trans_w, name):
    n_rows, kx = x.shape
    tm = _row_tile(n_rows, MM_TILES)
    n_sh, _, k, n = w4.shape
    n_out = k if trans_w else n
    dims = (((1,), (1,)), ((), ())) if trans_w else (((1,), (0,)), ((), ()))

    def body(x_ref, w_ref, o_ref):
        x16 = x_ref[...].astype(BF16)
        for s in range(n_sh):
            o_ref[:, s * n_out:(s + 1) * n_out] = lax.dot_general(
                x16, w_ref[s], dims, preferred_element_type=F32).astype(o_ref.dtype)

    return pl.pallas_call(
        body, name=name, grid=(n_rows // tm,),
        in_specs=[pl.BlockSpec((tm, kx), lambda i: (i, 0)), _w4_spec(w4)],
        out_specs=pl.BlockSpec((tm, n_sh * n_out), lambda i: (i, 0)),
        out_shape=SDS((n_rows, n_sh * n_out), BF16),
        compiler_params=_params(1),
    )(x, w4)


def _mm_acc(x, w4, trans_w, name, res=None, rms_bwd=None, out_dtype=F32):
    n_rows = x.shape[0]
    tm = _row_tile(n_rows, MM_TILES)
    n_sh, _, k, n = w4.shape
    kx, n_out = (n, k) if trans_w else (k, n)
    dims = (((1,), (1,)), ((), ())) if trans_w else (((1,), (0,)), ((), ()))

    def body(*refs):
        if rms_bwd is not None:
            x_ref, w_ref, h_ref, wn_ref, dres_ref, o_ref, dw_ref = refs
        elif res is not None:
            x_ref, w_ref, res_ref, o_ref = refs
        else:
            x_ref, w_ref, o_ref = refs
        acc = None
        for s in range(n_sh):
            part = lax.dot_general(x_ref[:, s * kx:(s + 1) * kx].astype(BF16), w_ref[s], dims, preferred_element_type=F32)
            acc = part if acc is None else acc + part
        if rms_bwd is not None:
            dh, dw_rows = _rms_bwd(acc, h_ref[...], wn_ref[...])
            o_ref[...] = (dres_ref[...] + dh).astype(o_ref.dtype)

            @pl.when(pl.program_id(0) == 0)
            def _():
                dw_ref[...] = jnp.zeros_like(dw_ref)

            dw_ref[...] += _fold8(dw_rows)
        elif res is not None:
            o_ref[...] = (res_ref[...] + acc).astype(o_ref.dtype)
        else:
            o_ref[...] = acc.astype(o_ref.dtype)

    row = lambda i: (i, 0)
    in_specs = [pl.BlockSpec((tm, n_sh * kx), row), _w4_spec(w4)]
    args = [x, w4]
    out_specs = pl.BlockSpec((tm, n_out), row)
    out_shape = SDS((n_rows, n_out), out_dtype)
    if rms_bwd is not None:
        h, wn, dres = rms_bwd
        in_specs += [pl.BlockSpec((tm, n_out), row), pl.BlockSpec((1, n_out), lambda i: (0, 0)),
                     pl.BlockSpec((tm, n_out), row)]
        args += [h, wn, dres]
        out_specs = [out_specs, pl.BlockSpec((8, n_out), lambda i: (0, 0))]
        out_shape = [out_shape, SDS((8, n_out), F32)]
    elif res is not None:
        in_specs.append(pl.BlockSpec((tm, n_out), row))
        args.append(res)
    return pl.pallas_call(
        body, name=name, grid=(n_rows // tm,), in_specs=in_specs, out_specs=out_specs, out_shape=out_shape,
        compiler_params=_params(1),
    )(*args)


def _mm_tn(a, b, n_sh, a_sharded, name):
    n_rows = a.shape[0]
    tm = _row_tile(n_rows, TN_TILES)
    ka = a.shape[1] // n_sh if a_sharded else a.shape[1]
    nb = b.shape[1] if a_sharded else b.shape[1] // n_sh
    n_i = n_rows // tm

    def body(a_ref, b_ref, o_ref, acc):
        i = pl.program_id(0)

        @pl.when(i == 0)
        def _():
            acc[...] = jnp.zeros_like(acc)

        for s in range(n_sh):
            a_s = a_ref[:, s * ka:(s + 1) * ka] if a_sharded else a_ref[...]
            b_s = b_ref[...] if a_sharded else b_ref[:, s * nb:(s + 1) * nb]
            acc[s] += lax.dot_general(a_s.astype(BF16), b_s.astype(BF16), (((0,), (0,)), ((), ())),
                                      preferred_element_type=F32)

        @pl.when(i == n_i - 1)
        def _():
            o_ref[...] = acc[...].astype(o_ref.dtype)

    return pl.pallas_call(
        body, name=name, grid=(n_i,),
        in_specs=[pl.BlockSpec((tm, a.shape[1]), lambda i: (i, 0)), pl.BlockSpec((tm, b.shape[1]), lambda i: (i, 0))],
        out_specs=pl.BlockSpec((n_sh, ka, nb), lambda i: (0, 0, 0)),
        out_shape=SDS((n_sh, ka, nb), BF16),
        scratch_shapes=[pltpu.VMEM((n_sh, ka, nb), F32)], compiler_params=_params(1),
    )(a, b)


def _mlp_fwd(h, wn, w_up4, w_down4, layer, name):
    n_rows = h.shape[0]
    TM = _row_tile(n_rows, MM_TILES)
    n_sh = w_up4.shape[0]
    f_sh = D_FF // n_sh

    def body(h_ref, wn_ref, wu_ref, wd_ref, o_ref, a_ref, hn_ref, hn_s, acc):
        s = pl.program_id(1)

        @pl.when(s == 0)
        def _():
            hn = _rms(h_ref[...], wn_ref[...]).astype(BF16)
            hn_s[...] = hn
            hn_ref[...] = hn
            acc[...] = jnp.zeros_like(acc)

        a = jnp.dot(hn_s[...], wu_ref[...], preferred_element_type=F32)
        a_ref[...] = a.astype(BF16)
        act = jnp.maximum(a, 0.0)
        acc[...] += jnp.dot((act * act).astype(BF16), wd_ref[...], preferred_element_type=F32)

        @pl.when(s == n_sh - 1)
        def _():
            o_ref[...] = h_ref[...] + acc[...]

    row = lambda i, s: (i, 0)
    return pl.pallas_call(
        body, name=name, grid=(n_rows // TM, n_sh),
        in_specs=[pl.BlockSpec((TM, D_MODEL), row), pl.BlockSpec((1, D_MODEL), lambda i, s: (0, 0)),
                  pl.BlockSpec((None, None, D_MODEL, f_sh), lambda i, s: (s, layer, 0, 0)),
                  pl.BlockSpec((None, None, f_sh, D_MODEL), lambda i, s: (s, layer, 0, 0))],
        out_specs=[pl.BlockSpec((TM, D_MODEL), row), pl.BlockSpec((TM, f_sh), lambda i, s: (i, s)),
                   pl.BlockSpec((TM, D_MODEL), row)],
        out_shape=[SDS((n_rows, D_MODEL), F32), SDS((n_rows, D_FF), BF16), SDS((n_rows, D_MODEL), BF16)],
        scratch_shapes=[pltpu.VMEM((TM, D_MODEL), BF16), pltpu.VMEM((TM, D_MODEL), F32)],
        compiler_params=_params(2),
    )(h, wn, w_up4, w_down4)


def _mlp_bwd_shard(s, dh, a, hn, dhn_prev, h, wn, w_up4, w_down4, dw_up_buf, dw_down_buf, name):
    n_rows = dh.shape[0]
    n_sh = w_up4.shape[0]
    f_sh = D_FF // n_sh
    tm = _row_tile(n_rows, MLP_BWD_TILES)
    n_i = n_rows // tm
    last = h is not None
    nt = (((1,), (1,)), ((), ()))
    tn = (((0,), (0,)), ((), ()))

    def body(*refs):
        refs = list(refs)
        dh_ref, a_ref, hn_ref, wu_ref, wd_ref = refs[:5]
        at = 5
        prev_ref = None
        if dhn_prev is not None:
            prev_ref = refs[at]
            at += 1
        if last:
            h_ref, wn_ref = refs[at:at + 2]
            at += 2
        if dw_up_buf is not None:
            at += 2
        o_ref, dwu_ref, dwd_ref = refs[at:at + 3]
        at += 3
        if last:
            dnorm_ref = refs[at]
            at += 1
        acc_u, acc_d = refs[at:at + 2]
        i = pl.program_id(0)

        @pl.when(i == 0)
        def _():
            acc_u[...] = jnp.zeros_like(acc_u)
            acc_d[...] = jnp.zeros_like(acc_d)
            if last:
                dnorm_ref[...] = jnp.zeros_like(dnorm_ref)

        dh16 = dh_ref[...].astype(BF16)
        r = jnp.maximum(a_ref[...].astype(F32), 0.0)
        dact = lax.dot_general(dh16, wd_ref[...], nt, preferred_element_type=F32)
        da16 = (dact * (2.0 * r)).astype(BF16)
        acc_d[...] += lax.dot_general((r * r).astype(BF16), dh16, tn, preferred_element_type=F32)
        acc_u[...] += lax.dot_general(hn_ref[...], da16, tn, preferred_element_type=F32)
        dhn = lax.dot_general(da16, wu_ref[...], nt, preferred_element_type=F32)
        if prev_ref is not None:
            dhn = dhn + prev_ref[...]
        if last:
            d_rms, dw_rows = _rms_bwd(dhn, h_ref[...], wn_ref[...])
            o_ref[...] = dh_ref[...] + d_rms
            dnorm_ref[...] += _fold8(dw_rows)
        else:
            o_ref[...] = dhn

        @pl.when(i == n_i - 1)
        def _():
            dwu_ref[...] = acc_u[...].astype(BF16)
            dwd_ref[...] = acc_d[...].astype(BF16)

    row = lambda i: (i, 0)
    tile = pl.BlockSpec((tm, D_MODEL), row)
    in_specs = [tile, pl.BlockSpec((tm, f_sh), lambda i: (i, s)), tile,
                pl.BlockSpec((None, None, D_MODEL, f_sh), lambda i: (s, 0, 0, 0)),
                pl.BlockSpec((None, None, f_sh, D_MODEL), lambda i: (s, 0, 0, 0))]
    args = [dh, a, hn, w_up4, w_down4]
    if dhn_prev is not None:
        in_specs.append(tile)
        args.append(dhn_prev)
    if last:
        in_specs += [tile, pl.BlockSpec((1, D_MODEL), lambda i: (0, 0))]
        args += [h, wn]
    aliases = {}
    if dw_up_buf is not None:
        aliases = {len(args): 1, len(args) + 1: 2}
        in_specs += [pl.BlockSpec(memory_space=pl.ANY)] * 2
        args += [dw_up_buf, dw_down_buf]
    out_specs = [tile, pl.BlockSpec((None, D_MODEL, f_sh), lambda i: (s, 0, 0)),
                 pl.BlockSpec((None, f_sh, D_MODEL), lambda i: (s, 0, 0))]
    out_shape = [SDS((n_rows, D_MODEL), F32), SDS((n_sh, D_MODEL, f_sh), BF16), SDS((n_sh, f_sh, D_MODEL), BF16)]
    if last:
        out_specs.append(pl.BlockSpec((8, D_MODEL), lambda i: (0, 0)))
        out_shape.append(SDS((8, D_MODEL), F32))
    return pl.pallas_call(
        body, name=name, grid=(n_i,), in_specs=in_specs, out_specs=out_specs, out_shape=out_shape,
        input_output_aliases=aliases,
        scratch_shapes=[pltpu.VMEM((D_MODEL, f_sh), F32), pltpu.VMEM((f_sh, D_MODEL), F32)],
        compiler_params=_params(1),
    )(*args)


def _attn_masks(n):
    qi = lax.broadcasted_iota(jnp.int32, (BLOCK, 3 * BLOCK), 0)
    col = lax.broadcasted_iota(jnp.int32, (BLOCK, 3 * BLOCK), 1)
    kj = col - BLOCK
    dist = BLOCK + qi - kj
    kmin = jnp.where(n == 0, 2 * BLOCK, jnp.where(n == 1, BLOCK, 0))
    band_ok = (col >= BLOCK) & (dist >= 0) & (dist < BLOCK) & (kj >= kmin)
    q_pos = n * BLOCK + qi - PAD
    meta_ok = (col >= PAD) & (col < BLOCK) & (col - PAD <= q_pos)
    distf = jnp.where(col >= BLOCK, dist, 0).astype(F32)
    return band_ok | meta_ok, distf


def _alibi_slope(h):
    return float(2.0 ** (-8.0 * (h + 1) / N_HEADS))


def _attn_fwd(qkv, sinks, n_ex, nb):
    n_rows = qkv.shape[0]
    kvb = N_HEADS * HEAD_DIM // KV_DIM

    def body(sink_ref, q_ref, kvm_ref, kvp_ref, kvc_ref, o_ref, lse_ref, k_s, v_s, q_s):
        n = pl.program_id(1)
        ok, distf = _attn_masks(n)
        v_s[...] = jnp.ones_like(v_s)
        for part, ref in enumerate((kvm_ref, kvp_ref, kvc_ref)):
            rows = slice(part * BLOCK, (part + 1) * BLOCK)
            k_s[rows, :] = ref[:, 0:N_KV * HEAD_DIM]
            for kv in range(N_KV):
                v_s[rows, kv * 2 * HEAD_DIM:kv * 2 * HEAD_DIM + HEAD_DIM] = \
                    ref[:, (N_KV + kv) * HEAD_DIM:(N_KV + kv + 1) * HEAD_DIM]
        for kv in range(N_KV):
            for g in range(GQA):
                h = kv * GQA + g
                q_s[kv, g * BLOCK:(g + 1) * BLOCK, :] = q_ref[:, h * HEAD_DIM:(h + 1) * HEAD_DIM]
            s4 = lax.dot_general(q_s[kv], k_s[:, kv * HEAD_DIM:(kv + 1) * HEAD_DIM], (((1,), (1,)), ((), ())),
                                 preferred_element_type=F32) * (HEAD_DIM ** -0.5)
            es, ms, sink_es = [], [], []
            for g in range(GQA):
                h = kv * GQA + g
                s = jnp.where(ok, s4[g * BLOCK:(g + 1) * BLOCK] - _alibi_slope(h) * distf, NEG_INF)
                sink = sink_ref[0, h]
                m = jnp.maximum(jnp.max(s, axis=-1, keepdims=True), sink)
                es.append(jnp.exp(s - m).astype(BF16))
                ms.append(m)
                sink_es.append(jnp.exp(sink - m))
            pv = jnp.dot(jnp.concatenate(es, axis=0), v_s[:, kv * 2 * HEAD_DIM:(kv + 1) * 2 * HEAD_DIM],
                         preferred_element_type=F32)
            for g in range(GQA):
                h = kv * GQA + g
                pg = pv[g * BLOCK:(g + 1) * BLOCK]
                l = pg[:, HEAD_DIM:HEAD_DIM + 1] + sink_es[g]
                o_ref[:, h * HEAD_DIM:(h + 1) * HEAD_DIM] = (pg[:, 0:HEAD_DIM] / l).astype(BF16)
                lse_ref[:, h:h + 1] = ms[g] + jnp.log(l)

    return pl.pallas_call(
        body, name="attn_fwd", grid=(n_ex, nb),
        in_specs=[pl.BlockSpec(memory_space=pltpu.SMEM),
                  pl.BlockSpec((BLOCK, N_HEADS * HEAD_DIM), lambda b, n: (b * nb + n, 0)),
                  pl.BlockSpec((BLOCK, KV_DIM), lambda b, n: (b * nb, kvb)),
                  pl.BlockSpec((BLOCK, KV_DIM), lambda b, n: (b * nb + jnp.maximum(n - 1, 0), kvb)),
                  pl.BlockSpec((BLOCK, KV_DIM), lambda b, n: (b * nb + n, kvb))],
        out_specs=[pl.BlockSpec((BLOCK, N_HEADS * HEAD_DIM), lambda b, n: (b * nb + n, 0)),
                   pl.BlockSpec((BLOCK, N_HEADS), lambda b, n: (b * nb + n, 0))],
        out_shape=[SDS((n_rows, N_HEADS * HEAD_DIM), BF16), SDS((n_rows, N_HEADS), F32)],
        scratch_shapes=[pltpu.VMEM((3 * BLOCK, N_KV * HEAD_DIM), BF16), pltpu.VMEM((3 * BLOCK, 2 * N_KV * HEAD_DIM), BF16),
                        pltpu.VMEM((N_KV, GQA * BLOCK, HEAD_DIM), BF16)],
        compiler_params=_params(2),
    )(sinks, qkv, qkv, qkv, qkv)


def _attn_bwd(qkv, sinks, o, lse, do, n_ex, nb):
    n_rows = qkv.shape[0]
    kvb = N_HEADS * HEAD_DIM // KV_DIM
    scale = HEAD_DIM ** -0.5
    nq = lambda r: nb - 1 - r

    def body(sink_ref, q_ref, kvm_ref, kvp_ref, kvc_ref, o_ref, lse_ref, do_ref, dqkv_ref, dsink_ref,
             k_s, v_s, dkv_s, carry_s, meta_s, q_s, do_s):
        b, r = pl.program_id(0), pl.program_id(1)
        n = nq(r)
        ok, distf = _attn_masks(n)

        @pl.when((b == 0) & (r == 0))
        def _():
            dsink_ref[...] = jnp.zeros_like(dsink_ref)

        @pl.when(r == 0)
        def _():
            carry_s[...] = jnp.zeros_like(carry_s)
            meta_s[...] = jnp.zeros_like(meta_s)

        for part, ref in enumerate((kvm_ref, kvp_ref, kvc_ref)):
            k_s[part * BLOCK:(part + 1) * BLOCK, :] = ref[:, 0:N_KV * HEAD_DIM]
            v_s[part * BLOCK:(part + 1) * BLOCK, :] = ref[:, N_KV * HEAD_DIM:KV_DIM]
        nt = (((1,), (1,)), ((), ()))
        tn = (((0,), (0,)), ((), ()))
        for kv in range(N_KV):
            kcols = slice(kv * HEAD_DIM, (kv + 1) * HEAD_DIM)
            vcols = slice(N_KV * HEAD_DIM + kv * HEAD_DIM, N_KV * HEAD_DIM + (kv + 1) * HEAD_DIM)
            for g in range(GQA):
                cols = slice((kv * GQA + g) * HEAD_DIM, (kv * GQA + g + 1) * HEAD_DIM)
                q_s[kv, g * BLOCK:(g + 1) * BLOCK, :] = q_ref[:, cols]
                do_s[kv, g * BLOCK:(g + 1) * BLOCK, :] = do_ref[:, cols]
            kh, vh = k_s[:, kcols], v_s[:, kcols]
            s4 = lax.dot_general(q_s[kv], kh, nt, preferred_element_type=F32) * scale
            dp4 = lax.dot_general(do_s[kv], vh, nt, preferred_element_type=F32)
            ps, dss = [], []
            for g in range(GQA):
                h = kv * GQA + g
                cols = slice(h * HEAD_DIM, (h + 1) * HEAD_DIM)
                rows = slice(g * BLOCK, (g + 1) * BLOCK)
                s = jnp.where(ok, s4[rows] - _alibi_slope(h) * distf, NEG_INF)
                lse_h = lse_ref[:, h:h + 1]
                p = jnp.exp(s - lse_h)
                delta = jnp.sum(do_ref[:, cols].astype(F32) * o_ref[:, cols].astype(F32), axis=-1, keepdims=True)
                dsink_ref[:, h:h + 1] += -jnp.exp(sink_ref[0, h] - lse_h) * delta
                ps.append(p.astype(BF16))
                dss.append((p * (dp4[rows] - delta)).astype(BF16))
            p4, ds4 = jnp.concatenate(ps, axis=0), jnp.concatenate(dss, axis=0)
            dq4 = jnp.dot(ds4, kh, preferred_element_type=F32) * scale
            for g in range(GQA):
                cols = slice((kv * GQA + g) * HEAD_DIM, (kv * GQA + g + 1) * HEAD_DIM)
                dqkv_ref[:, cols] = dq4[g * BLOCK:(g + 1) * BLOCK].astype(BF16)
            dkv_s[:, kcols] = lax.dot_general(ds4, q_s[kv], tn, preferred_element_type=F32) * scale
            dkv_s[:, vcols] = lax.dot_general(p4, do_s[kv], tn, preferred_element_type=F32)

        meta_s[...] += dkv_s[0:BLOCK, :]
        cur = dkv_s[2 * BLOCK:3 * BLOCK, :] + carry_s[...]
        carry_s[...] = dkv_s[BLOCK:2 * BLOCK, :]

        @pl.when(n > 0)
        def _():
            dqkv_ref[:, N_HEADS * HEAD_DIM:QKV_DIM] = cur.astype(BF16)

        @pl.when(n == 0)
        def _():
            dqkv_ref[:, N_HEADS * HEAD_DIM:QKV_DIM] = (cur + meta_s[...]).astype(BF16)

    blk = lambda b, r: (b * nb + nq(r), 0)
    return pl.pallas_call(
        body, name="attn_bwd", grid=(n_ex, nb),
        in_specs=[pl.BlockSpec(memory_space=pltpu.SMEM),
                  pl.BlockSpec((BLOCK, N_HEADS * HEAD_DIM), blk),
                  pl.BlockSpec((BLOCK, KV_DIM), lambda b, r: (b * nb, kvb)),
                  pl.BlockSpec((BLOCK, KV_DIM), lambda b, r: (b * nb + jnp.maximum(nq(r) - 1, 0), kvb)),
                  pl.BlockSpec((BLOCK, KV_DIM), lambda b, r: (b * nb + nq(r), kvb)),
                  pl.BlockSpec((BLOCK, N_HEADS * HEAD_DIM), blk),
                  pl.BlockSpec((BLOCK, N_HEADS), blk),
                  pl.BlockSpec((BLOCK, N_HEADS * HEAD_DIM), blk)],
        out_specs=[pl.BlockSpec((BLOCK, QKV_DIM), blk),
                   pl.BlockSpec((BLOCK, N_HEADS), lambda b, r: (0, 0))],
        out_shape=[SDS((n_rows, QKV_DIM), BF16), SDS((BLOCK, N_HEADS), F32)],
        scratch_shapes=[pltpu.VMEM((3 * BLOCK, N_KV * HEAD_DIM), BF16), pltpu.VMEM((3 * BLOCK, N_KV * HEAD_DIM), BF16),
                        pltpu.VMEM((3 * BLOCK, KV_DIM), F32), pltpu.VMEM((BLOCK, KV_DIM), F32),
                        pltpu.VMEM((BLOCK, KV_DIM), F32), pltpu.VMEM((N_KV, GQA * BLOCK, HEAD_DIM), BF16),
                        pltpu.VMEM((N_KV, GQA * BLOCK, HEAD_DIM), BF16)],
        compiler_params=_params(2),
    )(sinks, qkv, qkv, qkv, qkv, o, lse, do)


SEG = TM // 8


def _cmul_add(xr, xi, mr, mi, sr, si):
    return xr + mr * sr - mi * si, xi + mr * si + mi * sr


def _to_segments(src_ref, dst):
    for s in range(SEG):
        dst[s * 8:(s + 1) * 8, :] = src_ref[pl.ds(s, 8, stride=SEG), :]


def _from_segments(src, i):
    return src[pl.ds(i, SEG, stride=8), :]


def _scan_segments(buf, tab_ref, pw_ref, carry_s, reverse):
    shifts = (7, 6, 4) if reverse else (1, 2, 4)
    row_id = lax.broadcasted_iota(jnp.int32, (8, 128), 0)

    def local(si, prev):
        s = (SEG - 1 - si) if reverse else si
        row = pl.multiple_of(s * 8, 8)
        out = []
        for j in range(PAIRS_PER_CHUNK):
            xr, xi = _cmul_add(buf[j, pl.ds(row, 8), 0:128], buf[j, pl.ds(row, 8), 128:256],
                               tab_ref[j, 0], tab_ref[j, 1], prev[2 * j], prev[2 * j + 1])
            buf[j, pl.ds(row, 8), 0:128] = xr
            buf[j, pl.ds(row, 8), 128:256] = xi
            out += [xr, xi]
        return tuple(out)

    zero = jnp.zeros((8, 128), F32)
    edge = lax.fori_loop(0, SEG, local, (zero,) * (2 * PAIRS_PER_CHUNK))

    entering = []
    for j in range(PAIRS_PER_CHUNK):
        er, ei = edge[2 * j], edge[2 * j + 1]
        if reverse:
            sr = jnp.where(row_id == 7, carry_s[2 * j], pltpu.roll(er, 7, 0))
            si_ = jnp.where(row_id == 7, carry_s[2 * j + 1], pltpu.roll(ei, 7, 0))
        else:
            sr = jnp.where(row_id == 0, carry_s[2 * j], pltpu.roll(er, 1, 0))
            si_ = jnp.where(row_id == 0, carry_s[2 * j + 1], pltpu.roll(ei, 1, 0))
        for lvl, sh in enumerate(shifts):
            sr, si_ = _cmul_add(sr, si_, tab_ref[j, 2 + 2 * lvl], tab_ref[j, 3 + 2 * lvl],
                                pltpu.roll(sr, sh, 0), pltpu.roll(si_, sh, 0))
        entering += [sr, si_]
        tr, ti = _cmul_add(er, ei, tab_ref[j, 2], tab_ref[j, 3], sr, si_)
        out_row = slice(0, 1) if reverse else slice(7, 8)
        carry_s[2 * j] = jnp.broadcast_to(tr[out_row], (8, 128))
        carry_s[2 * j + 1] = jnp.broadcast_to(ti[out_row], (8, 128))

    def fix(s, _):
        row = pl.multiple_of(s * 8, 8)
        for j in range(PAIRS_PER_CHUNK):
            xr, xi = _cmul_add(buf[j, pl.ds(row, 8), 0:128], buf[j, pl.ds(row, 8), 128:256],
                               pw_ref[j, 0, s], pw_ref[j, 1, s], entering[2 * j], entering[2 * j + 1])
            buf[j, pl.ds(row, 8), 0:128] = xr
            buf[j, pl.ds(row, 8), 128:256] = xi
        return 0

    lax.fori_loop(0, SEG, fix, 0)


def _ssm_fwd(u, b_pad, c_pad, tab, pw, d_skip, n_ex, lp):
    n_rows = u.shape[0]
    n_t = lp // TM
    n_chunk = D_MODEL // 128

    def body(u_ref, bp_ref, cp_ref, tab_ref, pw_ref, d_ref, yg_ref, y_ref, xs_ref, buf, carry_s, us, ys):
        @pl.when(pl.program_id(2) == 0)
        def _():
            carry_s[...] = jnp.zeros_like(carry_s)

        _to_segments(u_ref, us)
        ub = us[...]
        u16 = ub.astype(BF16)
        for j in range(PAIRS_PER_CHUNK):
            buf[j] = jnp.dot(u16, bp_ref[j], preferred_element_type=F32)
        _scan_segments(buf, tab_ref, pw_ref, carry_s, reverse=False)
        y = d_ref[...] * ub
        for j in range(PAIRS_PER_CHUNK):
            xb = buf[j].astype(BF16)
            xs_ref[j] = xb
            y = y + jnp.dot(xb, cp_ref[j], preferred_element_type=F32)
        ys[...] = y
        for i in range(8):
            yi = _from_segments(ys, i)
            y_ref[i * SEG:(i + 1) * SEG, :] = yi
            yg_ref[i * SEG:(i + 1) * SEG, :] = _gelu(yi).astype(BF16)

    rows = lambda b, q, t: (b * n_t + t, q)
    return pl.pallas_call(
        body, name="ssm_fwd", grid=(n_ex, n_chunk, n_t),
        in_specs=[pl.BlockSpec((TM, 128), rows),
                  pl.BlockSpec((PAIRS_PER_CHUNK, 128, 256), lambda b, q, t: (q, 0, 0)),
                  pl.BlockSpec((PAIRS_PER_CHUNK, 256, 128), lambda b, q, t: (q, 0, 0)),
                  pl.BlockSpec((PAIRS_PER_CHUNK, 8, 8, 128), lambda b, q, t: (q, 0, 0, 0)),
                  pl.BlockSpec((PAIRS_PER_CHUNK, 2, SEG, 8, 128), lambda b, q, t: (q, 0, 0, 0, 0)),
                  pl.BlockSpec((1, 128), lambda b, q, t: (0, q))],
        out_specs=[pl.BlockSpec((TM, 128), rows), pl.BlockSpec((TM, 128), rows),
                   pl.BlockSpec((PAIRS_PER_CHUNK, TM, 256), lambda b, q, t: (q, b * n_t + t, 0))],
        out_shape=[SDS((n_rows, D_MODEL), BF16), SDS((n_rows, D_MODEL), F32), SDS((N_PAIR, n_rows, 256), BF16)],
        scratch_shapes=[pltpu.VMEM((PAIRS_PER_CHUNK, TM, 256), F32), pltpu.VMEM((2 * PAIRS_PER_CHUNK, 8, 128), F32),
                        pltpu.VMEM((TM, 128), F32), pltpu.VMEM((TM, 128), F32)],
        compiler_params=_params(3),
    )(u, b_pad, c_pad, tab, pw, d_skip)


def _ssm_bwd(dyg, y, u, xs, ct_pad, bt_pad, tab_rev, pw_rev, d_skip, n_ex, lp):
    n_rows = u.shape[0]
    n_t = lp // TM
    n_chunk = D_MODEL // 128
    tile = lambda q, b, t: (b * n_t + (n_t - 1 - t), q)

    def body(dyg_ref, y_ref, u_ref, xs_ref, xp_ref, ct_ref, bt_ref, tab_ref, pw_ref, d_ref,
             du_ref, db_ref, dc_ref, da_ref, dd_ref, buf, xf, carry_s, us, dys, dyp):
        b, t = pl.program_id(1), pl.program_id(2)

        @pl.when((b == 0) & (t == 0))
        def _():
            db_ref[...] = jnp.zeros_like(db_ref)
            dc_ref[...] = jnp.zeros_like(dc_ref)
            da_ref[...] = jnp.zeros_like(da_ref)
            dd_ref[...] = jnp.zeros_like(dd_ref)

        @pl.when(t == 0)
        def _():
            carry_s[...] = jnp.zeros_like(carry_s)

        dys[...] = dyg_ref[...].astype(F32) * _gelu_grad(y_ref[...])
        dd_ref[...] += _fold8(dys[...] * u_ref[...])
        _to_segments(dys, dyp)
        dy = dyp[...]
        _to_segments(u_ref, us)
        dy16 = dy.astype(BF16)
        first_tile = t == n_t - 1
        for j in range(PAIRS_PER_CHUNK):
            buf[j] = jnp.dot(dy16, ct_ref[j], preferred_element_type=F32)
            dc_ref[j] += lax.dot_general(dy16, xs_ref[j], (((0,), (0,)), ((), ())), preferred_element_type=F32)
        du = d_ref[...] * dy
        for j in range(PAIRS_PER_CHUNK):
            xf[j, 16:16 + TM, :] = xs_ref[j].astype(F32)
            xf[j, 0:16, :] = jnp.where(first_tile, 0.0, xp_ref[j].astype(F32))
        _scan_segments(buf, tab_ref, pw_ref, carry_s, reverse=True)
        u16 = us[...].astype(BF16)
        row_id = lax.broadcasted_iota(jnp.int32, (8, 256), 0)
        for j in range(PAIRS_PER_CHUNK):
            g = buf[j]
            g16 = g.astype(BF16)
            du = du + jnp.dot(g16, bt_ref[j], preferred_element_type=F32)
            db_ref[j] += lax.dot_general(u16, g16, (((0,), (0,)), ((), ())), preferred_element_type=F32)
            first = jnp.where(row_id == 0, jnp.broadcast_to(xf[j, 15:16, :], (8, 256)),
                              pltpu.roll(xf[j, 8 + TM:16 + TM, :], 1, 0))
            for g_rows, x_prev in ((g[0:8], first), (g[8:TM], xf[j, 16:8 + TM, :])):
                gr, gi = g_rows[:, 0:128], g_rows[:, 128:256]
                pr, pi = x_prev[:, 0:128], x_prev[:, 128:256]
                da_ref[j, 0] += _fold8(gr * pr + gi * pi)
                da_ref[j, 1] += _fold8(gi * pr - gr * pi)
        dys[...] = du
        for i in range(8):
            du_ref[i * SEG:(i + 1) * SEG, :] = _from_segments(dys, i)

    prev16 = lambda q, b, t: (q, jnp.maximum((b * n_t + (n_t - 1 - t)) * (TM // 16) - 1, 0), 0)
    return pl.pallas_call(
        body, name="ssm_bwd", grid=(n_chunk, n_ex, n_t),
        in_specs=[pl.BlockSpec((TM, 128), tile), pl.BlockSpec((TM, 128), tile), pl.BlockSpec((TM, 128), tile),
                  pl.BlockSpec((PAIRS_PER_CHUNK, TM, 256), lambda q, b, t: (q, b * n_t + (n_t - 1 - t), 0)),
                  pl.BlockSpec((PAIRS_PER_CHUNK, 16, 256), prev16),
                  pl.BlockSpec((PAIRS_PER_CHUNK, 128, 256), lambda q, b, t: (q, 0, 0)),
                  pl.BlockSpec((PAIRS_PER_CHUNK, 256, 128), lambda q, b, t: (q, 0, 0)),
                  pl.BlockSpec((PAIRS_PER_CHUNK, 8, 8, 128), lambda q, b, t: (q, 0, 0, 0)),
                  pl.BlockSpec((PAIRS_PER_CHUNK, 2, SEG, 8, 128), lambda q, b, t: (q, 0, 0, 0, 0)),
                  pl.BlockSpec((1, 128), lambda q, b, t: (0, q))],
        out_specs=[pl.BlockSpec((TM, 128), tile),
                   pl.BlockSpec((PAIRS_PER_CHUNK, 128, 256), lambda q, b, t: (q, 0, 0)),
                   pl.BlockSpec((PAIRS_PER_CHUNK, 128, 256), lambda q, b, t: (q, 0, 0)),
                   pl.BlockSpec((PAIRS_PER_CHUNK, 2, 8, 128), lambda q, b, t: (q, 0, 0, 0)),
                   pl.BlockSpec((8, 128), lambda q, b, t: (0, q))],
        out_shape=[SDS((n_rows, D_MODEL), F32), SDS((N_PAIR, 128, 256), F32), SDS((N_PAIR, 128, 256), F32),
                   SDS((N_PAIR, 2, 8, 128), F32), SDS((8, D_MODEL), F32)],
        scratch_shapes=[pltpu.VMEM((PAIRS_PER_CHUNK, TM, 256), F32), pltpu.VMEM((PAIRS_PER_CHUNK, TM + 16, 256), F32),
                        pltpu.VMEM((2 * PAIRS_PER_CHUNK, 8, 128), F32), pltpu.VMEM((TM, 128), F32),
                        pltpu.VMEM((TM, 128), F32), pltpu.VMEM((TM, 128), F32)],
        compiler_params=_params(3),
    )(dyg, y, u, xs, xs, ct_pad, bt_pad, tab_rev, pw_rev, d_skip)


def _rms_fwd(h, wn, name):
    n_rows = h.shape[0]

    def body(h_ref, wn_ref, o_ref):
        o_ref[...] = _rms(h_ref[...], wn_ref[...])

    return pl.pallas_call(
        body, name=name, grid=(n_rows // TM,),
        in_specs=[pl.BlockSpec((TM, D_MODEL), lambda i: (i, 0)), pl.BlockSpec((1, D_MODEL), lambda i: (0, 0))],
        out_specs=pl.BlockSpec((TM, D_MODEL), lambda i: (i, 0)),
        out_shape=SDS((n_rows, D_MODEL), F32), compiler_params=_params(1),
    )(h, wn)


def _rms_bwd_call(dhn, h, wn, dres, name):
    n_rows = h.shape[0]

    def body(dhn_ref, h_ref, wn_ref, dres_ref, o_ref, dw_ref):
        @pl.when(pl.program_id(0) == 0)
        def _():
            dw_ref[...] = jnp.zeros_like(dw_ref)

        dh, dw_rows = _rms_bwd(dhn_ref[...], h_ref[...], wn_ref[...])
        o_ref[...] = dres_ref[...] + dh
        dw_ref[...] += _fold8(dw_rows)

    row = lambda i: (i, 0)
    return pl.pallas_call(
        body, name=name, grid=(n_rows // TM,),
        in_specs=[pl.BlockSpec((TM, D_MODEL), row), pl.BlockSpec((TM, D_MODEL), row),
                  pl.BlockSpec((1, D_MODEL), lambda i: (0, 0)), pl.BlockSpec((TM, D_MODEL), row)],
        out_specs=[pl.BlockSpec((TM, D_MODEL), row), pl.BlockSpec((8, D_MODEL), lambda i: (0, 0))],
        out_shape=[SDS((n_rows, D_MODEL), F32), SDS((8, D_MODEL), F32)], compiler_params=_params(1),
    )(dhn, h, wn, dres)


def _glu_fwd(h, z):
    n_rows = h.shape[0]

    def body(h_ref, val_ref, gate_ref, o_ref):
        o_ref[...] = h_ref[...] + val_ref[...].astype(F32) * jax.nn.sigmoid(gate_ref[...].astype(F32))

    row = lambda i: (i, 0)
    return pl.pallas_call(
        body, name="glu_fwd", grid=(n_rows // TM,),
        in_specs=[pl.BlockSpec((TM, D_MODEL), row), pl.BlockSpec((TM, D_MODEL), row),
                  pl.BlockSpec((TM, D_MODEL), lambda i: (i, 1))],
        out_specs=pl.BlockSpec((TM, D_MODEL), row),
        out_shape=SDS((n_rows, D_MODEL), F32), compiler_params=_params(1),
    )(h, z, z)


def _glu_bwd(dh, z):
    n_rows = dh.shape[0]

    def body(dh_ref, val_ref, gate_ref, dz_ref):
        sg = jax.nn.sigmoid(gate_ref[...].astype(F32))
        d = dh_ref[...]
        dz_ref[:, 0:D_MODEL] = (d * sg).astype(BF16)
        dz_ref[:, D_MODEL:2 * D_MODEL] = (d * val_ref[...].astype(F32) * sg * (1.0 - sg)).astype(BF16)

    row = lambda i: (i, 0)
    return pl.pallas_call(
        body, name="glu_bwd", grid=(n_rows // TM,),
        in_specs=[pl.BlockSpec((TM, D_MODEL), row), pl.BlockSpec((TM, D_MODEL), row),
                  pl.BlockSpec((TM, D_MODEL), lambda i: (i, 1))],
        out_specs=pl.BlockSpec((TM, 2 * D_MODEL), row),
        out_shape=SDS((n_rows, 2 * D_MODEL), BF16), compiler_params=_params(1),
    )(dh, z, z)


def _loss_head(h, wn, target, n_ex, nb):
    n_rows = h.shape[0]

    def body(h_ref, wn_ref, t_ref, dh_ref, loss_ref, dw_ref):
        b, n = pl.program_id(0), pl.program_id(1)

        @pl.when((b == 0) & (n == 0))
        def _():
            loss_ref[...] = jnp.zeros_like(loss_ref)
            dw_ref[...] = jnp.zeros_like(dw_ref)

        @pl.when(n == 0)
        def _():
            dh_ref[...] = jnp.zeros_like(dh_ref)

        @pl.when(n > 0)
        def _():
            hh = h_ref[...]
            diff = _rms(hh, wn_ref[...]) - t_ref[...]
            loss_ref[...] += 0.5 * jnp.sum(diff * diff) * (1.0 / D_MODEL)
            dh, dw_rows = _rms_bwd(diff * (1.0 / D_MODEL), hh, wn_ref[...])
            dh_ref[...] = dh
            dw_ref[...] += _fold8(dw_rows)

    return pl.pallas_call(
        body, name="loss_head", grid=(n_ex, nb),
        in_specs=[pl.BlockSpec((BLOCK, D_MODEL), lambda b, n: (b * nb + n, 0)),
                  pl.BlockSpec((1, D_MODEL), lambda b, n: (0, 0)),
                  pl.BlockSpec((BLOCK, D_MODEL), lambda b, n: (b * (nb - 1) + jnp.maximum(n - 1, 0), 0))],
        out_specs=[pl.BlockSpec((BLOCK, D_MODEL), lambda b, n: (b * nb + n, 0)),
                   pl.BlockSpec((8, 128), lambda b, n: (0, 0)),
                   pl.BlockSpec((8, D_MODEL), lambda b, n: (0, 0))],
        out_shape=[SDS((n_rows, D_MODEL), F32), SDS((8, 128), F32), SDS((8, D_MODEL), F32)],
        compiler_params=_params(2),
    )(h, wn, target)


def _adamw(pieces, w, m, v, name):
    n_layers = len(pieces)
    rows, cols = pieces[0].shape[1:]
    rb = rows
    for cand in (256, 136, 128, 64, 32, 16, 8):
        if rows % cand == 0 and rows > cand:
            rb = cand
            break
    n_blk = rows // rb
    c1 = 1.0 / (1.0 - ADAM_B1 ** ADAM_STEP)
    c2 = 1.0 / (1.0 - ADAM_B2 ** ADAM_STEP)

    def body(*refs):
        p_refs = refs[:n_layers]
        w_ref, m_ref, v_ref, g_out, d_out, m_out, v_out = refs[n_layers:]
        layer = pl.program_id(0)
        g = None
        for l, p_ref in enumerate(p_refs):
            gl = p_ref[0].astype(F32)
            for k in range(1, N_DEV):
                gl = gl + p_ref[k].astype(F32)
            g = gl if g is None else jnp.where(layer == l, gl, g)
        m_new = ADAM_B1 * m_ref[...] + (1.0 - ADAM_B1) * g
        v_new = ADAM_B2 * v_ref[...] + (1.0 - ADAM_B2) * (g * g)
        g_out[...] = g
        m_out[...] = m_new
        v_out[...] = v_new
        d_out[...] = -ADAM_LR * ((m_new * c1) / (jnp.sqrt(v_new * c2) + ADAM_EPS) + ADAM_WD * w_ref[...])

    def piece_spec(l):
        return pl.BlockSpec((N_DEV, rb, cols), lambda ly, i: (0, jnp.where(ly == l, i, 0), 0))

    blk = pl.BlockSpec((rb, cols), lambda ly, i: (ly * n_blk + i, 0))
    return pl.pallas_call(
        body, name=name, grid=(n_layers, n_blk),
        in_specs=[piece_spec(l) for l in range(n_layers)] + [blk, blk, blk],
        out_specs=[blk, blk, blk, blk],
        out_shape=[SDS((n_layers * rows, cols), F32)] * 4, compiler_params=_params(2),
    )(*pieces, w, m, v)


_HBM = pl.BlockSpec(memory_space=pltpu.HBM)
_SEM = pl.BlockSpec(memory_space=pltpu.SEMAPHORE)
_EFFECT = pltpu.SideEffectType.DATAFLOW_SIDE_EFFECTING
N_GATHER_PEERS = N_CHIPS - 1
N_EXCHANGE_PEERS = N_DEV - 1


def _gather_copies(srcs, lands, send_sems, recv_sems):
    x, y, c = lax.axis_index("x"), lax.axis_index("y"), lax.axis_index("c")
    mine = 2 * x + y
    chips = [(1 - x, y), (x, 1 - y), (1 - x, 1 - y)]
    out, inc = [], []
    for a in range(len(srcs)):
        for k, (px, py) in enumerate(chips):
            j = a * N_GATHER_PEERS + k
            sems = dict(send_sem=send_sems.at[j], recv_sem=recv_sems.at[j], device_id=(px, py, c),
                        device_id_type=pl.DeviceIdType.MESH)
            out.append(pltpu.make_async_remote_copy(src_ref=srcs[a], dst_ref=lands[a].at[mine], **sems))
            inc.append(pltpu.make_async_remote_copy(src_ref=srcs[a], dst_ref=lands[a].at[2 * px + py], **sems))
    return out, inc


def _exchange_copies(n_scatter):
    def copies(srcs, lands, send_sems, recv_sems):
        x, y, c = lax.axis_index("x"), lax.axis_index("y"), lax.axis_index("c")
        me = 4 * x + 2 * y + c
        peers = [(x ^ (k >> 2), y ^ ((k >> 1) & 1), c ^ (k & 1)) for k in range(1, N_DEV)]
        out, inc = [], []
        for a in range(len(srcs)):
            for k, (px, py, pc) in enumerate(peers):
                j = a * N_EXCHANGE_PEERS + k
                sems = dict(send_sem=send_sems.at[j], recv_sem=recv_sems.at[j], device_id=(px, py, pc),
                            device_id_type=pl.DeviceIdType.MESH)
                theirs = srcs[a].at[2 * px + py] if a < n_scatter else srcs[a]
                mine = srcs[a].at[2 * x + y] if a < n_scatter else srcs[a]
                out.append(pltpu.make_async_remote_copy(src_ref=theirs, dst_ref=lands[a].at[me], **sems))
                inc.append(pltpu.make_async_remote_copy(src_ref=mine, dst_ref=lands[a].at[4 * px + 2 * py + pc], **sems))
        return out, inc

    return copies


def _split_start(groups, copies_fn, n_peers, name):
    sizes = [len(srcs) for srcs, _ in groups]
    flat = [a for srcs, lands in groups for a in list(srcs) + list(lands)]
    n_flat, n_grp = len(flat), len(groups)

    def body(*refs):
        sems = refs[2 * n_flat:2 * n_flat + 2 * n_grp]
        token = refs[-1]
        at = 0
        for gi, n in enumerate(sizes):
            out, _ = copies_fn(refs[at:at + n], refs[at + n:at + 2 * n], sems[2 * gi], sems[2 * gi + 1])
            for cp in out:
                cp.start()
            at += 2 * n
        token[...] = jnp.zeros_like(token)

    sem_shapes = []
    for n in sizes:
        sem_shapes += [pltpu.SemaphoreType.DMA((n * n_peers,)), pltpu.SemaphoreType.DMA((n * n_peers,))]
    res = pl.pallas_call(
        body, name=name,
        out_shape=(*[pltpu.HBM(a.shape, a.dtype) for a in flat], *sem_shapes, SDS((8, 128), F32)),
        in_specs=[_HBM] * n_flat,
        out_specs=(*[_HBM] * n_flat, *[_SEM] * (2 * n_grp), pl.BlockSpec(memory_space=pltpu.VMEM)),
        input_output_aliases={i: i for i in range(n_flat)},
        compiler_params=pltpu.CompilerParams(has_side_effects=_EFFECT),
    )(*[pltpu.with_memory_space_constraint(a, pltpu.HBM) for a in flat])
    handles, at = [], 0
    for gi, n in enumerate(sizes):
        handles.append((res[n_flat + 2 * gi], res[n_flat + 2 * gi + 1], list(res[at:at + n]), list(res[at + n:at + 2 * n])))
        at += 2 * n
    return handles, res[-1]


def _split_wait(handle, after, copies_fn, name):
    send_sems, recv_sems, srcs, lands = handle
    n = len(srcs)
    after = list(after) if isinstance(after, (list, tuple)) else [after]

    def body(*refs):
        out, inc = copies_fn(refs[:n], refs[n:2 * n], refs[2 * n], refs[2 * n + 1])
        for cp in out:
            cp.wait_send()
        for cp in inc:
            cp.wait_recv()

    flat = list(srcs) + list(lands)
    res = pl.pallas_call(
        body, name=name,
        out_shape=tuple(pltpu.HBM(a.shape, a.dtype) for a in flat),
        in_specs=[_HBM] * (2 * n) + [_SEM, _SEM] + [pl.BlockSpec(memory_space=pl.ANY)] * len(after),
        out_specs=tuple([_HBM] * (2 * n)),
        input_output_aliases={i: i for i in range(2 * n)},
        compiler_params=pltpu.CompilerParams(has_side_effects=_EFFECT),
    )(*flat, send_sems, recv_sems, *after)
    return list(res[n:])


def _landing(own, slot, n_slots):
    return lax.dynamic_update_index_in_dim(lax.empty((n_slots,) + own.shape, own.dtype), own, slot, 0)


def _ssm_discretize(lam_re, lam_im, log_dt, b_re, b_im):
    lr = jnp.minimum(lam_re, LAMBDA_RE_MAX)
    li = lam_im
    dt = jnp.exp(log_dt)[:, None]
    mag = jnp.exp(lr * dt)
    ar, ai = mag * jnp.cos(li * dt), mag * jnp.sin(li * dt)
    den = lr * lr + li * li
    nr, ni = ar - 1.0, ai
    gr, gi = (nr * lr + ni * li) / den, (ni * lr - nr * li) / den
    bbr = gr[:, :, None] * b_re - gi[:, :, None] * b_im
    bbi = gr[:, :, None] * b_im + gi[:, :, None] * b_re
    return ar, ai, bbr, bbi


def _pair_lanes(t):
    return t.reshape(N_PAIR, 2 * SSM_STATE)


def _chan_state_blocks(t_gcp):
    t = t_gcp.reshape(N_PAIR, 2, SSM_GROUP, SSM_STATE)
    eye2 = jnp.eye(2, dtype=t.dtype)
    blk = jnp.einsum("rgcp,gh->rgchp", t, eye2).reshape(N_PAIR, 2 * SSM_GROUP, 2 * SSM_STATE)
    place = jax.nn.one_hot(jnp.arange(N_PAIR) % PAIRS_PER_CHUNK, PAIRS_PER_CHUNK, dtype=t.dtype)
    return jnp.einsum("rcl,rj->rjcl", blk, place).reshape(N_PAIR, 128, 2 * SSM_STATE)


def _chan_state_unblock(t):
    t = t.reshape(N_PAIR, PAIRS_PER_CHUNK, 2, SSM_GROUP, 2, SSM_STATE)
    place = jax.nn.one_hot(jnp.arange(N_PAIR) % PAIRS_PER_CHUNK, PAIRS_PER_CHUNK, dtype=t.dtype)
    t = jnp.einsum("rjgchp,rj->rgchp", t, place)
    t = jnp.einsum("rgchp,gh->rgcp", t, jnp.eye(2, dtype=t.dtype))
    return t.reshape(SSM_NG, SSM_GROUP, SSM_STATE)


def _scan_tables(ar, ai, reverse):
    ar, ai = _pair_lanes(ar), _pair_lanes(ai)
    if reverse:
        ai = -ai
    pows = [(ar, ai)]
    for _ in range(SEG - 1):
        pr, pi = pows[-1]
        pows.append((pr * ar - pi * ai, pr * ai + pi * ar))
    big = [pows[-1]]
    for _ in range(2):
        br, bi = big[-1]
        big.append((br * br - bi * bi, 2.0 * br * bi))
    rows = jnp.arange(8)[None, :, None]
    tiles = [jnp.broadcast_to(part[:, None, :], (N_PAIR, 8, 128)) for part in pows[0]]
    for lvl, k in enumerate((1, 2, 4)):
        keep = (rows <= 7 - k) if reverse else (rows >= k)
        for part in big[lvl]:
            tiles.append(jnp.where(keep, part[:, None, :], 0.0))
    order = list(range(SEG - 1, -1, -1)) if reverse else list(range(SEG))
    pw = jnp.stack([jnp.stack([pows[e][comp] for e in order], axis=1) for comp in (0, 1)], axis=1)
    return jnp.stack(tiles, axis=1), jnp.broadcast_to(pw[:, :, :, None, :], (N_PAIR, 2, SEG, 8, 128))


def _ssm_operands(w):
    ar, ai, bbr, bbi = _ssm_discretize(w["ssm_lambda_re"], w["ssm_lambda_im"], w["ssm_log_dt"], w["ssm_b_re"], w["ssm_b_im"])
    b_blk = jnp.concatenate([_chan_state_blocks(jnp.swapaxes(bbr, 1, 2)), _chan_state_blocks(jnp.swapaxes(bbi, 1, 2))], axis=2)
    c_blk = jnp.concatenate([_chan_state_blocks(w["ssm_c_re"]), -_chan_state_blocks(w["ssm_c_im"])], axis=2)
    return (b_blk.astype(BF16), jnp.swapaxes(b_blk, 1, 2).astype(BF16), c_blk.astype(BF16),
            jnp.swapaxes(c_blk, 1, 2).astype(BF16), *_scan_tables(ar, ai, False), *_scan_tables(ar, ai, True))


def _local_step(x, target, w, late_weights, on_grads):
    n_ex, seq, _ = x.shape
    lp = seq + BLOCK
    nb = lp // BLOCK
    n_rows = n_ex * lp
    g = {}

    head = jnp.concatenate([jnp.zeros((PAD, D_MODEL), F32), w["meta_tokens"]], axis=0)
    h0 = jnp.concatenate([jnp.broadcast_to(head[None], (n_ex, BLOCK, D_MODEL)), x], axis=1).reshape(n_rows, D_MODEL)

    qkv, hn_a = _rms_mm_cols(h0, w["attn_norm_w"], w["attn_w_qkv"], "qkv_fwd")
    att, lse = _attn_fwd(qkv, w["attn_sinks"], n_ex, nb)
    h1 = _mm_acc(att, w["attn_w_o"], False, "attn_out_fwd", res=h0)
    w = {**w, **late_weights(0, att)}
    h2, a0, hn_m0 = _mlp_fwd(h1, w["mlp_norm_w"][0:1], w["mlp_w_up"][0], w["mlp_w_down"][0], 0, "mlp0_fwd")
    late = late_weights(1, h2)
    w["ssm_w_glu"] = late["ssm_w_glu"]
    w["mlp_w_up"], w["mlp_w_down"] = w["mlp_w_up"] + late["mlp_w_up"], w["mlp_w_down"] + late["mlp_w_down"]

    ops = w["ssm_operands"] if "ssm_operands" in w else _ssm_operands(w)
    b_pad, bt_pad, ct_pad, c_pad, tab_fwd, pw_fwd, tab_rev, pw_rev = ops
    u = _rms_fwd(h2, w["ssm_norm_w"], "ssm_norm_fwd")
    yg, y, xs = _ssm_fwd(u, b_pad, c_pad, tab_fwd, pw_fwd, w["ssm_d"], n_ex, lp)
    z = _mm_cols(yg, w["ssm_w_glu"], False, "glu_mm_fwd")
    h3 = _glu_fwd(h2, z)
    h4, a1, hn_m1 = _mlp_fwd(h3, w["mlp_norm_w"][1:2], w["mlp_w_up"][1], w["mlp_w_down"][1], 0, "mlp1_fwd")

    dh4, loss_tile, dnorm_f = _loss_head(h4, w["final_norm_w"], target.reshape(n_ex * seq, D_MODEL), n_ex, nb)

    def mlp_bwd(dh_out, h_in, a, hn, layer, tag, norm_w):
        dhn, dw_up, dw_down = None, None, None
        for s in range(N_CHIPS):
            final = s == N_CHIPS - 1
            res = _mlp_bwd_shard(s, dh_out, a, hn, dhn, h_in if final else None, norm_w,
                                 w["mlp_w_up"][layer], w["mlp_w_down"][layer], dw_up, dw_down, f"{tag}_bwd{s}")
            dhn, dw_up, dw_down = res[:3]
        return dhn, res[3], dw_up, dw_down

    dh3, dnorm_m1, dwu1, dwd1 = mlp_bwd(dh4, h3, a1, hn_m1, 1, "mlp1", w["mlp_norm_w"][1:2])
    tok = on_grads("mlp1", {"mlp_w_up": dwu1, "mlp_w_down": dwd1})
    dz = _glu_bwd(dh3, z)
    dyg = _mm_acc(dz, w["ssm_w_glu"], True, "glu_mm_dx", out_dtype=BF16)
    g["ssm_w_glu"] = _mm_tn(yg, dz, N_CHIPS, False, "glu_mm_dw")
    du, db_blk, dc_blk, da_t, dd_t = _ssm_bwd(dyg, y, u, xs, ct_pad, bt_pad, tab_rev, pw_rev, w["ssm_d"] + tok, n_ex, lp)
    dh2, dnorm_s = _rms_bwd_call(du, h2, w["ssm_norm_w"], dh3, "ssm_norm_bwd")
    g["ssm_c_re"] = _chan_state_unblock(dc_blk[:, :, 0:128])
    g["ssm_c_im"] = -_chan_state_unblock(dc_blk[:, :, 128:256])
    g_bbr = jnp.swapaxes(_chan_state_unblock(db_blk[:, :, 0:128]), 1, 2)
    g_bbi = jnp.swapaxes(_chan_state_unblock(db_blk[:, :, 128:256]), 1, 2)
    g_a = jnp.sum(da_t, axis=2).reshape(N_PAIR, 2, 2, SSM_STATE)
    g_ar, g_ai = g_a[:, 0].reshape(SSM_NG, SSM_STATE), g_a[:, 1].reshape(SSM_NG, SSM_STATE)
    _, vjp = jax.vjp(_ssm_discretize, w["ssm_lambda_re"], w["ssm_lambda_im"], w["ssm_log_dt"], w["ssm_b_re"], w["ssm_b_im"])
    g["ssm_lambda_re"], g["ssm_lambda_im"], g["ssm_log_dt"], g["ssm_b_re"], g["ssm_b_im"] = vjp((g_ar, g_ai, g_bbr, g_bbi))
    tok = on_grads("ssm", g)
    g = {}
    dh1, dnorm_m0, dwu0, dwd0 = mlp_bwd(dh2, h1, a0, hn_m0, 0, "mlp0", w["mlp_norm_w"][0:1] + tok)
    datt = _mm_cols(dh1, w["attn_w_o"], True, "attn_out_dx")
    dw_o = _mm_tn(att, dh1, N_CHIPS, True, "attn_out_dw")
    tok = on_grads("mlp0", {"mlp_w_up": dwu0, "mlp_w_down": dwd0, "attn_w_o": dw_o})
    dqkv, dsink_rows = _attn_bwd(qkv, w["attn_sinks"] + tok, att, lse, datt, n_ex, nb)
    tok = on_grads("qkv", {"attn_w_qkv": _mm_tn(hn_a, dqkv, N_CHIPS, False, "qkv_dw")})
    dh0, dnorm_a = _mm_acc(dqkv, w["attn_w_qkv"], True, "qkv_dx", rms_bwd=(h0, w["attn_norm_w"] + tok, dh1))

    dh0 = dh0.reshape(n_ex, lp, D_MODEL)
    on_grads("rest", {
        "mlp_norm_w": jnp.stack([jnp.sum(dnorm_m0, axis=0), jnp.sum(dnorm_m1, axis=0)]),
        "final_norm_w": jnp.sum(dnorm_f, axis=0),
        "attn_norm_w": jnp.sum(dnorm_a, axis=0)[None],
        "ssm_norm_w": jnp.sum(dnorm_s, axis=0)[None],
        "attn_sinks": jnp.sum(dsink_rows, axis=0)[None],
        "ssm_d": jnp.sum(dd_t, axis=0)[None],
        "meta_tokens": jnp.sum(dh0[:, PAD:BLOCK], axis=0)})
    return loss_tile, dh0[:, BLOCK:]


_SHARDED_SMALL = ("meta_tokens", "ssm_norm_w", "ssm_d")
_REP_SSM = ("ssm_lambda_re", "ssm_lambda_im", "ssm_log_dt", "ssm_b_re", "ssm_b_im", "ssm_c_re", "ssm_c_im")
_REP_MISC = ("attn_norm_w", "attn_sinks", "mlp_norm_w", "final_norm_w")
_BIG = ("attn_w_qkv", "attn_w_o", "ssm_w_glu", "mlp_w_up", "mlp_w_down")


def _pack(parts, cols):
    flat = jnp.concatenate([p.reshape(-1) for p in parts])
    rows = -(-flat.shape[0] // (8 * cols)) * 8
    return jnp.pad(flat, (0, rows * cols - flat.shape[0])).reshape(rows, cols)


def _unpack(packed, like):
    flat = packed.reshape(-1)
    out, at = [], 0
    for p in like:
        out.append(flat[at:at + p.size].reshape(p.shape))
        at += p.size
    return out


def kernel(x, meta_tokens, attn_norm_w, attn_w_qkv, attn_sinks, attn_w_o, ssm_norm_w, ssm_lambda_re, ssm_lambda_im, ssm_log_dt, ssm_b_re, ssm_b_im, ssm_c_re, ssm_c_im, ssm_d, ssm_w_glu, mlp_norm_w, mlp_w_up, mlp_w_down, final_norm_w, loss_target, m_meta_tokens, m_attn_norm_w, m_attn_w_qkv, m_attn_sinks, m_attn_w_o, m_ssm_norm_w, m_ssm_lambda_re, m_ssm_lambda_im, m_ssm_log_dt, m_ssm_b_re, m_ssm_b_im, m_ssm_c_re, m_ssm_c_im, m_ssm_d, m_ssm_w_glu, m_mlp_norm_w, m_mlp_w_up, m_mlp_w_down, m_final_norm_w, v_meta_tokens, v_attn_norm_w, v_attn_w_qkv, v_attn_sinks, v_attn_w_o, v_ssm_norm_w, v_ssm_lambda_re, v_ssm_lambda_im, v_ssm_log_dt, v_ssm_b_re, v_ssm_b_im, v_ssm_c_re, v_ssm_c_im, v_ssm_d, v_ssm_w_glu, v_mlp_norm_w, v_mlp_w_up, v_mlp_w_down, v_final_norm_w):
    names = ("meta_tokens", "attn_norm_w", "attn_w_qkv", "attn_sinks", "attn_w_o", "ssm_norm_w", "ssm_lambda_re",
             "ssm_lambda_im", "ssm_log_dt", "ssm_b_re", "ssm_b_im", "ssm_c_re", "ssm_c_im", "ssm_d", "ssm_w_glu",
             "mlp_norm_w", "mlp_w_up", "mlp_w_down", "final_norm_w")
    wts = dict(zip(names, (meta_tokens, attn_norm_w, attn_w_qkv, attn_sinks, attn_w_o, ssm_norm_w, ssm_lambda_re,
                           ssm_lambda_im, ssm_log_dt, ssm_b_re, ssm_b_im, ssm_c_re, ssm_c_im, ssm_d, ssm_w_glu,
                           mlp_norm_w, mlp_w_up, mlp_w_down, final_norm_w)))
    mom = dict(zip(names, (m_meta_tokens, m_attn_norm_w, m_attn_w_qkv, m_attn_sinks, m_attn_w_o, m_ssm_norm_w,
                           m_ssm_lambda_re, m_ssm_lambda_im, m_ssm_log_dt, m_ssm_b_re, m_ssm_b_im, m_ssm_c_re,
                           m_ssm_c_im, m_ssm_d, m_ssm_w_glu, m_mlp_norm_w, m_mlp_w_up, m_mlp_w_down, m_final_norm_w)))
    var = dict(zip(names, (v_meta_tokens, v_attn_norm_w, v_attn_w_qkv, v_attn_sinks, v_attn_w_o, v_ssm_norm_w,
                           v_ssm_lambda_re, v_ssm_lambda_im, v_ssm_log_dt, v_ssm_b_re, v_ssm_b_im, v_ssm_c_re,
                           v_ssm_c_im, v_ssm_d, v_ssm_w_glu, v_mlp_norm_w, v_mlp_w_up, v_mlp_w_down, v_final_norm_w)))

    my_chip = 2 * lax.axis_index("x") + lax.axis_index("y")
    my_dev = 2 * my_chip + lax.axis_index("c")
    small_mine = _pack([wts[n] for n in _SHARDED_SMALL], 128)
    first = [attn_w_qkv.astype(BF16), attn_w_o.astype(BF16), small_mine]
    up16, down16 = mlp_w_up.astype(BF16), mlp_w_down.astype(BF16)
    mlp0 = [up16[0:1], down16[0:1]]
    rest = [ssm_w_glu.astype(BF16), up16[1:2], down16[1:2]]
    handles, _ = _split_start([(srcs, [_landing(a, my_chip, N_CHIPS) for a in srcs]) for srcs in (first, mlp0, rest)],
                              _gather_copies, N_GATHER_PEERS, "gather_start")
    full = {n: wts[n] for n in _REP_MISC}
    full["final_norm_w"] = final_norm_w[None]
    for n in _REP_SSM:
        full[n] = wts[n][0]
    full["ssm_operands"] = _ssm_operands(full)
    got = _split_wait(handles[0], full["ssm_operands"], _gather_copies, "gather_wait_first")
    full["attn_w_qkv"], full["attn_w_o"] = got[0], got[1]
    smalls = [_unpack(got[2][s], [wts[n] for n in _SHARDED_SMALL]) for s in range(N_CHIPS)]
    for k, n in enumerate(_SHARDED_SMALL):
        full[n] = jnp.concatenate([smalls[s][k] for s in range(N_CHIPS)], axis=1)

    def late_weights(stage, after):
        if stage == 0:
            up, down = _split_wait(handles[1], after, _gather_copies, "gather_wait_mlp0")
            return {"mlp_w_up": [up], "mlp_w_down": [down]}
        glu, up, down = _split_wait(handles[2], after, _gather_copies, "gather_wait_rest")
        return {"ssm_w_glu": glu, "mlp_w_up": [up], "mlp_w_down": [down]}

    def shard_cols(t):
        return jnp.swapaxes(t.reshape(t.shape[0], N_CHIPS, t.shape[1] // N_CHIPS), 0, 1)

    pending = {}

    def on_grads(tag, g):
        scatter = [g[n] for n in _BIG if n in g]
        whole = []
        if tag == "ssm":
            whole = [_pack([g[n] for n in _REP_SSM], D_MODEL)]
        if tag == "rest":
            parts = [shard_cols(g[n]) for n in _SHARDED_SMALL]
            scatter = [jnp.stack([_pack([p[s] for p in parts], 128) for s in range(N_CHIPS)])]
            whole = [_pack([g[n] for n in _REP_MISC], D_MODEL)]
        srcs = scatter + whole
        lands = [_landing(lax.dynamic_index_in_dim(a, my_chip, 0, keepdims=False), my_dev, N_DEV) for a in scatter]
        lands += [_landing(a, my_dev, N_DEV) for a in whole]
        hs, token = _split_start([(srcs, lands)], _exchange_copies(len(scatter)), N_EXCHANGE_PEERS, "exchange_start_" + tag)
        pending[tag] = (hs[0], len(scatter))
        return token[0, 0]

    loss_tile, grad_x = _local_step(x, loss_target, full, late_weights, on_grads)
    loss = lax.psum(loss_tile[0, 0], ("x", "y", "c"))

    recv = {}
    for tag, (handle, n_scatter) in pending.items():
        recv[tag] = _split_wait(handle, grad_x, _exchange_copies(n_scatter), "exchange_wait_" + tag)

    out = {}

    def update(tag, pieces, w2, m2, v2):
        return _adamw(pieces, w2, m2, v2, "adamw_" + tag)

    def update_weight(n, pieces):
        shp = wts[n].shape
        r2 = (math.prod(shp[:-1]), shp[-1])
        res = update(n, pieces, wts[n].reshape(r2), mom[n].reshape(r2), var[n].reshape(r2))
        out[n] = [t.reshape(shp) for t in res]

    update_weight("mlp_w_up", [recv["mlp0"][1], recv["mlp1"][0]])
    update_weight("mlp_w_down", [recv["mlp0"][2], recv["mlp1"][1]])
    update_weight("attn_w_o", [recv["mlp0"][0]])
    update_weight("ssm_w_glu", [recv["ssm"][0]])
    update_weight("attn_w_qkv", [recv["qkv"][0]])
    for tag, group, pieces, cols in (("small", _SHARDED_SMALL, recv["rest"][0], 128),
                                     ("rep_ssm", _REP_SSM, recv["ssm"][1], D_MODEL),
                                     ("rep_misc", _REP_MISC, recv["rest"][1], D_MODEL)):
        like = [wts[n] for n in group]
        res = update(tag, [pieces], _pack(like, cols), _pack([mom[n] for n in group], cols),
                     _pack([var[n] for n in group], cols))
        for k, n in enumerate(group):
            out[n] = [_unpack(t, like)[k] for t in res]

    return (loss, grad_x, *[out[n][0] for n in names], *[out[n][1] for n in names],
            *[out[n][2] for n in names], *[out[n][3] for n in names])
```

```python
import functools
import math

import jax
import jax.numpy as jnp
from jax import lax
from jax.experimental import pallas as pl
from jax.experimental.pallas import tpu as pltpu

F32 = jnp.float32
BF16 = jnp.bfloat16
SDS = jax.ShapeDtypeStruct

D_MODEL = 1024
N_HEADS = 16
N_KV = 4
GQA = N_HEADS // N_KV
HEAD_DIM = 64
BLOCK = 128
N_META = 16
PAD = BLOCK - N_META
QKV_DIM = (N_HEADS + 2 * N_KV) * HEAD_DIM
KV_DIM = 2 * N_KV * HEAD_DIM
D_FF = 4 * D_MODEL
N_CHIPS = 4
N_DEV = 8
SSM_GROUP = 16
SSM_NG = D_MODEL // SSM_GROUP
SSM_STATE = 64
N_PAIR = SSM_NG // 2
PAIRS_PER_CHUNK = 4
RMS_EPS = 1e-6
NEG_INF = -1e30
LAMBDA_RE_MAX = -1e-4
ADAM_LR, ADAM_B1, ADAM_B2, ADAM_EPS, ADAM_WD, ADAM_STEP = 0.001, 0.9, 0.999, 1e-08, 0.01, 10

TM = 384
MM_TILES = (768, 384)
MLP_FWD_TILES = (1056, 768, 384)
MLP_BWD_TILES = (768, 384)
TN_TILES = (1408, 768, 384)
VMEM_LIMIT = 56 * 1024 * 1024


def _params(n_grid):
    return pltpu.CompilerParams(dimension_semantics=("arbitrary",) * n_grid, vmem_limit_bytes=VMEM_LIMIT)


def _row_tile(n_rows, tiles):
    return next(t for t in tiles if n_rows % t == 0)


def _rms(h, w):
    r = lax.rsqrt(jnp.mean(h * h, axis=-1, keepdims=True) + RMS_EPS)
    return h * r * w


def _rms_bwd(dhn, h, w):
    r = lax.rsqrt(jnp.mean(h * h, axis=-1, keepdims=True) + RMS_EPS)
    g = dhn * w
    proj = jnp.sum(g * h, axis=-1, keepdims=True) * (1.0 / D_MODEL)
    return r * g - h * (r * r * r) * proj, dhn * h * r


def _fold8(t):
    return jnp.sum(t.reshape(t.shape[0] // 8, 8, t.shape[1]), axis=0)


def _gelu(y):
    return 0.5 * y * (1.0 + jnp.tanh(0.7978845608028654 * (y + 0.044715 * y * y * y)))


def _gelu_grad(y):
    t = jnp.tanh(0.7978845608028654 * (y + 0.044715 * y * y * y))
    return 0.5 * (1.0 + t) + 0.5 * y * (1.0 - t * t) * 0.7978845608028654 * (1.0 + 3.0 * 0.044715 * y * y)


def _w4_spec(w4):
    n_sh, _, k, n = w4.shape
    return pl.BlockSpec((n_sh, None, k, n), lambda i: (0, 0, 0, 0))


def _rms_mm_cols(h, wn, w4, name):
    n_rows = h.shape[0]
    n_sh, _, k, n = w4.shape
    tm = _row_tile(n_rows, MM_TILES)

    def body(h_ref, wn_ref, w_ref, o_ref, hn_ref):
        hn = _rms(h_ref[...], wn_ref[...]).astype(BF16)
        hn_ref[...] = hn
        for s in range(n_sh):
            o_ref[:, s * n:(s + 1) * n] = jnp.dot(hn, w_ref[s], preferred_element_type=F32).astype(o_ref.dtype)

    return pl.pallas_call(
        body, name=name, grid=(n_rows // tm,),
        in_specs=[pl.BlockSpec((tm, k), lambda i: (i, 0)), pl.BlockSpec((1, k), lambda i: (0, 0)), _w4_spec(w4)],
        out_specs=[pl.BlockSpec((tm, n_sh * n), lambda i: (i, 0)), pl.BlockSpec((tm, k), lambda i: (i, 0))],
        out_shape=[SDS((n_rows, n_sh * n), BF16), SDS((n_rows, k), BF16)],
        compiler_params=_params(1),
    )(h, wn, w4)


def _mm_cols(x, w4, trans_w, name):
    n_rows, kx = x.shape
    tm = _row_tile(n_rows, MM_TILES)
    n_sh, _, k, n = w4.shape
    n_out = k if trans_w else n
    dims = (((1,), (1,)), ((), ())) if trans_w else (((1,), (0,)), ((), ()))

    def body(x_ref, w_ref, o_ref):
        x16 = x_ref[...].astype(BF16)
        for s in range(n_sh):
            o_ref[:, s * n_out:(s + 1) * n_out] = lax.dot_general(
                x16, w_ref[s], dims, preferred_element_type=F32).astype(o_ref.dtype)

    return pl.pallas_call(
        body, name=name, grid=(n_rows // tm,),
        in_specs=[pl.BlockSpec((tm, kx), lambda i: (i, 0)), _w4_spec(w4)],
        out_specs=pl.BlockSpec((tm, n_sh * n_out), lambda i: (i, 0)),
        out_shape=SDS((n_rows, n_sh * n_out), BF16),
        compiler_params=_params(1),
    )(x, w4)


def _mm_acc(x, w4, trans_w, name, res=None, rms_bwd=None, out_dtype=F32):
    n_rows = x.shape[0]
    tm = _row_tile(n_rows, MM_TILES)
    n_sh, _, k, n = w4.shape
    kx, n_out = (n, k) if trans_w else (k, n)
    dims = (((1,), (1,)), ((), ())) if trans_w else (((1,), (0,)), ((), ()))

    def body(*refs):
        if rms_bwd is not None:
            x_ref, w_ref, h_ref, wn_ref, dres_ref, o_ref, dw_ref = refs
        elif res is not None:
            x_ref, w_ref, res_ref, o_ref = refs
        else:
            x_ref, w_ref, o_ref = refs
        acc = None
        for s in range(n_sh):
            part = lax.dot_general(x_ref[:, s * kx:(s + 1) * kx].astype(BF16), w_ref[s], dims, preferred_element_type=F32)
            acc = part if acc is None else acc + part
        if rms_bwd is not None:
            dh, dw_rows = _rms_bwd(acc, h_ref[...], wn_ref[...])
            o_ref[...] = (dres_ref[...] + dh).astype(o_ref.dtype)

            @pl.when(pl.program_id(0) == 0)
            def _():
                dw_ref[...] = jnp.zeros_like(dw_ref)

            dw_ref[...] += _fold8(dw_rows)
        elif res is not None:
            o_ref[...] = (res_ref[...] + acc).astype(o_ref.dtype)
        else:
            o_ref[...] = acc.astype(o_ref.dtype)

    row = lambda i: (i, 0)
    in_specs = [pl.BlockSpec((tm, n_sh * kx), row), _w4_spec(w4)]
    args = [x, w4]
    out_specs = pl.BlockSpec((tm, n_out), row)
    out_shape = SDS((n_rows, n_out), out_dtype)
    if rms_bwd is not None:
        h, wn, dres = rms_bwd
        in_specs += [pl.BlockSpec((tm, n_out), row), pl.BlockSpec((1, n_out), lambda i: (0, 0)),
                     pl.BlockSpec((tm, n_out), row)]
        args += [h, wn, dres]
        out_specs = [out_specs, pl.BlockSpec((8, n_out), lambda i: (0, 0))]
        out_shape = [out_shape, SDS((8, n_out), F32)]
    elif res is not None:
        in_specs.append(pl.BlockSpec((tm, n_out), row))
        args.append(res)
    return pl.pallas_call(
        body, name=name, grid=(n_rows // tm,), in_specs=in_specs, out_specs=out_specs, out_shape=out_shape,
        compiler_params=_params(1),
    )(*args)


def _mm_tn(a, b, n_sh, a_sharded, name):
    n_rows = a.shape[0]
    tm = _row_tile(n_rows, TN_TILES)
    ka = a.shape[1] // n_sh if a_sharded else a.shape[1]
    nb = b.shape[1] if a_sharded else b.shape[1] // n_sh
    n_i = n_rows // tm

    def body(a_ref, b_ref, o_ref, acc):
        i = pl.program_id(0)

        @pl.when(i == 0)
        def _():
            acc[...] = jnp.zeros_like(acc)

        for s in range(n_sh):
            a_s = a_ref[:, s * ka:(s + 1) * ka] if a_sharded else a_ref[...]
            b_s = b_ref[...] if a_sharded else b_ref[:, s * nb:(s + 1) * nb]
            acc[s] += lax.dot_general(a_s.astype(BF16), b_s.astype(BF16), (((0,), (0,)), ((), ())),
                                      preferred_element_type=F32)

        @pl.when(i == n_i - 1)
        def _():
            o_ref[...] = acc[...].astype(o_ref.dtype)

    return pl.pallas_call(
        body, name=name, grid=(n_i,),
        in_specs=[pl.BlockSpec((tm, a.shape[1]), lambda i: (i, 0)), pl.BlockSpec((tm, b.shape[1]), lambda i: (i, 0))],
        out_specs=pl.BlockSpec((n_sh, ka, nb), lambda i: (0, 0, 0)),
        out_shape=SDS((n_sh, ka, nb), BF16),
        scratch_shapes=[pltpu.VMEM((n_sh, ka, nb), F32)], compiler_params=_params(1),
    )(a, b)


def _mlp_fwd(h, wn, w_up4, w_down4, layer, name):
    n_rows = h.shape[0]
    TM = _row_tile(n_rows, MLP_FWD_TILES)
    n_sh = w_up4.shape[0]
    f_sh = D_FF // n_sh

    def body(h_ref, wn_ref, wu_ref, wd_ref, o_ref, a_ref, hn_ref, hn_s, acc):
        s = pl.program_id(1)

        @pl.when(s == 0)
        def _():
            hn = _rms(h_ref[...], wn_ref[...]).astype(BF16)
            hn_s[...] = hn
            hn_ref[...] = hn
            acc[...] = jnp.zeros_like(acc)

        a = jnp.dot(hn_s[...], wu_ref[...], preferred_element_type=F32)
        a_ref[...] = a.astype(BF16)
        act = jnp.maximum(a, 0.0)
        acc[...] += jnp.dot((act * act).astype(BF16), wd_ref[...], preferred_element_type=F32)

        @pl.when(s == n_sh - 1)
        def _():
            o_ref[...] = h_ref[...] + acc[...]

    row = lambda i, s: (i, 0)
    return pl.pallas_call(
        body, name=name, grid=(n_rows // TM, n_sh),
        in_specs=[pl.BlockSpec((TM, D_MODEL), row), pl.BlockSpec((1, D_MODEL), lambda i, s: (0, 0)),
                  pl.BlockSpec((None, None, D_MODEL, f_sh), lambda i, s: (s, layer, 0, 0)),
                  pl.BlockSpec((None, None, f_sh, D_MODEL), lambda i, s: (s, layer, 0, 0))],
        out_specs=[pl.BlockSpec((TM, D_MODEL), row), pl.BlockSpec((TM, f_sh), lambda i, s: (i, s)),
                   pl.BlockSpec((TM, D_MODEL), row)],
        out_shape=[SDS((n_rows, D_MODEL), F32), SDS((n_rows, D_FF), BF16), SDS((n_rows, D_MODEL), BF16)],
        scratch_shapes=[pltpu.VMEM((TM, D_MODEL), BF16), pltpu.VMEM((TM, D_MODEL), F32)],
        compiler_params=_params(2),
    )(h, wn, w_up4, w_down4)


def _mlp_bwd_shard(s, dh, a, hn, dhn_prev, h, wn, w_up4, w_down4, dw_up_buf, dw_down_buf, name):
    n_rows = dh.shape[0]
    n_sh = w_up4.shape[0]
    f_sh = D_FF // n_sh
    tm = _row_tile(n_rows, MLP_BWD_TILES)
    n_i = n_rows // tm
    last = h is not None
    nt = (((1,), (1,)), ((), ()))
    tn = (((0,), (0,)), ((), ()))

    def body(*refs):
        refs = list(refs)
        dh_ref, a_ref, hn_ref, wu_ref, wd_ref = refs[:5]
        at = 5
        prev_ref = None
        if dhn_prev is not None:
            prev_ref = refs[at]
            at += 1
        if last:
            h_ref, wn_ref = refs[at:at + 2]
            at += 2
        if dw_up_buf is not None:
            at += 2
        o_ref, dwu_ref, dwd_ref = refs[at:at + 3]
        at += 3
        if last:
            dnorm_ref = refs[at]
            at += 1
        acc_u, acc_d = refs[at:at + 2]
        i = pl.program_id(0)

        @pl.when(i == 0)
        def _():
            acc_u[...] = jnp.zeros_like(acc_u)
            acc_d[...] = jnp.zeros_like(acc_d)
            if last:
                dnorm_ref[...] = jnp.zeros_like(dnorm_ref)

        dh16 = dh_ref[...].astype(BF16)
        r = jnp.maximum(a_ref[...].astype(F32), 0.0)
        dact = lax.dot_general(dh16, wd_ref[...], nt, preferred_element_type=F32)
        da16 = (dact * (2.0 * r)).astype(BF16)
        acc_d[...] += lax.dot_general((r * r).astype(BF16), dh16, tn, preferred_element_type=F32)
        acc_u[...] += lax.dot_general(hn_ref[...], da16, tn, preferred_element_type=F32)
        dhn = lax.dot_general(da16, wu_ref[...], nt, preferred_element_type=F32)
        if prev_ref is not None:
            dhn = dhn + prev_ref[...]
        if last:
            d_rms, dw_rows = _rms_bwd(dhn, h_ref[...], wn_ref[...])
            o_ref[...] = dh_ref[...] + d_rms
            dnorm_ref[...] += _fold8(dw_rows)
        else:
            o_ref[...] = dhn

        @pl.when(i == n_i - 1)
        def _():
            dwu_ref[...] = acc_u[...].astype(BF16)
            dwd_ref[...] = acc_d[...].astype(BF16)

    row = lambda i: (i, 0)
    tile = pl.BlockSpec((tm, D_MODEL), row)
    in_specs = [tile, pl.BlockSpec((tm, f_sh), lambda i: (i, s)), tile,
                pl.BlockSpec((None, None, D_MODEL, f_sh), lambda i: (s, 0, 0, 0)),
                pl.BlockSpec((None, None, f_sh, D_MODEL), lambda i: (s, 0, 0, 0))]
    args = [dh, a, hn, w_up4, w_down4]
    if dhn_prev is not None:
        in_specs.append(tile)
        args.append(dhn_prev)
    if last:
        in_specs += [tile, pl.BlockSpec((1, D_MODEL), lambda i: (0, 0))]
        args += [h, wn]
    aliases = {}
    if dw_up_buf is not None:
        aliases = {len(args): 1, len(args) + 1: 2}
        in_specs += [pl.BlockSpec(memory_space=pl.ANY)] * 2
        args += [dw_up_buf, dw_down_buf]
    out_specs = [tile, pl.BlockSpec((None, D_MODEL, f_sh), lambda i: (s, 0, 0)),
                 pl.BlockSpec((None, f_sh, D_MODEL), lambda i: (s, 0, 0))]
    out_shape = [SDS((n_rows, D_MODEL), F32), SDS((n_sh, D_MODEL, f_sh), BF16), SDS((n_sh, f_sh, D_MODEL), BF16)]
    if last:
        out_specs.append(pl.BlockSpec((8, D_MODEL), lambda i: (0, 0)))
        out_shape.append(SDS((8, D_MODEL), F32))
    return pl.pallas_call(
        body, name=name, grid=(n_i,), in_specs=in_specs, out_specs=out_specs, out_shape=out_shape,
        input_output_aliases=aliases,
        scratch_shapes=[pltpu.VMEM((D_MODEL, f_sh), F32), pltpu.VMEM((f_sh, D_MODEL), F32)],
        compiler_params=_params(1),
    )(*args)


def _attn_masks(n):
    qi = lax.broadcasted_iota(jnp.int32, (BLOCK, 3 * BLOCK), 0)
    col = lax.broadcasted_iota(jnp.int32, (BLOCK, 3 * BLOCK), 1)
    kj = col - BLOCK
    dist = BLOCK + qi - kj
    kmin = jnp.where(n == 0, 2 * BLOCK, jnp.where(n == 1, BLOCK, 0))
    band_ok = (col >= BLOCK) & (dist >= 0) & (dist < BLOCK) & (kj >= kmin)
    q_pos = n * BLOCK + qi - PAD
    meta_ok = (col >= PAD) & (col < BLOCK) & (col - PAD <= q_pos)
    distf = jnp.where(col >= BLOCK, dist, 0).astype(F32)
    return band_ok | meta_ok, distf


def _alibi_slope(h):
    return float(2.0 ** (-8.0 * (h + 1) / N_HEADS))


def _attn_fwd(qkv, sinks, n_ex, nb):
    n_rows = qkv.shape[0]
    kvb = N_HEADS * HEAD_DIM // KV_DIM

    def body(sink_ref, q_ref, kvm_ref, kvp_ref, kvc_ref, o_ref, lse_ref, k_s, v_s, q_s):
        n = pl.program_id(1)
        ok, distf = _attn_masks(n)
        v_s[...] = jnp.ones_like(v_s)
        for part, ref in enumerate((kvm_ref, kvp_ref, kvc_ref)):
            rows = slice(part * BLOCK, (part + 1) * BLOCK)
            k_s[rows, :] = ref[:, 0:N_KV * HEAD_DIM]
            for kv in range(N_KV):
                v_s[rows, kv * 2 * HEAD_DIM:kv * 2 * HEAD_DIM + HEAD_DIM] = \
                    ref[:, (N_KV + kv) * HEAD_DIM:(N_KV + kv + 1) * HEAD_DIM]
        for kv in range(N_KV):
            for g in range(GQA):
                h = kv * GQA + g
                q_s[kv, g * BLOCK:(g + 1) * BLOCK, :] = q_ref[:, h * HEAD_DIM:(h + 1) * HEAD_DIM]
            s4 = lax.dot_general(q_s[kv], k_s[:, kv * HEAD_DIM:(kv + 1) * HEAD_DIM], (((1,), (1,)), ((), ())),
                                 preferred_element_type=F32) * (HEAD_DIM ** -0.5)
            es, ms, sink_es = [], [], []
            for g in range(GQA):
                h = kv * GQA + g
                s = jnp.where(ok, s4[g * BLOCK:(g + 1) * BLOCK] - _alibi_slope(h) * distf, NEG_INF)
                sink = sink_ref[0, h]
                m = jnp.maximum(jnp.max(s, axis=-1, keepdims=True), sink)
                es.append(jnp.exp(s - m).astype(BF16))
                ms.append(m)
                sink_es.append(jnp.exp(sink - m))
            pv = jnp.dot(jnp.concatenate(es, axis=0), v_s[:, kv * 2 * HEAD_DIM:(kv + 1) * 2 * HEAD_DIM],
                         preferred_element_type=F32)
            for g in range(GQA):
                h = kv * GQA + g
                pg = pv[g * BLOCK:(g + 1) * BLOCK]
                l = pg[:, HEAD_DIM:HEAD_DIM + 1] + sink_es[g]
                o_ref[:, h * HEAD_DIM:(h + 1) * HEAD_DIM] = (pg[:, 0:HEAD_DIM] / l).astype(BF16)
                lse_ref[:, h:h + 1] = ms[g] + jnp.log(l)

    return pl.pallas_call(
        body, name="attn_fwd", grid=(n_ex, nb),
        in_specs=[pl.BlockSpec(memory_space=pltpu.SMEM),
                  pl.BlockSpec((BLOCK, N_HEADS * HEAD_DIM), lambda b, n: (b * nb + n, 0)),
                  pl.BlockSpec((BLOCK, KV_DIM), lambda b, n: (b * nb, kvb)),
                  pl.BlockSpec((BLOCK, KV_DIM), lambda b, n: (b * nb + jnp.maximum(n - 1, 0), kvb)),
                  pl.BlockSpec((BLOCK, KV_DIM), lambda b, n: (b * nb + n, kvb))],
        out_specs=[pl.BlockSpec((BLOCK, N_HEADS * HEAD_DIM), lambda b, n: (b * nb + n, 0)),
                   pl.BlockSpec((BLOCK, N_HEADS), lambda b, n: (b * nb + n, 0))],
        out_shape=[SDS((n_rows, N_HEADS * HEAD_DIM), BF16), SDS((n_rows, N_HEADS), F32)],
        scratch_shapes=[pltpu.VMEM((3 * BLOCK, N_KV * HEAD_DIM), BF16), pltpu.VMEM((3 * BLOCK, 2 * N_KV * HEAD_DIM), BF16),
                        pltpu.VMEM((N_KV, GQA * BLOCK, HEAD_DIM), BF16)],
        compiler_params=_params(2),
    )(sinks, qkv, qkv, qkv, qkv)


def _attn_bwd(qkv, sinks, o, lse, do, n_ex, nb):
    n_rows = qkv.shape[0]
    kvb = N_HEADS * HEAD_DIM // KV_DIM
    scale = HEAD_DIM ** -0.5
    nq = lambda r: nb - 1 - r

    def body(sink_ref, q_ref, kvm_ref, kvp_ref, kvc_ref, o_ref, lse_ref, do_ref, dqkv_ref, dsink_ref,
             k_s, v_s, dkv_s, carry_s, meta_s, q_s, do_s):
        b, r = pl.program_id(0), pl.program_id(1)
        n = nq(r)
        ok, distf = _attn_masks(n)

        @pl.when((b == 0) & (r == 0))
        def _():
            dsink_ref[...] = jnp.zeros_like(dsink_ref)

        @pl.when(r == 0)
        def _():
            carry_s[...] = jnp.zeros_like(carry_s)
            meta_s[...] = jnp.zeros_like(meta_s)

        for part, ref in enumerate((kvm_ref, kvp_ref, kvc_ref)):
            k_s[part * BLOCK:(part + 1) * BLOCK, :] = ref[:, 0:N_KV * HEAD_DIM]
            v_s[part * BLOCK:(part + 1) * BLOCK, :] = ref[:, N_KV * HEAD_DIM:KV_DIM]
        nt = (((1,), (1,)), ((), ()))
        tn = (((0,), (0,)), ((), ()))
        for kv in range(N_KV):
            kcols = slice(kv * HEAD_DIM, (kv + 1) * HEAD_DIM)
            vcols = slice(N_KV * HEAD_DIM + kv * HEAD_DIM, N_KV * HEAD_DIM + (kv + 1) * HEAD_DIM)
            for g in range(GQA):
                cols = slice((kv * GQA + g) * HEAD_DIM, (kv * GQA + g + 1) * HEAD_DIM)
                q_s[kv, g * BLOCK:(g + 1) * BLOCK, :] = q_ref[:, cols]
                do_s[kv, g * BLOCK:(g + 1) * BLOCK, :] = do_ref[:, cols]
            kh, vh = k_s[:, kcols], v_s[:, kcols]
            s4 = lax.dot_general(q_s[kv], kh, nt, preferred_element_type=F32) * scale
            dp4 = lax.dot_general(do_s[kv], vh, nt, preferred_element_type=F32)
            ps, dss = [], []
            for g in range(GQA):
                h = kv * GQA + g
                cols = slice(h * HEAD_DIM, (h + 1) * HEAD_DIM)
                rows = slice(g * BLOCK, (g + 1) * BLOCK)
                s = jnp.where(ok, s4[rows] - _alibi_slope(h) * distf, NEG_INF)
                lse_h = lse_ref[:, h:h + 1]
                p = jnp.exp(s - lse_h)
                delta = jnp.sum(do_ref[:, cols].astype(F32) * o_ref[:, cols].astype(F32), axis=-1, keepdims=True)
                dsink_ref[:, h:h + 1] += -jnp.exp(sink_ref[0, h] - lse_h) * delta
                ps.append(p.astype(BF16))
                dss.append((p * (dp4[rows] - delta)).astype(BF16))
            p4, ds4 = jnp.concatenate(ps, axis=0), jnp.concatenate(dss, axis=0)
            dq4 = jnp.dot(ds4, kh, preferred_element_type=F32) * scale
            for g in range(GQA):
                cols = slice((kv * GQA + g) * HEAD_DIM, (kv * GQA + g + 1) * HEAD_DIM)
                dqkv_ref[:, cols] = dq4[g * BLOCK:(g + 1) * BLOCK].astype(BF16)
            dkv_s[:, kcols] = lax.dot_general(ds4, q_s[kv], tn, preferred_element_type=F32) * scale
            dkv_s[:, vcols] = lax.dot_general(p4, do_s[kv], tn, preferred_element_type=F32)

        meta_s[...] += dkv_s[0:BLOCK, :]
        cur = dkv_s[2 * BLOCK:3 * BLOCK, :] + carry_s[...]
        carry_s[...] = dkv_s[BLOCK:2 * BLOCK, :]

        @pl.when(n > 0)
        def _():
            dqkv_ref[:, N_HEADS * HEAD_DIM:QKV_DIM] = cur.astype(BF16)

        @pl.when(n == 0)
        def _():
            dqkv_ref[:, N_HEADS * HEAD_DIM:QKV_DIM] = (cur + meta_s[...]).astype(BF16)

    blk = lambda b, r: (b * nb + nq(r), 0)
    return pl.pallas_call(
        body, name="attn_bwd", grid=(n_ex, nb),
        in_specs=[pl.BlockSpec(memory_space=pltpu.SMEM),
                  pl.BlockSpec((BLOCK, N_HEADS * HEAD_DIM), blk),
                  pl.BlockSpec((BLOCK, KV_DIM), lambda b, r: (b * nb, kvb)),
                  pl.BlockSpec((BLOCK, KV_DIM), lambda b, r: (b * nb + jnp.maximum(nq(r) - 1, 0), kvb)),
                  pl.BlockSpec((BLOCK, KV_DIM), lambda b, r: (b * nb + nq(r), kvb)),
                  pl.BlockSpec((BLOCK, N_HEADS * HEAD_DIM), blk),
                  pl.BlockSpec((BLOCK, N_HEADS), blk),
                  pl.BlockSpec((BLOCK, N_HEADS * HEAD_DIM), blk)],
        out_specs=[pl.BlockSpec((BLOCK, QKV_DIM), blk),
                   pl.BlockSpec((BLOCK, N_HEADS), lambda b, r: (0, 0))],
        out_shape=[SDS((n_rows, QKV_DIM), BF16), SDS((BLOCK, N_HEADS), F32)],
        scratch_shapes=[pltpu.VMEM((3 * BLOCK, N_KV * HEAD_DIM), BF16), pltpu.VMEM((3 * BLOCK, N_KV * HEAD_DIM), BF16),
                        pltpu.VMEM((3 * BLOCK, KV_DIM), F32), pltpu.VMEM((BLOCK, KV_DIM), F32),
                        pltpu.VMEM((BLOCK, KV_DIM), F32), pltpu.VMEM((N_KV, GQA * BLOCK, HEAD_DIM), BF16),
                        pltpu.VMEM((N_KV, GQA * BLOCK, HEAD_DIM), BF16)],
        compiler_params=_params(2),
    )(sinks, qkv, qkv, qkv, qkv, o, lse, do)


SEG = TM // 8


def _cmul_add(xr, xi, mr, mi, sr, si):
    return xr + mr * sr - mi * si, xi + mr * si + mi * sr


def _to_segments(src_ref, dst):
    for s in range(SEG):
        dst[s * 8:(s + 1) * 8, :] = src_ref[pl.ds(s, 8, stride=SEG), :]


def _from_segments(src, i):
    return src[pl.ds(i, SEG, stride=8), :]


def _scan_segments(buf, tab_ref, pw_ref, carry_s, reverse):
    shifts = (7, 6, 4) if reverse else (1, 2, 4)
    row_id = lax.broadcasted_iota(jnp.int32, (8, 128), 0)

    def local(si, prev):
        s = (SEG - 1 - si) if reverse else si
        row = pl.multiple_of(s * 8, 8)
        out = []
        for j in range(PAIRS_PER_CHUNK):
            xr, xi = _cmul_add(buf[j, pl.ds(row, 8), 0:128], buf[j, pl.ds(row, 8), 128:256],
                               tab_ref[j, 0], tab_ref[j, 1], prev[2 * j], prev[2 * j + 1])
            buf[j, pl.ds(row, 8), 0:128] = xr
            buf[j, pl.ds(row, 8), 128:256] = xi
            out += [xr, xi]
        return tuple(out)

    zero = jnp.zeros((8, 128), F32)
    edge = lax.fori_loop(0, SEG, local, (zero,) * (2 * PAIRS_PER_CHUNK))

    entering = []
    for j in range(PAIRS_PER_CHUNK):
        er, ei = edge[2 * j], edge[2 * j + 1]
        if reverse:
            sr = jnp.where(row_id == 7, carry_s[2 * j], pltpu.roll(er, 7, 0))
            si_ = jnp.where(row_id == 7, carry_s[2 * j + 1], pltpu.roll(ei, 7, 0))
        else:
            sr = jnp.where(row_id == 0, carry_s[2 * j], pltpu.roll(er, 1, 0))
            si_ = jnp.where(row_id == 0, carry_s[2 * j + 1], pltpu.roll(ei, 1, 0))
        for lvl, sh in enumerate(shifts):
            sr, si_ = _cmul_add(sr, si_, tab_ref[j, 2 + 2 * lvl], tab_ref[j, 3 + 2 * lvl],
                                pltpu.roll(sr, sh, 0), pltpu.roll(si_, sh, 0))
        entering += [sr, si_]
        tr, ti = _cmul_add(er, ei, tab_ref[j, 2], tab_ref[j, 3], sr, si_)
        out_row = slice(0, 1) if reverse else slice(7, 8)
        carry_s[2 * j] = jnp.broadcast_to(tr[out_row], (8, 128))
        carry_s[2 * j + 1] = jnp.broadcast_to(ti[out_row], (8, 128))

    def fix(s, _):
        row = pl.multiple_of(s * 8, 8)
        for j in range(PAIRS_PER_CHUNK):
            xr, xi = _cmul_add(buf[j, pl.ds(row, 8), 0:128], buf[j, pl.ds(row, 8), 128:256],
                               pw_ref[j, 0, s], pw_ref[j, 1, s], entering[2 * j], entering[2 * j + 1])
            buf[j, pl.ds(row, 8), 0:128] = xr
            buf[j, pl.ds(row, 8), 128:256] = xi
        return 0

    lax.fori_loop(0, SEG, fix, 0)


def _ssm_fwd(u, b_pad, c_pad, tab, pw, d_skip, n_ex, lp):
    n_rows = u.shape[0]
    n_t = lp // TM
    n_chunk = D_MODEL // 128

    def body(u_ref, bp_ref, cp_ref, tab_ref, pw_ref, d_ref, yg_ref, y_ref, xs_ref, buf, carry_s, us, ys):
        @pl.when(pl.program_id(2) == 0)
        def _():
            carry_s[...] = jnp.zeros_like(carry_s)

        _to_segments(u_ref, us)
        ub = us[...]
        u16 = ub.astype(BF16)
        for j in range(PAIRS_PER_CHUNK):
            buf[j] = jnp.dot(u16, bp_ref[j], preferred_element_type=F32)
        _scan_segments(buf, tab_ref, pw_ref, carry_s, reverse=False)
        y = d_ref[...] * ub
        for j in range(PAIRS_PER_CHUNK):
            xb = buf[j].astype(BF16)
            xs_ref[j] = xb
            y = y + jnp.dot(xb, cp_ref[j], preferred_element_type=F32)
        ys[...] = y
        for i in range(8):
            yi = _from_segments(ys, i)
            y_ref[i * SEG:(i + 1) * SEG, :] = yi
            yg_ref[i * SEG:(i + 1) * SEG, :] = _gelu(yi).astype(BF16)

    rows = lambda b, q, t: (b * n_t + t, q)
    return pl.pallas_call(
        body, name="ssm_fwd", grid=(n_ex, n_chunk, n_t),
        in_specs=[pl.BlockSpec((TM, 128), rows),
                  pl.BlockSpec((PAIRS_PER_CHUNK, 128, 256), lambda b, q, t: (q, 0, 0)),
                  pl.BlockSpec((PAIRS_PER_CHUNK, 256, 128), lambda b, q, t: (q, 0, 0)),
                  pl.BlockSpec((PAIRS_PER_CHUNK, 8, 8, 128), lambda b, q, t: (q, 0, 0, 0)),
                  pl.BlockSpec((PAIRS_PER_CHUNK, 2, SEG, 8, 128), lambda b, q, t: (q, 0, 0, 0, 0)),
                  pl.BlockSpec((1, 128), lambda b, q, t: (0, q))],
        out_specs=[pl.BlockSpec((TM, 128), rows), pl.BlockSpec((TM, 128), rows),
                   pl.BlockSpec((PAIRS_PER_CHUNK, TM, 256), lambda b, q, t: (q, b * n_t + t, 0))],
        out_shape=[SDS((n_rows, D_MODEL), BF16), SDS((n_rows, D_MODEL), F32), SDS((N_PAIR, n_rows, 256), BF16)],
        scratch_shapes=[pltpu.VMEM((PAIRS_PER_CHUNK, TM, 256), F32), pltpu.VMEM((2 * PAIRS_PER_CHUNK, 8, 128), F32),
                        pltpu.VMEM((TM, 128), F32), pltpu.VMEM((TM, 128), F32)],
        compiler_params=_params(3),
    )(u, b_pad, c_pad, tab, pw, d_skip)


def _ssm_bwd(dyg, y, u, xs, ct_pad, bt_pad, tab_rev, pw_rev, d_skip, n_ex, lp):
    n_rows = u.shape[0]
    n_t = lp // TM
    n_chunk = D_MODEL // 128
    tile = lambda q, b, t: (b * n_t + (n_t - 1 - t), q)

    def body(dyg_ref, y_ref, u_ref, xs_ref, xp_ref, ct_ref, bt_ref, tab_ref, pw_ref, d_ref,
             du_ref, db_ref, dc_ref, da_ref, dd_ref, buf, xf, carry_s, us, dys, dyp):
        b, t = pl.program_id(1), pl.program_id(2)

        @pl.when((b == 0) & (t == 0))
        def _():
            db_ref[...] = jnp.zeros_like(db_ref)
            dc_ref[...] = jnp.zeros_like(dc_ref)
            da_ref[...] = jnp.zeros_like(da_ref)
            dd_ref[...] = jnp.zeros_like(dd_ref)

        @pl.when(t == 0)
        def _():
            carry_s[...] = jnp.zeros_like(carry_s)

        dys[...] = dyg_ref[...].astype(F32) * _gelu_grad(y_ref[...])
        dd_ref[...] += _fold8(dys[...] * u_ref[...])
        _to_segments(dys, dyp)
        dy = dyp[...]
        _to_segments(u_ref, us)
        dy16 = dy.astype(BF16)
        first_tile = t == n_t - 1
        for j in range(PAIRS_PER_CHUNK):
            buf[j] = jnp.dot(dy16, ct_ref[j], preferred_element_type=F32)
            dc_ref[j] += lax.dot_general(dy16, xs_ref[j], (((0,), (0,)), ((), ())), preferred_element_type=F32)
        du = d_ref[...] * dy
        for j in range(PAIRS_PER_CHUNK):
            xf[j, 16:16 + TM, :] = xs_ref[j].astype(F32)
            xf[j, 0:16, :] = jnp.where(first_tile, 0.0, xp_ref[j].astype(F32))
        _scan_segments(buf, tab_ref, pw_ref, carry_s, reverse=True)
        u16 = us[...].astype(BF16)
        row_id = lax.broadcasted_iota(jnp.int32, (8, 256), 0)
        for j in range(PAIRS_PER_CHUNK):
            g = buf[j]
            g16 = g.astype(BF16)
            du = du + jnp.dot(g16, bt_ref[j], preferred_element_type=F32)
            db_ref[j] += lax.dot_general(u16, g16, (((0,), (0,)), ((), ())), preferred_element_type=F32)
            first = jnp.where(row_id == 0, jnp.broadcast_to(xf[j, 15:16, :], (8, 256)),
                              pltpu.roll(xf[j, 8 + TM:16 + TM, :], 1, 0))
            for g_rows, x_prev in ((g[0:8], first), (g[8:TM], xf[j, 16:8 + TM, :])):
                gr, gi = g_rows[:, 0:128], g_rows[:, 128:256]
                pr, pi = x_prev[:, 0:128], x_prev[:, 128:256]
                da_ref[j, 0] += _fold8(gr * pr + gi * pi)
                da_ref[j, 1] += _fold8(gi * pr - gr * pi)
        dys[...] = du
        for i in range(8):
            du_ref[i * SEG:(i + 1) * SEG, :] = _from_segments(dys, i)

    prev16 = lambda q, b, t: (q, jnp.maximum((b * n_t + (n_t - 1 - t)) * (TM // 16) - 1, 0), 0)
    return pl.pallas_call(
        body, name="ssm_bwd", grid=(n_chunk, n_ex, n_t),
        in_specs=[pl.BlockSpec((TM, 128), tile), pl.BlockSpec((TM, 128), tile), pl.BlockSpec((TM, 128), tile),
                  pl.BlockSpec((PAIRS_PER_CHUNK, TM, 256), lambda q, b, t: (q, b * n_t + (n_t - 1 - t), 0)),
                  pl.BlockSpec((PAIRS_PER_CHUNK, 16, 256), prev16),
                  pl.BlockSpec((PAIRS_PER_CHUNK, 128, 256), lambda q, b, t: (q, 0, 0)),
                  pl.BlockSpec((PAIRS_PER_CHUNK, 256, 128), lambda q, b, t: (q, 0, 0)),
                  pl.BlockSpec((PAIRS_PER_CHUNK, 8, 8, 128), lambda q, b, t: (q, 0, 0, 0)),
                  pl.BlockSpec((PAIRS_PER_CHUNK, 2, SEG, 8, 128), lambda q, b, t: (q, 0, 0, 0, 0)),
                  pl.BlockSpec((1, 128), lambda q, b, t: (0, q))],
        out_specs=[pl.BlockSpec((TM, 128), tile),
                   pl.BlockSpec((PAIRS_PER_CHUNK, 128, 256), lambda q, b, t: (q, 0, 0)),
                   pl.BlockSpec((PAIRS_PER_CHUNK, 128, 256), lambda q, b, t: (q, 0, 0)),
                   pl.BlockSpec((PAIRS_PER_CHUNK, 2, 8, 128), lambda q, b, t: (q, 0, 0, 0)),
                   pl.BlockSpec((8, 128), lambda q, b, t: (0, q))],
        out_shape=[SDS((n_rows, D_MODEL), F32), SDS((N_PAIR, 128, 256), F32), SDS((N_PAIR, 128, 256), F32),
                   SDS((N_PAIR, 2, 8, 128), F32), SDS((8, D_MODEL), F32)],
        scratch_shapes=[pltpu.VMEM((PAIRS_PER_CHUNK, TM, 256), F32), pltpu.VMEM((PAIRS_PER_CHUNK, TM + 16, 256), F32),
                        pltpu.VMEM((2 * PAIRS_PER_CHUNK, 8, 128), F32), pltpu.VMEM((TM, 128), F32),
                        pltpu.VMEM((TM, 128), F32), pltpu.VMEM((TM, 128), F32)],
        compiler_params=_params(3),
    )(dyg, y, u, xs, xs, ct_pad, bt_pad, tab_rev, pw_rev, d_skip)


def _rms_fwd(h, wn, name):
    n_rows = h.shape[0]

    def body(h_ref, wn_ref, o_ref):
        o_ref[...] = _rms(h_ref[...], wn_ref[...])

    return pl.pallas_call(
        body, name=name, grid=(n_rows // TM,),
        in_specs=[pl.BlockSpec((TM, D_MODEL), lambda i: (i, 0)), pl.BlockSpec((1, D_MODEL), lambda i: (0, 0))],
        out_specs=pl.BlockSpec((TM, D_MODEL), lambda i: (i, 0)),
        out_shape=SDS((n_rows, D_MODEL), F32), compiler_params=_params(1),
    )(h, wn)


def _rms_bwd_call(dhn, h, wn, dres, name):
    n_rows = h.shape[0]

    def body(dhn_ref, h_ref, wn_ref, dres_ref, o_ref, dw_ref):
        @pl.when(pl.program_id(0) == 0)
        def _():
            dw_ref[...] = jnp.zeros_like(dw_ref)

        dh, dw_rows = _rms_bwd(dhn_ref[...], h_ref[...], wn_ref[...])
        o_ref[...] = dres_ref[...] + dh
        dw_ref[...] += _fold8(dw_rows)

    row = lambda i: (i, 0)
    return pl.pallas_call(
        body, name=name, grid=(n_rows // TM,),
        in_specs=[pl.BlockSpec((TM, D_MODEL), row), pl.BlockSpec((TM, D_MODEL), row),
                  pl.BlockSpec((1, D_MODEL), lambda i: (0, 0)), pl.BlockSpec((TM, D_MODEL), row)],
        out_specs=[pl.BlockSpec((TM, D_MODEL), row), pl.BlockSpec((8, D_MODEL), lambda i: (0, 0))],
        out_shape=[SDS((n_rows, D_MODEL), F32), SDS((8, D_MODEL), F32)], compiler_params=_params(1),
    )(dhn, h, wn, dres)


def _glu_fwd(h, z):
    n_rows = h.shape[0]

    def body(h_ref, val_ref, gate_ref, o_ref):
        o_ref[...] = h_ref[...] + val_ref[...].astype(F32) * jax.nn.sigmoid(gate_ref[...].astype(F32))

    row = lambda i: (i, 0)
    return pl.pallas_call(
        body, name="glu_fwd", grid=(n_rows // TM,),
        in_specs=[pl.BlockSpec((TM, D_MODEL), row), pl.BlockSpec((TM, D_MODEL), row),
                  pl.BlockSpec((TM, D_MODEL), lambda i: (i, 1))],
        out_specs=pl.BlockSpec((TM, D_MODEL), row),
        out_shape=SDS((n_rows, D_MODEL), F32), compiler_params=_params(1),
    )(h, z, z)


def _glu_bwd(dh, z):
    n_rows = dh.shape[0]

    def body(dh_ref, val_ref, gate_ref, dz_ref):
        sg = jax.nn.sigmoid(gate_ref[...].astype(F32))
        d = dh_ref[...]
        dz_ref[:, 0:D_MODEL] = (d * sg).astype(BF16)
        dz_ref[:, D_MODEL:2 * D_MODEL] = (d * val_ref[...].astype(F32) * sg * (1.0 - sg)).astype(BF16)

    row = lambda i: (i, 0)
    return pl.pallas_call(
        body, name="glu_bwd", grid=(n_rows // TM,),
        in_specs=[pl.BlockSpec((TM, D_MODEL), row), pl.BlockSpec((TM, D_MODEL), row),
                  pl.BlockSpec((TM, D_MODEL), lambda i: (i, 1))],
        out_specs=pl.BlockSpec((TM, 2 * D_MODEL), row),
        out_shape=SDS((n_rows, 2 * D_MODEL), BF16), compiler_params=_params(1),
    )(dh, z, z)


def _loss_head(h, wn, target, n_ex, nb):
    n_rows = h.shape[0]

    def body(h_ref, wn_ref, t_ref, dh_ref, loss_ref, dw_ref):
        b, n = pl.program_id(0), pl.program_id(1)

        @pl.when((b == 0) & (n == 0))
        def _():
            loss_ref[...] = jnp.zeros_like(loss_ref)
            dw_ref[...] = jnp.zeros_like(dw_ref)

        @pl.when(n == 0)
        def _():
            dh_ref[...] = jnp.zeros_like(dh_ref)

        @pl.when(n > 0)
        def _():
            hh = h_ref[...]
            diff = _rms(hh, wn_ref[...]) - t_ref[...]
            loss_ref[...] += 0.5 * jnp.sum(diff * diff) * (1.0 / D_MODEL)
            dh, dw_rows = _rms_bwd(diff * (1.0 / D_MODEL), hh, wn_ref[...])
            dh_ref[...] = dh
            dw_ref[...] += _fold8(dw_rows)

    return pl.pallas_call(
        body, name="loss_head", grid=(n_ex, nb),
        in_specs=[pl.BlockSpec((BLOCK, D_MODEL), lambda b, n: (b * nb + n, 0)),
                  pl.BlockSpec((1, D_MODEL), lambda b, n: (0, 0)),
                  pl.BlockSpec((BLOCK, D_MODEL), lambda b, n: (b * (nb - 1) + jnp.maximum(n - 1, 0), 0))],
        out_specs=[pl.BlockSpec((BLOCK, D_MODEL), lambda b, n: (b * nb + n, 0)),
                   pl.BlockSpec((8, 128), lambda b, n: (0, 0)),
                   pl.BlockSpec((8, D_MODEL), lambda b, n: (0, 0))],
        out_shape=[SDS((n_rows, D_MODEL), F32), SDS((8, 128), F32), SDS((8, D_MODEL), F32)],
        compiler_params=_params(2),
    )(h, wn, target)


def _adamw(pieces, w, m, v, name):
    n_layers = len(pieces)
    rows, cols = pieces[0].shape[1:]
    rb = rows
    for cand in (256, 136, 128, 64, 32, 16, 8):
        if rows % cand == 0 and rows > cand:
            rb = cand
            break
    n_blk = rows // rb
    c1 = 1.0 / (1.0 - ADAM_B1 ** ADAM_STEP)
    c2 = 1.0 / (1.0 - ADAM_B2 ** ADAM_STEP)

    def body(*refs):
        p_refs = refs[:n_layers]
        w_ref, m_ref, v_ref, g_out, d_out, m_out, v_out = refs[n_layers:]
        layer = pl.program_id(0)
        g = None
        for l, p_ref in enumerate(p_refs):
            gl = p_ref[0].astype(F32)
            for k in range(1, N_DEV):
                gl = gl + p_ref[k].astype(F32)
            g = gl if g is None else jnp.where(layer == l, gl, g)
        m_new = ADAM_B1 * m_ref[...] + (1.0 - ADAM_B1) * g
        v_new = ADAM_B2 * v_ref[...] + (1.0 - ADAM_B2) * (g * g)
        g_out[...] = g
        m_out[...] = m_new
        v_out[...] = v_new
        d_out[...] = -ADAM_LR * ((m_new * c1) / (jnp.sqrt(v_new * c2) + ADAM_EPS) + ADAM_WD * w_ref[...])

    def piece_spec(l):
        return pl.BlockSpec((N_DEV, rb, cols), lambda ly, i: (0, jnp.where(ly == l, i, 0), 0))

    blk = pl.BlockSpec((rb, cols), lambda ly, i: (ly * n_blk + i, 0))
    return pl.pallas_call(
        body, name=name, grid=(n_layers, n_blk),
        in_specs=[piece_spec(l) for l in range(n_layers)] + [blk, blk, blk],
        out_specs=[blk, blk, blk, blk],
        out_shape=[SDS((n_layers * rows, cols), F32)] * 4, compiler_params=_params(2),
    )(*pieces, w, m, v)


_HBM = pl.BlockSpec(memory_space=pltpu.HBM)
_SEM = pl.BlockSpec(memory_space=pltpu.SEMAPHORE)
_EFFECT = pltpu.SideEffectType.DATAFLOW_SIDE_EFFECTING
N_GATHER_PEERS = N_CHIPS - 1
N_EXCHANGE_PEERS = N_DEV - 1


def _gather_copies(srcs, lands, send_sems, recv_sems):
    x, y, c = lax.axis_index("x"), lax.axis_index("y"), lax.axis_index("c")
    mine = 2 * x + y
    chips = [(1 - x, y), (x, 1 - y), (1 - x, 1 - y)]
    out, inc = [], []
    for a in range(len(srcs)):
        for k, (px, py) in enumerate(chips):
            j = a * N_GATHER_PEERS + k
            sems = dict(send_sem=send_sems.at[j], recv_sem=recv_sems.at[j], device_id=(px, py, c),
                        device_id_type=pl.DeviceIdType.MESH)
            out.append(pltpu.make_async_remote_copy(src_ref=srcs[a], dst_ref=lands[a].at[mine], **sems))
            inc.append(pltpu.make_async_remote_copy(src_ref=srcs[a], dst_ref=lands[a].at[2 * px + py], **sems))
    return out, inc


def _exchange_copies(n_scatter):
    def copies(srcs, lands, send_sems, recv_sems):
        x, y, c = lax.axis_index("x"), lax.axis_index("y"), lax.axis_index("c")
        me = 4 * x + 2 * y + c
        peers = [(x ^ (k >> 2), y ^ ((k >> 1) & 1), c ^ (k & 1)) for k in range(1, N_DEV)]
        out, inc = [], []
        for a in range(len(srcs)):
            for k, (px, py, pc) in enumerate(peers):
                j = a * N_EXCHANGE_PEERS + k
                sems = dict(send_sem=send_sems.at[j], recv_sem=recv_sems.at[j], device_id=(px, py, pc),
                            device_id_type=pl.DeviceIdType.MESH)
                theirs = srcs[a].at[2 * px + py] if a < n_scatter else srcs[a]
                mine = srcs[a].at[2 * x + y] if a < n_scatter else srcs[a]
                out.append(pltpu.make_async_remote_copy(src_ref=theirs, dst_ref=lands[a].at[me], **sems))
                inc.append(pltpu.make_async_remote_copy(src_ref=mine, dst_ref=lands[a].at[4 * px + 2 * py + pc], **sems))
        return out, inc

    return copies


def _split_start(groups, copies_fn, n_peers, name):
    sizes = [len(srcs) for srcs, _ in groups]
    flat = [a for srcs, lands in groups for a in list(srcs) + list(lands)]
    n_flat, n_grp = len(flat), len(groups)

    def body(*refs):
        sems = refs[2 * n_flat:2 * n_flat + 2 * n_grp]
        token = refs[-1]
        at = 0
        for gi, n in enumerate(sizes):
            out, _ = copies_fn(refs[at:at + n], refs[at + n:at + 2 * n], sems[2 * gi], sems[2 * gi + 1])
            for cp in out:
                cp.start()
            at += 2 * n
        token[...] = jnp.zeros_like(token)

    sem_shapes = []
    for n in sizes:
        sem_shapes += [pltpu.SemaphoreType.DMA((n * n_peers,)), pltpu.SemaphoreType.DMA((n * n_peers,))]
    res = pl.pallas_call(
        body, name=name,
        out_shape=(*[pltpu.HBM(a.shape, a.dtype) for a in flat], *sem_shapes, SDS((8, 128), F32)),
        in_specs=[_HBM] * n_flat,
        out_specs=(*[_HBM] * n_flat, *[_SEM] * (2 * n_grp), pl.BlockSpec(memory_space=pltpu.VMEM)),
        input_output_aliases={i: i for i in range(n_flat)},
        compiler_params=pltpu.CompilerParams(has_side_effects=_EFFECT),
    )(*[pltpu.with_memory_space_constraint(a, pltpu.HBM) for a in flat])
    handles, at = [], 0
    for gi, n in enumerate(sizes):
        handles.append((res[n_flat + 2 * gi], res[n_flat + 2 * gi + 1], list(res[at:at + n]), list(res[at + n:at + 2 * n])))
        at += 2 * n
    return handles, res[-1]


def _split_wait(handle, after, copies_fn, name):
    send_sems, recv_sems, srcs, lands = handle
    n = len(srcs)
    after = list(after) if isinstance(after, (list, tuple)) else [after]

    def body(*refs):
        out, inc = copies_fn(refs[:n], refs[n:2 * n], refs[2 * n], refs[2 * n + 1])
        for cp in out:
            cp.wait_send()
        for cp in inc:
            cp.wait_recv()

    flat = list(srcs) + list(lands)
    res = pl.pallas_call(
        body, name=name,
        out_shape=tuple(pltpu.HBM(a.shape, a.dtype) for a in flat),
        in_specs=[_HBM] * (2 * n) + [_SEM, _SEM] + [pl.BlockSpec(memory_space=pl.ANY)] * len(after),
        out_specs=tuple([_HBM] * (2 * n)),
        input_output_aliases={i: i for i in range(2 * n)},
        compiler_params=pltpu.CompilerParams(has_side_effects=_EFFECT),
    )(*flat, send_sems, recv_sems, *after)
    return list(res[n:])


def _landing(own, slot, n_slots):
    return lax.dynamic_update_index_in_dim(lax.empty((n_slots,) + own.shape, own.dtype), own, slot, 0)


def _ssm_discretize(lam_re, lam_im, log_dt, b_re, b_im):
    lr = jnp.minimum(lam_re, LAMBDA_RE_MAX)
    li = lam_im
    dt = jnp.exp(log_dt)[:, None]
    mag = jnp.exp(lr * dt)
    ar, ai = mag * jnp.cos(li * dt), mag * jnp.sin(li * dt)
    den = lr * lr + li * li
    nr, ni = ar - 1.0, ai
    gr, gi = (nr * lr + ni * li) / den, (ni * lr - nr * li) / den
    bbr = gr[:, :, None] * b_re - gi[:, :, None] * b_im
    bbi = gr[:, :, None] * b_im + gi[:, :, None] * b_re
    return ar, ai, bbr, bbi


def _pair_lanes(t):
    return t.reshape(N_PAIR, 2 * SSM_STATE)


def _chan_state_blocks(t_gcp):
    t = t_gcp.reshape(N_PAIR, 2, SSM_GROUP, SSM_STATE)
    eye2 = jnp.eye(2, dtype=t.dtype)
    blk = jnp.einsum("rgcp,gh->rgchp", t, eye2).reshape(N_PAIR, 2 * SSM_GROUP, 2 * SSM_STATE)
    place = jax.nn.one_hot(jnp.arange(N_PAIR) % PAIRS_PER_CHUNK, PAIRS_PER_CHUNK, dtype=t.dtype)
    return jnp.einsum("rcl,rj->rjcl", blk, place).reshape(N_PAIR, 128, 2 * SSM_STATE)


def _chan_state_unblock(t):
    t = t.reshape(N_PAIR, PAIRS_PER_CHUNK, 2, SSM_GROUP, 2, SSM_STATE)
    place = jax.nn.one_hot(jnp.arange(N_PAIR) % PAIRS_PER_CHUNK, PAIRS_PER_CHUNK, dtype=t.dtype)
    t = jnp.einsum("rjgchp,rj->rgchp", t, place)
    t = jnp.einsum("rgchp,gh->rgcp", t, jnp.eye(2, dtype=t.dtype))
    return t.reshape(SSM_NG, SSM_GROUP, SSM_STATE)


def _scan_tables(zr, zi, reverse):
    zr, zi = _pair_lanes(zr), _pair_lanes(-zi if reverse else zi)
    k = jnp.arange(1, SEG + 1, dtype=F32)
    k = (k[::-1] if reverse else k)[None, :, None]
    mag = jnp.exp(k * zr[:, None, :])
    pw = jnp.stack([mag * jnp.cos(k * zi[:, None, :]), mag * jnp.sin(k * zi[:, None, :])], axis=1)
    first, last = (SEG - 1, 0) if reverse else (0, SEG - 1)
    big = [(pw[:, 0, last], pw[:, 1, last])]
    for _ in range(2):
        br, bi = big[-1]
        big.append((br * br - bi * bi, 2.0 * br * bi))
    rows = jnp.arange(8)[None, :, None]
    tiles = [jnp.broadcast_to(pw[:, comp, first][:, None, :], (N_PAIR, 8, 128)) for comp in (0, 1)]
    for lvl, step in enumerate((1, 2, 4)):
        keep = (rows <= 7 - step) if reverse else (rows >= step)
        for part in big[lvl]:
            tiles.append(jnp.where(keep, part[:, None, :], 0.0))
    return jnp.stack(tiles, axis=1), jnp.broadcast_to(pw[:, :, :, None, :], (N_PAIR, 2, SEG, 8, 128))


def _ssm_operands(w):
    ar, ai, bbr, bbi = _ssm_discretize(w["ssm_lambda_re"], w["ssm_lambda_im"], w["ssm_log_dt"], w["ssm_b_re"], w["ssm_b_im"])
    b_blk = jnp.concatenate([_chan_state_blocks(jnp.swapaxes(bbr, 1, 2)), _chan_state_blocks(jnp.swapaxes(bbi, 1, 2))], axis=2)
    c_blk = jnp.concatenate([_chan_state_blocks(w["ssm_c_re"]), -_chan_state_blocks(w["ssm_c_im"])], axis=2)
    dt = jnp.exp(w["ssm_log_dt"])[:, None]
    zr, zi = jnp.minimum(w["ssm_lambda_re"], LAMBDA_RE_MAX) * dt, w["ssm_lambda_im"] * dt
    return (b_blk.astype(BF16), jnp.swapaxes(b_blk, 1, 2).astype(BF16), c_blk.astype(BF16),
            jnp.swapaxes(c_blk, 1, 2).astype(BF16), *_scan_tables(zr, zi, False), *_scan_tables(zr, zi, True))


def _local_step(x, target, w, late_weights, on_grads):
    n_ex, seq, _ = x.shape
    lp = seq + BLOCK
    nb = lp // BLOCK
    n_rows = n_ex * lp
    g = {}

    head = jnp.concatenate([jnp.zeros((PAD, D_MODEL), F32), w["meta_tokens"]], axis=0)
    h0 = jnp.concatenate([jnp.broadcast_to(head[None], (n_ex, BLOCK, D_MODEL)), x], axis=1).reshape(n_rows, D_MODEL)

    qkv, hn_a = _rms_mm_cols(h0, w["attn_norm_w"], w["attn_w_qkv"], "qkv_fwd")
    att, lse = _attn_fwd(qkv, w["attn_sinks"], n_ex, nb)
    h1 = _mm_acc(att, w["attn_w_o"], False, "attn_out_fwd", res=h0)
    w = {**w, **late_weights(0, att)}
    h2, a0, hn_m0 = _mlp_fwd(h1, w["mlp_norm_w"][0:1], w["mlp_w_up"][0], w["mlp_w_down"][0], 0, "mlp0_fwd")
    late = late_weights(1, h2)
    w["ssm_w_glu"] = late["ssm_w_glu"]
    w["mlp_w_up"], w["mlp_w_down"] = w["mlp_w_up"] + late["mlp_w_up"], w["mlp_w_down"] + late["mlp_w_down"]

    ops = w["ssm_operands"] if "ssm_operands" in w else _ssm_operands(w)
    b_pad, bt_pad, ct_pad, c_pad, tab_fwd, pw_fwd, tab_rev, pw_rev = ops
    u = _rms_fwd(h2, w["ssm_norm_w"], "ssm_norm_fwd")
    yg, y, xs = _ssm_fwd(u, b_pad, c_pad, tab_fwd, pw_fwd, w["ssm_d"], n_ex, lp)
    z = _mm_cols(yg, w["ssm_w_glu"], False, "glu_mm_fwd")
    h3 = _glu_fwd(h2, z)
    h4, a1, hn_m1 = _mlp_fwd(h3, w["mlp_norm_w"][1:2], w["mlp_w_up"][1], w["mlp_w_down"][1], 0, "mlp1_fwd")

    dh4, loss_tile, dnorm_f = _loss_head(h4, w["final_norm_w"], target.reshape(n_ex * seq, D_MODEL), n_ex, nb)

    def mlp_bwd(dh_out, h_in, a, hn, layer, tag, norm_w):
        dhn, dw_up, dw_down = None, None, None
        for s in range(N_CHIPS):
            final = s == N_CHIPS - 1
            res = _mlp_bwd_shard(s, dh_out, a, hn, dhn, h_in if final else None, norm_w,
                                 w["mlp_w_up"][layer], w["mlp_w_down"][layer], dw_up, dw_down, f"{tag}_bwd{s}")
            dhn, dw_up, dw_down = res[:3]
        return dhn, res[3], dw_up, dw_down

    dh3, dnorm_m1, dwu1, dwd1 = mlp_bwd(dh4, h3, a1, hn_m1, 1, "mlp1", w["mlp_norm_w"][1:2])
    tok = on_grads("mlp1", {"mlp_w_up": dwu1, "mlp_w_down": dwd1})
    dz = _glu_bwd(dh3, z)
    dyg = _mm_acc(dz, w["ssm_w_glu"], True, "glu_mm_dx", out_dtype=BF16)
    g["ssm_w_glu"] = _mm_tn(yg, dz, N_CHIPS, False, "glu_mm_dw")
    du, db_blk, dc_blk, da_t, dd_t = _ssm_bwd(dyg, y, u, xs, ct_pad, bt_pad, tab_rev, pw_rev, w["ssm_d"] + tok, n_ex, lp)
    dh2, dnorm_s = _rms_bwd_call(du, h2, w["ssm_norm_w"], dh3, "ssm_norm_bwd")
    g["ssm_c_re"] = _chan_state_unblock(dc_blk[:, :, 0:128])
    g["ssm_c_im"] = -_chan_state_unblock(dc_blk[:, :, 128:256])
    g_bbr = jnp.swapaxes(_chan_state_unblock(db_blk[:, :, 0:128]), 1, 2)
    g_bbi = jnp.swapaxes(_chan_state_unblock(db_blk[:, :, 128:256]), 1, 2)
    g_a = jnp.sum(da_t, axis=2).reshape(N_PAIR, 2, 2, SSM_STATE)
    g_ar, g_ai = g_a[:, 0].reshape(SSM_NG, SSM_STATE), g_a[:, 1].reshape(SSM_NG, SSM_STATE)
    _, vjp = jax.vjp(_ssm_discretize, w["ssm_lambda_re"], w["ssm_lambda_im"], w["ssm_log_dt"], w["ssm_b_re"], w["ssm_b_im"])
    g["ssm_lambda_re"], g["ssm_lambda_im"], g["ssm_log_dt"], g["ssm_b_re"], g["ssm_b_im"] = vjp((g_ar, g_ai, g_bbr, g_bbi))
    tok = on_grads("ssm", g)
    g = {}
    dh1, dnorm_m0, dwu0, dwd0 = mlp_bwd(dh2, h1, a0, hn_m0, 0, "mlp0", w["mlp_norm_w"][0:1] + tok)
    datt = _mm_cols(dh1, w["attn_w_o"], True, "attn_out_dx")
    dw_o = _mm_tn(att, dh1, N_CHIPS, True, "attn_out_dw")
    tok = on_grads("mlp0", {"mlp_w_up": dwu0, "mlp_w_down": dwd0, "attn_w_o": dw_o})
    dqkv, dsink_rows = _attn_bwd(qkv, w["attn_sinks"] + tok, att, lse, datt, n_ex, nb)
    tok = on_grads("qkv", {"attn_w_qkv": _mm_tn(hn_a, dqkv, N_CHIPS, False, "qkv_dw")})
    dh0, dnorm_a = _mm_acc(dqkv, w["attn_w_qkv"], True, "qkv_dx", rms_bwd=(h0, w["attn_norm_w"] + tok, dh1))

    dh0 = dh0.reshape(n_ex, lp, D_MODEL)
    on_grads("rest", {
        "mlp_norm_w": jnp.stack([jnp.sum(dnorm_m0, axis=0), jnp.sum(dnorm_m1, axis=0)]),
        "final_norm_w": jnp.sum(dnorm_f, axis=0),
        "attn_norm_w": jnp.sum(dnorm_a, axis=0)[None],
        "ssm_norm_w": jnp.sum(dnorm_s, axis=0)[None],
        "attn_sinks": jnp.sum(dsink_rows, axis=0)[None],
        "ssm_d": jnp.sum(dd_t, axis=0)[None],
        "meta_tokens": jnp.sum(dh0[:, PAD:BLOCK], axis=0)})
    return loss_tile, dh0[:, BLOCK:]


_SHARDED_SMALL = ("meta_tokens", "ssm_norm_w", "ssm_d")
_REP_SSM = ("ssm_lambda_re", "ssm_lambda_im", "ssm_log_dt", "ssm_b_re", "ssm_b_im", "ssm_c_re", "ssm_c_im")
_REP_MISC = ("attn_norm_w", "attn_sinks", "mlp_norm_w", "final_norm_w")
_BIG = ("attn_w_qkv", "attn_w_o", "ssm_w_glu", "mlp_w_up", "mlp_w_down")


def _pack(parts, cols):
    flat = jnp.concatenate([p.reshape(-1) for p in parts])
    rows = -(-flat.shape[0] // (8 * cols)) * 8
    return jnp.pad(flat, (0, rows * cols - flat.shape[0])).reshape(rows, cols)


def _unpack(packed, like):
    flat = packed.reshape(-1)
    out, at = [], 0
    for p in like:
        out.append(flat[at:at + p.size].reshape(p.shape))
        at += p.size
    return out


def kernel(x, meta_tokens, attn_norm_w, attn_w_qkv, attn_sinks, attn_w_o, ssm_norm_w, ssm_lambda_re, ssm_lambda_im, ssm_log_dt, ssm_b_re, ssm_b_im, ssm_c_re, ssm_c_im, ssm_d, ssm_w_glu, mlp_norm_w, mlp_w_up, mlp_w_down, final_norm_w, loss_target, m_meta_tokens, m_attn_norm_w, m_attn_w_qkv, m_attn_sinks, m_attn_w_o, m_ssm_norm_w, m_ssm_lambda_re, m_ssm_lambda_im, m_ssm_log_dt, m_ssm_b_re, m_ssm_b_im, m_ssm_c_re, m_ssm_c_im, m_ssm_d, m_ssm_w_glu, m_mlp_norm_w, m_mlp_w_up, m_mlp_w_down, m_final_norm_w, v_meta_tokens, v_attn_norm_w, v_attn_w_qkv, v_attn_sinks, v_attn_w_o, v_ssm_norm_w, v_ssm_lambda_re, v_ssm_lambda_im, v_ssm_log_dt, v_ssm_b_re, v_ssm_b_im, v_ssm_c_re, v_ssm_c_im, v_ssm_d, v_ssm_w_glu, v_mlp_norm_w, v_mlp_w_up, v_mlp_w_down, v_final_norm_w):
    names = ("meta_tokens", "attn_norm_w", "attn_w_qkv", "attn_sinks", "attn_w_o", "ssm_norm_w", "ssm_lambda_re",
             "ssm_lambda_im", "ssm_log_dt", "ssm_b_re", "ssm_b_im", "ssm_c_re", "ssm_c_im", "ssm_d", "ssm_w_glu",
             "mlp_norm_w", "mlp_w_up", "mlp_w_down", "final_norm_w")
    wts = dict(zip(names, (meta_tokens, attn_norm_w, attn_w_qkv, attn_sinks, attn_w_o, ssm_norm_w, ssm_lambda_re,
                           ssm_lambda_im, ssm_log_dt, ssm_b_re, ssm_b_im, ssm_c_re, ssm_c_im, ssm_d, ssm_w_glu,
                           mlp_norm_w, mlp_w_up, mlp_w_down, final_norm_w)))
    mom = dict(zip(names, (m_meta_tokens, m_attn_norm_w, m_attn_w_qkv, m_attn_sinks, m_attn_w_o, m_ssm_norm_w,
                           m_ssm_lambda_re, m_ssm_lambda_im, m_ssm_log_dt, m_ssm_b_re, m_ssm_b_im, m_ssm_c_re,
                           m_ssm_c_im, m_ssm_d, m_ssm_w_glu, m_mlp_norm_w, m_mlp_w_up, m_mlp_w_down, m_final_norm_w)))
    var = dict(zip(names, (v_meta_tokens, v_attn_norm_w, v_attn_w_qkv, v_attn_sinks, v_attn_w_o, v_ssm_norm_w,
                           v_ssm_lambda_re, v_ssm_lambda_im, v_ssm_log_dt, v_ssm_b_re, v_ssm_b_im, v_ssm_c_re,
                           v_ssm_c_im, v_ssm_d, v_ssm_w_glu, v_mlp_norm_w, v_mlp_w_up, v_mlp_w_down, v_final_norm_w)))

    my_chip = 2 * lax.axis_index("x") + lax.axis_index("y")
    my_dev = 2 * my_chip + lax.axis_index("c")
    small_mine = _pack([wts[n] for n in _SHARDED_SMALL], 128)
    first = [attn_w_qkv.astype(BF16), attn_w_o.astype(BF16), small_mine]
    up16, down16 = mlp_w_up.astype(BF16), mlp_w_down.astype(BF16)
    mlp0 = [up16[0:1], down16[0:1]]
    rest = [ssm_w_glu.astype(BF16), up16[1:2], down16[1:2]]
    handles, _ = _split_start([(srcs, [_landing(a, my_chip, N_CHIPS) for a in srcs]) for srcs in (first, mlp0, rest)],
                              _gather_copies, N_GATHER_PEERS, "gather_start")
    full = {n: wts[n] for n in _REP_MISC}
    full["final_norm_w"] = final_norm_w[None]
    for n in _REP_SSM:
        full[n] = wts[n][0]
    full["ssm_operands"] = _ssm_operands(full)
    got = _split_wait(handles[0], full["ssm_operands"], _gather_copies, "gather_wait_first")
    full["attn_w_qkv"], full["attn_w_o"] = got[0], got[1]
    smalls = [_unpack(got[2][s], [wts[n] for n in _SHARDED_SMALL]) for s in range(N_CHIPS)]
    for k, n in enumerate(_SHARDED_SMALL):
        full[n] = jnp.concatenate([smalls[s][k] for s in range(N_CHIPS)], axis=1)

    def late_weights(stage, after):
        if stage == 0:
            up, down = _split_wait(handles[1], after, _gather_copies, "gather_wait_mlp0")
            return {"mlp_w_up": [up], "mlp_w_down": [down]}
        glu, up, down = _split_wait(handles[2], after, _gather_copies, "gather_wait_rest")
        return {"ssm_w_glu": glu, "mlp_w_up": [up], "mlp_w_down": [down]}

    def shard_cols(t):
        return jnp.swapaxes(t.reshape(t.shape[0], N_CHIPS, t.shape[1] // N_CHIPS), 0, 1)

    pending = {}

    def on_grads(tag, g):
        scatter = [g[n] for n in _BIG if n in g]
        whole = []
        if tag == "ssm":
            whole = [_pack([g[n] for n in _REP_SSM], D_MODEL)]
        if tag == "rest":
            parts = [shard_cols(g[n]) for n in _SHARDED_SMALL]
            scatter = [jnp.stack([_pack([p[s] for p in parts], 128) for s in range(N_CHIPS)])]
            whole = [_pack([g[n] for n in _REP_MISC], D_MODEL)]
        srcs = scatter + whole
        lands = [_landing(lax.dynamic_index_in_dim(a, my_chip, 0, keepdims=False), my_dev, N_DEV) for a in scatter]
        lands += [_landing(a, my_dev, N_DEV) for a in whole]
        hs, token = _split_start([(srcs, lands)], _exchange_copies(len(scatter)), N_EXCHANGE_PEERS, "exchange_start_" + tag)
        pending[tag] = (hs[0], len(scatter))
        return token[0, 0]

    loss_tile, grad_x = _local_step(x, loss_target, full, late_weights, on_grads)
    loss = lax.psum(loss_tile[0, 0], ("x", "y", "c"))

    recv = {}
    for tag, (handle, n_scatter) in pending.items():
        recv[tag] = _split_wait(handle, grad_x, _exchange_copies(n_scatter), "exchange_wait_" + tag)

    out = {}

    def update(tag, pieces, w2, m2, v2):
        return _adamw(pieces, w2, m2, v2, "adamw_" + tag)

    def update_weight(n, pieces):
        shp = wts[n].shape
        r2 = (math.prod(shp[:-1]), shp[-1])
        res = update(n, pieces, wts[n].reshape(r2), mom[n].reshape(r2), var[n].reshape(r2))
        out[n] = [t.reshape(shp) for t in res]

    update_weight("mlp_w_up", [recv["mlp0"][1], recv["mlp1"][0]])
    update_weight("mlp_w_down", [recv["mlp0"][2], recv["mlp1"][1]])
    update_weight("attn_w_o", [recv["mlp0"][0]])
    update_weight("ssm_w_glu", [recv["ssm"][0]])
    update_weight("attn_w_qkv", [recv["qkv"][0]])
    for tag, group, pieces, cols in (("small", _SHARDED_SMALL, recv["rest"][0], 128),
                                     ("rep_ssm", _REP_SSM, recv["ssm"][1], D_MODEL),
                                     ("rep_misc", _REP_MISC, recv["rest"][1], D_MODEL)):
        like = [wts[n] for n in group]
        res = update(tag, [pieces], _pack(like, cols), _pack([mom[n] for n in group], cols),
                     _pack([var[n] for n in group], cols))
        for k, n in enumerate(group):
            out[n] = [_unpack(t, like)[k] for t in res]

    return (loss, grad_x, *[out[n][0] for n in names], *[out[n][1] for n in names],
            *[out[n][2] for n in names], *[out[n][3] for n in names])
```

```python
import functools
import math

import jax
import jax.numpy as jnp
from jax import lax
from jax.experimental import pallas as pl
from jax.experimental.pallas import tpu as pltpu

F32 = jnp.float32
BF16 = jnp.bfloat16
SDS = jax.ShapeDtypeStruct

D_MODEL = 1024
N_HEADS = 16
N_KV = 4
GQA = N_HEADS // N_KV
HEAD_DIM = 64
BLOCK = 128
N_META = 16
PAD = BLOCK - N_META
QKV_DIM = (N_HEADS + 2 * N_KV) * HEAD_DIM
KV_DIM = 2 * N_KV * HEAD_DIM
D_FF = 4 * D_MODEL
N_CHIPS = 4
N_DEV = 8
SSM_GROUP = 16
SSM_NG = D_MODEL // SSM_GROUP
SSM_STATE = 64
N_PAIR = SSM_NG // 2
PAIRS_PER_CHUNK = 4
RMS_EPS = 1e-6
NEG_INF = -1e30
LAMBDA_RE_MAX = -1e-4
ADAM_LR, ADAM_B1, ADAM_B2, ADAM_EPS, ADAM_WD, ADAM_STEP = 0.001, 0.9, 0.999, 1e-08, 0.01, 10

TM = 384
MM_TILES = (768, 384)
MLP_FWD_TILES = (1056, 768, 384)
MLP_BWD_TILES = (768, 384)
TN_TILES = (1408, 768, 384)
VMEM_LIMIT = 56 * 1024 * 1024


def _params(n_grid):
    return pltpu.CompilerParams(dimension_semantics=("arbitrary",) * n_grid, vmem_limit_bytes=VMEM_LIMIT)


def _row_tile(n_rows, tiles):
    return next(t for t in tiles if n_rows % t == 0)


def _rms(h, w):
    r = lax.rsqrt(jnp.mean(h * h, axis=-1, keepdims=True) + RMS_EPS)
    return h * r * w


def _rms_bwd(dhn, h, w):
    r = lax.rsqrt(jnp.mean(h * h, axis=-1, keepdims=True) + RMS_EPS)
    g = dhn * w
    proj = jnp.sum(g * h, axis=-1, keepdims=True) * (1.0 / D_MODEL)
    return r * g - h * (r * r * r) * proj, dhn * h * r


def _fold8(t):
    return jnp.sum(t.reshape(t.shape[0] // 8, 8, t.shape[1]), axis=0)


def _gelu(y):
    return 0.5 * y * (1.0 + jnp.tanh(0.7978845608028654 * (y + 0.044715 * y * y * y)))


def _gelu_grad(y):
    t = jnp.tanh(0.7978845608028654 * (y + 0.044715 * y * y * y))
    return 0.5 * (1.0 + t) + 0.5 * y * (1.0 - t * t) * 0.7978845608028654 * (1.0 + 3.0 * 0.044715 * y * y)


def _w4_spec(w4):
    n_sh, _, k, n = w4.shape
    return pl.BlockSpec((n_sh, None, k, n), lambda i: (0, 0, 0, 0))


def _rms_mm_cols(h, wn, w4, name):
    n_rows = h.shape[0]
    n_sh, _, k, n = w4.shape
    tm = _row_tile(n_rows, MM_TILES)

    def body(h_ref, wn_ref, w_ref, o_ref, hn_ref):
        hn = _rms(h_ref[...], wn_ref[...]).astype(BF16)
        hn_ref[...] = hn
        for s in range(n_sh):
            o_ref[:, s * n:(s + 1) * n] = jnp.dot(hn, w_ref[s], preferred_element_type=F32).astype(o_ref.dtype)

    return pl.pallas_call(
        body, name=name, grid=(n_rows // tm,),
        in_specs=[pl.BlockSpec((tm, k), lambda i: (i, 0)), pl.BlockSpec((1, k), lambda i: (0, 0)), _w4_spec(w4)],
        out_specs=[pl.BlockSpec((tm, n_sh * n), lambda i: (i, 0)), pl.BlockSpec((tm, k), lambda i: (i, 0))],
        out_shape=[SDS((n_rows, n_sh * n), BF16), SDS((n_rows, k), BF16)],
        compiler_params=_params(1),
    )(h, wn, w4)


def _mm_cols(x, w4, trans_w, name):
    n_rows, kx = x.shape
    tm = _row_tile(n_rows, MM_TILES)
    n_sh, _, k, n = w4.shape
    n_out = k if trans_w else n
    dims = (((1,), (1,)), ((), ())) if trans_w else (((1,), (0,)), ((), ()))

    def body(x_ref, w_ref, o_ref):
        x16 = x_ref[...].astype(BF16)
        for s in range(n_sh):
            o_ref[:, s * n_out:(s + 1) * n_out] = lax.dot_general(
                x16, w_ref[s], dims, preferred_element_type=F32).astype(o_ref.dtype)

    return pl.pallas_call(
        body, name=name, grid=(n_rows // tm,),
        in_specs=[pl.BlockSpec((tm, kx), lambda i: (i, 0)), _w4_spec(w4)],
        out_specs=pl.BlockSpec((tm, n_sh * n_out), lambda i: (i, 0)),
        out_shape=SDS((n_rows, n_sh * n_out), BF16),
        compiler_params=_params(1),
    )(x, w4)


def _mm_acc(x, w4, trans_w, name, res=None, rms_bwd=None, out_dtype=F32):
    n_rows = x.shape[0]
    tm = _row_tile(n_rows, MM_TILES)
    n_sh, _, k, n = w4.shape
    kx, n_out = (n, k) if trans_w else (k, n)
    dims = (((1,), (1,)), ((), ())) if trans_w else (((1,), (0,)), ((), ()))

    def body(*refs):
        if rms_bwd is not None:
            x_ref, w_ref, h_ref, wn_ref, dres_ref, o_ref, dw_ref = refs
        elif res is not None:
            x_ref, w_ref, res_ref, o_ref = refs
        else:
            x_ref, w_ref, o_ref = refs
        acc = None
        for s in range(n_sh):
            part = lax.dot_general(x_ref[:, s * kx:(s + 1) * kx].astype(BF16), w_ref[s], dims, preferred_element_type=F32)
            acc = part if acc is None else acc + part
        if rms_bwd is not None:
            dh, dw_rows = _rms_bwd(acc, h_ref[...], wn_ref[...])
            o_ref[...] = (dres_ref[...] + dh).astype(o_ref.dtype)

            @pl.when(pl.program_id(0) == 0)
            def _():
                dw_ref[...] = jnp.zeros_like(dw_ref)

            dw_ref[...] += _fold8(dw_rows)
        elif res is not None:
            o_ref[...] = (res_ref[...] + acc).astype(o_ref.dtype)
        else:
            o_ref[...] = acc.astype(o_ref.dtype)

    row = lambda i: (i, 0)
    in_specs = [pl.BlockSpec((tm, n_sh * kx), row), _w4_spec(w4)]
    args = [x, w4]
    out_specs = pl.BlockSpec((tm, n_out), row)
    out_shape = SDS((n_rows, n_out), out_dtype)
    if rms_bwd is not None:
        h, wn, dres = rms_bwd
        in_specs += [pl.BlockSpec((tm, n_out), row), pl.BlockSpec((1, n_out), lambda i: (0, 0)),
                     pl.BlockSpec((tm, n_out), row)]
        args += [h, wn, dres]
        out_specs = [out_specs, pl.BlockSpec((8, n_out), lambda i: (0, 0))]
        out_shape = [out_shape, SDS((8, n_out), F32)]
    elif res is not None:
        in_specs.append(pl.BlockSpec((tm, n_out), row))
        args.append(res)
    return pl.pallas_call(
        body, name=name, grid=(n_rows // tm,), in_specs=in_specs, out_specs=out_specs, out_shape=out_shape,
        compiler_params=_params(1),
    )(*args)


def _mm_tn(a, b, n_sh, a_sharded, name):
    n_rows = a.shape[0]
    tm = _row_tile(n_rows, TN_TILES)
    ka = a.shape[1] // n_sh if a_sharded else a.shape[1]
    nb = b.shape[1] if a_sharded else b.shape[1] // n_sh
    n_i = n_rows // tm

    def body(a_ref, b_ref, o_ref, acc):
        i = pl.program_id(0)

        @pl.when(i == 0)
        def _():
            acc[...] = jnp.zeros_like(acc)

        for s in range(n_sh):
            a_s = a_ref[:, s * ka:(s + 1) * ka] if a_sharded else a_ref[...]
            b_s = b_ref[...] if a_sharded else b_ref[:, s * nb:(s + 1) * nb]
            acc[s] += lax.dot_general(a_s.astype(BF16), b_s.astype(BF16), (((0,), (0,)), ((), ())),
                                      preferred_element_type=F32)

        @pl.when(i == n_i - 1)
        def _():
            o_ref[...] = acc[...].astype(o_ref.dtype)

    return pl.pallas_call(
        body, name=name, grid=(n_i,),
        in_specs=[pl.BlockSpec((tm, a.shape[1]), lambda i: (i, 0)), pl.BlockSpec((tm, b.shape[1]), lambda i: (i, 0))],
        out_specs=pl.BlockSpec((n_sh, ka, nb), lambda i: (0, 0, 0)),
        out_shape=SDS((n_sh, ka, nb), BF16),
        scratch_shapes=[pltpu.VMEM((n_sh, ka, nb), F32)], compiler_params=_params(1),
    )(a, b)


def _mlp_fwd(h, wn, w_up4, w_down4, layer, name):
    n_rows = h.shape[0]
    TM = _row_tile(n_rows, MLP_FWD_TILES)
    n_sh = w_up4.shape[0]
    f_sh = D_FF // n_sh

    def body(h_ref, wn_ref, wu_ref, wd_ref, o_ref, a_ref, hn_ref, hn_s, acc):
        s = pl.program_id(1)

        @pl.when(s == 0)
        def _():
            hn = _rms(h_ref[...], wn_ref[...]).astype(BF16)
            hn_s[...] = hn
            hn_ref[...] = hn
            acc[...] = jnp.zeros_like(acc)

        a = jnp.dot(hn_s[...], wu_ref[...], preferred_element_type=F32)
        a_ref[...] = a.astype(BF16)
        act = jnp.maximum(a, 0.0)
        acc[...] += jnp.dot((act * act).astype(BF16), wd_ref[...], preferred_element_type=F32)

        @pl.when(s == n_sh - 1)
        def _():
            o_ref[...] = h_ref[...] + acc[...]

    row = lambda i, s: (i, 0)
    return pl.pallas_call(
        body, name=name, grid=(n_rows // TM, n_sh),
        in_specs=[pl.BlockSpec((TM, D_MODEL), row), pl.BlockSpec((1, D_MODEL), lambda i, s: (0, 0)),
                  pl.BlockSpec((None, None, D_MODEL, f_sh), lambda i, s: (s, layer, 0, 0)),
                  pl.BlockSpec((None, None, f_sh, D_MODEL), lambda i, s: (s, layer, 0, 0))],
        out_specs=[pl.BlockSpec((TM, D_MODEL), row), pl.BlockSpec((TM, f_sh), lambda i, s: (i, s)),
                   pl.BlockSpec((TM, D_MODEL), row)],
        out_shape=[SDS((n_rows, D_MODEL), F32), SDS((n_rows, D_FF), BF16), SDS((n_rows, D_MODEL), BF16)],
        scratch_shapes=[pltpu.VMEM((TM, D_MODEL), BF16), pltpu.VMEM((TM, D_MODEL), F32)],
        compiler_params=_params(2),
    )(h, wn, w_up4, w_down4)


def _mlp_bwd_shard(s, dh, a, hn, dhn_prev, h, wn, w_up4, w_down4, dw_up_buf, dw_down_buf, name):
    n_rows = dh.shape[0]
    n_sh = w_up4.shape[0]
    f_sh = D_FF // n_sh
    tm = _row_tile(n_rows, MLP_BWD_TILES)
    n_i = n_rows // tm
    last = h is not None
    nt = (((1,), (1,)), ((), ()))
    tn = (((0,), (0,)), ((), ()))

    def body(*refs):
        refs = list(refs)
        dh_ref, a_ref, hn_ref, wu_ref, wd_ref = refs[:5]
        at = 5
        prev_ref = None
        if dhn_prev is not None:
            prev_ref = refs[at]
            at += 1
        if last:
            h_ref, wn_ref = refs[at:at + 2]
            at += 2
        if dw_up_buf is not None:
            at += 2
        o_ref, dwu_ref, dwd_ref = refs[at:at + 3]
        at += 3
        if last:
            dnorm_ref = refs[at]
            at += 1
        acc_u, acc_d = refs[at:at + 2]
        i = pl.program_id(0)

        @pl.when(i == 0)
        def _():
            acc_u[...] = jnp.zeros_like(acc_u)
            acc_d[...] = jnp.zeros_like(acc_d)
            if last:
                dnorm_ref[...] = jnp.zeros_like(dnorm_ref)

        dh16 = dh_ref[...].astype(BF16)
        r = jnp.maximum(a_ref[...].astype(F32), 0.0)
        dact = lax.dot_general(dh16, wd_ref[...], nt, preferred_element_type=F32)
        da16 = (dact * (2.0 * r)).astype(BF16)
        acc_d[...] += lax.dot_general((r * r).astype(BF16), dh16, tn, preferred_element_type=F32)
        acc_u[...] += lax.dot_general(hn_ref[...], da16, tn, preferred_element_type=F32)
        dhn = lax.dot_general(da16, wu_ref[...], nt, preferred_element_type=F32)
        if prev_ref is not None:
            dhn = dhn + prev_ref[...]
        if last:
            d_rms, dw_rows = _rms_bwd(dhn, h_ref[...], wn_ref[...])
            o_ref[...] = dh_ref[...] + d_rms
            dnorm_ref[...] += _fold8(dw_rows)
        else:
            o_ref[...] = dhn

        @pl.when(i == n_i - 1)
        def _():
            dwu_ref[...] = acc_u[...].astype(BF16)
            dwd_ref[...] = acc_d[...].astype(BF16)

    row = lambda i: (i, 0)
    tile = pl.BlockSpec((tm, D_MODEL), row)
    in_specs = [tile, pl.BlockSpec((tm, f_sh), lambda i: (i, s)), tile,
                pl.BlockSpec((None, None, D_MODEL, f_sh), lambda i: (s, 0, 0, 0)),
                pl.BlockSpec((None, None, f_sh, D_MODEL), lambda i: (s, 0, 0, 0))]
    args = [dh, a, hn, w_up4, w_down4]
    if dhn_prev is not None:
        in_specs.append(tile)
        args.append(dhn_prev)
    if last:
        in_specs += [tile, pl.BlockSpec((1, D_MODEL), lambda i: (0, 0))]
        args += [h, wn]
    aliases = {}
    if dw_up_buf is not None:
        aliases = {len(args): 1, len(args) + 1: 2}
        in_specs += [pl.BlockSpec(memory_space=pl.ANY)] * 2
        args += [dw_up_buf, dw_down_buf]
    out_specs = [tile, pl.BlockSpec((None, D_MODEL, f_sh), lambda i: (s, 0, 0)),
                 pl.BlockSpec((None, f_sh, D_MODEL), lambda i: (s, 0, 0))]
    out_shape = [SDS((n_rows, D_MODEL), F32), SDS((n_sh, D_MODEL, f_sh), BF16), SDS((n_sh, f_sh, D_MODEL), BF16)]
    if last:
        out_specs.append(pl.BlockSpec((8, D_MODEL), lambda i: (0, 0)))
        out_shape.append(SDS((8, D_MODEL), F32))
    return pl.pallas_call(
        body, name=name, grid=(n_i,), in_specs=in_specs, out_specs=out_specs, out_shape=out_shape,
        input_output_aliases=aliases,
        scratch_shapes=[pltpu.VMEM((D_MODEL, f_sh), F32), pltpu.VMEM((f_sh, D_MODEL), F32)],
        compiler_params=_params(1),
    )(*args)


def _attn_masks(n):
    qi = lax.broadcasted_iota(jnp.int32, (BLOCK, 3 * BLOCK), 0)
    col = lax.broadcasted_iota(jnp.int32, (BLOCK, 3 * BLOCK), 1)
    kj = col - BLOCK
    dist = BLOCK + qi - kj
    kmin = jnp.where(n == 0, 2 * BLOCK, jnp.where(n == 1, BLOCK, 0))
    band_ok = (col >= BLOCK) & (dist >= 0) & (dist < BLOCK) & (kj >= kmin)
    q_pos = n * BLOCK + qi - PAD
    meta_ok = (col >= PAD) & (col < BLOCK) & (col - PAD <= q_pos)
    distf = jnp.where(col >= BLOCK, dist, 0).astype(F32)
    return band_ok | meta_ok, distf


def _alibi_slope(h):
    return float(2.0 ** (-8.0 * (h + 1) / N_HEADS))


def _attn_bias(n, bias_s):
    ok, distf = _attn_masks(n)
    for h in range(N_HEADS):
        bias_s[h] = jnp.where(ok, -_alibi_slope(h) * distf, NEG_INF)


def _attn_fwd(qkv, sinks, n_ex, nb):
    n_rows = qkv.shape[0]
    kvb = N_HEADS * HEAD_DIM // KV_DIM

    def body(sink_ref, q_ref, kvm_ref, kvp_ref, kvc_ref, o_ref, lse_ref, k_s, v_s, q_s, bias_s):
        n = pl.program_id(1)

        @pl.when(n <= 2)
        def _():
            _attn_bias(n, bias_s)

        v_s[...] = jnp.ones_like(v_s)
        for part, ref in enumerate((kvm_ref, kvp_ref, kvc_ref)):
            rows = slice(part * BLOCK, (part + 1) * BLOCK)
            k_s[rows, :] = ref[:, 0:N_KV * HEAD_DIM]
            for kv in range(N_KV):
                v_s[rows, kv * 2 * HEAD_DIM:kv * 2 * HEAD_DIM + HEAD_DIM] = \
                    ref[:, (N_KV + kv) * HEAD_DIM:(N_KV + kv + 1) * HEAD_DIM]
        for kv in range(N_KV):
            for g in range(GQA):
                h = kv * GQA + g
                q_s[kv, g * BLOCK:(g + 1) * BLOCK, :] = q_ref[:, h * HEAD_DIM:(h + 1) * HEAD_DIM] * (HEAD_DIM ** -0.5)
            s4 = lax.dot_general(q_s[kv], k_s[:, kv * HEAD_DIM:(kv + 1) * HEAD_DIM], (((1,), (1,)), ((), ())),
                                 preferred_element_type=F32)
            es, ms, sink_es = [], [], []
            for g in range(GQA):
                h = kv * GQA + g
                s = s4[g * BLOCK:(g + 1) * BLOCK] + bias_s[h]
                sink = sink_ref[0, h]
                m = jnp.maximum(jnp.max(s, axis=-1, keepdims=True), sink)
                es.append(jnp.exp(s - m).astype(BF16))
                ms.append(m)
                sink_es.append(jnp.exp(sink - m))
            pv = jnp.dot(jnp.concatenate(es, axis=0), v_s[:, kv * 2 * HEAD_DIM:(kv + 1) * 2 * HEAD_DIM],
                         preferred_element_type=F32)
            for g in range(GQA):
                h = kv * GQA + g
                pg = pv[g * BLOCK:(g + 1) * BLOCK]
                l = pg[:, HEAD_DIM:HEAD_DIM + 1] + sink_es[g]
                o_ref[:, h * HEAD_DIM:(h + 1) * HEAD_DIM] = (pg[:, 0:HEAD_DIM] / l).astype(BF16)
                lse_ref[:, h:h + 1] = ms[g] + jnp.log(l)

    return pl.pallas_call(
        body, name="attn_fwd", grid=(n_ex, nb),
        in_specs=[pl.BlockSpec(memory_space=pltpu.SMEM),
                  pl.BlockSpec((BLOCK, N_HEADS * HEAD_DIM), lambda b, n: (b * nb + n, 0)),
                  pl.BlockSpec((BLOCK, KV_DIM), lambda b, n: (b * nb, kvb)),
                  pl.BlockSpec((BLOCK, KV_DIM), lambda b, n: (b * nb + jnp.maximum(n - 1, 0), kvb)),
                  pl.BlockSpec((BLOCK, KV_DIM), lambda b, n: (b * nb + n, kvb))],
        out_specs=[pl.BlockSpec((BLOCK, N_HEADS * HEAD_DIM), lambda b, n: (b * nb + n, 0)),
                   pl.BlockSpec((BLOCK, N_HEADS), lambda b, n: (b * nb + n, 0))],
        out_shape=[SDS((n_rows, N_HEADS * HEAD_DIM), BF16), SDS((n_rows, N_HEADS), F32)],
        scratch_shapes=[pltpu.VMEM((3 * BLOCK, N_KV * HEAD_DIM), BF16), pltpu.VMEM((3 * BLOCK, 2 * N_KV * HEAD_DIM), BF16),
                        pltpu.VMEM((N_KV, GQA * BLOCK, HEAD_DIM), BF16), pltpu.VMEM((N_HEADS, BLOCK, 3 * BLOCK), F32)],
        compiler_params=_params(2),
    )(sinks, qkv, qkv, qkv, qkv)


def _attn_bwd(qkv, sinks, o, lse, do, n_ex, nb):
    n_rows = qkv.shape[0]
    kvb = N_HEADS * HEAD_DIM // KV_DIM
    scale = HEAD_DIM ** -0.5
    nq = lambda r: nb - 1 - r

    def body(sink_ref, q_ref, kvm_ref, kvp_ref, kvc_ref, o_ref, lse_ref, do_ref, dqkv_ref, dsink_ref,
             k_s, v_s, dkv_s, carry_s, meta_s, q_s, do_s, bias_s):
        b, r = pl.program_id(0), pl.program_id(1)
        n = nq(r)

        @pl.when((r == 0) | (n <= 1))
        def _():
            _attn_bias(n, bias_s)

        @pl.when((b == 0) & (r == 0))
        def _():
            dsink_ref[...] = jnp.zeros_like(dsink_ref)

        @pl.when(r == 0)
        def _():
            carry_s[...] = jnp.zeros_like(carry_s)
            meta_s[...] = jnp.zeros_like(meta_s)

        for part, ref in enumerate((kvm_ref, kvp_ref, kvc_ref)):
            k_s[part * BLOCK:(part + 1) * BLOCK, :] = ref[:, 0:N_KV * HEAD_DIM]
            v_s[part * BLOCK:(part + 1) * BLOCK, :] = ref[:, N_KV * HEAD_DIM:KV_DIM]
        nt = (((1,), (1,)), ((), ()))
        tn = (((0,), (0,)), ((), ()))
        for kv in range(N_KV):
            kcols = slice(kv * HEAD_DIM, (kv + 1) * HEAD_DIM)
            vcols = slice(N_KV * HEAD_DIM + kv * HEAD_DIM, N_KV * HEAD_DIM + (kv + 1) * HEAD_DIM)
            for g in range(GQA):
                cols = slice((kv * GQA + g) * HEAD_DIM, (kv * GQA + g + 1) * HEAD_DIM)
                q_s[kv, g * BLOCK:(g + 1) * BLOCK, :] = q_ref[:, cols] * scale
                do_s[kv, g * BLOCK:(g + 1) * BLOCK, :] = do_ref[:, cols]
            kh, vh = k_s[:, kcols], v_s[:, kcols]
            s4 = lax.dot_general(q_s[kv], kh, nt, preferred_element_type=F32)
            dp4 = lax.dot_general(do_s[kv], vh, nt, preferred_element_type=F32)
            ps, dss = [], []
            for g in range(GQA):
                h = kv * GQA + g
                cols = slice(h * HEAD_DIM, (h + 1) * HEAD_DIM)
                rows = slice(g * BLOCK, (g + 1) * BLOCK)
                s = s4[rows] + bias_s[h]
                lse_h = lse_ref[:, h:h + 1]
                p = jnp.exp(s - lse_h)
                delta = jnp.sum(do_ref[:, cols].astype(F32) * o_ref[:, cols].astype(F32), axis=-1, keepdims=True)
                dsink_ref[:, h:h + 1] += -jnp.exp(sink_ref[0, h] - lse_h) * delta
                ps.append(p.astype(BF16))
                dss.append((p * (dp4[rows] - delta)).astype(BF16))
            p4, ds4 = jnp.concatenate(ps, axis=0), jnp.concatenate(dss, axis=0)
            dq4 = jnp.dot(ds4, kh, preferred_element_type=F32) * scale
            for g in range(GQA):
                cols = slice((kv * GQA + g) * HEAD_DIM, (kv * GQA + g + 1) * HEAD_DIM)
                dqkv_ref[:, cols] = dq4[g * BLOCK:(g + 1) * BLOCK].astype(BF16)
            dkv_s[:, kcols] = lax.dot_general(ds4, q_s[kv], tn, preferred_element_type=F32)
            dkv_s[:, vcols] = lax.dot_general(p4, do_s[kv], tn, preferred_element_type=F32)

        meta_s[...] += dkv_s[0:BLOCK, :]
        cur = dkv_s[2 * BLOCK:3 * BLOCK, :] + carry_s[...]
        carry_s[...] = dkv_s[BLOCK:2 * BLOCK, :]

        @pl.when(n > 0)
        def _():
            dqkv_ref[:, N_HEADS * HEAD_DIM:QKV_DIM] = cur.astype(BF16)

        @pl.when(n == 0)
        def _():
            dqkv_ref[:, N_HEADS * HEAD_DIM:QKV_DIM] = (cur + meta_s[...]).astype(BF16)

    blk = lambda b, r: (b * nb + nq(r), 0)
    return pl.pallas_call(
        body, name="attn_bwd", grid=(n_ex, nb),
        in_specs=[pl.BlockSpec(memory_space=pltpu.SMEM),
                  pl.BlockSpec((BLOCK, N_HEADS * HEAD_DIM), blk),
                  pl.BlockSpec((BLOCK, KV_DIM), lambda b, r: (b * nb, kvb)),
                  pl.BlockSpec((BLOCK, KV_DIM), lambda b, r: (b * nb + jnp.maximum(nq(r) - 1, 0), kvb)),
                  pl.BlockSpec((BLOCK, KV_DIM), lambda b, r: (b * nb + nq(r), kvb)),
                  pl.BlockSpec((BLOCK, N_HEADS * HEAD_DIM), blk),
                  pl.BlockSpec((BLOCK, N_HEADS), blk),
                  pl.BlockSpec((BLOCK, N_HEADS * HEAD_DIM), blk)],
        out_specs=[pl.BlockSpec((BLOCK, QKV_DIM), blk),
                   pl.BlockSpec((BLOCK, N_HEADS), lambda b, r: (0, 0))],
        out_shape=[SDS((n_rows, QKV_DIM), BF16), SDS((BLOCK, N_HEADS), F32)],
        scratch_shapes=[pltpu.VMEM((3 * BLOCK, N_KV * HEAD_DIM), BF16), pltpu.VMEM((3 * BLOCK, N_KV * HEAD_DIM), BF16),
                        pltpu.VMEM((3 * BLOCK, KV_DIM), F32), pltpu.VMEM((BLOCK, KV_DIM), F32),
                        pltpu.VMEM((BLOCK, KV_DIM), F32), pltpu.VMEM((N_KV, GQA * BLOCK, HEAD_DIM), BF16),
                        pltpu.VMEM((N_KV, GQA * BLOCK, HEAD_DIM), BF16), pltpu.VMEM((N_HEADS, BLOCK, 3 * BLOCK), F32)],
        compiler_params=_params(2),
    )(sinks, qkv, qkv, qkv, qkv, o, lse, do)


SEG = TM // 8
XW = 256 * PAIRS_PER_CHUNK


def _cmul_add(xr, xi, mr, mi, sr, si):
    return xr + mr * sr - mi * si, xi + mr * si + mi * sr


def _to_segments(src_ref, dst):
    for s in range(SEG):
        dst[s * 8:(s + 1) * 8, :] = src_ref[pl.ds(s, 8, stride=SEG), :]


def _from_segments(src, i):
    return src[pl.ds(i, SEG, stride=8), :]


def _scan_segments(buf, tab_ref, pw_ref, carry_s, reverse):
    shifts = (7, 6, 4) if reverse else (1, 2, 4)
    row_id = lax.broadcasted_iota(jnp.int32, (8, 128), 0)

    def local(si, prev):
        s = (SEG - 1 - si) if reverse else si
        row = pl.multiple_of(s * 8, 8)
        out = []
        for j in range(PAIRS_PER_CHUNK):
            re, im = slice(256 * j, 256 * j + 128), slice(256 * j + 128, 256 * j + 256)
            xr, xi = _cmul_add(buf[pl.ds(row, 8), re], buf[pl.ds(row, 8), im],
                               tab_ref[j, 0], tab_ref[j, 1], prev[2 * j], prev[2 * j + 1])
            buf[pl.ds(row, 8), re] = xr
            buf[pl.ds(row, 8), im] = xi
            out += [xr, xi]
        return tuple(out)

    zero = jnp.zeros((8, 128), F32)
    edge = lax.fori_loop(0, SEG, local, (zero,) * (2 * PAIRS_PER_CHUNK))

    entering = []
    for j in range(PAIRS_PER_CHUNK):
        er, ei = edge[2 * j], edge[2 * j + 1]
        if reverse:
            sr = jnp.where(row_id == 7, carry_s[2 * j], pltpu.roll(er, 7, 0))
            si_ = jnp.where(row_id == 7, carry_s[2 * j + 1], pltpu.roll(ei, 7, 0))
        else:
            sr = jnp.where(row_id == 0, carry_s[2 * j], pltpu.roll(er, 1, 0))
            si_ = jnp.where(row_id == 0, carry_s[2 * j + 1], pltpu.roll(ei, 1, 0))
        for lvl, sh in enumerate(shifts):
            sr, si_ = _cmul_add(sr, si_, tab_ref[j, 2 + 2 * lvl], tab_ref[j, 3 + 2 * lvl],
                                pltpu.roll(sr, sh, 0), pltpu.roll(si_, sh, 0))
        entering += [sr, si_]
        tr, ti = _cmul_add(er, ei, tab_ref[j, 2], tab_ref[j, 3], sr, si_)
        out_row = slice(0, 1) if reverse else slice(7, 8)
        carry_s[2 * j] = jnp.broadcast_to(tr[out_row], (8, 128))
        carry_s[2 * j + 1] = jnp.broadcast_to(ti[out_row], (8, 128))

    def fix(s, _):
        row = pl.multiple_of(s * 8, 8)
        for j in range(PAIRS_PER_CHUNK):
            re, im = slice(256 * j, 256 * j + 128), slice(256 * j + 128, 256 * j + 256)
            xr, xi = _cmul_add(buf[pl.ds(row, 8), re], buf[pl.ds(row, 8), im],
                               pw_ref[j, 0, s], pw_ref[j, 1, s], entering[2 * j], entering[2 * j + 1])
            buf[pl.ds(row, 8), re] = xr
            buf[pl.ds(row, 8), im] = xi
        return 0

    lax.fori_loop(0, SEG, fix, 0)


def _ssm_fwd(u, b_pad, c_pad, tab, pw, d_skip, n_ex, lp):
    n_rows = u.shape[0]
    n_t = lp // TM
    n_chunk = D_MODEL // 128

    def body(u_ref, bp_ref, cp_ref, tab_ref, pw_ref, d_ref, yg_ref, y_ref, xs_ref, buf, carry_s, us, ys):
        @pl.when(pl.program_id(2) == 0)
        def _():
            carry_s[...] = jnp.zeros_like(carry_s)

        _to_segments(u_ref, us)
        ub = us[...]
        u16 = ub.astype(BF16)
        buf[...] = jnp.dot(u16, bp_ref[...], preferred_element_type=F32)
        _scan_segments(buf, tab_ref, pw_ref, carry_s, reverse=False)
        xb = buf[...].astype(BF16)
        xs_ref[...] = xb
        ys[...] = d_ref[...] * ub + jnp.dot(xb, cp_ref[...], preferred_element_type=F32)
        for i in range(8):
            yi = _from_segments(ys, i)
            y_ref[i * SEG:(i + 1) * SEG, :] = yi
            yg_ref[i * SEG:(i + 1) * SEG, :] = _gelu(yi).astype(BF16)

    rows = lambda b, q, t: (b * n_t + t, q)
    return pl.pallas_call(
        body, name="ssm_fwd", grid=(n_ex, n_chunk, n_t),
        in_specs=[pl.BlockSpec((TM, 128), rows),
                  pl.BlockSpec((None, 128, XW), lambda b, q, t: (q, 0, 0)),
                  pl.BlockSpec((None, XW, 128), lambda b, q, t: (q, 0, 0)),
                  pl.BlockSpec((PAIRS_PER_CHUNK, 8, 8, 128), lambda b, q, t: (q, 0, 0, 0)),
                  pl.BlockSpec((PAIRS_PER_CHUNK, 2, SEG, 8, 128), lambda b, q, t: (q, 0, 0, 0, 0)),
                  pl.BlockSpec((1, 128), lambda b, q, t: (0, q))],
        out_specs=[pl.BlockSpec((TM, 128), rows), pl.BlockSpec((TM, 128), rows),
                   pl.BlockSpec((None, TM, XW), lambda b, q, t: (q, b * n_t + t, 0))],
        out_shape=[SDS((n_rows, D_MODEL), BF16), SDS((n_rows, D_MODEL), F32), SDS((n_chunk, n_rows, XW), BF16)],
        scratch_shapes=[pltpu.VMEM((TM, XW), F32), pltpu.VMEM((2 * PAIRS_PER_CHUNK, 8, 128), F32),
                        pltpu.VMEM((TM, 128), F32), pltpu.VMEM((TM, 128), F32)],
        compiler_params=_params(3),
    )(u, b_pad, c_pad, tab, pw, d_skip)


def _ssm_bwd(dyg, y, u, xs, ct_pad, bt_pad, tab_rev, pw_rev, d_skip, n_ex, lp):
    n_rows = u.shape[0]
    n_t = lp // TM
    n_chunk = D_MODEL // 128
    tile = lambda q, b, t: (b * n_t + (n_t - 1 - t), q)

    def body(dyg_ref, y_ref, u_ref, xs_ref, xp_ref, ct_ref, bt_ref, tab_ref, pw_ref, d_ref,
             du_ref, db_ref, dc_ref, da_ref, dd_ref, buf, xf, carry_s, us, dys, dyp):
        b, t = pl.program_id(1), pl.program_id(2)

        @pl.when((b == 0) & (t == 0))
        def _():
            db_ref[...] = jnp.zeros_like(db_ref)
            dc_ref[...] = jnp.zeros_like(dc_ref)
            da_ref[...] = jnp.zeros_like(da_ref)
            dd_ref[...] = jnp.zeros_like(dd_ref)

        @pl.when(t == 0)
        def _():
            carry_s[...] = jnp.zeros_like(carry_s)

        dys[...] = dyg_ref[...].astype(F32) * _gelu_grad(y_ref[...])
        dd_ref[...] += _fold8(dys[...] * u_ref[...])
        _to_segments(dys, dyp)
        dy = dyp[...]
        _to_segments(u_ref, us)
        dy16 = dy.astype(BF16)
        first_tile = t == n_t - 1
        tn = (((0,), (0,)), ((), ()))
        buf[...] = jnp.dot(dy16, ct_ref[...], preferred_element_type=F32)
        dc_ref[...] += lax.dot_general(dy16, xs_ref[...], tn, preferred_element_type=F32)
        xf[16:16 + TM, :] = xs_ref[...].astype(F32)
        xf[0:16, :] = jnp.where(first_tile, 0.0, xp_ref[...].astype(F32))
        _scan_segments(buf, tab_ref, pw_ref, carry_s, reverse=True)
        g16 = buf[...].astype(BF16)
        dys[...] = d_ref[...] * dy + jnp.dot(g16, bt_ref[...], preferred_element_type=F32)
        db_ref[...] += lax.dot_general(us[...].astype(BF16), g16, tn, preferred_element_type=F32)
        row_id = lax.broadcasted_iota(jnp.int32, (8, 128), 0)
        for j in range(PAIRS_PER_CHUNK):
            re, im = slice(256 * j, 256 * j + 128), slice(256 * j + 128, 256 * j + 256)
            first = [jnp.where(row_id == 0, jnp.broadcast_to(xf[15:16, c], (8, 128)),
                               pltpu.roll(xf[8 + TM:16 + TM, c], 1, 0)) for c in (re, im)]
            for rows, pr, pi in ((slice(0, 8), first[0], first[1]),
                                 (slice(8, TM), xf[16:8 + TM, re], xf[16:8 + TM, im])):
                gr, gi = buf[rows, re], buf[rows, im]
                da_ref[j, 0] += _fold8(gr * pr + gi * pi)
                da_ref[j, 1] += _fold8(gi * pr - gr * pi)
        for i in range(8):
            du_ref[i * SEG:(i + 1) * SEG, :] = _from_segments(dys, i)

    prev16 = lambda q, b, t: (q, jnp.maximum((b * n_t + (n_t - 1 - t)) * (TM // 16) - 1, 0), 0)
    return pl.pallas_call(
        body, name="ssm_bwd", grid=(n_chunk, n_ex, n_t),
        in_specs=[pl.BlockSpec((TM, 128), tile), pl.BlockSpec((TM, 128), tile), pl.BlockSpec((TM, 128), tile),
                  pl.BlockSpec((None, TM, XW), lambda q, b, t: (q, b * n_t + (n_t - 1 - t), 0)),
                  pl.BlockSpec((None, 16, XW), prev16),
                  pl.BlockSpec((None, 128, XW), lambda q, b, t: (q, 0, 0)),
                  pl.BlockSpec((None, XW, 128), lambda q, b, t: (q, 0, 0)),
                  pl.BlockSpec((PAIRS_PER_CHUNK, 8, 8, 128), lambda q, b, t: (q, 0, 0, 0)),
                  pl.BlockSpec((PAIRS_PER_CHUNK, 2, SEG, 8, 128), lambda q, b, t: (q, 0, 0, 0, 0)),
                  pl.BlockSpec((1, 128), lambda q, b, t: (0, q))],
        out_specs=[pl.BlockSpec((TM, 128), tile),
                   pl.BlockSpec((None, 128, XW), lambda q, b, t: (q, 0, 0)),
                   pl.BlockSpec((None, 128, XW), lambda q, b, t: (q, 0, 0)),
                   pl.BlockSpec((PAIRS_PER_CHUNK, 2, 8, 128), lambda q, b, t: (q, 0, 0, 0)),
                   pl.BlockSpec((8, 128), lambda q, b, t: (0, q))],
        out_shape=[SDS((n_rows, D_MODEL), F32), SDS((n_chunk, 128, XW), F32), SDS((n_chunk, 128, XW), F32),
                   SDS((N_PAIR, 2, 8, 128), F32), SDS((8, D_MODEL), F32)],
        scratch_shapes=[pltpu.VMEM((TM, XW), F32), pltpu.VMEM((TM + 16, XW), F32),
                        pltpu.VMEM((2 * PAIRS_PER_CHUNK, 8, 128), F32), pltpu.VMEM((TM, 128), F32),
                        pltpu.VMEM((TM, 128), F32), pltpu.VMEM((TM, 128), F32)],
        compiler_params=_params(3),
    )(dyg, y, u, xs, xs, ct_pad, bt_pad, tab_rev, pw_rev, d_skip)


def _rms_fwd(h, wn, name):
    n_rows = h.shape[0]

    def body(h_ref, wn_ref, o_ref):
        o_ref[...] = _rms(h_ref[...], wn_ref[...])

    return pl.pallas_call(
        body, name=name, grid=(n_rows // TM,),
        in_specs=[pl.BlockSpec((TM, D_MODEL), lambda i: (i, 0)), pl.BlockSpec((1, D_MODEL), lambda i: (0, 0))],
        out_specs=pl.BlockSpec((TM, D_MODEL), lambda i: (i, 0)),
        out_shape=SDS((n_rows, D_MODEL), F32), compiler_params=_params(1),
    )(h, wn)


def _rms_bwd_call(dhn, h, wn, dres, name):
    n_rows = h.shape[0]

    def body(dhn_ref, h_ref, wn_ref, dres_ref, o_ref, dw_ref):
        @pl.when(pl.program_id(0) == 0)
        def _():
            dw_ref[...] = jnp.zeros_like(dw_ref)

        dh, dw_rows = _rms_bwd(dhn_ref[...], h_ref[...], wn_ref[...])
        o_ref[...] = dres_ref[...] + dh
        dw_ref[...] += _fold8(dw_rows)

    row = lambda i: (i, 0)
    return pl.pallas_call(
        body, name=name, grid=(n_rows // TM,),
        in_specs=[pl.BlockSpec((TM, D_MODEL), row), pl.BlockSpec((TM, D_MODEL), row),
                  pl.BlockSpec((1, D_MODEL), lambda i: (0, 0)), pl.BlockSpec((TM, D_MODEL), row)],
        out_specs=[pl.BlockSpec((TM, D_MODEL), row), pl.BlockSpec((8, D_MODEL), lambda i: (0, 0))],
        out_shape=[SDS((n_rows, D_MODEL), F32), SDS((8, D_MODEL), F32)], compiler_params=_params(1),
    )(dhn, h, wn, dres)


def _glu_fwd(h, z):
    n_rows = h.shape[0]

    def body(h_ref, val_ref, gate_ref, o_ref):
        o_ref[...] = h_ref[...] + val_ref[...].astype(F32) * jax.nn.sigmoid(gate_ref[...].astype(F32))

    row = lambda i: (i, 0)
    return pl.pallas_call(
        body, name="glu_fwd", grid=(n_rows // TM,),
        in_specs=[pl.BlockSpec((TM, D_MODEL), row), pl.BlockSpec((TM, D_MODEL), row),
                  pl.BlockSpec((TM, D_MODEL), lambda i: (i, 1))],
        out_specs=pl.BlockSpec((TM, D_MODEL), row),
        out_shape=SDS((n_rows, D_MODEL), F32), compiler_params=_params(1),
    )(h, z, z)


def _glu_bwd(dh, z):
    n_rows = dh.shape[0]

    def body(dh_ref, val_ref, gate_ref, dz_ref):
        sg = jax.nn.sigmoid(gate_ref[...].astype(F32))
        d = dh_ref[...]
        dz_ref[:, 0:D_MODEL] = (d * sg).astype(BF16)
        dz_ref[:, D_MODEL:2 * D_MODEL] = (d * val_ref[...].astype(F32) * sg * (1.0 - sg)).astype(BF16)

    row = lambda i: (i, 0)
    return pl.pallas_call(
        body, name="glu_bwd", grid=(n_rows // TM,),
        in_specs=[pl.BlockSpec((TM, D_MODEL), row), pl.BlockSpec((TM, D_MODEL), row),
                  pl.BlockSpec((TM, D_MODEL), lambda i: (i, 1))],
        out_specs=pl.BlockSpec((TM, 2 * D_MODEL), row),
        out_shape=SDS((n_rows, 2 * D_MODEL), BF16), compiler_params=_params(1),
    )(dh, z, z)


def _loss_head(h, wn, target, n_ex, nb):
    n_rows = h.shape[0]

    def body(h_ref, wn_ref, t_ref, dh_ref, loss_ref, dw_ref):
        b, n = pl.program_id(0), pl.program_id(1)

        @pl.when((b == 0) & (n == 0))
        def _():
            loss_ref[...] = jnp.zeros_like(loss_ref)
            dw_ref[...] = jnp.zeros_like(dw_ref)

        @pl.when(n == 0)
        def _():
            dh_ref[...] = jnp.zeros_like(dh_ref)

        @pl.when(n > 0)
        def _():
            hh = h_ref[...]
            diff = _rms(hh, wn_ref[...]) - t_ref[...]
            loss_ref[...] += 0.5 * jnp.sum(diff * diff) * (1.0 / D_MODEL)
            dh, dw_rows = _rms_bwd(diff * (1.0 / D_MODEL), hh, wn_ref[...])
            dh_ref[...] = dh
            dw_ref[...] += _fold8(dw_rows)

    return pl.pallas_call(
        body, name="loss_head", grid=(n_ex, nb),
        in_specs=[pl.BlockSpec((BLOCK, D_MODEL), lambda b, n: (b * nb + n, 0)),
                  pl.BlockSpec((1, D_MODEL), lambda b, n: (0, 0)),
                  pl.BlockSpec((BLOCK, D_MODEL), lambda b, n: (b * (nb - 1) + jnp.maximum(n - 1, 0), 0))],
        out_specs=[pl.BlockSpec((BLOCK, D_MODEL), lambda b, n: (b * nb + n, 0)),
                   pl.BlockSpec((8, 128), lambda b, n: (0, 0)),
                   pl.BlockSpec((8, D_MODEL), lambda b, n: (0, 0))],
        out_shape=[SDS((n_rows, D_MODEL), F32), SDS((8, 128), F32), SDS((8, D_MODEL), F32)],
        compiler_params=_params(2),
    )(h, wn, target)


def _adamw(pieces, w, m, v, name):
    n_layers = len(pieces)
    rows, cols = pieces[0].shape[1:]
    rb = rows
    for cand in (256, 136, 128, 64, 32, 16, 8):
        if rows % cand == 0 and rows > cand:
            rb = cand
            break
    n_blk = rows // rb
    c1 = 1.0 / (1.0 - ADAM_B1 ** ADAM_STEP)
    c2 = 1.0 / (1.0 - ADAM_B2 ** ADAM_STEP)

    def body(*refs):
        p_refs = refs[:n_layers]
        w_ref, m_ref, v_ref, g_out, d_out, m_out, v_out = refs[n_layers:]
        layer = pl.program_id(0)
        g = None
        for l, p_ref in enumerate(p_refs):
            gl = p_ref[0].astype(F32)
            for k in range(1, N_DEV):
                gl = gl + p_ref[k].astype(F32)
            g = gl if g is None else jnp.where(layer == l, gl, g)
        m_new = ADAM_B1 * m_ref[...] + (1.0 - ADAM_B1) * g
        v_new = ADAM_B2 * v_ref[...] + (1.0 - ADAM_B2) * (g * g)
        g_out[...] = g
        m_out[...] = m_new
        v_out[...] = v_new
        d_out[...] = -ADAM_LR * ((m_new * c1) / (jnp.sqrt(v_new * c2) + ADAM_EPS) + ADAM_WD * w_ref[...])

    def piece_spec(l):
        return pl.BlockSpec((N_DEV, rb, cols), lambda ly, i: (0, jnp.where(ly == l, i, 0), 0))

    blk = pl.BlockSpec((rb, cols), lambda ly, i: (ly * n_blk + i, 0))
    return pl.pallas_call(
        body, name=name, grid=(n_layers, n_blk),
        in_specs=[piece_spec(l) for l in range(n_layers)] + [blk, blk, blk],
        out_specs=[blk, blk, blk, blk],
        out_shape=[SDS((n_layers * rows, cols), F32)] * 4, compiler_params=_params(2),
    )(*pieces, w, m, v)


_HBM = pl.BlockSpec(memory_space=pltpu.HBM)
_SEM = pl.BlockSpec(memory_space=pltpu.SEMAPHORE)
_EFFECT = pltpu.SideEffectType.DATAFLOW_SIDE_EFFECTING
N_GATHER_PEERS = N_CHIPS - 1
N_EXCHANGE_PEERS = N_DEV - 1


def _gather_copies(srcs, lands, send_sems, recv_sems):
    x, y, c = lax.axis_index("x"), lax.axis_index("y"), lax.axis_index("c")
    mine = 2 * x + y
    chips = [(1 - x, y), (x, 1 - y), (1 - x, 1 - y)]
    out, inc = [], []
    for a in range(len(srcs)):
        for k, (px, py) in enumerate(chips):
            j = a * N_GATHER_PEERS + k
            sems = dict(send_sem=send_sems.at[j], recv_sem=recv_sems.at[j], device_id=(px, py, c),
                        device_id_type=pl.DeviceIdType.MESH)
            out.append(pltpu.make_async_remote_copy(src_ref=srcs[a], dst_ref=lands[a].at[mine], **sems))
            inc.append(pltpu.make_async_remote_copy(src_ref=srcs[a], dst_ref=lands[a].at[2 * px + py], **sems))
    return out, inc


def _exchange_copies(n_scatter):
    def copies(srcs, lands, send_sems, recv_sems):
        x, y, c = lax.axis_index("x"), lax.axis_index("y"), lax.axis_index("c")
        me = 4 * x + 2 * y + c
        peers = [(x ^ (k >> 2), y ^ ((k >> 1) & 1), c ^ (k & 1)) for k in range(1, N_DEV)]
        out, inc = [], []
        for a in range(len(srcs)):
            for k, (px, py, pc) in enumerate(peers):
                j = a * N_EXCHANGE_PEERS + k
                sems = dict(send_sem=send_sems.at[j], recv_sem=recv_sems.at[j], device_id=(px, py, pc),
                            device_id_type=pl.DeviceIdType.MESH)
                theirs = srcs[a].at[2 * px + py] if a < n_scatter else srcs[a]
                mine = srcs[a].at[2 * x + y] if a < n_scatter else srcs[a]
                out.append(pltpu.make_async_remote_copy(src_ref=theirs, dst_ref=lands[a].at[me], **sems))
                inc.append(pltpu.make_async_remote_copy(src_ref=mine, dst_ref=lands[a].at[4 * px + 2 * py + pc], **sems))
        return out, inc

    return copies


def _split_start(groups, copies_fn, n_peers, name):
    sizes = [len(srcs) for srcs, _ in groups]
    flat = [a for srcs, lands in groups for a in list(srcs) + list(lands)]
    n_flat, n_grp = len(flat), len(groups)

    def body(*refs):
        sems = refs[2 * n_flat:2 * n_flat + 2 * n_grp]
        token = refs[-1]
        at = 0
        for gi, n in enumerate(sizes):
            out, _ = copies_fn(refs[at:at + n], refs[at + n:at + 2 * n], sems[2 * gi], sems[2 * gi + 1])
            for cp in out:
                cp.start()
            at += 2 * n
        token[...] = jnp.zeros_like(token)

    sem_shapes = []
    for n in sizes:
        sem_shapes += [pltpu.SemaphoreType.DMA((n * n_peers,)), pltpu.SemaphoreType.DMA((n * n_peers,))]
    res = pl.pallas_call(
        body, name=name,
        out_shape=(*[pltpu.HBM(a.shape, a.dtype) for a in flat], *sem_shapes, SDS((8, 128), F32)),
        in_specs=[_HBM] * n_flat,
        out_specs=(*[_HBM] * n_flat, *[_SEM] * (2 * n_grp), pl.BlockSpec(memory_space=pltpu.VMEM)),
        input_output_aliases={i: i for i in range(n_flat)},
        compiler_params=pltpu.CompilerParams(has_side_effects=_EFFECT),
    )(*[pltpu.with_memory_space_constraint(a, pltpu.HBM) for a in flat])
    handles, at = [], 0
    for gi, n in enumerate(sizes):
        handles.append((res[n_flat + 2 * gi], res[n_flat + 2 * gi + 1], list(res[at:at + n]), list(res[at + n:at + 2 * n])))
        at += 2 * n
    return handles, res[-1]


def _split_wait(handle, after, copies_fn, name):
    send_sems, recv_sems, srcs, lands = handle
    n = len(srcs)
    after = list(after) if isinstance(after, (list, tuple)) else [after]

    def body(*refs):
        out, inc = copies_fn(refs[:n], refs[n:2 * n], refs[2 * n], refs[2 * n + 1])
        for cp in out:
            cp.wait_send()
        for cp in inc:
            cp.wait_recv()

    flat = list(srcs) + list(lands)
    res = pl.pallas_call(
        body, name=name,
        out_shape=tuple(pltpu.HBM(a.shape, a.dtype) for a in flat),
        in_specs=[_HBM] * (2 * n) + [_SEM, _SEM] + [pl.BlockSpec(memory_space=pl.ANY)] * len(after),
        out_specs=tuple([_HBM] * (2 * n)),
        input_output_aliases={i: i for i in range(2 * n)},
        compiler_params=pltpu.CompilerParams(has_side_effects=_EFFECT),
    )(*flat, send_sems, recv_sems, *after)
    return list(res[n:])


def _landing(own, slot, n_slots):
    return lax.dynamic_update_index_in_dim(lax.empty((n_slots,) + own.shape, own.dtype), own, slot, 0)


def _ssm_discretize(lam_re, lam_im, log_dt, b_re, b_im):
    lr = jnp.minimum(lam_re, LAMBDA_RE_MAX)
    li = lam_im
    dt = jnp.exp(log_dt)[:, None]
    mag = jnp.exp(lr * dt)
    ar, ai = mag * jnp.cos(li * dt), mag * jnp.sin(li * dt)
    den = lr * lr + li * li
    nr, ni = ar - 1.0, ai
    gr, gi = (nr * lr + ni * li) / den, (ni * lr - nr * li) / den
    bbr = gr[:, :, None] * b_re - gi[:, :, None] * b_im
    bbi = gr[:, :, None] * b_im + gi[:, :, None] * b_re
    return ar, ai, bbr, bbi


def _pair_lanes(t):
    return t.reshape(N_PAIR, 2 * SSM_STATE)


def _chan_state_blocks(t_gcp):
    t = t_gcp.reshape(N_PAIR, 2, SSM_GROUP, SSM_STATE)
    eye2 = jnp.eye(2, dtype=t.dtype)
    blk = jnp.einsum("rgcp,gh->rgchp", t, eye2).reshape(N_PAIR, 2 * SSM_GROUP, 2 * SSM_STATE)
    place = jax.nn.one_hot(jnp.arange(N_PAIR) % PAIRS_PER_CHUNK, PAIRS_PER_CHUNK, dtype=t.dtype)
    return jnp.einsum("rcl,rj->rjcl", blk, place).reshape(N_PAIR, 128, 2 * SSM_STATE)


def _chan_state_unblock(t):
    t = t.reshape(N_PAIR, PAIRS_PER_CHUNK, 2, SSM_GROUP, 2, SSM_STATE)
    place = jax.nn.one_hot(jnp.arange(N_PAIR) % PAIRS_PER_CHUNK, PAIRS_PER_CHUNK, dtype=t.dtype)
    t = jnp.einsum("rjgchp,rj->rgchp", t, place)
    t = jnp.einsum("rgchp,gh->rgcp", t, jnp.eye(2, dtype=t.dtype))
    return t.reshape(SSM_NG, SSM_GROUP, SSM_STATE)


def _scan_tables(zr, zi, reverse):
    zr, zi = _pair_lanes(zr), _pair_lanes(-zi if reverse else zi)
    k = jnp.arange(1, SEG + 1, dtype=F32)
    k = (k[::-1] if reverse else k)[None, :, None]
    mag = jnp.exp(k * zr[:, None, :])
    pw = jnp.stack([mag * jnp.cos(k * zi[:, None, :]), mag * jnp.sin(k * zi[:, None, :])], axis=1)
    first, last = (SEG - 1, 0) if reverse else (0, SEG - 1)
    big = [(pw[:, 0, last], pw[:, 1, last])]
    for _ in range(2):
        br, bi = big[-1]
        big.append((br * br - bi * bi, 2.0 * br * bi))
    rows = jnp.arange(8)[None, :, None]
    tiles = [jnp.broadcast_to(pw[:, comp, first][:, None, :], (N_PAIR, 8, 128)) for comp in (0, 1)]
    for lvl, step in enumerate((1, 2, 4)):
        keep = (rows <= 7 - step) if reverse else (rows >= step)
        for part in big[lvl]:
            tiles.append(jnp.where(keep, part[:, None, :], 0.0))
    return jnp.stack(tiles, axis=1), jnp.broadcast_to(pw[:, :, :, None, :], (N_PAIR, 2, SEG, 8, 128))


def _pairs_to_chunks(t):
    n_chunk = N_PAIR // PAIRS_PER_CHUNK
    return jnp.swapaxes(t.reshape(n_chunk, PAIRS_PER_CHUNK, 128, 256), 1, 2).reshape(n_chunk, 128, XW)


def _chunks_to_pairs(t):
    n_chunk = N_PAIR // PAIRS_PER_CHUNK
    return jnp.swapaxes(t.reshape(n_chunk, 128, PAIRS_PER_CHUNK, 256), 1, 2).reshape(N_PAIR, 128, 256)


def _ssm_operands(w):
    ar, ai, bbr, bbi = _ssm_discretize(w["ssm_lambda_re"], w["ssm_lambda_im"], w["ssm_log_dt"], w["ssm_b_re"], w["ssm_b_im"])
    b_blk = jnp.concatenate([_chan_state_blocks(jnp.swapaxes(bbr, 1, 2)), _chan_state_blocks(jnp.swapaxes(bbi, 1, 2))], axis=2)
    c_blk = jnp.concatenate([_chan_state_blocks(w["ssm_c_re"]), -_chan_state_blocks(w["ssm_c_im"])], axis=2)
    dt = jnp.exp(w["ssm_log_dt"])[:, None]
    zr, zi = jnp.minimum(w["ssm_lambda_re"], LAMBDA_RE_MAX) * dt, w["ssm_lambda_im"] * dt
    b_cat, c_cat = _pairs_to_chunks(b_blk).astype(BF16), _pairs_to_chunks(c_blk).astype(BF16)
    return (b_cat, jnp.swapaxes(b_cat, 1, 2), c_cat, jnp.swapaxes(c_cat, 1, 2),
            *_scan_tables(zr, zi, False), *_scan_tables(zr, zi, True))


def _local_step(x, target, w, late_weights, on_grads):
    n_ex, seq, _ = x.shape
    lp = seq + BLOCK
    nb = lp // BLOCK
    n_rows = n_ex * lp
    g = {}

    head = jnp.concatenate([jnp.zeros((PAD, D_MODEL), F32), w["meta_tokens"]], axis=0)
    h0 = jnp.concatenate([jnp.broadcast_to(head[None], (n_ex, BLOCK, D_MODEL)), x], axis=1).reshape(n_rows, D_MODEL)

    qkv, hn_a = _rms_mm_cols(h0, w["attn_norm_w"], w["attn_w_qkv"], "qkv_fwd")
    att, lse = _attn_fwd(qkv, w["attn_sinks"], n_ex, nb)
    h1 = _mm_acc(att, w["attn_w_o"], False, "attn_out_fwd", res=h0)
    w = {**w, **late_weights(0, att)}
    h2, a0, hn_m0 = _mlp_fwd(h1, w["mlp_norm_w"][0:1], w["mlp_w_up"][0], w["mlp_w_down"][0], 0, "mlp0_fwd")
    late = late_weights(1, h2)
    w["ssm_w_glu"] = late["ssm_w_glu"]
    w["mlp_w_up"], w["mlp_w_down"] = w["mlp_w_up"] + late["mlp_w_up"], w["mlp_w_down"] + late["mlp_w_down"]

    ops = w["ssm_operands"] if "ssm_operands" in w else _ssm_operands(w)
    b_pad, bt_pad, ct_pad, c_pad, tab_fwd, pw_fwd, tab_rev, pw_rev = ops
    u = _rms_fwd(h2, w["ssm_norm_w"], "ssm_norm_fwd")
    yg, y, xs = _ssm_fwd(u, b_pad, c_pad, tab_fwd, pw_fwd, w["ssm_d"], n_ex, lp)
    z = _mm_cols(yg, w["ssm_w_glu"], False, "glu_mm_fwd")
    h3 = _glu_fwd(h2, z)
    h4, a1, hn_m1 = _mlp_fwd(h3, w["mlp_norm_w"][1:2], w["mlp_w_up"][1], w["mlp_w_down"][1], 0, "mlp1_fwd")

    dh4, loss_tile, dnorm_f = _loss_head(h4, w["final_norm_w"], target.reshape(n_ex * seq, D_MODEL), n_ex, nb)

    def mlp_bwd(dh_out, h_in, a, hn, layer, tag, norm_w):
        dhn, dw_up, dw_down = None, None, None
        for s in range(N_CHIPS):
            final = s == N_CHIPS - 1
            res = _mlp_bwd_shard(s, dh_out, a, hn, dhn, h_in if final else None, norm_w,
                                 w["mlp_w_up"][layer], w["mlp_w_down"][layer], dw_up, dw_down, f"{tag}_bwd{s}")
            dhn, dw_up, dw_down = res[:3]
        return dhn, res[3], dw_up, dw_down

    dh3, dnorm_m1, dwu1, dwd1 = mlp_bwd(dh4, h3, a1, hn_m1, 1, "mlp1", w["mlp_norm_w"][1:2])
    tok = on_grads("mlp1", {"mlp_w_up": dwu1, "mlp_w_down": dwd1})
    dz = _glu_bwd(dh3, z)
    dyg = _mm_acc(dz, w["ssm_w_glu"], True, "glu_mm_dx", out_dtype=BF16)
    g["ssm_w_glu"] = _mm_tn(yg, dz, N_CHIPS, False, "glu_mm_dw")
    du, db_blk, dc_blk, da_t, dd_t = _ssm_bwd(dyg, y, u, xs, ct_pad, bt_pad, tab_rev, pw_rev, w["ssm_d"] + tok, n_ex, lp)
    dh2, dnorm_s = _rms_bwd_call(du, h2, w["ssm_norm_w"], dh3, "ssm_norm_bwd")
    db_blk, dc_blk = _chunks_to_pairs(db_blk), _chunks_to_pairs(dc_blk)
    g["ssm_c_re"] = _chan_state_unblock(dc_blk[:, :, 0:128])
    g["ssm_c_im"] = -_chan_state_unblock(dc_blk[:, :, 128:256])
    g_bbr = jnp.swapaxes(_chan_state_unblock(db_blk[:, :, 0:128]), 1, 2)
    g_bbi = jnp.swapaxes(_chan_state_unblock(db_blk[:, :, 128:256]), 1, 2)
    g_a = jnp.sum(da_t, axis=2).reshape(N_PAIR, 2, 2, SSM_STATE)
    g_ar, g_ai = g_a[:, 0].reshape(SSM_NG, SSM_STATE), g_a[:, 1].reshape(SSM_NG, SSM_STATE)
    _, vjp = jax.vjp(_ssm_discretize, w["ssm_lambda_re"], w["ssm_lambda_im"], w["ssm_log_dt"], w["ssm_b_re"], w["ssm_b_im"])
    g["ssm_lambda_re"], g["ssm_lambda_im"], g["ssm_log_dt"], g["ssm_b_re"], g["ssm_b_im"] = vjp((g_ar, g_ai, g_bbr, g_bbi))
    tok = on_grads("ssm", g)
    g = {}
    dh1, dnorm_m0, dwu0, dwd0 = mlp_bwd(dh2, h1, a0, hn_m0, 0, "mlp0", w["mlp_norm_w"][0:1] + tok)
    datt = _mm_cols(dh1, w["attn_w_o"], True, "attn_out_dx")
    dw_o = _mm_tn(att, dh1, N_CHIPS, True, "attn_out_dw")
    tok = on_grads("mlp0", {"mlp_w_up": dwu0, "mlp_w_down": dwd0, "attn_w_o": dw_o})
    dqkv, dsink_rows = _attn_bwd(qkv, w["attn_sinks"] + tok, att, lse, datt, n_ex, nb)
    tok = on_grads("qkv", {"attn_w_qkv": _mm_tn(hn_a, dqkv, N_CHIPS, False, "qkv_dw")})
    dh0, dnorm_a = _mm_acc(dqkv, w["attn_w_qkv"], True, "qkv_dx", rms_bwd=(h0, w["attn_norm_w"] + tok, dh1))

    dh0 = dh0.reshape(n_ex, lp, D_MODEL)
    on_grads("rest", {
        "mlp_norm_w": jnp.stack([jnp.sum(dnorm_m0, axis=0), jnp.sum(dnorm_m1, axis=0)]),
        "final_norm_w": jnp.sum(dnorm_f, axis=0),
        "attn_norm_w": jnp.sum(dnorm_a, axis=0)[None],
        "ssm_norm_w": jnp.sum(dnorm_s, axis=0)[None],
        "attn_sinks": jnp.sum(dsink_rows, axis=0)[None],
        "ssm_d": jnp.sum(dd_t, axis=0)[None],
        "meta_tokens": jnp.sum(dh0[:, PAD:BLOCK], axis=0)})
    return loss_tile, dh0[:, BLOCK:]


_SHARDED_SMALL = ("meta_tokens", "ssm_norm_w", "ssm_d")
_REP_SSM = ("ssm_lambda_re", "ssm_lambda_im", "ssm_log_dt", "ssm_b_re", "ssm_b_im", "ssm_c_re", "ssm_c_im")
_REP_MISC = ("attn_norm_w", "attn_sinks", "mlp_norm_w", "final_norm_w")
_BIG = ("attn_w_qkv", "attn_w_o", "ssm_w_glu", "mlp_w_up", "mlp_w_down")


def _pack(parts, cols):
    flat = jnp.concatenate([p.reshape(-1) for p in parts])
    rows = -(-flat.shape[0] // (8 * cols)) * 8
    return jnp.pad(flat, (0, rows * cols - flat.shape[0])).reshape(rows, cols)


def _unpack(packed, like):
    flat = packed.reshape(-1)
    out, at = [], 0
    for p in like:
        out.append(flat[at:at + p.size].reshape(p.shape))
        at += p.size
    return out


def kernel(x, meta_tokens, attn_norm_w, attn_w_qkv, attn_sinks, attn_w_o, ssm_norm_w, ssm_lambda_re, ssm_lambda_im, ssm_log_dt, ssm_b_re, ssm_b_im, ssm_c_re, ssm_c_im, ssm_d, ssm_w_glu, mlp_norm_w, mlp_w_up, mlp_w_down, final_norm_w, loss_target, m_meta_tokens, m_attn_norm_w, m_attn_w_qkv, m_attn_sinks, m_attn_w_o, m_ssm_norm_w, m_ssm_lambda_re, m_ssm_lambda_im, m_ssm_log_dt, m_ssm_b_re, m_ssm_b_im, m_ssm_c_re, m_ssm_c_im, m_ssm_d, m_ssm_w_glu, m_mlp_norm_w, m_mlp_w_up, m_mlp_w_down, m_final_norm_w, v_meta_tokens, v_attn_norm_w, v_attn_w_qkv, v_attn_sinks, v_attn_w_o, v_ssm_norm_w, v_ssm_lambda_re, v_ssm_lambda_im, v_ssm_log_dt, v_ssm_b_re, v_ssm_b_im, v_ssm_c_re, v_ssm_c_im, v_ssm_d, v_ssm_w_glu, v_mlp_norm_w, v_mlp_w_up, v_mlp_w_down, v_final_norm_w):
    names = ("meta_tokens", "attn_norm_w", "attn_w_qkv", "attn_sinks", "attn_w_o", "ssm_norm_w", "ssm_lambda_re",
             "ssm_lambda_im", "ssm_log_dt", "ssm_b_re", "ssm_b_im", "ssm_c_re", "ssm_c_im", "ssm_d", "ssm_w_glu",
             "mlp_norm_w", "mlp_w_up", "mlp_w_down", "final_norm_w")
    wts = dict(zip(names, (meta_tokens, attn_norm_w, attn_w_qkv, attn_sinks, attn_w_o, ssm_norm_w, ssm_lambda_re,
                           ssm_lambda_im, ssm_log_dt, ssm_b_re, ssm_b_im, ssm_c_re, ssm_c_im, ssm_d, ssm_w_glu,
                           mlp_norm_w, mlp_w_up, mlp_w_down, final_norm_w)))
    mom = dict(zip(names, (m_meta_tokens, m_attn_norm_w, m_attn_w_qkv, m_attn_sinks, m_attn_w_o, m_ssm_norm_w,
                           m_ssm_lambda_re, m_ssm_lambda_im, m_ssm_log_dt, m_ssm_b_re, m_ssm_b_im, m_ssm_c_re,
                           m_ssm_c_im, m_ssm_d, m_ssm_w_glu, m_mlp_norm_w, m_mlp_w_up, m_mlp_w_down, m_final_norm_w)))
    var = dict(zip(names, (v_meta_tokens, v_attn_norm_w, v_attn_w_qkv, v_attn_sinks, v_attn_w_o, v_ssm_norm_w,
                           v_ssm_lambda_re, v_ssm_lambda_im, v_ssm_log_dt, v_ssm_b_re, v_ssm_b_im, v_ssm_c_re,
                           v_ssm_c_im, v_ssm_d, v_ssm_w_glu, v_mlp_norm_w, v_mlp_w_up, v_mlp_w_down, v_final_norm_w)))

    my_chip = 2 * lax.axis_index("x") + lax.axis_index("y")
    my_dev = 2 * my_chip + lax.axis_index("c")
    small_mine = _pack([wts[n] for n in _SHARDED_SMALL], 128)
    first = [attn_w_qkv.astype(BF16), attn_w_o.astype(BF16), small_mine]
    up16, down16 = mlp_w_up.astype(BF16), mlp_w_down.astype(BF16)
    mlp0 = [up16[0:1], down16[0:1]]
    rest = [ssm_w_glu.astype(BF16), up16[1:2], down16[1:2]]
    handles, _ = _split_start([(srcs, [_landing(a, my_chip, N_CHIPS) for a in srcs]) for srcs in (first, mlp0, rest)],
                              _gather_copies, N_GATHER_PEERS, "gather_start")
    full = {n: wts[n] for n in _REP_MISC}
    full["final_norm_w"] = final_norm_w[None]
    for n in _REP_SSM:
        full[n] = wts[n][0]
    full["ssm_operands"] = _ssm_operands(full)
    got = _split_wait(handles[0], full["ssm_operands"], _gather_copies, "gather_wait_first")
    full["attn_w_qkv"], full["attn_w_o"] = got[0], got[1]
    smalls = [_unpack(got[2][s], [wts[n] for n in _SHARDED_SMALL]) for s in range(N_CHIPS)]
    for k, n in enumerate(_SHARDED_SMALL):
        full[n] = jnp.concatenate([smalls[s][k] for s in range(N_CHIPS)], axis=1)

    def late_weights(stage, after):
        if stage == 0:
            up, down = _split_wait(handles[1], after, _gather_copies, "gather_wait_mlp0")
            return {"mlp_w_up": [up], "mlp_w_down": [down]}
        glu, up, down = _split_wait(handles[2], after, _gather_copies, "gather_wait_rest")
        return {"ssm_w_glu": glu, "mlp_w_up": [up], "mlp_w_down": [down]}

    def shard_cols(t):
        return jnp.swapaxes(t.reshape(t.shape[0], N_CHIPS, t.shape[1] // N_CHIPS), 0, 1)

    pending = {}

    def on_grads(tag, g):
        scatter = [g[n] for n in _BIG if n in g]
        whole = []
        if tag == "ssm":
            whole = [_pack([g[n] for n in _REP_SSM], D_MODEL)]
        if tag == "rest":
            parts = [shard_cols(g[n]) for n in _SHARDED_SMALL]
            scatter = [jnp.stack([_pack([p[s] for p in parts], 128) for s in range(N_CHIPS)])]
            whole = [_pack([g[n] for n in _REP_MISC], D_MODEL)]
        srcs = scatter + whole
        lands = [_landing(lax.dynamic_index_in_dim(a, my_chip, 0, keepdims=False), my_dev, N_DEV) for a in scatter]
        lands += [_landing(a, my_dev, N_DEV) for a in whole]
        hs, token = _split_start([(srcs, lands)], _exchange_copies(len(scatter)), N_EXCHANGE_PEERS, "exchange_start_" + tag)
        pending[tag] = (hs[0], len(scatter))
        return token[0, 0]

    loss_tile, grad_x = _local_step(x, loss_target, full, late_weights, on_grads)
    loss = lax.psum(loss_tile[0, 0], ("x", "y", "c"))

    recv = {}
    for tag, (handle, n_scatter) in pending.items():
        recv[tag] = _split_wait(handle, grad_x, _exchange_copies(n_scatter), "exchange_wait_" + tag)

    out = {}

    def update(tag, pieces, w2, m2, v2):
        return _adamw(pieces, w2, m2, v2, "adamw_" + tag)

    def update_weight(n, pieces):
        shp = wts[n].shape
        r2 = (math.prod(shp[:-1]), shp[-1])
        res = update(n, pieces, wts[n].reshape(r2), mom[n].reshape(r2), var[n].reshape(r2))
        out[n] = [t.reshape(shp) for t in res]

    update_weight("mlp_w_up", [recv["mlp0"][1], recv["mlp1"][0]])
    update_weight("mlp_w_down", [recv["mlp0"][2], recv["mlp1"][1]])
    update_weight("attn_w_o", [recv["mlp0"][0]])
    update_weight("ssm_w_glu", [recv["ssm"][0]])
    update_weight("attn_w_qkv", [recv["qkv"][0]])
    for tag, group, pieces, cols in (("small", _SHARDED_SMALL, recv["rest"][0], 128),
                                     ("rep_ssm", _REP_SSM, recv["ssm"][1], D_MODEL),
                                     ("rep_misc", _REP_MISC, recv["rest"][1], D_MODEL)):
        like = [wts[n] for n in group]
        res = update(tag, [pieces], _pack(like, cols), _pack([mom[n] for n in group], cols),
                     _pack([var[n] for n in group], cols))
        for k, n in enumerate(group):
            out[n] = [_unpack(t, like)[k] for t in res]

    return (loss, grad_x, *[out[n][0] for n in names], *[out[n][1] for n in names],
            *[out[n][2] for n in names], *[out[n][3] for n in names])
```

```python
import functools
import math

import jax
import jax.numpy as jnp
from jax import lax
from jax.experimental import pallas as pl
from jax.experimental.pallas import tpu as pltpu

F32 = jnp.float32
BF16 = jnp.bfloat16
SDS = jax.ShapeDtypeStruct

D_MODEL = 1024
N_HEADS = 16
N_KV = 4
GQA = N_HEADS // N_KV
HEAD_DIM = 64
BLOCK = 128
N_META = 16
PAD = BLOCK - N_META
QKV_DIM = (N_HEADS + 2 * N_KV) * HEAD_DIM
KV_DIM = 2 * N_KV * HEAD_DIM
D_FF = 4 * D_MODEL
N_CHIPS = 4
N_DEV = 8
SSM_GROUP = 16
SSM_NG = D_MODEL // SSM_GROUP
SSM_STATE = 64
N_PAIR = SSM_NG // 2
PAIRS_PER_CHUNK = 4
RMS_EPS = 1e-6
NEG_INF = -1e30
LAMBDA_RE_MAX = -1e-4
ADAM_LR, ADAM_B1, ADAM_B2, ADAM_EPS, ADAM_WD, ADAM_STEP = 0.001, 0.9, 0.999, 1e-08, 0.01, 10

TM = 384
MM_TILES = (768, 384)
MLP_FWD_TILES = (1056, 768, 384)
MLP_BWD_TILES = (768, 384)
TN_TILES = (1408, 768, 384)
VMEM_LIMIT = 56 * 1024 * 1024


def _params(n_grid):
    return pltpu.CompilerParams(dimension_semantics=("arbitrary",) * n_grid, vmem_limit_bytes=VMEM_LIMIT)


def _row_tile(n_rows, tiles):
    return next(t for t in tiles if n_rows % t == 0)


def _rms(h, w):
    r = lax.rsqrt(jnp.mean(h * h, axis=-1, keepdims=True) + RMS_EPS)
    return h * r * w


def _rms_bwd(dhn, h, w):
    r = lax.rsqrt(jnp.mean(h * h, axis=-1, keepdims=True) + RMS_EPS)
    g = dhn * w
    proj = jnp.sum(g * h, axis=-1, keepdims=True) * (1.0 / D_MODEL)
    return r * g - h * (r * r * r) * proj, dhn * h * r


def _fold8(t):
    return jnp.sum(t.reshape(t.shape[0] // 8, 8, t.shape[1]), axis=0)


def _gelu(y):
    return 0.5 * y * (1.0 + jnp.tanh(0.7978845608028654 * (y + 0.044715 * y * y * y)))


def _gelu_grad(y):
    t = jnp.tanh(0.7978845608028654 * (y + 0.044715 * y * y * y))
    return 0.5 * (1.0 + t) + 0.5 * y * (1.0 - t * t) * 0.7978845608028654 * (1.0 + 3.0 * 0.044715 * y * y)


def _w4_spec(w4):
    n_sh, _, k, n = w4.shape
    return pl.BlockSpec((n_sh, None, k, n), lambda i: (0, 0, 0, 0))


def _rms_mm_cols(h, wn, w4, name):
    n_rows = h.shape[0]
    n_sh, _, k, n = w4.shape
    tm = _row_tile(n_rows, MM_TILES)

    def body(h_ref, wn_ref, w_ref, o_ref, hn_ref):
        hn = _rms(h_ref[...], wn_ref[...]).astype(BF16)
        hn_ref[...] = hn
        for s in range(n_sh):
            o_ref[:, s * n:(s + 1) * n] = jnp.dot(hn, w_ref[s], preferred_element_type=F32).astype(o_ref.dtype)

    return pl.pallas_call(
        body, name=name, grid=(n_rows // tm,),
        in_specs=[pl.BlockSpec((tm, k), lambda i: (i, 0)), pl.BlockSpec((1, k), lambda i: (0, 0)), _w4_spec(w4)],
        out_specs=[pl.BlockSpec((tm, n_sh * n), lambda i: (i, 0)), pl.BlockSpec((tm, k), lambda i: (i, 0))],
        out_shape=[SDS((n_rows, n_sh * n), BF16), SDS((n_rows, k), BF16)],
        compiler_params=_params(1),
    )(h, wn, w4)


def _mm_cols(x, w4, trans_w, name):
    n_rows, kx = x.shape
    tm = _row_tile(n_rows, MM_TILES)
    n_sh, _, k, n = w4.shape
    n_out = k if trans_w else n
    dims = (((1,), (1,)), ((), ())) if trans_w else (((1,), (0,)), ((), ()))

    def body(x_ref, w_ref, o_ref):
        x16 = x_ref[...].astype(BF16)
        for s in range(n_sh):
            o_ref[:, s * n_out:(s + 1) * n_out] = lax.dot_general(
                x16, w_ref[s], dims, preferred_element_type=F32).astype(o_ref.dtype)

    return pl.pallas_call(
        body, name=name, grid=(n_rows // tm,),
        in_specs=[pl.BlockSpec((tm, kx), lambda i: (i, 0)), _w4_spec(w4)],
        out_specs=pl.BlockSpec((tm, n_sh * n_out), lambda i: (i, 0)),
        out_shape=SDS((n_rows, n_sh * n_out), BF16),
        compiler_params=_params(1),
    )(x, w4)


def _mm_acc(x, w4, trans_w, name, res=None, rms_bwd=None, out_dtype=F32):
    n_rows = x.shape[0]
    tm = _row_tile(n_rows, MM_TILES)
    n_sh, _, k, n = w4.shape
    kx, n_out = (n, k) if trans_w else (k, n)
    dims = (((1,), (1,)), ((), ())) if trans_w else (((1,), (0,)), ((), ()))

    def body(*refs):
        if rms_bwd is not None:
            x_ref, w_ref, h_ref, wn_ref, dres_ref, o_ref, dw_ref = refs
        elif res is not None:
            x_ref, w_ref, res_ref, o_ref = refs
        else:
            x_ref, w_ref, o_ref = refs
        acc = None
        for s in range(n_sh):
            part = lax.dot_general(x_ref[:, s * kx:(s + 1) * kx].astype(BF16), w_ref[s], dims, preferred_element_type=F32)
            acc = part if acc is None else acc + part
        if rms_bwd is not None:
            dh, dw_rows = _rms_bwd(acc, h_ref[...], wn_ref[...])
            o_ref[...] = (dres_ref[...] + dh).astype(o_ref.dtype)

            @pl.when(pl.program_id(0) == 0)
            def _():
                dw_ref[...] = jnp.zeros_like(dw_ref)

            dw_ref[...] += _fold8(dw_rows)
        elif res is not None:
            o_ref[...] = (res_ref[...] + acc).astype(o_ref.dtype)
        else:
            o_ref[...] = acc.astype(o_ref.dtype)

    row = lambda i: (i, 0)
    in_specs = [pl.BlockSpec((tm, n_sh * kx), row), _w4_spec(w4)]
    args = [x, w4]
    out_specs = pl.BlockSpec((tm, n_out), row)
    out_shape = SDS((n_rows, n_out), out_dtype)
    if rms_bwd is not None:
        h, wn, dres = rms_bwd
        in_specs += [pl.BlockSpec((tm, n_out), row), pl.BlockSpec((1, n_out), lambda i: (0, 0)),
                     pl.BlockSpec((tm, n_out), row)]
        args += [h, wn, dres]
        out_specs = [out_specs, pl.BlockSpec((8, n_out), lambda i: (0, 0))]
        out_shape = [out_shape, SDS((8, n_out), F32)]
    elif res is not None:
        in_specs.append(pl.BlockSpec((tm, n_out), row))
        args.append(res)
    return pl.pallas_call(
        body, name=name, grid=(n_rows // tm,), in_specs=in_specs, out_specs=out_specs, out_shape=out_shape,
        compiler_params=_params(1),
    )(*args)


def _mm_tn(a, b, n_sh, a_sharded, name):
    n_rows = a.shape[0]
    tm = _row_tile(n_rows, TN_TILES)
    ka = a.shape[1] // n_sh if a_sharded else a.shape[1]
    nb = b.shape[1] if a_sharded else b.shape[1] // n_sh
    n_i = n_rows // tm

    def body(a_ref, b_ref, o_ref, acc):
        i = pl.program_id(0)

        @pl.when(i == 0)
        def _():
            acc[...] = jnp.zeros_like(acc)

        for s in range(n_sh):
            a_s = a_ref[:, s * ka:(s + 1) * ka] if a_sharded else a_ref[...]
            b_s = b_ref[...] if a_sharded else b_ref[:, s * nb:(s + 1) * nb]
            acc[s] += lax.dot_general(a_s.astype(BF16), b_s.astype(BF16), (((0,), (0,)), ((), ())),
                                      preferred_element_type=F32)

        @pl.when(i == n_i - 1)
        def _():
            o_ref[...] = acc[...].astype(o_ref.dtype)

    return pl.pallas_call(
        body, name=name, grid=(n_i,),
        in_specs=[pl.BlockSpec((tm, a.shape[1]), lambda i: (i, 0)), pl.BlockSpec((tm, b.shape[1]), lambda i: (i, 0))],
        out_specs=pl.BlockSpec((n_sh, ka, nb), lambda i: (0, 0, 0)),
        out_shape=SDS((n_sh, ka, nb), BF16),
        scratch_shapes=[pltpu.VMEM((n_sh, ka, nb), F32)], compiler_params=_params(1),
    )(a, b)


def _mlp_fwd(h, wn, w_up4, w_down4, layer, name, next_norm=None):
    n_rows = h.shape[0]
    TM = _row_tile(n_rows, MLP_FWD_TILES)
    n_sh = w_up4.shape[0]
    f_sh = D_FF // n_sh

    def body(*refs):
        if next_norm is None:
            h_ref, wn_ref, wu_ref, wd_ref, o_ref, a_ref, hn_ref, hn_s, acc = refs
        else:
            h_ref, wn_ref, wu_ref, wd_ref, nn_ref, o_ref, a_ref, hn_ref, u_ref, hn_s, acc = refs
        s = pl.program_id(1)

        @pl.when(s == 0)
        def _():
            hn = _rms(h_ref[...], wn_ref[...]).astype(BF16)
            hn_s[...] = hn
            hn_ref[...] = hn
            acc[...] = jnp.zeros_like(acc)

        a = jnp.dot(hn_s[...], wu_ref[...], preferred_element_type=F32)
        a_ref[...] = a.astype(BF16)
        act = jnp.maximum(a, 0.0)
        acc[...] += jnp.dot((act * act).astype(BF16), wd_ref[...], preferred_element_type=F32)

        @pl.when(s == n_sh - 1)
        def _():
            out = h_ref[...] + acc[...]
            o_ref[...] = out
            if next_norm is not None:
                u_ref[...] = _rms(out, nn_ref[...])

    row = lambda i, s: (i, 0)
    vec = pl.BlockSpec((1, D_MODEL), lambda i, s: (0, 0))
    in_specs = [pl.BlockSpec((TM, D_MODEL), row), vec,
                pl.BlockSpec((None, None, D_MODEL, f_sh), lambda i, s: (s, layer, 0, 0)),
                pl.BlockSpec((None, None, f_sh, D_MODEL), lambda i, s: (s, layer, 0, 0))]
    out_specs = [pl.BlockSpec((TM, D_MODEL), row), pl.BlockSpec((TM, f_sh), lambda i, s: (i, s)),
                 pl.BlockSpec((TM, D_MODEL), row)]
    out_shape = [SDS((n_rows, D_MODEL), F32), SDS((n_rows, D_FF), BF16), SDS((n_rows, D_MODEL), BF16)]
    args = [h, wn, w_up4, w_down4]
    if next_norm is not None:
        in_specs.append(vec)
        args.append(next_norm)
        out_specs.append(pl.BlockSpec((TM, D_MODEL), row))
        out_shape.append(SDS((n_rows, D_MODEL), F32))
    return pl.pallas_call(
        body, name=name, grid=(n_rows // TM, n_sh), in_specs=in_specs, out_specs=out_specs, out_shape=out_shape,
        scratch_shapes=[pltpu.VMEM((TM, D_MODEL), BF16), pltpu.VMEM((TM, D_MODEL), F32)],
        compiler_params=_params(2),
    )(*args)


def _mlp_bwd_shard(s, dh, a, hn, dhn_prev, h, wn, w_up4, w_down4, dw_up_buf, dw_down_buf, name):
    n_rows = dh.shape[0]
    n_sh = w_up4.shape[0]
    f_sh = D_FF // n_sh
    tm = _row_tile(n_rows, MLP_BWD_TILES)
    n_i = n_rows // tm
    last = h is not None
    nt = (((1,), (1,)), ((), ()))
    tn = (((0,), (0,)), ((), ()))

    def body(*refs):
        refs = list(refs)
        dh_ref, a_ref, hn_ref, wu_ref, wd_ref = refs[:5]
        at = 5
        prev_ref = None
        if dhn_prev is not None:
            prev_ref = refs[at]
            at += 1
        if last:
            h_ref, wn_ref = refs[at:at + 2]
            at += 2
        if dw_up_buf is not None:
            at += 2
        o_ref, dwu_ref, dwd_ref = refs[at:at + 3]
        at += 3
        if last:
            dnorm_ref = refs[at]
            at += 1
        acc_u, acc_d = refs[at:at + 2]
        i = pl.program_id(0)

        @pl.when(i == 0)
        def _():
            acc_u[...] = jnp.zeros_like(acc_u)
            acc_d[...] = jnp.zeros_like(acc_d)
            if last:
                dnorm_ref[...] = jnp.zeros_like(dnorm_ref)

        dh16 = dh_ref[...].astype(BF16)
        r = jnp.maximum(a_ref[...].astype(F32), 0.0)
        dact = lax.dot_general(dh16, wd_ref[...], nt, preferred_element_type=F32)
        da16 = (dact * (2.0 * r)).astype(BF16)
        acc_d[...] += lax.dot_general((r * r).astype(BF16), dh16, tn, preferred_element_type=F32)
        acc_u[...] += lax.dot_general(hn_ref[...], da16, tn, preferred_element_type=F32)
        dhn = lax.dot_general(da16, wu_ref[...], nt, preferred_element_type=F32)
        if prev_ref is not None:
            dhn = dhn + prev_ref[...]
        if last:
            d_rms, dw_rows = _rms_bwd(dhn, h_ref[...], wn_ref[...])
            o_ref[...] = dh_ref[...] + d_rms
            dnorm_ref[...] += _fold8(dw_rows)
        else:
            o_ref[...] = dhn

        @pl.when(i == n_i - 1)
        def _():
            dwu_ref[...] = acc_u[...].astype(BF16)
            dwd_ref[...] = acc_d[...].astype(BF16)

    row = lambda i: (i, 0)
    tile = pl.BlockSpec((tm, D_MODEL), row)
    in_specs = [tile, pl.BlockSpec((tm, f_sh), lambda i: (i, s)), tile,
                pl.BlockSpec((None, None, D_MODEL, f_sh), lambda i: (s, 0, 0, 0)),
                pl.BlockSpec((None, None, f_sh, D_MODEL), lambda i: (s, 0, 0, 0))]
    args = [dh, a, hn, w_up4, w_down4]
    if dhn_prev is not None:
        in_specs.append(tile)
        args.append(dhn_prev)
    if last:
        in_specs += [tile, pl.BlockSpec((1, D_MODEL), lambda i: (0, 0))]
        args += [h, wn]
    aliases = {}
    if dw_up_buf is not None:
        aliases = {len(args): 1, len(args) + 1: 2}
        in_specs += [pl.BlockSpec(memory_space=pl.ANY)] * 2
        args += [dw_up_buf, dw_down_buf]
    out_specs = [tile, pl.BlockSpec((None, D_MODEL, f_sh), lambda i: (s, 0, 0)),
                 pl.BlockSpec((None, f_sh, D_MODEL), lambda i: (s, 0, 0))]
    out_shape = [SDS((n_rows, D_MODEL), F32), SDS((n_sh, D_MODEL, f_sh), BF16), SDS((n_sh, f_sh, D_MODEL), BF16)]
    if last:
        out_specs.append(pl.BlockSpec((8, D_MODEL), lambda i: (0, 0)))
        out_shape.append(SDS((8, D_MODEL), F32))
    return pl.pallas_call(
        body, name=name, grid=(n_i,), in_specs=in_specs, out_specs=out_specs, out_shape=out_shape,
        input_output_aliases=aliases,
        scratch_shapes=[pltpu.VMEM((D_MODEL, f_sh), F32), pltpu.VMEM((f_sh, D_MODEL), F32)],
        compiler_params=_params(1),
    )(*args)


def _attn_masks(n):
    qi = lax.broadcasted_iota(jnp.int32, (BLOCK, 3 * BLOCK), 0)
    col = lax.broadcasted_iota(jnp.int32, (BLOCK, 3 * BLOCK), 1)
    kj = col - BLOCK
    dist = BLOCK + qi - kj
    kmin = jnp.where(n == 0, 2 * BLOCK, jnp.where(n == 1, BLOCK, 0))
    band_ok = (col >= BLOCK) & (dist >= 0) & (dist < BLOCK) & (kj >= kmin)
    q_pos = n * BLOCK + qi - PAD
    meta_ok = (col >= PAD) & (col < BLOCK) & (col - PAD <= q_pos)
    distf = jnp.where(col >= BLOCK, dist, 0).astype(F32)
    return band_ok | meta_ok, distf


def _alibi_slope(h):
    return float(2.0 ** (-8.0 * (h + 1) / N_HEADS))


def _attn_bias(n, bias_s):
    ok, distf = _attn_masks(n)
    for h in range(N_HEADS):
        bias_s[h] = jnp.where(ok, -_alibi_slope(h) * distf, NEG_INF)


def _attn_fwd(qkv, sinks, n_ex, nb):
    n_rows = qkv.shape[0]
    kvb = N_HEADS * HEAD_DIM // KV_DIM

    def body(sink_ref, q_ref, kvm_ref, kvp_ref, kvc_ref, o_ref, lse_ref, k_s, v_s, q_s, bias_s):
        n = pl.program_id(1)

        @pl.when(n <= 2)
        def _():
            _attn_bias(n, bias_s)

        v_s[...] = jnp.ones_like(v_s)
        for part, ref in enumerate((kvm_ref, kvp_ref, kvc_ref)):
            rows = slice(part * BLOCK, (part + 1) * BLOCK)
            k_s[rows, :] = ref[:, 0:N_KV * HEAD_DIM]
            for kv in range(N_KV):
                v_s[rows, kv * 2 * HEAD_DIM:kv * 2 * HEAD_DIM + HEAD_DIM] = \
                    ref[:, (N_KV + kv) * HEAD_DIM:(N_KV + kv + 1) * HEAD_DIM]
        for kv in range(N_KV):
            for g in range(GQA):
                h = kv * GQA + g
                q_s[kv, g * BLOCK:(g + 1) * BLOCK, :] = q_ref[:, h * HEAD_DIM:(h + 1) * HEAD_DIM] * (HEAD_DIM ** -0.5)
            s4 = lax.dot_general(q_s[kv], k_s[:, kv * HEAD_DIM:(kv + 1) * HEAD_DIM], (((1,), (1,)), ((), ())),
                                 preferred_element_type=F32)
            es, ms, sink_es = [], [], []
            for g in range(GQA):
                h = kv * GQA + g
                s = s4[g * BLOCK:(g + 1) * BLOCK] + bias_s[h]
                sink = sink_ref[0, h]
                m = jnp.maximum(jnp.max(s, axis=-1, keepdims=True), sink)
                es.append(jnp.exp(s - m).astype(BF16))
                ms.append(m)
                sink_es.append(jnp.exp(sink - m))
            pv = jnp.dot(jnp.concatenate(es, axis=0), v_s[:, kv * 2 * HEAD_DIM:(kv + 1) * 2 * HEAD_DIM],
                         preferred_element_type=F32)
            for g in range(GQA):
                h = kv * GQA + g
                pg = pv[g * BLOCK:(g + 1) * BLOCK]
                l = pg[:, HEAD_DIM:HEAD_DIM + 1] + sink_es[g]
                o_ref[:, h * HEAD_DIM:(h + 1) * HEAD_DIM] = (pg[:, 0:HEAD_DIM] * (1.0 / l)).astype(BF16)
                lse_ref[:, h:h + 1] = ms[g] + jnp.log(l)

    return pl.pallas_call(
        body, name="attn_fwd", grid=(n_ex, nb),
        in_specs=[pl.BlockSpec(memory_space=pltpu.SMEM),
                  pl.BlockSpec((BLOCK, N_HEADS * HEAD_DIM), lambda b, n: (b * nb + n, 0)),
                  pl.BlockSpec((BLOCK, KV_DIM), lambda b, n: (b * nb, kvb)),
                  pl.BlockSpec((BLOCK, KV_DIM), lambda b, n: (b * nb + jnp.maximum(n - 1, 0), kvb)),
                  pl.BlockSpec((BLOCK, KV_DIM), lambda b, n: (b * nb + n, kvb))],
        out_specs=[pl.BlockSpec((BLOCK, N_HEADS * HEAD_DIM), lambda b, n: (b * nb + n, 0)),
                   pl.BlockSpec((BLOCK, N_HEADS), lambda b, n: (b * nb + n, 0))],
        out_shape=[SDS((n_rows, N_HEADS * HEAD_DIM), BF16), SDS((n_rows, N_HEADS), F32)],
        scratch_shapes=[pltpu.VMEM((3 * BLOCK, N_KV * HEAD_DIM), BF16), pltpu.VMEM((3 * BLOCK, 2 * N_KV * HEAD_DIM), BF16),
                        pltpu.VMEM((N_KV, GQA * BLOCK, HEAD_DIM), BF16), pltpu.VMEM((N_HEADS, BLOCK, 3 * BLOCK), F32)],
        compiler_params=_params(2),
    )(sinks, qkv, qkv, qkv, qkv)


def _attn_bwd(qkv, sinks, o, lse, do, n_ex, nb):
    n_rows = qkv.shape[0]
    kvb = N_HEADS * HEAD_DIM // KV_DIM
    scale = HEAD_DIM ** -0.5
    nq = lambda r: nb - 1 - r

    def body(sink_ref, q_ref, kvm_ref, kvp_ref, kvc_ref, o_ref, lse_ref, do_ref, dqkv_ref, dsink_ref,
             k_s, v_s, dkv_s, carry_s, meta_s, q_s, do_s, bias_s):
        b, r = pl.program_id(0), pl.program_id(1)
        n = nq(r)

        @pl.when((r == 0) | (n <= 1))
        def _():
            _attn_bias(n, bias_s)

        @pl.when((b == 0) & (r == 0))
        def _():
            dsink_ref[...] = jnp.zeros_like(dsink_ref)

        @pl.when(r == 0)
        def _():
            carry_s[...] = jnp.zeros_like(carry_s)
            meta_s[...] = jnp.zeros_like(meta_s)

        for part, ref in enumerate((kvm_ref, kvp_ref, kvc_ref)):
            k_s[part * BLOCK:(part + 1) * BLOCK, :] = ref[:, 0:N_KV * HEAD_DIM]
            v_s[part * BLOCK:(part + 1) * BLOCK, :] = ref[:, N_KV * HEAD_DIM:KV_DIM]
        nt = (((1,), (1,)), ((), ()))
        tn = (((0,), (0,)), ((), ()))
        for kv in range(N_KV):
            kcols = slice(kv * HEAD_DIM, (kv + 1) * HEAD_DIM)
            vcols = slice(N_KV * HEAD_DIM + kv * HEAD_DIM, N_KV * HEAD_DIM + (kv + 1) * HEAD_DIM)
            for g in range(GQA):
                cols = slice((kv * GQA + g) * HEAD_DIM, (kv * GQA + g + 1) * HEAD_DIM)
                q_s[kv, g * BLOCK:(g + 1) * BLOCK, :] = q_ref[:, cols] * scale
                do_s[kv, g * BLOCK:(g + 1) * BLOCK, :] = do_ref[:, cols]
            kh, vh = k_s[:, kcols], v_s[:, kcols]
            s4 = lax.dot_general(q_s[kv], kh, nt, preferred_element_type=F32)
            dp4 = lax.dot_general(do_s[kv], vh, nt, preferred_element_type=F32)
            ps, dss = [], []
            for g in range(GQA):
                h = kv * GQA + g
                cols = slice(h * HEAD_DIM, (h + 1) * HEAD_DIM)
                rows = slice(g * BLOCK, (g + 1) * BLOCK)
                s = s4[rows] + bias_s[h]
                lse_h = lse_ref[:, h:h + 1]
                p = jnp.exp(s - lse_h)
                delta = jnp.sum(do_ref[:, cols].astype(F32) * o_ref[:, cols].astype(F32), axis=-1, keepdims=True)
                dsink_ref[:, h:h + 1] += -jnp.exp(sink_ref[0, h] - lse_h) * delta
                ps.append(p.astype(BF16))
                dss.append((p * (dp4[rows] - delta)).astype(BF16))
            p4, ds4 = jnp.concatenate(ps, axis=0), jnp.concatenate(dss, axis=0)
            dq4 = jnp.dot(ds4, kh, preferred_element_type=F32) * scale
            for g in range(GQA):
                cols = slice((kv * GQA + g) * HEAD_DIM, (kv * GQA + g + 1) * HEAD_DIM)
                dqkv_ref[:, cols] = dq4[g * BLOCK:(g + 1) * BLOCK].astype(BF16)
            dkv_s[:, kcols] = lax.dot_general(ds4, q_s[kv], tn, preferred_element_type=F32)
            dkv_s[:, vcols] = lax.dot_general(p4, do_s[kv], tn, preferred_element_type=F32)

        meta_s[...] += dkv_s[0:BLOCK, :]
        cur = dkv_s[2 * BLOCK:3 * BLOCK, :] + carry_s[...]
        carry_s[...] = dkv_s[BLOCK:2 * BLOCK, :]

        @pl.when(n > 0)
        def _():
            dqkv_ref[:, N_HEADS * HEAD_DIM:QKV_DIM] = cur.astype(BF16)

        @pl.when(n == 0)
        def _():
            dqkv_ref[:, N_HEADS * HEAD_DIM:QKV_DIM] = (cur + meta_s[...]).astype(BF16)

    blk = lambda b, r: (b * nb + nq(r), 0)
    return pl.pallas_call(
        body, name="attn_bwd", grid=(n_ex, nb),
        in_specs=[pl.BlockSpec(memory_space=pltpu.SMEM),
                  pl.BlockSpec((BLOCK, N_HEADS * HEAD_DIM), blk),
                  pl.BlockSpec((BLOCK, KV_DIM), lambda b, r: (b * nb, kvb)),
                  pl.BlockSpec((BLOCK, KV_DIM), lambda b, r: (b * nb + jnp.maximum(nq(r) - 1, 0), kvb)),
                  pl.BlockSpec((BLOCK, KV_DIM), lambda b, r: (b * nb + nq(r), kvb)),
                  pl.BlockSpec((BLOCK, N_HEADS * HEAD_DIM), blk),
                  pl.BlockSpec((BLOCK, N_HEADS), blk),
                  pl.BlockSpec((BLOCK, N_HEADS * HEAD_DIM), blk)],
        out_specs=[pl.BlockSpec((BLOCK, QKV_DIM), blk),
                   pl.BlockSpec((BLOCK, N_HEADS), lambda b, r: (0, 0))],
        out_shape=[SDS((n_rows, QKV_DIM), BF16), SDS((BLOCK, N_HEADS), F32)],
        scratch_shapes=[pltpu.VMEM((3 * BLOCK, N_KV * HEAD_DIM), BF16), pltpu.VMEM((3 * BLOCK, N_KV * HEAD_DIM), BF16),
                        pltpu.VMEM((3 * BLOCK, KV_DIM), F32), pltpu.VMEM((BLOCK, KV_DIM), F32),
                        pltpu.VMEM((BLOCK, KV_DIM), F32), pltpu.VMEM((N_KV, GQA * BLOCK, HEAD_DIM), BF16),
                        pltpu.VMEM((N_KV, GQA * BLOCK, HEAD_DIM), BF16), pltpu.VMEM((N_HEADS, BLOCK, 3 * BLOCK), F32)],
        compiler_params=_params(2),
    )(sinks, qkv, qkv, qkv, qkv, o, lse, do)


SEG = TM // 8
XW = 256 * PAIRS_PER_CHUNK


def _cmul_add(xr, xi, mr, mi, sr, si):
    return xr + mr * sr - mi * si, xi + mr * si + mi * sr


def _to_segments(src_ref, dst):
    for s in range(SEG):
        dst[s * 8:(s + 1) * 8, :] = src_ref[pl.ds(s, 8, stride=SEG), :]


def _from_segments(src, i):
    return src[pl.ds(i, SEG, stride=8), :]


def _scan_segments(buf, tab_ref, pw_ref, carry_s, reverse):
    shifts = (7, 6, 4) if reverse else (1, 2, 4)
    row_id = lax.broadcasted_iota(jnp.int32, (8, 128), 0)

    def local(si, prev):
        s = (SEG - 1 - si) if reverse else si
        row = pl.multiple_of(s * 8, 8)
        out = []
        for j in range(PAIRS_PER_CHUNK):
            re, im = slice(256 * j, 256 * j + 128), slice(256 * j + 128, 256 * j + 256)
            xr, xi = _cmul_add(buf[pl.ds(row, 8), re], buf[pl.ds(row, 8), im],
                               tab_ref[j, 0], tab_ref[j, 1], prev[2 * j], prev[2 * j + 1])
            buf[pl.ds(row, 8), re] = xr
            buf[pl.ds(row, 8), im] = xi
            out += [xr, xi]
        return tuple(out)

    zero = jnp.zeros((8, 128), F32)
    edge = lax.fori_loop(0, SEG, local, (zero,) * (2 * PAIRS_PER_CHUNK))

    entering = []
    for j in range(PAIRS_PER_CHUNK):
        er, ei = edge[2 * j], edge[2 * j + 1]
        if reverse:
            sr = jnp.where(row_id == 7, carry_s[2 * j], pltpu.roll(er, 7, 0))
            si_ = jnp.where(row_id == 7, carry_s[2 * j + 1], pltpu.roll(ei, 7, 0))
        else:
            sr = jnp.where(row_id == 0, carry_s[2 * j], pltpu.roll(er, 1, 0))
            si_ = jnp.where(row_id == 0, carry_s[2 * j + 1], pltpu.roll(ei, 1, 0))
        for lvl, sh in enumerate(shifts):
            sr, si_ = _cmul_add(sr, si_, tab_ref[j, 2 + 2 * lvl], tab_ref[j, 3 + 2 * lvl],
                                pltpu.roll(sr, sh, 0), pltpu.roll(si_, sh, 0))
        entering += [sr, si_]
        tr, ti = _cmul_add(er, ei, tab_ref[j, 2], tab_ref[j, 3], sr, si_)
        out_row = slice(0, 1) if reverse else slice(7, 8)
        carry_s[2 * j] = jnp.broadcast_to(tr[out_row], (8, 128))
        carry_s[2 * j + 1] = jnp.broadcast_to(ti[out_row], (8, 128))

    def fix(s, _):
        row = pl.multiple_of(s * 8, 8)
        for j in range(PAIRS_PER_CHUNK):
            re, im = slice(256 * j, 256 * j + 128), slice(256 * j + 128, 256 * j + 256)
            xr, xi = _cmul_add(buf[pl.ds(row, 8), re], buf[pl.ds(row, 8), im],
                               pw_ref[j, 0, s], pw_ref[j, 1, s], entering[2 * j], entering[2 * j + 1])
            buf[pl.ds(row, 8), re] = xr
            buf[pl.ds(row, 8), im] = xi
        return 0

    lax.fori_loop(0, SEG, fix, 0)


def _ssm_fwd(u, b_pad, c_pad, tab, pw, d_skip, n_ex, lp):
    n_rows = u.shape[0]
    n_t = lp // TM
    n_chunk = D_MODEL // 128

    def body(u_ref, bp_ref, cp_ref, tab_ref, pw_ref, d_ref, yg_ref, y_ref, xs_ref, buf, carry_s, us, ys):
        @pl.when(pl.program_id(2) == 0)
        def _():
            carry_s[...] = jnp.zeros_like(carry_s)

        _to_segments(u_ref, us)
        ub = us[...]
        u16 = ub.astype(BF16)
        buf[...] = jnp.dot(u16, bp_ref[...], preferred_element_type=F32)
        _scan_segments(buf, tab_ref, pw_ref, carry_s, reverse=False)
        xb = buf[...].astype(BF16)
        xs_ref[...] = xb
        ys[...] = d_ref[...] * ub + jnp.dot(xb, cp_ref[...], preferred_element_type=F32)
        for i in range(8):
            yi = _from_segments(ys, i)
            y_ref[i * SEG:(i + 1) * SEG, :] = yi
            yg_ref[i * SEG:(i + 1) * SEG, :] = _gelu(yi).astype(BF16)

    rows = lambda b, q, t: (b * n_t + t, q)
    return pl.pallas_call(
        body, name="ssm_fwd", grid=(n_ex, n_chunk, n_t),
        in_specs=[pl.BlockSpec((TM, 128), rows),
                  pl.BlockSpec((None, 128, XW), lambda b, q, t: (q, 0, 0)),
                  pl.BlockSpec((None, XW, 128), lambda b, q, t: (q, 0, 0)),
                  pl.BlockSpec((PAIRS_PER_CHUNK, 8, 8, 128), lambda b, q, t: (q, 0, 0, 0)),
                  pl.BlockSpec((PAIRS_PER_CHUNK, 2, SEG, 8, 128), lambda b, q, t: (q, 0, 0, 0, 0)),
                  pl.BlockSpec((1, 128), lambda b, q, t: (0, q))],
        out_specs=[pl.BlockSpec((TM, 128), rows), pl.BlockSpec((TM, 128), rows),
                   pl.BlockSpec((None, TM, XW), lambda b, q, t: (q, b * n_t + t, 0))],
        out_shape=[SDS((n_rows, D_MODEL), BF16), SDS((n_rows, D_MODEL), F32), SDS((n_chunk, n_rows, XW), BF16)],
        scratch_shapes=[pltpu.VMEM((TM, XW), F32), pltpu.VMEM((2 * PAIRS_PER_CHUNK, 8, 128), F32),
                        pltpu.VMEM((TM, 128), F32), pltpu.VMEM((TM, 128), F32)],
        compiler_params=_params(3),
    )(u, b_pad, c_pad, tab, pw, d_skip)


def _ssm_bwd(dyg, y, u, xs, ct_pad, bt_pad, tab_rev, pw_rev, d_skip, n_ex, lp):
    n_rows = u.shape[0]
    n_t = lp // TM
    n_chunk = D_MODEL // 128
    tile = lambda q, b, t: (b * n_t + (n_t - 1 - t), q)

    def body(dyg_ref, y_ref, u_ref, xs_ref, xp_ref, ct_ref, bt_ref, tab_ref, pw_ref, d_ref,
             du_ref, db_ref, dc_ref, da_ref, dd_ref, buf, xf, carry_s, us, dys, dyp):
        b, t = pl.program_id(1), pl.program_id(2)

        @pl.when((b == 0) & (t == 0))
        def _():
            db_ref[...] = jnp.zeros_like(db_ref)
            dc_ref[...] = jnp.zeros_like(dc_ref)
            da_ref[...] = jnp.zeros_like(da_ref)
            dd_ref[...] = jnp.zeros_like(dd_ref)

        @pl.when(t == 0)
        def _():
            carry_s[...] = jnp.zeros_like(carry_s)

        dys[...] = dyg_ref[...].astype(F32) * _gelu_grad(y_ref[...])
        dd_ref[...] += _fold8(dys[...] * u_ref[...])
        _to_segments(dys, dyp)
        dy = dyp[...]
        _to_segments(u_ref, us)
        dy16 = dy.astype(BF16)
        first_tile = t == n_t - 1
        tn = (((0,), (0,)), ((), ()))
        buf[...] = jnp.dot(dy16, ct_ref[...], preferred_element_type=F32)
        dc_ref[...] += lax.dot_general(dy16, xs_ref[...], tn, preferred_element_type=F32)
        xf[16:16 + TM, :] = xs_ref[...].astype(F32)
        xf[0:16, :] = jnp.where(first_tile, 0.0, xp_ref[...].astype(F32))
        _scan_segments(buf, tab_ref, pw_ref, carry_s, reverse=True)
        g16 = buf[...].astype(BF16)
        dys[...] = d_ref[...] * dy + jnp.dot(g16, bt_ref[...], preferred_element_type=F32)
        db_ref[...] += lax.dot_general(us[...].astype(BF16), g16, tn, preferred_element_type=F32)
        row_id = lax.broadcasted_iota(jnp.int32, (8, 128), 0)
        for j in range(PAIRS_PER_CHUNK):
            re, im = slice(256 * j, 256 * j + 128), slice(256 * j + 128, 256 * j + 256)
            first = [jnp.where(row_id == 0, jnp.broadcast_to(xf[15:16, c], (8, 128)),
                               pltpu.roll(xf[8 + TM:16 + TM, c], 1, 0)) for c in (re, im)]
            for rows, pr, pi in ((slice(0, 8), first[0], first[1]),
                                 (slice(8, TM), xf[16:8 + TM, re], xf[16:8 + TM, im])):
                gr, gi = buf[rows, re], buf[rows, im]
                da_ref[j, 0] += _fold8(gr * pr + gi * pi)
                da_ref[j, 1] += _fold8(gi * pr - gr * pi)
        for i in range(8):
            du_ref[i * SEG:(i + 1) * SEG, :] = _from_segments(dys, i)

    prev16 = lambda q, b, t: (q, jnp.maximum((b * n_t + (n_t - 1 - t)) * (TM // 16) - 1, 0), 0)
    return pl.pallas_call(
        body, name="ssm_bwd", grid=(n_chunk, n_ex, n_t),
        in_specs=[pl.BlockSpec((TM, 128), tile), pl.BlockSpec((TM, 128), tile), pl.BlockSpec((TM, 128), tile),
                  pl.BlockSpec((None, TM, XW), lambda q, b, t: (q, b * n_t + (n_t - 1 - t), 0)),
                  pl.BlockSpec((None, 16, XW), prev16),
                  pl.BlockSpec((None, 128, XW), lambda q, b, t: (q, 0, 0)),
                  pl.BlockSpec((None, XW, 128), lambda q, b, t: (q, 0, 0)),
                  pl.BlockSpec((PAIRS_PER_CHUNK, 8, 8, 128), lambda q, b, t: (q, 0, 0, 0)),
                  pl.BlockSpec((PAIRS_PER_CHUNK, 2, SEG, 8, 128), lambda q, b, t: (q, 0, 0, 0, 0)),
                  pl.BlockSpec((1, 128), lambda q, b, t: (0, q))],
        out_specs=[pl.BlockSpec((TM, 128), tile),
                   pl.BlockSpec((None, 128, XW), lambda q, b, t: (q, 0, 0)),
                   pl.BlockSpec((None, 128, XW), lambda q, b, t: (q, 0, 0)),
                   pl.BlockSpec((PAIRS_PER_CHUNK, 2, 8, 128), lambda q, b, t: (q, 0, 0, 0)),
                   pl.BlockSpec((8, 128), lambda q, b, t: (0, q))],
        out_shape=[SDS((n_rows, D_MODEL), F32), SDS((n_chunk, 128, XW), F32), SDS((n_chunk, 128, XW), F32),
                   SDS((N_PAIR, 2, 8, 128), F32), SDS((8, D_MODEL), F32)],
        scratch_shapes=[pltpu.VMEM((TM, XW), F32), pltpu.VMEM((TM + 16, XW), F32),
                        pltpu.VMEM((2 * PAIRS_PER_CHUNK, 8, 128), F32), pltpu.VMEM((TM, 128), F32),
                        pltpu.VMEM((TM, 128), F32), pltpu.VMEM((TM, 128), F32)],
        compiler_params=_params(3),
    )(dyg, y, u, xs, xs, ct_pad, bt_pad, tab_rev, pw_rev, d_skip)


def _rms_bwd_call(dhn, h, wn, dres, name):
    n_rows = h.shape[0]

    def body(dhn_ref, h_ref, wn_ref, dres_ref, o_ref, dw_ref):
        @pl.when(pl.program_id(0) == 0)
        def _():
            dw_ref[...] = jnp.zeros_like(dw_ref)

        dh, dw_rows = _rms_bwd(dhn_ref[...], h_ref[...], wn_ref[...])
        o_ref[...] = dres_ref[...] + dh
        dw_ref[...] += _fold8(dw_rows)

    row = lambda i: (i, 0)
    return pl.pallas_call(
        body, name=name, grid=(n_rows // TM,),
        in_specs=[pl.BlockSpec((TM, D_MODEL), row), pl.BlockSpec((TM, D_MODEL), row),
                  pl.BlockSpec((1, D_MODEL), lambda i: (0, 0)), pl.BlockSpec((TM, D_MODEL), row)],
        out_specs=[pl.BlockSpec((TM, D_MODEL), row), pl.BlockSpec((8, D_MODEL), lambda i: (0, 0))],
        out_shape=[SDS((n_rows, D_MODEL), F32), SDS((8, D_MODEL), F32)], compiler_params=_params(1),
    )(dhn, h, wn, dres)


def _glu_fwd(h, z):
    n_rows = h.shape[0]

    def body(h_ref, val_ref, gate_ref, o_ref):
        o_ref[...] = h_ref[...] + val_ref[...].astype(F32) * jax.nn.sigmoid(gate_ref[...].astype(F32))

    row = lambda i: (i, 0)
    return pl.pallas_call(
        body, name="glu_fwd", grid=(n_rows // TM,),
        in_specs=[pl.BlockSpec((TM, D_MODEL), row), pl.BlockSpec((TM, D_MODEL), row),
                  pl.BlockSpec((TM, D_MODEL), lambda i: (i, 1))],
        out_specs=pl.BlockSpec((TM, D_MODEL), row),
        out_shape=SDS((n_rows, D_MODEL), F32), compiler_params=_params(1),
    )(h, z, z)


def _glu_bwd(dh, z):
    n_rows = dh.shape[0]

    def body(dh_ref, val_ref, gate_ref, dz_ref):
        sg = jax.nn.sigmoid(gate_ref[...].astype(F32))
        d = dh_ref[...]
        dz_ref[:, 0:D_MODEL] = (d * sg).astype(BF16)
        dz_ref[:, D_MODEL:2 * D_MODEL] = (d * val_ref[...].astype(F32) * sg * (1.0 - sg)).astype(BF16)

    row = lambda i: (i, 0)
    return pl.pallas_call(
        body, name="glu_bwd", grid=(n_rows // TM,),
        in_specs=[pl.BlockSpec((TM, D_MODEL), row), pl.BlockSpec((TM, D_MODEL), row),
                  pl.BlockSpec((TM, D_MODEL), lambda i: (i, 1))],
        out_specs=pl.BlockSpec((TM, 2 * D_MODEL), row),
        out_shape=SDS((n_rows, 2 * D_MODEL), BF16), compiler_params=_params(1),
    )(dh, z, z)


def _loss_head(h, wn, target, n_ex, nb):
    n_rows = h.shape[0]
    per_tile = TM // BLOCK
    n_tiles = nb // per_tile

    def body(h_ref, wn_ref, *rest):
        t_refs, (dh_ref, loss_ref, dw_ref) = rest[:per_tile], rest[per_tile:]
        b, j = pl.program_id(0), pl.program_id(1)

        @pl.when((b == 0) & (j == 0))
        def _():
            loss_ref[...] = jnp.zeros_like(loss_ref)
            dw_ref[...] = jnp.zeros_like(dw_ref)

        def block(k):
            rows = slice(k * BLOCK, (k + 1) * BLOCK)
            hh = h_ref[rows, :]
            diff = _rms(hh, wn_ref[...]) - t_refs[k][...]
            loss_ref[...] += 0.5 * jnp.sum(diff * diff) * (1.0 / D_MODEL)
            dh, dw_rows = _rms_bwd(diff * (1.0 / D_MODEL), hh, wn_ref[...])
            dh_ref[rows, :] = dh
            dw_ref[...] += _fold8(dw_rows)

        @pl.when(j == 0)
        def _():
            dh_ref[0:BLOCK, :] = jnp.zeros((BLOCK, D_MODEL), F32)

        pl.when(j > 0)(lambda: block(0))
        for k in range(1, per_tile):
            block(k)

    def t_spec(k):
        return pl.BlockSpec((BLOCK, D_MODEL), lambda b, j: (b * (nb - 1) + jnp.maximum(per_tile * j + k - 1, 0), 0))

    tile = pl.BlockSpec((TM, D_MODEL), lambda b, j: (b * n_tiles + j, 0))
    return pl.pallas_call(
        body, name="loss_head", grid=(n_ex, n_tiles),
        in_specs=[tile, pl.BlockSpec((1, D_MODEL), lambda b, j: (0, 0))] + [t_spec(k) for k in range(per_tile)],
        out_specs=[tile, pl.BlockSpec((8, 128), lambda b, j: (0, 0)), pl.BlockSpec((8, D_MODEL), lambda b, j: (0, 0))],
        out_shape=[SDS((n_rows, D_MODEL), F32), SDS((8, 128), F32), SDS((8, D_MODEL), F32)],
        compiler_params=_params(2),
    )(h, wn, *([target] * per_tile))


def _adamw(pieces, w, m, v, name):
    n_layers = len(pieces)
    rows, cols = pieces[0].shape[1:]
    rb = rows
    for cand in (256, 136, 128, 64, 32, 16, 8):
        if rows % cand == 0 and rows > cand:
            rb = cand
            break
    n_blk = rows // rb
    c1 = 1.0 / (1.0 - ADAM_B1 ** ADAM_STEP)
    c2 = 1.0 / (1.0 - ADAM_B2 ** ADAM_STEP)

    def body(*refs):
        p_refs = refs[:n_layers]
        w_ref, m_ref, v_ref, g_out, d_out, m_out, v_out = refs[n_layers:]
        layer = pl.program_id(0)
        g = None
        for l, p_ref in enumerate(p_refs):
            gl = p_ref[0].astype(F32)
            for k in range(1, N_DEV):
                gl = gl + p_ref[k].astype(F32)
            g = gl if g is None else jnp.where(layer == l, gl, g)
        m_new = ADAM_B1 * m_ref[...] + (1.0 - ADAM_B1) * g
        v_new = ADAM_B2 * v_ref[...] + (1.0 - ADAM_B2) * (g * g)
        g_out[...] = g
        m_out[...] = m_new
        v_out[...] = v_new
        d_out[...] = -ADAM_LR * ((m_new * c1) / (jnp.sqrt(v_new * c2) + ADAM_EPS) + ADAM_WD * w_ref[...])

    def piece_spec(l):
        return pl.BlockSpec((N_DEV, rb, cols), lambda ly, i: (0, jnp.where(ly == l, i, 0), 0))

    blk = pl.BlockSpec((rb, cols), lambda ly, i: (ly * n_blk + i, 0))
    return pl.pallas_call(
        body, name=name, grid=(n_layers, n_blk),
        in_specs=[piece_spec(l) for l in range(n_layers)] + [blk, blk, blk],
        out_specs=[blk, blk, blk, blk],
        out_shape=[SDS((n_layers * rows, cols), F32)] * 4, compiler_params=_params(2),
    )(*pieces, w, m, v)


_HBM = pl.BlockSpec(memory_space=pltpu.HBM)
_SEM = pl.BlockSpec(memory_space=pltpu.SEMAPHORE)
_EFFECT = pltpu.SideEffectType.DATAFLOW_SIDE_EFFECTING
N_GATHER_PEERS = N_CHIPS - 1
N_EXCHANGE_PEERS = N_DEV - 1


def _gather_copies(srcs, lands, send_sems, recv_sems):
    x, y, c = lax.axis_index("x"), lax.axis_index("y"), lax.axis_index("c")
    mine = 2 * x + y
    chips = [(1 - x, y), (x, 1 - y), (1 - x, 1 - y)]
    out, inc = [], []
    for a in range(len(srcs)):
        for k, (px, py) in enumerate(chips):
            j = a * N_GATHER_PEERS + k
            sems = dict(send_sem=send_sems.at[j], recv_sem=recv_sems.at[j], device_id=(px, py, c),
                        device_id_type=pl.DeviceIdType.MESH)
            out.append(pltpu.make_async_remote_copy(src_ref=srcs[a], dst_ref=lands[a].at[mine], **sems))
            inc.append(pltpu.make_async_remote_copy(src_ref=srcs[a], dst_ref=lands[a].at[2 * px + py], **sems))
    return out, inc


def _exchange_copies(n_scatter):
    def copies(srcs, lands, send_sems, recv_sems):
        x, y, c = lax.axis_index("x"), lax.axis_index("y"), lax.axis_index("c")
        me = 4 * x + 2 * y + c
        peers = [(x ^ (k >> 2), y ^ ((k >> 1) & 1), c ^ (k & 1)) for k in range(1, N_DEV)]
        out, inc = [], []
        for a in range(len(srcs)):
            for k, (px, py, pc) in enumerate(peers):
                j = a * N_EXCHANGE_PEERS + k
                sems = dict(send_sem=send_sems.at[j], recv_sem=recv_sems.at[j], device_id=(px, py, pc),
                            device_id_type=pl.DeviceIdType.MESH)
                theirs = srcs[a].at[2 * px + py] if a < n_scatter else srcs[a]
                mine = srcs[a].at[2 * x + y] if a < n_scatter else srcs[a]
                out.append(pltpu.make_async_remote_copy(src_ref=theirs, dst_ref=lands[a].at[me], **sems))
                inc.append(pltpu.make_async_remote_copy(src_ref=mine, dst_ref=lands[a].at[4 * px + 2 * py + pc], **sems))
        return out, inc

    return copies


def _split_start(groups, copies_fn, n_peers, name):
    sizes = [len(srcs) for srcs, _ in groups]
    flat = [a for srcs, lands in groups for a in list(srcs) + list(lands)]
    n_flat, n_grp = len(flat), len(groups)

    def body(*refs):
        sems = refs[2 * n_flat:2 * n_flat + 2 * n_grp]
        token = refs[-1]
        at = 0
        for gi, n in enumerate(sizes):
            out, _ = copies_fn(refs[at:at + n], refs[at + n:at + 2 * n], sems[2 * gi], sems[2 * gi + 1])
            for cp in out:
                cp.start()
            at += 2 * n
        token[...] = jnp.zeros_like(token)

    sem_shapes = []
    for n in sizes:
        sem_shapes += [pltpu.SemaphoreType.DMA((n * n_peers,)), pltpu.SemaphoreType.DMA((n * n_peers,))]
    res = pl.pallas_call(
        body, name=name,
        out_shape=(*[pltpu.HBM(a.shape, a.dtype) for a in flat], *sem_shapes, SDS((8, 128), F32)),
        in_specs=[_HBM] * n_flat,
        out_specs=(*[_HBM] * n_flat, *[_SEM] * (2 * n_grp), pl.BlockSpec(memory_space=pltpu.VMEM)),
        input_output_aliases={i: i for i in range(n_flat)},
        compiler_params=pltpu.CompilerParams(has_side_effects=_EFFECT),
    )(*[pltpu.with_memory_space_constraint(a, pltpu.HBM) for a in flat])
    handles, at = [], 0
    for gi, n in enumerate(sizes):
        handles.append((res[n_flat + 2 * gi], res[n_flat + 2 * gi + 1], list(res[at:at + n]), list(res[at + n:at + 2 * n])))
        at += 2 * n
    return handles, res[-1]


def _split_wait(handle, after, copies_fn, name):
    send_sems, recv_sems, srcs, lands = handle
    n = len(srcs)
    after = list(after) if isinstance(after, (list, tuple)) else [after]

    def body(*refs):
        out, inc = copies_fn(refs[:n], refs[n:2 * n], refs[2 * n], refs[2 * n + 1])
        for cp in out:
            cp.wait_send()
        for cp in inc:
            cp.wait_recv()

    flat = list(srcs) + list(lands)
    res = pl.pallas_call(
        body, name=name,
        out_shape=tuple(pltpu.HBM(a.shape, a.dtype) for a in flat),
        in_specs=[_HBM] * (2 * n) + [_SEM, _SEM] + [pl.BlockSpec(memory_space=pl.ANY)] * len(after),
        out_specs=tuple([_HBM] * (2 * n)),
        input_output_aliases={i: i for i in range(2 * n)},
        compiler_params=pltpu.CompilerParams(has_side_effects=_EFFECT),
    )(*flat, send_sems, recv_sems, *after)
    return list(res[n:])


def _landing(own, slot, n_slots):
    return lax.dynamic_update_index_in_dim(lax.empty((n_slots,) + own.shape, own.dtype), own, slot, 0)


def _ssm_discretize(lam_re, lam_im, log_dt, b_re, b_im):
    lr = jnp.minimum(lam_re, LAMBDA_RE_MAX)
    li = lam_im
    dt = jnp.exp(log_dt)[:, None]
    mag = jnp.exp(lr * dt)
    ar, ai = mag * jnp.cos(li * dt), mag * jnp.sin(li * dt)
    den = lr * lr + li * li
    nr, ni = ar - 1.0, ai
    gr, gi = (nr * lr + ni * li) / den, (ni * lr - nr * li) / den
    bbr = gr[:, :, None] * b_re - gi[:, :, None] * b_im
    bbi = gr[:, :, None] * b_im + gi[:, :, None] * b_re
    return ar, ai, bbr, bbi


def _pair_lanes(t):
    return t.reshape(N_PAIR, 2 * SSM_STATE)


def _chan_state_blocks(t_gcp):
    t = t_gcp.reshape(N_PAIR, 2, SSM_GROUP, SSM_STATE)
    eye2 = jnp.eye(2, dtype=t.dtype)
    blk = jnp.einsum("rgcp,gh->rgchp", t, eye2).reshape(N_PAIR, 2 * SSM_GROUP, 2 * SSM_STATE)
    place = jax.nn.one_hot(jnp.arange(N_PAIR) % PAIRS_PER_CHUNK, PAIRS_PER_CHUNK, dtype=t.dtype)
    return jnp.einsum("rcl,rj->rjcl", blk, place).reshape(N_PAIR, 128, 2 * SSM_STATE)


def _chan_state_unblock(t):
    t = t.reshape(N_PAIR, PAIRS_PER_CHUNK, 2, SSM_GROUP, 2, SSM_STATE)
    place = jax.nn.one_hot(jnp.arange(N_PAIR) % PAIRS_PER_CHUNK, PAIRS_PER_CHUNK, dtype=t.dtype)
    t = jnp.einsum("rjgchp,rj->rgchp", t, place)
    t = jnp.einsum("rgchp,gh->rgcp", t, jnp.eye(2, dtype=t.dtype))
    return t.reshape(SSM_NG, SSM_GROUP, SSM_STATE)


def _scan_tables(zr, zi, reverse):
    zr, zi = _pair_lanes(zr), _pair_lanes(-zi if reverse else zi)
    k = jnp.arange(1, SEG + 1, dtype=F32)
    k = (k[::-1] if reverse else k)[None, :, None]
    mag = jnp.exp(k * zr[:, None, :])
    pw = jnp.stack([mag * jnp.cos(k * zi[:, None, :]), mag * jnp.sin(k * zi[:, None, :])], axis=1)
    first, last = (SEG - 1, 0) if reverse else (0, SEG - 1)
    big = [(pw[:, 0, last], pw[:, 1, last])]
    for _ in range(2):
        br, bi = big[-1]
        big.append((br * br - bi * bi, 2.0 * br * bi))
    rows = jnp.arange(8)[None, :, None]
    tiles = [jnp.broadcast_to(pw[:, comp, first][:, None, :], (N_PAIR, 8, 128)) for comp in (0, 1)]
    for lvl, step in enumerate((1, 2, 4)):
        keep = (rows <= 7 - step) if reverse else (rows >= step)
        for part in big[lvl]:
            tiles.append(jnp.where(keep, part[:, None, :], 0.0))
    return jnp.stack(tiles, axis=1), jnp.broadcast_to(pw[:, :, :, None, :], (N_PAIR, 2, SEG, 8, 128))


def _pairs_to_chunks(t):
    n_chunk = N_PAIR // PAIRS_PER_CHUNK
    return jnp.swapaxes(t.reshape(n_chunk, PAIRS_PER_CHUNK, 128, 256), 1, 2).reshape(n_chunk, 128, XW)


def _chunks_to_pairs(t):
    n_chunk = N_PAIR // PAIRS_PER_CHUNK
    return jnp.swapaxes(t.reshape(n_chunk, 128, PAIRS_PER_CHUNK, 256), 1, 2).reshape(N_PAIR, 128, 256)


def _ssm_operands(w):
    ar, ai, bbr, bbi = _ssm_discretize(w["ssm_lambda_re"], w["ssm_lambda_im"], w["ssm_log_dt"], w["ssm_b_re"], w["ssm_b_im"])
    b_blk = jnp.concatenate([_chan_state_blocks(jnp.swapaxes(bbr, 1, 2)), _chan_state_blocks(jnp.swapaxes(bbi, 1, 2))], axis=2)
    c_blk = jnp.concatenate([_chan_state_blocks(w["ssm_c_re"]), -_chan_state_blocks(w["ssm_c_im"])], axis=2)
    dt = jnp.exp(w["ssm_log_dt"])[:, None]
    zr, zi = jnp.minimum(w["ssm_lambda_re"], LAMBDA_RE_MAX) * dt, w["ssm_lambda_im"] * dt
    b_cat, c_cat = _pairs_to_chunks(b_blk).astype(BF16), _pairs_to_chunks(c_blk).astype(BF16)
    return (b_cat, jnp.swapaxes(b_cat, 1, 2), c_cat, jnp.swapaxes(c_cat, 1, 2),
            *_scan_tables(zr, zi, False), *_scan_tables(zr, zi, True))


def _local_step(x, target, w, late_weights, on_grads):
    n_ex, seq, _ = x.shape
    lp = seq + BLOCK
    nb = lp // BLOCK
    n_rows = n_ex * lp
    g = {}

    head = jnp.concatenate([jnp.zeros((PAD, D_MODEL), F32), w["meta_tokens"]], axis=0)
    h0 = jnp.concatenate([jnp.broadcast_to(head[None], (n_ex, BLOCK, D_MODEL)), x], axis=1).reshape(n_rows, D_MODEL)

    qkv, hn_a = _rms_mm_cols(h0, w["attn_norm_w"], w["attn_w_qkv"], "qkv_fwd")
    att, lse = _attn_fwd(qkv, w["attn_sinks"], n_ex, nb)
    h1 = _mm_acc(att, w["attn_w_o"], False, "attn_out_fwd", res=h0)
    w = {**w, **late_weights(0, att)}
    h2, a0, hn_m0, u = _mlp_fwd(h1, w["mlp_norm_w"][0:1], w["mlp_w_up"][0], w["mlp_w_down"][0], 0, "mlp0_fwd",
                                next_norm=w["ssm_norm_w"])
    late = late_weights(1, h2)
    w["ssm_w_glu"] = late["ssm_w_glu"]
    w["mlp_w_up"], w["mlp_w_down"] = w["mlp_w_up"] + late["mlp_w_up"], w["mlp_w_down"] + late["mlp_w_down"]

    ops = w["ssm_operands"] if "ssm_operands" in w else _ssm_operands(w)
    b_pad, bt_pad, ct_pad, c_pad, tab_fwd, pw_fwd, tab_rev, pw_rev = ops
    yg, y, xs = _ssm_fwd(u, b_pad, c_pad, tab_fwd, pw_fwd, w["ssm_d"], n_ex, lp)
    z = _mm_cols(yg, w["ssm_w_glu"], False, "glu_mm_fwd")
    h3 = _glu_fwd(h2, z)
    h4, a1, hn_m1 = _mlp_fwd(h3, w["mlp_norm_w"][1:2], w["mlp_w_up"][1], w["mlp_w_down"][1], 0, "mlp1_fwd")

    dh4, loss_tile, dnorm_f = _loss_head(h4, w["final_norm_w"], target.reshape(n_ex * seq, D_MODEL), n_ex, nb)

    def mlp_bwd(dh_out, h_in, a, hn, layer, tag, norm_w):
        dhn, dw_up, dw_down = None, None, None
        for s in range(N_CHIPS):
            final = s == N_CHIPS - 1
            res = _mlp_bwd_shard(s, dh_out, a, hn, dhn, h_in if final else None, norm_w,
                                 w["mlp_w_up"][layer], w["mlp_w_down"][layer], dw_up, dw_down, f"{tag}_bwd{s}")
            dhn, dw_up, dw_down = res[:3]
        return dhn, res[3], dw_up, dw_down

    dh3, dnorm_m1, dwu1, dwd1 = mlp_bwd(dh4, h3, a1, hn_m1, 1, "mlp1", w["mlp_norm_w"][1:2])
    tok = on_grads("mlp1", {"mlp_w_up": dwu1, "mlp_w_down": dwd1})
    dz = _glu_bwd(dh3, z)
    dyg = _mm_acc(dz, w["ssm_w_glu"], True, "glu_mm_dx", out_dtype=BF16)
    g["ssm_w_glu"] = _mm_tn(yg, dz, N_CHIPS, False, "glu_mm_dw")
    du, db_blk, dc_blk, da_t, dd_t = _ssm_bwd(dyg, y, u, xs, ct_pad, bt_pad, tab_rev, pw_rev, w["ssm_d"] + tok, n_ex, lp)
    dh2, dnorm_s = _rms_bwd_call(du, h2, w["ssm_norm_w"], dh3, "ssm_norm_bwd")
    db_blk, dc_blk = _chunks_to_pairs(db_blk), _chunks_to_pairs(dc_blk)
    g["ssm_c_re"] = _chan_state_unblock(dc_blk[:, :, 0:128])
    g["ssm_c_im"] = -_chan_state_unblock(dc_blk[:, :, 128:256])
    g_bbr = jnp.swapaxes(_chan_state_unblock(db_blk[:, :, 0:128]), 1, 2)
    g_bbi = jnp.swapaxes(_chan_state_unblock(db_blk[:, :, 128:256]), 1, 2)
    g_a = jnp.sum(da_t, axis=2).reshape(N_PAIR, 2, 2, SSM_STATE)
    g_ar, g_ai = g_a[:, 0].reshape(SSM_NG, SSM_STATE), g_a[:, 1].reshape(SSM_NG, SSM_STATE)
    _, vjp = jax.vjp(_ssm_discretize, w["ssm_lambda_re"], w["ssm_lambda_im"], w["ssm_log_dt"], w["ssm_b_re"], w["ssm_b_im"])
    g["ssm_lambda_re"], g["ssm_lambda_im"], g["ssm_log_dt"], g["ssm_b_re"], g["ssm_b_im"] = vjp((g_ar, g_ai, g_bbr, g_bbi))
    tok = on_grads("ssm", g)
    g = {}
    dh1, dnorm_m0, dwu0, dwd0 = mlp_bwd(dh2, h1, a0, hn_m0, 0, "mlp0", w["mlp_norm_w"][0:1] + tok)
    datt = _mm_cols(dh1, w["attn_w_o"], True, "attn_out_dx")
    dw_o = _mm_tn(att, dh1, N_CHIPS, True, "attn_out_dw")
    tok = on_grads("mlp0", {"mlp_w_up": dwu0, "mlp_w_down": dwd0, "attn_w_o": dw_o})
    dqkv, dsink_rows = _attn_bwd(qkv, w["attn_sinks"] + tok, att, lse, datt, n_ex, nb)
    tok = on_grads("qkv", {"attn_w_qkv": _mm_tn(hn_a, dqkv, N_CHIPS, False, "qkv_dw")})
    dh0, dnorm_a = _mm_acc(dqkv, w["attn_w_qkv"], True, "qkv_dx", rms_bwd=(h0, w["attn_norm_w"] + tok, dh1))

    dh0 = dh0.reshape(n_ex, lp, D_MODEL)
    on_grads("rest", {
        "mlp_norm_w": jnp.stack([jnp.sum(dnorm_m0, axis=0), jnp.sum(dnorm_m1, axis=0)]),
        "final_norm_w": jnp.sum(dnorm_f, axis=0),
        "attn_norm_w": jnp.sum(dnorm_a, axis=0)[None],
        "ssm_norm_w": jnp.sum(dnorm_s, axis=0)[None],
        "attn_sinks": jnp.sum(dsink_rows, axis=0)[None],
        "ssm_d": jnp.sum(dd_t, axis=0)[None],
        "meta_tokens": jnp.sum(dh0[:, PAD:BLOCK], axis=0),
        "loss": loss_tile[0, 0:1]})
    return loss_tile, dh0[:, BLOCK:]


_SHARDED_SMALL = ("meta_tokens", "ssm_norm_w", "ssm_d")
_REP_SSM = ("ssm_lambda_re", "ssm_lambda_im", "ssm_log_dt", "ssm_b_re", "ssm_b_im", "ssm_c_re", "ssm_c_im")
_REP_MISC = ("attn_norm_w", "attn_sinks", "mlp_norm_w", "final_norm_w")
_BIG = ("attn_w_qkv", "attn_w_o", "ssm_w_glu", "mlp_w_up", "mlp_w_down")


def _pack(parts, cols):
    flat = jnp.concatenate([p.reshape(-1) for p in parts])
    rows = -(-flat.shape[0] // (8 * cols)) * 8
    return jnp.pad(flat, (0, rows * cols - flat.shape[0])).reshape(rows, cols)


def _unpack(packed, like):
    flat = packed.reshape(-1)
    out, at = [], 0
    for p in like:
        out.append(flat[at:at + p.size].reshape(p.shape))
        at += p.size
    return out


def kernel(x, meta_tokens, attn_norm_w, attn_w_qkv, attn_sinks, attn_w_o, ssm_norm_w, ssm_lambda_re, ssm_lambda_im, ssm_log_dt, ssm_b_re, ssm_b_im, ssm_c_re, ssm_c_im, ssm_d, ssm_w_glu, mlp_norm_w, mlp_w_up, mlp_w_down, final_norm_w, loss_target, m_meta_tokens, m_attn_norm_w, m_attn_w_qkv, m_attn_sinks, m_attn_w_o, m_ssm_norm_w, m_ssm_lambda_re, m_ssm_lambda_im, m_ssm_log_dt, m_ssm_b_re, m_ssm_b_im, m_ssm_c_re, m_ssm_c_im, m_ssm_d, m_ssm_w_glu, m_mlp_norm_w, m_mlp_w_up, m_mlp_w_down, m_final_norm_w, v_meta_tokens, v_attn_norm_w, v_attn_w_qkv, v_attn_sinks, v_attn_w_o, v_ssm_norm_w, v_ssm_lambda_re, v_ssm_lambda_im, v_ssm_log_dt, v_ssm_b_re, v_ssm_b_im, v_ssm_c_re, v_ssm_c_im, v_ssm_d, v_ssm_w_glu, v_mlp_norm_w, v_mlp_w_up, v_mlp_w_down, v_final_norm_w):
    names = ("meta_tokens", "attn_norm_w", "attn_w_qkv", "attn_sinks", "attn_w_o", "ssm_norm_w", "ssm_lambda_re",
             "ssm_lambda_im", "ssm_log_dt", "ssm_b_re", "ssm_b_im", "ssm_c_re", "ssm_c_im", "ssm_d", "ssm_w_glu",
             "mlp_norm_w", "mlp_w_up", "mlp_w_down", "final_norm_w")
    wts = dict(zip(names, (meta_tokens, attn_norm_w, attn_w_qkv, attn_sinks, attn_w_o, ssm_norm_w, ssm_lambda_re,
                           ssm_lambda_im, ssm_log_dt, ssm_b_re, ssm_b_im, ssm_c_re, ssm_c_im, ssm_d, ssm_w_glu,
                           mlp_norm_w, mlp_w_up, mlp_w_down, final_norm_w)))
    mom = dict(zip(names, (m_meta_tokens, m_attn_norm_w, m_attn_w_qkv, m_attn_sinks, m_attn_w_o, m_ssm_norm_w,
                           m_ssm_lambda_re, m_ssm_lambda_im, m_ssm_log_dt, m_ssm_b_re, m_ssm_b_im, m_ssm_c_re,
                           m_ssm_c_im, m_ssm_d, m_ssm_w_glu, m_mlp_norm_w, m_mlp_w_up, m_mlp_w_down, m_final_norm_w)))
    var = dict(zip(names, (v_meta_tokens, v_attn_norm_w, v_attn_w_qkv, v_attn_sinks, v_attn_w_o, v_ssm_norm_w,
                           v_ssm_lambda_re, v_ssm_lambda_im, v_ssm_log_dt, v_ssm_b_re, v_ssm_b_im, v_ssm_c_re,
                           v_ssm_c_im, v_ssm_d, v_ssm_w_glu, v_mlp_norm_w, v_mlp_w_up, v_mlp_w_down, v_final_norm_w)))

    my_chip = 2 * lax.axis_index("x") + lax.axis_index("y")
    my_dev = 2 * my_chip + lax.axis_index("c")
    small_mine = _pack([wts[n] for n in _SHARDED_SMALL], 128)
    first = [attn_w_qkv.astype(BF16), attn_w_o.astype(BF16), small_mine]
    up16, down16 = mlp_w_up.astype(BF16), mlp_w_down.astype(BF16)
    mlp0 = [up16[0:1], down16[0:1]]
    rest = [ssm_w_glu.astype(BF16), up16[1:2], down16[1:2]]
    handles, _ = _split_start([(srcs, [_landing(a, my_chip, N_CHIPS) for a in srcs]) for srcs in (first, mlp0, rest)],
                              _gather_copies, N_GATHER_PEERS, "gather_start")
    full = {n: wts[n] for n in _REP_MISC}
    full["final_norm_w"] = final_norm_w[None]
    for n in _REP_SSM:
        full[n] = wts[n][0]
    full["ssm_operands"] = _ssm_operands(full)
    got = _split_wait(handles[0], full["ssm_operands"], _gather_copies, "gather_wait_first")
    full["attn_w_qkv"], full["attn_w_o"] = got[0], got[1]
    smalls = [_unpack(got[2][s], [wts[n] for n in _SHARDED_SMALL]) for s in range(N_CHIPS)]
    for k, n in enumerate(_SHARDED_SMALL):
        full[n] = jnp.concatenate([smalls[s][k] for s in range(N_CHIPS)], axis=1)

    def late_weights(stage, after):
        if stage == 0:
            up, down = _split_wait(handles[1], after, _gather_copies, "gather_wait_mlp0")
            return {"mlp_w_up": [up], "mlp_w_down": [down]}
        glu, up, down = _split_wait(handles[2], after, _gather_copies, "gather_wait_rest")
        return {"ssm_w_glu": glu, "mlp_w_up": [up], "mlp_w_down": [down]}

    def shard_cols(t):
        return jnp.swapaxes(t.reshape(t.shape[0], N_CHIPS, t.shape[1] // N_CHIPS), 0, 1)

    pending = {}

    def on_grads(tag, g):
        scatter = [g[n] for n in _BIG if n in g]
        whole = []
        if tag == "ssm":
            whole = [_pack([g[n] for n in _REP_SSM], D_MODEL)]
        if tag == "rest":
            parts = [shard_cols(g[n]) for n in _SHARDED_SMALL]
            scatter = [jnp.stack([_pack([p[s] for p in parts], 128) for s in range(N_CHIPS)])]
            whole = [_pack([g[n] for n in _REP_MISC] + [g["loss"]], D_MODEL)]
        srcs = scatter + whole
        lands = [_landing(lax.dynamic_index_in_dim(a, my_chip, 0, keepdims=False), my_dev, N_DEV) for a in scatter]
        lands += [_landing(a, my_dev, N_DEV) for a in whole]
        hs, token = _split_start([(srcs, lands)], _exchange_copies(len(scatter)), N_EXCHANGE_PEERS, "exchange_start_" + tag)
        pending[tag] = (hs[0], len(scatter))
        return token[0, 0]

    _, grad_x = _local_step(x, loss_target, full, late_weights, on_grads)

    recv = {}
    for tag, (handle, n_scatter) in pending.items():
        recv[tag] = _split_wait(handle, grad_x, _exchange_copies(n_scatter), "exchange_wait_" + tag)
    loss = jnp.sum(recv["rest"][1].reshape(N_DEV, -1)[:, sum(wts[n].size for n in _REP_MISC)])

    out = {}

    def update(tag, pieces, w2, m2, v2):
        return _adamw(pieces, w2, m2, v2, "adamw_" + tag)

    def update_weight(n, pieces):
        shp = wts[n].shape
        r2 = (math.prod(shp[:-1]), shp[-1])
        res = update(n, pieces, wts[n].reshape(r2), mom[n].reshape(r2), var[n].reshape(r2))
        out[n] = [t.reshape(shp) for t in res]

    update_weight("mlp_w_up", [recv["mlp0"][1], recv["mlp1"][0]])
    update_weight("mlp_w_down", [recv["mlp0"][2], recv["mlp1"][1]])
    update_weight("attn_w_o", [recv["mlp0"][0]])
    update_weight("ssm_w_glu", [recv["ssm"][0]])
    update_weight("attn_w_qkv", [recv["qkv"][0]])
    for tag, group, pieces, cols in (("small", _SHARDED_SMALL, recv["rest"][0], 128),
                                     ("rep_ssm", _REP_SSM, recv["ssm"][1], D_MODEL),
                                     ("rep_misc", _REP_MISC, recv["rest"][1], D_MODEL)):
        like = [wts[n] for n in group]
        res = update(tag, [pieces], _pack(like, cols), _pack([mom[n] for n in group], cols),
                     _pack([var[n] for n in group], cols))
        for k, n in enumerate(group):
            out[n] = [_unpack(t, like)[k] for t in res]

    return (loss, grad_x, *[out[n][0] for n in names], *[out[n][1] for n in names],
            *[out[n][2] for n in names], *[out[n][3] for n in names])
```

```python
import functools
import math

import jax
import jax.numpy as jnp
from jax import lax
from jax.experimental import pallas as pl
from jax.experimental.pallas import tpu as pltpu

F32 = jnp.float32
BF16 = jnp.bfloat16
SDS = jax.ShapeDtypeStruct

D_MODEL = 1024
N_HEADS = 16
N_KV = 4
GQA = N_HEADS // N_KV
HEAD_DIM = 64
BLOCK = 128
N_META = 16
PAD = BLOCK - N_META
QKV_DIM = (N_HEADS + 2 * N_KV) * HEAD_DIM
KV_DIM = 2 * N_KV * HEAD_DIM
D_FF = 4 * D_MODEL
N_CHIPS = 4
N_DEV = 8
SSM_GROUP = 16
SSM_NG = D_MODEL // SSM_GROUP
SSM_STATE = 64
N_PAIR = SSM_NG // 2
PAIRS_PER_CHUNK = 4
RMS_EPS = 1e-6
NEG_INF = -1e30
LAMBDA_RE_MAX = -1e-4
ADAM_LR, ADAM_B1, ADAM_B2, ADAM_EPS, ADAM_WD, ADAM_STEP = 0.001, 0.9, 0.999, 1e-08, 0.01, 10

TM = 384
MM_TILES = (768, 384)
MLP_FWD_TILES = (384,)
MLP_BWD_TILES = (768, 384)
TN_TILES = (1408, 768, 384)
VMEM_LIMIT = 56 * 1024 * 1024


def _params(n_grid):
    return pltpu.CompilerParams(dimension_semantics=("arbitrary",) * n_grid, vmem_limit_bytes=VMEM_LIMIT)


def _row_tile(n_rows, tiles):
    return next(t for t in tiles if n_rows % t == 0)


def _rms(h, w):
    r = lax.rsqrt(jnp.mean(h * h, axis=-1, keepdims=True) + RMS_EPS)
    return h * r * w


def _rms_bwd(dhn, h, w):
    r = lax.rsqrt(jnp.mean(h * h, axis=-1, keepdims=True) + RMS_EPS)
    g = dhn * w
    proj = jnp.sum(g * h, axis=-1, keepdims=True) * (1.0 / D_MODEL)
    return r * g - h * (r * r * r) * proj, dhn * h * r


def _fold8(t):
    return jnp.sum(t.reshape(t.shape[0] // 8, 8, t.shape[1]), axis=0)


def _gelu(y):
    return 0.5 * y * (1.0 + jnp.tanh(0.7978845608028654 * (y + 0.044715 * y * y * y)))


def _gelu_grad(y):
    t = jnp.tanh(0.7978845608028654 * (y + 0.044715 * y * y * y))
    return 0.5 * (1.0 + t) + 0.5 * y * (1.0 - t * t) * 0.7978845608028654 * (1.0 + 3.0 * 0.044715 * y * y)


def _w4_spec(w4):
    n_sh, _, k, n = w4.shape
    return pl.BlockSpec((n_sh, None, k, n), lambda i: (0, 0, 0, 0))


def _rms_mm_cols(h, wn, w4, name):
    n_rows = h.shape[0]
    n_sh, _, k, n = w4.shape
    tm = _row_tile(n_rows, MM_TILES)

    def body(h_ref, wn_ref, w_ref, o_ref, hn_ref):
        hn = _rms(h_ref[...], wn_ref[...]).astype(BF16)
        hn_ref[...] = hn
        for s in range(n_sh):
            o_ref[:, s * n:(s + 1) * n] = jnp.dot(hn, w_ref[s], preferred_element_type=F32).astype(o_ref.dtype)

    return pl.pallas_call(
        body, name=name, grid=(n_rows // tm,),
        in_specs=[pl.BlockSpec((tm, k), lambda i: (i, 0)), pl.BlockSpec((1, k), lambda i: (0, 0)), _w4_spec(w4)],
        out_specs=[pl.BlockSpec((tm, n_sh * n), lambda i: (i, 0)), pl.BlockSpec((tm, k), lambda i: (i, 0))],
        out_shape=[SDS((n_rows, n_sh * n), BF16), SDS((n_rows, k), BF16)],
        compiler_params=_params(1),
    )(h, wn, w4)


def _mm_cols(x, w4, trans_w, name):
    n_rows, kx = x.shape
    tm = _row_tile(n_rows, MM_TILES)
    n_sh, _, k, n = w4.shape
    n_out = k if trans_w else n
    dims = (((1,), (1,)), ((), ())) if trans_w else (((1,), (0,)), ((), ()))

    def body(x_ref, w_ref, o_ref):
        x16 = x_ref[...].astype(BF16)
        for s in range(n_sh):
            o_ref[:, s * n_out:(s + 1) * n_out] = lax.dot_general(
                x16, w_ref[s], dims, preferred_element_type=F32).astype(o_ref.dtype)

    return pl.pallas_call(
        body, name=name, grid=(n_rows // tm,),
        in_specs=[pl.BlockSpec((tm, kx), lambda i: (i, 0)), _w4_spec(w4)],
        out_specs=pl.BlockSpec((tm, n_sh * n_out), lambda i: (i, 0)),
        out_shape=SDS((n_rows, n_sh * n_out), BF16),
        compiler_params=_params(1),
    )(x, w4)


def _mm_acc(x, w4, trans_w, name, res=None, rms_bwd=None, out_dtype=F32):
    n_rows = x.shape[0]
    tm = _row_tile(n_rows, MM_TILES)
    n_sh, _, k, n = w4.shape
    kx, n_out = (n, k) if trans_w else (k, n)
    dims = (((1,), (1,)), ((), ())) if trans_w else (((1,), (0,)), ((), ()))

    def body(*refs):
        if rms_bwd is not None:
            x_ref, w_ref, h_ref, wn_ref, dres_ref, o_ref, dw_ref = refs
        elif res is not None:
            x_ref, w_ref, res_ref, o_ref = refs
        else:
            x_ref, w_ref, o_ref = refs
        acc = None
        for s in range(n_sh):
            part = lax.dot_general(x_ref[:, s * kx:(s + 1) * kx].astype(BF16), w_ref[s], dims, preferred_element_type=F32)
            acc = part if acc is None else acc + part
        if rms_bwd is not None:
            dh, dw_rows = _rms_bwd(acc, h_ref[...], wn_ref[...])
            o_ref[...] = (dres_ref[...] + dh).astype(o_ref.dtype)

            @pl.when(pl.program_id(0) == 0)
            def _():
                dw_ref[...] = jnp.zeros_like(dw_ref)

            dw_ref[...] += _fold8(dw_rows)
        elif res is not None:
            o_ref[...] = (res_ref[...] + acc).astype(o_ref.dtype)
        else:
            o_ref[...] = acc.astype(o_ref.dtype)

    row = lambda i: (i, 0)
    in_specs = [pl.BlockSpec((tm, n_sh * kx), row), _w4_spec(w4)]
    args = [x, w4]
    out_specs = pl.BlockSpec((tm, n_out), row)
    out_shape = SDS((n_rows, n_out), out_dtype)
    if rms_bwd is not None:
        h, wn, dres = rms_bwd
        in_specs += [pl.BlockSpec((tm, n_out), row), pl.BlockSpec((1, n_out), lambda i: (0, 0)),
                     pl.BlockSpec((tm, n_out), row)]
        args += [h, wn, dres]
        out_specs = [out_specs, pl.BlockSpec((8, n_out), lambda i: (0, 0))]
        out_shape = [out_shape, SDS((8, n_out), F32)]
    elif res is not None:
        in_specs.append(pl.BlockSpec((tm, n_out), row))
        args.append(res)
    return pl.pallas_call(
        body, name=name, grid=(n_rows // tm,), in_specs=in_specs, out_specs=out_specs, out_shape=out_shape,
        compiler_params=_params(1),
    )(*args)


def _mm_tn(a, b, n_sh, a_sharded, name):
    n_rows = a.shape[0]
    tm = _row_tile(n_rows, TN_TILES)
    ka = a.shape[1] // n_sh if a_sharded else a.shape[1]
    nb = b.shape[1] if a_sharded else b.shape[1] // n_sh
    n_i = n_rows // tm

    def body(a_ref, b_ref, o_ref, acc):
        i = pl.program_id(0)

        @pl.when(i == 0)
        def _():
            acc[...] = jnp.zeros_like(acc)

        for s in range(n_sh):
            a_s = a_ref[:, s * ka:(s + 1) * ka] if a_sharded else a_ref[...]
            b_s = b_ref[...] if a_sharded else b_ref[:, s * nb:(s + 1) * nb]
            acc[s] += lax.dot_general(a_s.astype(BF16), b_s.astype(BF16), (((0,), (0,)), ((), ())),
                                      preferred_element_type=F32)

        @pl.when(i == n_i - 1)
        def _():
            o_ref[...] = acc[...].astype(o_ref.dtype)

    return pl.pallas_call(
        body, name=name, grid=(n_i,),
        in_specs=[pl.BlockSpec((tm, a.shape[1]), lambda i: (i, 0)), pl.BlockSpec((tm, b.shape[1]), lambda i: (i, 0))],
        out_specs=pl.BlockSpec((n_sh, ka, nb), lambda i: (0, 0, 0)),
        out_shape=SDS((n_sh, ka, nb), BF16),
        scratch_shapes=[pltpu.VMEM((n_sh, ka, nb), F32)], compiler_params=_params(1),
    )(a, b)


def _mlp_fwd(h, wn, w_up4, w_down4, name, next_norm=None, glu_z=None):
    n_rows = h.shape[0]
    tm = _row_tile(n_rows, MLP_FWD_TILES)
    n_sh = w_up4.shape[0]
    f_sh = D_FF // n_sh
    w_down = w_down4.reshape(D_FF, D_MODEL)

    def body(*refs):
        refs = list(refs)
        h_ref, wn_ref, wu_ref, wd_ref = refs[:4]
        at = 4
        if next_norm is not None:
            nn_ref = refs[at]
            at += 1
        if glu_z is not None:
            z_ref = refs[at]
            at += 1
        o_ref, a_ref, hn_ref = refs[at:at + 3]
        at += 3
        if next_norm is not None:
            u_ref = refs[at]
            at += 1
        if glu_z is not None:
            hin_ref = refs[at]
            at += 1
        act_s = refs[at]
        h_in = h_ref[...]
        if glu_z is not None:
            h_in = h_in + z_ref[:, 0:D_MODEL].astype(F32) * jax.nn.sigmoid(z_ref[:, D_MODEL:2 * D_MODEL].astype(F32))
            hin_ref[...] = h_in
        hn = _rms(h_in, wn_ref[...]).astype(BF16)
        hn_ref[...] = hn
        for s in range(n_sh):
            cols = slice(s * f_sh, (s + 1) * f_sh)
            a = jnp.dot(hn, wu_ref[s], preferred_element_type=F32)
            a_ref[:, cols] = a.astype(BF16)
            act = jnp.maximum(a, 0.0)
            act_s[:, cols] = (act * act).astype(BF16)
        out = h_in + jnp.dot(act_s[...], wd_ref[...], preferred_element_type=F32)
        o_ref[...] = out
        if next_norm is not None:
            u_ref[...] = _rms(out, nn_ref[...])

    row = lambda i: (i, 0)
    vec = pl.BlockSpec((1, D_MODEL), lambda i: (0, 0))
    in_specs = [pl.BlockSpec((tm, D_MODEL), row), vec,
                pl.BlockSpec((n_sh, None, D_MODEL, f_sh), lambda i: (0, 0, 0, 0), pipeline_mode=pl.Buffered(1)),
                pl.BlockSpec((D_FF, D_MODEL), lambda i: (0, 0), pipeline_mode=pl.Buffered(1))]
    out_specs = [pl.BlockSpec((tm, D_MODEL), row), pl.BlockSpec((tm, D_FF), row), pl.BlockSpec((tm, D_MODEL), row)]
    out_shape = [SDS((n_rows, D_MODEL), F32), SDS((n_rows, D_FF), BF16), SDS((n_rows, D_MODEL), BF16)]
    args = [h, wn, w_up4, w_down]
    if next_norm is not None:
        in_specs.append(vec)
        args.append(next_norm)
    if glu_z is not None:
        in_specs.append(pl.BlockSpec((tm, 2 * D_MODEL), row))
        args.append(glu_z)
    for extra in (next_norm, glu_z):
        if extra is not None:
            out_specs.append(pl.BlockSpec((tm, D_MODEL), row))
            out_shape.append(SDS((n_rows, D_MODEL), F32))
    return pl.pallas_call(
        body, name=name, grid=(n_rows // tm,), in_specs=in_specs, out_specs=out_specs, out_shape=out_shape,
        scratch_shapes=[pltpu.VMEM((tm, D_FF), BF16)],
        compiler_params=_params(1),
    )(*args)


def _mlp_bwd_shard(s, dh, a, hn, dhn_prev, h, wn, w_up4, w_down4, dw_up_buf, dw_down_buf, name):
    n_rows = dh.shape[0]
    n_sh = w_up4.shape[0]
    f_sh = D_FF // n_sh
    tm = _row_tile(n_rows, MLP_BWD_TILES)
    n_i = n_rows // tm
    last = h is not None
    nt = (((1,), (1,)), ((), ()))
    tn = (((0,), (0,)), ((), ()))

    def body(*refs):
        refs = list(refs)
        dh_ref, a_ref, hn_ref, wu_ref, wd_ref = refs[:5]
        at = 5
        prev_ref = None
        if dhn_prev is not None:
            prev_ref = refs[at]
            at += 1
        if last:
            h_ref, wn_ref = refs[at:at + 2]
            at += 2
        if dw_up_buf is not None:
            at += 2
        o_ref, dwu_ref, dwd_ref = refs[at:at + 3]
        at += 3
        if last:
            dnorm_ref = refs[at]
            at += 1
        acc_u, acc_d = refs[at:at + 2]
        i = pl.program_id(0)

        @pl.when(i == 0)
        def _():
            acc_u[...] = jnp.zeros_like(acc_u)
            acc_d[...] = jnp.zeros_like(acc_d)
            if last:
                dnorm_ref[...] = jnp.zeros_like(dnorm_ref)

        dh16 = dh_ref[...].astype(BF16)
        r = jnp.maximum(a_ref[...].astype(F32), 0.0)
        dact = lax.dot_general(dh16, wd_ref[...], nt, preferred_element_type=F32)
        da16 = (dact * (2.0 * r)).astype(BF16)
        acc_d[...] += lax.dot_general((r * r).astype(BF16), dh16, tn, preferred_element_type=F32)
        acc_u[...] += lax.dot_general(hn_ref[...], da16, tn, preferred_element_type=F32)
        dhn = lax.dot_general(da16, wu_ref[...], nt, preferred_element_type=F32)
        if prev_ref is not None:
            dhn = dhn + prev_ref[...]
        if last:
            d_rms, dw_rows = _rms_bwd(dhn, h_ref[...], wn_ref[...])
            o_ref[...] = dh_ref[...] + d_rms
            dnorm_ref[...] += _fold8(dw_rows)
        else:
            o_ref[...] = dhn

        @pl.when(i == n_i - 1)
        def _():
            dwu_ref[...] = acc_u[...].astype(BF16)
            dwd_ref[...] = acc_d[...].astype(BF16)

    row = lambda i: (i, 0)
    tile = pl.BlockSpec((tm, D_MODEL), row)
    in_specs = [tile, pl.BlockSpec((tm, f_sh), lambda i: (i, s)), tile,
                pl.BlockSpec((None, None, D_MODEL, f_sh), lambda i: (s, 0, 0, 0)),
                pl.BlockSpec((None, None, f_sh, D_MODEL), lambda i: (s, 0, 0, 0))]
    args = [dh, a, hn, w_up4, w_down4]
    if dhn_prev is not None:
        in_specs.append(tile)
        args.append(dhn_prev)
    if last:
        in_specs += [tile, pl.BlockSpec((1, D_MODEL), lambda i: (0, 0))]
        args += [h, wn]
    aliases = {}
    if dw_up_buf is not None:
        aliases = {len(args): 1, len(args) + 1: 2}
        in_specs += [pl.BlockSpec(memory_space=pl.ANY)] * 2
        args += [dw_up_buf, dw_down_buf]
    out_specs = [tile, pl.BlockSpec((None, D_MODEL, f_sh), lambda i: (s, 0, 0)),
                 pl.BlockSpec((None, f_sh, D_MODEL), lambda i: (s, 0, 0))]
    out_shape = [SDS((n_rows, D_MODEL), F32), SDS((n_sh, D_MODEL, f_sh), BF16), SDS((n_sh, f_sh, D_MODEL), BF16)]
    if last:
        out_specs.append(pl.BlockSpec((8, D_MODEL), lambda i: (0, 0)))
        out_shape.append(SDS((8, D_MODEL), F32))
    return pl.pallas_call(
        body, name=name, grid=(n_i,), in_specs=in_specs, out_specs=out_specs, out_shape=out_shape,
        input_output_aliases=aliases,
        scratch_shapes=[pltpu.VMEM((D_MODEL, f_sh), F32), pltpu.VMEM((f_sh, D_MODEL), F32)],
        compiler_params=_params(1),
    )(*args)


def _attn_masks(n):
    qi = lax.broadcasted_iota(jnp.int32, (BLOCK, 3 * BLOCK), 0)
    col = lax.broadcasted_iota(jnp.int32, (BLOCK, 3 * BLOCK), 1)
    kj = col - BLOCK
    dist = BLOCK + qi - kj
    kmin = jnp.where(n == 0, 2 * BLOCK, jnp.where(n == 1, BLOCK, 0))
    band_ok = (col >= BLOCK) & (dist >= 0) & (dist < BLOCK) & (kj >= kmin)
    q_pos = n * BLOCK + qi - PAD
    meta_ok = (col >= PAD) & (col < BLOCK) & (col - PAD <= q_pos)
    distf = jnp.where(col >= BLOCK, dist, 0).astype(F32)
    return band_ok | meta_ok, distf


def _alibi_slope(h):
    return float(2.0 ** (-8.0 * (h + 1) / N_HEADS))


def _attn_bias(n, bias_s):
    ok, distf = _attn_masks(n)
    for h in range(N_HEADS):
        bias_s[h] = jnp.where(ok, -_alibi_slope(h) * distf, NEG_INF)


def _attn_fwd(qkv, sinks, n_ex, nb):
    n_rows = qkv.shape[0]
    kvb = N_HEADS * HEAD_DIM // KV_DIM

    def body(sink_ref, q_ref, kvm_ref, kvp_ref, kvc_ref, o_ref, lse_ref, k_s, v_s, q_s, bias_s):
        n = pl.program_id(1)

        @pl.when(n <= 2)
        def _():
            _attn_bias(n, bias_s)

        v_s[...] = jnp.ones_like(v_s)
        for part, ref in enumerate((kvm_ref, kvp_ref, kvc_ref)):
            rows = slice(part * BLOCK, (part + 1) * BLOCK)
            k_s[rows, :] = ref[:, 0:N_KV * HEAD_DIM]
            for kv in range(N_KV):
                v_s[rows, kv * 2 * HEAD_DIM:kv * 2 * HEAD_DIM + HEAD_DIM] = \
                    ref[:, (N_KV + kv) * HEAD_DIM:(N_KV + kv + 1) * HEAD_DIM]
        for kv in range(N_KV):
            for g in range(GQA):
                h = kv * GQA + g
                q_s[kv, g * BLOCK:(g + 1) * BLOCK, :] = q_ref[:, h * HEAD_DIM:(h + 1) * HEAD_DIM] * (HEAD_DIM ** -0.5)
            s4 = lax.dot_general(q_s[kv], k_s[:, kv * HEAD_DIM:(kv + 1) * HEAD_DIM], (((1,), (1,)), ((), ())),
                                 preferred_element_type=F32)
            es, ms, sink_es = [], [], []
            for g in range(GQA):
                h = kv * GQA + g
                s = s4[g * BLOCK:(g + 1) * BLOCK] + bias_s[h]
                sink = sink_ref[0, h]
                m = jnp.maximum(jnp.max(s, axis=-1, keepdims=True), sink)
                es.append(jnp.exp(s - m).astype(BF16))
                ms.append(m)
                sink_es.append(jnp.exp(sink - m))
            pv = jnp.dot(jnp.concatenate(es, axis=0), v_s[:, kv * 2 * HEAD_DIM:(kv + 1) * 2 * HEAD_DIM],
                         preferred_element_type=F32)
            for g in range(GQA):
                h = kv * GQA + g
                pg = pv[g * BLOCK:(g + 1) * BLOCK]
                l = pg[:, HEAD_DIM:HEAD_DIM + 1] + sink_es[g]
                o_ref[:, h * HEAD_DIM:(h + 1) * HEAD_DIM] = (pg[:, 0:HEAD_DIM] * (1.0 / l)).astype(BF16)
                lse_ref[:, h:h + 1] = ms[g] + jnp.log(l)

    return pl.pallas_call(
        body, name="attn_fwd", grid=(n_ex, nb),
        in_specs=[pl.BlockSpec(memory_space=pltpu.SMEM),
                  pl.BlockSpec((BLOCK, N_HEADS * HEAD_DIM), lambda b, n: (b * nb + n, 0)),
                  pl.BlockSpec((BLOCK, KV_DIM), lambda b, n: (b * nb, kvb)),
                  pl.BlockSpec((BLOCK, KV_DIM), lambda b, n: (b * nb + jnp.maximum(n - 1, 0), kvb)),
                  pl.BlockSpec((BLOCK, KV_DIM), lambda b, n: (b * nb + n, kvb))],
        out_specs=[pl.BlockSpec((BLOCK, N_HEADS * HEAD_DIM), lambda b, n: (b * nb + n, 0)),
                   pl.BlockSpec((BLOCK, N_HEADS), lambda b, n: (b * nb + n, 0))],
        out_shape=[SDS((n_rows, N_HEADS * HEAD_DIM), BF16), SDS((n_rows, N_HEADS), F32)],
        scratch_shapes=[pltpu.VMEM((3 * BLOCK, N_KV * HEAD_DIM), BF16), pltpu.VMEM((3 * BLOCK, 2 * N_KV * HEAD_DIM), BF16),
                        pltpu.VMEM((N_KV, GQA * BLOCK, HEAD_DIM), BF16), pltpu.VMEM((N_HEADS, BLOCK, 3 * BLOCK), F32)],
        compiler_params=_params(2),
    )(sinks, qkv, qkv, qkv, qkv)


def _attn_bwd(qkv, sinks, o, lse, do, n_ex, nb):
    n_rows = qkv.shape[0]
    kvb = N_HEADS * HEAD_DIM // KV_DIM
    scale = HEAD_DIM ** -0.5
    nq = lambda r: nb - 1 - r

    def body(sink_ref, q_ref, kvm_ref, kvp_ref, kvc_ref, o_ref, lse_ref, do_ref, dqkv_ref, dsink_ref,
             k_s, v_s, dkv_s, carry_s, meta_s, q_s, do_s, bias_s):
        b, r = pl.program_id(0), pl.program_id(1)
        n = nq(r)

        @pl.when((r == 0) | (n <= 1))
        def _():
            _attn_bias(n, bias_s)

        @pl.when((b == 0) & (r == 0))
        def _():
            dsink_ref[...] = jnp.zeros_like(dsink_ref)

        @pl.when(r == 0)
        def _():
            carry_s[...] = jnp.zeros_like(carry_s)
            meta_s[...] = jnp.zeros_like(meta_s)

        for part, ref in enumerate((kvm_ref, kvp_ref, kvc_ref)):
            k_s[part * BLOCK:(part + 1) * BLOCK, :] = ref[:, 0:N_KV * HEAD_DIM]
            v_s[part * BLOCK:(part + 1) * BLOCK, :] = ref[:, N_KV * HEAD_DIM:KV_DIM]
        nt = (((1,), (1,)), ((), ()))
        tn = (((0,), (0,)), ((), ()))
        for kv in range(N_KV):
            kcols = slice(kv * HEAD_DIM, (kv + 1) * HEAD_DIM)
            vcols = slice(N_KV * HEAD_DIM + kv * HEAD_DIM, N_KV * HEAD_DIM + (kv + 1) * HEAD_DIM)
            for g in range(GQA):
                cols = slice((kv * GQA + g) * HEAD_DIM, (kv * GQA + g + 1) * HEAD_DIM)
                q_s[kv, g * BLOCK:(g + 1) * BLOCK, :] = q_ref[:, cols] * scale
                do_s[kv, g * BLOCK:(g + 1) * BLOCK, :] = do_ref[:, cols]
            kh, vh = k_s[:, kcols], v_s[:, kcols]
            s4 = lax.dot_general(q_s[kv], kh, nt, preferred_element_type=F32)
            dp4 = lax.dot_general(do_s[kv], vh, nt, preferred_element_type=F32)
            ps, dss = [], []
            for g in range(GQA):
                h = kv * GQA + g
                cols = slice(h * HEAD_DIM, (h + 1) * HEAD_DIM)
                rows = slice(g * BLOCK, (g + 1) * BLOCK)
                s = s4[rows] + bias_s[h]
                lse_h = lse_ref[:, h:h + 1]
                p = jnp.exp(s - lse_h)
                delta = jnp.sum(do_ref[:, cols].astype(F32) * o_ref[:, cols].astype(F32), axis=-1, keepdims=True)
                dsink_ref[:, h:h + 1] += -jnp.exp(sink_ref[0, h] - lse_h) * delta
                ps.append(p.astype(BF16))
                dss.append((p * (dp4[rows] - delta)).astype(BF16))
            p4, ds4 = jnp.concatenate(ps, axis=0), jnp.concatenate(dss, axis=0)
            dq4 = jnp.dot(ds4, kh, preferred_element_type=F32) * scale
            for g in range(GQA):
                cols = slice((kv * GQA + g) * HEAD_DIM, (kv * GQA + g + 1) * HEAD_DIM)
                dqkv_ref[:, cols] = dq4[g * BLOCK:(g + 1) * BLOCK].astype(BF16)
            dkv_s[:, kcols] = lax.dot_general(ds4, q_s[kv], tn, preferred_element_type=F32)
            dkv_s[:, vcols] = lax.dot_general(p4, do_s[kv], tn, preferred_element_type=F32)

        meta_s[...] += dkv_s[0:BLOCK, :]
        cur = dkv_s[2 * BLOCK:3 * BLOCK, :] + carry_s[...]
        carry_s[...] = dkv_s[BLOCK:2 * BLOCK, :]

        @pl.when(n > 0)
        def _():
            dqkv_ref[:, N_HEADS * HEAD_DIM:QKV_DIM] = cur.astype(BF16)

        @pl.when(n == 0)
        def _():
            dqkv_ref[:, N_HEADS * HEAD_DIM:QKV_DIM] = (cur + meta_s[...]).astype(BF16)

    blk = lambda b, r: (b * nb + nq(r), 0)
    return pl.pallas_call(
        body, name="attn_bwd", grid=(n_ex, nb),
        in_specs=[pl.BlockSpec(memory_space=pltpu.SMEM),
                  pl.BlockSpec((BLOCK, N_HEADS * HEAD_DIM), blk),
                  pl.BlockSpec((BLOCK, KV_DIM), lambda b, r: (b * nb, kvb)),
                  pl.BlockSpec((BLOCK, KV_DIM), lambda b, r: (b * nb + jnp.maximum(nq(r) - 1, 0), kvb)),
                  pl.BlockSpec((BLOCK, KV_DIM), lambda b, r: (b * nb + nq(r), kvb)),
                  pl.BlockSpec((BLOCK, N_HEADS * HEAD_DIM), blk),
                  pl.BlockSpec((BLOCK, N_HEADS), blk),
                  pl.BlockSpec((BLOCK, N_HEADS * HEAD_DIM), blk)],
        out_specs=[pl.BlockSpec((BLOCK, QKV_DIM), blk),
                   pl.BlockSpec((BLOCK, N_HEADS), lambda b, r: (0, 0))],
        out_shape=[SDS((n_rows, QKV_DIM), BF16), SDS((BLOCK, N_HEADS), F32)],
        scratch_shapes=[pltpu.VMEM((3 * BLOCK, N_KV * HEAD_DIM), BF16), pltpu.VMEM((3 * BLOCK, N_KV * HEAD_DIM), BF16),
                        pltpu.VMEM((3 * BLOCK, KV_DIM), F32), pltpu.VMEM((BLOCK, KV_DIM), F32),
                        pltpu.VMEM((BLOCK, KV_DIM), F32), pltpu.VMEM((N_KV, GQA * BLOCK, HEAD_DIM), BF16),
                        pltpu.VMEM((N_KV, GQA * BLOCK, HEAD_DIM), BF16), pltpu.VMEM((N_HEADS, BLOCK, 3 * BLOCK), F32)],
        compiler_params=_params(2),
    )(sinks, qkv, qkv, qkv, qkv, o, lse, do)


SEG = TM // 8
XW = 256 * PAIRS_PER_CHUNK


def _cmul_add(xr, xi, mr, mi, sr, si):
    return xr + mr * sr - mi * si, xi + mr * si + mi * sr


def _to_segments(src_ref, dst):
    for s in range(SEG):
        dst[s * 8:(s + 1) * 8, :] = src_ref[pl.ds(s, 8, stride=SEG), :]


def _from_segments(src, i):
    return src[pl.ds(i, SEG, stride=8), :]


def _scan_segments(buf, tab_ref, pw_ref, carry_s, reverse):
    shifts = (7, 6, 4) if reverse else (1, 2, 4)
    row_id = lax.broadcasted_iota(jnp.int32, (8, 128), 0)

    def local(si, prev):
        s = (SEG - 1 - si) if reverse else si
        row = pl.multiple_of(s * 8, 8)
        out = []
        for j in range(PAIRS_PER_CHUNK):
            re, im = slice(256 * j, 256 * j + 128), slice(256 * j + 128, 256 * j + 256)
            xr, xi = _cmul_add(buf[pl.ds(row, 8), re], buf[pl.ds(row, 8), im],
                               tab_ref[j, 0], tab_ref[j, 1], prev[2 * j], prev[2 * j + 1])
            buf[pl.ds(row, 8), re] = xr
            buf[pl.ds(row, 8), im] = xi
            out += [xr, xi]
        return tuple(out)

    zero = jnp.zeros((8, 128), F32)
    edge = lax.fori_loop(0, SEG, local, (zero,) * (2 * PAIRS_PER_CHUNK))

    entering = []
    for j in range(PAIRS_PER_CHUNK):
        er, ei = edge[2 * j], edge[2 * j + 1]
        if reverse:
            sr = jnp.where(row_id == 7, carry_s[2 * j], pltpu.roll(er, 7, 0))
            si_ = jnp.where(row_id == 7, carry_s[2 * j + 1], pltpu.roll(ei, 7, 0))
        else:
            sr = jnp.where(row_id == 0, carry_s[2 * j], pltpu.roll(er, 1, 0))
            si_ = jnp.where(row_id == 0, carry_s[2 * j + 1], pltpu.roll(ei, 1, 0))
        for lvl, sh in enumerate(shifts):
            sr, si_ = _cmul_add(sr, si_, tab_ref[j, 2 + 2 * lvl], tab_ref[j, 3 + 2 * lvl],
                                pltpu.roll(sr, sh, 0), pltpu.roll(si_, sh, 0))
        entering += [sr, si_]
        tr, ti = _cmul_add(er, ei, tab_ref[j, 2], tab_ref[j, 3], sr, si_)
        out_row = slice(0, 1) if reverse else slice(7, 8)
        carry_s[2 * j] = jnp.broadcast_to(tr[out_row], (8, 128))
        carry_s[2 * j + 1] = jnp.broadcast_to(ti[out_row], (8, 128))

    def fix(s, _):
        row = pl.multiple_of(s * 8, 8)
        for j in range(PAIRS_PER_CHUNK):
            re, im = slice(256 * j, 256 * j + 128), slice(256 * j + 128, 256 * j + 256)
            xr, xi = _cmul_add(buf[pl.ds(row, 8), re], buf[pl.ds(row, 8), im],
                               pw_ref[j, 0, s], pw_ref[j, 1, s], entering[2 * j], entering[2 * j + 1])
            buf[pl.ds(row, 8), re] = xr
            buf[pl.ds(row, 8), im] = xi
        return 0

    lax.fori_loop(0, SEG, fix, 0)


def _ssm_fwd(u, b_pad, c_pad, tab, pw, d_skip, n_ex, lp):
    n_rows = u.shape[0]
    n_t = lp // TM
    n_chunk = D_MODEL // 128

    def body(u_ref, bp_ref, cp_ref, tab_ref, pw_ref, d_ref, yg_ref, y_ref, xs_ref, buf, carry_s, us, ys):
        @pl.when(pl.program_id(2) == 0)
        def _():
            carry_s[...] = jnp.zeros_like(carry_s)

        _to_segments(u_ref, us)
        ub = us[...]
        u16 = ub.astype(BF16)
        buf[...] = jnp.dot(u16, bp_ref[...], preferred_element_type=F32)
        _scan_segments(buf, tab_ref, pw_ref, carry_s, reverse=False)
        xb = buf[...].astype(BF16)
        xs_ref[...] = xb
        ys[...] = d_ref[...] * ub + jnp.dot(xb, cp_ref[...], preferred_element_type=F32)
        for i in range(8):
            yi = _from_segments(ys, i)
            y_ref[i * SEG:(i + 1) * SEG, :] = yi
            yg_ref[i * SEG:(i + 1) * SEG, :] = _gelu(yi).astype(BF16)

    rows = lambda b, q, t: (b * n_t + t, q)
    return pl.pallas_call(
        body, name="ssm_fwd", grid=(n_ex, n_chunk, n_t),
        in_specs=[pl.BlockSpec((TM, 128), rows),
                  pl.BlockSpec((None, 128, XW), lambda b, q, t: (q, 0, 0)),
                  pl.BlockSpec((None, XW, 128), lambda b, q, t: (q, 0, 0)),
                  pl.BlockSpec((PAIRS_PER_CHUNK, 8, 8, 128), lambda b, q, t: (q, 0, 0, 0)),
                  pl.BlockSpec((PAIRS_PER_CHUNK, 2, SEG, 8, 128), lambda b, q, t: (q, 0, 0, 0, 0)),
                  pl.BlockSpec((1, 128), lambda b, q, t: (0, q))],
        out_specs=[pl.BlockSpec((TM, 128), rows), pl.BlockSpec((TM, 128), rows),
                   pl.BlockSpec((None, TM, XW), lambda b, q, t: (q, b * n_t + t, 0))],
        out_shape=[SDS((n_rows, D_MODEL), BF16), SDS((n_rows, D_MODEL), F32), SDS((n_chunk, n_rows, XW), BF16)],
        scratch_shapes=[pltpu.VMEM((TM, XW), F32), pltpu.VMEM((2 * PAIRS_PER_CHUNK, 8, 128), F32),
                        pltpu.VMEM((TM, 128), F32), pltpu.VMEM((TM, 128), F32)],
        compiler_params=_params(3),
    )(u, b_pad, c_pad, tab, pw, d_skip)


def _ssm_bwd(dyg, y, u, xs, ct_pad, bt_pad, tab_rev, pw_rev, d_skip, n_ex, lp):
    n_rows = u.shape[0]
    n_t = lp // TM
    n_chunk = D_MODEL // 128
    tile = lambda q, b, t: (b * n_t + (n_t - 1 - t), q)

    def body(dyg_ref, y_ref, u_ref, xs_ref, xp_ref, ct_ref, bt_ref, tab_ref, pw_ref, d_ref,
             du_ref, db_ref, dc_ref, da_ref, dd_ref, buf, xf, carry_s, us, dys, dyp):
        b, t = pl.program_id(1), pl.program_id(2)

        @pl.when((b == 0) & (t == 0))
        def _():
            db_ref[...] = jnp.zeros_like(db_ref)
            dc_ref[...] = jnp.zeros_like(dc_ref)
            da_ref[...] = jnp.zeros_like(da_ref)
            dd_ref[...] = jnp.zeros_like(dd_ref)

        @pl.when(t == 0)
        def _():
            carry_s[...] = jnp.zeros_like(carry_s)

        dys[...] = dyg_ref[...].astype(F32) * _gelu_grad(y_ref[...])
        dd_ref[...] += _fold8(dys[...] * u_ref[...])
        _to_segments(dys, dyp)
        dy = dyp[...]
        _to_segments(u_ref, us)
        dy16 = dy.astype(BF16)
        first_tile = t == n_t - 1
        tn = (((0,), (0,)), ((), ()))
        buf[...] = jnp.dot(dy16, ct_ref[...], preferred_element_type=F32)
        dc_ref[...] += lax.dot_general(dy16, xs_ref[...], tn, preferred_element_type=F32)
        xf[16:16 + TM, :] = xs_ref[...].astype(F32)
        xf[0:16, :] = jnp.where(first_tile, 0.0, xp_ref[...].astype(F32))
        _scan_segments(buf, tab_ref, pw_ref, carry_s, reverse=True)
        g16 = buf[...].astype(BF16)
        dys[...] = d_ref[...] * dy + jnp.dot(g16, bt_ref[...], preferred_element_type=F32)
        db_ref[...] += lax.dot_general(us[...].astype(BF16), g16, tn, preferred_element_type=F32)
        row_id = lax.broadcasted_iota(jnp.int32, (8, 128), 0)
        for j in range(PAIRS_PER_CHUNK):
            re, im = slice(256 * j, 256 * j + 128), slice(256 * j + 128, 256 * j + 256)
            first = [jnp.where(row_id == 0, jnp.broadcast_to(xf[15:16, c], (8, 128)),
                               pltpu.roll(xf[8 + TM:16 + TM, c], 1, 0)) for c in (re, im)]
            for rows, pr, pi in ((slice(0, 8), first[0], first[1]),
                                 (slice(8, TM), xf[16:8 + TM, re], xf[16:8 + TM, im])):
                gr, gi = buf[rows, re], buf[rows, im]
                da_ref[j, 0] += _fold8(gr * pr + gi * pi)
                da_ref[j, 1] += _fold8(gi * pr - gr * pi)
        for i in range(8):
            du_ref[i * SEG:(i + 1) * SEG, :] = _from_segments(dys, i)

    prev16 = lambda q, b, t: (q, jnp.maximum((b * n_t + (n_t - 1 - t)) * (TM // 16) - 1, 0), 0)
    return pl.pallas_call(
        body, name="ssm_bwd", grid=(n_chunk, n_ex, n_t),
        in_specs=[pl.BlockSpec((TM, 128), tile), pl.BlockSpec((TM, 128), tile), pl.BlockSpec((TM, 128), tile),
                  pl.BlockSpec((None, TM, XW), lambda q, b, t: (q, b * n_t + (n_t - 1 - t), 0)),
                  pl.BlockSpec((None, 16, XW), prev16),
                  pl.BlockSpec((None, 128, XW), lambda q, b, t: (q, 0, 0)),
                  pl.BlockSpec((None, XW, 128), lambda q, b, t: (q, 0, 0)),
                  pl.BlockSpec((PAIRS_PER_CHUNK, 8, 8, 128), lambda q, b, t: (q, 0, 0, 0)),
                  pl.BlockSpec((PAIRS_PER_CHUNK, 2, SEG, 8, 128), lambda q, b, t: (q, 0, 0, 0, 0)),
                  pl.BlockSpec((1, 128), lambda q, b, t: (0, q))],
        out_specs=[pl.BlockSpec((TM, 128), tile),
                   pl.BlockSpec((None, 128, XW), lambda q, b, t: (q, 0, 0)),
                   pl.BlockSpec((None, 128, XW), lambda q, b, t: (q, 0, 0)),
                   pl.BlockSpec((PAIRS_PER_CHUNK, 2, 8, 128), lambda q, b, t: (q, 0, 0, 0)),
                   pl.BlockSpec((8, 128), lambda q, b, t: (0, q))],
        out_shape=[SDS((n_rows, D_MODEL), F32), SDS((n_chunk, 128, XW), F32), SDS((n_chunk, 128, XW), F32),
                   SDS((N_PAIR, 2, 8, 128), F32), SDS((8, D_MODEL), F32)],
        scratch_shapes=[pltpu.VMEM((TM, XW), F32), pltpu.VMEM((TM + 16, XW), F32),
                        pltpu.VMEM((2 * PAIRS_PER_CHUNK, 8, 128), F32), pltpu.VMEM((TM, 128), F32),
                        pltpu.VMEM((TM, 128), F32), pltpu.VMEM((TM, 128), F32)],
        compiler_params=_params(3),
    )(dyg, y, u, xs, xs, ct_pad, bt_pad, tab_rev, pw_rev, d_skip)


def _rms_bwd_call(dhn, h, wn, dres, name):
    n_rows = h.shape[0]

    def body(dhn_ref, h_ref, wn_ref, dres_ref, o_ref, dw_ref):
        @pl.when(pl.program_id(0) == 0)
        def _():
            dw_ref[...] = jnp.zeros_like(dw_ref)

        dh, dw_rows = _rms_bwd(dhn_ref[...], h_ref[...], wn_ref[...])
        o_ref[...] = dres_ref[...] + dh
        dw_ref[...] += _fold8(dw_rows)

    row = lambda i: (i, 0)
    return pl.pallas_call(
        body, name=name, grid=(n_rows // TM,),
        in_specs=[pl.BlockSpec((TM, D_MODEL), row), pl.BlockSpec((TM, D_MODEL), row),
                  pl.BlockSpec((1, D_MODEL), lambda i: (0, 0)), pl.BlockSpec((TM, D_MODEL), row)],
        out_specs=[pl.BlockSpec((TM, D_MODEL), row), pl.BlockSpec((8, D_MODEL), lambda i: (0, 0))],
        out_shape=[SDS((n_rows, D_MODEL), F32), SDS((8, D_MODEL), F32)], compiler_params=_params(1),
    )(dhn, h, wn, dres)


def _glu_bwd(dh, z, w4):
    n_rows = dh.shape[0]
    tm = _row_tile(n_rows, MM_TILES)
    n_sh, _, k, n = w4.shape

    def body(dh_ref, z_ref, w_ref, dz_ref, dyg_ref):
        sg = jax.nn.sigmoid(z_ref[:, D_MODEL:2 * D_MODEL].astype(F32))
        d = dh_ref[...]
        dz_ref[:, 0:D_MODEL] = (d * sg).astype(BF16)
        dz_ref[:, D_MODEL:2 * D_MODEL] = (d * z_ref[:, 0:D_MODEL].astype(F32) * sg * (1.0 - sg)).astype(BF16)
        acc = None
        for s in range(n_sh):
            part = lax.dot_general(dz_ref[:, s * n:(s + 1) * n], w_ref[s], (((1,), (1,)), ((), ())),
                                   preferred_element_type=F32)
            acc = part if acc is None else acc + part
        dyg_ref[...] = acc.astype(BF16)

    row = lambda i: (i, 0)
    return pl.pallas_call(
        body, name="glu_bwd", grid=(n_rows // tm,),
        in_specs=[pl.BlockSpec((tm, D_MODEL), row), pl.BlockSpec((tm, 2 * D_MODEL), row), _w4_spec(w4)],
        out_specs=[pl.BlockSpec((tm, 2 * D_MODEL), row), pl.BlockSpec((tm, k), row)],
        out_shape=[SDS((n_rows, 2 * D_MODEL), BF16), SDS((n_rows, k), BF16)], compiler_params=_params(1),
    )(dh, z, w4)


def _loss_head(h, wn, target, n_ex, nb):
    n_rows = h.shape[0]
    per_tile = TM // BLOCK
    n_tiles = nb // per_tile

    def body(h_ref, wn_ref, *rest):
        t_refs, (dh_ref, loss_ref, dw_ref) = rest[:per_tile], rest[per_tile:]
        b, j = pl.program_id(0), pl.program_id(1)

        @pl.when((b == 0) & (j == 0))
        def _():
            loss_ref[...] = jnp.zeros_like(loss_ref)
            dw_ref[...] = jnp.zeros_like(dw_ref)

        def block(k):
            rows = slice(k * BLOCK, (k + 1) * BLOCK)
            hh = h_ref[rows, :]
            diff = _rms(hh, wn_ref[...]) - t_refs[k][...]
            loss_ref[...] += 0.5 * jnp.sum(diff * diff) * (1.0 / D_MODEL)
            dh, dw_rows = _rms_bwd(diff * (1.0 / D_MODEL), hh, wn_ref[...])
            dh_ref[rows, :] = dh
            dw_ref[...] += _fold8(dw_rows)

        @pl.when(j == 0)
        def _():
            dh_ref[0:BLOCK, :] = jnp.zeros((BLOCK, D_MODEL), F32)

        pl.when(j > 0)(lambda: block(0))
        for k in range(1, per_tile):
            block(k)

    def t_spec(k):
        return pl.BlockSpec((BLOCK, D_MODEL), lambda b, j: (b * (nb - 1) + jnp.maximum(per_tile * j + k - 1, 0), 0))

    tile = pl.BlockSpec((TM, D_MODEL), lambda b, j: (b * n_tiles + j, 0))
    return pl.pallas_call(
        body, name="loss_head", grid=(n_ex, n_tiles),
        in_specs=[tile, pl.BlockSpec((1, D_MODEL), lambda b, j: (0, 0))] + [t_spec(k) for k in range(per_tile)],
        out_specs=[tile, pl.BlockSpec((8, 128), lambda b, j: (0, 0)), pl.BlockSpec((8, D_MODEL), lambda b, j: (0, 0))],
        out_shape=[SDS((n_rows, D_MODEL), F32), SDS((8, 128), F32), SDS((8, D_MODEL), F32)],
        compiler_params=_params(2),
    )(h, wn, *([target] * per_tile))


def _adamw(pieces, w, m, v, name):
    n_layers = len(pieces)
    rows, cols = pieces[0].shape[1:]
    rb = rows
    for cand in (256, 136, 128, 64, 32, 16, 8):
        if rows % cand == 0 and rows > cand:
            rb = cand
            break
    n_blk = rows // rb
    c1 = 1.0 / (1.0 - ADAM_B1 ** ADAM_STEP)
    c2 = 1.0 / (1.0 - ADAM_B2 ** ADAM_STEP)

    def body(*refs):
        p_refs = refs[:n_layers]
        w_ref, m_ref, v_ref, g_out, d_out, m_out, v_out = refs[n_layers:]
        layer = pl.program_id(0)
        g = None
        for l, p_ref in enumerate(p_refs):
            gl = p_ref[0].astype(F32)
            for k in range(1, N_DEV):
                gl = gl + p_ref[k].astype(F32)
            g = gl if g is None else jnp.where(layer == l, gl, g)
        m_new = ADAM_B1 * m_ref[...] + (1.0 - ADAM_B1) * g
        v_new = ADAM_B2 * v_ref[...] + (1.0 - ADAM_B2) * (g * g)
        g_out[...] = g
        m_out[...] = m_new
        v_out[...] = v_new
        d_out[...] = -ADAM_LR * ((m_new * c1) / (jnp.sqrt(v_new * c2) + ADAM_EPS) + ADAM_WD * w_ref[...])

    def piece_spec(l):
        return pl.BlockSpec((N_DEV, rb, cols), lambda ly, i: (0, jnp.where(ly == l, i, 0), 0))

    blk = pl.BlockSpec((rb, cols), lambda ly, i: (ly * n_blk + i, 0))
    return pl.pallas_call(
        body, name=name, grid=(n_layers, n_blk),
        in_specs=[piece_spec(l) for l in range(n_layers)] + [blk, blk, blk],
        out_specs=[blk, blk, blk, blk],
        out_shape=[SDS((n_layers * rows, cols), F32)] * 4, compiler_params=_params(2),
    )(*pieces, w, m, v)


_HBM = pl.BlockSpec(memory_space=pltpu.HBM)
_SEM = pl.BlockSpec(memory_space=pltpu.SEMAPHORE)
_EFFECT = pltpu.SideEffectType.DATAFLOW_SIDE_EFFECTING
N_GATHER_PEERS = N_CHIPS - 1
N_EXCHANGE_PEERS = N_DEV - 1


def _gather_copies(srcs, lands, send_sems, recv_sems):
    x, y, c = lax.axis_index("x"), lax.axis_index("y"), lax.axis_index("c")
    mine = 2 * x + y
    chips = [(1 - x, y), (x, 1 - y), (1 - x, 1 - y)]
    out, inc = [], []
    for a in range(len(srcs)):
        for k, (px, py) in enumerate(chips):
            j = a * N_GATHER_PEERS + k
            sems = dict(send_sem=send_sems.at[j], recv_sem=recv_sems.at[j], device_id=(px, py, c),
                        device_id_type=pl.DeviceIdType.MESH)
            out.append(pltpu.make_async_remote_copy(src_ref=srcs[a], dst_ref=lands[a].at[mine], **sems))
            inc.append(pltpu.make_async_remote_copy(src_ref=srcs[a], dst_ref=lands[a].at[2 * px + py], **sems))
    return out, inc


def _exchange_copies(n_scatter):
    def copies(srcs, lands, send_sems, recv_sems):
        x, y, c = lax.axis_index("x"), lax.axis_index("y"), lax.axis_index("c")
        me = 4 * x + 2 * y + c
        peers = [(x ^ (k >> 2), y ^ ((k >> 1) & 1), c ^ (k & 1)) for k in range(1, N_DEV)]
        out, inc = [], []
        for a in range(len(srcs)):
            for k, (px, py, pc) in enumerate(peers):
                j = a * N_EXCHANGE_PEERS + k
                sems = dict(send_sem=send_sems.at[j], recv_sem=recv_sems.at[j], device_id=(px, py, pc),
                            device_id_type=pl.DeviceIdType.MESH)
                theirs = srcs[a].at[2 * px + py] if a < n_scatter else srcs[a]
                mine = srcs[a].at[2 * x + y] if a < n_scatter else srcs[a]
                out.append(pltpu.make_async_remote_copy(src_ref=theirs, dst_ref=lands[a].at[me], **sems))
                inc.append(pltpu.make_async_remote_copy(src_ref=mine, dst_ref=lands[a].at[4 * px + 2 * py + pc], **sems))
        return out, inc

    return copies


def _split_start(groups, copies_fn, n_peers, name):
    sizes = [len(srcs) for srcs, _ in groups]
    flat = [a for srcs, lands in groups for a in list(srcs) + list(lands)]
    n_flat, n_grp = len(flat), len(groups)

    def body(*refs):
        sems = refs[2 * n_flat:2 * n_flat + 2 * n_grp]
        token = refs[-1]
        at = 0
        for gi, n in enumerate(sizes):
            out, _ = copies_fn(refs[at:at + n], refs[at + n:at + 2 * n], sems[2 * gi], sems[2 * gi + 1])
            for cp in out:
                cp.start()
            at += 2 * n
        token[...] = jnp.zeros_like(token)

    sem_shapes = []
    for n in sizes:
        sem_shapes += [pltpu.SemaphoreType.DMA((n * n_peers,)), pltpu.SemaphoreType.DMA((n * n_peers,))]
    res = pl.pallas_call(
        body, name=name,
        out_shape=(*[pltpu.HBM(a.shape, a.dtype) for a in flat], *sem_shapes, SDS((8, 128), F32)),
        in_specs=[_HBM] * n_flat,
        out_specs=(*[_HBM] * n_flat, *[_SEM] * (2 * n_grp), pl.BlockSpec(memory_space=pltpu.VMEM)),
        input_output_aliases={i: i for i in range(n_flat)},
        compiler_params=pltpu.CompilerParams(has_side_effects=_EFFECT),
    )(*[pltpu.with_memory_space_constraint(a, pltpu.HBM) for a in flat])
    handles, at = [], 0
    for gi, n in enumerate(sizes):
        handles.append((res[n_flat + 2 * gi], res[n_flat + 2 * gi + 1], list(res[at:at + n]), list(res[at + n:at + 2 * n])))
        at += 2 * n
    return handles, res[-1]


def _split_wait(handle, after, copies_fn, name):
    send_sems, recv_sems, srcs, lands = handle
    n = len(srcs)
    after = list(after) if isinstance(after, (list, tuple)) else [after]

    def body(*refs):
        out, inc = copies_fn(refs[:n], refs[n:2 * n], refs[2 * n], refs[2 * n + 1])
        for cp in out:
            cp.wait_send()
        for cp in inc:
            cp.wait_recv()

    flat = list(srcs) + list(lands)
    res = pl.pallas_call(
        body, name=name,
        out_shape=tuple(pltpu.HBM(a.shape, a.dtype) for a in flat),
        in_specs=[_HBM] * (2 * n) + [_SEM, _SEM] + [pl.BlockSpec(memory_space=pl.ANY)] * len(after),
        out_specs=tuple([_HBM] * (2 * n)),
        input_output_aliases={i: i for i in range(2 * n)},
        compiler_params=pltpu.CompilerParams(has_side_effects=_EFFECT),
    )(*flat, send_sems, recv_sems, *after)
    return list(res[n:])


def _landing(own, slot, n_slots):
    return lax.dynamic_update_index_in_dim(lax.empty((n_slots,) + own.shape, own.dtype), own, slot, 0)


def _ssm_discretize(lam_re, lam_im, log_dt, b_re, b_im):
    lr = jnp.minimum(lam_re, LAMBDA_RE_MAX)
    li = lam_im
    dt = jnp.exp(log_dt)[:, None]
    mag = jnp.exp(lr * dt)
    ar, ai = mag * jnp.cos(li * dt), mag * jnp.sin(li * dt)
    den = lr * lr + li * li
    nr, ni = ar - 1.0, ai
    gr, gi = (nr * lr + ni * li) / den, (ni * lr - nr * li) / den
    bbr = gr[:, :, None] * b_re - gi[:, :, None] * b_im
    bbi = gr[:, :, None] * b_im + gi[:, :, None] * b_re
    return ar, ai, bbr, bbi


def _pair_lanes(t):
    return t.reshape(N_PAIR, 2 * SSM_STATE)


def _chan_state_blocks(t_gcp):
    t = t_gcp.reshape(N_PAIR, 2, SSM_GROUP, SSM_STATE)
    eye2 = jnp.eye(2, dtype=t.dtype)
    blk = jnp.einsum("rgcp,gh->rgchp", t, eye2).reshape(N_PAIR, 2 * SSM_GROUP, 2 * SSM_STATE)
    place = jax.nn.one_hot(jnp.arange(N_PAIR) % PAIRS_PER_CHUNK, PAIRS_PER_CHUNK, dtype=t.dtype)
    return jnp.einsum("rcl,rj->rjcl", blk, place).reshape(N_PAIR, 128, 2 * SSM_STATE)


def _chan_state_unblock(t):
    t = t.reshape(N_PAIR, PAIRS_PER_CHUNK, 2, SSM_GROUP, 2, SSM_STATE)
    place = jax.nn.one_hot(jnp.arange(N_PAIR) % PAIRS_PER_CHUNK, PAIRS_PER_CHUNK, dtype=t.dtype)
    t = jnp.einsum("rjgchp,rj->rgchp", t, place)
    t = jnp.einsum("rgchp,gh->rgcp", t, jnp.eye(2, dtype=t.dtype))
    return t.reshape(SSM_NG, SSM_GROUP, SSM_STATE)


def _scan_tables(zr, zi, reverse):
    zr, zi = _pair_lanes(zr), _pair_lanes(-zi if reverse else zi)
    k = jnp.arange(1, SEG + 1, dtype=F32)
    k = (k[::-1] if reverse else k)[None, :, None]
    mag = jnp.exp(k * zr[:, None, :])
    pw = jnp.stack([mag * jnp.cos(k * zi[:, None, :]), mag * jnp.sin(k * zi[:, None, :])], axis=1)
    first, last = (SEG - 1, 0) if reverse else (0, SEG - 1)
    big = [(pw[:, 0, last], pw[:, 1, last])]
    for _ in range(2):
        br, bi = big[-1]
        big.append((br * br - bi * bi, 2.0 * br * bi))
    rows = jnp.arange(8)[None, :, None]
    tiles = [jnp.broadcast_to(pw[:, comp, first][:, None, :], (N_PAIR, 8, 128)) for comp in (0, 1)]
    for lvl, step in enumerate((1, 2, 4)):
        keep = (rows <= 7 - step) if reverse else (rows >= step)
        for part in big[lvl]:
            tiles.append(jnp.where(keep, part[:, None, :], 0.0))
    return jnp.stack(tiles, axis=1), jnp.broadcast_to(pw[:, :, :, None, :], (N_PAIR, 2, SEG, 8, 128))


def _pairs_to_chunks(t):
    n_chunk = N_PAIR // PAIRS_PER_CHUNK
    return jnp.swapaxes(t.reshape(n_chunk, PAIRS_PER_CHUNK, 128, 256), 1, 2).reshape(n_chunk, 128, XW)


def _chunks_to_pairs(t):
    n_chunk = N_PAIR // PAIRS_PER_CHUNK
    return jnp.swapaxes(t.reshape(n_chunk, 128, PAIRS_PER_CHUNK, 256), 1, 2).reshape(N_PAIR, 128, 256)


def _ssm_operands(w):
    ar, ai, bbr, bbi = _ssm_discretize(w["ssm_lambda_re"], w["ssm_lambda_im"], w["ssm_log_dt"], w["ssm_b_re"], w["ssm_b_im"])
    b_blk = jnp.concatenate([_chan_state_blocks(jnp.swapaxes(bbr, 1, 2)), _chan_state_blocks(jnp.swapaxes(bbi, 1, 2))], axis=2)
    c_blk = jnp.concatenate([_chan_state_blocks(w["ssm_c_re"]), -_chan_state_blocks(w["ssm_c_im"])], axis=2)
    dt = jnp.exp(w["ssm_log_dt"])[:, None]
    zr, zi = jnp.minimum(w["ssm_lambda_re"], LAMBDA_RE_MAX) * dt, w["ssm_lambda_im"] * dt
    b_cat, c_cat = _pairs_to_chunks(b_blk).astype(BF16), _pairs_to_chunks(c_blk).astype(BF16)
    return (b_cat, jnp.swapaxes(b_cat, 1, 2), c_cat, jnp.swapaxes(c_cat, 1, 2),
            *_scan_tables(zr, zi, False), *_scan_tables(zr, zi, True))


def _local_step(x, target, w, late_weights, on_grads):
    n_ex, seq, _ = x.shape
    lp = seq + BLOCK
    nb = lp // BLOCK
    n_rows = n_ex * lp
    g = {}

    head = jnp.concatenate([jnp.zeros((PAD, D_MODEL), F32), w["meta_tokens"]], axis=0)
    h0 = jnp.concatenate([jnp.broadcast_to(head[None], (n_ex, BLOCK, D_MODEL)), x], axis=1).reshape(n_rows, D_MODEL)

    qkv, hn_a = _rms_mm_cols(h0, w["attn_norm_w"], w["attn_w_qkv"], "qkv_fwd")
    att, lse = _attn_fwd(qkv, w["attn_sinks"], n_ex, nb)
    h1 = _mm_acc(att, w["attn_w_o"], False, "attn_out_fwd", res=h0)
    w = {**w, **late_weights(0, att)}
    h2, a0, hn_m0, u = _mlp_fwd(h1, w["mlp_norm_w"][0:1], w["mlp_w_up"][0], w["mlp_w_down"][0], "mlp0_fwd",
                                next_norm=w["ssm_norm_w"])
    late = late_weights(1, h2)
    w["ssm_w_glu"] = late["ssm_w_glu"]
    w["mlp_w_up"], w["mlp_w_down"] = w["mlp_w_up"] + late["mlp_w_up"], w["mlp_w_down"] + late["mlp_w_down"]

    ops = w["ssm_operands"] if "ssm_operands" in w else _ssm_operands(w)
    b_pad, bt_pad, ct_pad, c_pad, tab_fwd, pw_fwd, tab_rev, pw_rev = ops
    yg, y, xs = _ssm_fwd(u, b_pad, c_pad, tab_fwd, pw_fwd, w["ssm_d"], n_ex, lp)
    z = _mm_cols(yg, w["ssm_w_glu"], False, "glu_mm_fwd")
    h4, a1, hn_m1, h3 = _mlp_fwd(h2, w["mlp_norm_w"][1:2], w["mlp_w_up"][1], w["mlp_w_down"][1], "mlp1_fwd", glu_z=z)

    dh4, loss_tile, dnorm_f = _loss_head(h4, w["final_norm_w"], target.reshape(n_ex * seq, D_MODEL), n_ex, nb)

    def mlp_bwd(dh_out, h_in, a, hn, layer, tag, norm_w):
        dhn, dw_up, dw_down = None, None, None
        for s in range(N_CHIPS):
            final = s == N_CHIPS - 1
            res = _mlp_bwd_shard(s, dh_out, a, hn, dhn, h_in if final else None, norm_w,
                                 w["mlp_w_up"][layer], w["mlp_w_down"][layer], dw_up, dw_down, f"{tag}_bwd{s}")
            dhn, dw_up, dw_down = res[:3]
        return dhn, res[3], dw_up, dw_down

    dh3, dnorm_m1, dwu1, dwd1 = mlp_bwd(dh4, h3, a1, hn_m1, 1, "mlp1", w["mlp_norm_w"][1:2])
    tok = on_grads("mlp1", {"mlp_w_up": dwu1, "mlp_w_down": dwd1})
    dz, dyg = _glu_bwd(dh3, z, w["ssm_w_glu"])
    g["ssm_w_glu"] = _mm_tn(yg, dz, N_CHIPS, False, "glu_mm_dw")
    du, db_blk, dc_blk, da_t, dd_t = _ssm_bwd(dyg, y, u, xs, ct_pad, bt_pad, tab_rev, pw_rev, w["ssm_d"] + tok, n_ex, lp)
    dh2, dnorm_s = _rms_bwd_call(du, h2, w["ssm_norm_w"], dh3, "ssm_norm_bwd")
    db_blk, dc_blk = _chunks_to_pairs(db_blk), _chunks_to_pairs(dc_blk)
    g["ssm_c_re"] = _chan_state_unblock(dc_blk[:, :, 0:128])
    g["ssm_c_im"] = -_chan_state_unblock(dc_blk[:, :, 128:256])
    g_bbr = jnp.swapaxes(_chan_state_unblock(db_blk[:, :, 0:128]), 1, 2)
    g_bbi = jnp.swapaxes(_chan_state_unblock(db_blk[:, :, 128:256]), 1, 2)
    g_a = jnp.sum(da_t, axis=2).reshape(N_PAIR, 2, 2, SSM_STATE)
    g_ar, g_ai = g_a[:, 0].reshape(SSM_NG, SSM_STATE), g_a[:, 1].reshape(SSM_NG, SSM_STATE)
    _, vjp = jax.vjp(_ssm_discretize, w["ssm_lambda_re"], w["ssm_lambda_im"], w["ssm_log_dt"], w["ssm_b_re"], w["ssm_b_im"])
    g["ssm_lambda_re"], g["ssm_lambda_im"], g["ssm_log_dt"], g["ssm_b_re"], g["ssm_b_im"] = vjp((g_ar, g_ai, g_bbr, g_bbi))
    tok = on_grads("ssm", g)
    g = {}
    dh1, dnorm_m0, dwu0, dwd0 = mlp_bwd(dh2, h1, a0, hn_m0, 0, "mlp0", w["mlp_norm_w"][0:1] + tok)
    datt = _mm_cols(dh1, w["attn_w_o"], True, "attn_out_dx")
    dw_o = _mm_tn(att, dh1, N_CHIPS, True, "attn_out_dw")
    tok = on_grads("mlp0", {"mlp_w_up": dwu0, "mlp_w_down": dwd0, "attn_w_o": dw_o})
    dqkv, dsink_rows = _attn_bwd(qkv, w["attn_sinks"] + tok, att, lse, datt, n_ex, nb)
    tok = on_grads("qkv", {"attn_w_qkv": _mm_tn(hn_a, dqkv, N_CHIPS, False, "qkv_dw")})
    dh0, dnorm_a = _mm_acc(dqkv, w["attn_w_qkv"], True, "qkv_dx", rms_bwd=(h0, w["attn_norm_w"] + tok, dh1))

    dh0 = dh0.reshape(n_ex, lp, D_MODEL)
    on_grads("rest", {
        "mlp_norm_w": jnp.stack([jnp.sum(dnorm_m0, axis=0), jnp.sum(dnorm_m1, axis=0)]),
        "final_norm_w": jnp.sum(dnorm_f, axis=0),
        "attn_norm_w": jnp.sum(dnorm_a, axis=0)[None],
        "ssm_norm_w": jnp.sum(dnorm_s, axis=0)[None],
        "attn_sinks": jnp.sum(dsink_rows, axis=0)[None],
        "ssm_d": jnp.sum(dd_t, axis=0)[None],
        "meta_tokens": jnp.sum(dh0[:, PAD:BLOCK], axis=0),
        "loss": loss_tile[0, 0:1]})
    return loss_tile, dh0[:, BLOCK:]


_SHARDED_SMALL = ("meta_tokens", "ssm_norm_w", "ssm_d")
_REP_SSM = ("ssm_lambda_re", "ssm_lambda_im", "ssm_log_dt", "ssm_b_re", "ssm_b_im", "ssm_c_re", "ssm_c_im")
_REP_MISC = ("attn_norm_w", "attn_sinks", "mlp_norm_w", "final_norm_w")
_BIG = ("attn_w_qkv", "attn_w_o", "ssm_w_glu", "mlp_w_up", "mlp_w_down")


def _pack(parts, cols):
    flat = jnp.concatenate([p.reshape(-1) for p in parts])
    rows = -(-flat.shape[0] // (8 * cols)) * 8
    return jnp.pad(flat, (0, rows * cols - flat.shape[0])).reshape(rows, cols)


def _unpack(packed, like):
    flat = packed.reshape(-1)
    out, at = [], 0
    for p in like:
        out.append(flat[at:at + p.size].reshape(p.shape))
        at += p.size
    return out


def kernel(x, meta_tokens, attn_norm_w, attn_w_qkv, attn_sinks, attn_w_o, ssm_norm_w, ssm_lambda_re, ssm_lambda_im, ssm_log_dt, ssm_b_re, ssm_b_im, ssm_c_re, ssm_c_im, ssm_d, ssm_w_glu, mlp_norm_w, mlp_w_up, mlp_w_down, final_norm_w, loss_target, m_meta_tokens, m_attn_norm_w, m_attn_w_qkv, m_attn_sinks, m_attn_w_o, m_ssm_norm_w, m_ssm_lambda_re, m_ssm_lambda_im, m_ssm_log_dt, m_ssm_b_re, m_ssm_b_im, m_ssm_c_re, m_ssm_c_im, m_ssm_d, m_ssm_w_glu, m_mlp_norm_w, m_mlp_w_up, m_mlp_w_down, m_final_norm_w, v_meta_tokens, v_attn_norm_w, v_attn_w_qkv, v_attn_sinks, v_attn_w_o, v_ssm_norm_w, v_ssm_lambda_re, v_ssm_lambda_im, v_ssm_log_dt, v_ssm_b_re, v_ssm_b_im, v_ssm_c_re, v_ssm_c_im, v_ssm_d, v_ssm_w_glu, v_mlp_norm_w, v_mlp_w_up, v_mlp_w_down, v_final_norm_w):
    names = ("meta_tokens", "attn_norm_w", "attn_w_qkv", "attn_sinks", "attn_w_o", "ssm_norm_w", "ssm_lambda_re",
             "ssm_lambda_im", "ssm_log_dt", "ssm_b_re", "ssm_b_im", "ssm_c_re", "ssm_c_im", "ssm_d", "ssm_w_glu",
             "mlp_norm_w", "mlp_w_up", "mlp_w_down", "final_norm_w")
    wts = dict(zip(names, (meta_tokens, attn_norm_w, attn_w_qkv, attn_sinks, attn_w_o, ssm_norm_w, ssm_lambda_re,
                           ssm_lambda_im, ssm_log_dt, ssm_b_re, ssm_b_im, ssm_c_re, ssm_c_im, ssm_d, ssm_w_glu,
                           mlp_norm_w, mlp_w_up, mlp_w_down, final_norm_w)))
    mom = dict(zip(names, (m_meta_tokens, m_attn_norm_w, m_attn_w_qkv, m_attn_sinks, m_attn_w_o, m_ssm_norm_w,
                           m_ssm_lambda_re, m_ssm_lambda_im, m_ssm_log_dt, m_ssm_b_re, m_ssm_b_im, m_ssm_c_re,
                           m_ssm_c_im, m_ssm_d, m_ssm_w_glu, m_mlp_norm_w, m_mlp_w_up, m_mlp_w_down, m_final_norm_w)))
    var = dict(zip(names, (v_meta_tokens, v_attn_norm_w, v_attn_w_qkv, v_attn_sinks, v_attn_w_o, v_ssm_norm_w,
                           v_ssm_lambda_re, v_ssm_lambda_im, v_ssm_log_dt, v_ssm_b_re, v_ssm_b_im, v_ssm_c_re,
                           v_ssm_c_im, v_ssm_d, v_ssm_w_glu, v_mlp_norm_w, v_mlp_w_up, v_mlp_w_down, v_final_norm_w)))

    my_chip = 2 * lax.axis_index("x") + lax.axis_index("y")
    my_dev = 2 * my_chip + lax.axis_index("c")
    small_mine = _pack([wts[n] for n in _SHARDED_SMALL], 128)
    first = [attn_w_qkv.astype(BF16), attn_w_o.astype(BF16), small_mine]
    up16, down16 = mlp_w_up.astype(BF16), mlp_w_down.astype(BF16)
    mlp0 = [up16[0:1], down16[0:1]]
    rest = [ssm_w_glu.astype(BF16), up16[1:2], down16[1:2]]
    handles, _ = _split_start([(srcs, [_landing(a, my_chip, N_CHIPS) for a in srcs]) for srcs in (first, mlp0, rest)],
                              _gather_copies, N_GATHER_PEERS, "gather_start")
    full = {n: wts[n] for n in _REP_MISC}
    full["final_norm_w"] = final_norm_w[None]
    for n in _REP_SSM:
        full[n] = wts[n][0]
    full["ssm_operands"] = _ssm_operands(full)
    got = _split_wait(handles[0], full["ssm_operands"], _gather_copies, "gather_wait_first")
    full["attn_w_qkv"], full["attn_w_o"] = got[0], got[1]
    smalls = [_unpack(got[2][s], [wts[n] for n in _SHARDED_SMALL]) for s in range(N_CHIPS)]
    for k, n in enumerate(_SHARDED_SMALL):
        full[n] = jnp.concatenate([smalls[s][k] for s in range(N_CHIPS)], axis=1)

    def late_weights(stage, after):
        if stage == 0:
            up, down = _split_wait(handles[1], after, _gather_copies, "gather_wait_mlp0")
            return {"mlp_w_up": [up], "mlp_w_down": [down]}
        glu, up, down = _split_wait(handles[2], after, _gather_copies, "gather_wait_rest")
        return {"ssm_w_glu": glu, "mlp_w_up": [up], "mlp_w_down": [down]}

    def shard_cols(t):
        return jnp.swapaxes(t.reshape(t.shape[0], N_CHIPS, t.shape[1] // N_CHIPS), 0, 1)

    pending = {}

    def on_grads(tag, g):
        scatter = [g[n] for n in _BIG if n in g]
        whole = []
        if tag == "ssm":
            whole = [_pack([g[n] for n in _REP_SSM], D_MODEL)]
        if tag == "rest":
            parts = [shard_cols(g[n]) for n in _SHARDED_SMALL]
            scatter = [jnp.stack([_pack([p[s] for p in parts], 128) for s in range(N_CHIPS)])]
            whole = [_pack([g[n] for n in _REP_MISC] + [g["loss"]], D_MODEL)]
        srcs = scatter + whole
        lands = [_landing(lax.dynamic_index_in_dim(a, my_chip, 0, keepdims=False), my_dev, N_DEV) for a in scatter]
        lands += [_landing(a, my_dev, N_DEV) for a in whole]
        hs, token = _split_start([(srcs, lands)], _exchange_copies(len(scatter)), N_EXCHANGE_PEERS, "exchange_start_" + tag)
        pending[tag] = (hs[0], len(scatter))
        return token[0, 0]

    _, grad_x = _local_step(x, loss_target, full, late_weights, on_grads)

    recv = {}
    for tag, (handle, n_scatter) in pending.items():
        recv[tag] = _split_wait(handle, grad_x, _exchange_copies(n_scatter), "exchange_wait_" + tag)
    loss = jnp.sum(recv["rest"][1].reshape(N_DEV, -1)[:, sum(wts[n].size for n in _REP_MISC)])

    out = {}

    def update(tag, pieces, w2, m2, v2):
        return _adamw(pieces, w2, m2, v2, "adamw_" + tag)

    def update_weight(n, pieces):
        shp = wts[n].shape
        r2 = (math.prod(shp[:-1]), shp[-1])
        res = update(n, pieces, wts[n].reshape(r2), mom[n].reshape(r2), var[n].reshape(r2))
        out[n] = [t.reshape(shp) for t in res]

    update_weight("mlp_w_up", [recv["mlp0"][1], recv["mlp1"][0]])
    update_weight("mlp_w_down", [recv["mlp0"][2], recv["mlp1"][1]])
    update_weight("attn_w_o", [recv["mlp0"][0]])
    update_weight("ssm_w_glu", [recv["ssm"][0]])
    update_weight("attn_w_qkv", [recv["qkv"][0]])
    for tag, group, pieces, cols in (("small", _SHARDED_SMALL, recv["rest"][0], 128),
                                     ("rep_ssm", _REP_SSM, recv["ssm"][1], D_MODEL),
                                     ("rep_misc", _REP_MISC, recv["rest"][1], D_MODEL)):
        like = [wts[n] for n in group]
        res = update(tag, [pieces], _pack(like, cols), _pack([mom[n] for n in group], cols),
                     _pack([var[n] for n in group], cols))
        for k, n in enumerate(group):
            out[n] = [_unpack(t, like)[k] for t in res]

    return (loss, grad_x, *[out[n][0] for n in names], *[out[n][1] for n in names],
            *[out[n][2] for n in names], *[out[n][3] for n in names])
```

```python
import functools
import math

import jax
import jax.numpy as jnp
from jax import lax
from jax.experimental import pallas as pl
from jax.experimental.pallas import tpu as pltpu

F32 = jnp.float32
BF16 = jnp.bfloat16
SDS = jax.ShapeDtypeStruct

D_MODEL = 1024
N_HEADS = 16
N_KV = 4
GQA = N_HEADS // N_KV
HEAD_DIM = 64
BLOCK = 128
N_META = 16
PAD = BLOCK - N_META
QKV_DIM = (N_HEADS + 2 * N_KV) * HEAD_DIM
KV_DIM = 2 * N_KV * HEAD_DIM
D_FF = 4 * D_MODEL
N_CHIPS = 4
N_DEV = 8
SSM_GROUP = 16
SSM_NG = D_MODEL // SSM_GROUP
SSM_STATE = 64
N_PAIR = SSM_NG // 2
PAIRS_PER_CHUNK = 4
RMS_EPS = 1e-6
NEG_INF = -1e30
LAMBDA_RE_MAX = -1e-4
ADAM_LR, ADAM_B1, ADAM_B2, ADAM_EPS, ADAM_WD, ADAM_STEP = 0.001, 0.9, 0.999, 1e-08, 0.01, 10

TM = 384
MM_TILES = (768, 384)
MLP_FWD_TILES = (384,)
MLP_BWD_TILES = (768, 384)
TN_TILES = (1408, 768, 384)
VMEM_LIMIT = 56 * 1024 * 1024


def _params(n_grid):
    return pltpu.CompilerParams(dimension_semantics=("arbitrary",) * n_grid, vmem_limit_bytes=VMEM_LIMIT)


def _row_tile(n_rows, tiles):
    return next(t for t in tiles if n_rows % t == 0)


def _rms(h, w):
    r = lax.rsqrt(jnp.mean(h * h, axis=-1, keepdims=True) + RMS_EPS)
    return h * r * w


def _rms_bwd(dhn, h, w):
    r = lax.rsqrt(jnp.mean(h * h, axis=-1, keepdims=True) + RMS_EPS)
    g = dhn * w
    proj = jnp.sum(g * h, axis=-1, keepdims=True) * (1.0 / D_MODEL)
    return r * g - h * (r * r * r) * proj, dhn * h * r


def _fold8(t):
    return jnp.sum(t.reshape(t.shape[0] // 8, 8, t.shape[1]), axis=0)


def _gelu(y):
    return 0.5 * y * (1.0 + jnp.tanh(0.7978845608028654 * (y + 0.044715 * y * y * y)))


def _gelu_grad(y):
    t = jnp.tanh(0.7978845608028654 * (y + 0.044715 * y * y * y))
    return 0.5 * (1.0 + t) + 0.5 * y * (1.0 - t * t) * 0.7978845608028654 * (1.0 + 3.0 * 0.044715 * y * y)


def _w4_spec(w4):
    n_sh, _, k, n = w4.shape
    return pl.BlockSpec((n_sh, None, k, n), lambda i: (0, 0, 0, 0))


def _rms_mm_cols(h, wn, w4, name):
    n_rows = h.shape[0]
    n_sh, _, k, n = w4.shape
    tm = _row_tile(n_rows, MM_TILES)

    def body(h_ref, wn_ref, w_ref, o_ref, hn_ref):
        hn = _rms(h_ref[...], wn_ref[...]).astype(BF16)
        hn_ref[...] = hn
        for s in range(n_sh):
            o_ref[:, s * n:(s + 1) * n] = jnp.dot(hn, w_ref[s], preferred_element_type=F32).astype(o_ref.dtype)

    return pl.pallas_call(
        body, name=name, grid=(n_rows // tm,),
        in_specs=[pl.BlockSpec((tm, k), lambda i: (i, 0)), pl.BlockSpec((1, k), lambda i: (0, 0)), _w4_spec(w4)],
        out_specs=[pl.BlockSpec((tm, n_sh * n), lambda i: (i, 0)), pl.BlockSpec((tm, k), lambda i: (i, 0))],
        out_shape=[SDS((n_rows, n_sh * n), BF16), SDS((n_rows, k), BF16)],
        compiler_params=_params(1),
    )(h, wn, w4)


def _mm_cols(x, w4, trans_w, name):
    n_rows, kx = x.shape
    tm = _row_tile(n_rows, MM_TILES)
    n_sh, _, k, n = w4.shape
    n_out = k if trans_w else n
    dims = (((1,), (1,)), ((), ())) if trans_w else (((1,), (0,)), ((), ()))

    def body(x_ref, w_ref, o_ref):
        x16 = x_ref[...].astype(BF16)
        for s in range(n_sh):
            o_ref[:, s * n_out:(s + 1) * n_out] = lax.dot_general(
                x16, w_ref[s], dims, preferred_element_type=F32).astype(o_ref.dtype)

    return pl.pallas_call(
        body, name=name, grid=(n_rows // tm,),
        in_specs=[pl.BlockSpec((tm, kx), lambda i: (i, 0)), _w4_spec(w4)],
        out_specs=pl.BlockSpec((tm, n_sh * n_out), lambda i: (i, 0)),
        out_shape=SDS((n_rows, n_sh * n_out), BF16),
        compiler_params=_params(1),
    )(x, w4)


def _mm_acc(x, w4, trans_w, name, res=None, rms_bwd=None, out_dtype=F32):
    n_rows = x.shape[0]
    tm = _row_tile(n_rows, MM_TILES)
    n_sh, _, k, n = w4.shape
    kx, n_out = (n, k) if trans_w else (k, n)
    dims = (((1,), (1,)), ((), ())) if trans_w else (((1,), (0,)), ((), ()))

    def body(*refs):
        if rms_bwd is not None:
            x_ref, w_ref, h_ref, wn_ref, dres_ref, o_ref, dw_ref = refs
        elif res is not None:
            x_ref, w_ref, res_ref, o_ref = refs
        else:
            x_ref, w_ref, o_ref = refs
        acc = None
        for s in range(n_sh):
            part = lax.dot_general(x_ref[:, s * kx:(s + 1) * kx].astype(BF16), w_ref[s], dims, preferred_element_type=F32)
            acc = part if acc is None else acc + part
        if rms_bwd is not None:
            dh, dw_rows = _rms_bwd(acc, h_ref[...], wn_ref[...])
            o_ref[...] = (dres_ref[...] + dh).astype(o_ref.dtype)

            @pl.when(pl.program_id(0) == 0)
            def _():
                dw_ref[...] = jnp.zeros_like(dw_ref)

            dw_ref[...] += _fold8(dw_rows)
        elif res is not None:
            o_ref[...] = (res_ref[...] + acc).astype(o_ref.dtype)
        else:
            o_ref[...] = acc.astype(o_ref.dtype)

    row = lambda i: (i, 0)
    in_specs = [pl.BlockSpec((tm, n_sh * kx), row), _w4_spec(w4)]
    args = [x, w4]
    out_specs = pl.BlockSpec((tm, n_out), row)
    out_shape = SDS((n_rows, n_out), out_dtype)
    if rms_bwd is not None:
        h, wn, dres = rms_bwd
        in_specs += [pl.BlockSpec((tm, n_out), row), pl.BlockSpec((1, n_out), lambda i: (0, 0)),
                     pl.BlockSpec((tm, n_out), row)]
        args += [h, wn, dres]
        out_specs = [out_specs, pl.BlockSpec((8, n_out), lambda i: (0, 0))]
        out_shape = [out_shape, SDS((8, n_out), F32)]
    elif res is not None:
        in_specs.append(pl.BlockSpec((tm, n_out), row))
        args.append(res)
    return pl.pallas_call(
        body, name=name, grid=(n_rows // tm,), in_specs=in_specs, out_specs=out_specs, out_shape=out_shape,
        compiler_params=_params(1),
    )(*args)


def _mm_tn(a, b, n_sh, a_sharded, name):
    n_rows = a.shape[0]
    tm = _row_tile(n_rows, TN_TILES)
    ka = a.shape[1] // n_sh if a_sharded else a.shape[1]
    nb = b.shape[1] if a_sharded else b.shape[1] // n_sh
    n_i = n_rows // tm

    def body(a_ref, b_ref, o_ref, acc):
        i = pl.program_id(0)

        @pl.when(i == 0)
        def _():
            acc[...] = jnp.zeros_like(acc)

        for s in range(n_sh):
            a_s = a_ref[:, s * ka:(s + 1) * ka] if a_sharded else a_ref[...]
            b_s = b_ref[...] if a_sharded else b_ref[:, s * nb:(s + 1) * nb]
            acc[s] += lax.dot_general(a_s.astype(BF16), b_s.astype(BF16), (((0,), (0,)), ((), ())),
                                      preferred_element_type=F32)

        @pl.when(i == n_i - 1)
        def _():
            o_ref[...] = acc[...].astype(o_ref.dtype)

    return pl.pallas_call(
        body, name=name, grid=(n_i,),
        in_specs=[pl.BlockSpec((tm, a.shape[1]), lambda i: (i, 0)), pl.BlockSpec((tm, b.shape[1]), lambda i: (i, 0))],
        out_specs=pl.BlockSpec((n_sh, ka, nb), lambda i: (0, 0, 0)),
        out_shape=SDS((n_sh, ka, nb), BF16),
        scratch_shapes=[pltpu.VMEM((n_sh, ka, nb), F32)], compiler_params=_params(1),
    )(a, b)


def _mlp_fwd(h, wn, w_up4, w_down4, name, next_norm=None, glu_z=None):
    n_rows = h.shape[0]
    tm = _row_tile(n_rows, MLP_FWD_TILES)
    n_sh = w_up4.shape[0]
    f_sh = D_FF // n_sh
    w_down = w_down4.reshape(D_FF, D_MODEL)

    def body(*refs):
        refs = list(refs)
        h_ref, wn_ref, wu_ref, wd_ref = refs[:4]
        at = 4
        if next_norm is not None:
            nn_ref = refs[at]
            at += 1
        if glu_z is not None:
            z_ref = refs[at]
            at += 1
        o_ref, a_ref, hn_ref = refs[at:at + 3]
        at += 3
        if next_norm is not None:
            u_ref = refs[at]
            at += 1
        if glu_z is not None:
            hin_ref = refs[at]
            at += 1
        act_s = refs[at]
        h_in = h_ref[...]
        if glu_z is not None:
            h_in = h_in + z_ref[:, 0:D_MODEL].astype(F32) * jax.nn.sigmoid(z_ref[:, D_MODEL:2 * D_MODEL].astype(F32))
            hin_ref[...] = h_in
        hn = _rms(h_in, wn_ref[...]).astype(BF16)
        hn_ref[...] = hn
        for s in range(n_sh):
            cols = slice(s * f_sh, (s + 1) * f_sh)
            a = jnp.dot(hn, wu_ref[s], preferred_element_type=F32)
            a_ref[:, cols] = a.astype(BF16)
            act = jnp.maximum(a, 0.0)
            act_s[:, cols] = (act * act).astype(BF16)
        out = h_in + jnp.dot(act_s[...], wd_ref[...], preferred_element_type=F32)
        o_ref[...] = out
        if next_norm is not None:
            u_ref[...] = _rms(out, nn_ref[...])

    row = lambda i: (i, 0)
    vec = pl.BlockSpec((1, D_MODEL), lambda i: (0, 0))
    in_specs = [pl.BlockSpec((tm, D_MODEL), row), vec,
                pl.BlockSpec((n_sh, None, D_MODEL, f_sh), lambda i: (0, 0, 0, 0), pipeline_mode=pl.Buffered(1)),
                pl.BlockSpec((D_FF, D_MODEL), lambda i: (0, 0), pipeline_mode=pl.Buffered(1))]
    out_specs = [pl.BlockSpec((tm, D_MODEL), row), pl.BlockSpec((tm, D_FF), row), pl.BlockSpec((tm, D_MODEL), row)]
    out_shape = [SDS((n_rows, D_MODEL), F32), SDS((n_rows, D_FF), BF16), SDS((n_rows, D_MODEL), BF16)]
    args = [h, wn, w_up4, w_down]
    if next_norm is not None:
        in_specs.append(vec)
        args.append(next_norm)
    if glu_z is not None:
        in_specs.append(pl.BlockSpec((tm, 2 * D_MODEL), row))
        args.append(glu_z)
    for extra in (next_norm, glu_z):
        if extra is not None:
            out_specs.append(pl.BlockSpec((tm, D_MODEL), row))
            out_shape.append(SDS((n_rows, D_MODEL), F32))
    return pl.pallas_call(
        body, name=name, grid=(n_rows // tm,), in_specs=in_specs, out_specs=out_specs, out_shape=out_shape,
        scratch_shapes=[pltpu.VMEM((tm, D_FF), BF16)],
        compiler_params=_params(1),
    )(*args)


def _mlp_bwd_shard(s, dh, a, hn, dhn_prev, h, wn, w_up4, w_down4, dw_up_buf, dw_down_buf, name):
    n_rows = dh.shape[0]
    n_sh = w_up4.shape[0]
    f_sh = D_FF // n_sh
    tm = _row_tile(n_rows, MLP_BWD_TILES)
    n_i = n_rows // tm
    last = h is not None
    nt = (((1,), (1,)), ((), ()))
    tn = (((0,), (0,)), ((), ()))

    def body(*refs):
        refs = list(refs)
        dh_ref, a_ref, hn_ref, wu_ref, wd_ref = refs[:5]
        at = 5
        prev_ref = None
        if dhn_prev is not None:
            prev_ref = refs[at]
            at += 1
        if last:
            h_ref, wn_ref = refs[at:at + 2]
            at += 2
        if dw_up_buf is not None:
            at += 2
        o_ref, dwu_ref, dwd_ref = refs[at:at + 3]
        at += 3
        if last:
            dnorm_ref = refs[at]
            at += 1
        acc_u, acc_d = refs[at:at + 2]
        i = pl.program_id(0)

        @pl.when(i == 0)
        def _():
            acc_u[...] = jnp.zeros_like(acc_u)
            acc_d[...] = jnp.zeros_like(acc_d)
            if last:
                dnorm_ref[...] = jnp.zeros_like(dnorm_ref)

        dh16 = dh_ref[...].astype(BF16)
        r = jnp.maximum(a_ref[...].astype(F32), 0.0)
        dact = lax.dot_general(dh16, wd_ref[...], nt, preferred_element_type=F32)
        da16 = (dact * (2.0 * r)).astype(BF16)
        acc_d[...] += lax.dot_general((r * r).astype(BF16), dh16, tn, preferred_element_type=F32)
        acc_u[...] += lax.dot_general(hn_ref[...], da16, tn, preferred_element_type=F32)
        dhn = lax.dot_general(da16, wu_ref[...], nt, preferred_element_type=F32)
        if prev_ref is not None:
            dhn = dhn + prev_ref[...]
        if last:
            d_rms, dw_rows = _rms_bwd(dhn, h_ref[...], wn_ref[...])
            o_ref[...] = dh_ref[...] + d_rms
            dnorm_ref[...] += _fold8(dw_rows)
        else:
            o_ref[...] = dhn

        @pl.when(i == n_i - 1)
        def _():
            dwu_ref[...] = acc_u[...].astype(BF16)
            dwd_ref[...] = acc_d[...].astype(BF16)

    row = lambda i: (i, 0)
    tile = pl.BlockSpec((tm, D_MODEL), row)
    in_specs = [tile, pl.BlockSpec((tm, f_sh), lambda i: (i, s)), tile,
                pl.BlockSpec((None, None, D_MODEL, f_sh), lambda i: (s, 0, 0, 0)),
                pl.BlockSpec((None, None, f_sh, D_MODEL), lambda i: (s, 0, 0, 0))]
    args = [dh, a, hn, w_up4, w_down4]
    if dhn_prev is not None:
        in_specs.append(tile)
        args.append(dhn_prev)
    if last:
        in_specs += [tile, pl.BlockSpec((1, D_MODEL), lambda i: (0, 0))]
        args += [h, wn]
    aliases = {}
    if dw_up_buf is not None:
        aliases = {len(args): 1, len(args) + 1: 2}
        in_specs += [pl.BlockSpec(memory_space=pl.ANY)] * 2
        args += [dw_up_buf, dw_down_buf]
    out_specs = [tile, pl.BlockSpec((None, D_MODEL, f_sh), lambda i: (s, 0, 0)),
                 pl.BlockSpec((None, f_sh, D_MODEL), lambda i: (s, 0, 0))]
    out_shape = [SDS((n_rows, D_MODEL), F32), SDS((n_sh, D_MODEL, f_sh), BF16), SDS((n_sh, f_sh, D_MODEL), BF16)]
    if last:
        out_specs.append(pl.BlockSpec((8, D_MODEL), lambda i: (0, 0)))
        out_shape.append(SDS((8, D_MODEL), F32))
    return pl.pallas_call(
        body, name=name, grid=(n_i,), in_specs=in_specs, out_specs=out_specs, out_shape=out_shape,
        input_output_aliases=aliases,
        scratch_shapes=[pltpu.VMEM((D_MODEL, f_sh), F32), pltpu.VMEM((f_sh, D_MODEL), F32)],
        compiler_params=_params(1),
    )(*args)


def _attn_masks(n):
    qi = lax.broadcasted_iota(jnp.int32, (BLOCK, 3 * BLOCK), 0)
    col = lax.broadcasted_iota(jnp.int32, (BLOCK, 3 * BLOCK), 1)
    kj = col - BLOCK
    dist = BLOCK + qi - kj
    kmin = jnp.where(n == 0, 2 * BLOCK, jnp.where(n == 1, BLOCK, 0))
    band_ok = (col >= BLOCK) & (dist >= 0) & (dist < BLOCK) & (kj >= kmin)
    q_pos = n * BLOCK + qi - PAD
    meta_ok = (col >= PAD) & (col < BLOCK) & (col - PAD <= q_pos)
    distf = jnp.where(col >= BLOCK, dist, 0).astype(F32)
    return band_ok | meta_ok, distf


def _alibi_slope(h):
    return float(2.0 ** (-8.0 * (h + 1) / N_HEADS))


def _attn_bias(n, bias_s):
    ok, distf = _attn_masks(n)
    for h in range(N_HEADS):
        bias_s[h] = jnp.where(ok, -_alibi_slope(h) * distf, NEG_INF)


def _attn_fwd(qkv, sinks, n_ex, nb):
    n_rows = qkv.shape[0]
    kvb = N_HEADS * HEAD_DIM // KV_DIM

    def body(sink_ref, q_ref, kvm_ref, kvp_ref, kvc_ref, o_ref, lse_ref, k_s, v_s, q_s, bias_s):
        n = pl.program_id(1)

        @pl.when(n <= 2)
        def _():
            _attn_bias(n, bias_s)

        v_s[...] = jnp.ones_like(v_s)
        for part, ref in enumerate((kvm_ref, kvp_ref, kvc_ref)):
            rows = slice(part * BLOCK, (part + 1) * BLOCK)
            k_s[rows, :] = ref[:, 0:N_KV * HEAD_DIM]
            for kv in range(N_KV):
                v_s[rows, kv * 2 * HEAD_DIM:kv * 2 * HEAD_DIM + HEAD_DIM] = \
                    ref[:, (N_KV + kv) * HEAD_DIM:(N_KV + kv + 1) * HEAD_DIM]
        for kv in range(N_KV):
            for g in range(GQA):
                h = kv * GQA + g
                q_s[kv, g * BLOCK:(g + 1) * BLOCK, :] = q_ref[:, h * HEAD_DIM:(h + 1) * HEAD_DIM] * (HEAD_DIM ** -0.5)
            s4 = lax.dot_general(q_s[kv], k_s[:, kv * HEAD_DIM:(kv + 1) * HEAD_DIM], (((1,), (1,)), ((), ())),
                                 preferred_element_type=F32)
            es, ms, sink_es = [], [], []
            for g in range(GQA):
                h = kv * GQA + g
                s = s4[g * BLOCK:(g + 1) * BLOCK] + bias_s[h]
                sink = sink_ref[0, h]
                m = jnp.maximum(jnp.max(s, axis=-1, keepdims=True), sink)
                es.append(jnp.exp(s - m).astype(BF16))
                ms.append(m)
                sink_es.append(jnp.exp(sink - m))
            pv = jnp.dot(jnp.concatenate(es, axis=0), v_s[:, kv * 2 * HEAD_DIM:(kv + 1) * 2 * HEAD_DIM],
                         preferred_element_type=F32)
            for g in range(GQA):
                h = kv * GQA + g
                pg = pv[g * BLOCK:(g + 1) * BLOCK]
                l = pg[:, HEAD_DIM:HEAD_DIM + 1] + sink_es[g]
                o_ref[:, h * HEAD_DIM:(h + 1) * HEAD_DIM] = (pg[:, 0:HEAD_DIM] * (1.0 / l)).astype(BF16)
                lse_ref[:, h:h + 1] = ms[g] + jnp.log(l)

    return pl.pallas_call(
        body, name="attn_fwd", grid=(n_ex, nb),
        in_specs=[pl.BlockSpec(memory_space=pltpu.SMEM),
                  pl.BlockSpec((BLOCK, N_HEADS * HEAD_DIM), lambda b, n: (b * nb + n, 0)),
                  pl.BlockSpec((BLOCK, KV_DIM), lambda b, n: (b * nb, kvb)),
                  pl.BlockSpec((BLOCK, KV_DIM), lambda b, n: (b * nb + jnp.maximum(n - 1, 0), kvb)),
                  pl.BlockSpec((BLOCK, KV_DIM), lambda b, n: (b * nb + n, kvb))],
        out_specs=[pl.BlockSpec((BLOCK, N_HEADS * HEAD_DIM), lambda b, n: (b * nb + n, 0)),
                   pl.BlockSpec((BLOCK, N_HEADS), lambda b, n: (b * nb + n, 0))],
        out_shape=[SDS((n_rows, N_HEADS * HEAD_DIM), BF16), SDS((n_rows, N_HEADS), F32)],
        scratch_shapes=[pltpu.VMEM((3 * BLOCK, N_KV * HEAD_DIM), BF16), pltpu.VMEM((3 * BLOCK, 2 * N_KV * HEAD_DIM), BF16),
                        pltpu.VMEM((N_KV, GQA * BLOCK, HEAD_DIM), BF16), pltpu.VMEM((N_HEADS, BLOCK, 3 * BLOCK), F32)],
        compiler_params=_params(2),
    )(sinks, qkv, qkv, qkv, qkv)


def _attn_bwd(qkv, sinks, o, lse, do, n_ex, nb):
    n_rows = qkv.shape[0]
    kvb = N_HEADS * HEAD_DIM // KV_DIM
    scale = HEAD_DIM ** -0.5
    nq = lambda r: nb - 1 - r

    def body(sink_ref, q_ref, kvm_ref, kvp_ref, kvc_ref, o_ref, lse_ref, do_ref, dqkv_ref, dsink_ref,
             k_s, v_s, dkv_s, carry_s, meta_s, q_s, do_s, bias_s):
        b, r = pl.program_id(0), pl.program_id(1)
        n = nq(r)

        @pl.when((r == 0) | (n <= 1))
        def _():
            _attn_bias(n, bias_s)

        @pl.when((b == 0) & (r == 0))
        def _():
            dsink_ref[...] = jnp.zeros_like(dsink_ref)

        @pl.when(r == 0)
        def _():
            carry_s[...] = jnp.zeros_like(carry_s)
            meta_s[...] = jnp.zeros_like(meta_s)

        for part, ref in enumerate((kvm_ref, kvp_ref, kvc_ref)):
            k_s[part * BLOCK:(part + 1) * BLOCK, :] = ref[:, 0:N_KV * HEAD_DIM]
            v_s[part * BLOCK:(part + 1) * BLOCK, :] = ref[:, N_KV * HEAD_DIM:KV_DIM]
        nt = (((1,), (1,)), ((), ()))
        tn = (((0,), (0,)), ((), ()))
        for kv in range(N_KV):
            kcols = slice(kv * HEAD_DIM, (kv + 1) * HEAD_DIM)
            vcols = slice(N_KV * HEAD_DIM + kv * HEAD_DIM, N_KV * HEAD_DIM + (kv + 1) * HEAD_DIM)
            for g in range(GQA):
                cols = slice((kv * GQA + g) * HEAD_DIM, (kv * GQA + g + 1) * HEAD_DIM)
                q_s[kv, g * BLOCK:(g + 1) * BLOCK, :] = q_ref[:, cols] * scale
                do_s[kv, g * BLOCK:(g + 1) * BLOCK, :] = do_ref[:, cols]
            kh, vh = k_s[:, kcols], v_s[:, kcols]
            s4 = lax.dot_general(q_s[kv], kh, nt, preferred_element_type=F32)
            dp4 = lax.dot_general(do_s[kv], vh, nt, preferred_element_type=F32)
            ps, dss = [], []
            for g in range(GQA):
                h = kv * GQA + g
                cols = slice(h * HEAD_DIM, (h + 1) * HEAD_DIM)
                rows = slice(g * BLOCK, (g + 1) * BLOCK)
                s = s4[rows] + bias_s[h]
                lse_h = lse_ref[:, h:h + 1]
                p = jnp.exp(s - lse_h)
                delta = jnp.sum(do_ref[:, cols].astype(F32) * o_ref[:, cols].astype(F32), axis=-1, keepdims=True)
                dsink_ref[:, h:h + 1] += -jnp.exp(sink_ref[0, h] - lse_h) * delta
                ps.append(p.astype(BF16))
                dss.append((p * (dp4[rows] - delta)).astype(BF16))
            p4, ds4 = jnp.concatenate(ps, axis=0), jnp.concatenate(dss, axis=0)
            dq4 = jnp.dot(ds4, kh, preferred_element_type=F32) * scale
            for g in range(GQA):
                cols = slice((kv * GQA + g) * HEAD_DIM, (kv * GQA + g + 1) * HEAD_DIM)
                dqkv_ref[:, cols] = dq4[g * BLOCK:(g + 1) * BLOCK].astype(BF16)
            dkv_s[:, kcols] = lax.dot_general(ds4, q_s[kv], tn, preferred_element_type=F32)
            dkv_s[:, vcols] = lax.dot_general(p4, do_s[kv], tn, preferred_element_type=F32)

        meta_s[...] += dkv_s[0:BLOCK, :]
        cur = dkv_s[2 * BLOCK:3 * BLOCK, :] + carry_s[...]
        carry_s[...] = dkv_s[BLOCK:2 * BLOCK, :]

        @pl.when(n > 0)
        def _():
            dqkv_ref[:, N_HEADS * HEAD_DIM:QKV_DIM] = cur.astype(BF16)

        @pl.when(n == 0)
        def _():
            dqkv_ref[:, N_HEADS * HEAD_DIM:QKV_DIM] = (cur + meta_s[...]).astype(BF16)

    blk = lambda b, r: (b * nb + nq(r), 0)
    return pl.pallas_call(
        body, name="attn_bwd", grid=(n_ex, nb),
        in_specs=[pl.BlockSpec(memory_space=pltpu.SMEM),
                  pl.BlockSpec((BLOCK, N_HEADS * HEAD_DIM), blk),
                  pl.BlockSpec((BLOCK, KV_DIM), lambda b, r: (b * nb, kvb)),
                  pl.BlockSpec((BLOCK, KV_DIM), lambda b, r: (b * nb + jnp.maximum(nq(r) - 1, 0), kvb)),
                  pl.BlockSpec((BLOCK, KV_DIM), lambda b, r: (b * nb + nq(r), kvb)),
                  pl.BlockSpec((BLOCK, N_HEADS * HEAD_DIM), blk),
                  pl.BlockSpec((BLOCK, N_HEADS), blk),
                  pl.BlockSpec((BLOCK, N_HEADS * HEAD_DIM), blk)],
        out_specs=[pl.BlockSpec((BLOCK, QKV_DIM), blk),
                   pl.BlockSpec((BLOCK, N_HEADS), lambda b, r: (0, 0))],
        out_shape=[SDS((n_rows, QKV_DIM), BF16), SDS((BLOCK, N_HEADS), F32)],
        scratch_shapes=[pltpu.VMEM((3 * BLOCK, N_KV * HEAD_DIM), BF16), pltpu.VMEM((3 * BLOCK, N_KV * HEAD_DIM), BF16),
                        pltpu.VMEM((3 * BLOCK, KV_DIM), F32), pltpu.VMEM((BLOCK, KV_DIM), F32),
                        pltpu.VMEM((BLOCK, KV_DIM), F32), pltpu.VMEM((N_KV, GQA * BLOCK, HEAD_DIM), BF16),
                        pltpu.VMEM((N_KV, GQA * BLOCK, HEAD_DIM), BF16), pltpu.VMEM((N_HEADS, BLOCK, 3 * BLOCK), F32)],
        compiler_params=_params(2),
    )(sinks, qkv, qkv, qkv, qkv, o, lse, do)


SSM_TILES = (1408, 384)
XW = 256 * PAIRS_PER_CHUNK


def _cmul_add(xr, xi, mr, mi, sr, si):
    return xr + mr * sr - mi * si, xi + mr * si + mi * sr


def _to_segments(src_ref, dst, seg):
    for s in range(seg):
        dst[s * 8:(s + 1) * 8, :] = src_ref[pl.ds(s, 8, stride=seg), :]


def _from_segments(src, i, seg):
    return src[pl.ds(i, seg, stride=8), :]


def _scan_segments(buf, tab_ref, pw_ref, carry_s, seg, reverse):
    shifts = (7, 6, 4) if reverse else (1, 2, 4)
    row_id = lax.broadcasted_iota(jnp.int32, (8, 128), 0)

    def local(si, prev):
        s = (seg - 1 - si) if reverse else si
        row = pl.multiple_of(s * 8, 8)
        out = []
        for j in range(PAIRS_PER_CHUNK):
            re, im = slice(256 * j, 256 * j + 128), slice(256 * j + 128, 256 * j + 256)
            xr, xi = _cmul_add(buf[pl.ds(row, 8), re], buf[pl.ds(row, 8), im],
                               tab_ref[j, 0], tab_ref[j, 1], prev[2 * j], prev[2 * j + 1])
            buf[pl.ds(row, 8), re] = xr
            buf[pl.ds(row, 8), im] = xi
            out += [xr, xi]
        return tuple(out)

    zero = jnp.zeros((8, 128), F32)
    edge = lax.fori_loop(0, seg, local, (zero,) * (2 * PAIRS_PER_CHUNK))

    entering = []
    for j in range(PAIRS_PER_CHUNK):
        er, ei = edge[2 * j], edge[2 * j + 1]
        if reverse:
            sr = jnp.where(row_id == 7, carry_s[2 * j], pltpu.roll(er, 7, 0))
            si_ = jnp.where(row_id == 7, carry_s[2 * j + 1], pltpu.roll(ei, 7, 0))
        else:
            sr = jnp.where(row_id == 0, carry_s[2 * j], pltpu.roll(er, 1, 0))
            si_ = jnp.where(row_id == 0, carry_s[2 * j + 1], pltpu.roll(ei, 1, 0))
        for lvl, sh in enumerate(shifts):
            sr, si_ = _cmul_add(sr, si_, tab_ref[j, 2 + 2 * lvl], tab_ref[j, 3 + 2 * lvl],
                                pltpu.roll(sr, sh, 0), pltpu.roll(si_, sh, 0))
        entering += [sr, si_]
        tr, ti = _cmul_add(er, ei, tab_ref[j, 2], tab_ref[j, 3], sr, si_)
        out_row = slice(0, 1) if reverse else slice(7, 8)
        carry_s[2 * j] = jnp.broadcast_to(tr[out_row], (8, 128))
        carry_s[2 * j + 1] = jnp.broadcast_to(ti[out_row], (8, 128))

    def fix(s8, _):
        for j in range(PAIRS_PER_CHUNK):
            re, im = slice(256 * j, 256 * j + 128), slice(256 * j + 128, 256 * j + 256)
            pr8 = pw_ref[j, 0, pl.ds(pl.multiple_of(s8 * 8, 8), 8), :]
            pi8 = pw_ref[j, 1, pl.ds(pl.multiple_of(s8 * 8, 8), 8), :]
            for r in range(8):
                row = pl.multiple_of(s8 * 64 + r * 8, 8)
                pr, pi = jnp.broadcast_to(pr8[r:r + 1], (8, 128)), jnp.broadcast_to(pi8[r:r + 1], (8, 128))
                xr, xi = _cmul_add(buf[pl.ds(row, 8), re], buf[pl.ds(row, 8), im],
                                   pr, pi, entering[2 * j], entering[2 * j + 1])
                buf[pl.ds(row, 8), re] = xr
                buf[pl.ds(row, 8), im] = xi
        return 0

    lax.fori_loop(0, seg // 8, fix, 0)


def _ssm_fwd(u, b_pad, c_pad, tab, pw, d_skip, n_ex, lp):
    n_rows = u.shape[0]
    TM = _row_tile(lp, SSM_TILES)
    SEG = TM // 8
    n_t = lp // TM
    n_chunk = D_MODEL // 128

    def body(u_ref, bp_ref, cp_ref, tab_ref, pw_ref, d_ref, yg_ref, y_ref, xs_ref, buf, carry_s, us, ys):
        @pl.when(pl.program_id(2) == 0)
        def _():
            carry_s[...] = jnp.zeros_like(carry_s)

        _to_segments(u_ref, us, SEG)
        ub = us[...]
        u16 = ub.astype(BF16)
        buf[...] = jnp.dot(u16, bp_ref[...], preferred_element_type=F32)
        _scan_segments(buf, tab_ref, pw_ref, carry_s, SEG, reverse=False)
        xb = buf[...].astype(BF16)
        xs_ref[...] = xb
        ys[...] = d_ref[...] * ub + jnp.dot(xb, cp_ref[...], preferred_element_type=F32)
        for i in range(8):
            yi = _from_segments(ys, i, SEG)
            y_ref[i * SEG:(i + 1) * SEG, :] = yi
            yg_ref[i * SEG:(i + 1) * SEG, :] = _gelu(yi).astype(BF16)

    rows = lambda b, q, t: (b * n_t + t, q)
    return pl.pallas_call(
        body, name="ssm_fwd", grid=(n_ex, n_chunk, n_t),
        in_specs=[pl.BlockSpec((TM, 128), rows),
                  pl.BlockSpec((None, 128, XW), lambda b, q, t: (q, 0, 0)),
                  pl.BlockSpec((None, XW, 128), lambda b, q, t: (q, 0, 0)),
                  pl.BlockSpec((PAIRS_PER_CHUNK, 8, 8, 128), lambda b, q, t: (q, 0, 0, 0)),
                  pl.BlockSpec((PAIRS_PER_CHUNK, 2, SEG, 128), lambda b, q, t: (q, 0, 0, 0)),
                  pl.BlockSpec((1, 128), lambda b, q, t: (0, q))],
        out_specs=[pl.BlockSpec((TM, 128), rows), pl.BlockSpec((TM, 128), rows),
                   pl.BlockSpec((None, TM, XW), lambda b, q, t: (q, b * n_t + t, 0))],
        out_shape=[SDS((n_rows, D_MODEL), BF16), SDS((n_rows, D_MODEL), F32), SDS((n_chunk, n_rows, XW), BF16)],
        scratch_shapes=[pltpu.VMEM((TM, XW), F32), pltpu.VMEM((2 * PAIRS_PER_CHUNK, 8, 128), F32),
                        pltpu.VMEM((TM, 128), F32), pltpu.VMEM((TM, 128), F32)],
        compiler_params=_params(3),
    )(u, b_pad, c_pad, tab, pw, d_skip)


def _ssm_bwd(dyg, y, u, xs, ct_pad, bt_pad, tab_rev, pw_rev, d_skip, n_ex, lp):
    n_rows = u.shape[0]
    TM = _row_tile(lp, SSM_TILES)
    SEG = TM // 8
    n_t = lp // TM
    n_chunk = D_MODEL // 128
    tile = lambda q, b, t: (b * n_t + (n_t - 1 - t), q)

    def body(dyg_ref, y_ref, u_ref, xs_ref, xp_ref, ct_ref, bt_ref, tab_ref, pw_ref, d_ref,
             du_ref, db_ref, dc_ref, da_ref, dd_ref, buf, xf, carry_s, us, dys, dyp):
        b, t = pl.program_id(1), pl.program_id(2)

        @pl.when((b == 0) & (t == 0))
        def _():
            db_ref[...] = jnp.zeros_like(db_ref)
            dc_ref[...] = jnp.zeros_like(dc_ref)
            da_ref[...] = jnp.zeros_like(da_ref)
            dd_ref[...] = jnp.zeros_like(dd_ref)

        @pl.when(t == 0)
        def _():
            carry_s[...] = jnp.zeros_like(carry_s)

        dys[...] = dyg_ref[...].astype(F32) * _gelu_grad(y_ref[...])
        dd_ref[...] += _fold8(dys[...] * u_ref[...])
        _to_segments(dys, dyp, SEG)
        dy = dyp[...]
        _to_segments(u_ref, us, SEG)
        dy16 = dy.astype(BF16)
        first_tile = t == n_t - 1
        tn = (((0,), (0,)), ((), ()))
        buf[...] = jnp.dot(dy16, ct_ref[...], preferred_element_type=F32)
        dc_ref[...] += lax.dot_general(dy16, xs_ref[...], tn, preferred_element_type=F32)
        xf[16:16 + TM, :] = xs_ref[...].astype(F32)
        xf[0:16, :] = jnp.where(first_tile, 0.0, xp_ref[...].astype(F32))
        _scan_segments(buf, tab_ref, pw_ref, carry_s, SEG, reverse=True)
        g16 = buf[...].astype(BF16)
        dys[...] = d_ref[...] * dy + jnp.dot(g16, bt_ref[...], preferred_element_type=F32)
        db_ref[...] += lax.dot_general(us[...].astype(BF16), g16, tn, preferred_element_type=F32)
        row_id = lax.broadcasted_iota(jnp.int32, (8, 128), 0)
        for j in range(PAIRS_PER_CHUNK):
            re, im = slice(256 * j, 256 * j + 128), slice(256 * j + 128, 256 * j + 256)
            first = [jnp.where(row_id == 0, jnp.broadcast_to(xf[15:16, c], (8, 128)),
                               pltpu.roll(xf[8 + TM:16 + TM, c], 1, 0)) for c in (re, im)]
            for rows, pr, pi in ((slice(0, 8), first[0], first[1]),
                                 (slice(8, TM), xf[16:8 + TM, re], xf[16:8 + TM, im])):
                gr, gi = buf[rows, re], buf[rows, im]
                da_ref[j, 0] += _fold8(gr * pr + gi * pi)
                da_ref[j, 1] += _fold8(gi * pr - gr * pi)
        for i in range(8):
            du_ref[i * SEG:(i + 1) * SEG, :] = _from_segments(dys, i, SEG)

    prev16 = lambda q, b, t: (q, jnp.maximum((b * n_t + (n_t - 1 - t)) * (TM // 16) - 1, 0), 0)
    return pl.pallas_call(
        body, name="ssm_bwd", grid=(n_chunk, n_ex, n_t),
        in_specs=[pl.BlockSpec((TM, 128), tile), pl.BlockSpec((TM, 128), tile), pl.BlockSpec((TM, 128), tile),
                  pl.BlockSpec((None, TM, XW), lambda q, b, t: (q, b * n_t + (n_t - 1 - t), 0)),
                  pl.BlockSpec((None, 16, XW), prev16),
                  pl.BlockSpec((None, 128, XW), lambda q, b, t: (q, 0, 0)),
                  pl.BlockSpec((None, XW, 128), lambda q, b, t: (q, 0, 0)),
                  pl.BlockSpec((PAIRS_PER_CHUNK, 8, 8, 128), lambda q, b, t: (q, 0, 0, 0)),
                  pl.BlockSpec((PAIRS_PER_CHUNK, 2, SEG, 128), lambda q, b, t: (q, 0, 0, 0)),
                  pl.BlockSpec((1, 128), lambda q, b, t: (0, q))],
        out_specs=[pl.BlockSpec((TM, 128), tile),
                   pl.BlockSpec((None, 128, XW), lambda q, b, t: (q, 0, 0)),
                   pl.BlockSpec((None, 128, XW), lambda q, b, t: (q, 0, 0)),
                   pl.BlockSpec((PAIRS_PER_CHUNK, 2, 8, 128), lambda q, b, t: (q, 0, 0, 0)),
                   pl.BlockSpec((8, 128), lambda q, b, t: (0, q))],
        out_shape=[SDS((n_rows, D_MODEL), F32), SDS((n_chunk, 128, XW), F32), SDS((n_chunk, 128, XW), F32),
                   SDS((N_PAIR, 2, 8, 128), F32), SDS((8, D_MODEL), F32)],
        scratch_shapes=[pltpu.VMEM((TM, XW), F32), pltpu.VMEM((TM + 16, XW), F32),
                        pltpu.VMEM((2 * PAIRS_PER_CHUNK, 8, 128), F32), pltpu.VMEM((TM, 128), F32),
                        pltpu.VMEM((TM, 128), F32), pltpu.VMEM((TM, 128), F32)],
        compiler_params=_params(3),
    )(dyg, y, u, xs, xs, ct_pad, bt_pad, tab_rev, pw_rev, d_skip)


def _rms_bwd_call(dhn, h, wn, dres, name):
    n_rows = h.shape[0]

    def body(dhn_ref, h_ref, wn_ref, dres_ref, o_ref, dw_ref):
        @pl.when(pl.program_id(0) == 0)
        def _():
            dw_ref[...] = jnp.zeros_like(dw_ref)

        dh, dw_rows = _rms_bwd(dhn_ref[...], h_ref[...], wn_ref[...])
        o_ref[...] = dres_ref[...] + dh
        dw_ref[...] += _fold8(dw_rows)

    row = lambda i: (i, 0)
    return pl.pallas_call(
        body, name=name, grid=(n_rows // TM,),
        in_specs=[pl.BlockSpec((TM, D_MODEL), row), pl.BlockSpec((TM, D_MODEL), row),
                  pl.BlockSpec((1, D_MODEL), lambda i: (0, 0)), pl.BlockSpec((TM, D_MODEL), row)],
        out_specs=[pl.BlockSpec((TM, D_MODEL), row), pl.BlockSpec((8, D_MODEL), lambda i: (0, 0))],
        out_shape=[SDS((n_rows, D_MODEL), F32), SDS((8, D_MODEL), F32)], compiler_params=_params(1),
    )(dhn, h, wn, dres)


def _glu_bwd(dh, z, w4):
    n_rows = dh.shape[0]
    tm = _row_tile(n_rows, MM_TILES)
    n_sh, _, k, n = w4.shape

    def body(dh_ref, z_ref, w_ref, dz_ref, dyg_ref):
        sg = jax.nn.sigmoid(z_ref[:, D_MODEL:2 * D_MODEL].astype(F32))
        d = dh_ref[...]
        dz_ref[:, 0:D_MODEL] = (d * sg).astype(BF16)
        dz_ref[:, D_MODEL:2 * D_MODEL] = (d * z_ref[:, 0:D_MODEL].astype(F32) * sg * (1.0 - sg)).astype(BF16)
        acc = None
        for s in range(n_sh):
            part = lax.dot_general(dz_ref[:, s * n:(s + 1) * n], w_ref[s], (((1,), (1,)), ((), ())),
                                   preferred_element_type=F32)
            acc = part if acc is None else acc + part
        dyg_ref[...] = acc.astype(BF16)

    row = lambda i: (i, 0)
    return pl.pallas_call(
        body, name="glu_bwd", grid=(n_rows // tm,),
        in_specs=[pl.BlockSpec((tm, D_MODEL), row), pl.BlockSpec((tm, 2 * D_MODEL), row), _w4_spec(w4)],
        out_specs=[pl.BlockSpec((tm, 2 * D_MODEL), row), pl.BlockSpec((tm, k), row)],
        out_shape=[SDS((n_rows, 2 * D_MODEL), BF16), SDS((n_rows, k), BF16)], compiler_params=_params(1),
    )(dh, z, w4)


def _loss_head(h, wn, target, n_ex, nb):
    n_rows = h.shape[0]
    per_tile = TM // BLOCK
    n_tiles = nb // per_tile

    def body(h_ref, wn_ref, *rest):
        t_refs, (dh_ref, loss_ref, dw_ref) = rest[:per_tile], rest[per_tile:]
        b, j = pl.program_id(0), pl.program_id(1)

        @pl.when((b == 0) & (j == 0))
        def _():
            loss_ref[...] = jnp.zeros_like(loss_ref)
            dw_ref[...] = jnp.zeros_like(dw_ref)

        def block(k):
            rows = slice(k * BLOCK, (k + 1) * BLOCK)
            hh = h_ref[rows, :]
            diff = _rms(hh, wn_ref[...]) - t_refs[k][...]
            loss_ref[...] += 0.5 * jnp.sum(diff * diff) * (1.0 / D_MODEL)
            dh, dw_rows = _rms_bwd(diff * (1.0 / D_MODEL), hh, wn_ref[...])
            dh_ref[rows, :] = dh
            dw_ref[...] += _fold8(dw_rows)

        @pl.when(j == 0)
        def _():
            dh_ref[0:BLOCK, :] = jnp.zeros((BLOCK, D_MODEL), F32)

        pl.when(j > 0)(lambda: block(0))
        for k in range(1, per_tile):
            block(k)

    def t_spec(k):
        return pl.BlockSpec((BLOCK, D_MODEL), lambda b, j: (b * (nb - 1) + jnp.maximum(per_tile * j + k - 1, 0), 0))

    tile = pl.BlockSpec((TM, D_MODEL), lambda b, j: (b * n_tiles + j, 0))
    return pl.pallas_call(
        body, name="loss_head", grid=(n_ex, n_tiles),
        in_specs=[tile, pl.BlockSpec((1, D_MODEL), lambda b, j: (0, 0))] + [t_spec(k) for k in range(per_tile)],
        out_specs=[tile, pl.BlockSpec((8, 128), lambda b, j: (0, 0)), pl.BlockSpec((8, D_MODEL), lambda b, j: (0, 0))],
        out_shape=[SDS((n_rows, D_MODEL), F32), SDS((8, 128), F32), SDS((8, D_MODEL), F32)],
        compiler_params=_params(2),
    )(h, wn, *([target] * per_tile))


def _adamw(pieces, w, m, v, name):
    n_layers = len(pieces)
    rows, cols = pieces[0].shape[1:]
    rb = rows
    for cand in (256, 136, 128, 64, 32, 16, 8):
        if rows % cand == 0 and rows > cand:
            rb = cand
            break
    n_blk = rows // rb
    c1 = 1.0 / (1.0 - ADAM_B1 ** ADAM_STEP)
    c2 = 1.0 / (1.0 - ADAM_B2 ** ADAM_STEP)

    def body(*refs):
        p_refs = refs[:n_layers]
        w_ref, m_ref, v_ref, g_out, d_out, m_out, v_out = refs[n_layers:]
        layer = pl.program_id(0)
        g = None
        for l, p_ref in enumerate(p_refs):
            gl = p_ref[0].astype(F32)
            for k in range(1, N_DEV):
                gl = gl + p_ref[k].astype(F32)
            g = gl if g is None else jnp.where(layer == l, gl, g)
        m_new = ADAM_B1 * m_ref[...] + (1.0 - ADAM_B1) * g
        v_new = ADAM_B2 * v_ref[...] + (1.0 - ADAM_B2) * (g * g)
        g_out[...] = g
        m_out[...] = m_new
        v_out[...] = v_new
        d_out[...] = -ADAM_LR * ((m_new * c1) / (jnp.sqrt(v_new * c2) + ADAM_EPS) + ADAM_WD * w_ref[...])

    def piece_spec(l):
        return pl.BlockSpec((N_DEV, rb, cols), lambda ly, i: (0, jnp.where(ly == l, i, 0), 0))

    blk = pl.BlockSpec((rb, cols), lambda ly, i: (ly * n_blk + i, 0))
    return pl.pallas_call(
        body, name=name, grid=(n_layers, n_blk),
        in_specs=[piece_spec(l) for l in range(n_layers)] + [blk, blk, blk],
        out_specs=[blk, blk, blk, blk],
        out_shape=[SDS((n_layers * rows, cols), F32)] * 4, compiler_params=_params(2),
    )(*pieces, w, m, v)


_HBM = pl.BlockSpec(memory_space=pltpu.HBM)
_SEM = pl.BlockSpec(memory_space=pltpu.SEMAPHORE)
_EFFECT = pltpu.SideEffectType.DATAFLOW_SIDE_EFFECTING
N_GATHER_PEERS = N_CHIPS - 1
N_EXCHANGE_PEERS = N_DEV - 1


def _gather_copies(srcs, lands, send_sems, recv_sems):
    x, y, c = lax.axis_index("x"), lax.axis_index("y"), lax.axis_index("c")
    mine = 2 * x + y
    chips = [(1 - x, y), (x, 1 - y), (1 - x, 1 - y)]
    out, inc = [], []
    for a in range(len(srcs)):
        for k, (px, py) in enumerate(chips):
            j = a * N_GATHER_PEERS + k
            sems = dict(send_sem=send_sems.at[j], recv_sem=recv_sems.at[j], device_id=(px, py, c),
                        device_id_type=pl.DeviceIdType.MESH)
            out.append(pltpu.make_async_remote_copy(src_ref=srcs[a], dst_ref=lands[a].at[mine], **sems))
            inc.append(pltpu.make_async_remote_copy(src_ref=srcs[a], dst_ref=lands[a].at[2 * px + py], **sems))
    return out, inc


def _exchange_copies(n_scatter):
    def copies(srcs, lands, send_sems, recv_sems):
        x, y, c = lax.axis_index("x"), lax.axis_index("y"), lax.axis_index("c")
        me = 4 * x + 2 * y + c
        peers = [(x ^ (k >> 2), y ^ ((k >> 1) & 1), c ^ (k & 1)) for k in range(1, N_DEV)]
        out, inc = [], []
        for a in range(len(srcs)):
            for k, (px, py, pc) in enumerate(peers):
                j = a * N_EXCHANGE_PEERS + k
                sems = dict(send_sem=send_sems.at[j], recv_sem=recv_sems.at[j], device_id=(px, py, pc),
                            device_id_type=pl.DeviceIdType.MESH)
                theirs = srcs[a].at[2 * px + py] if a < n_scatter else srcs[a]
                mine = srcs[a].at[2 * x + y] if a < n_scatter else srcs[a]
                out.append(pltpu.make_async_remote_copy(src_ref=theirs, dst_ref=lands[a].at[me], **sems))
                inc.append(pltpu.make_async_remote_copy(src_ref=mine, dst_ref=lands[a].at[4 * px + 2 * py + pc], **sems))
        return out, inc

    return copies


def _split_start(groups, copies_fn, n_peers, name):
    sizes = [len(srcs) for srcs, _ in groups]
    flat = [a for srcs, lands in groups for a in list(srcs) + list(lands)]
    n_flat, n_grp = len(flat), len(groups)

    def body(*refs):
        sems = refs[2 * n_flat:2 * n_flat + 2 * n_grp]
        token = refs[-1]
        at = 0
        for gi, n in enumerate(sizes):
            out, _ = copies_fn(refs[at:at + n], refs[at + n:at + 2 * n], sems[2 * gi], sems[2 * gi + 1])
            for cp in out:
                cp.start()
            at += 2 * n
        token[...] = jnp.zeros_like(token)

    sem_shapes = []
    for n in sizes:
        sem_shapes += [pltpu.SemaphoreType.DMA((n * n_peers,)), pltpu.SemaphoreType.DMA((n * n_peers,))]
    res = pl.pallas_call(
        body, name=name,
        out_shape=(*[pltpu.HBM(a.shape, a.dtype) for a in flat], *sem_shapes, SDS((8, 128), F32)),
        in_specs=[_HBM] * n_flat,
        out_specs=(*[_HBM] * n_flat, *[_SEM] * (2 * n_grp), pl.BlockSpec(memory_space=pltpu.VMEM)),
        input_output_aliases={i: i for i in range(n_flat)},
        compiler_params=pltpu.CompilerParams(has_side_effects=_EFFECT),
    )(*[pltpu.with_memory_space_constraint(a, pltpu.HBM) for a in flat])
    handles, at = [], 0
    for gi, n in enumerate(sizes):
        handles.append((res[n_flat + 2 * gi], res[n_flat + 2 * gi + 1], list(res[at:at + n]), list(res[at + n:at + 2 * n])))
        at += 2 * n
    return handles, res[-1]


def _split_wait(handle, after, copies_fn, name):
    send_sems, recv_sems, srcs, lands = handle
    n = len(srcs)
    after = list(after) if isinstance(after, (list, tuple)) else [after]

    def body(*refs):
        out, inc = copies_fn(refs[:n], refs[n:2 * n], refs[2 * n], refs[2 * n + 1])
        for cp in out:
            cp.wait_send()
        for cp in inc:
            cp.wait_recv()

    flat = list(srcs) + list(lands)
    res = pl.pallas_call(
        body, name=name,
        out_shape=tuple(pltpu.HBM(a.shape, a.dtype) for a in flat),
        in_specs=[_HBM] * (2 * n) + [_SEM, _SEM] + [pl.BlockSpec(memory_space=pl.ANY)] * len(after),
        out_specs=tuple([_HBM] * (2 * n)),
        input_output_aliases={i: i for i in range(2 * n)},
        compiler_params=pltpu.CompilerParams(has_side_effects=_EFFECT),
    )(*flat, send_sems, recv_sems, *after)
    return list(res[n:])


def _landing(own, slot, n_slots):
    return lax.dynamic_update_index_in_dim(lax.empty((n_slots,) + own.shape, own.dtype), own, slot, 0)


def _ssm_discretize(lam_re, lam_im, log_dt, b_re, b_im):
    lr = jnp.minimum(lam_re, LAMBDA_RE_MAX)
    li = lam_im
    dt = jnp.exp(log_dt)[:, None]
    mag = jnp.exp(lr * dt)
    ar, ai = mag * jnp.cos(li * dt), mag * jnp.sin(li * dt)
    den = lr * lr + li * li
    nr, ni = ar - 1.0, ai
    gr, gi = (nr * lr + ni * li) / den, (ni * lr - nr * li) / den
    bbr = gr[:, :, None] * b_re - gi[:, :, None] * b_im
    bbi = gr[:, :, None] * b_im + gi[:, :, None] * b_re
    return ar, ai, bbr, bbi


def _pair_lanes(t):
    return t.reshape(N_PAIR, 2 * SSM_STATE)


def _chan_state_blocks(t_gcp):
    t = t_gcp.reshape(N_PAIR, 2, SSM_GROUP, SSM_STATE)
    eye2 = jnp.eye(2, dtype=t.dtype)
    blk = jnp.einsum("rgcp,gh->rgchp", t, eye2).reshape(N_PAIR, 2 * SSM_GROUP, 2 * SSM_STATE)
    place = jax.nn.one_hot(jnp.arange(N_PAIR) % PAIRS_PER_CHUNK, PAIRS_PER_CHUNK, dtype=t.dtype)
    return jnp.einsum("rcl,rj->rjcl", blk, place).reshape(N_PAIR, 128, 2 * SSM_STATE)


def _chan_state_unblock(t):
    t = t.reshape(N_PAIR, PAIRS_PER_CHUNK, 2, SSM_GROUP, 2, SSM_STATE)
    place = jax.nn.one_hot(jnp.arange(N_PAIR) % PAIRS_PER_CHUNK, PAIRS_PER_CHUNK, dtype=t.dtype)
    t = jnp.einsum("rjgchp,rj->rgchp", t, place)
    t = jnp.einsum("rgchp,gh->rgcp", t, jnp.eye(2, dtype=t.dtype))
    return t.reshape(SSM_NG, SSM_GROUP, SSM_STATE)


def _scan_tables(zr, zi, reverse, seg):
    zr, zi = _pair_lanes(zr), _pair_lanes(-zi if reverse else zi)
    k = jnp.arange(1, seg + 1, dtype=F32)
    k = (k[::-1] if reverse else k)[None, :, None]
    mag = jnp.exp(k * zr[:, None, :])
    pw = jnp.stack([mag * jnp.cos(k * zi[:, None, :]), mag * jnp.sin(k * zi[:, None, :])], axis=1)
    first, last = (seg - 1, 0) if reverse else (0, seg - 1)
    big = [(pw[:, 0, last], pw[:, 1, last])]
    for _ in range(2):
        br, bi = big[-1]
        big.append((br * br - bi * bi, 2.0 * br * bi))
    rows = jnp.arange(8)[None, :, None]
    tiles = [jnp.broadcast_to(pw[:, comp, first][:, None, :], (N_PAIR, 8, 128)) for comp in (0, 1)]
    for lvl, step in enumerate((1, 2, 4)):
        keep = (rows <= 7 - step) if reverse else (rows >= step)
        for part in big[lvl]:
            tiles.append(jnp.where(keep, part[:, None, :], 0.0))
    return jnp.stack(tiles, axis=1), pw


def _pairs_to_chunks(t):
    n_chunk = N_PAIR // PAIRS_PER_CHUNK
    return jnp.swapaxes(t.reshape(n_chunk, PAIRS_PER_CHUNK, 128, 256), 1, 2).reshape(n_chunk, 128, XW)


def _chunks_to_pairs(t):
    n_chunk = N_PAIR // PAIRS_PER_CHUNK
    return jnp.swapaxes(t.reshape(n_chunk, 128, PAIRS_PER_CHUNK, 256), 1, 2).reshape(N_PAIR, 128, 256)


def _ssm_operands(w, lp):
    seg = _row_tile(lp, SSM_TILES) // 8
    ar, ai, bbr, bbi = _ssm_discretize(w["ssm_lambda_re"], w["ssm_lambda_im"], w["ssm_log_dt"], w["ssm_b_re"], w["ssm_b_im"])
    b_blk = jnp.concatenate([_chan_state_blocks(jnp.swapaxes(bbr, 1, 2)), _chan_state_blocks(jnp.swapaxes(bbi, 1, 2))], axis=2)
    c_blk = jnp.concatenate([_chan_state_blocks(w["ssm_c_re"]), -_chan_state_blocks(w["ssm_c_im"])], axis=2)
    dt = jnp.exp(w["ssm_log_dt"])[:, None]
    zr, zi = jnp.minimum(w["ssm_lambda_re"], LAMBDA_RE_MAX) * dt, w["ssm_lambda_im"] * dt
    b_cat, c_cat = _pairs_to_chunks(b_blk).astype(BF16), _pairs_to_chunks(c_blk).astype(BF16)
    return (b_cat, jnp.swapaxes(b_cat, 1, 2), c_cat, jnp.swapaxes(c_cat, 1, 2),
            *_scan_tables(zr, zi, False, seg), *_scan_tables(zr, zi, True, seg))


def _local_step(x, target, w, late_weights, on_grads):
    n_ex, seq, _ = x.shape
    lp = seq + BLOCK
    nb = lp // BLOCK
    n_rows = n_ex * lp
    g = {}

    head = jnp.concatenate([jnp.zeros((PAD, D_MODEL), F32), w["meta_tokens"]], axis=0)
    h0 = jnp.concatenate([jnp.broadcast_to(head[None], (n_ex, BLOCK, D_MODEL)), x], axis=1).reshape(n_rows, D_MODEL)

    qkv, hn_a = _rms_mm_cols(h0, w["attn_norm_w"], w["attn_w_qkv"], "qkv_fwd")
    att, lse = _attn_fwd(qkv, w["attn_sinks"], n_ex, nb)
    h1 = _mm_acc(att, w["attn_w_o"], False, "attn_out_fwd", res=h0)
    w = {**w, **late_weights(0, att)}
    h2, a0, hn_m0, u = _mlp_fwd(h1, w["mlp_norm_w"][0:1], w["mlp_w_up"][0], w["mlp_w_down"][0], "mlp0_fwd",
                                next_norm=w["ssm_norm_w"])
    late = late_weights(1, h2)
    w["ssm_w_glu"] = late["ssm_w_glu"]
    w["mlp_w_up"], w["mlp_w_down"] = w["mlp_w_up"] + late["mlp_w_up"], w["mlp_w_down"] + late["mlp_w_down"]

    ops = w["ssm_operands"] if "ssm_operands" in w else _ssm_operands(w, lp)
    b_pad, bt_pad, ct_pad, c_pad, tab_fwd, pw_fwd, tab_rev, pw_rev = ops
    yg, y, xs = _ssm_fwd(u, b_pad, c_pad, tab_fwd, pw_fwd, w["ssm_d"], n_ex, lp)
    z = _mm_cols(yg, w["ssm_w_glu"], False, "glu_mm_fwd")
    h4, a1, hn_m1, h3 = _mlp_fwd(h2, w["mlp_norm_w"][1:2], w["mlp_w_up"][1], w["mlp_w_down"][1], "mlp1_fwd", glu_z=z)

    dh4, loss_tile, dnorm_f = _loss_head(h4, w["final_norm_w"], target.reshape(n_ex * seq, D_MODEL), n_ex, nb)

    def mlp_bwd(dh_out, h_in, a, hn, layer, tag, norm_w):
        dhn, dw_up, dw_down = None, None, None
        for s in range(N_CHIPS):
            final = s == N_CHIPS - 1
            res = _mlp_bwd_shard(s, dh_out, a, hn, dhn, h_in if final else None, norm_w,
                                 w["mlp_w_up"][layer], w["mlp_w_down"][layer], dw_up, dw_down, f"{tag}_bwd{s}")
            dhn, dw_up, dw_down = res[:3]
        return dhn, res[3], dw_up, dw_down

    dh3, dnorm_m1, dwu1, dwd1 = mlp_bwd(dh4, h3, a1, hn_m1, 1, "mlp1", w["mlp_norm_w"][1:2])
    tok = on_grads("mlp1", {"mlp_w_up": dwu1, "mlp_w_down": dwd1})
    dz, dyg = _glu_bwd(dh3, z, w["ssm_w_glu"])
    g["ssm_w_glu"] = _mm_tn(yg, dz, N_CHIPS, False, "glu_mm_dw")
    du, db_blk, dc_blk, da_t, dd_t = _ssm_bwd(dyg, y, u, xs, ct_pad, bt_pad, tab_rev, pw_rev, w["ssm_d"] + tok, n_ex, lp)
    dh2, dnorm_s = _rms_bwd_call(du, h2, w["ssm_norm_w"], dh3, "ssm_norm_bwd")
    db_blk, dc_blk = _chunks_to_pairs(db_blk), _chunks_to_pairs(dc_blk)
    g["ssm_c_re"] = _chan_state_unblock(dc_blk[:, :, 0:128])
    g["ssm_c_im"] = -_chan_state_unblock(dc_blk[:, :, 128:256])
    g_bbr = jnp.swapaxes(_chan_state_unblock(db_blk[:, :, 0:128]), 1, 2)
    g_bbi = jnp.swapaxes(_chan_state_unblock(db_blk[:, :, 128:256]), 1, 2)
    g_a = jnp.sum(da_t, axis=2).reshape(N_PAIR, 2, 2, SSM_STATE)
    g_ar, g_ai = g_a[:, 0].reshape(SSM_NG, SSM_STATE), g_a[:, 1].reshape(SSM_NG, SSM_STATE)
    _, vjp = jax.vjp(_ssm_discretize, w["ssm_lambda_re"], w["ssm_lambda_im"], w["ssm_log_dt"], w["ssm_b_re"], w["ssm_b_im"])
    g["ssm_lambda_re"], g["ssm_lambda_im"], g["ssm_log_dt"], g["ssm_b_re"], g["ssm_b_im"] = vjp((g_ar, g_ai, g_bbr, g_bbi))
    tok = on_grads("ssm", g)
    g = {}
    dh1, dnorm_m0, dwu0, dwd0 = mlp_bwd(dh2, h1, a0, hn_m0, 0, "mlp0", w["mlp_norm_w"][0:1] + tok)
    datt = _mm_cols(dh1, w["attn_w_o"], True, "attn_out_dx")
    dw_o = _mm_tn(att, dh1, N_CHIPS, True, "attn_out_dw")
    tok = on_grads("mlp0", {"mlp_w_up": dwu0, "mlp_w_down": dwd0, "attn_w_o": dw_o})
    dqkv, dsink_rows = _attn_bwd(qkv, w["attn_sinks"] + tok, att, lse, datt, n_ex, nb)
    tok = on_grads("qkv", {"attn_w_qkv": _mm_tn(hn_a, dqkv, N_CHIPS, False, "qkv_dw")})
    dh0, dnorm_a = _mm_acc(dqkv, w["attn_w_qkv"], True, "qkv_dx", rms_bwd=(h0, w["attn_norm_w"] + tok, dh1))

    dh0 = dh0.reshape(n_ex, lp, D_MODEL)
    on_grads("rest", {
        "mlp_norm_w": jnp.stack([jnp.sum(dnorm_m0, axis=0), jnp.sum(dnorm_m1, axis=0)]),
        "final_norm_w": jnp.sum(dnorm_f, axis=0),
        "attn_norm_w": jnp.sum(dnorm_a, axis=0)[None],
        "ssm_norm_w": jnp.sum(dnorm_s, axis=0)[None],
        "attn_sinks": jnp.sum(dsink_rows, axis=0)[None],
        "ssm_d": jnp.sum(dd_t, axis=0)[None],
        "meta_tokens": jnp.sum(dh0[:, PAD:BLOCK], axis=0),
        "loss": loss_tile[0, 0:1]})
    return loss_tile, dh0[:, BLOCK:]


_SHARDED_SMALL = ("meta_tokens", "ssm_norm_w", "ssm_d")
_REP_SSM = ("ssm_lambda_re", "ssm_lambda_im", "ssm_log_dt", "ssm_b_re", "ssm_b_im", "ssm_c_re", "ssm_c_im")
_REP_MISC = ("attn_norm_w", "attn_sinks", "mlp_norm_w", "final_norm_w")
_BIG = ("attn_w_qkv", "attn_w_o", "ssm_w_glu", "mlp_w_up", "mlp_w_down")


def _pack(parts, cols):
    flat = jnp.concatenate([p.reshape(-1) for p in parts])
    rows = -(-flat.shape[0] // (8 * cols)) * 8
    return jnp.pad(flat, (0, rows * cols - flat.shape[0])).reshape(rows, cols)


def _unpack(packed, like):
    flat = packed.reshape(-1)
    out, at = [], 0
    for p in like:
        out.append(flat[at:at + p.size].reshape(p.shape))
        at += p.size
    return out


def kernel(x, meta_tokens, attn_norm_w, attn_w_qkv, attn_sinks, attn_w_o, ssm_norm_w, ssm_lambda_re, ssm_lambda_im, ssm_log_dt, ssm_b_re, ssm_b_im, ssm_c_re, ssm_c_im, ssm_d, ssm_w_glu, mlp_norm_w, mlp_w_up, mlp_w_down, final_norm_w, loss_target, m_meta_tokens, m_attn_norm_w, m_attn_w_qkv, m_attn_sinks, m_attn_w_o, m_ssm_norm_w, m_ssm_lambda_re, m_ssm_lambda_im, m_ssm_log_dt, m_ssm_b_re, m_ssm_b_im, m_ssm_c_re, m_ssm_c_im, m_ssm_d, m_ssm_w_glu, m_mlp_norm_w, m_mlp_w_up, m_mlp_w_down, m_final_norm_w, v_meta_tokens, v_attn_norm_w, v_attn_w_qkv, v_attn_sinks, v_attn_w_o, v_ssm_norm_w, v_ssm_lambda_re, v_ssm_lambda_im, v_ssm_log_dt, v_ssm_b_re, v_ssm_b_im, v_ssm_c_re, v_ssm_c_im, v_ssm_d, v_ssm_w_glu, v_mlp_norm_w, v_mlp_w_up, v_mlp_w_down, v_final_norm_w):
    names = ("meta_tokens", "attn_norm_w", "attn_w_qkv", "attn_sinks", "attn_w_o", "ssm_norm_w", "ssm_lambda_re",
             "ssm_lambda_im", "ssm_log_dt", "ssm_b_re", "ssm_b_im", "ssm_c_re", "ssm_c_im", "ssm_d", "ssm_w_glu",
             "mlp_norm_w", "mlp_w_up", "mlp_w_down", "final_norm_w")
    wts = dict(zip(names, (meta_tokens, attn_norm_w, attn_w_qkv, attn_sinks, attn_w_o, ssm_norm_w, ssm_lambda_re,
                           ssm_lambda_im, ssm_log_dt, ssm_b_re, ssm_b_im, ssm_c_re, ssm_c_im, ssm_d, ssm_w_glu,
                           mlp_norm_w, mlp_w_up, mlp_w_down, final_norm_w)))
    mom = dict(zip(names, (m_meta_tokens, m_attn_norm_w, m_attn_w_qkv, m_attn_sinks, m_attn_w_o, m_ssm_norm_w,
                           m_ssm_lambda_re, m_ssm_lambda_im, m_ssm_log_dt, m_ssm_b_re, m_ssm_b_im, m_ssm_c_re,
                           m_ssm_c_im, m_ssm_d, m_ssm_w_glu, m_mlp_norm_w, m_mlp_w_up, m_mlp_w_down, m_final_norm_w)))
    var = dict(zip(names, (v_meta_tokens, v_attn_norm_w, v_attn_w_qkv, v_attn_sinks, v_attn_w_o, v_ssm_norm_w,
                           v_ssm_lambda_re, v_ssm_lambda_im, v_ssm_log_dt, v_ssm_b_re, v_ssm_b_im, v_ssm_c_re,
                           v_ssm_c_im, v_ssm_d, v_ssm_w_glu, v_mlp_norm_w, v_mlp_w_up, v_mlp_w_down, v_final_norm_w)))

    my_chip = 2 * lax.axis_index("x") + lax.axis_index("y")
    my_dev = 2 * my_chip + lax.axis_index("c")
    small_mine = _pack([wts[n] for n in _SHARDED_SMALL], 128)
    first = [attn_w_qkv.astype(BF16), attn_w_o.astype(BF16), small_mine]
    up16, down16 = mlp_w_up.astype(BF16), mlp_w_down.astype(BF16)
    mlp0 = [up16[0:1], down16[0:1]]
    rest = [ssm_w_glu.astype(BF16), up16[1:2], down16[1:2]]
    handles, _ = _split_start([(srcs, [_landing(a, my_chip, N_CHIPS) for a in srcs]) for srcs in (first, mlp0, rest)],
                              _gather_copies, N_GATHER_PEERS, "gather_start")
    full = {n: wts[n] for n in _REP_MISC}
    full["final_norm_w"] = final_norm_w[None]
    for n in _REP_SSM:
        full[n] = wts[n][0]
    full["ssm_operands"] = _ssm_operands(full, x.shape[1] + BLOCK)
    got = _split_wait(handles[0], full["ssm_operands"], _gather_copies, "gather_wait_first")
    full["attn_w_qkv"], full["attn_w_o"] = got[0], got[1]
    smalls = [_unpack(got[2][s], [wts[n] for n in _SHARDED_SMALL]) for s in range(N_CHIPS)]
    for k, n in enumerate(_SHARDED_SMALL):
        full[n] = jnp.concatenate([smalls[s][k] for s in range(N_CHIPS)], axis=1)

    def late_weights(stage, after):
        if stage == 0:
            up, down = _split_wait(handles[1], after, _gather_copies, "gather_wait_mlp0")
            return {"mlp_w_up": [up], "mlp_w_down": [down]}
        glu, up, down = _split_wait(handles[2], after, _gather_copies, "gather_wait_rest")
        return {"ssm_w_glu": glu, "mlp_w_up": [up], "mlp_w_down": [down]}

    def shard_cols(t):
        return jnp.swapaxes(t.reshape(t.shape[0], N_CHIPS, t.shape[1] // N_CHIPS), 0, 1)

    pending = {}

    def on_grads(tag, g):
        scatter = [g[n] for n in _BIG if n in g]
        whole = []
        if tag == "ssm":
            whole = [_pack([g[n] for n in _REP_SSM], D_MODEL)]
        if tag == "rest":
            parts = [shard_cols(g[n]) for n in _SHARDED_SMALL]
            scatter = [jnp.stack([_pack([p[s] for p in parts], 128) for s in range(N_CHIPS)])]
            whole = [_pack([g[n] for n in _REP_MISC] + [g["loss"]], D_MODEL)]
        srcs = scatter + whole
        lands = [_landing(lax.dynamic_index_in_dim(a, my_chip, 0, keepdims=False), my_dev, N_DEV) for a in scatter]
        lands += [_landing(a, my_dev, N_DEV) for a in whole]
        hs, token = _split_start([(srcs, lands)], _exchange_copies(len(scatter)), N_EXCHANGE_PEERS, "exchange_start_" + tag)
        pending[tag] = (hs[0], len(scatter))
        return token[0, 0]

    _, grad_x = _local_step(x, loss_target, full, late_weights, on_grads)

    recv = {}
    for tag, (handle, n_scatter) in pending.items():
        recv[tag] = _split_wait(handle, grad_x, _exchange_copies(n_scatter), "exchange_wait_" + tag)
    loss = jnp.sum(recv["rest"][1].reshape(N_DEV, -1)[:, sum(wts[n].size for n in _REP_MISC)])

    out = {}

    def update(tag, pieces, w2, m2, v2):
        return _adamw(pieces, w2, m2, v2, "adamw_" + tag)

    def update_weight(n, pieces):
        shp = wts[n].shape
        r2 = (math.prod(shp[:-1]), shp[-1])
        res = update(n, pieces, wts[n].reshape(r2), mom[n].reshape(r2), var[n].reshape(r2))
        out[n] = [t.reshape(shp) for t in res]

    update_weight("mlp_w_up", [recv["mlp0"][1], recv["mlp1"][0]])
    update_weight("mlp_w_down", [recv["mlp0"][2], recv["mlp1"][1]])
    update_weight("attn_w_o", [recv["mlp0"][0]])
    update_weight("ssm_w_glu", [recv["ssm"][0]])
    update_weight("attn_w_qkv", [recv["qkv"][0]])
    for tag, group, pieces, cols in (("small", _SHARDED_SMALL, recv["rest"][0], 128),
                                     ("rep_ssm", _REP_SSM, recv["ssm"][1], D_MODEL),
                                     ("rep_misc", _REP_MISC, recv["rest"][1], D_MODEL)):
        like = [wts[n] for n in group]
        res = update(tag, [pieces], _pack(like, cols), _pack([mom[n] for n in group], cols),
                     _pack([var[n] for n in group], cols))
        for k, n in enumerate(group):
            out[n] = [_unpack(t, like)[k] for t in res]

    return (loss, grad_x, *[out[n][0] for n in names], *[out[n][1] for n in names],
            *[out[n][2] for n in names], *[out[n][3] for n in names])
```

```python
import functools
import math

import jax
import jax.numpy as jnp
from jax import lax
from jax.experimental import pallas as pl
from jax.experimental.pallas import tpu as pltpu

F32 = jnp.float32
BF16 = jnp.bfloat16
SDS = jax.ShapeDtypeStruct

D_MODEL = 1024
N_HEADS = 16
N_KV = 4
GQA = N_HEADS // N_KV
HEAD_DIM = 64
BLOCK = 128
N_META = 16
PAD = BLOCK - N_META
QKV_DIM = (N_HEADS + 2 * N_KV) * HEAD_DIM
KV_DIM = 2 * N_KV * HEAD_DIM
D_FF = 4 * D_MODEL
N_CHIPS = 4
N_DEV = 8
SSM_GROUP = 16
SSM_NG = D_MODEL // SSM_GROUP
SSM_STATE = 64
N_PAIR = SSM_NG // 2
PAIRS_PER_CHUNK = 4
RMS_EPS = 1e-6
NEG_INF = -1e30
LAMBDA_RE_MAX = -1e-4
ADAM_LR, ADAM_B1, ADAM_B2, ADAM_EPS, ADAM_WD, ADAM_STEP = 0.001, 0.9, 0.999, 1e-08, 0.01, 10

TM = 384
MM_TILES = (768, 384)
MLP_FWD_TILES = (384,)
MLP_BWD_TILES = (768, 384)
TN_TILES = (1408, 768, 384)
VMEM_LIMIT = 56 * 1024 * 1024


def _params(n_grid):
    return pltpu.CompilerParams(dimension_semantics=("arbitrary",) * n_grid, vmem_limit_bytes=VMEM_LIMIT)


def _row_tile(n_rows, tiles):
    return next(t for t in tiles if n_rows % t == 0)


def _rms(h, w):
    r = lax.rsqrt(jnp.mean(h * h, axis=-1, keepdims=True) + RMS_EPS)
    return h * r * w


def _rms_bwd(dhn, h, w):
    r = lax.rsqrt(jnp.mean(h * h, axis=-1, keepdims=True) + RMS_EPS)
    g = dhn * w
    proj = jnp.sum(g * h, axis=-1, keepdims=True) * (1.0 / D_MODEL)
    return r * g - h * (r * r * r) * proj, dhn * h * r


def _fold8(t):
    return jnp.sum(t.reshape(t.shape[0] // 8, 8, t.shape[1]), axis=0)


def _gelu(y):
    return 0.5 * y * (1.0 + jnp.tanh(0.7978845608028654 * (y + 0.044715 * y * y * y)))


def _gelu_grad(y):
    t = jnp.tanh(0.7978845608028654 * (y + 0.044715 * y * y * y))
    return 0.5 * (1.0 + t) + 0.5 * y * (1.0 - t * t) * 0.7978845608028654 * (1.0 + 3.0 * 0.044715 * y * y)


def _w4_spec(w4):
    n_sh, _, k, n = w4.shape
    return pl.BlockSpec((n_sh, None, k, n), lambda i: (0, 0, 0, 0))


def _rms_mm_cols(h, wn, w4, name):
    n_rows = h.shape[0]
    n_sh, _, k, n = w4.shape
    tm = _row_tile(n_rows, MM_TILES)

    def body(h_ref, wn_ref, w_ref, o_ref, hn_ref):
        hn = _rms(h_ref[...], wn_ref[...]).astype(BF16)
        hn_ref[...] = hn
        for s in range(n_sh):
            o_ref[:, s * n:(s + 1) * n] = jnp.dot(hn, w_ref[s], preferred_element_type=F32).astype(o_ref.dtype)

    return pl.pallas_call(
        body, name=name, grid=(n_rows // tm,),
        in_specs=[pl.BlockSpec((tm, k), lambda i: (i, 0)), pl.BlockSpec((1, k), lambda i: (0, 0)), _w4_spec(w4)],
        out_specs=[pl.BlockSpec((tm, n_sh * n), lambda i: (i, 0)), pl.BlockSpec((tm, k), lambda i: (i, 0))],
        out_shape=[SDS((n_rows, n_sh * n), BF16), SDS((n_rows, k), BF16)],
        compiler_params=_params(1),
    )(h, wn, w4)


def _mm_cols(x, w4, trans_w, name):
    n_rows, kx = x.shape
    tm = _row_tile(n_rows, MM_TILES)
    n_sh, _, k, n = w4.shape
    n_out = k if trans_w else n
    dims = (((1,), (1,)), ((), ())) if trans_w else (((1,), (0,)), ((), ()))

    def body(x_ref, w_ref, o_ref):
        x16 = x_ref[...].astype(BF16)
        for s in range(n_sh):
            o_ref[:, s * n_out:(s + 1) * n_out] = lax.dot_general(
                x16, w_ref[s], dims, preferred_element_type=F32).astype(o_ref.dtype)

    return pl.pallas_call(
        body, name=name, grid=(n_rows // tm,),
        in_specs=[pl.BlockSpec((tm, kx), lambda i: (i, 0)), _w4_spec(w4)],
        out_specs=pl.BlockSpec((tm, n_sh * n_out), lambda i: (i, 0)),
        out_shape=SDS((n_rows, n_sh * n_out), BF16),
        compiler_params=_params(1),
    )(x, w4)


def _mm_acc(x, w4, trans_w, name, res=None, rms_bwd=None, out_dtype=F32):
    n_rows = x.shape[0]
    tm = _row_tile(n_rows, MM_TILES)
    n_sh, _, k, n = w4.shape
    kx, n_out = (n, k) if trans_w else (k, n)
    dims = (((1,), (1,)), ((), ())) if trans_w else (((1,), (0,)), ((), ()))

    def body(*refs):
        if rms_bwd is not None:
            x_ref, w_ref, h_ref, wn_ref, dres_ref, o_ref, dw_ref = refs
        elif res is not None:
            x_ref, w_ref, res_ref, o_ref = refs
        else:
            x_ref, w_ref, o_ref = refs
        acc = None
        for s in range(n_sh):
            part = lax.dot_general(x_ref[:, s * kx:(s + 1) * kx].astype(BF16), w_ref[s], dims, preferred_element_type=F32)
            acc = part if acc is None else acc + part
        if rms_bwd is not None:
            dh, dw_rows = _rms_bwd(acc, h_ref[...], wn_ref[...])
            o_ref[...] = (dres_ref[...] + dh).astype(o_ref.dtype)

            @pl.when(pl.program_id(0) == 0)
            def _():
                dw_ref[...] = jnp.zeros_like(dw_ref)

            dw_ref[...] += _fold8(dw_rows)
        elif res is not None:
            o_ref[...] = (res_ref[...] + acc).astype(o_ref.dtype)
        else:
            o_ref[...] = acc.astype(o_ref.dtype)

    row = lambda i: (i, 0)
    in_specs = [pl.BlockSpec((tm, n_sh * kx), row), _w4_spec(w4)]
    args = [x, w4]
    out_specs = pl.BlockSpec((tm, n_out), row)
    out_shape = SDS((n_rows, n_out), out_dtype)
    if rms_bwd is not None:
        h, wn, dres = rms_bwd
        in_specs += [pl.BlockSpec((tm, n_out), row), pl.BlockSpec((1, n_out), lambda i: (0, 0)),
                     pl.BlockSpec((tm, n_out), row)]
        args += [h, wn, dres]
        out_specs = [out_specs, pl.BlockSpec((8, n_out), lambda i: (0, 0))]
        out_shape = [out_shape, SDS((8, n_out), F32)]
    elif res is not None:
        in_specs.append(pl.BlockSpec((tm, n_out), row))
        args.append(res)
    return pl.pallas_call(
        body, name=name, grid=(n_rows // tm,), in_specs=in_specs, out_specs=out_specs, out_shape=out_shape,
        compiler_params=_params(1),
    )(*args)


def _mm_tn(a, b, n_sh, a_sharded, name):
    n_rows = a.shape[0]
    tm = _row_tile(n_rows, TN_TILES)
    ka = a.shape[1] // n_sh if a_sharded else a.shape[1]
    nb = b.shape[1] if a_sharded else b.shape[1] // n_sh
    n_i = n_rows // tm

    def body(a_ref, b_ref, o_ref, acc):
        i = pl.program_id(0)

        @pl.when(i == 0)
        def _():
            acc[...] = jnp.zeros_like(acc)

        for s in range(n_sh):
            a_s = a_ref[:, s * ka:(s + 1) * ka] if a_sharded else a_ref[...]
            b_s = b_ref[...] if a_sharded else b_ref[:, s * nb:(s + 1) * nb]
            acc[s] += lax.dot_general(a_s.astype(BF16), b_s.astype(BF16), (((0,), (0,)), ((), ())),
                                      preferred_element_type=F32)

        @pl.when(i == n_i - 1)
        def _():
            o_ref[...] = acc[...].astype(o_ref.dtype)

    return pl.pallas_call(
        body, name=name, grid=(n_i,),
        in_specs=[pl.BlockSpec((tm, a.shape[1]), lambda i: (i, 0)), pl.BlockSpec((tm, b.shape[1]), lambda i: (i, 0))],
        out_specs=pl.BlockSpec((n_sh, ka, nb), lambda i: (0, 0, 0)),
        out_shape=SDS((n_sh, ka, nb), BF16),
        scratch_shapes=[pltpu.VMEM((n_sh, ka, nb), F32)], compiler_params=_params(1),
    )(a, b)


def _mlp_fwd(h, wn, w_up4, w_down4, name, next_norm=None, glu_z=None):
    n_rows = h.shape[0]
    tm = _row_tile(n_rows, MLP_FWD_TILES)
    n_sh = w_up4.shape[0]
    f_sh = D_FF // n_sh
    w_down = w_down4.reshape(D_FF, D_MODEL)

    def body(*refs):
        refs = list(refs)
        h_ref, wn_ref, wu_ref, wd_ref = refs[:4]
        at = 4
        if next_norm is not None:
            nn_ref = refs[at]
            at += 1
        if glu_z is not None:
            z_ref = refs[at]
            at += 1
        o_ref, a_ref, hn_ref = refs[at:at + 3]
        at += 3
        if next_norm is not None:
            u_ref = refs[at]
            at += 1
        if glu_z is not None:
            hin_ref = refs[at]
            at += 1
        act_s = refs[at]
        h_in = h_ref[...]
        if glu_z is not None:
            h_in = h_in + z_ref[:, 0:D_MODEL].astype(F32) * jax.nn.sigmoid(z_ref[:, D_MODEL:2 * D_MODEL].astype(F32))
            hin_ref[...] = h_in
        hn = _rms(h_in, wn_ref[...]).astype(BF16)
        hn_ref[...] = hn
        for s in range(n_sh):
            cols = slice(s * f_sh, (s + 1) * f_sh)
            a = jnp.dot(hn, wu_ref[s], preferred_element_type=F32)
            a_ref[:, cols] = a.astype(BF16)
            act = jnp.maximum(a, 0.0)
            act_s[:, cols] = (act * act).astype(BF16)
        out = h_in + jnp.dot(act_s[...], wd_ref[...], preferred_element_type=F32)
        o_ref[...] = out
        if next_norm is not None:
            u_ref[...] = _rms(out, nn_ref[...])

    row = lambda i: (i, 0)
    vec = pl.BlockSpec((1, D_MODEL), lambda i: (0, 0))
    in_specs = [pl.BlockSpec((tm, D_MODEL), row), vec,
                pl.BlockSpec((n_sh, None, D_MODEL, f_sh), lambda i: (0, 0, 0, 0), pipeline_mode=pl.Buffered(1)),
                pl.BlockSpec((D_FF, D_MODEL), lambda i: (0, 0), pipeline_mode=pl.Buffered(1))]
    out_specs = [pl.BlockSpec((tm, D_MODEL), row), pl.BlockSpec((tm, D_FF), row), pl.BlockSpec((tm, D_MODEL), row)]
    out_shape = [SDS((n_rows, D_MODEL), F32), SDS((n_rows, D_FF), BF16), SDS((n_rows, D_MODEL), BF16)]
    args = [h, wn, w_up4, w_down]
    if next_norm is not None:
        in_specs.append(vec)
        args.append(next_norm)
    if glu_z is not None:
        in_specs.append(pl.BlockSpec((tm, 2 * D_MODEL), row))
        args.append(glu_z)
    for extra in (next_norm, glu_z):
        if extra is not None:
            out_specs.append(pl.BlockSpec((tm, D_MODEL), row))
            out_shape.append(SDS((n_rows, D_MODEL), F32))
    return pl.pallas_call(
        body, name=name, grid=(n_rows // tm,), in_specs=in_specs, out_specs=out_specs, out_shape=out_shape,
        scratch_shapes=[pltpu.VMEM((tm, D_FF), BF16)],
        compiler_params=_params(1),
    )(*args)


def _mlp_bwd_shard(s, dh, a, hn, dhn_prev, h, wn, w_up4, w_down4, dw_up_buf, dw_down_buf, name):
    n_rows = dh.shape[0]
    n_sh = w_up4.shape[0]
    f_sh = D_FF // n_sh
    tm = _row_tile(n_rows, MLP_BWD_TILES)
    n_i = n_rows // tm
    last = h is not None
    nt = (((1,), (1,)), ((), ()))
    tn = (((0,), (0,)), ((), ()))

    def body(*refs):
        refs = list(refs)
        dh_ref, a_ref, hn_ref, wu_ref, wd_ref = refs[:5]
        at = 5
        prev_ref = None
        if dhn_prev is not None:
            prev_ref = refs[at]
            at += 1
        if last:
            h_ref, wn_ref = refs[at:at + 2]
            at += 2
        if dw_up_buf is not None:
            at += 2
        o_ref, dwu_ref, dwd_ref = refs[at:at + 3]
        at += 3
        if last:
            dnorm_ref = refs[at]
            at += 1
        acc_u, acc_d = refs[at:at + 2]
        i = pl.program_id(0)

        @pl.when(i == 0)
        def _():
            acc_u[...] = jnp.zeros_like(acc_u)
            acc_d[...] = jnp.zeros_like(acc_d)
            if last:
                dnorm_ref[...] = jnp.zeros_like(dnorm_ref)

        dh16 = dh_ref[...].astype(BF16)
        r = jnp.maximum(a_ref[...].astype(F32), 0.0)
        dact = lax.dot_general(dh16, wd_ref[...], nt, preferred_element_type=F32)
        da16 = (dact * (2.0 * r)).astype(BF16)
        acc_d[...] += lax.dot_general((r * r).astype(BF16), dh16, tn, preferred_element_type=F32)
        acc_u[...] += lax.dot_general(hn_ref[...], da16, tn, preferred_element_type=F32)
        dhn = lax.dot_general(da16, wu_ref[...], nt, preferred_element_type=F32)
        if prev_ref is not None:
            dhn = dhn + prev_ref[...]
        if last:
            d_rms, dw_rows = _rms_bwd(dhn, h_ref[...], wn_ref[...])
            o_ref[...] = dh_ref[...] + d_rms
            dnorm_ref[...] += _fold8(dw_rows)
        else:
            o_ref[...] = dhn

        @pl.when(i == n_i - 1)
        def _():
            dwu_ref[...] = acc_u[...].astype(BF16)
            dwd_ref[...] = acc_d[...].astype(BF16)

    row = lambda i: (i, 0)
    tile = pl.BlockSpec((tm, D_MODEL), row)
    in_specs = [tile, pl.BlockSpec((tm, f_sh), lambda i: (i, s)), tile,
                pl.BlockSpec((None, None, D_MODEL, f_sh), lambda i: (s, 0, 0, 0)),
                pl.BlockSpec((None, None, f_sh, D_MODEL), lambda i: (s, 0, 0, 0))]
    args = [dh, a, hn, w_up4, w_down4]
    if dhn_prev is not None:
        in_specs.append(tile)
        args.append(dhn_prev)
    if last:
        in_specs += [tile, pl.BlockSpec((1, D_MODEL), lambda i: (0, 0))]
        args += [h, wn]
    aliases = {}
    if dw_up_buf is not None:
        aliases = {len(args): 1, len(args) + 1: 2}
        in_specs += [pl.BlockSpec(memory_space=pl.ANY)] * 2
        args += [dw_up_buf, dw_down_buf]
    out_specs = [tile, pl.BlockSpec((None, D_MODEL, f_sh), lambda i: (s, 0, 0)),
                 pl.BlockSpec((None, f_sh, D_MODEL), lambda i: (s, 0, 0))]
    out_shape = [SDS((n_rows, D_MODEL), F32), SDS((n_sh, D_MODEL, f_sh), BF16), SDS((n_sh, f_sh, D_MODEL), BF16)]
    if last:
        out_specs.append(pl.BlockSpec((8, D_MODEL), lambda i: (0, 0)))
        out_shape.append(SDS((8, D_MODEL), F32))
    return pl.pallas_call(
        body, name=name, grid=(n_i,), in_specs=in_specs, out_specs=out_specs, out_shape=out_shape,
        input_output_aliases=aliases,
        scratch_shapes=[pltpu.VMEM((D_MODEL, f_sh), F32), pltpu.VMEM((f_sh, D_MODEL), F32)],
        compiler_params=_params(1),
    )(*args)


def _attn_masks(n):
    qi = lax.broadcasted_iota(jnp.int32, (BLOCK, 3 * BLOCK), 0)
    col = lax.broadcasted_iota(jnp.int32, (BLOCK, 3 * BLOCK), 1)
    kj = col - BLOCK
    dist = BLOCK + qi - kj
    kmin = jnp.where(n == 0, 2 * BLOCK, jnp.where(n == 1, BLOCK, 0))
    band_ok = (col >= BLOCK) & (dist >= 0) & (dist < BLOCK) & (kj >= kmin)
    q_pos = n * BLOCK + qi - PAD
    meta_ok = (col >= PAD) & (col < BLOCK) & (col - PAD <= q_pos)
    distf = jnp.where(col >= BLOCK, dist, 0).astype(F32)
    return band_ok | meta_ok, distf


def _alibi_slope(h):
    return float(2.0 ** (-8.0 * (h + 1) / N_HEADS))


def _attn_bias(n, bias_s):
    ok, distf = _attn_masks(n)
    for h in range(N_HEADS):
        bias_s[h] = jnp.where(ok, -_alibi_slope(h) * distf, NEG_INF)


def _attn_fwd(qkv, sinks, n_ex, nb):
    n_rows = qkv.shape[0]
    kvb = N_HEADS * HEAD_DIM // KV_DIM

    def body(sink_ref, q_ref, kvm_ref, kvp_ref, kvc_ref, o_ref, lse_ref, k_s, v_s, q_s, bias_s):
        n = pl.program_id(1)

        @pl.when(n <= 2)
        def _():
            _attn_bias(n, bias_s)

        v_s[...] = jnp.ones_like(v_s)
        for part, ref in enumerate((kvm_ref, kvp_ref, kvc_ref)):
            rows = slice(part * BLOCK, (part + 1) * BLOCK)
            k_s[rows, :] = ref[:, 0:N_KV * HEAD_DIM]
            for kv in range(N_KV):
                v_s[rows, kv * 2 * HEAD_DIM:kv * 2 * HEAD_DIM + HEAD_DIM] = \
                    ref[:, (N_KV + kv) * HEAD_DIM:(N_KV + kv + 1) * HEAD_DIM]
        for kv in range(N_KV):
            for g in range(GQA):
                h = kv * GQA + g
                q_s[kv, g * BLOCK:(g + 1) * BLOCK, :] = q_ref[:, h * HEAD_DIM:(h + 1) * HEAD_DIM] * (HEAD_DIM ** -0.5)
            s4 = lax.dot_general(q_s[kv], k_s[:, kv * HEAD_DIM:(kv + 1) * HEAD_DIM], (((1,), (1,)), ((), ())),
                                 preferred_element_type=F32)
            es, ms, sink_es = [], [], []
            for g in range(GQA):
                h = kv * GQA + g
                s = s4[g * BLOCK:(g + 1) * BLOCK] + bias_s[h]
                sink = sink_ref[0, h]
                m = jnp.maximum(jnp.max(s, axis=-1, keepdims=True), sink)
                es.append(jnp.exp(s - m).astype(BF16))
                ms.append(m)
                sink_es.append(jnp.exp(sink - m))
            pv = jnp.dot(jnp.concatenate(es, axis=0), v_s[:, kv * 2 * HEAD_DIM:(kv + 1) * 2 * HEAD_DIM],
                         preferred_element_type=F32)
            for g in range(GQA):
                h = kv * GQA + g
                pg = pv[g * BLOCK:(g + 1) * BLOCK]
                l = pg[:, HEAD_DIM:HEAD_DIM + 1] + sink_es[g]
                o_ref[:, h * HEAD_DIM:(h + 1) * HEAD_DIM] = (pg[:, 0:HEAD_DIM] * (1.0 / l)).astype(BF16)
                lse_ref[:, h:h + 1] = ms[g] + jnp.log(l)

    return pl.pallas_call(
        body, name="attn_fwd", grid=(n_ex, nb),
        in_specs=[pl.BlockSpec(memory_space=pltpu.SMEM),
                  pl.BlockSpec((BLOCK, N_HEADS * HEAD_DIM), lambda b, n: (b * nb + n, 0)),
                  pl.BlockSpec((BLOCK, KV_DIM), lambda b, n: (b * nb, kvb)),
                  pl.BlockSpec((BLOCK, KV_DIM), lambda b, n: (b * nb + jnp.maximum(n - 1, 0), kvb)),
                  pl.BlockSpec((BLOCK, KV_DIM), lambda b, n: (b * nb + n, kvb))],
        out_specs=[pl.BlockSpec((BLOCK, N_HEADS * HEAD_DIM), lambda b, n: (b * nb + n, 0)),
                   pl.BlockSpec((BLOCK, N_HEADS), lambda b, n: (b * nb + n, 0))],
        out_shape=[SDS((n_rows, N_HEADS * HEAD_DIM), BF16), SDS((n_rows, N_HEADS), F32)],
        scratch_shapes=[pltpu.VMEM((3 * BLOCK, N_KV * HEAD_DIM), BF16), pltpu.VMEM((3 * BLOCK, 2 * N_KV * HEAD_DIM), BF16),
                        pltpu.VMEM((N_KV, GQA * BLOCK, HEAD_DIM), BF16), pltpu.VMEM((N_HEADS, BLOCK, 3 * BLOCK), F32)],
        compiler_params=_params(2),
    )(sinks, qkv, qkv, qkv, qkv)


def _attn_bwd(qkv, sinks, o, lse, do, n_ex, nb):
    n_rows = qkv.shape[0]
    kvb = N_HEADS * HEAD_DIM // KV_DIM
    scale = HEAD_DIM ** -0.5
    nq = lambda r: nb - 1 - r

    def body(sink_ref, q_ref, kvm_ref, kvp_ref, kvc_ref, o_ref, lse_ref, do_ref, dqkv_ref, dsink_ref,
             k_s, v_s, dkv_s, carry_s, meta_s, q_s, do_s, bias_s):
        b, r = pl.program_id(0), pl.program_id(1)
        n = nq(r)

        @pl.when((r == 0) | (n <= 1))
        def _():
            _attn_bias(n, bias_s)

        @pl.when((b == 0) & (r == 0))
        def _():
            dsink_ref[...] = jnp.zeros_like(dsink_ref)

        @pl.when(r == 0)
        def _():
            carry_s[...] = jnp.zeros_like(carry_s)
            meta_s[...] = jnp.zeros_like(meta_s)

        for part, ref in enumerate((kvm_ref, kvp_ref, kvc_ref)):
            k_s[part * BLOCK:(part + 1) * BLOCK, :] = ref[:, 0:N_KV * HEAD_DIM]
            v_s[part * BLOCK:(part + 1) * BLOCK, :] = ref[:, N_KV * HEAD_DIM:KV_DIM]
        nt = (((1,), (1,)), ((), ()))
        tn = (((0,), (0,)), ((), ()))
        for kv in range(N_KV):
            kcols = slice(kv * HEAD_DIM, (kv + 1) * HEAD_DIM)
            vcols = slice(N_KV * HEAD_DIM + kv * HEAD_DIM, N_KV * HEAD_DIM + (kv + 1) * HEAD_DIM)
            for g in range(GQA):
                cols = slice((kv * GQA + g) * HEAD_DIM, (kv * GQA + g + 1) * HEAD_DIM)
                q_s[kv, g * BLOCK:(g + 1) * BLOCK, :] = q_ref[:, cols] * scale
                do_s[kv, g * BLOCK:(g + 1) * BLOCK, :] = do_ref[:, cols]
            kh, vh = k_s[:, kcols], v_s[:, kcols]
            s4 = lax.dot_general(q_s[kv], kh, nt, preferred_element_type=F32)
            dp4 = lax.dot_general(do_s[kv], vh, nt, preferred_element_type=F32)
            ps, dss = [], []
            for g in range(GQA):
                h = kv * GQA + g
                cols = slice(h * HEAD_DIM, (h + 1) * HEAD_DIM)
                rows = slice(g * BLOCK, (g + 1) * BLOCK)
                s = s4[rows] + bias_s[h]
                lse_h = lse_ref[:, h:h + 1]
                p = jnp.exp(s - lse_h)
                delta = jnp.sum(do_ref[:, cols].astype(F32) * o_ref[:, cols].astype(F32), axis=-1, keepdims=True)
                dsink_ref[:, h:h + 1] += -jnp.exp(sink_ref[0, h] - lse_h) * delta
                ps.append(p.astype(BF16))
                dss.append((p * (dp4[rows] - delta)).astype(BF16))
            p4, ds4 = jnp.concatenate(ps, axis=0), jnp.concatenate(dss, axis=0)
            dq4 = jnp.dot(ds4, kh, preferred_element_type=F32) * scale
            for g in range(GQA):
                cols = slice((kv * GQA + g) * HEAD_DIM, (kv * GQA + g + 1) * HEAD_DIM)
                dqkv_ref[:, cols] = dq4[g * BLOCK:(g + 1) * BLOCK].astype(BF16)
            dkv_s[:, kcols] = lax.dot_general(ds4, q_s[kv], tn, preferred_element_type=F32)
            dkv_s[:, vcols] = lax.dot_general(p4, do_s[kv], tn, preferred_element_type=F32)

        meta_s[...] += dkv_s[0:BLOCK, :]
        cur = dkv_s[2 * BLOCK:3 * BLOCK, :] + carry_s[...]
        carry_s[...] = dkv_s[BLOCK:2 * BLOCK, :]

        @pl.when(n > 0)
        def _():
            dqkv_ref[:, N_HEADS * HEAD_DIM:QKV_DIM] = cur.astype(BF16)

        @pl.when(n == 0)
        def _():
            dqkv_ref[:, N_HEADS * HEAD_DIM:QKV_DIM] = (cur + meta_s[...]).astype(BF16)

    blk = lambda b, r: (b * nb + nq(r), 0)
    return pl.pallas_call(
        body, name="attn_bwd", grid=(n_ex, nb),
        in_specs=[pl.BlockSpec(memory_space=pltpu.SMEM),
                  pl.BlockSpec((BLOCK, N_HEADS * HEAD_DIM), blk),
                  pl.BlockSpec((BLOCK, KV_DIM), lambda b, r: (b * nb, kvb)),
                  pl.BlockSpec((BLOCK, KV_DIM), lambda b, r: (b * nb + jnp.maximum(nq(r) - 1, 0), kvb)),
                  pl.BlockSpec((BLOCK, KV_DIM), lambda b, r: (b * nb + nq(r), kvb)),
                  pl.BlockSpec((BLOCK, N_HEADS * HEAD_DIM), blk),
                  pl.BlockSpec((BLOCK, N_HEADS), blk),
                  pl.BlockSpec((BLOCK, N_HEADS * HEAD_DIM), blk)],
        out_specs=[pl.BlockSpec((BLOCK, QKV_DIM), blk),
                   pl.BlockSpec((BLOCK, N_HEADS), lambda b, r: (0, 0))],
        out_shape=[SDS((n_rows, QKV_DIM), BF16), SDS((BLOCK, N_HEADS), F32)],
        scratch_shapes=[pltpu.VMEM((3 * BLOCK, N_KV * HEAD_DIM), BF16), pltpu.VMEM((3 * BLOCK, N_KV * HEAD_DIM), BF16),
                        pltpu.VMEM((3 * BLOCK, KV_DIM), F32), pltpu.VMEM((BLOCK, KV_DIM), F32),
                        pltpu.VMEM((BLOCK, KV_DIM), F32), pltpu.VMEM((N_KV, GQA * BLOCK, HEAD_DIM), BF16),
                        pltpu.VMEM((N_KV, GQA * BLOCK, HEAD_DIM), BF16), pltpu.VMEM((N_HEADS, BLOCK, 3 * BLOCK), F32)],
        compiler_params=_params(2),
    )(sinks, qkv, qkv, qkv, qkv, o, lse, do)


SSM_TILES = (1408, 384)
XW = 256 * PAIRS_PER_CHUNK


def _cmul_add(xr, xi, mr, mi, sr, si):
    return xr + mr * sr - mi * si, xi + mr * si + mi * sr


def _to_segments(src_ref, dst, seg):
    for s in range(seg):
        dst[s * 8:(s + 1) * 8, :] = src_ref[pl.ds(s, 8, stride=seg), :]


def _from_segments(src, i, seg):
    return src[pl.ds(i, seg, stride=8), :]


def _scan_segments(buf, tab_ref, carry_s, seg, reverse):
    shifts = (7, 6, 4) if reverse else (1, 2, 4)
    row_id = lax.broadcasted_iota(jnp.int32, (8, 128), 0)

    def local(si, prev):
        s = (seg - 1 - si) if reverse else si
        row = pl.multiple_of(s * 8, 8)
        out = []
        for j in range(PAIRS_PER_CHUNK):
            re, im = slice(256 * j, 256 * j + 128), slice(256 * j + 128, 256 * j + 256)
            xr, xi = _cmul_add(buf[pl.ds(row, 8), re], buf[pl.ds(row, 8), im],
                               tab_ref[j, 0], tab_ref[j, 1], prev[2 * j], prev[2 * j + 1])
            buf[pl.ds(row, 8), re] = xr
            buf[pl.ds(row, 8), im] = xi
            out += [xr, xi]
        return tuple(out)

    zero = jnp.zeros((8, 128), F32)
    edge = lax.fori_loop(0, seg, local, (zero,) * (2 * PAIRS_PER_CHUNK))

    entering = []
    for j in range(PAIRS_PER_CHUNK):
        er, ei = edge[2 * j], edge[2 * j + 1]
        if reverse:
            sr = jnp.where(row_id == 7, carry_s[2 * j], pltpu.roll(er, 7, 0))
            si_ = jnp.where(row_id == 7, carry_s[2 * j + 1], pltpu.roll(ei, 7, 0))
        else:
            sr = jnp.where(row_id == 0, carry_s[2 * j], pltpu.roll(er, 1, 0))
            si_ = jnp.where(row_id == 0, carry_s[2 * j + 1], pltpu.roll(ei, 1, 0))
        for lvl, sh in enumerate(shifts):
            sr, si_ = _cmul_add(sr, si_, tab_ref[j, 2 + 2 * lvl], tab_ref[j, 3 + 2 * lvl],
                                pltpu.roll(sr, sh, 0), pltpu.roll(si_, sh, 0))
        entering += [sr, si_]
        tr, ti = _cmul_add(er, ei, tab_ref[j, 2], tab_ref[j, 3], sr, si_)
        out_row = slice(0, 1) if reverse else slice(7, 8)
        carry_s[2 * j] = jnp.broadcast_to(tr[out_row], (8, 128))
        carry_s[2 * j + 1] = jnp.broadcast_to(ti[out_row], (8, 128))

    def fix(si, carried):
        s = (seg - 1 - si) if reverse else si
        row = pl.multiple_of(s * 8, 8)
        out = []
        for j in range(PAIRS_PER_CHUNK):
            re, im = slice(256 * j, 256 * j + 128), slice(256 * j + 128, 256 * j + 256)
            ar, ai, fr, fi = tab_ref[j, 0], tab_ref[j, 1], carried[2 * j], carried[2 * j + 1]
            fr, fi = ar * fr - ai * fi, ar * fi + ai * fr
            buf[pl.ds(row, 8), re] += fr
            buf[pl.ds(row, 8), im] += fi
            out += [fr, fi]
        return tuple(out)

    lax.fori_loop(0, seg, fix, tuple(entering))


def _ssm_fwd(u, b_pad, c_pad, tab, d_skip, n_ex, lp):
    n_rows = u.shape[0]
    TM = _row_tile(lp, SSM_TILES)
    SEG = TM // 8
    n_t = lp // TM
    n_chunk = D_MODEL // 128

    def body(u_ref, bp_ref, cp_ref, tab_ref, d_ref, yg_ref, y_ref, xs_ref, buf, carry_s, us, ys):
        @pl.when(pl.program_id(2) == 0)
        def _():
            carry_s[...] = jnp.zeros_like(carry_s)

        _to_segments(u_ref, us, SEG)
        ub = us[...]
        u16 = ub.astype(BF16)
        buf[...] = jnp.dot(u16, bp_ref[...], preferred_element_type=F32)
        _scan_segments(buf, tab_ref, carry_s, SEG, reverse=False)
        xb = buf[...].astype(BF16)
        xs_ref[...] = xb
        ys[...] = d_ref[...] * ub + jnp.dot(xb, cp_ref[...], preferred_element_type=F32)
        for i in range(8):
            yi = _from_segments(ys, i, SEG)
            y_ref[i * SEG:(i + 1) * SEG, :] = yi
            yg_ref[i * SEG:(i + 1) * SEG, :] = _gelu(yi).astype(BF16)

    rows = lambda b, q, t: (b * n_t + t, q)
    return pl.pallas_call(
        body, name="ssm_fwd", grid=(n_ex, n_chunk, n_t),
        in_specs=[pl.BlockSpec((TM, 128), rows),
                  pl.BlockSpec((None, 128, XW), lambda b, q, t: (q, 0, 0)),
                  pl.BlockSpec((None, XW, 128), lambda b, q, t: (q, 0, 0)),
                  pl.BlockSpec((PAIRS_PER_CHUNK, 8, 8, 128), lambda b, q, t: (q, 0, 0, 0)),
                  pl.BlockSpec((1, 128), lambda b, q, t: (0, q))],
        out_specs=[pl.BlockSpec((TM, 128), rows), pl.BlockSpec((TM, 128), rows),
                   pl.BlockSpec((None, TM, XW), lambda b, q, t: (q, b * n_t + t, 0))],
        out_shape=[SDS((n_rows, D_MODEL), BF16), SDS((n_rows, D_MODEL), F32), SDS((n_chunk, n_rows, XW), BF16)],
        scratch_shapes=[pltpu.VMEM((TM, XW), F32), pltpu.VMEM((2 * PAIRS_PER_CHUNK, 8, 128), F32),
                        pltpu.VMEM((TM, 128), F32), pltpu.VMEM((TM, 128), F32)],
        compiler_params=_params(3),
    )(u, b_pad, c_pad, tab, d_skip)


def _ssm_bwd(dyg, y, u, xs, ct_pad, bt_pad, tab_rev, d_skip, n_ex, lp):
    n_rows = u.shape[0]
    TM = _row_tile(lp, SSM_TILES)
    SEG = TM // 8
    n_t = lp // TM
    n_chunk = D_MODEL // 128
    tile = lambda q, b, t: (b * n_t + (n_t - 1 - t), q)

    def body(dyg_ref, y_ref, u_ref, xs_ref, xp_ref, ct_ref, bt_ref, tab_ref, d_ref,
             du_ref, db_ref, dc_ref, da_ref, dd_ref, buf, xf, carry_s, us, dys, dyp):
        b, t = pl.program_id(1), pl.program_id(2)

        @pl.when((b == 0) & (t == 0))
        def _():
            db_ref[...] = jnp.zeros_like(db_ref)
            dc_ref[...] = jnp.zeros_like(dc_ref)
            da_ref[...] = jnp.zeros_like(da_ref)
            dd_ref[...] = jnp.zeros_like(dd_ref)

        @pl.when(t == 0)
        def _():
            carry_s[...] = jnp.zeros_like(carry_s)

        dys[...] = dyg_ref[...].astype(F32) * _gelu_grad(y_ref[...])
        dd_ref[...] += _fold8(dys[...] * u_ref[...])
        _to_segments(dys, dyp, SEG)
        dy = dyp[...]
        _to_segments(u_ref, us, SEG)
        dy16 = dy.astype(BF16)
        first_tile = t == n_t - 1
        tn = (((0,), (0,)), ((), ()))
        buf[...] = jnp.dot(dy16, ct_ref[...], preferred_element_type=F32)
        dc_ref[...] += lax.dot_general(dy16, xs_ref[...], tn, preferred_element_type=F32)
        xf[16:16 + TM, :] = xs_ref[...].astype(F32)
        xf[0:16, :] = jnp.where(first_tile, 0.0, xp_ref[...].astype(F32))
        _scan_segments(buf, tab_ref, carry_s, SEG, reverse=True)
        g16 = buf[...].astype(BF16)
        dys[...] = d_ref[...] * dy + jnp.dot(g16, bt_ref[...], preferred_element_type=F32)
        db_ref[...] += lax.dot_general(us[...].astype(BF16), g16, tn, preferred_element_type=F32)
        row_id = lax.broadcasted_iota(jnp.int32, (8, 128), 0)
        for j in range(PAIRS_PER_CHUNK):
            re, im = slice(256 * j, 256 * j + 128), slice(256 * j + 128, 256 * j + 256)
            first = [jnp.where(row_id == 0, jnp.broadcast_to(xf[15:16, c], (8, 128)),
                               pltpu.roll(xf[8 + TM:16 + TM, c], 1, 0)) for c in (re, im)]
            for rows, pr, pi in ((slice(0, 8), first[0], first[1]),
                                 (slice(8, TM), xf[16:8 + TM, re], xf[16:8 + TM, im])):
                gr, gi = buf[rows, re], buf[rows, im]
                da_ref[j, 0] += _fold8(gr * pr + gi * pi)
                da_ref[j, 1] += _fold8(gi * pr - gr * pi)
        for i in range(8):
            du_ref[i * SEG:(i + 1) * SEG, :] = _from_segments(dys, i, SEG)

    prev16 = lambda q, b, t: (q, jnp.maximum((b * n_t + (n_t - 1 - t)) * (TM // 16) - 1, 0), 0)
    return pl.pallas_call(
        body, name="ssm_bwd", grid=(n_chunk, n_ex, n_t),
        in_specs=[pl.BlockSpec((TM, 128), tile), pl.BlockSpec((TM, 128), tile), pl.BlockSpec((TM, 128), tile),
                  pl.BlockSpec((None, TM, XW), lambda q, b, t: (q, b * n_t + (n_t - 1 - t), 0)),
                  pl.BlockSpec((None, 16, XW), prev16),
                  pl.BlockSpec((None, 128, XW), lambda q, b, t: (q, 0, 0)),
                  pl.BlockSpec((None, XW, 128), lambda q, b, t: (q, 0, 0)),
                  pl.BlockSpec((PAIRS_PER_CHUNK, 8, 8, 128), lambda q, b, t: (q, 0, 0, 0)),
                  pl.BlockSpec((1, 128), lambda q, b, t: (0, q))],
        out_specs=[pl.BlockSpec((TM, 128), tile),
                   pl.BlockSpec((None, 128, XW), lambda q, b, t: (q, 0, 0)),
                   pl.BlockSpec((None, 128, XW), lambda q, b, t: (q, 0, 0)),
                   pl.BlockSpec((PAIRS_PER_CHUNK, 2, 8, 128), lambda q, b, t: (q, 0, 0, 0)),
                   pl.BlockSpec((8, 128), lambda q, b, t: (0, q))],
        out_shape=[SDS((n_rows, D_MODEL), F32), SDS((n_chunk, 128, XW), F32), SDS((n_chunk, 128, XW), F32),
                   SDS((N_PAIR, 2, 8, 128), F32), SDS((8, D_MODEL), F32)],
        scratch_shapes=[pltpu.VMEM((TM, XW), F32), pltpu.VMEM((TM + 16, XW), F32),
                        pltpu.VMEM((2 * PAIRS_PER_CHUNK, 8, 128), F32), pltpu.VMEM((TM, 128), F32),
                        pltpu.VMEM((TM, 128), F32), pltpu.VMEM((TM, 128), F32)],
        compiler_params=_params(3),
    )(dyg, y, u, xs, xs, ct_pad, bt_pad, tab_rev, d_skip)


def _rms_bwd_call(dhn, h, wn, dres, name):
    n_rows = h.shape[0]

    def body(dhn_ref, h_ref, wn_ref, dres_ref, o_ref, dw_ref):
        @pl.when(pl.program_id(0) == 0)
        def _():
            dw_ref[...] = jnp.zeros_like(dw_ref)

        dh, dw_rows = _rms_bwd(dhn_ref[...], h_ref[...], wn_ref[...])
        o_ref[...] = dres_ref[...] + dh
        dw_ref[...] += _fold8(dw_rows)

    row = lambda i: (i, 0)
    return pl.pallas_call(
        body, name=name, grid=(n_rows // TM,),
        in_specs=[pl.BlockSpec((TM, D_MODEL), row), pl.BlockSpec((TM, D_MODEL), row),
                  pl.BlockSpec((1, D_MODEL), lambda i: (0, 0)), pl.BlockSpec((TM, D_MODEL), row)],
        out_specs=[pl.BlockSpec((TM, D_MODEL), row), pl.BlockSpec((8, D_MODEL), lambda i: (0, 0))],
        out_shape=[SDS((n_rows, D_MODEL), F32), SDS((8, D_MODEL), F32)], compiler_params=_params(1),
    )(dhn, h, wn, dres)


def _glu_bwd(dh, z, w4):
    n_rows = dh.shape[0]
    tm = _row_tile(n_rows, MM_TILES)
    n_sh, _, k, n = w4.shape

    def body(dh_ref, z_ref, w_ref, dz_ref, dyg_ref):
        sg = jax.nn.sigmoid(z_ref[:, D_MODEL:2 * D_MODEL].astype(F32))
        d = dh_ref[...]
        dz_ref[:, 0:D_MODEL] = (d * sg).astype(BF16)
        dz_ref[:, D_MODEL:2 * D_MODEL] = (d * z_ref[:, 0:D_MODEL].astype(F32) * sg * (1.0 - sg)).astype(BF16)
        acc = None
        for s in range(n_sh):
            part = lax.dot_general(dz_ref[:, s * n:(s + 1) * n], w_ref[s], (((1,), (1,)), ((), ())),
                                   preferred_element_type=F32)
            acc = part if acc is None else acc + part
        dyg_ref[...] = acc.astype(BF16)

    row = lambda i: (i, 0)
    return pl.pallas_call(
        body, name="glu_bwd", grid=(n_rows // tm,),
        in_specs=[pl.BlockSpec((tm, D_MODEL), row), pl.BlockSpec((tm, 2 * D_MODEL), row), _w4_spec(w4)],
        out_specs=[pl.BlockSpec((tm, 2 * D_MODEL), row), pl.BlockSpec((tm, k), row)],
        out_shape=[SDS((n_rows, 2 * D_MODEL), BF16), SDS((n_rows, k), BF16)], compiler_params=_params(1),
    )(dh, z, w4)


def _loss_head(h, wn, target, n_ex, nb):
    n_rows = h.shape[0]
    per_tile = TM // BLOCK
    n_tiles = nb // per_tile

    def body(h_ref, wn_ref, *rest):
        t_refs, (dh_ref, loss_ref, dw_ref) = rest[:per_tile], rest[per_tile:]
        b, j = pl.program_id(0), pl.program_id(1)

        @pl.when((b == 0) & (j == 0))
        def _():
            loss_ref[...] = jnp.zeros_like(loss_ref)
            dw_ref[...] = jnp.zeros_like(dw_ref)

        def block(k):
            rows = slice(k * BLOCK, (k + 1) * BLOCK)
            hh = h_ref[rows, :]
            diff = _rms(hh, wn_ref[...]) - t_refs[k][...]
            loss_ref[...] += 0.5 * jnp.sum(diff * diff) * (1.0 / D_MODEL)
            dh, dw_rows = _rms_bwd(diff * (1.0 / D_MODEL), hh, wn_ref[...])
            dh_ref[rows, :] = dh
            dw_ref[...] += _fold8(dw_rows)

        @pl.when(j == 0)
        def _():
            dh_ref[0:BLOCK, :] = jnp.zeros((BLOCK, D_MODEL), F32)

        pl.when(j > 0)(lambda: block(0))
        for k in range(1, per_tile):
            block(k)

    def t_spec(k):
        return pl.BlockSpec((BLOCK, D_MODEL), lambda b, j: (b * (nb - 1) + jnp.maximum(per_tile * j + k - 1, 0), 0))

    tile = pl.BlockSpec((TM, D_MODEL), lambda b, j: (b * n_tiles + j, 0))
    return pl.pallas_call(
        body, name="loss_head", grid=(n_ex, n_tiles),
        in_specs=[tile, pl.BlockSpec((1, D_MODEL), lambda b, j: (0, 0))] + [t_spec(k) for k in range(per_tile)],
        out_specs=[tile, pl.BlockSpec((8, 128), lambda b, j: (0, 0)), pl.BlockSpec((8, D_MODEL), lambda b, j: (0, 0))],
        out_shape=[SDS((n_rows, D_MODEL), F32), SDS((8, 128), F32), SDS((8, D_MODEL), F32)],
        compiler_params=_params(2),
    )(h, wn, *([target] * per_tile))


def _adamw(pieces, w, m, v, name):
    n_layers = len(pieces)
    rows, cols = pieces[0].shape[1:]
    rb = rows
    for cand in (256, 136, 128, 64, 32, 16, 8):
        if rows % cand == 0 and rows > cand:
            rb = cand
            break
    n_blk = rows // rb
    c1 = 1.0 / (1.0 - ADAM_B1 ** ADAM_STEP)
    c2 = 1.0 / (1.0 - ADAM_B2 ** ADAM_STEP)

    def body(*refs):
        p_refs = refs[:n_layers]
        w_ref, m_ref, v_ref, g_out, d_out, m_out, v_out = refs[n_layers:]
        layer = pl.program_id(0)
        g = None
        for l, p_ref in enumerate(p_refs):
            gl = p_ref[0].astype(F32)
            for k in range(1, N_DEV):
                gl = gl + p_ref[k].astype(F32)
            g = gl if g is None else jnp.where(layer == l, gl, g)
        m_new = ADAM_B1 * m_ref[...] + (1.0 - ADAM_B1) * g
        v_new = ADAM_B2 * v_ref[...] + (1.0 - ADAM_B2) * (g * g)
        g_out[...] = g
        m_out[...] = m_new
        v_out[...] = v_new
        d_out[...] = -ADAM_LR * ((m_new * c1) / (jnp.sqrt(v_new * c2) + ADAM_EPS) + ADAM_WD * w_ref[...])

    def piece_spec(l):
        return pl.BlockSpec((N_DEV, rb, cols), lambda ly, i: (0, jnp.where(ly == l, i, 0), 0))

    blk = pl.BlockSpec((rb, cols), lambda ly, i: (ly * n_blk + i, 0))
    return pl.pallas_call(
        body, name=name, grid=(n_layers, n_blk),
        in_specs=[piece_spec(l) for l in range(n_layers)] + [blk, blk, blk],
        out_specs=[blk, blk, blk, blk],
        out_shape=[SDS((n_layers * rows, cols), F32)] * 4, compiler_params=_params(2),
    )(*pieces, w, m, v)


_HBM = pl.BlockSpec(memory_space=pltpu.HBM)
_SEM = pl.BlockSpec(memory_space=pltpu.SEMAPHORE)
_EFFECT = pltpu.SideEffectType.DATAFLOW_SIDE_EFFECTING
N_GATHER_PEERS = N_CHIPS - 1
N_EXCHANGE_PEERS = N_DEV - 1


def _gather_copies(srcs, lands, send_sems, recv_sems):
    x, y, c = lax.axis_index("x"), lax.axis_index("y"), lax.axis_index("c")
    mine = 2 * x + y
    chips = [(1 - x, y), (x, 1 - y), (1 - x, 1 - y)]
    out, inc = [], []
    for a in range(len(srcs)):
        for k, (px, py) in enumerate(chips):
            j = a * N_GATHER_PEERS + k
            sems = dict(send_sem=send_sems.at[j], recv_sem=recv_sems.at[j], device_id=(px, py, c),
                        device_id_type=pl.DeviceIdType.MESH)
            out.append(pltpu.make_async_remote_copy(src_ref=srcs[a], dst_ref=lands[a].at[mine], **sems))
            inc.append(pltpu.make_async_remote_copy(src_ref=srcs[a], dst_ref=lands[a].at[2 * px + py], **sems))
    return out, inc


def _exchange_copies(n_scatter):
    def copies(srcs, lands, send_sems, recv_sems):
        x, y, c = lax.axis_index("x"), lax.axis_index("y"), lax.axis_index("c")
        me = 4 * x + 2 * y + c
        peers = [(x ^ (k >> 2), y ^ ((k >> 1) & 1), c ^ (k & 1)) for k in range(1, N_DEV)]
        out, inc = [], []
        for a in range(len(srcs)):
            for k, (px, py, pc) in enumerate(peers):
                j = a * N_EXCHANGE_PEERS + k
                sems = dict(send_sem=send_sems.at[j], recv_sem=recv_sems.at[j], device_id=(px, py, pc),
                            device_id_type=pl.DeviceIdType.MESH)
                theirs = srcs[a].at[2 * px + py] if a < n_scatter else srcs[a]
                mine = srcs[a].at[2 * x + y] if a < n_scatter else srcs[a]
                out.append(pltpu.make_async_remote_copy(src_ref=theirs, dst_ref=lands[a].at[me], **sems))
                inc.append(pltpu.make_async_remote_copy(src_ref=mine, dst_ref=lands[a].at[4 * px + 2 * py + pc], **sems))
        return out, inc

    return copies


def _split_start(groups, copies_fn, n_peers, name):
    sizes = [len(srcs) for srcs, _ in groups]
    flat = [a for srcs, lands in groups for a in list(srcs) + list(lands)]
    n_flat, n_grp = len(flat), len(groups)

    def body(*refs):
        sems = refs[2 * n_flat:2 * n_flat + 2 * n_grp]
        token = refs[-1]
        at = 0
        for gi, n in enumerate(sizes):
            out, _ = copies_fn(refs[at:at + n], refs[at + n:at + 2 * n], sems[2 * gi], sems[2 * gi + 1])
            for cp in out:
                cp.start()
            at += 2 * n
        token[...] = jnp.zeros_like(token)

    sem_shapes = []
    for n in sizes:
        sem_shapes += [pltpu.SemaphoreType.DMA((n * n_peers,)), pltpu.SemaphoreType.DMA((n * n_peers,))]
    res = pl.pallas_call(
        body, name=name,
        out_shape=(*[pltpu.HBM(a.shape, a.dtype) for a in flat], *sem_shapes, SDS((8, 128), F32)),
        in_specs=[_HBM] * n_flat,
        out_specs=(*[_HBM] * n_flat, *[_SEM] * (2 * n_grp), pl.BlockSpec(memory_space=pltpu.VMEM)),
        input_output_aliases={i: i for i in range(n_flat)},
        compiler_params=pltpu.CompilerParams(has_side_effects=_EFFECT),
    )(*[pltpu.with_memory_space_constraint(a, pltpu.HBM) for a in flat])
    handles, at = [], 0
    for gi, n in enumerate(sizes):
        handles.append((res[n_flat + 2 * gi], res[n_flat + 2 * gi + 1], list(res[at:at + n]), list(res[at + n:at + 2 * n])))
        at += 2 * n
    return handles, res[-1]


def _split_wait(handle, after, copies_fn, name):
    send_sems, recv_sems, srcs, lands = handle
    n = len(srcs)
    after = list(after) if isinstance(after, (list, tuple)) else [after]

    def body(*refs):
        out, inc = copies_fn(refs[:n], refs[n:2 * n], refs[2 * n], refs[2 * n + 1])
        for cp in out:
            cp.wait_send()
        for cp in inc:
            cp.wait_recv()

    flat = list(srcs) + list(lands)
    res = pl.pallas_call(
        body, name=name,
        out_shape=tuple(pltpu.HBM(a.shape, a.dtype) for a in flat),
        in_specs=[_HBM] * (2 * n) + [_SEM, _SEM] + [pl.BlockSpec(memory_space=pl.ANY)] * len(after),
        out_specs=tuple([_HBM] * (2 * n)),
        input_output_aliases={i: i for i in range(2 * n)},
        compiler_params=pltpu.CompilerParams(has_side_effects=_EFFECT),
    )(*flat, send_sems, recv_sems, *after)
    return list(res[n:])


def _landing(own, slot, n_slots):
    return lax.dynamic_update_index_in_dim(lax.empty((n_slots,) + own.shape, own.dtype), own, slot, 0)


def _ssm_discretize(lam_re, lam_im, log_dt, b_re, b_im):
    lr = jnp.minimum(lam_re, LAMBDA_RE_MAX)
    li = lam_im
    dt = jnp.exp(log_dt)[:, None]
    mag = jnp.exp(lr * dt)
    ar, ai = mag * jnp.cos(li * dt), mag * jnp.sin(li * dt)
    den = lr * lr + li * li
    nr, ni = ar - 1.0, ai
    gr, gi = (nr * lr + ni * li) / den, (ni * lr - nr * li) / den
    bbr = gr[:, :, None] * b_re - gi[:, :, None] * b_im
    bbi = gr[:, :, None] * b_im + gi[:, :, None] * b_re
    return ar, ai, bbr, bbi


def _pair_lanes(t):
    return t.reshape(N_PAIR, 2 * SSM_STATE)


def _chan_state_blocks(t_gcp):
    t = t_gcp.reshape(N_PAIR, 2, SSM_GROUP, SSM_STATE)
    eye2 = jnp.eye(2, dtype=t.dtype)
    blk = jnp.einsum("rgcp,gh->rgchp", t, eye2).reshape(N_PAIR, 2 * SSM_GROUP, 2 * SSM_STATE)
    place = jax.nn.one_hot(jnp.arange(N_PAIR) % PAIRS_PER_CHUNK, PAIRS_PER_CHUNK, dtype=t.dtype)
    return jnp.einsum("rcl,rj->rjcl", blk, place).reshape(N_PAIR, 128, 2 * SSM_STATE)


def _chan_state_unblock(t):
    t = t.reshape(N_PAIR, PAIRS_PER_CHUNK, 2, SSM_GROUP, 2, SSM_STATE)
    place = jax.nn.one_hot(jnp.arange(N_PAIR) % PAIRS_PER_CHUNK, PAIRS_PER_CHUNK, dtype=t.dtype)
    t = jnp.einsum("rjgchp,rj->rgchp", t, place)
    t = jnp.einsum("rgchp,gh->rgcp", t, jnp.eye(2, dtype=t.dtype))
    return t.reshape(SSM_NG, SSM_GROUP, SSM_STATE)


def _scan_tables(zr, zi, reverse, seg):
    zr, zi = _pair_lanes(zr), _pair_lanes(-zi if reverse else zi)
    a = (jnp.exp(zr) * jnp.cos(zi), jnp.exp(zr) * jnp.sin(zi))
    cmul = lambda p, q: (p[0] * q[0] - p[1] * q[1], p[0] * q[1] + p[1] * q[0])
    big, square, bits = None, a, seg
    while bits:
        if bits & 1:
            big = square if big is None else cmul(big, square)
        square, bits = cmul(square, square), bits >> 1
    powers = [a, big]
    for _ in range(2):
        powers.append(cmul(powers[-1], powers[-1]))
    rows = jnp.arange(8)[None, :, None]
    tiles = [jnp.broadcast_to(part[:, None, :], (N_PAIR, 8, 128)) for part in powers[0]]
    for lvl, step in enumerate((1, 2, 4)):
        keep = (rows <= 7 - step) if reverse else (rows >= step)
        for part in powers[1 + lvl]:
            tiles.append(jnp.where(keep, part[:, None, :], 0.0))
    return jnp.stack(tiles, axis=1)


def _pairs_to_chunks(t):
    n_chunk = N_PAIR // PAIRS_PER_CHUNK
    return jnp.swapaxes(t.reshape(n_chunk, PAIRS_PER_CHUNK, 128, 256), 1, 2).reshape(n_chunk, 128, XW)


def _chunks_to_pairs(t):
    n_chunk = N_PAIR // PAIRS_PER_CHUNK
    return jnp.swapaxes(t.reshape(n_chunk, 128, PAIRS_PER_CHUNK, 256), 1, 2).reshape(N_PAIR, 128, 256)


def _ssm_operands(w, lp):
    seg = _row_tile(lp, SSM_TILES) // 8
    ar, ai, bbr, bbi = _ssm_discretize(w["ssm_lambda_re"], w["ssm_lambda_im"], w["ssm_log_dt"], w["ssm_b_re"], w["ssm_b_im"])
    b_blk = jnp.concatenate([_chan_state_blocks(jnp.swapaxes(bbr, 1, 2)), _chan_state_blocks(jnp.swapaxes(bbi, 1, 2))], axis=2)
    c_blk = jnp.concatenate([_chan_state_blocks(w["ssm_c_re"]), -_chan_state_blocks(w["ssm_c_im"])], axis=2)
    dt = jnp.exp(w["ssm_log_dt"])[:, None]
    zr, zi = jnp.minimum(w["ssm_lambda_re"], LAMBDA_RE_MAX) * dt, w["ssm_lambda_im"] * dt
    b_cat, c_cat = _pairs_to_chunks(b_blk).astype(BF16), _pairs_to_chunks(c_blk).astype(BF16)
    return (b_cat, jnp.swapaxes(b_cat, 1, 2), c_cat, jnp.swapaxes(c_cat, 1, 2),
            _scan_tables(zr, zi, False, seg), _scan_tables(zr, zi, True, seg))


def _local_step(x, target, w, late_weights, on_grads):
    n_ex, seq, _ = x.shape
    lp = seq + BLOCK
    nb = lp // BLOCK
    n_rows = n_ex * lp
    g = {}

    head = jnp.concatenate([jnp.zeros((PAD, D_MODEL), F32), w["meta_tokens"]], axis=0)
    h0 = jnp.concatenate([jnp.broadcast_to(head[None], (n_ex, BLOCK, D_MODEL)), x], axis=1).reshape(n_rows, D_MODEL)

    qkv, hn_a = _rms_mm_cols(h0, w["attn_norm_w"], w["attn_w_qkv"], "qkv_fwd")
    att, lse = _attn_fwd(qkv, w["attn_sinks"], n_ex, nb)
    h1 = _mm_acc(att, w["attn_w_o"], False, "attn_out_fwd", res=h0)
    w = {**w, **late_weights(0, att)}
    h2, a0, hn_m0, u = _mlp_fwd(h1, w["mlp_norm_w"][0:1], w["mlp_w_up"][0], w["mlp_w_down"][0], "mlp0_fwd",
                                next_norm=w["ssm_norm_w"])
    late = late_weights(1, h2)
    w["ssm_w_glu"] = late["ssm_w_glu"]
    w["mlp_w_up"], w["mlp_w_down"] = w["mlp_w_up"] + late["mlp_w_up"], w["mlp_w_down"] + late["mlp_w_down"]

    ops = w["ssm_operands"] if "ssm_operands" in w else _ssm_operands(w, lp)
    b_pad, bt_pad, ct_pad, c_pad, tab_fwd, tab_rev = ops
    yg, y, xs = _ssm_fwd(u, b_pad, c_pad, tab_fwd, w["ssm_d"], n_ex, lp)
    z = _mm_cols(yg, w["ssm_w_glu"], False, "glu_mm_fwd")
    h4, a1, hn_m1, h3 = _mlp_fwd(h2, w["mlp_norm_w"][1:2], w["mlp_w_up"][1], w["mlp_w_down"][1], "mlp1_fwd", glu_z=z)

    dh4, loss_tile, dnorm_f = _loss_head(h4, w["final_norm_w"], target.reshape(n_ex * seq, D_MODEL), n_ex, nb)

    def mlp_bwd(dh_out, h_in, a, hn, layer, tag, norm_w):
        dhn, dw_up, dw_down = None, None, None
        for s in range(N_CHIPS):
            final = s == N_CHIPS - 1
            res = _mlp_bwd_shard(s, dh_out, a, hn, dhn, h_in if final else None, norm_w,
                                 w["mlp_w_up"][layer], w["mlp_w_down"][layer], dw_up, dw_down, f"{tag}_bwd{s}")
            dhn, dw_up, dw_down = res[:3]
        return dhn, res[3], dw_up, dw_down

    dh3, dnorm_m1, dwu1, dwd1 = mlp_bwd(dh4, h3, a1, hn_m1, 1, "mlp1", w["mlp_norm_w"][1:2])
    tok = on_grads("mlp1", {"mlp_w_up": dwu1, "mlp_w_down": dwd1})
    dz, dyg = _glu_bwd(dh3, z, w["ssm_w_glu"])
    g["ssm_w_glu"] = _mm_tn(yg, dz, N_CHIPS, False, "glu_mm_dw")
    du, db_blk, dc_blk, da_t, dd_t = _ssm_bwd(dyg, y, u, xs, ct_pad, bt_pad, tab_rev, w["ssm_d"] + tok, n_ex, lp)
    dh2, dnorm_s = _rms_bwd_call(du, h2, w["ssm_norm_w"], dh3, "ssm_norm_bwd")
    db_blk, dc_blk = _chunks_to_pairs(db_blk), _chunks_to_pairs(dc_blk)
    g["ssm_c_re"] = _chan_state_unblock(dc_blk[:, :, 0:128])
    g["ssm_c_im"] = -_chan_state_unblock(dc_blk[:, :, 128:256])
    g_bbr = jnp.swapaxes(_chan_state_unblock(db_blk[:, :, 0:128]), 1, 2)
    g_bbi = jnp.swapaxes(_chan_state_unblock(db_blk[:, :, 128:256]), 1, 2)
    g_a = jnp.sum(da_t, axis=2).reshape(N_PAIR, 2, 2, SSM_STATE)
    g_ar, g_ai = g_a[:, 0].reshape(SSM_NG, SSM_STATE), g_a[:, 1].reshape(SSM_NG, SSM_STATE)
    _, vjp = jax.vjp(_ssm_discretize, w["ssm_lambda_re"], w["ssm_lambda_im"], w["ssm_log_dt"], w["ssm_b_re"], w["ssm_b_im"])
    g["ssm_lambda_re"], g["ssm_lambda_im"], g["ssm_log_dt"], g["ssm_b_re"], g["ssm_b_im"] = vjp((g_ar, g_ai, g_bbr, g_bbi))
    tok = on_grads("ssm", g)
    g = {}
    dh1, dnorm_m0, dwu0, dwd0 = mlp_bwd(dh2, h1, a0, hn_m0, 0, "mlp0", w["mlp_norm_w"][0:1] + tok)
    datt = _mm_cols(dh1, w["attn_w_o"], True, "attn_out_dx")
    dw_o = _mm_tn(att, dh1, N_CHIPS, True, "attn_out_dw")
    tok = on_grads("mlp0", {"mlp_w_up": dwu0, "mlp_w_down": dwd0, "attn_w_o": dw_o})
    dqkv, dsink_rows = _attn_bwd(qkv, w["attn_sinks"] + tok, att, lse, datt, n_ex, nb)
    tok = on_grads("qkv", {"attn_w_qkv": _mm_tn(hn_a, dqkv, N_CHIPS, False, "qkv_dw")})
    dh0, dnorm_a = _mm_acc(dqkv, w["attn_w_qkv"], True, "qkv_dx", rms_bwd=(h0, w["attn_norm_w"] + tok, dh1))

    dh0 = dh0.reshape(n_ex, lp, D_MODEL)
    on_grads("rest", {
        "mlp_norm_w": jnp.stack([jnp.sum(dnorm_m0, axis=0), jnp.sum(dnorm_m1, axis=0)]),
        "final_norm_w": jnp.sum(dnorm_f, axis=0),
        "attn_norm_w": jnp.sum(dnorm_a, axis=0)[None],
        "ssm_norm_w": jnp.sum(dnorm_s, axis=0)[None],
        "attn_sinks": jnp.sum(dsink_rows, axis=0)[None],
        "ssm_d": jnp.sum(dd_t, axis=0)[None],
        "meta_tokens": jnp.sum(dh0[:, PAD:BLOCK], axis=0),
        "loss": loss_tile[0, 0:1]})
    return loss_tile, dh0[:, BLOCK:]


_SHARDED_SMALL = ("meta_tokens", "ssm_norm_w", "ssm_d")
_REP_SSM = ("ssm_lambda_re", "ssm_lambda_im", "ssm_log_dt", "ssm_b_re", "ssm_b_im", "ssm_c_re", "ssm_c_im")
_REP_MISC = ("attn_norm_w", "attn_sinks", "mlp_norm_w", "final_norm_w")
_BIG = ("attn_w_qkv", "attn_w_o", "ssm_w_glu", "mlp_w_up", "mlp_w_down")


def _pack(parts, cols):
    flat = jnp.concatenate([p.reshape(-1) for p in parts])
    rows = -(-flat.shape[0] // (8 * cols)) * 8
    return jnp.pad(flat, (0, rows * cols - flat.shape[0])).reshape(rows, cols)


def _unpack(packed, like):
    flat = packed.reshape(-1)
    out, at = [], 0
    for p in like:
        out.append(flat[at:at + p.size].reshape(p.shape))
        at += p.size
    return out


def kernel(x, meta_tokens, attn_norm_w, attn_w_qkv, attn_sinks, attn_w_o, ssm_norm_w, ssm_lambda_re, ssm_lambda_im, ssm_log_dt, ssm_b_re, ssm_b_im, ssm_c_re, ssm_c_im, ssm_d, ssm_w_glu, mlp_norm_w, mlp_w_up, mlp_w_down, final_norm_w, loss_target, m_meta_tokens, m_attn_norm_w, m_attn_w_qkv, m_attn_sinks, m_attn_w_o, m_ssm_norm_w, m_ssm_lambda_re, m_ssm_lambda_im, m_ssm_log_dt, m_ssm_b_re, m_ssm_b_im, m_ssm_c_re, m_ssm_c_im, m_ssm_d, m_ssm_w_glu, m_mlp_norm_w, m_mlp_w_up, m_mlp_w_down, m_final_norm_w, v_meta_tokens, v_attn_norm_w, v_attn_w_qkv, v_attn_sinks, v_attn_w_o, v_ssm_norm_w, v_ssm_lambda_re, v_ssm_lambda_im, v_ssm_log_dt, v_ssm_b_re, v_ssm_b_im, v_ssm_c_re, v_ssm_c_im, v_ssm_d, v_ssm_w_glu, v_mlp_norm_w, v_mlp_w_up, v_mlp_w_down, v_final_norm_w):
    names = ("meta_tokens", "attn_norm_w", "attn_w_qkv", "attn_sinks", "attn_w_o", "ssm_norm_w", "ssm_lambda_re",
             "ssm_lambda_im", "ssm_log_dt", "ssm_b_re", "ssm_b_im", "ssm_c_re", "ssm_c_im", "ssm_d", "ssm_w_glu",
             "mlp_norm_w", "mlp_w_up", "mlp_w_down", "final_norm_w")
    wts = dict(zip(names, (meta_tokens, attn_norm_w, attn_w_qkv, attn_sinks, attn_w_o, ssm_norm_w, ssm_lambda_re,
                           ssm_lambda_im, ssm_log_dt, ssm_b_re, ssm_b_im, ssm_c_re, ssm_c_im, ssm_d, ssm_w_glu,
                           mlp_norm_w, mlp_w_up, mlp_w_down, final_norm_w)))
    mom = dict(zip(names, (m_meta_tokens, m_attn_norm_w, m_attn_w_qkv, m_attn_sinks, m_attn_w_o, m_ssm_norm_w,
                           m_ssm_lambda_re, m_ssm_lambda_im, m_ssm_log_dt, m_ssm_b_re, m_ssm_b_im, m_ssm_c_re,
                           m_ssm_c_im, m_ssm_d, m_ssm_w_glu, m_mlp_norm_w, m_mlp_w_up, m_mlp_w_down, m_final_norm_w)))
    var = dict(zip(names, (v_meta_tokens, v_attn_norm_w, v_attn_w_qkv, v_attn_sinks, v_attn_w_o, v_ssm_norm_w,
                           v_ssm_lambda_re, v_ssm_lambda_im, v_ssm_log_dt, v_ssm_b_re, v_ssm_b_im, v_ssm_c_re,
                           v_ssm_c_im, v_ssm_d, v_ssm_w_glu, v_mlp_norm_w, v_mlp_w_up, v_mlp_w_down, v_final_norm_w)))

    my_chip = 2 * lax.axis_index("x") + lax.axis_index("y")
    my_dev = 2 * my_chip + lax.axis_index("c")
    small_mine = _pack([wts[n] for n in _SHARDED_SMALL], 128)
    first = [attn_w_qkv.astype(BF16), attn_w_o.astype(BF16), small_mine]
    up16, down16 = mlp_w_up.astype(BF16), mlp_w_down.astype(BF16)
    mlp0 = [up16[0:1], down16[0:1]]
    rest = [ssm_w_glu.astype(BF16), up16[1:2], down16[1:2]]
    handles, _ = _split_start([(srcs, [_landing(a, my_chip, N_CHIPS) for a in srcs]) for srcs in (first, mlp0, rest)],
                              _gather_copies, N_GATHER_PEERS, "gather_start")
    full = {n: wts[n] for n in _REP_MISC}
    full["final_norm_w"] = final_norm_w[None]
    for n in _REP_SSM:
        full[n] = wts[n][0]
    full["ssm_operands"] = _ssm_operands(full, x.shape[1] + BLOCK)
    got = _split_wait(handles[0], full["ssm_operands"], _gather_copies, "gather_wait_first")
    full["attn_w_qkv"], full["attn_w_o"] = got[0], got[1]
    smalls = [_unpack(got[2][s], [wts[n] for n in _SHARDED_SMALL]) for s in range(N_CHIPS)]
    for k, n in enumerate(_SHARDED_SMALL):
        full[n] = jnp.concatenate([smalls[s][k] for s in range(N_CHIPS)], axis=1)

    def late_weights(stage, after):
        if stage == 0:
            up, down = _split_wait(handles[1], after, _gather_copies, "gather_wait_mlp0")
            return {"mlp_w_up": [up], "mlp_w_down": [down]}
        glu, up, down = _split_wait(handles[2], after, _gather_copies, "gather_wait_rest")
        return {"ssm_w_glu": glu, "mlp_w_up": [up], "mlp_w_down": [down]}

    def shard_cols(t):
        return jnp.swapaxes(t.reshape(t.shape[0], N_CHIPS, t.shape[1] // N_CHIPS), 0, 1)

    pending = {}

    def on_grads(tag, g):
        scatter = [g[n] for n in _BIG if n in g]
        whole = []
        if tag == "ssm":
            whole = [_pack([g[n] for n in _REP_SSM], D_MODEL)]
        if tag == "rest":
            parts = [shard_cols(g[n]) for n in _SHARDED_SMALL]
            scatter = [jnp.stack([_pack([p[s] for p in parts], 128) for s in range(N_CHIPS)])]
            whole = [_pack([g[n] for n in _REP_MISC] + [g["loss"]], D_MODEL)]
        srcs = scatter + whole
        lands = [_landing(lax.dynamic_index_in_dim(a, my_chip, 0, keepdims=False), my_dev, N_DEV) for a in scatter]
        lands += [_landing(a, my_dev, N_DEV) for a in whole]
        hs, token = _split_start([(srcs, lands)], _exchange_copies(len(scatter)), N_EXCHANGE_PEERS, "exchange_start_" + tag)
        pending[tag] = (hs[0], len(scatter))
        return token[0, 0]

    _, grad_x = _local_step(x, loss_target, full, late_weights, on_grads)

    recv = {}
    for tag, (handle, n_scatter) in pending.items():
        recv[tag] = _split_wait(handle, grad_x, _exchange_copies(n_scatter), "exchange_wait_" + tag)
    loss = jnp.sum(recv["rest"][1].reshape(N_DEV, -1)[:, sum(wts[n].size for n in _REP_MISC)])

    out = {}

    def update(tag, pieces, w2, m2, v2):
        return _adamw(pieces, w2, m2, v2, "adamw_" + tag)

    def update_weight(n, pieces):
        shp = wts[n].shape
        r2 = (math.prod(shp[:-1]), shp[-1])
        res = update(n, pieces, wts[n].reshape(r2), mom[n].reshape(r2), var[n].reshape(r2))
        out[n] = [t.reshape(shp) for t in res]

    update_weight("mlp_w_up", [recv["mlp0"][1], recv["mlp1"][0]])
    update_weight("mlp_w_down", [recv["mlp0"][2], recv["mlp1"][1]])
    update_weight("attn_w_o", [recv["mlp0"][0]])
    update_weight("ssm_w_glu", [recv["ssm"][0]])
    update_weight("attn_w_qkv", [recv["qkv"][0]])
    for tag, group, pieces, cols in (("small", _SHARDED_SMALL, recv["rest"][0], 128),
                                     ("rep_ssm", _REP_SSM, recv["ssm"][1], D_MODEL),
                                     ("rep_misc", _REP_MISC, recv["rest"][1], D_MODEL)):
        like = [wts[n] for n in group]
        res = update(tag, [pieces], _pack(like, cols), _pack([mom[n] for n in group], cols),
                     _pack([var[n] for n in group], cols))
        for k, n in enumerate(group):
            out[n] = [_unpack(t, like)[k] for t in res]

    return (loss, grad_x, *[out[n][0] for n in names], *[out[n][1] for n in names],
            *[out[n][2] for n in names], *[out[n][3] for n in names])
```

```python
import functools
import math

import jax
import jax.numpy as jnp
from jax import lax
from jax.experimental import pallas as pl
from jax.experimental.pallas import tpu as pltpu

F32 = jnp.float32
BF16 = jnp.bfloat16
SDS = jax.ShapeDtypeStruct

D_MODEL = 1024
N_HEADS = 16
N_KV = 4
GQA = N_HEADS // N_KV
HEAD_DIM = 64
BLOCK = 128
N_META = 16
PAD = BLOCK - N_META
QKV_DIM = (N_HEADS + 2 * N_KV) * HEAD_DIM
KV_DIM = 2 * N_KV * HEAD_DIM
D_FF = 4 * D_MODEL
N_CHIPS = 4
N_DEV = 8
SSM_GROUP = 16
SSM_NG = D_MODEL // SSM_GROUP
SSM_STATE = 64
N_PAIR = SSM_NG // 2
PAIRS_PER_CHUNK = 4
RMS_EPS = 1e-6
NEG_INF = -1e30
LAMBDA_RE_MAX = -1e-4
ADAM_LR, ADAM_B1, ADAM_B2, ADAM_EPS, ADAM_WD, ADAM_STEP = 0.001, 0.9, 0.999, 1e-08, 0.01, 10

TM = 384
MM_TILES = (1056, 768, 384)
MLP_FWD_TILES = (384,)
MLP_BWD_TILES = (768, 384)
TN_TILES = (1408, 768, 384)
VMEM_LIMIT = 56 * 1024 * 1024


def _params(n_grid):
    return pltpu.CompilerParams(dimension_semantics=("arbitrary",) * n_grid, vmem_limit_bytes=VMEM_LIMIT)


def _row_tile(n_rows, tiles):
    return next(t for t in tiles if n_rows % t == 0)


def _rms(h, w):
    r = lax.rsqrt(jnp.mean(h * h, axis=-1, keepdims=True) + RMS_EPS)
    return h * r * w


def _rms_bwd(dhn, h, w):
    r = lax.rsqrt(jnp.mean(h * h, axis=-1, keepdims=True) + RMS_EPS)
    g = dhn * w
    proj = jnp.sum(g * h, axis=-1, keepdims=True) * (1.0 / D_MODEL)
    return r * g - h * (r * r * r) * proj, dhn * h * r


def _fold8(t):
    return jnp.sum(t.reshape(t.shape[0] // 8, 8, t.shape[1]), axis=0)


def _gelu(y):
    return 0.5 * y * (1.0 + jnp.tanh(0.7978845608028654 * (y + 0.044715 * y * y * y)))


def _gelu_grad(y):
    t = jnp.tanh(0.7978845608028654 * (y + 0.044715 * y * y * y))
    return 0.5 * (1.0 + t) + 0.5 * y * (1.0 - t * t) * 0.7978845608028654 * (1.0 + 3.0 * 0.044715 * y * y)


def _w4_spec(w4):
    n_sh, _, k, n = w4.shape
    return pl.BlockSpec((n_sh, None, k, n), lambda i: (0, 0, 0, 0))


def _rms_mm_cols(h, wn, w4, name):
    n_rows = h.shape[0]
    n_sh, _, k, n = w4.shape
    tm = _row_tile(n_rows, MM_TILES)

    def body(h_ref, wn_ref, w_ref, o_ref, hn_ref):
        hn = _rms(h_ref[...], wn_ref[...]).astype(BF16)
        hn_ref[...] = hn
        for s in range(n_sh):
            o_ref[:, s * n:(s + 1) * n] = jnp.dot(hn, w_ref[s], preferred_element_type=F32).astype(o_ref.dtype)

    return pl.pallas_call(
        body, name=name, grid=(n_rows // tm,),
        in_specs=[pl.BlockSpec((tm, k), lambda i: (i, 0)), pl.BlockSpec((1, k), lambda i: (0, 0)), _w4_spec(w4)],
        out_specs=[pl.BlockSpec((tm, n_sh * n), lambda i: (i, 0)), pl.BlockSpec((tm, k), lambda i: (i, 0))],
        out_shape=[SDS((n_rows, n_sh * n), BF16), SDS((n_rows, k), BF16)],
        compiler_params=_params(1),
    )(h, wn, w4)


def _mm_cols(x, w4, trans_w, name):
    n_rows, kx = x.shape
    tm = _row_tile(n_rows, MM_TILES)
    n_sh, _, k, n = w4.shape
    n_out = k if trans_w else n
    dims = (((1,), (1,)), ((), ())) if trans_w else (((1,), (0,)), ((), ()))

    def body(x_ref, w_ref, o_ref):
        x16 = x_ref[...].astype(BF16)
        for s in range(n_sh):
            o_ref[:, s * n_out:(s + 1) * n_out] = lax.dot_general(
                x16, w_ref[s], dims, preferred_element_type=F32).astype(o_ref.dtype)

    return pl.pallas_call(
        body, name=name, grid=(n_rows // tm,),
        in_specs=[pl.BlockSpec((tm, kx), lambda i: (i, 0)), _w4_spec(w4)],
        out_specs=pl.BlockSpec((tm, n_sh * n_out), lambda i: (i, 0)),
        out_shape=SDS((n_rows, n_sh * n_out), BF16),
        compiler_params=_params(1),
    )(x, w4)


def _mm_acc(x, w4, trans_w, name, res=None, rms_bwd=None, out_dtype=F32):
    n_rows = x.shape[0]
    tm = _row_tile(n_rows, MM_TILES)
    n_sh, _, k, n = w4.shape
    kx, n_out = (n, k) if trans_w else (k, n)
    dims = (((1,), (1,)), ((), ())) if trans_w else (((1,), (0,)), ((), ()))

    def body(*refs):
        if rms_bwd is not None:
            x_ref, w_ref, h_ref, wn_ref, dres_ref, o_ref, dw_ref = refs
        elif res is not None:
            x_ref, w_ref, res_ref, o_ref = refs
        else:
            x_ref, w_ref, o_ref = refs
        acc = None
        for s in range(n_sh):
            part = lax.dot_general(x_ref[:, s * kx:(s + 1) * kx].astype(BF16), w_ref[s], dims, preferred_element_type=F32)
            acc = part if acc is None else acc + part
        if rms_bwd is not None:
            dh, dw_rows = _rms_bwd(acc, h_ref[...], wn_ref[...])
            o_ref[...] = (dres_ref[...] + dh).astype(o_ref.dtype)

            @pl.when(pl.program_id(0) == 0)
            def _():
                dw_ref[...] = jnp.zeros_like(dw_ref)

            dw_ref[...] += _fold8(dw_rows)
        elif res is not None:
            o_ref[...] = (res_ref[...] + acc).astype(o_ref.dtype)
        else:
            o_ref[...] = acc.astype(o_ref.dtype)

    row = lambda i: (i, 0)
    in_specs = [pl.BlockSpec((tm, n_sh * kx), row), _w4_spec(w4)]
    args = [x, w4]
    out_specs = pl.BlockSpec((tm, n_out), row)
    out_shape = SDS((n_rows, n_out), out_dtype)
    if rms_bwd is not None:
        h, wn, dres = rms_bwd
        in_specs += [pl.BlockSpec((tm, n_out), row), pl.BlockSpec((1, n_out), lambda i: (0, 0)),
                     pl.BlockSpec((tm, n_out), row)]
        args += [h, wn, dres]
        out_specs = [out_specs, pl.BlockSpec((8, n_out), lambda i: (0, 0))]
        out_shape = [out_shape, SDS((8, n_out), F32)]
    elif res is not None:
        in_specs.append(pl.BlockSpec((tm, n_out), row))
        args.append(res)
    return pl.pallas_call(
        body, name=name, grid=(n_rows // tm,), in_specs=in_specs, out_specs=out_specs, out_shape=out_shape,
        compiler_params=_params(1),
    )(*args)


def _mm_tn(a, b, n_sh, a_sharded, name):
    n_rows = a.shape[0]
    tm = _row_tile(n_rows, TN_TILES)
    ka = a.shape[1] // n_sh if a_sharded else a.shape[1]
    nb = b.shape[1] if a_sharded else b.shape[1] // n_sh
    n_i = n_rows // tm

    def body(a_ref, b_ref, o_ref, acc):
        i = pl.program_id(0)

        @pl.when(i == 0)
        def _():
            acc[...] = jnp.zeros_like(acc)

        for s in range(n_sh):
            a_s = a_ref[:, s * ka:(s + 1) * ka] if a_sharded else a_ref[...]
            b_s = b_ref[...] if a_sharded else b_ref[:, s * nb:(s + 1) * nb]
            acc[s] += lax.dot_general(a_s.astype(BF16), b_s.astype(BF16), (((0,), (0,)), ((), ())),
                                      preferred_element_type=F32)

        @pl.when(i == n_i - 1)
        def _():
            o_ref[...] = acc[...].astype(o_ref.dtype)

    return pl.pallas_call(
        body, name=name, grid=(n_i,),
        in_specs=[pl.BlockSpec((tm, a.shape[1]), lambda i: (i, 0)), pl.BlockSpec((tm, b.shape[1]), lambda i: (i, 0))],
        out_specs=pl.BlockSpec((n_sh, ka, nb), lambda i: (0, 0, 0)),
        out_shape=SDS((n_sh, ka, nb), BF16),
        scratch_shapes=[pltpu.VMEM((n_sh, ka, nb), F32)], compiler_params=_params(1),
    )(a, b)


def _mlp_fwd(h, wn, w_up4, w_down4, name, next_norm=None, glu_z=None):
    n_rows = h.shape[0]
    tm = _row_tile(n_rows, MLP_FWD_TILES)
    n_sh = w_up4.shape[0]
    f_sh = D_FF // n_sh
    w_down = w_down4.reshape(D_FF, D_MODEL)

    def body(*refs):
        refs = list(refs)
        h_ref, wn_ref, wu_ref, wd_ref = refs[:4]
        at = 4
        if next_norm is not None:
            nn_ref = refs[at]
            at += 1
        if glu_z is not None:
            z_ref = refs[at]
            at += 1
        o_ref, a_ref, hn_ref = refs[at:at + 3]
        at += 3
        if next_norm is not None:
            u_ref = refs[at]
            at += 1
        if glu_z is not None:
            hin_ref = refs[at]
            at += 1
        act_s = refs[at]
        h_in = h_ref[...]
        if glu_z is not None:
            h_in = h_in + z_ref[:, 0:D_MODEL].astype(F32) * jax.nn.sigmoid(z_ref[:, D_MODEL:2 * D_MODEL].astype(F32))
            hin_ref[...] = h_in
        hn = _rms(h_in, wn_ref[...]).astype(BF16)
        hn_ref[...] = hn
        for s in range(n_sh):
            cols = slice(s * f_sh, (s + 1) * f_sh)
            a = jnp.dot(hn, wu_ref[s], preferred_element_type=F32)
            a_ref[:, cols] = a.astype(BF16)
            act = jnp.maximum(a, 0.0)
            act_s[:, cols] = (act * act).astype(BF16)
        out = h_in + jnp.dot(act_s[...], wd_ref[...], preferred_element_type=F32)
        o_ref[...] = out
        if next_norm is not None:
            u_ref[...] = _rms(out, nn_ref[...])

    row = lambda i: (i, 0)
    vec = pl.BlockSpec((1, D_MODEL), lambda i: (0, 0))
    in_specs = [pl.BlockSpec((tm, D_MODEL), row), vec,
                pl.BlockSpec((n_sh, None, D_MODEL, f_sh), lambda i: (0, 0, 0, 0), pipeline_mode=pl.Buffered(1)),
                pl.BlockSpec((D_FF, D_MODEL), lambda i: (0, 0), pipeline_mode=pl.Buffered(1))]
    out_specs = [pl.BlockSpec((tm, D_MODEL), row), pl.BlockSpec((tm, D_FF), row), pl.BlockSpec((tm, D_MODEL), row)]
    out_shape = [SDS((n_rows, D_MODEL), F32), SDS((n_rows, D_FF), BF16), SDS((n_rows, D_MODEL), BF16)]
    args = [h, wn, w_up4, w_down]
    if next_norm is not None:
        in_specs.append(vec)
        args.append(next_norm)
    if glu_z is not None:
        in_specs.append(pl.BlockSpec((tm, 2 * D_MODEL), row))
        args.append(glu_z)
    for extra in (next_norm, glu_z):
        if extra is not None:
            out_specs.append(pl.BlockSpec((tm, D_MODEL), row))
            out_shape.append(SDS((n_rows, D_MODEL), F32))
    return pl.pallas_call(
        body, name=name, grid=(n_rows // tm,), in_specs=in_specs, out_specs=out_specs, out_shape=out_shape,
        scratch_shapes=[pltpu.VMEM((tm, D_FF), BF16)],
        compiler_params=_params(1),
    )(*args)


def _mlp_bwd_shard(s, dh, a, hn, dhn_prev, h, wn, w_up4, w_down4, dw_up_buf, dw_down_buf, name):
    n_rows = dh.shape[0]
    n_sh = w_up4.shape[0]
    f_sh = D_FF // n_sh
    tm = _row_tile(n_rows, MLP_BWD_TILES)
    n_i = n_rows // tm
    last = h is not None
    nt = (((1,), (1,)), ((), ()))
    tn = (((0,), (0,)), ((), ()))

    def body(*refs):
        refs = list(refs)
        dh_ref, a_ref, hn_ref, wu_ref, wd_ref = refs[:5]
        at = 5
        prev_ref = None
        if dhn_prev is not None:
            prev_ref = refs[at]
            at += 1
        if last:
            h_ref, wn_ref = refs[at:at + 2]
            at += 2
        if dw_up_buf is not None:
            at += 2
        o_ref, dwu_ref, dwd_ref = refs[at:at + 3]
        at += 3
        if last:
            dnorm_ref = refs[at]
            at += 1
        acc_u, acc_d = refs[at:at + 2]
        i = pl.program_id(0)

        @pl.when(i == 0)
        def _():
            acc_u[...] = jnp.zeros_like(acc_u)
            acc_d[...] = jnp.zeros_like(acc_d)
            if last:
                dnorm_ref[...] = jnp.zeros_like(dnorm_ref)

        dh16 = dh_ref[...].astype(BF16)
        r = jnp.maximum(a_ref[...].astype(F32), 0.0)
        dact = lax.dot_general(dh16, wd_ref[...], nt, preferred_element_type=F32)
        da16 = (dact * (2.0 * r)).astype(BF16)
        acc_d[...] += lax.dot_general((r * r).astype(BF16), dh16, tn, preferred_element_type=F32)
        acc_u[...] += lax.dot_general(hn_ref[...], da16, tn, preferred_element_type=F32)
        dhn = lax.dot_general(da16, wu_ref[...], nt, preferred_element_type=F32)
        if prev_ref is not None:
            dhn = dhn + prev_ref[...]
        if last:
            d_rms, dw_rows = _rms_bwd(dhn, h_ref[...], wn_ref[...])
            o_ref[...] = dh_ref[...] + d_rms
            dnorm_ref[...] += _fold8(dw_rows)
        else:
            o_ref[...] = dhn

        @pl.when(i == n_i - 1)
        def _():
            dwu_ref[...] = acc_u[...].astype(BF16)
            dwd_ref[...] = acc_d[...].astype(BF16)

    row = lambda i: (i, 0)
    tile = pl.BlockSpec((tm, D_MODEL), row)
    in_specs = [tile, pl.BlockSpec((tm, f_sh), lambda i: (i, s)), tile,
                pl.BlockSpec((None, None, D_MODEL, f_sh), lambda i: (s, 0, 0, 0)),
                pl.BlockSpec((None, None, f_sh, D_MODEL), lambda i: (s, 0, 0, 0))]
    args = [dh, a, hn, w_up4, w_down4]
    if dhn_prev is not None:
        in_specs.append(tile)
        args.append(dhn_prev)
    if last:
        in_specs += [tile, pl.BlockSpec((1, D_MODEL), lambda i: (0, 0))]
        args += [h, wn]
    aliases = {}
    if dw_up_buf is not None:
        aliases = {len(args): 1, len(args) + 1: 2}
        in_specs += [pl.BlockSpec(memory_space=pl.ANY)] * 2
        args += [dw_up_buf, dw_down_buf]
    out_specs = [tile, pl.BlockSpec((None, D_MODEL, f_sh), lambda i: (s, 0, 0)),
                 pl.BlockSpec((None, f_sh, D_MODEL), lambda i: (s, 0, 0))]
    out_shape = [SDS((n_rows, D_MODEL), F32), SDS((n_sh, D_MODEL, f_sh), BF16), SDS((n_sh, f_sh, D_MODEL), BF16)]
    if last:
        out_specs.append(pl.BlockSpec((8, D_MODEL), lambda i: (0, 0)))
        out_shape.append(SDS((8, D_MODEL), F32))
    return pl.pallas_call(
        body, name=name, grid=(n_i,), in_specs=in_specs, out_specs=out_specs, out_shape=out_shape,
        input_output_aliases=aliases,
        scratch_shapes=[pltpu.VMEM((D_MODEL, f_sh), F32), pltpu.VMEM((f_sh, D_MODEL), F32)],
        compiler_params=_params(1),
    )(*args)


def _attn_masks(n):
    qi = lax.broadcasted_iota(jnp.int32, (BLOCK, 3 * BLOCK), 0)
    col = lax.broadcasted_iota(jnp.int32, (BLOCK, 3 * BLOCK), 1)
    kj = col - BLOCK
    dist = BLOCK + qi - kj
    kmin = jnp.where(n == 0, 2 * BLOCK, jnp.where(n == 1, BLOCK, 0))
    band_ok = (col >= BLOCK) & (dist >= 0) & (dist < BLOCK) & (kj >= kmin)
    q_pos = n * BLOCK + qi - PAD
    meta_ok = (col >= PAD) & (col < BLOCK) & (col - PAD <= q_pos)
    distf = jnp.where(col >= BLOCK, dist, 0).astype(F32)
    return band_ok | meta_ok, distf


def _alibi_slope(h):
    return float(2.0 ** (-8.0 * (h + 1) / N_HEADS))


def _attn_bias(n, bias_s):
    ok, distf = _attn_masks(n)
    for h in range(N_HEADS):
        bias_s[h] = jnp.where(ok, -_alibi_slope(h) * distf, NEG_INF)


def _attn_fwd(qkv, sinks, n_ex, nb):
    n_rows = qkv.shape[0]
    kvb = N_HEADS * HEAD_DIM // KV_DIM

    def body(sink_ref, q_ref, kvm_ref, kvp_ref, kvc_ref, o_ref, lse_ref, k_s, v_s, q_s, bias_s):
        n = pl.program_id(1)

        @pl.when(n <= 2)
        def _():
            _attn_bias(n, bias_s)

        v_s[...] = jnp.ones_like(v_s)
        for part, ref in enumerate((kvm_ref, kvp_ref, kvc_ref)):
            rows = slice(part * BLOCK, (part + 1) * BLOCK)
            k_s[rows, :] = ref[:, 0:N_KV * HEAD_DIM]
            for kv in range(N_KV):
                v_s[rows, kv * 2 * HEAD_DIM:kv * 2 * HEAD_DIM + HEAD_DIM] = \
                    ref[:, (N_KV + kv) * HEAD_DIM:(N_KV + kv + 1) * HEAD_DIM]
        for kv in range(N_KV):
            for g in range(GQA):
                h = kv * GQA + g
                q_s[kv, g * BLOCK:(g + 1) * BLOCK, :] = q_ref[:, h * HEAD_DIM:(h + 1) * HEAD_DIM] * (HEAD_DIM ** -0.5)
            s4 = lax.dot_general(q_s[kv], k_s[:, kv * HEAD_DIM:(kv + 1) * HEAD_DIM], (((1,), (1,)), ((), ())),
                                 preferred_element_type=F32)
            es, ms, sink_es = [], [], []
            for g in range(GQA):
                h = kv * GQA + g
                s = s4[g * BLOCK:(g + 1) * BLOCK] + bias_s[h]
                sink = sink_ref[0, h]
                m = jnp.maximum(jnp.max(s, axis=-1, keepdims=True), sink)
                es.append(jnp.exp(s - m).astype(BF16))
                ms.append(m)
                sink_es.append(jnp.exp(sink - m))
            pv = jnp.dot(jnp.concatenate(es, axis=0), v_s[:, kv * 2 * HEAD_DIM:(kv + 1) * 2 * HEAD_DIM],
                         preferred_element_type=F32)
            for g in range(GQA):
                h = kv * GQA + g
                pg = pv[g * BLOCK:(g + 1) * BLOCK]
                l = pg[:, HEAD_DIM:HEAD_DIM + 1] + sink_es[g]
                o_ref[:, h * HEAD_DIM:(h + 1) * HEAD_DIM] = (pg[:, 0:HEAD_DIM] * (1.0 / l)).astype(BF16)
                lse_ref[:, h:h + 1] = ms[g] + jnp.log(l)

    return pl.pallas_call(
        body, name="attn_fwd", grid=(n_ex, nb),
        in_specs=[pl.BlockSpec(memory_space=pltpu.SMEM),
                  pl.BlockSpec((BLOCK, N_HEADS * HEAD_DIM), lambda b, n: (b * nb + n, 0)),
                  pl.BlockSpec((BLOCK, KV_DIM), lambda b, n: (b * nb, kvb)),
                  pl.BlockSpec((BLOCK, KV_DIM), lambda b, n: (b * nb + jnp.maximum(n - 1, 0), kvb)),
                  pl.BlockSpec((BLOCK, KV_DIM), lambda b, n: (b * nb + n, kvb))],
        out_specs=[pl.BlockSpec((BLOCK, N_HEADS * HEAD_DIM), lambda b, n: (b * nb + n, 0)),
                   pl.BlockSpec((BLOCK, N_HEADS), lambda b, n: (b * nb + n, 0))],
        out_shape=[SDS((n_rows, N_HEADS * HEAD_DIM), BF16), SDS((n_rows, N_HEADS), F32)],
        scratch_shapes=[pltpu.VMEM((3 * BLOCK, N_KV * HEAD_DIM), BF16), pltpu.VMEM((3 * BLOCK, 2 * N_KV * HEAD_DIM), BF16),
                        pltpu.VMEM((N_KV, GQA * BLOCK, HEAD_DIM), BF16), pltpu.VMEM((N_HEADS, BLOCK, 3 * BLOCK), F32)],
        compiler_params=_params(2),
    )(sinks, qkv, qkv, qkv, qkv)


def _attn_bwd(qkv, sinks, o, lse, do, n_ex, nb):
    n_rows = qkv.shape[0]
    kvb = N_HEADS * HEAD_DIM // KV_DIM
    scale = HEAD_DIM ** -0.5
    nq = lambda r: nb - 1 - r

    def body(sink_ref, q_ref, kvm_ref, kvp_ref, kvc_ref, o_ref, lse_ref, do_ref, dqkv_ref, dsink_ref,
             k_s, v_s, dkv_s, carry_s, meta_s, q_s, do_s, bias_s):
        b, r = pl.program_id(0), pl.program_id(1)
        n = nq(r)

        @pl.when((r == 0) | (n <= 1))
        def _():
            _attn_bias(n, bias_s)

        @pl.when((b == 0) & (r == 0))
        def _():
            dsink_ref[...] = jnp.zeros_like(dsink_ref)

        @pl.when(r == 0)
        def _():
            carry_s[...] = jnp.zeros_like(carry_s)
            meta_s[...] = jnp.zeros_like(meta_s)

        for part, ref in enumerate((kvm_ref, kvp_ref, kvc_ref)):
            k_s[part * BLOCK:(part + 1) * BLOCK, :] = ref[:, 0:N_KV * HEAD_DIM]
            v_s[part * BLOCK:(part + 1) * BLOCK, :] = ref[:, N_KV * HEAD_DIM:KV_DIM]
        nt = (((1,), (1,)), ((), ()))
        tn = (((0,), (0,)), ((), ()))
        for kv in range(N_KV):
            kcols = slice(kv * HEAD_DIM, (kv + 1) * HEAD_DIM)
            vcols = slice(N_KV * HEAD_DIM + kv * HEAD_DIM, N_KV * HEAD_DIM + (kv + 1) * HEAD_DIM)
            for g in range(GQA):
                cols = slice((kv * GQA + g) * HEAD_DIM, (kv * GQA + g + 1) * HEAD_DIM)
                q_s[kv, g * BLOCK:(g + 1) * BLOCK, :] = q_ref[:, cols] * scale
                do_s[kv, g * BLOCK:(g + 1) * BLOCK, :] = do_ref[:, cols]
            kh, vh = k_s[:, kcols], v_s[:, kcols]
            s4 = lax.dot_general(q_s[kv], kh, nt, preferred_element_type=F32)
            dp4 = lax.dot_general(do_s[kv], vh, nt, preferred_element_type=F32)
            ps, dss = [], []
            for g in range(GQA):
                h = kv * GQA + g
                cols = slice(h * HEAD_DIM, (h + 1) * HEAD_DIM)
                rows = slice(g * BLOCK, (g + 1) * BLOCK)
                s = s4[rows] + bias_s[h]
                lse_h = lse_ref[:, h:h + 1]
                p = jnp.exp(s - lse_h)
                delta = jnp.sum(do_ref[:, cols].astype(F32) * o_ref[:, cols].astype(F32), axis=-1, keepdims=True)
                dsink_ref[:, h:h + 1] += -jnp.exp(sink_ref[0, h] - lse_h) * delta
                ps.append(p.astype(BF16))
                dss.append((p * (dp4[rows] - delta)).astype(BF16))
            p4, ds4 = jnp.concatenate(ps, axis=0), jnp.concatenate(dss, axis=0)
            dq4 = jnp.dot(ds4, kh, preferred_element_type=F32) * scale
            for g in range(GQA):
                cols = slice((kv * GQA + g) * HEAD_DIM, (kv * GQA + g + 1) * HEAD_DIM)
                dqkv_ref[:, cols] = dq4[g * BLOCK:(g + 1) * BLOCK].astype(BF16)
            dkv_s[:, kcols] = lax.dot_general(ds4, q_s[kv], tn, preferred_element_type=F32)
            dkv_s[:, vcols] = lax.dot_general(p4, do_s[kv], tn, preferred_element_type=F32)

        meta_s[...] += dkv_s[0:BLOCK, :]
        cur = dkv_s[2 * BLOCK:3 * BLOCK, :] + carry_s[...]
        carry_s[...] = dkv_s[BLOCK:2 * BLOCK, :]

        @pl.when(n > 0)
        def _():
            dqkv_ref[:, N_HEADS * HEAD_DIM:QKV_DIM] = cur.astype(BF16)

        @pl.when(n == 0)
        def _():
            dqkv_ref[:, N_HEADS * HEAD_DIM:QKV_DIM] = (cur + meta_s[...]).astype(BF16)

    blk = lambda b, r: (b * nb + nq(r), 0)
    return pl.pallas_call(
        body, name="attn_bwd", grid=(n_ex, nb),
        in_specs=[pl.BlockSpec(memory_space=pltpu.SMEM),
                  pl.BlockSpec((BLOCK, N_HEADS * HEAD_DIM), blk),
                  pl.BlockSpec((BLOCK, KV_DIM), lambda b, r: (b * nb, kvb)),
                  pl.BlockSpec((BLOCK, KV_DIM), lambda b, r: (b * nb + jnp.maximum(nq(r) - 1, 0), kvb)),
                  pl.BlockSpec((BLOCK, KV_DIM), lambda b, r: (b * nb + nq(r), kvb)),
                  pl.BlockSpec((BLOCK, N_HEADS * HEAD_DIM), blk),
                  pl.BlockSpec((BLOCK, N_HEADS), blk),
                  pl.BlockSpec((BLOCK, N_HEADS * HEAD_DIM), blk)],
        out_specs=[pl.BlockSpec((BLOCK, QKV_DIM), blk),
                   pl.BlockSpec((BLOCK, N_HEADS), lambda b, r: (0, 0))],
        out_shape=[SDS((n_rows, QKV_DIM), BF16), SDS((BLOCK, N_HEADS), F32)],
        scratch_shapes=[pltpu.VMEM((3 * BLOCK, N_KV * HEAD_DIM), BF16), pltpu.VMEM((3 * BLOCK, N_KV * HEAD_DIM), BF16),
                        pltpu.VMEM((3 * BLOCK, KV_DIM), F32), pltpu.VMEM((BLOCK, KV_DIM), F32),
                        pltpu.VMEM((BLOCK, KV_DIM), F32), pltpu.VMEM((N_KV, GQA * BLOCK, HEAD_DIM), BF16),
                        pltpu.VMEM((N_KV, GQA * BLOCK, HEAD_DIM), BF16), pltpu.VMEM((N_HEADS, BLOCK, 3 * BLOCK), F32)],
        compiler_params=_params(2),
    )(sinks, qkv, qkv, qkv, qkv, o, lse, do)


SSM_TILES = (1408, 384)
XW = 256 * PAIRS_PER_CHUNK


def _cmul_add(xr, xi, mr, mi, sr, si):
    return xr + mr * sr - mi * si, xi + mr * si + mi * sr


def _to_segments(src_ref, dst, seg):
    for s in range(seg):
        dst[s * 8:(s + 1) * 8, :] = src_ref[pl.ds(s, 8, stride=seg), :]


def _from_segments(src, i, seg):
    return src[pl.ds(i, seg, stride=8), :]


def _scan_segments(buf, tab_ref, carry_s, seg, reverse):
    shifts = (7, 6, 4) if reverse else (1, 2, 4)
    row_id = lax.broadcasted_iota(jnp.int32, (8, 128), 0)
    a_tiles = [tab_ref[j, c] for j in range(PAIRS_PER_CHUNK) for c in (0, 1)]

    def local(si, prev):
        s = (seg - 1 - si) if reverse else si
        row = pl.multiple_of(s * 8, 8)
        out = []
        for j in range(PAIRS_PER_CHUNK):
            re, im = slice(256 * j, 256 * j + 128), slice(256 * j + 128, 256 * j + 256)
            xr, xi = _cmul_add(buf[pl.ds(row, 8), re], buf[pl.ds(row, 8), im],
                               a_tiles[2 * j], a_tiles[2 * j + 1], prev[2 * j], prev[2 * j + 1])
            buf[pl.ds(row, 8), re] = xr
            buf[pl.ds(row, 8), im] = xi
            out += [xr, xi]
        return tuple(out)

    zero = jnp.zeros((8, 128), F32)
    edge = lax.fori_loop(0, seg, local, (zero,) * (2 * PAIRS_PER_CHUNK))

    entering = []
    for j in range(PAIRS_PER_CHUNK):
        er, ei = edge[2 * j], edge[2 * j + 1]
        if reverse:
            sr = jnp.where(row_id == 7, carry_s[2 * j], pltpu.roll(er, 7, 0))
            si_ = jnp.where(row_id == 7, carry_s[2 * j + 1], pltpu.roll(ei, 7, 0))
        else:
            sr = jnp.where(row_id == 0, carry_s[2 * j], pltpu.roll(er, 1, 0))
            si_ = jnp.where(row_id == 0, carry_s[2 * j + 1], pltpu.roll(ei, 1, 0))
        for lvl, sh in enumerate(shifts):
            sr, si_ = _cmul_add(sr, si_, tab_ref[j, 2 + 2 * lvl], tab_ref[j, 3 + 2 * lvl],
                                pltpu.roll(sr, sh, 0), pltpu.roll(si_, sh, 0))
        entering += [sr, si_]
        tr, ti = _cmul_add(er, ei, tab_ref[j, 2], tab_ref[j, 3], sr, si_)
        out_row = slice(0, 1) if reverse else slice(7, 8)
        carry_s[2 * j] = jnp.broadcast_to(tr[out_row], (8, 128))
        carry_s[2 * j + 1] = jnp.broadcast_to(ti[out_row], (8, 128))

    def fix(si, carried):
        s = (seg - 1 - si) if reverse else si
        row = pl.multiple_of(s * 8, 8)
        out = []
        for j in range(PAIRS_PER_CHUNK):
            re, im = slice(256 * j, 256 * j + 128), slice(256 * j + 128, 256 * j + 256)
            ar, ai, fr, fi = a_tiles[2 * j], a_tiles[2 * j + 1], carried[2 * j], carried[2 * j + 1]
            fr, fi = ar * fr - ai * fi, ar * fi + ai * fr
            buf[pl.ds(row, 8), re] += fr
            buf[pl.ds(row, 8), im] += fi
            out += [fr, fi]
        return tuple(out)

    lax.fori_loop(0, seg, fix, tuple(entering))


def _ssm_fwd(u, b_pad, c_pad, tab, d_skip, n_ex, lp):
    n_rows = u.shape[0]
    TM = _row_tile(lp, SSM_TILES)
    SEG = TM // 8
    n_t = lp // TM
    n_chunk = D_MODEL // 128

    def body(u_ref, bp_ref, cp_ref, tab_ref, d_ref, yg_ref, y_ref, xs_ref, buf, carry_s, us, ys):
        @pl.when(pl.program_id(2) == 0)
        def _():
            carry_s[...] = jnp.zeros_like(carry_s)

        _to_segments(u_ref, us, SEG)
        ub = us[...]
        u16 = ub.astype(BF16)
        buf[...] = jnp.dot(u16, bp_ref[...], preferred_element_type=F32)
        _scan_segments(buf, tab_ref, carry_s, SEG, reverse=False)
        xb = buf[...].astype(BF16)
        xs_ref[...] = xb
        ys[...] = d_ref[...] * ub + jnp.dot(xb, cp_ref[...], preferred_element_type=F32)
        for i in range(8):
            yi = _from_segments(ys, i, SEG)
            y_ref[i * SEG:(i + 1) * SEG, :] = yi
            yg_ref[i * SEG:(i + 1) * SEG, :] = _gelu(yi).astype(BF16)

    rows = lambda b, q, t: (b * n_t + t, q)
    return pl.pallas_call(
        body, name="ssm_fwd", grid=(n_ex, n_chunk, n_t),
        in_specs=[pl.BlockSpec((TM, 128), rows),
                  pl.BlockSpec((None, 128, XW), lambda b, q, t: (q, 0, 0)),
                  pl.BlockSpec((None, XW, 128), lambda b, q, t: (q, 0, 0)),
                  pl.BlockSpec((PAIRS_PER_CHUNK, 8, 8, 128), lambda b, q, t: (q, 0, 0, 0)),
                  pl.BlockSpec((1, 128), lambda b, q, t: (0, q))],
        out_specs=[pl.BlockSpec((TM, 128), rows), pl.BlockSpec((TM, 128), rows),
                   pl.BlockSpec((None, TM, XW), lambda b, q, t: (q, b * n_t + t, 0))],
        out_shape=[SDS((n_rows, D_MODEL), BF16), SDS((n_rows, D_MODEL), F32), SDS((n_chunk, n_rows, XW), BF16)],
        scratch_shapes=[pltpu.VMEM((TM, XW), F32), pltpu.VMEM((2 * PAIRS_PER_CHUNK, 8, 128), F32),
                        pltpu.VMEM((TM, 128), F32), pltpu.VMEM((TM, 128), F32)],
        compiler_params=_params(3),
    )(u, b_pad, c_pad, tab, d_skip)


def _ssm_bwd(dyg, y, u, xs, ct_pad, bt_pad, tab_rev, d_skip, n_ex, lp):
    n_rows = u.shape[0]
    TM = _row_tile(lp, SSM_TILES)
    SEG = TM // 8
    n_t = lp // TM
    n_chunk = D_MODEL // 128
    tile = lambda q, b, t: (b * n_t + (n_t - 1 - t), q)

    def body(dyg_ref, y_ref, u_ref, xs_ref, xp_ref, ct_ref, bt_ref, tab_ref, d_ref,
             du_ref, db_ref, dc_ref, da_ref, dd_ref, buf, xf, carry_s, us, dys, dyp):
        b, t = pl.program_id(1), pl.program_id(2)

        @pl.when((b == 0) & (t == 0))
        def _():
            db_ref[...] = jnp.zeros_like(db_ref)
            dc_ref[...] = jnp.zeros_like(dc_ref)
            da_ref[...] = jnp.zeros_like(da_ref)
            dd_ref[...] = jnp.zeros_like(dd_ref)

        @pl.when(t == 0)
        def _():
            carry_s[...] = jnp.zeros_like(carry_s)

        dys[...] = dyg_ref[...].astype(F32) * _gelu_grad(y_ref[...])
        dd_ref[...] += _fold8(dys[...] * u_ref[...])
        _to_segments(dys, dyp, SEG)
        dy = dyp[...]
        _to_segments(u_ref, us, SEG)
        dy16 = dy.astype(BF16)
        first_tile = t == n_t - 1
        tn = (((0,), (0,)), ((), ()))
        buf[...] = jnp.dot(dy16, ct_ref[...], preferred_element_type=F32)
        dc_ref[...] += lax.dot_general(dy16, xs_ref[...], tn, preferred_element_type=F32)
        xf[16:16 + TM, :] = xs_ref[...].astype(F32)
        xf[0:16, :] = jnp.where(first_tile, 0.0, xp_ref[...].astype(F32))
        _scan_segments(buf, tab_ref, carry_s, SEG, reverse=True)
        g16 = buf[...].astype(BF16)
        dys[...] = d_ref[...] * dy + jnp.dot(g16, bt_ref[...], preferred_element_type=F32)
        db_ref[...] += lax.dot_general(us[...].astype(BF16), g16, tn, preferred_element_type=F32)
        row_id = lax.broadcasted_iota(jnp.int32, (8, 128), 0)
        for j in range(PAIRS_PER_CHUNK):
            re, im = slice(256 * j, 256 * j + 128), slice(256 * j + 128, 256 * j + 256)
            first = [jnp.where(row_id == 0, jnp.broadcast_to(xf[15:16, c], (8, 128)),
                               pltpu.roll(xf[8 + TM:16 + TM, c], 1, 0)) for c in (re, im)]
            for rows, pr, pi in ((slice(0, 8), first[0], first[1]),
                                 (slice(8, TM), xf[16:8 + TM, re], xf[16:8 + TM, im])):
                gr, gi = buf[rows, re], buf[rows, im]
                da_ref[j, 0] += _fold8(gr * pr + gi * pi)
                da_ref[j, 1] += _fold8(gi * pr - gr * pi)
        for i in range(8):
            du_ref[i * SEG:(i + 1) * SEG, :] = _from_segments(dys, i, SEG)

    prev16 = lambda q, b, t: (q, jnp.maximum((b * n_t + (n_t - 1 - t)) * (TM // 16) - 1, 0), 0)
    return pl.pallas_call(
        body, name="ssm_bwd", grid=(n_chunk, n_ex, n_t),
        in_specs=[pl.BlockSpec((TM, 128), tile), pl.BlockSpec((TM, 128), tile), pl.BlockSpec((TM, 128), tile),
                  pl.BlockSpec((None, TM, XW), lambda q, b, t: (q, b * n_t + (n_t - 1 - t), 0)),
                  pl.BlockSpec((None, 16, XW), prev16),
                  pl.BlockSpec((None, 128, XW), lambda q, b, t: (q, 0, 0)),
                  pl.BlockSpec((None, XW, 128), lambda q, b, t: (q, 0, 0)),
                  pl.BlockSpec((PAIRS_PER_CHUNK, 8, 8, 128), lambda q, b, t: (q, 0, 0, 0)),
                  pl.BlockSpec((1, 128), lambda q, b, t: (0, q))],
        out_specs=[pl.BlockSpec((TM, 128), tile),
                   pl.BlockSpec((None, 128, XW), lambda q, b, t: (q, 0, 0)),
                   pl.BlockSpec((None, 128, XW), lambda q, b, t: (q, 0, 0)),
                   pl.BlockSpec((PAIRS_PER_CHUNK, 2, 8, 128), lambda q, b, t: (q, 0, 0, 0)),
                   pl.BlockSpec((8, 128), lambda q, b, t: (0, q))],
        out_shape=[SDS((n_rows, D_MODEL), F32), SDS((n_chunk, 128, XW), F32), SDS((n_chunk, 128, XW), F32),
                   SDS((N_PAIR, 2, 8, 128), F32), SDS((8, D_MODEL), F32)],
        scratch_shapes=[pltpu.VMEM((TM, XW), F32), pltpu.VMEM((TM + 16, XW), F32),
                        pltpu.VMEM((2 * PAIRS_PER_CHUNK, 8, 128), F32), pltpu.VMEM((TM, 128), F32),
                        pltpu.VMEM((TM, 128), F32), pltpu.VMEM((TM, 128), F32)],
        compiler_params=_params(3),
    )(dyg, y, u, xs, xs, ct_pad, bt_pad, tab_rev, d_skip)


def _rms_bwd_call(dhn, h, wn, dres, name):
    n_rows = h.shape[0]

    def body(dhn_ref, h_ref, wn_ref, dres_ref, o_ref, dw_ref):
        @pl.when(pl.program_id(0) == 0)
        def _():
            dw_ref[...] = jnp.zeros_like(dw_ref)

        dh, dw_rows = _rms_bwd(dhn_ref[...], h_ref[...], wn_ref[...])
        o_ref[...] = dres_ref[...] + dh
        dw_ref[...] += _fold8(dw_rows)

    row = lambda i: (i, 0)
    return pl.pallas_call(
        body, name=name, grid=(n_rows // TM,),
        in_specs=[pl.BlockSpec((TM, D_MODEL), row), pl.BlockSpec((TM, D_MODEL), row),
                  pl.BlockSpec((1, D_MODEL), lambda i: (0, 0)), pl.BlockSpec((TM, D_MODEL), row)],
        out_specs=[pl.BlockSpec((TM, D_MODEL), row), pl.BlockSpec((8, D_MODEL), lambda i: (0, 0))],
        out_shape=[SDS((n_rows, D_MODEL), F32), SDS((8, D_MODEL), F32)], compiler_params=_params(1),
    )(dhn, h, wn, dres)


def _glu_bwd(dh, z, w4):
    n_rows = dh.shape[0]
    tm = _row_tile(n_rows, MM_TILES)
    n_sh, _, k, n = w4.shape

    def body(dh_ref, z_ref, w_ref, dz_ref, dyg_ref):
        sg = jax.nn.sigmoid(z_ref[:, D_MODEL:2 * D_MODEL].astype(F32))
        d = dh_ref[...]
        dz_ref[:, 0:D_MODEL] = (d * sg).astype(BF16)
        dz_ref[:, D_MODEL:2 * D_MODEL] = (d * z_ref[:, 0:D_MODEL].astype(F32) * sg * (1.0 - sg)).astype(BF16)
        acc = None
        for s in range(n_sh):
            part = lax.dot_general(dz_ref[:, s * n:(s + 1) * n], w_ref[s], (((1,), (1,)), ((), ())),
                                   preferred_element_type=F32)
            acc = part if acc is None else acc + part
        dyg_ref[...] = acc.astype(BF16)

    row = lambda i: (i, 0)
    return pl.pallas_call(
        body, name="glu_bwd", grid=(n_rows // tm,),
        in_specs=[pl.BlockSpec((tm, D_MODEL), row), pl.BlockSpec((tm, 2 * D_MODEL), row), _w4_spec(w4)],
        out_specs=[pl.BlockSpec((tm, 2 * D_MODEL), row), pl.BlockSpec((tm, k), row)],
        out_shape=[SDS((n_rows, 2 * D_MODEL), BF16), SDS((n_rows, k), BF16)], compiler_params=_params(1),
    )(dh, z, w4)


def _loss_head(h, wn, target, n_ex, nb):
    n_rows = h.shape[0]
    per_tile = TM // BLOCK
    n_tiles = nb // per_tile

    def body(h_ref, wn_ref, *rest):
        t_refs, (dh_ref, loss_ref, dw_ref) = rest[:per_tile], rest[per_tile:]
        b, j = pl.program_id(0), pl.program_id(1)

        @pl.when((b == 0) & (j == 0))
        def _():
            loss_ref[...] = jnp.zeros_like(loss_ref)
            dw_ref[...] = jnp.zeros_like(dw_ref)

        def block(k):
            rows = slice(k * BLOCK, (k + 1) * BLOCK)
            hh = h_ref[rows, :]
            diff = _rms(hh, wn_ref[...]) - t_refs[k][...]
            loss_ref[...] += 0.5 * jnp.sum(diff * diff) * (1.0 / D_MODEL)
            dh, dw_rows = _rms_bwd(diff * (1.0 / D_MODEL), hh, wn_ref[...])
            dh_ref[rows, :] = dh
            dw_ref[...] += _fold8(dw_rows)

        @pl.when(j == 0)
        def _():
            dh_ref[0:BLOCK, :] = jnp.zeros((BLOCK, D_MODEL), F32)

        pl.when(j > 0)(lambda: block(0))
        for k in range(1, per_tile):
            block(k)

    def t_spec(k):
        return pl.BlockSpec((BLOCK, D_MODEL), lambda b, j: (b * (nb - 1) + jnp.maximum(per_tile * j + k - 1, 0), 0))

    tile = pl.BlockSpec((TM, D_MODEL), lambda b, j: (b * n_tiles + j, 0))
    return pl.pallas_call(
        body, name="loss_head", grid=(n_ex, n_tiles),
        in_specs=[tile, pl.BlockSpec((1, D_MODEL), lambda b, j: (0, 0))] + [t_spec(k) for k in range(per_tile)],
        out_specs=[tile, pl.BlockSpec((8, 128), lambda b, j: (0, 0)), pl.BlockSpec((8, D_MODEL), lambda b, j: (0, 0))],
        out_shape=[SDS((n_rows, D_MODEL), F32), SDS((8, 128), F32), SDS((8, D_MODEL), F32)],
        compiler_params=_params(2),
    )(h, wn, *([target] * per_tile))


def _adamw(pieces, w, m, v, name):
    n_layers = len(pieces)
    rows, cols = pieces[0].shape[1:]
    rb = rows
    for cand in (256, 136, 128, 64, 32, 16, 8):
        if rows % cand == 0 and rows > cand:
            rb = cand
            break
    n_blk = rows // rb
    c1 = 1.0 / (1.0 - ADAM_B1 ** ADAM_STEP)
    c2 = 1.0 / (1.0 - ADAM_B2 ** ADAM_STEP)

    def body(*refs):
        p_refs = refs[:n_layers]
        w_ref, m_ref, v_ref, g_out, d_out, m_out, v_out = refs[n_layers:]
        layer = pl.program_id(0)
        g = None
        for l, p_ref in enumerate(p_refs):
            gl = p_ref[0].astype(F32)
            for k in range(1, N_DEV):
                gl = gl + p_ref[k].astype(F32)
            g = gl if g is None else jnp.where(layer == l, gl, g)
        m_new = ADAM_B1 * m_ref[...] + (1.0 - ADAM_B1) * g
        v_new = ADAM_B2 * v_ref[...] + (1.0 - ADAM_B2) * (g * g)
        g_out[...] = g
        m_out[...] = m_new
        v_out[...] = v_new
        d_out[...] = -ADAM_LR * ((m_new * c1) / (jnp.sqrt(v_new * c2) + ADAM_EPS) + ADAM_WD * w_ref[...])

    def piece_spec(l):
        return pl.BlockSpec((N_DEV, rb, cols), lambda ly, i: (0, jnp.where(ly == l, i, 0), 0))

    blk = pl.BlockSpec((rb, cols), lambda ly, i: (ly * n_blk + i, 0))
    return pl.pallas_call(
        body, name=name, grid=(n_layers, n_blk),
        in_specs=[piece_spec(l) for l in range(n_layers)] + [blk, blk, blk],
        out_specs=[blk, blk, blk, blk],
        out_shape=[SDS((n_layers * rows, cols), F32)] * 4, compiler_params=_params(2),
    )(*pieces, w, m, v)


_HBM = pl.BlockSpec(memory_space=pltpu.HBM)
_SEM = pl.BlockSpec(memory_space=pltpu.SEMAPHORE)
_EFFECT = pltpu.SideEffectType.DATAFLOW_SIDE_EFFECTING
N_GATHER_PEERS = N_CHIPS - 1
N_EXCHANGE_PEERS = N_DEV - 1


def _gather_copies(srcs, lands, send_sems, recv_sems):
    x, y, c = lax.axis_index("x"), lax.axis_index("y"), lax.axis_index("c")
    mine = 2 * x + y
    chips = [(1 - x, y), (x, 1 - y), (1 - x, 1 - y)]
    out, inc = [], []
    for a in range(len(srcs)):
        for k, (px, py) in enumerate(chips):
            j = a * N_GATHER_PEERS + k
            sems = dict(send_sem=send_sems.at[j], recv_sem=recv_sems.at[j], device_id=(px, py, c),
                        device_id_type=pl.DeviceIdType.MESH)
            out.append(pltpu.make_async_remote_copy(src_ref=srcs[a], dst_ref=lands[a].at[mine], **sems))
            inc.append(pltpu.make_async_remote_copy(src_ref=srcs[a], dst_ref=lands[a].at[2 * px + py], **sems))
    return out, inc


def _exchange_copies(n_scatter):
    def copies(srcs, lands, send_sems, recv_sems):
        x, y, c = lax.axis_index("x"), lax.axis_index("y"), lax.axis_index("c")
        me = 4 * x + 2 * y + c
        peers = [(x ^ (k >> 2), y ^ ((k >> 1) & 1), c ^ (k & 1)) for k in range(1, N_DEV)]
        out, inc = [], []
        for a in range(len(srcs)):
            for k, (px, py, pc) in enumerate(peers):
                j = a * N_EXCHANGE_PEERS + k
                sems = dict(send_sem=send_sems.at[j], recv_sem=recv_sems.at[j], device_id=(px, py, pc),
                            device_id_type=pl.DeviceIdType.MESH)
                theirs = srcs[a].at[2 * px + py] if a < n_scatter else srcs[a]
                mine = srcs[a].at[2 * x + y] if a < n_scatter else srcs[a]
                out.append(pltpu.make_async_remote_copy(src_ref=theirs, dst_ref=lands[a].at[me], **sems))
                inc.append(pltpu.make_async_remote_copy(src_ref=mine, dst_ref=lands[a].at[4 * px + 2 * py + pc], **sems))
        return out, inc

    return copies


def _split_start(groups, copies_fn, n_peers, name):
    sizes = [len(srcs) for srcs, _ in groups]
    flat = [a for srcs, lands in groups for a in list(srcs) + list(lands)]
    n_flat, n_grp = len(flat), len(groups)

    def body(*refs):
        sems = refs[2 * n_flat:2 * n_flat + 2 * n_grp]
        token = refs[-1]
        at = 0
        for gi, n in enumerate(sizes):
            out, _ = copies_fn(refs[at:at + n], refs[at + n:at + 2 * n], sems[2 * gi], sems[2 * gi + 1])
            for cp in out:
                cp.start()
            at += 2 * n
        token[...] = jnp.zeros_like(token)

    sem_shapes = []
    for n in sizes:
        sem_shapes += [pltpu.SemaphoreType.DMA((n * n_peers,)), pltpu.SemaphoreType.DMA((n * n_peers,))]
    res = pl.pallas_call(
        body, name=name,
        out_shape=(*[pltpu.HBM(a.shape, a.dtype) for a in flat], *sem_shapes, SDS((8, 128), F32)),
        in_specs=[_HBM] * n_flat,
        out_specs=(*[_HBM] * n_flat, *[_SEM] * (2 * n_grp), pl.BlockSpec(memory_space=pltpu.VMEM)),
        input_output_aliases={i: i for i in range(n_flat)},
        compiler_params=pltpu.CompilerParams(has_side_effects=_EFFECT),
    )(*[pltpu.with_memory_space_constraint(a, pltpu.HBM) for a in flat])
    handles, at = [], 0
    for gi, n in enumerate(sizes):
        handles.append((res[n_flat + 2 * gi], res[n_flat + 2 * gi + 1], list(res[at:at + n]), list(res[at + n:at + 2 * n])))
        at += 2 * n
    return handles, res[-1]


def _split_wait(handle, after, copies_fn, name):
    send_sems, recv_sems, srcs, lands = handle
    n = len(srcs)
    after = list(after) if isinstance(after, (list, tuple)) else [after]

    def body(*refs):
        out, inc = copies_fn(refs[:n], refs[n:2 * n], refs[2 * n], refs[2 * n + 1])
        for cp in out:
            cp.wait_send()
        for cp in inc:
            cp.wait_recv()

    flat = list(srcs) + list(lands)
    res = pl.pallas_call(
        body, name=name,
        out_shape=tuple(pltpu.HBM(a.shape, a.dtype) for a in flat),
        in_specs=[_HBM] * (2 * n) + [_SEM, _SEM] + [pl.BlockSpec(memory_space=pl.ANY)] * len(after),
        out_specs=tuple([_HBM] * (2 * n)),
        input_output_aliases={i: i for i in range(2 * n)},
        compiler_params=pltpu.CompilerParams(has_side_effects=_EFFECT),
    )(*flat, send_sems, recv_sems, *after)
    return list(res[n:])


def _landing(own, slot, n_slots):
    return lax.dynamic_update_index_in_dim(lax.empty((n_slots,) + own.shape, own.dtype), own, slot, 0)


def _ssm_discretize(lam_re, lam_im, log_dt, b_re, b_im):
    lr = jnp.minimum(lam_re, LAMBDA_RE_MAX)
    li = lam_im
    dt = jnp.exp(log_dt)[:, None]
    mag = jnp.exp(lr * dt)
    ar, ai = mag * jnp.cos(li * dt), mag * jnp.sin(li * dt)
    den = lr * lr + li * li
    nr, ni = ar - 1.0, ai
    gr, gi = (nr * lr + ni * li) / den, (ni * lr - nr * li) / den
    bbr = gr[:, :, None] * b_re - gi[:, :, None] * b_im
    bbi = gr[:, :, None] * b_im + gi[:, :, None] * b_re
    return ar, ai, bbr, bbi


def _pair_lanes(t):
    return t.reshape(N_PAIR, 2 * SSM_STATE)


def _chan_state_blocks(t_gcp):
    t = t_gcp.reshape(N_PAIR, 2, SSM_GROUP, SSM_STATE)
    eye2 = jnp.eye(2, dtype=t.dtype)
    blk = jnp.einsum("rgcp,gh->rgchp", t, eye2).reshape(N_PAIR, 2 * SSM_GROUP, 2 * SSM_STATE)
    place = jax.nn.one_hot(jnp.arange(N_PAIR) % PAIRS_PER_CHUNK, PAIRS_PER_CHUNK, dtype=t.dtype)
    return jnp.einsum("rcl,rj->rjcl", blk, place).reshape(N_PAIR, 128, 2 * SSM_STATE)


def _chan_state_unblock(t):
    t = t.reshape(N_PAIR, PAIRS_PER_CHUNK, 2, SSM_GROUP, 2, SSM_STATE)
    place = jax.nn.one_hot(jnp.arange(N_PAIR) % PAIRS_PER_CHUNK, PAIRS_PER_CHUNK, dtype=t.dtype)
    t = jnp.einsum("rjgchp,rj->rgchp", t, place)
    t = jnp.einsum("rgchp,gh->rgcp", t, jnp.eye(2, dtype=t.dtype))
    return t.reshape(SSM_NG, SSM_GROUP, SSM_STATE)


def _scan_tables(zr, zi, reverse, seg):
    zr, zi = _pair_lanes(zr), _pair_lanes(-zi if reverse else zi)
    a = (jnp.exp(zr) * jnp.cos(zi), jnp.exp(zr) * jnp.sin(zi))
    cmul = lambda p, q: (p[0] * q[0] - p[1] * q[1], p[0] * q[1] + p[1] * q[0])
    big, square, bits = None, a, seg
    while bits:
        if bits & 1:
            big = square if big is None else cmul(big, square)
        square, bits = cmul(square, square), bits >> 1
    powers = [a, big]
    for _ in range(2):
        powers.append(cmul(powers[-1], powers[-1]))
    rows = jnp.arange(8)[None, :, None]
    tiles = [jnp.broadcast_to(part[:, None, :], (N_PAIR, 8, 128)) for part in powers[0]]
    for lvl, step in enumerate((1, 2, 4)):
        keep = (rows <= 7 - step) if reverse else (rows >= step)
        for part in powers[1 + lvl]:
            tiles.append(jnp.where(keep, part[:, None, :], 0.0))
    return jnp.stack(tiles, axis=1)


def _pairs_to_chunks(t):
    n_chunk = N_PAIR // PAIRS_PER_CHUNK
    return jnp.swapaxes(t.reshape(n_chunk, PAIRS_PER_CHUNK, 128, 256), 1, 2).reshape(n_chunk, 128, XW)


def _chunks_to_pairs(t):
    n_chunk = N_PAIR // PAIRS_PER_CHUNK
    return jnp.swapaxes(t.reshape(n_chunk, 128, PAIRS_PER_CHUNK, 256), 1, 2).reshape(N_PAIR, 128, 256)


def _ssm_operands(w, lp):
    seg = _row_tile(lp, SSM_TILES) // 8
    ar, ai, bbr, bbi = _ssm_discretize(w["ssm_lambda_re"], w["ssm_lambda_im"], w["ssm_log_dt"], w["ssm_b_re"], w["ssm_b_im"])
    b_blk = jnp.concatenate([_chan_state_blocks(jnp.swapaxes(bbr, 1, 2)), _chan_state_blocks(jnp.swapaxes(bbi, 1, 2))], axis=2)
    c_blk = jnp.concatenate([_chan_state_blocks(w["ssm_c_re"]), -_chan_state_blocks(w["ssm_c_im"])], axis=2)
    dt = jnp.exp(w["ssm_log_dt"])[:, None]
    zr, zi = jnp.minimum(w["ssm_lambda_re"], LAMBDA_RE_MAX) * dt, w["ssm_lambda_im"] * dt
    b_cat, c_cat = _pairs_to_chunks(b_blk).astype(BF16), _pairs_to_chunks(c_blk).astype(BF16)
    return (b_cat, jnp.swapaxes(b_cat, 1, 2), c_cat, jnp.swapaxes(c_cat, 1, 2),
            _scan_tables(zr, zi, False, seg), _scan_tables(zr, zi, True, seg))


def _local_step(x, target, w, late_weights, on_grads):
    n_ex, seq, _ = x.shape
    lp = seq + BLOCK
    nb = lp // BLOCK
    n_rows = n_ex * lp
    g = {}

    head = jnp.concatenate([jnp.zeros((PAD, D_MODEL), F32), w["meta_tokens"]], axis=0)
    h0 = jnp.concatenate([jnp.broadcast_to(head[None], (n_ex, BLOCK, D_MODEL)), x], axis=1).reshape(n_rows, D_MODEL)

    qkv, hn_a = _rms_mm_cols(h0, w["attn_norm_w"], w["attn_w_qkv"], "qkv_fwd")
    att, lse = _attn_fwd(qkv, w["attn_sinks"], n_ex, nb)
    h1 = _mm_acc(att, w["attn_w_o"], False, "attn_out_fwd", res=h0)
    w = {**w, **late_weights(0, att)}
    h2, a0, hn_m0, u = _mlp_fwd(h1, w["mlp_norm_w"][0:1], w["mlp_w_up"][0], w["mlp_w_down"][0], "mlp0_fwd",
                                next_norm=w["ssm_norm_w"])
    late = late_weights(1, h2)
    w["ssm_w_glu"] = late["ssm_w_glu"]
    w["mlp_w_up"], w["mlp_w_down"] = w["mlp_w_up"] + late["mlp_w_up"], w["mlp_w_down"] + late["mlp_w_down"]

    ops = w["ssm_operands"] if "ssm_operands" in w else _ssm_operands(w, lp)
    b_pad, bt_pad, ct_pad, c_pad, tab_fwd, tab_rev = ops
    yg, y, xs = _ssm_fwd(u, b_pad, c_pad, tab_fwd, w["ssm_d"], n_ex, lp)
    z = _mm_cols(yg, w["ssm_w_glu"], False, "glu_mm_fwd")
    h4, a1, hn_m1, h3 = _mlp_fwd(h2, w["mlp_norm_w"][1:2], w["mlp_w_up"][1], w["mlp_w_down"][1], "mlp1_fwd", glu_z=z)

    dh4, loss_tile, dnorm_f = _loss_head(h4, w["final_norm_w"], target.reshape(n_ex * seq, D_MODEL), n_ex, nb)

    def mlp_bwd(dh_out, h_in, a, hn, layer, tag, norm_w):
        dhn, dw_up, dw_down = None, None, None
        for s in range(N_CHIPS):
            final = s == N_CHIPS - 1
            res = _mlp_bwd_shard(s, dh_out, a, hn, dhn, h_in if final else None, norm_w,
                                 w["mlp_w_up"][layer], w["mlp_w_down"][layer], dw_up, dw_down, f"{tag}_bwd{s}")
            dhn, dw_up, dw_down = res[:3]
        return dhn, res[3], dw_up, dw_down

    dh3, dnorm_m1, dwu1, dwd1 = mlp_bwd(dh4, h3, a1, hn_m1, 1, "mlp1", w["mlp_norm_w"][1:2])
    tok = on_grads("mlp1", {"mlp_w_up": dwu1, "mlp_w_down": dwd1})
    dz, dyg = _glu_bwd(dh3, z, w["ssm_w_glu"])
    g["ssm_w_glu"] = _mm_tn(yg, dz, N_CHIPS, False, "glu_mm_dw")
    du, db_blk, dc_blk, da_t, dd_t = _ssm_bwd(dyg, y, u, xs, ct_pad, bt_pad, tab_rev, w["ssm_d"] + tok, n_ex, lp)
    dh2, dnorm_s = _rms_bwd_call(du, h2, w["ssm_norm_w"], dh3, "ssm_norm_bwd")
    db_blk, dc_blk = _chunks_to_pairs(db_blk), _chunks_to_pairs(dc_blk)
    g["ssm_c_re"] = _chan_state_unblock(dc_blk[:, :, 0:128])
    g["ssm_c_im"] = -_chan_state_unblock(dc_blk[:, :, 128:256])
    g_bbr = jnp.swapaxes(_chan_state_unblock(db_blk[:, :, 0:128]), 1, 2)
    g_bbi = jnp.swapaxes(_chan_state_unblock(db_blk[:, :, 128:256]), 1, 2)
    g_a = jnp.sum(da_t, axis=2).reshape(N_PAIR, 2, 2, SSM_STATE)
    g_ar, g_ai = g_a[:, 0].reshape(SSM_NG, SSM_STATE), g_a[:, 1].reshape(SSM_NG, SSM_STATE)
    _, vjp = jax.vjp(_ssm_discretize, w["ssm_lambda_re"], w["ssm_lambda_im"], w["ssm_log_dt"], w["ssm_b_re"], w["ssm_b_im"])
    g["ssm_lambda_re"], g["ssm_lambda_im"], g["ssm_log_dt"], g["ssm_b_re"], g["ssm_b_im"] = vjp((g_ar, g_ai, g_bbr, g_bbi))
    tok = on_grads("ssm", g)
    g = {}
    dh1, dnorm_m0, dwu0, dwd0 = mlp_bwd(dh2, h1, a0, hn_m0, 0, "mlp0", w["mlp_norm_w"][0:1] + tok)
    datt = _mm_cols(dh1, w["attn_w_o"], True, "attn_out_dx")
    dw_o = _mm_tn(att, dh1, N_CHIPS, True, "attn_out_dw")
    tok = on_grads("mlp0", {"mlp_w_up": dwu0, "mlp_w_down": dwd0, "attn_w_o": dw_o})
    dqkv, dsink_rows = _attn_bwd(qkv, w["attn_sinks"] + tok, att, lse, datt, n_ex, nb)
    tok = on_grads("qkv", {"attn_w_qkv": _mm_tn(hn_a, dqkv, N_CHIPS, False, "qkv_dw")})
    dh0, dnorm_a = _mm_acc(dqkv, w["attn_w_qkv"], True, "qkv_dx", rms_bwd=(h0, w["attn_norm_w"] + tok, dh1))

    dh0 = dh0.reshape(n_ex, lp, D_MODEL)
    on_grads("rest", {
        "mlp_norm_w": jnp.stack([jnp.sum(dnorm_m0, axis=0), jnp.sum(dnorm_m1, axis=0)]),
        "final_norm_w": jnp.sum(dnorm_f, axis=0),
        "attn_norm_w": jnp.sum(dnorm_a, axis=0)[None],
        "ssm_norm_w": jnp.sum(dnorm_s, axis=0)[None],
        "attn_sinks": jnp.sum(dsink_rows, axis=0)[None],
        "ssm_d": jnp.sum(dd_t, axis=0)[None],
        "meta_tokens": jnp.sum(dh0[:, PAD:BLOCK], axis=0),
        "loss": loss_tile[0, 0:1]})
    return loss_tile, dh0[:, BLOCK:]


_SHARDED_SMALL = ("meta_tokens", "ssm_norm_w", "ssm_d")
_REP_SSM = ("ssm_lambda_re", "ssm_lambda_im", "ssm_log_dt", "ssm_b_re", "ssm_b_im", "ssm_c_re", "ssm_c_im")
_REP_MISC = ("attn_norm_w", "attn_sinks", "mlp_norm_w", "final_norm_w")
_BIG = ("attn_w_qkv", "attn_w_o", "ssm_w_glu", "mlp_w_up", "mlp_w_down")


def _pack(parts, cols):
    flat = jnp.concatenate([p.reshape(-1) for p in parts])
    rows = -(-flat.shape[0] // (8 * cols)) * 8
    return jnp.pad(flat, (0, rows * cols - flat.shape[0])).reshape(rows, cols)


def _unpack(packed, like):
    flat = packed.reshape(-1)
    out, at = [], 0
    for p in like:
        out.append(flat[at:at + p.size].reshape(p.shape))
        at += p.size
    return out


def kernel(x, meta_tokens, attn_norm_w, attn_w_qkv, attn_sinks, attn_w_o, ssm_norm_w, ssm_lambda_re, ssm_lambda_im, ssm_log_dt, ssm_b_re, ssm_b_im, ssm_c_re, ssm_c_im, ssm_d, ssm_w_glu, mlp_norm_w, mlp_w_up, mlp_w_down, final_norm_w, loss_target, m_meta_tokens, m_attn_norm_w, m_attn_w_qkv, m_attn_sinks, m_attn_w_o, m_ssm_norm_w, m_ssm_lambda_re, m_ssm_lambda_im, m_ssm_log_dt, m_ssm_b_re, m_ssm_b_im, m_ssm_c_re, m_ssm_c_im, m_ssm_d, m_ssm_w_glu, m_mlp_norm_w, m_mlp_w_up, m_mlp_w_down, m_final_norm_w, v_meta_tokens, v_attn_norm_w, v_attn_w_qkv, v_attn_sinks, v_attn_w_o, v_ssm_norm_w, v_ssm_lambda_re, v_ssm_lambda_im, v_ssm_log_dt, v_ssm_b_re, v_ssm_b_im, v_ssm_c_re, v_ssm_c_im, v_ssm_d, v_ssm_w_glu, v_mlp_norm_w, v_mlp_w_up, v_mlp_w_down, v_final_norm_w):
    names = ("meta_tokens", "attn_norm_w", "attn_w_qkv", "attn_sinks", "attn_w_o", "ssm_norm_w", "ssm_lambda_re",
             "ssm_lambda_im", "ssm_log_dt", "ssm_b_re", "ssm_b_im", "ssm_c_re", "ssm_c_im", "ssm_d", "ssm_w_glu",
             "mlp_norm_w", "mlp_w_up", "mlp_w_down", "final_norm_w")
    wts = dict(zip(names, (meta_tokens, attn_norm_w, attn_w_qkv, attn_sinks, attn_w_o, ssm_norm_w, ssm_lambda_re,
                           ssm_lambda_im, ssm_log_dt, ssm_b_re, ssm_b_im, ssm_c_re, ssm_c_im, ssm_d, ssm_w_glu,
                           mlp_norm_w, mlp_w_up, mlp_w_down, final_norm_w)))
    mom = dict(zip(names, (m_meta_tokens, m_attn_norm_w, m_attn_w_qkv, m_attn_sinks, m_attn_w_o, m_ssm_norm_w,
                           m_ssm_lambda_re, m_ssm_lambda_im, m_ssm_log_dt, m_ssm_b_re, m_ssm_b_im, m_ssm_c_re,
                           m_ssm_c_im, m_ssm_d, m_ssm_w_glu, m_mlp_norm_w, m_mlp_w_up, m_mlp_w_down, m_final_norm_w)))
    var = dict(zip(names, (v_meta_tokens, v_attn_norm_w, v_attn_w_qkv, v_attn_sinks, v_attn_w_o, v_ssm_norm_w,
                           v_ssm_lambda_re, v_ssm_lambda_im, v_ssm_log_dt, v_ssm_b_re, v_ssm_b_im, v_ssm_c_re,
                           v_ssm_c_im, v_ssm_d, v_ssm_w_glu, v_mlp_norm_w, v_mlp_w_up, v_mlp_w_down, v_final_norm_w)))

    my_chip = 2 * lax.axis_index("x") + lax.axis_index("y")
    my_dev = 2 * my_chip + lax.axis_index("c")
    small_mine = _pack([wts[n] for n in _SHARDED_SMALL], 128)
    first = [attn_w_qkv.astype(BF16), attn_w_o.astype(BF16), small_mine]
    up16, down16 = mlp_w_up.astype(BF16), mlp_w_down.astype(BF16)
    mlp0 = [up16[0:1], down16[0:1]]
    rest = [ssm_w_glu.astype(BF16), up16[1:2], down16[1:2]]
    handles, _ = _split_start([(srcs, [_landing(a, my_chip, N_CHIPS) for a in srcs]) for srcs in (first, mlp0, rest)],
                              _gather_copies, N_GATHER_PEERS, "gather_start")
    full = {n: wts[n] for n in _REP_MISC}
    full["final_norm_w"] = final_norm_w[None]
    for n in _REP_SSM:
        full[n] = wts[n][0]
    full["ssm_operands"] = _ssm_operands(full, x.shape[1] + BLOCK)
    got = _split_wait(handles[0], full["ssm_operands"], _gather_copies, "gather_wait_first")
    full["attn_w_qkv"], full["attn_w_o"] = got[0], got[1]
    smalls = [_unpack(got[2][s], [wts[n] for n in _SHARDED_SMALL]) for s in range(N_CHIPS)]
    for k, n in enumerate(_SHARDED_SMALL):
        full[n] = jnp.concatenate([smalls[s][k] for s in range(N_CHIPS)], axis=1)

    def late_weights(stage, after):
        if stage == 0:
            up, down = _split_wait(handles[1], after, _gather_copies, "gather_wait_mlp0")
            return {"mlp_w_up": [up], "mlp_w_down": [down]}
        glu, up, down = _split_wait(handles[2], after, _gather_copies, "gather_wait_rest")
        return {"ssm_w_glu": glu, "mlp_w_up": [up], "mlp_w_down": [down]}

    def shard_cols(t):
        return jnp.swapaxes(t.reshape(t.shape[0], N_CHIPS, t.shape[1] // N_CHIPS), 0, 1)

    pending = {}

    def on_grads(tag, g):
        scatter = [g[n] for n in _BIG if n in g]
        whole = []
        if tag == "ssm":
            whole = [_pack([g[n] for n in _REP_SSM], D_MODEL)]
        if tag == "rest":
            parts = [shard_cols(g[n]) for n in _SHARDED_SMALL]
            scatter = [jnp.stack([_pack([p[s] for p in parts], 128) for s in range(N_CHIPS)])]
            whole = [_pack([g[n] for n in _REP_MISC] + [g["loss"]], D_MODEL)]
        srcs = scatter + whole
        lands = [_landing(lax.dynamic_index_in_dim(a, my_chip, 0, keepdims=False), my_dev, N_DEV) for a in scatter]
        lands += [_landing(a, my_dev, N_DEV) for a in whole]
        hs, token = _split_start([(srcs, lands)], _exchange_copies(len(scatter)), N_EXCHANGE_PEERS, "exchange_start_" + tag)
        pending[tag] = (hs[0], len(scatter))
        return token[0, 0]

    _, grad_x = _local_step(x, loss_target, full, late_weights, on_grads)

    recv = {}
    for tag, (handle, n_scatter) in pending.items():
        recv[tag] = _split_wait(handle, grad_x, _exchange_copies(n_scatter), "exchange_wait_" + tag)
    loss = jnp.sum(recv["rest"][1].reshape(N_DEV, -1)[:, sum(wts[n].size for n in _REP_MISC)])

    out = {}

    def update(tag, pieces, w2, m2, v2):
        return _adamw(pieces, w2, m2, v2, "adamw_" + tag)

    def update_weight(n, pieces):
        shp = wts[n].shape
        r2 = (math.prod(shp[:-1]), shp[-1])
        res = update(n, pieces, wts[n].reshape(r2), mom[n].reshape(r2), var[n].reshape(r2))
        out[n] = [t.reshape(shp) for t in res]

    update_weight("mlp_w_up", [recv["mlp0"][1], recv["mlp1"][0]])
    update_weight("mlp_w_down", [recv["mlp0"][2], recv["mlp1"][1]])
    update_weight("attn_w_o", [recv["mlp0"][0]])
    update_weight("ssm_w_glu", [recv["ssm"][0]])
    update_weight("attn_w_qkv", [recv["qkv"][0]])
    for tag, group, pieces, cols in (("small", _SHARDED_SMALL, recv["rest"][0], 128),
                                     ("rep_ssm", _REP_SSM, recv["ssm"][1], D_MODEL),
                                     ("rep_misc", _REP_MISC, recv["rest"][1], D_MODEL)):
        like = [wts[n] for n in group]
        res = update(tag, [pieces], _pack(like, cols), _pack([mom[n] for n in group], cols),
                     _pack([var[n] for n in group], cols))
        for k, n in enumerate(group):
            out[n] = [_unpack(t, like)[k] for t in res]

    return (loss, grad_x, *[out[n][0] for n in names], *[out[n][1] for n in names],
            *[out[n][2] for n in names], *[out[n][3] for n in names])
```

```python
import functools
import math

import jax
import jax.numpy as jnp
from jax import lax
from jax.experimental import pallas as pl
from jax.experimental.pallas import tpu as pltpu

F32 = jnp.float32
BF16 = jnp.bfloat16
SDS = jax.ShapeDtypeStruct

D_MODEL = 1024
N_HEADS = 16
N_KV = 4
GQA = N_HEADS // N_KV
HEAD_DIM = 64
BLOCK = 128
N_META = 16
PAD = BLOCK - N_META
QKV_DIM = (N_HEADS + 2 * N_KV) * HEAD_DIM
KV_DIM = 2 * N_KV * HEAD_DIM
D_FF = 4 * D_MODEL
N_CHIPS = 4
N_DEV = 8
SSM_GROUP = 16
SSM_NG = D_MODEL // SSM_GROUP
SSM_STATE = 64
N_PAIR = SSM_NG // 2
PAIRS_PER_CHUNK = 4
RMS_EPS = 1e-6
NEG_INF = -1e30
LAMBDA_RE_MAX = -1e-4
ADAM_LR, ADAM_B1, ADAM_B2, ADAM_EPS, ADAM_WD, ADAM_STEP = 0.001, 0.9, 0.999, 1e-08, 0.01, 10

TM = 384
MM_TILES = (1056, 768, 384)
MLP_FWD_TILES = (384,)
MLP_BWD_TILES = (768, 384)
TN_TILES = (1408, 768, 384)
VMEM_LIMIT = 56 * 1024 * 1024


def _params(n_grid):
    return pltpu.CompilerParams(dimension_semantics=("arbitrary",) * n_grid, vmem_limit_bytes=VMEM_LIMIT)


def _row_tile(n_rows, tiles):
    return next(t for t in tiles if n_rows % t == 0)


def _rms(h, w):
    r = lax.rsqrt(jnp.mean(h * h, axis=-1, keepdims=True) + RMS_EPS)
    return h * r * w


def _rms_bwd(dhn, h, w):
    r = lax.rsqrt(jnp.mean(h * h, axis=-1, keepdims=True) + RMS_EPS)
    g = dhn * w
    proj = jnp.sum(g * h, axis=-1, keepdims=True) * (1.0 / D_MODEL)
    return r * g - h * (r * r * r) * proj, dhn * h * r


def _fold8(t):
    return jnp.sum(t.reshape(t.shape[0] // 8, 8, t.shape[1]), axis=0)


def _gelu(y):
    return 0.5 * y * (1.0 + jnp.tanh(0.7978845608028654 * (y + 0.044715 * y * y * y)))


def _gelu_grad(y):
    t = jnp.tanh(0.7978845608028654 * (y + 0.044715 * y * y * y))
    return 0.5 * (1.0 + t) + 0.5 * y * (1.0 - t * t) * 0.7978845608028654 * (1.0 + 3.0 * 0.044715 * y * y)


def _w4_spec(w4):
    n_sh, _, k, n = w4.shape
    return pl.BlockSpec((n_sh, None, k, n), lambda i: (0, 0, 0, 0))


def _rms_mm_cols(h, wn, w4, name):
    n_rows = h.shape[0]
    n_sh, _, k, n = w4.shape
    tm = _row_tile(n_rows, MM_TILES)

    def body(h_ref, wn_ref, w_ref, o_ref, hn_ref):
        hn = _rms(h_ref[...], wn_ref[...]).astype(BF16)
        hn_ref[...] = hn
        for s in range(n_sh):
            o_ref[:, s * n:(s + 1) * n] = jnp.dot(hn, w_ref[s], preferred_element_type=F32).astype(o_ref.dtype)

    return pl.pallas_call(
        body, name=name, grid=(n_rows // tm,),
        in_specs=[pl.BlockSpec((tm, k), lambda i: (i, 0)), pl.BlockSpec((1, k), lambda i: (0, 0)), _w4_spec(w4)],
        out_specs=[pl.BlockSpec((tm, n_sh * n), lambda i: (i, 0)), pl.BlockSpec((tm, k), lambda i: (i, 0))],
        out_shape=[SDS((n_rows, n_sh * n), BF16), SDS((n_rows, k), BF16)],
        compiler_params=_params(1),
    )(h, wn, w4)


def _mm_cols(x, w4, trans_w, name):
    n_rows, kx = x.shape
    tm = _row_tile(n_rows, MM_TILES)
    n_sh, _, k, n = w4.shape
    n_out = k if trans_w else n
    dims = (((1,), (1,)), ((), ())) if trans_w else (((1,), (0,)), ((), ()))

    def body(x_ref, w_ref, o_ref):
        x16 = x_ref[...].astype(BF16)
        for s in range(n_sh):
            o_ref[:, s * n_out:(s + 1) * n_out] = lax.dot_general(
                x16, w_ref[s], dims, preferred_element_type=F32).astype(o_ref.dtype)

    return pl.pallas_call(
        body, name=name, grid=(n_rows // tm,),
        in_specs=[pl.BlockSpec((tm, kx), lambda i: (i, 0)), _w4_spec(w4)],
        out_specs=pl.BlockSpec((tm, n_sh * n_out), lambda i: (i, 0)),
        out_shape=SDS((n_rows, n_sh * n_out), BF16),
        compiler_params=_params(1),
    )(x, w4)


def _mm_acc(x, w4, trans_w, name, res=None, rms_bwd=None, out_dtype=F32):
    n_rows = x.shape[0]
    tm = _row_tile(n_rows, MM_TILES)
    n_sh, _, k, n = w4.shape
    kx, n_out = (n, k) if trans_w else (k, n)
    dims = (((1,), (1,)), ((), ())) if trans_w else (((1,), (0,)), ((), ()))

    def body(*refs):
        if rms_bwd is not None:
            x_ref, w_ref, h_ref, wn_ref, dres_ref, o_ref, dw_ref = refs
        elif res is not None:
            x_ref, w_ref, res_ref, o_ref = refs
        else:
            x_ref, w_ref, o_ref = refs
        acc = None
        for s in range(n_sh):
            part = lax.dot_general(x_ref[:, s * kx:(s + 1) * kx].astype(BF16), w_ref[s], dims, preferred_element_type=F32)
            acc = part if acc is None else acc + part
        if rms_bwd is not None:
            dh, dw_rows = _rms_bwd(acc, h_ref[...], wn_ref[...])
            o_ref[...] = (dres_ref[...] + dh).astype(o_ref.dtype)

            @pl.when(pl.program_id(0) == 0)
            def _():
                dw_ref[...] = jnp.zeros_like(dw_ref)

            dw_ref[...] += _fold8(dw_rows)
        elif res is not None:
            o_ref[...] = (res_ref[...] + acc).astype(o_ref.dtype)
        else:
            o_ref[...] = acc.astype(o_ref.dtype)

    row = lambda i: (i, 0)
    in_specs = [pl.BlockSpec((tm, n_sh * kx), row), _w4_spec(w4)]
    args = [x, w4]
    out_specs = pl.BlockSpec((tm, n_out), row)
    out_shape = SDS((n_rows, n_out), out_dtype)
    if rms_bwd is not None:
        h, wn, dres = rms_bwd
        in_specs += [pl.BlockSpec((tm, n_out), row), pl.BlockSpec((1, n_out), lambda i: (0, 0)),
                     pl.BlockSpec((tm, n_out), row)]
        args += [h, wn, dres]
        out_specs = [out_specs, pl.BlockSpec((8, n_out), lambda i: (0, 0))]
        out_shape = [out_shape, SDS((8, n_out), F32)]
    elif res is not None:
        in_specs.append(pl.BlockSpec((tm, n_out), row))
        args.append(res)
    return pl.pallas_call(
        body, name=name, grid=(n_rows // tm,), in_specs=in_specs, out_specs=out_specs, out_shape=out_shape,
        compiler_params=_params(1),
    )(*args)


def _mm_tn(a, b, n_sh, a_sharded, name):
    n_rows = a.shape[0]
    tm = _row_tile(n_rows, TN_TILES)
    ka = a.shape[1] // n_sh if a_sharded else a.shape[1]
    nb = b.shape[1] if a_sharded else b.shape[1] // n_sh
    n_i = n_rows // tm

    def body(a_ref, b_ref, o_ref, acc):
        i = pl.program_id(0)

        @pl.when(i == 0)
        def _():
            acc[...] = jnp.zeros_like(acc)

        for s in range(n_sh):
            a_s = a_ref[:, s * ka:(s + 1) * ka] if a_sharded else a_ref[...]
            b_s = b_ref[...] if a_sharded else b_ref[:, s * nb:(s + 1) * nb]
            acc[s] += lax.dot_general(a_s.astype(BF16), b_s.astype(BF16), (((0,), (0,)), ((), ())),
                                      preferred_element_type=F32)

        @pl.when(i == n_i - 1)
        def _():
            o_ref[...] = acc[...].astype(o_ref.dtype)

    return pl.pallas_call(
        body, name=name, grid=(n_i,),
        in_specs=[pl.BlockSpec((tm, a.shape[1]), lambda i: (i, 0)), pl.BlockSpec((tm, b.shape[1]), lambda i: (i, 0))],
        out_specs=pl.BlockSpec((n_sh, ka, nb), lambda i: (0, 0, 0)),
        out_shape=SDS((n_sh, ka, nb), BF16),
        scratch_shapes=[pltpu.VMEM((n_sh, ka, nb), F32)], compiler_params=_params(1),
    )(a, b)


def _mlp_fwd(h, wn, w_up4, w_down4, name, next_norm=None, glu_z=None):
    n_rows = h.shape[0]
    tm = _row_tile(n_rows, MLP_FWD_TILES)
    n_sh = w_up4.shape[0]
    f_sh = D_FF // n_sh
    w_down = w_down4.reshape(D_FF, D_MODEL)

    def body(*refs):
        refs = list(refs)
        h_ref, wn_ref, wu_ref, wd_ref = refs[:4]
        at = 4
        if next_norm is not None:
            nn_ref = refs[at]
            at += 1
        if glu_z is not None:
            z_ref = refs[at]
            at += 1
        o_ref, a_ref, hn_ref = refs[at:at + 3]
        at += 3
        if next_norm is not None:
            u_ref = refs[at]
            at += 1
        if glu_z is not None:
            hin_ref = refs[at]
            at += 1
        act_s = refs[at]
        h_in = h_ref[...]
        if glu_z is not None:
            h_in = h_in + z_ref[:, 0:D_MODEL].astype(F32) * jax.nn.sigmoid(z_ref[:, D_MODEL:2 * D_MODEL].astype(F32))
            hin_ref[...] = h_in
        hn = _rms(h_in, wn_ref[...]).astype(BF16)
        hn_ref[...] = hn
        for s in range(n_sh):
            cols = slice(s * f_sh, (s + 1) * f_sh)
            a = jnp.dot(hn, wu_ref[s], preferred_element_type=F32)
            a_ref[:, cols] = a.astype(BF16)
            act = jnp.maximum(a, 0.0)
            act_s[:, cols] = (act * act).astype(BF16)
        out = h_in + jnp.dot(act_s[...], wd_ref[...], preferred_element_type=F32)
        o_ref[...] = out
        if next_norm is not None:
            u_ref[...] = _rms(out, nn_ref[...])

    row = lambda i: (i, 0)
    vec = pl.BlockSpec((1, D_MODEL), lambda i: (0, 0))
    in_specs = [pl.BlockSpec((tm, D_MODEL), row), vec,
                pl.BlockSpec((n_sh, None, D_MODEL, f_sh), lambda i: (0, 0, 0, 0), pipeline_mode=pl.Buffered(1)),
                pl.BlockSpec((D_FF, D_MODEL), lambda i: (0, 0), pipeline_mode=pl.Buffered(1))]
    out_specs = [pl.BlockSpec((tm, D_MODEL), row), pl.BlockSpec((tm, D_FF), row), pl.BlockSpec((tm, D_MODEL), row)]
    out_shape = [SDS((n_rows, D_MODEL), F32), SDS((n_rows, D_FF), BF16), SDS((n_rows, D_MODEL), BF16)]
    args = [h, wn, w_up4, w_down]
    if next_norm is not None:
        in_specs.append(vec)
        args.append(next_norm)
    if glu_z is not None:
        in_specs.append(pl.BlockSpec((tm, 2 * D_MODEL), row))
        args.append(glu_z)
    for extra in (next_norm, glu_z):
        if extra is not None:
            out_specs.append(pl.BlockSpec((tm, D_MODEL), row))
            out_shape.append(SDS((n_rows, D_MODEL), F32))
    return pl.pallas_call(
        body, name=name, grid=(n_rows // tm,), in_specs=in_specs, out_specs=out_specs, out_shape=out_shape,
        scratch_shapes=[pltpu.VMEM((tm, D_FF), BF16)],
        compiler_params=_params(1),
    )(*args)


def _mlp_bwd_shard(s, dh, a, hn, dhn_prev, h, wn, w_up4, w_down4, dw_up_buf, dw_down_buf, name):
    n_rows = dh.shape[0]
    n_sh = w_up4.shape[0]
    f_sh = D_FF // n_sh
    tm = _row_tile(n_rows, MLP_BWD_TILES)
    n_i = n_rows // tm
    last = h is not None
    nt = (((1,), (1,)), ((), ()))
    tn = (((0,), (0,)), ((), ()))

    def body(*refs):
        refs = list(refs)
        dh_ref, a_ref, hn_ref, wu_ref, wd_ref = refs[:5]
        at = 5
        prev_ref = None
        if dhn_prev is not None:
            prev_ref = refs[at]
            at += 1
        if last:
            h_ref, wn_ref = refs[at:at + 2]
            at += 2
        if dw_up_buf is not None:
            at += 2
        o_ref, dwu_ref, dwd_ref = refs[at:at + 3]
        at += 3
        if last:
            dnorm_ref = refs[at]
            at += 1
        acc_u, acc_d = refs[at:at + 2]
        i = pl.program_id(0)

        @pl.when(i == 0)
        def _():
            acc_u[...] = jnp.zeros_like(acc_u)
            acc_d[...] = jnp.zeros_like(acc_d)
            if last:
                dnorm_ref[...] = jnp.zeros_like(dnorm_ref)

        dh16 = dh_ref[...].astype(BF16)
        r = jnp.maximum(a_ref[...].astype(F32), 0.0)
        dact = lax.dot_general(dh16, wd_ref[...], nt, preferred_element_type=F32)
        da16 = (dact * (2.0 * r)).astype(BF16)
        acc_d[...] += lax.dot_general((r * r).astype(BF16), dh16, tn, preferred_element_type=F32)
        acc_u[...] += lax.dot_general(hn_ref[...], da16, tn, preferred_element_type=F32)
        dhn = lax.dot_general(da16, wu_ref[...], nt, preferred_element_type=F32)
        if prev_ref is not None:
            dhn = dhn + prev_ref[...]
        if last:
            d_rms, dw_rows = _rms_bwd(dhn, h_ref[...], wn_ref[...])
            o_ref[...] = dh_ref[...] + d_rms
            dnorm_ref[...] += _fold8(dw_rows)
        else:
            o_ref[...] = dhn

        @pl.when(i == n_i - 1)
        def _():
            dwu_ref[...] = acc_u[...].astype(BF16)
            dwd_ref[...] = acc_d[...].astype(BF16)

    row = lambda i: (i, 0)
    tile = pl.BlockSpec((tm, D_MODEL), row)
    in_specs = [tile, pl.BlockSpec((tm, f_sh), lambda i: (i, s)), tile,
                pl.BlockSpec((None, None, D_MODEL, f_sh), lambda i: (s, 0, 0, 0)),
                pl.BlockSpec((None, None, f_sh, D_MODEL), lambda i: (s, 0, 0, 0))]
    args = [dh, a, hn, w_up4, w_down4]
    if dhn_prev is not None:
        in_specs.append(tile)
        args.append(dhn_prev)
    if last:
        in_specs += [tile, pl.BlockSpec((1, D_MODEL), lambda i: (0, 0))]
        args += [h, wn]
    aliases = {}
    if dw_up_buf is not None:
        aliases = {len(args): 1, len(args) + 1: 2}
        in_specs += [pl.BlockSpec(memory_space=pl.ANY)] * 2
        args += [dw_up_buf, dw_down_buf]
    out_specs = [tile, pl.BlockSpec((None, D_MODEL, f_sh), lambda i: (s, 0, 0)),
                 pl.BlockSpec((None, f_sh, D_MODEL), lambda i: (s, 0, 0))]
    out_shape = [SDS((n_rows, D_MODEL), F32), SDS((n_sh, D_MODEL, f_sh), BF16), SDS((n_sh, f_sh, D_MODEL), BF16)]
    if last:
        out_specs.append(pl.BlockSpec((8, D_MODEL), lambda i: (0, 0)))
        out_shape.append(SDS((8, D_MODEL), F32))
    return pl.pallas_call(
        body, name=name, grid=(n_i,), in_specs=in_specs, out_specs=out_specs, out_shape=out_shape,
        input_output_aliases=aliases,
        scratch_shapes=[pltpu.VMEM((D_MODEL, f_sh), F32), pltpu.VMEM((f_sh, D_MODEL), F32)],
        compiler_params=_params(1),
    )(*args)


def _attn_masks(n):
    qi = lax.broadcasted_iota(jnp.int32, (BLOCK, 3 * BLOCK), 0)
    col = lax.broadcasted_iota(jnp.int32, (BLOCK, 3 * BLOCK), 1)
    kj = col - BLOCK
    dist = BLOCK + qi - kj
    kmin = jnp.where(n == 0, 2 * BLOCK, jnp.where(n == 1, BLOCK, 0))
    band_ok = (col >= BLOCK) & (dist >= 0) & (dist < BLOCK) & (kj >= kmin)
    q_pos = n * BLOCK + qi - PAD
    meta_ok = (col >= PAD) & (col < BLOCK) & (col - PAD <= q_pos)
    distf = jnp.where(col >= BLOCK, dist, 0).astype(F32)
    return band_ok | meta_ok, distf


def _alibi_slope(h):
    return float(2.0 ** (-8.0 * (h + 1) / N_HEADS))


def _attn_bias(n, bias_s):
    ok, distf = _attn_masks(n)
    for h in range(N_HEADS):
        bias_s[h] = jnp.where(ok, -_alibi_slope(h) * distf, NEG_INF)


def _attn_fwd(qkv, sinks, n_ex, nb):
    n_rows = qkv.shape[0]
    kvb = N_HEADS * HEAD_DIM // KV_DIM

    def body(sink_ref, q_ref, kvm_ref, kvp_ref, kvc_ref, o_ref, lse_ref, k_s, v_s, q_s, bias_s):
        n = pl.program_id(1)

        @pl.when(n <= 2)
        def _():
            _attn_bias(n, bias_s)

        v_s[...] = jnp.ones_like(v_s)
        for part, ref in enumerate((kvm_ref, kvp_ref, kvc_ref)):
            rows = slice(part * BLOCK, (part + 1) * BLOCK)
            k_s[rows, :] = ref[:, 0:N_KV * HEAD_DIM]
            for kv in range(N_KV):
                v_s[rows, kv * 2 * HEAD_DIM:kv * 2 * HEAD_DIM + HEAD_DIM] = \
                    ref[:, (N_KV + kv) * HEAD_DIM:(N_KV + kv + 1) * HEAD_DIM]
        def scores(kv):
            for g in range(GQA):
                h = kv * GQA + g
                q_s[kv, g * BLOCK:(g + 1) * BLOCK, :] = q_ref[:, h * HEAD_DIM:(h + 1) * HEAD_DIM] * (HEAD_DIM ** -0.5)
            return lax.dot_general(q_s[kv], k_s[:, kv * HEAD_DIM:(kv + 1) * HEAD_DIM], (((1,), (1,)), ((), ())),
                                   preferred_element_type=F32)

        ahead = scores(0)
        for kv in range(N_KV):
            s4 = ahead
            if kv + 1 < N_KV:
                ahead = scores(kv + 1)
            es, ms, sink_es = [], [], []
            for g in range(GQA):
                h = kv * GQA + g
                s = s4[g * BLOCK:(g + 1) * BLOCK] + bias_s[h]
                sink = sink_ref[0, h]
                m = jnp.maximum(jnp.max(s, axis=-1, keepdims=True), sink)
                es.append(jnp.exp(s - m).astype(BF16))
                ms.append(m)
                sink_es.append(jnp.exp(sink - m))
            pv = jnp.dot(jnp.concatenate(es, axis=0), v_s[:, kv * 2 * HEAD_DIM:(kv + 1) * 2 * HEAD_DIM],
                         preferred_element_type=F32)
            for g in range(GQA):
                h = kv * GQA + g
                pg = pv[g * BLOCK:(g + 1) * BLOCK]
                l = pg[:, HEAD_DIM:HEAD_DIM + 1] + sink_es[g]
                o_ref[:, h * HEAD_DIM:(h + 1) * HEAD_DIM] = (pg[:, 0:HEAD_DIM] * (1.0 / l)).astype(BF16)
                lse_ref[:, h:h + 1] = ms[g] + jnp.log(l)

    return pl.pallas_call(
        body, name="attn_fwd", grid=(n_ex, nb),
        in_specs=[pl.BlockSpec(memory_space=pltpu.SMEM),
                  pl.BlockSpec((BLOCK, N_HEADS * HEAD_DIM), lambda b, n: (b * nb + n, 0)),
                  pl.BlockSpec((BLOCK, KV_DIM), lambda b, n: (b * nb, kvb)),
                  pl.BlockSpec((BLOCK, KV_DIM), lambda b, n: (b * nb + jnp.maximum(n - 1, 0), kvb)),
                  pl.BlockSpec((BLOCK, KV_DIM), lambda b, n: (b * nb + n, kvb))],
        out_specs=[pl.BlockSpec((BLOCK, N_HEADS * HEAD_DIM), lambda b, n: (b * nb + n, 0)),
                   pl.BlockSpec((BLOCK, N_HEADS), lambda b, n: (b * nb + n, 0))],
        out_shape=[SDS((n_rows, N_HEADS * HEAD_DIM), BF16), SDS((n_rows, N_HEADS), F32)],
        scratch_shapes=[pltpu.VMEM((3 * BLOCK, N_KV * HEAD_DIM), BF16), pltpu.VMEM((3 * BLOCK, 2 * N_KV * HEAD_DIM), BF16),
                        pltpu.VMEM((N_KV, GQA * BLOCK, HEAD_DIM), BF16), pltpu.VMEM((N_HEADS, BLOCK, 3 * BLOCK), F32)],
        compiler_params=_params(2),
    )(sinks, qkv, qkv, qkv, qkv)


def _attn_bwd(qkv, sinks, o, lse, do, n_ex, nb):
    n_rows = qkv.shape[0]
    kvb = N_HEADS * HEAD_DIM // KV_DIM
    scale = HEAD_DIM ** -0.5
    nq = lambda r: nb - 1 - r

    def body(sink_ref, q_ref, kvm_ref, kvp_ref, kvc_ref, o_ref, lse_ref, do_ref, dqkv_ref, dsink_ref,
             k_s, v_s, dkv_s, carry_s, meta_s, q_s, do_s, bias_s):
        b, r = pl.program_id(0), pl.program_id(1)
        n = nq(r)

        @pl.when((r == 0) | (n <= 1))
        def _():
            _attn_bias(n, bias_s)

        @pl.when((b == 0) & (r == 0))
        def _():
            dsink_ref[...] = jnp.zeros_like(dsink_ref)

        @pl.when(r == 0)
        def _():
            carry_s[...] = jnp.zeros_like(carry_s)
            meta_s[...] = jnp.zeros_like(meta_s)

        for part, ref in enumerate((kvm_ref, kvp_ref, kvc_ref)):
            k_s[part * BLOCK:(part + 1) * BLOCK, :] = ref[:, 0:N_KV * HEAD_DIM]
            v_s[part * BLOCK:(part + 1) * BLOCK, :] = ref[:, N_KV * HEAD_DIM:KV_DIM]
        nt = (((1,), (1,)), ((), ()))
        tn = (((0,), (0,)), ((), ()))
        for kv in range(N_KV):
            kcols = slice(kv * HEAD_DIM, (kv + 1) * HEAD_DIM)
            vcols = slice(N_KV * HEAD_DIM + kv * HEAD_DIM, N_KV * HEAD_DIM + (kv + 1) * HEAD_DIM)
            for g in range(GQA):
                cols = slice((kv * GQA + g) * HEAD_DIM, (kv * GQA + g + 1) * HEAD_DIM)
                q_s[kv, g * BLOCK:(g + 1) * BLOCK, :] = q_ref[:, cols] * scale
                do_s[kv, g * BLOCK:(g + 1) * BLOCK, :] = do_ref[:, cols]
            kh, vh = k_s[:, kcols], v_s[:, kcols]
            s4 = lax.dot_general(q_s[kv], kh, nt, preferred_element_type=F32)
            dp4 = lax.dot_general(do_s[kv], vh, nt, preferred_element_type=F32)
            ps, dss = [], []
            for g in range(GQA):
                h = kv * GQA + g
                cols = slice(h * HEAD_DIM, (h + 1) * HEAD_DIM)
                rows = slice(g * BLOCK, (g + 1) * BLOCK)
                s = s4[rows] + bias_s[h]
                lse_h = lse_ref[:, h:h + 1]
                p = jnp.exp(s - lse_h)
                delta = jnp.sum(do_ref[:, cols].astype(F32) * o_ref[:, cols].astype(F32), axis=-1, keepdims=True)
                dsink_ref[:, h:h + 1] += -jnp.exp(sink_ref[0, h] - lse_h) * delta
                ps.append(p.astype(BF16))
                dss.append((p * (dp4[rows] - delta)).astype(BF16))
            p4, ds4 = jnp.concatenate(ps, axis=0), jnp.concatenate(dss, axis=0)
            dq4 = jnp.dot(ds4, kh, preferred_element_type=F32) * scale
            for g in range(GQA):
                cols = slice((kv * GQA + g) * HEAD_DIM, (kv * GQA + g + 1) * HEAD_DIM)
                dqkv_ref[:, cols] = dq4[g * BLOCK:(g + 1) * BLOCK].astype(BF16)
            dkv_s[:, kcols] = lax.dot_general(ds4, q_s[kv], tn, preferred_element_type=F32)
            dkv_s[:, vcols] = lax.dot_general(p4, do_s[kv], tn, preferred_element_type=F32)

        meta_s[...] += dkv_s[0:BLOCK, :]
        cur = dkv_s[2 * BLOCK:3 * BLOCK, :] + carry_s[...]
        carry_s[...] = dkv_s[BLOCK:2 * BLOCK, :]

        @pl.when(n > 0)
        def _():
            dqkv_ref[:, N_HEADS * HEAD_DIM:QKV_DIM] = cur.astype(BF16)

        @pl.when(n == 0)
        def _():
            dqkv_ref[:, N_HEADS * HEAD_DIM:QKV_DIM] = (cur + meta_s[...]).astype(BF16)

    blk = lambda b, r: (b * nb + nq(r), 0)
    return pl.pallas_call(
        body, name="attn_bwd", grid=(n_ex, nb),
        in_specs=[pl.BlockSpec(memory_space=pltpu.SMEM),
                  pl.BlockSpec((BLOCK, N_HEADS * HEAD_DIM), blk),
                  pl.BlockSpec((BLOCK, KV_DIM), lambda b, r: (b * nb, kvb)),
                  pl.BlockSpec((BLOCK, KV_DIM), lambda b, r: (b * nb + jnp.maximum(nq(r) - 1, 0), kvb)),
                  pl.BlockSpec((BLOCK, KV_DIM), lambda b, r: (b * nb + nq(r), kvb)),
                  pl.BlockSpec((BLOCK, N_HEADS * HEAD_DIM), blk),
                  pl.BlockSpec((BLOCK, N_HEADS), blk),
                  pl.BlockSpec((BLOCK, N_HEADS * HEAD_DIM), blk)],
        out_specs=[pl.BlockSpec((BLOCK, QKV_DIM), blk),
                   pl.BlockSpec((BLOCK, N_HEADS), lambda b, r: (0, 0))],
        out_shape=[SDS((n_rows, QKV_DIM), BF16), SDS((BLOCK, N_HEADS), F32)],
        scratch_shapes=[pltpu.VMEM((3 * BLOCK, N_KV * HEAD_DIM), BF16), pltpu.VMEM((3 * BLOCK, N_KV * HEAD_DIM), BF16),
                        pltpu.VMEM((3 * BLOCK, KV_DIM), F32), pltpu.VMEM((BLOCK, KV_DIM), F32),
                        pltpu.VMEM((BLOCK, KV_DIM), F32), pltpu.VMEM((N_KV, GQA * BLOCK, HEAD_DIM), BF16),
                        pltpu.VMEM((N_KV, GQA * BLOCK, HEAD_DIM), BF16), pltpu.VMEM((N_HEADS, BLOCK, 3 * BLOCK), F32)],
        compiler_params=_params(2),
    )(sinks, qkv, qkv, qkv, qkv, o, lse, do)


SSM_TILES = (1408, 384)
XW = 256 * PAIRS_PER_CHUNK


def _cmul_add(xr, xi, mr, mi, sr, si):
    return xr + mr * sr - mi * si, xi + mr * si + mi * sr


def _to_segments(src_ref, dst, seg):
    for s in range(seg):
        dst[s * 8:(s + 1) * 8, :] = src_ref[pl.ds(s, 8, stride=seg), :]


def _from_segments(src, i, seg):
    return src[pl.ds(i, seg, stride=8), :]


def _scan_segments(buf, tab_ref, carry_s, seg, reverse):
    shifts = (7, 6, 4) if reverse else (1, 2, 4)
    row_id = lax.broadcasted_iota(jnp.int32, (8, 128), 0)
    a_tiles = [tab_ref[j, c] for j in range(PAIRS_PER_CHUNK) for c in (0, 1)]

    def local(si, prev):
        s = (seg - 1 - si) if reverse else si
        row = pl.multiple_of(s * 8, 8)
        out = []
        for j in range(PAIRS_PER_CHUNK):
            re, im = slice(256 * j, 256 * j + 128), slice(256 * j + 128, 256 * j + 256)
            xr, xi = _cmul_add(buf[pl.ds(row, 8), re], buf[pl.ds(row, 8), im],
                               a_tiles[2 * j], a_tiles[2 * j + 1], prev[2 * j], prev[2 * j + 1])
            buf[pl.ds(row, 8), re] = xr
            buf[pl.ds(row, 8), im] = xi
            out += [xr, xi]
        return tuple(out)

    zero = jnp.zeros((8, 128), F32)
    edge = lax.fori_loop(0, seg, local, (zero,) * (2 * PAIRS_PER_CHUNK))

    entering = []
    for j in range(PAIRS_PER_CHUNK):
        er, ei = edge[2 * j], edge[2 * j + 1]
        if reverse:
            sr = jnp.where(row_id == 7, carry_s[2 * j], pltpu.roll(er, 7, 0))
            si_ = jnp.where(row_id == 7, carry_s[2 * j + 1], pltpu.roll(ei, 7, 0))
        else:
            sr = jnp.where(row_id == 0, carry_s[2 * j], pltpu.roll(er, 1, 0))
            si_ = jnp.where(row_id == 0, carry_s[2 * j + 1], pltpu.roll(ei, 1, 0))
        for lvl, sh in enumerate(shifts):
            sr, si_ = _cmul_add(sr, si_, tab_ref[j, 2 + 2 * lvl], tab_ref[j, 3 + 2 * lvl],
                                pltpu.roll(sr, sh, 0), pltpu.roll(si_, sh, 0))
        entering += [sr, si_]
        tr, ti = _cmul_add(er, ei, tab_ref[j, 2], tab_ref[j, 3], sr, si_)
        out_row = slice(0, 1) if reverse else slice(7, 8)
        carry_s[2 * j] = jnp.broadcast_to(tr[out_row], (8, 128))
        carry_s[2 * j + 1] = jnp.broadcast_to(ti[out_row], (8, 128))

    def fix(si, carried):
        s = (seg - 1 - si) if reverse else si
        row = pl.multiple_of(s * 8, 8)
        out = []
        for j in range(PAIRS_PER_CHUNK):
            re, im = slice(256 * j, 256 * j + 128), slice(256 * j + 128, 256 * j + 256)
            ar, ai, fr, fi = a_tiles[2 * j], a_tiles[2 * j + 1], carried[2 * j], carried[2 * j + 1]
            fr, fi = ar * fr - ai * fi, ar * fi + ai * fr
            buf[pl.ds(row, 8), re] += fr
            buf[pl.ds(row, 8), im] += fi
            out += [fr, fi]
        return tuple(out)

    lax.fori_loop(0, seg, fix, tuple(entering))


def _ssm_fwd(u, b_pad, c_pad, tab, d_skip, n_ex, lp):
    n_rows = u.shape[0]
    TM = _row_tile(lp, SSM_TILES)
    SEG = TM // 8
    n_t = lp // TM
    n_chunk = D_MODEL // 128

    def body(u_ref, bp_ref, cp_ref, tab_ref, d_ref, yg_ref, y_ref, xs_ref, buf, carry_s, us, ys):
        @pl.when(pl.program_id(2) == 0)
        def _():
            carry_s[...] = jnp.zeros_like(carry_s)

        _to_segments(u_ref, us, SEG)
        ub = us[...]
        u16 = ub.astype(BF16)
        buf[...] = jnp.dot(u16, bp_ref[...], preferred_element_type=F32)
        _scan_segments(buf, tab_ref, carry_s, SEG, reverse=False)
        xb = buf[...].astype(BF16)
        xs_ref[...] = xb
        ys[...] = d_ref[...] * ub + jnp.dot(xb, cp_ref[...], preferred_element_type=F32)
        for i in range(8):
            yi = _from_segments(ys, i, SEG)
            y_ref[i * SEG:(i + 1) * SEG, :] = yi
            yg_ref[i * SEG:(i + 1) * SEG, :] = _gelu(yi).astype(BF16)

    rows = lambda b, q, t: (b * n_t + t, q)
    return pl.pallas_call(
        body, name="ssm_fwd", grid=(n_ex, n_chunk, n_t),
        in_specs=[pl.BlockSpec((TM, 128), rows),
                  pl.BlockSpec((None, 128, XW), lambda b, q, t: (q, 0, 0)),
                  pl.BlockSpec((None, XW, 128), lambda b, q, t: (q, 0, 0)),
                  pl.BlockSpec((PAIRS_PER_CHUNK, 8, 8, 128), lambda b, q, t: (q, 0, 0, 0)),
                  pl.BlockSpec((1, 128), lambda b, q, t: (0, q))],
        out_specs=[pl.BlockSpec((TM, 128), rows), pl.BlockSpec((TM, 128), rows),
                   pl.BlockSpec((None, TM, XW), lambda b, q, t: (q, b * n_t + t, 0))],
        out_shape=[SDS((n_rows, D_MODEL), BF16), SDS((n_rows, D_MODEL), F32), SDS((n_chunk, n_rows, XW), BF16)],
        scratch_shapes=[pltpu.VMEM((TM, XW), F32), pltpu.VMEM((2 * PAIRS_PER_CHUNK, 8, 128), F32),
                        pltpu.VMEM((TM, 128), F32), pltpu.VMEM((TM, 128), F32)],
        compiler_params=_params(3),
    )(u, b_pad, c_pad, tab, d_skip)


def _ssm_bwd(dyg, y, u, xs, ct_pad, bt_pad, tab_rev, d_skip, n_ex, lp):
    n_rows = u.shape[0]
    TM = _row_tile(lp, SSM_TILES)
    SEG = TM // 8
    n_t = lp // TM
    n_chunk = D_MODEL // 128
    tile = lambda q, b, t: (b * n_t + (n_t - 1 - t), q)

    def body(dyg_ref, y_ref, u_ref, xs_ref, xp_ref, ct_ref, bt_ref, tab_ref, d_ref,
             du_ref, db_ref, dc_ref, da_ref, dd_ref, buf, xf, carry_s, us, dys, dyp):
        b, t = pl.program_id(1), pl.program_id(2)

        @pl.when((b == 0) & (t == 0))
        def _():
            db_ref[...] = jnp.zeros_like(db_ref)
            dc_ref[...] = jnp.zeros_like(dc_ref)
            da_ref[...] = jnp.zeros_like(da_ref)
            dd_ref[...] = jnp.zeros_like(dd_ref)

        @pl.when(t == 0)
        def _():
            carry_s[...] = jnp.zeros_like(carry_s)

        dys[...] = dyg_ref[...].astype(F32) * _gelu_grad(y_ref[...])
        dd_ref[...] += _fold8(dys[...] * u_ref[...])
        _to_segments(dys, dyp, SEG)
        dy = dyp[...]
        _to_segments(u_ref, us, SEG)
        dy16 = dy.astype(BF16)
        first_tile = t == n_t - 1
        tn = (((0,), (0,)), ((), ()))
        buf[...] = jnp.dot(dy16, ct_ref[...], preferred_element_type=F32)
        dc_ref[...] += lax.dot_general(dy16, xs_ref[...], tn, preferred_element_type=F32)
        xf[16:16 + TM, :] = xs_ref[...].astype(F32)
        xf[0:16, :] = jnp.where(first_tile, 0.0, xp_ref[...].astype(F32))
        _scan_segments(buf, tab_ref, carry_s, SEG, reverse=True)
        g16 = buf[...].astype(BF16)
        dys[...] = d_ref[...] * dy + jnp.dot(g16, bt_ref[...], preferred_element_type=F32)
        db_ref[...] += lax.dot_general(us[...].astype(BF16), g16, tn, preferred_element_type=F32)
        row_id = lax.broadcasted_iota(jnp.int32, (8, 128), 0)
        for j in range(PAIRS_PER_CHUNK):
            re, im = slice(256 * j, 256 * j + 128), slice(256 * j + 128, 256 * j + 256)
            first = [jnp.where(row_id == 0, jnp.broadcast_to(xf[15:16, c], (8, 128)),
                               pltpu.roll(xf[8 + TM:16 + TM, c], 1, 0)) for c in (re, im)]
            for rows, pr, pi in ((slice(0, 8), first[0], first[1]),
                                 (slice(8, TM), xf[16:8 + TM, re], xf[16:8 + TM, im])):
                gr, gi = buf[rows, re], buf[rows, im]
                da_ref[j, 0] += _fold8(gr * pr + gi * pi)
                da_ref[j, 1] += _fold8(gi * pr - gr * pi)
        for i in range(8):
            du_ref[i * SEG:(i + 1) * SEG, :] = _from_segments(dys, i, SEG)

    prev16 = lambda q, b, t: (q, jnp.maximum((b * n_t + (n_t - 1 - t)) * (TM // 16) - 1, 0), 0)
    return pl.pallas_call(
        body, name="ssm_bwd", grid=(n_chunk, n_ex, n_t),
        in_specs=[pl.BlockSpec((TM, 128), tile), pl.BlockSpec((TM, 128), tile), pl.BlockSpec((TM, 128), tile),
                  pl.BlockSpec((None, TM, XW), lambda q, b, t: (q, b * n_t + (n_t - 1 - t), 0)),
                  pl.BlockSpec((None, 16, XW), prev16),
                  pl.BlockSpec((None, 128, XW), lambda q, b, t: (q, 0, 0)),
                  pl.BlockSpec((None, XW, 128), lambda q, b, t: (q, 0, 0)),
                  pl.BlockSpec((PAIRS_PER_CHUNK, 8, 8, 128), lambda q, b, t: (q, 0, 0, 0)),
                  pl.BlockSpec((1, 128), lambda q, b, t: (0, q))],
        out_specs=[pl.BlockSpec((TM, 128), tile),
                   pl.BlockSpec((None, 128, XW), lambda q, b, t: (q, 0, 0)),
                   pl.BlockSpec((None, 128, XW), lambda q, b, t: (q, 0, 0)),
                   pl.BlockSpec((PAIRS_PER_CHUNK, 2, 8, 128), lambda q, b, t: (q, 0, 0, 0)),
                   pl.BlockSpec((8, 128), lambda q, b, t: (0, q))],
        out_shape=[SDS((n_rows, D_MODEL), F32), SDS((n_chunk, 128, XW), F32), SDS((n_chunk, 128, XW), F32),
                   SDS((N_PAIR, 2, 8, 128), F32), SDS((8, D_MODEL), F32)],
        scratch_shapes=[pltpu.VMEM((TM, XW), F32), pltpu.VMEM((TM + 16, XW), F32),
                        pltpu.VMEM((2 * PAIRS_PER_CHUNK, 8, 128), F32), pltpu.VMEM((TM, 128), F32),
                        pltpu.VMEM((TM, 128), F32), pltpu.VMEM((TM, 128), F32)],
        compiler_params=_params(3),
    )(dyg, y, u, xs, xs, ct_pad, bt_pad, tab_rev, d_skip)


def _rms_bwd_call(dhn, h, wn, dres, name):
    n_rows = h.shape[0]

    def body(dhn_ref, h_ref, wn_ref, dres_ref, o_ref, dw_ref):
        @pl.when(pl.program_id(0) == 0)
        def _():
            dw_ref[...] = jnp.zeros_like(dw_ref)

        dh, dw_rows = _rms_bwd(dhn_ref[...], h_ref[...], wn_ref[...])
        o_ref[...] = dres_ref[...] + dh
        dw_ref[...] += _fold8(dw_rows)

    row = lambda i: (i, 0)
    return pl.pallas_call(
        body, name=name, grid=(n_rows // TM,),
        in_specs=[pl.BlockSpec((TM, D_MODEL), row), pl.BlockSpec((TM, D_MODEL), row),
                  pl.BlockSpec((1, D_MODEL), lambda i: (0, 0)), pl.BlockSpec((TM, D_MODEL), row)],
        out_specs=[pl.BlockSpec((TM, D_MODEL), row), pl.BlockSpec((8, D_MODEL), lambda i: (0, 0))],
        out_shape=[SDS((n_rows, D_MODEL), F32), SDS((8, D_MODEL), F32)], compiler_params=_params(1),
    )(dhn, h, wn, dres)


def _glu_bwd(dh, z, w4):
    n_rows = dh.shape[0]
    tm = _row_tile(n_rows, MM_TILES)
    n_sh, _, k, n = w4.shape

    def body(dh_ref, z_ref, w_ref, dz_ref, dyg_ref):
        sg = jax.nn.sigmoid(z_ref[:, D_MODEL:2 * D_MODEL].astype(F32))
        d = dh_ref[...]
        dz_ref[:, 0:D_MODEL] = (d * sg).astype(BF16)
        dz_ref[:, D_MODEL:2 * D_MODEL] = (d * z_ref[:, 0:D_MODEL].astype(F32) * sg * (1.0 - sg)).astype(BF16)
        acc = None
        for s in range(n_sh):
            part = lax.dot_general(dz_ref[:, s * n:(s + 1) * n], w_ref[s], (((1,), (1,)), ((), ())),
                                   preferred_element_type=F32)
            acc = part if acc is None else acc + part
        dyg_ref[...] = acc.astype(BF16)

    row = lambda i: (i, 0)
    return pl.pallas_call(
        body, name="glu_bwd", grid=(n_rows // tm,),
        in_specs=[pl.BlockSpec((tm, D_MODEL), row), pl.BlockSpec((tm, 2 * D_MODEL), row), _w4_spec(w4)],
        out_specs=[pl.BlockSpec((tm, 2 * D_MODEL), row), pl.BlockSpec((tm, k), row)],
        out_shape=[SDS((n_rows, 2 * D_MODEL), BF16), SDS((n_rows, k), BF16)], compiler_params=_params(1),
    )(dh, z, w4)


def _loss_head(h, wn, target, n_ex, nb):
    n_rows = h.shape[0]
    per_tile = TM // BLOCK
    n_tiles = nb // per_tile

    def body(h_ref, wn_ref, *rest):
        t_refs, (dh_ref, loss_ref, dw_ref) = rest[:per_tile], rest[per_tile:]
        b, j = pl.program_id(0), pl.program_id(1)

        @pl.when((b == 0) & (j == 0))
        def _():
            loss_ref[...] = jnp.zeros_like(loss_ref)
            dw_ref[...] = jnp.zeros_like(dw_ref)

        def block(k):
            rows = slice(k * BLOCK, (k + 1) * BLOCK)
            hh = h_ref[rows, :]
            diff = _rms(hh, wn_ref[...]) - t_refs[k][...]
            loss_ref[...] += 0.5 * jnp.sum(diff * diff) * (1.0 / D_MODEL)
            dh, dw_rows = _rms_bwd(diff * (1.0 / D_MODEL), hh, wn_ref[...])
            dh_ref[rows, :] = dh
            dw_ref[...] += _fold8(dw_rows)

        @pl.when(j == 0)
        def _():
            dh_ref[0:BLOCK, :] = jnp.zeros((BLOCK, D_MODEL), F32)

        pl.when(j > 0)(lambda: block(0))
        for k in range(1, per_tile):
            block(k)

    def t_spec(k):
        return pl.BlockSpec((BLOCK, D_MODEL), lambda b, j: (b * (nb - 1) + jnp.maximum(per_tile * j + k - 1, 0), 0))

    tile = pl.BlockSpec((TM, D_MODEL), lambda b, j: (b * n_tiles + j, 0))
    return pl.pallas_call(
        body, name="loss_head", grid=(n_ex, n_tiles),
        in_specs=[tile, pl.BlockSpec((1, D_MODEL), lambda b, j: (0, 0))] + [t_spec(k) for k in range(per_tile)],
        out_specs=[tile, pl.BlockSpec((8, 128), lambda b, j: (0, 0)), pl.BlockSpec((8, D_MODEL), lambda b, j: (0, 0))],
        out_shape=[SDS((n_rows, D_MODEL), F32), SDS((8, 128), F32), SDS((8, D_MODEL), F32)],
        compiler_params=_params(2),
    )(h, wn, *([target] * per_tile))


def _adamw(pieces, w, m, v, name):
    n_layers = len(pieces)
    rows, cols = pieces[0].shape[1:]
    rb = rows
    for cand in (256, 136, 128, 64, 32, 16, 8):
        if rows % cand == 0 and rows > cand:
            rb = cand
            break
    n_blk = rows // rb
    c1 = 1.0 / (1.0 - ADAM_B1 ** ADAM_STEP)
    c2 = 1.0 / (1.0 - ADAM_B2 ** ADAM_STEP)

    def body(*refs):
        p_refs = refs[:n_layers]
        w_ref, m_ref, v_ref, g_out, d_out, m_out, v_out = refs[n_layers:]
        layer = pl.program_id(0)
        g = None
        for l, p_ref in enumerate(p_refs):
            gl = p_ref[0].astype(F32)
            for k in range(1, N_DEV):
                gl = gl + p_ref[k].astype(F32)
            g = gl if g is None else jnp.where(layer == l, gl, g)
        m_new = ADAM_B1 * m_ref[...] + (1.0 - ADAM_B1) * g
        v_new = ADAM_B2 * v_ref[...] + (1.0 - ADAM_B2) * (g * g)
        g_out[...] = g
        m_out[...] = m_new
        v_out[...] = v_new
        d_out[...] = -ADAM_LR * ((m_new * c1) / (jnp.sqrt(v_new * c2) + ADAM_EPS) + ADAM_WD * w_ref[...])

    def piece_spec(l):
        return pl.BlockSpec((N_DEV, rb, cols), lambda ly, i: (0, jnp.where(ly == l, i, 0), 0))

    blk = pl.BlockSpec((rb, cols), lambda ly, i: (ly * n_blk + i, 0))
    return pl.pallas_call(
        body, name=name, grid=(n_layers, n_blk),
        in_specs=[piece_spec(l) for l in range(n_layers)] + [blk, blk, blk],
        out_specs=[blk, blk, blk, blk],
        out_shape=[SDS((n_layers * rows, cols), F32)] * 4, compiler_params=_params(2),
    )(*pieces, w, m, v)


_HBM = pl.BlockSpec(memory_space=pltpu.HBM)
_SEM = pl.BlockSpec(memory_space=pltpu.SEMAPHORE)
_EFFECT = pltpu.SideEffectType.DATAFLOW_SIDE_EFFECTING
N_GATHER_PEERS = N_CHIPS - 1
N_EXCHANGE_PEERS = N_DEV - 1


def _gather_copies(srcs, lands, send_sems, recv_sems):
    x, y, c = lax.axis_index("x"), lax.axis_index("y"), lax.axis_index("c")
    mine = 2 * x + y
    chips = [(1 - x, y), (x, 1 - y), (1 - x, 1 - y)]
    out, inc = [], []
    for a in range(len(srcs)):
        for k, (px, py) in enumerate(chips):
            j = a * N_GATHER_PEERS + k
            sems = dict(send_sem=send_sems.at[j], recv_sem=recv_sems.at[j], device_id=(px, py, c),
                        device_id_type=pl.DeviceIdType.MESH)
            out.append(pltpu.make_async_remote_copy(src_ref=srcs[a], dst_ref=lands[a].at[mine], **sems))
            inc.append(pltpu.make_async_remote_copy(src_ref=srcs[a], dst_ref=lands[a].at[2 * px + py], **sems))
    return out, inc


def _exchange_copies(n_scatter):
    def copies(srcs, lands, send_sems, recv_sems):
        x, y, c = lax.axis_index("x"), lax.axis_index("y"), lax.axis_index("c")
        me = 4 * x + 2 * y + c
        peers = [(x ^ (k >> 2), y ^ ((k >> 1) & 1), c ^ (k & 1)) for k in range(1, N_DEV)]
        out, inc = [], []
        for a in range(len(srcs)):
            for k, (px, py, pc) in enumerate(peers):
                j = a * N_EXCHANGE_PEERS + k
                sems = dict(send_sem=send_sems.at[j], recv_sem=recv_sems.at[j], device_id=(px, py, pc),
                            device_id_type=pl.DeviceIdType.MESH)
                theirs = srcs[a].at[2 * px + py] if a < n_scatter else srcs[a]
                mine = srcs[a].at[2 * x + y] if a < n_scatter else srcs[a]
                out.append(pltpu.make_async_remote_copy(src_ref=theirs, dst_ref=lands[a].at[me], **sems))
                inc.append(pltpu.make_async_remote_copy(src_ref=mine, dst_ref=lands[a].at[4 * px + 2 * py + pc], **sems))
        return out, inc

    return copies


def _split_start(groups, copies_fn, n_peers, name):
    sizes = [len(srcs) for srcs, _ in groups]
    flat = [a for srcs, lands in groups for a in list(srcs) + list(lands)]
    n_flat, n_grp = len(flat), len(groups)

    def body(*refs):
        sems = refs[2 * n_flat:2 * n_flat + 2 * n_grp]
        token = refs[-1]
        at = 0
        for gi, n in enumerate(sizes):
            out, _ = copies_fn(refs[at:at + n], refs[at + n:at + 2 * n], sems[2 * gi], sems[2 * gi + 1])
            for cp in out:
                cp.start()
            at += 2 * n
        token[...] = jnp.zeros_like(token)

    sem_shapes = []
    for n in sizes:
        sem_shapes += [pltpu.SemaphoreType.DMA((n * n_peers,)), pltpu.SemaphoreType.DMA((n * n_peers,))]
    res = pl.pallas_call(
        body, name=name,
        out_shape=(*[pltpu.HBM(a.shape, a.dtype) for a in flat], *sem_shapes, SDS((8, 128), F32)),
        in_specs=[_HBM] * n_flat,
        out_specs=(*[_HBM] * n_flat, *[_SEM] * (2 * n_grp), pl.BlockSpec(memory_space=pltpu.VMEM)),
        input_output_aliases={i: i for i in range(n_flat)},
        compiler_params=pltpu.CompilerParams(has_side_effects=_EFFECT),
    )(*[pltpu.with_memory_space_constraint(a, pltpu.HBM) for a in flat])
    handles, at = [], 0
    for gi, n in enumerate(sizes):
        handles.append((res[n_flat + 2 * gi], res[n_flat + 2 * gi + 1], list(res[at:at + n]), list(res[at + n:at + 2 * n])))
        at += 2 * n
    return handles, res[-1]


def _split_wait(handle, after, copies_fn, name):
    send_sems, recv_sems, srcs, lands = handle
    n = len(srcs)
    after = list(after) if isinstance(after, (list, tuple)) else [after]

    def body(*refs):
        out, inc = copies_fn(refs[:n], refs[n:2 * n], refs[2 * n], refs[2 * n + 1])
        for cp in out:
            cp.wait_send()
        for cp in inc:
            cp.wait_recv()

    flat = list(srcs) + list(lands)
    res = pl.pallas_call(
        body, name=name,
        out_shape=tuple(pltpu.HBM(a.shape, a.dtype) for a in flat),
        in_specs=[_HBM] * (2 * n) + [_SEM, _SEM] + [pl.BlockSpec(memory_space=pl.ANY)] * len(after),
        out_specs=tuple([_HBM] * (2 * n)),
        input_output_aliases={i: i for i in range(2 * n)},
        compiler_params=pltpu.CompilerParams(has_side_effects=_EFFECT),
    )(*flat, send_sems, recv_sems, *after)
    return list(res[n:])


def _landing(own, slot, n_slots):
    return lax.dynamic_update_index_in_dim(lax.empty((n_slots,) + own.shape, own.dtype), own, slot, 0)


def _ssm_discretize(lam_re, lam_im, log_dt, b_re, b_im):
    lr = jnp.minimum(lam_re, LAMBDA_RE_MAX)
    li = lam_im
    dt = jnp.exp(log_dt)[:, None]
    mag = jnp.exp(lr * dt)
    ar, ai = mag * jnp.cos(li * dt), mag * jnp.sin(li * dt)
    den = lr * lr + li * li
    nr, ni = ar - 1.0, ai
    gr, gi = (nr * lr + ni * li) / den, (ni * lr - nr * li) / den
    bbr = gr[:, :, None] * b_re - gi[:, :, None] * b_im
    bbi = gr[:, :, None] * b_im + gi[:, :, None] * b_re
    return ar, ai, bbr, bbi


def _pair_lanes(t):
    return t.reshape(N_PAIR, 2 * SSM_STATE)


def _chan_state_blocks(t_gcp):
    t = t_gcp.reshape(N_PAIR, 2, SSM_GROUP, SSM_STATE)
    eye2 = jnp.eye(2, dtype=t.dtype)
    blk = jnp.einsum("rgcp,gh->rgchp", t, eye2).reshape(N_PAIR, 2 * SSM_GROUP, 2 * SSM_STATE)
    place = jax.nn.one_hot(jnp.arange(N_PAIR) % PAIRS_PER_CHUNK, PAIRS_PER_CHUNK, dtype=t.dtype)
    return jnp.einsum("rcl,rj->rjcl", blk, place).reshape(N_PAIR, 128, 2 * SSM_STATE)


def _chan_state_unblock(t):
    t = t.reshape(N_PAIR, PAIRS_PER_CHUNK, 2, SSM_GROUP, 2, SSM_STATE)
    place = jax.nn.one_hot(jnp.arange(N_PAIR) % PAIRS_PER_CHUNK, PAIRS_PER_CHUNK, dtype=t.dtype)
    t = jnp.einsum("rjgchp,rj->rgchp", t, place)
    t = jnp.einsum("rgchp,gh->rgcp", t, jnp.eye(2, dtype=t.dtype))
    return t.reshape(SSM_NG, SSM_GROUP, SSM_STATE)


def _scan_tables(zr, zi, reverse, seg):
    zr, zi = _pair_lanes(zr), _pair_lanes(-zi if reverse else zi)
    a = (jnp.exp(zr) * jnp.cos(zi), jnp.exp(zr) * jnp.sin(zi))
    cmul = lambda p, q: (p[0] * q[0] - p[1] * q[1], p[0] * q[1] + p[1] * q[0])
    big, square, bits = None, a, seg
    while bits:
        if bits & 1:
            big = square if big is None else cmul(big, square)
        square, bits = cmul(square, square), bits >> 1
    powers = [a, big]
    for _ in range(2):
        powers.append(cmul(powers[-1], powers[-1]))
    rows = jnp.arange(8)[None, :, None]
    tiles = [jnp.broadcast_to(part[:, None, :], (N_PAIR, 8, 128)) for part in powers[0]]
    for lvl, step in enumerate((1, 2, 4)):
        keep = (rows <= 7 - step) if reverse else (rows >= step)
        for part in powers[1 + lvl]:
            tiles.append(jnp.where(keep, part[:, None, :], 0.0))
    return jnp.stack(tiles, axis=1)


def _pairs_to_chunks(t):
    n_chunk = N_PAIR // PAIRS_PER_CHUNK
    return jnp.swapaxes(t.reshape(n_chunk, PAIRS_PER_CHUNK, 128, 256), 1, 2).reshape(n_chunk, 128, XW)


def _chunks_to_pairs(t):
    n_chunk = N_PAIR // PAIRS_PER_CHUNK
    return jnp.swapaxes(t.reshape(n_chunk, 128, PAIRS_PER_CHUNK, 256), 1, 2).reshape(N_PAIR, 128, 256)


def _ssm_operands(w, lp):
    seg = _row_tile(lp, SSM_TILES) // 8
    ar, ai, bbr, bbi = _ssm_discretize(w["ssm_lambda_re"], w["ssm_lambda_im"], w["ssm_log_dt"], w["ssm_b_re"], w["ssm_b_im"])
    b_blk = jnp.concatenate([_chan_state_blocks(jnp.swapaxes(bbr, 1, 2)), _chan_state_blocks(jnp.swapaxes(bbi, 1, 2))], axis=2)
    c_blk = jnp.concatenate([_chan_state_blocks(w["ssm_c_re"]), -_chan_state_blocks(w["ssm_c_im"])], axis=2)
    dt = jnp.exp(w["ssm_log_dt"])[:, None]
    zr, zi = jnp.minimum(w["ssm_lambda_re"], LAMBDA_RE_MAX) * dt, w["ssm_lambda_im"] * dt
    b_cat, c_cat = _pairs_to_chunks(b_blk).astype(BF16), _pairs_to_chunks(c_blk).astype(BF16)
    return (b_cat, jnp.swapaxes(b_cat, 1, 2), c_cat, jnp.swapaxes(c_cat, 1, 2),
            _scan_tables(zr, zi, False, seg), _scan_tables(zr, zi, True, seg))


def _local_step(x, target, w, late_weights, on_grads):
    n_ex, seq, _ = x.shape
    lp = seq + BLOCK
    nb = lp // BLOCK
    n_rows = n_ex * lp
    g = {}

    head = jnp.concatenate([jnp.zeros((PAD, D_MODEL), F32), w["meta_tokens"]], axis=0)
    h0 = jnp.concatenate([jnp.broadcast_to(head[None], (n_ex, BLOCK, D_MODEL)), x], axis=1).reshape(n_rows, D_MODEL)

    qkv, hn_a = _rms_mm_cols(h0, w["attn_norm_w"], w["attn_w_qkv"], "qkv_fwd")
    att, lse = _attn_fwd(qkv, w["attn_sinks"], n_ex, nb)
    h1 = _mm_acc(att, w["attn_w_o"], False, "attn_out_fwd", res=h0)
    w = {**w, **late_weights(0, att)}
    h2, a0, hn_m0, u = _mlp_fwd(h1, w["mlp_norm_w"][0:1], w["mlp_w_up"][0], w["mlp_w_down"][0], "mlp0_fwd",
                                next_norm=w["ssm_norm_w"])
    late = late_weights(1, h2)
    w["ssm_w_glu"] = late["ssm_w_glu"]
    w["mlp_w_up"], w["mlp_w_down"] = w["mlp_w_up"] + late["mlp_w_up"], w["mlp_w_down"] + late["mlp_w_down"]

    ops = w["ssm_operands"] if "ssm_operands" in w else _ssm_operands(w, lp)
    b_pad, bt_pad, ct_pad, c_pad, tab_fwd, tab_rev = ops
    yg, y, xs = _ssm_fwd(u, b_pad, c_pad, tab_fwd, w["ssm_d"], n_ex, lp)
    z = _mm_cols(yg, w["ssm_w_glu"], False, "glu_mm_fwd")
    h4, a1, hn_m1, h3 = _mlp_fwd(h2, w["mlp_norm_w"][1:2], w["mlp_w_up"][1], w["mlp_w_down"][1], "mlp1_fwd", glu_z=z)

    dh4, loss_tile, dnorm_f = _loss_head(h4, w["final_norm_w"], target.reshape(n_ex * seq, D_MODEL), n_ex, nb)

    def mlp_bwd(dh_out, h_in, a, hn, layer, tag, norm_w):
        dhn, dw_up, dw_down = None, None, None
        for s in range(N_CHIPS):
            final = s == N_CHIPS - 1
            res = _mlp_bwd_shard(s, dh_out, a, hn, dhn, h_in if final else None, norm_w,
                                 w["mlp_w_up"][layer], w["mlp_w_down"][layer], dw_up, dw_down, f"{tag}_bwd{s}")
            dhn, dw_up, dw_down = res[:3]
        return dhn, res[3], dw_up, dw_down

    dh3, dnorm_m1, dwu1, dwd1 = mlp_bwd(dh4, h3, a1, hn_m1, 1, "mlp1", w["mlp_norm_w"][1:2])
    tok = on_grads("mlp1", {"mlp_w_up": dwu1, "mlp_w_down": dwd1})
    dz, dyg = _glu_bwd(dh3, z, w["ssm_w_glu"])
    g["ssm_w_glu"] = _mm_tn(yg, dz, N_CHIPS, False, "glu_mm_dw")
    du, db_blk, dc_blk, da_t, dd_t = _ssm_bwd(dyg, y, u, xs, ct_pad, bt_pad, tab_rev, w["ssm_d"] + tok, n_ex, lp)
    dh2, dnorm_s = _rms_bwd_call(du, h2, w["ssm_norm_w"], dh3, "ssm_norm_bwd")
    db_blk, dc_blk = _chunks_to_pairs(db_blk), _chunks_to_pairs(dc_blk)
    g["ssm_c_re"] = _chan_state_unblock(dc_blk[:, :, 0:128])
    g["ssm_c_im"] = -_chan_state_unblock(dc_blk[:, :, 128:256])
    g_bbr = jnp.swapaxes(_chan_state_unblock(db_blk[:, :, 0:128]), 1, 2)
    g_bbi = jnp.swapaxes(_chan_state_unblock(db_blk[:, :, 128:256]), 1, 2)
    g_a = jnp.sum(da_t, axis=2).reshape(N_PAIR, 2, 2, SSM_STATE)
    g_ar, g_ai = g_a[:, 0].reshape(SSM_NG, SSM_STATE), g_a[:, 1].reshape(SSM_NG, SSM_STATE)
    _, vjp = jax.vjp(_ssm_discretize, w["ssm_lambda_re"], w["ssm_lambda_im"], w["ssm_log_dt"], w["ssm_b_re"], w["ssm_b_im"])
    g["ssm_lambda_re"], g["ssm_lambda_im"], g["ssm_log_dt"], g["ssm_b_re"], g["ssm_b_im"] = vjp((g_ar, g_ai, g_bbr, g_bbi))
    tok = on_grads("ssm", g)
    g = {}
    dh1, dnorm_m0, dwu0, dwd0 = mlp_bwd(dh2, h1, a0, hn_m0, 0, "mlp0", w["mlp_norm_w"][0:1] + tok)
    datt = _mm_cols(dh1, w["attn_w_o"], True, "attn_out_dx")
    dw_o = _mm_tn(att, dh1, N_CHIPS, True, "attn_out_dw")
    tok = on_grads("mlp0", {"mlp_w_up": dwu0, "mlp_w_down": dwd0, "attn_w_o": dw_o})
    dqkv, dsink_rows = _attn_bwd(qkv, w["attn_sinks"] + tok, att, lse, datt, n_ex, nb)
    tok = on_grads("qkv", {"attn_w_qkv": _mm_tn(hn_a, dqkv, N_CHIPS, False, "qkv_dw")})
    dh0, dnorm_a = _mm_acc(dqkv, w["attn_w_qkv"], True, "qkv_dx", rms_bwd=(h0, w["attn_norm_w"] + tok, dh1))

    dh0 = dh0.reshape(n_ex, lp, D_MODEL)
    on_grads("rest", {
        "mlp_norm_w": jnp.stack([jnp.sum(dnorm_m0, axis=0), jnp.sum(dnorm_m1, axis=0)]),
        "final_norm_w": jnp.sum(dnorm_f, axis=0),
        "attn_norm_w": jnp.sum(dnorm_a, axis=0)[None],
        "ssm_norm_w": jnp.sum(dnorm_s, axis=0)[None],
        "attn_sinks": jnp.sum(dsink_rows, axis=0)[None],
        "ssm_d": jnp.sum(dd_t, axis=0)[None],
        "meta_tokens": jnp.sum(dh0[:, PAD:BLOCK], axis=0),
        "loss": loss_tile[0, 0:1]})
    return loss_tile, dh0[:, BLOCK:]


_SHARDED_SMALL = ("meta_tokens", "ssm_norm_w", "ssm_d")
_REP_SSM = ("ssm_lambda_re", "ssm_lambda_im", "ssm_log_dt", "ssm_b_re", "ssm_b_im", "ssm_c_re", "ssm_c_im")
_REP_MISC = ("attn_norm_w", "attn_sinks", "mlp_norm_w", "final_norm_w")
_BIG = ("attn_w_qkv", "attn_w_o", "ssm_w_glu", "mlp_w_up", "mlp_w_down")


def _pack(parts, cols):
    flat = jnp.concatenate([p.reshape(-1) for p in parts])
    rows = -(-flat.shape[0] // (8 * cols)) * 8
    return jnp.pad(flat, (0, rows * cols - flat.shape[0])).reshape(rows, cols)


def _unpack(packed, like):
    flat = packed.reshape(-1)
    out, at = [], 0
    for p in like:
        out.append(flat[at:at + p.size].reshape(p.shape))
        at += p.size
    return out


def kernel(x, meta_tokens, attn_norm_w, attn_w_qkv, attn_sinks, attn_w_o, ssm_norm_w, ssm_lambda_re, ssm_lambda_im, ssm_log_dt, ssm_b_re, ssm_b_im, ssm_c_re, ssm_c_im, ssm_d, ssm_w_glu, mlp_norm_w, mlp_w_up, mlp_w_down, final_norm_w, loss_target, m_meta_tokens, m_attn_norm_w, m_attn_w_qkv, m_attn_sinks, m_attn_w_o, m_ssm_norm_w, m_ssm_lambda_re, m_ssm_lambda_im, m_ssm_log_dt, m_ssm_b_re, m_ssm_b_im, m_ssm_c_re, m_ssm_c_im, m_ssm_d, m_ssm_w_glu, m_mlp_norm_w, m_mlp_w_up, m_mlp_w_down, m_final_norm_w, v_meta_tokens, v_attn_norm_w, v_attn_w_qkv, v_attn_sinks, v_attn_w_o, v_ssm_norm_w, v_ssm_lambda_re, v_ssm_lambda_im, v_ssm_log_dt, v_ssm_b_re, v_ssm_b_im, v_ssm_c_re, v_ssm_c_im, v_ssm_d, v_ssm_w_glu, v_mlp_norm_w, v_mlp_w_up, v_mlp_w_down, v_final_norm_w):
    names = ("meta_tokens", "attn_norm_w", "attn_w_qkv", "attn_sinks", "attn_w_o", "ssm_norm_w", "ssm_lambda_re",
             "ssm_lambda_im", "ssm_log_dt", "ssm_b_re", "ssm_b_im", "ssm_c_re", "ssm_c_im", "ssm_d", "ssm_w_glu",
             "mlp_norm_w", "mlp_w_up", "mlp_w_down", "final_norm_w")
    wts = dict(zip(names, (meta_tokens, attn_norm_w, attn_w_qkv, attn_sinks, attn_w_o, ssm_norm_w, ssm_lambda_re,
                           ssm_lambda_im, ssm_log_dt, ssm_b_re, ssm_b_im, ssm_c_re, ssm_c_im, ssm_d, ssm_w_glu,
                           mlp_norm_w, mlp_w_up, mlp_w_down, final_norm_w)))
    mom = dict(zip(names, (m_meta_tokens, m_attn_norm_w, m_attn_w_qkv, m_attn_sinks, m_attn_w_o, m_ssm_norm_w,
                           m_ssm_lambda_re, m_ssm_lambda_im, m_ssm_log_dt, m_ssm_b_re, m_ssm_b_im, m_ssm_c_re,
                           m_ssm_c_im, m_ssm_d, m_ssm_w_glu, m_mlp_norm_w, m_mlp_w_up, m_mlp_w_down, m_final_norm_w)))
    var = dict(zip(names, (v_meta_tokens, v_attn_norm_w, v_attn_w_qkv, v_attn_sinks, v_attn_w_o, v_ssm_norm_w,
                           v_ssm_lambda_re, v_ssm_lambda_im, v_ssm_log_dt, v_ssm_b_re, v_ssm_b_im, v_ssm_c_re,
                           v_ssm_c_im, v_ssm_d, v_ssm_w_glu, v_mlp_norm_w, v_mlp_w_up, v_mlp_w_down, v_final_norm_w)))

    my_chip = 2 * lax.axis_index("x") + lax.axis_index("y")
    my_dev = 2 * my_chip + lax.axis_index("c")
    small_mine = _pack([wts[n] for n in _SHARDED_SMALL], 128)
    first = [attn_w_qkv.astype(BF16), attn_w_o.astype(BF16), small_mine]
    up16, down16 = mlp_w_up.astype(BF16), mlp_w_down.astype(BF16)
    mlp0 = [up16[0:1], down16[0:1]]
    rest = [ssm_w_glu.astype(BF16), up16[1:2], down16[1:2]]
    handles, _ = _split_start([(srcs, [_landing(a, my_chip, N_CHIPS) for a in srcs]) for srcs in (first, mlp0, rest)],
                              _gather_copies, N_GATHER_PEERS, "gather_start")
    full = {n: wts[n] for n in _REP_MISC}
    full["final_norm_w"] = final_norm_w[None]
    for n in _REP_SSM:
        full[n] = wts[n][0]
    full["ssm_operands"] = _ssm_operands(full, x.shape[1] + BLOCK)
    got = _split_wait(handles[0], full["ssm_operands"], _gather_copies, "gather_wait_first")
    full["attn_w_qkv"], full["attn_w_o"] = got[0], got[1]
    smalls = [_unpack(got[2][s], [wts[n] for n in _SHARDED_SMALL]) for s in range(N_CHIPS)]
    for k, n in enumerate(_SHARDED_SMALL):
        full[n] = jnp.concatenate([smalls[s][k] for s in range(N_CHIPS)], axis=1)

    def late_weights(stage, after):
        if stage == 0:
            up, down = _split_wait(handles[1], after, _gather_copies, "gather_wait_mlp0")
            return {"mlp_w_up": [up], "mlp_w_down": [down]}
        glu, up, down = _split_wait(handles[2], after, _gather_copies, "gather_wait_rest")
        return {"ssm_w_glu": glu, "mlp_w_up": [up], "mlp_w_down": [down]}

    def shard_cols(t):
        return jnp.swapaxes(t.reshape(t.shape[0], N_CHIPS, t.shape[1] // N_CHIPS), 0, 1)

    pending = {}

    def on_grads(tag, g):
        scatter = [g[n] for n in _BIG if n in g]
        whole = []
        if tag == "ssm":
            whole = [_pack([g[n] for n in _REP_SSM], D_MODEL)]
        if tag == "rest":
            parts = [shard_cols(g[n]) for n in _SHARDED_SMALL]
            scatter = [jnp.stack([_pack([p[s] for p in parts], 128) for s in range(N_CHIPS)])]
            whole = [_pack([g[n] for n in _REP_MISC] + [g["loss"]], D_MODEL)]
        srcs = scatter + whole
        lands = [_landing(lax.dynamic_index_in_dim(a, my_chip, 0, keepdims=False), my_dev, N_DEV) for a in scatter]
        lands += [_landing(a, my_dev, N_DEV) for a in whole]
        hs, token = _split_start([(srcs, lands)], _exchange_copies(len(scatter)), N_EXCHANGE_PEERS, "exchange_start_" + tag)
        pending[tag] = (hs[0], len(scatter))
        return token[0, 0]

    _, grad_x = _local_step(x, loss_target, full, late_weights, on_grads)

    recv = {}
    for tag, (handle, n_scatter) in pending.items():
        recv[tag] = _split_wait(handle, grad_x, _exchange_copies(n_scatter), "exchange_wait_" + tag)
    loss = jnp.sum(recv["rest"][1].reshape(N_DEV, -1)[:, sum(wts[n].size for n in _REP_MISC)])

    out = {}

    def update(tag, pieces, w2, m2, v2):
        return _adamw(pieces, w2, m2, v2, "adamw_" + tag)

    def update_weight(n, pieces):
        shp = wts[n].shape
        r2 = (math.prod(shp[:-1]), shp[-1])
        res = update(n, pieces, wts[n].reshape(r2), mom[n].reshape(r2), var[n].reshape(r2))
        out[n] = [t.reshape(shp) for t in res]

    update_weight("mlp_w_up", [recv["mlp0"][1], recv["mlp1"][0]])
    update_weight("mlp_w_down", [recv["mlp0"][2], recv["mlp1"][1]])
    update_weight("attn_w_o", [recv["mlp0"][0]])
    update_weight("ssm_w_glu", [recv["ssm"][0]])
    update_weight("attn_w_qkv", [recv["qkv"][0]])
    for tag, group, pieces, cols in (("small", _SHARDED_SMALL, recv["rest"][0], 128),
                                     ("rep_ssm", _REP_SSM, recv["ssm"][1], D_MODEL),
                                     ("rep_misc", _REP_MISC, recv["rest"][1], D_MODEL)):
        like = [wts[n] for n in group]
        res = update(tag, [pieces], _pack(like, cols), _pack([mom[n] for n in group], cols),
                     _pack([var[n] for n in group], cols))
        for k, n in enumerate(group):
            out[n] = [_unpack(t, like)[k] for t in res]

    return (loss, grad_x, *[out[n][0] for n in names], *[out[n][1] for n in names],
            *[out[n][2] for n in names], *[out[n][3] for n in names])
```

```python
import functools
import math

import jax
import jax.numpy as jnp
from jax import lax
from jax.experimental import pallas as pl
from jax.experimental.pallas import tpu as pltpu

F32 = jnp.float32
BF16 = jnp.bfloat16
SDS = jax.ShapeDtypeStruct

D_MODEL = 1024
N_HEADS = 16
N_KV = 4
GQA = N_HEADS // N_KV
HEAD_DIM = 64
BLOCK = 128
N_META = 16
PAD = BLOCK - N_META
QKV_DIM = (N_HEADS + 2 * N_KV) * HEAD_DIM
KV_DIM = 2 * N_KV * HEAD_DIM
D_FF = 4 * D_MODEL
N_CHIPS = 4
N_DEV = 8
SSM_GROUP = 16
SSM_NG = D_MODEL // SSM_GROUP
SSM_STATE = 64
N_PAIR = SSM_NG // 2
PAIRS_PER_CHUNK = 4
RMS_EPS = 1e-6
NEG_INF = -1e30
LAMBDA_RE_MAX = -1e-4
ADAM_LR, ADAM_B1, ADAM_B2, ADAM_EPS, ADAM_WD, ADAM_STEP = 0.001, 0.9, 0.999, 1e-08, 0.01, 10

TM = 384
MM_TILES = (1056, 768, 384)
MLP_FWD_TILES = (384,)
MLP_BWD_TILES = (768, 384)
TN_TILES = (1408, 768, 384)
VMEM_LIMIT = 56 * 1024 * 1024


def _params(n_grid):
    return pltpu.CompilerParams(dimension_semantics=("arbitrary",) * n_grid, vmem_limit_bytes=VMEM_LIMIT)


def _row_tile(n_rows, tiles):
    return next(t for t in tiles if n_rows % t == 0)


def _rms(h, w):
    r = lax.rsqrt(jnp.mean(h * h, axis=-1, keepdims=True) + RMS_EPS)
    return h * r * w


def _rms_bwd(dhn, h, w):
    r = lax.rsqrt(jnp.mean(h * h, axis=-1, keepdims=True) + RMS_EPS)
    g = dhn * w
    proj = jnp.sum(g * h, axis=-1, keepdims=True) * (1.0 / D_MODEL)
    return r * g - h * (r * r * r) * proj, dhn * h * r


def _fold8(t):
    return jnp.sum(t.reshape(t.shape[0] // 8, 8, t.shape[1]), axis=0)


def _gelu(y):
    return 0.5 * y * (1.0 + jnp.tanh(0.7978845608028654 * (y + 0.044715 * y * y * y)))


def _gelu_grad(y):
    t = jnp.tanh(0.7978845608028654 * (y + 0.044715 * y * y * y))
    return 0.5 * (1.0 + t) + 0.5 * y * (1.0 - t * t) * 0.7978845608028654 * (1.0 + 3.0 * 0.044715 * y * y)


def _w4_spec(w4):
    n_sh, _, k, n = w4.shape
    return pl.BlockSpec((n_sh, None, k, n), lambda i: (0, 0, 0, 0))


def _rms_mm_cols(h, wn, w4, name):
    n_rows = h.shape[0]
    n_sh, _, k, n = w4.shape
    tm = _row_tile(n_rows, MM_TILES)

    def body(h_ref, wn_ref, w_ref, o_ref, hn_ref):
        hn = _rms(h_ref[...], wn_ref[...]).astype(BF16)
        hn_ref[...] = hn
        for s in range(n_sh):
            o_ref[:, s * n:(s + 1) * n] = jnp.dot(hn, w_ref[s], preferred_element_type=F32).astype(o_ref.dtype)

    return pl.pallas_call(
        body, name=name, grid=(n_rows // tm,),
        in_specs=[pl.BlockSpec((tm, k), lambda i: (i, 0)), pl.BlockSpec((1, k), lambda i: (0, 0)), _w4_spec(w4)],
        out_specs=[pl.BlockSpec((tm, n_sh * n), lambda i: (i, 0)), pl.BlockSpec((tm, k), lambda i: (i, 0))],
        out_shape=[SDS((n_rows, n_sh * n), BF16), SDS((n_rows, k), BF16)],
        compiler_params=_params(1),
    )(h, wn, w4)


def _mm_cols(x, w4, trans_w, name):
    n_rows, kx = x.shape
    tm = _row_tile(n_rows, MM_TILES)
    n_sh, _, k, n = w4.shape
    n_out = k if trans_w else n
    dims = (((1,), (1,)), ((), ())) if trans_w else (((1,), (0,)), ((), ()))

    def body(x_ref, w_ref, o_ref):
        x16 = x_ref[...].astype(BF16)
        for s in range(n_sh):
            o_ref[:, s * n_out:(s + 1) * n_out] = lax.dot_general(
                x16, w_ref[s], dims, preferred_element_type=F32).astype(o_ref.dtype)

    return pl.pallas_call(
        body, name=name, grid=(n_rows // tm,),
        in_specs=[pl.BlockSpec((tm, kx), lambda i: (i, 0)), _w4_spec(w4)],
        out_specs=pl.BlockSpec((tm, n_sh * n_out), lambda i: (i, 0)),
        out_shape=SDS((n_rows, n_sh * n_out), BF16),
        compiler_params=_params(1),
    )(x, w4)


def _mm_acc(x, w4, trans_w, name, res=None, rms_bwd=None, out_dtype=F32):
    n_rows = x.shape[0]
    tm = _row_tile(n_rows, MM_TILES)
    n_sh, _, k, n = w4.shape
    kx, n_out = (n, k) if trans_w else (k, n)
    dims = (((1,), (1,)), ((), ())) if trans_w else (((1,), (0,)), ((), ()))

    def body(*refs):
        if rms_bwd is not None:
            x_ref, w_ref, h_ref, wn_ref, dres_ref, o_ref, dw_ref = refs
        elif res is not None:
            x_ref, w_ref, res_ref, o_ref = refs
        else:
            x_ref, w_ref, o_ref = refs
        acc = None
        for s in range(n_sh):
            part = lax.dot_general(x_ref[:, s * kx:(s + 1) * kx].astype(BF16), w_ref[s], dims, preferred_element_type=F32)
            acc = part if acc is None else acc + part
        if rms_bwd is not None:
            dh, dw_rows = _rms_bwd(acc, h_ref[...], wn_ref[...])
            o_ref[...] = (dres_ref[...] + dh).astype(o_ref.dtype)

            @pl.when(pl.program_id(0) == 0)
            def _():
                dw_ref[...] = jnp.zeros_like(dw_ref)

            dw_ref[...] += _fold8(dw_rows)
        elif res is not None:
            o_ref[...] = (res_ref[...] + acc).astype(o_ref.dtype)
        else:
            o_ref[...] = acc.astype(o_ref.dtype)

    row = lambda i: (i, 0)
    in_specs = [pl.BlockSpec((tm, n_sh * kx), row), _w4_spec(w4)]
    args = [x, w4]
    out_specs = pl.BlockSpec((tm, n_out), row)
    out_shape = SDS((n_rows, n_out), out_dtype)
    if rms_bwd is not None:
        h, wn, dres = rms_bwd
        in_specs += [pl.BlockSpec((tm, n_out), row), pl.BlockSpec((1, n_out), lambda i: (0, 0)),
                     pl.BlockSpec((tm, n_out), row)]
        args += [h, wn, dres]
        out_specs = [out_specs, pl.BlockSpec((8, n_out), lambda i: (0, 0))]
        out_shape = [out_shape, SDS((8, n_out), F32)]
    elif res is not None:
        in_specs.append(pl.BlockSpec((tm, n_out), row))
        args.append(res)
    return pl.pallas_call(
        body, name=name, grid=(n_rows // tm,), in_specs=in_specs, out_specs=out_specs, out_shape=out_shape,
        compiler_params=_params(1),
    )(*args)


def _mm_tn(a, b, n_sh, a_sharded, name):
    n_rows = a.shape[0]
    tm = _row_tile(n_rows, TN_TILES)
    ka = a.shape[1] // n_sh if a_sharded else a.shape[1]
    nb = b.shape[1] if a_sharded else b.shape[1] // n_sh
    n_i = n_rows // tm

    def body(a_ref, b_ref, o_ref, acc):
        i = pl.program_id(0)

        @pl.when(i == 0)
        def _():
            acc[...] = jnp.zeros_like(acc)

        for s in range(n_sh):
            a_s = a_ref[:, s * ka:(s + 1) * ka] if a_sharded else a_ref[...]
            b_s = b_ref[...] if a_sharded else b_ref[:, s * nb:(s + 1) * nb]
            acc[s] += lax.dot_general(a_s.astype(BF16), b_s.astype(BF16), (((0,), (0,)), ((), ())),
                                      preferred_element_type=F32)

        @pl.when(i == n_i - 1)
        def _():
            o_ref[...] = acc[...].astype(o_ref.dtype)

    return pl.pallas_call(
        body, name=name, grid=(n_i,),
        in_specs=[pl.BlockSpec((tm, a.shape[1]), lambda i: (i, 0)), pl.BlockSpec((tm, b.shape[1]), lambda i: (i, 0))],
        out_specs=pl.BlockSpec((n_sh, ka, nb), lambda i: (0, 0, 0)),
        out_shape=SDS((n_sh, ka, nb), BF16),
        scratch_shapes=[pltpu.VMEM((n_sh, ka, nb), F32)], compiler_params=_params(1),
    )(a, b)


def _mlp_fwd(h, wn, w_up4, w_down4, name, next_norm=None, glu_z=None):
    n_rows = h.shape[0]
    tm = _row_tile(n_rows, MLP_FWD_TILES)
    n_sh = w_up4.shape[0]
    f_sh = D_FF // n_sh
    w_down = w_down4.reshape(D_FF, D_MODEL)

    def body(*refs):
        refs = list(refs)
        h_ref, wn_ref, wu_ref, wd_ref = refs[:4]
        at = 4
        if next_norm is not None:
            nn_ref = refs[at]
            at += 1
        if glu_z is not None:
            z_ref = refs[at]
            at += 1
        o_ref, a_ref, hn_ref = refs[at:at + 3]
        at += 3
        if next_norm is not None:
            u_ref = refs[at]
            at += 1
        if glu_z is not None:
            hin_ref = refs[at]
            at += 1
        act_s = refs[at]
        h_in = h_ref[...]
        if glu_z is not None:
            h_in = h_in + z_ref[:, 0:D_MODEL].astype(F32) * jax.nn.sigmoid(z_ref[:, D_MODEL:2 * D_MODEL].astype(F32))
            hin_ref[...] = h_in
        hn = _rms(h_in, wn_ref[...]).astype(BF16)
        hn_ref[...] = hn
        for s in range(n_sh):
            cols = slice(s * f_sh, (s + 1) * f_sh)
            a = jnp.dot(hn, wu_ref[s], preferred_element_type=F32)
            a_ref[:, cols] = a.astype(BF16)
            act = jnp.maximum(a, 0.0)
            act_s[:, cols] = (act * act).astype(BF16)
        out = h_in + jnp.dot(act_s[...], wd_ref[...], preferred_element_type=F32)
        o_ref[...] = out
        if next_norm is not None:
            u_ref[...] = _rms(out, nn_ref[...])

    row = lambda i: (i, 0)
    vec = pl.BlockSpec((1, D_MODEL), lambda i: (0, 0))
    in_specs = [pl.BlockSpec((tm, D_MODEL), row), vec,
                pl.BlockSpec((n_sh, None, D_MODEL, f_sh), lambda i: (0, 0, 0, 0), pipeline_mode=pl.Buffered(1)),
                pl.BlockSpec((D_FF, D_MODEL), lambda i: (0, 0), pipeline_mode=pl.Buffered(1))]
    out_specs = [pl.BlockSpec((tm, D_MODEL), row), pl.BlockSpec((tm, D_FF), row), pl.BlockSpec((tm, D_MODEL), row)]
    out_shape = [SDS((n_rows, D_MODEL), F32), SDS((n_rows, D_FF), BF16), SDS((n_rows, D_MODEL), BF16)]
    args = [h, wn, w_up4, w_down]
    if next_norm is not None:
        in_specs.append(vec)
        args.append(next_norm)
    if glu_z is not None:
        in_specs.append(pl.BlockSpec((tm, 2 * D_MODEL), row))
        args.append(glu_z)
    for extra in (next_norm, glu_z):
        if extra is not None:
            out_specs.append(pl.BlockSpec((tm, D_MODEL), row))
            out_shape.append(SDS((n_rows, D_MODEL), F32))
    return pl.pallas_call(
        body, name=name, grid=(n_rows // tm,), in_specs=in_specs, out_specs=out_specs, out_shape=out_shape,
        scratch_shapes=[pltpu.VMEM((tm, D_FF), BF16)],
        compiler_params=_params(1),
    )(*args)


def _mlp_bwd_shard(s, dh, a, hn, dhn_prev, h, wn, w_up4, w_down4, dw_up_buf, dw_down_buf, name):
    n_rows = dh.shape[0]
    n_sh = w_up4.shape[0]
    f_sh = D_FF // n_sh
    tm = _row_tile(n_rows, MLP_BWD_TILES)
    n_i = n_rows // tm
    last = h is not None
    nt = (((1,), (1,)), ((), ()))
    tn = (((0,), (0,)), ((), ()))

    def body(*refs):
        refs = list(refs)
        dh_ref, a_ref, hn_ref, wu_ref, wd_ref = refs[:5]
        at = 5
        prev_ref = None
        if dhn_prev is not None:
            prev_ref = refs[at]
            at += 1
        if last:
            h_ref, wn_ref = refs[at:at + 2]
            at += 2
        if dw_up_buf is not None:
            at += 2
        o_ref, dwu_ref, dwd_ref = refs[at:at + 3]
        at += 3
        if last:
            dnorm_ref = refs[at]
            at += 1
        acc_u, acc_d = refs[at:at + 2]
        i = pl.program_id(0)

        @pl.when(i == 0)
        def _():
            acc_u[...] = jnp.zeros_like(acc_u)
            acc_d[...] = jnp.zeros_like(acc_d)
            if last:
                dnorm_ref[...] = jnp.zeros_like(dnorm_ref)

        dh16 = dh_ref[...].astype(BF16)
        r = jnp.maximum(a_ref[...].astype(F32), 0.0)
        dact = lax.dot_general(dh16, wd_ref[...], nt, preferred_element_type=F32)
        da16 = (dact * (2.0 * r)).astype(BF16)
        acc_d[...] += lax.dot_general((r * r).astype(BF16), dh16, tn, preferred_element_type=F32)
        acc_u[...] += lax.dot_general(hn_ref[...], da16, tn, preferred_element_type=F32)
        dhn = lax.dot_general(da16, wu_ref[...], nt, preferred_element_type=F32)
        if prev_ref is not None:
            dhn = dhn + prev_ref[...]
        if last:
            d_rms, dw_rows = _rms_bwd(dhn, h_ref[...], wn_ref[...])
            o_ref[...] = dh_ref[...] + d_rms
            dnorm_ref[...] += _fold8(dw_rows)
        else:
            o_ref[...] = dhn

        @pl.when(i == n_i - 1)
        def _():
            dwu_ref[...] = acc_u[...].astype(BF16)
            dwd_ref[...] = acc_d[...].astype(BF16)

    row = lambda i: (i, 0)
    tile = pl.BlockSpec((tm, D_MODEL), row)
    in_specs = [tile, pl.BlockSpec((tm, f_sh), lambda i: (i, s)), tile,
                pl.BlockSpec((None, None, D_MODEL, f_sh), lambda i: (s, 0, 0, 0)),
                pl.BlockSpec((None, None, f_sh, D_MODEL), lambda i: (s, 0, 0, 0))]
    args = [dh, a, hn, w_up4, w_down4]
    if dhn_prev is not None:
        in_specs.append(tile)
        args.append(dhn_prev)
    if last:
        in_specs += [tile, pl.BlockSpec((1, D_MODEL), lambda i: (0, 0))]
        args += [h, wn]
    aliases = {}
    if dw_up_buf is not None:
        aliases = {len(args): 1, len(args) + 1: 2}
        in_specs += [pl.BlockSpec(memory_space=pl.ANY)] * 2
        args += [dw_up_buf, dw_down_buf]
    out_specs = [tile, pl.BlockSpec((None, D_MODEL, f_sh), lambda i: (s, 0, 0)),
                 pl.BlockSpec((None, f_sh, D_MODEL), lambda i: (s, 0, 0))]
    out_shape = [SDS((n_rows, D_MODEL), F32), SDS((n_sh, D_MODEL, f_sh), BF16), SDS((n_sh, f_sh, D_MODEL), BF16)]
    if last:
        out_specs.append(pl.BlockSpec((8, D_MODEL), lambda i: (0, 0)))
        out_shape.append(SDS((8, D_MODEL), F32))
    return pl.pallas_call(
        body, name=name, grid=(n_i,), in_specs=in_specs, out_specs=out_specs, out_shape=out_shape,
        input_output_aliases=aliases,
        scratch_shapes=[pltpu.VMEM((D_MODEL, f_sh), F32), pltpu.VMEM((f_sh, D_MODEL), F32)],
        compiler_params=_params(1),
    )(*args)


def _attn_masks(n):
    qi = lax.broadcasted_iota(jnp.int32, (BLOCK, 3 * BLOCK), 0)
    col = lax.broadcasted_iota(jnp.int32, (BLOCK, 3 * BLOCK), 1)
    kj = col - BLOCK
    dist = BLOCK + qi - kj
    kmin = jnp.where(n == 0, 2 * BLOCK, jnp.where(n == 1, BLOCK, 0))
    band_ok = (col >= BLOCK) & (dist >= 0) & (dist < BLOCK) & (kj >= kmin)
    q_pos = n * BLOCK + qi - PAD
    meta_ok = (col >= PAD) & (col < BLOCK) & (col - PAD <= q_pos)
    distf = jnp.where(col >= BLOCK, dist, 0).astype(F32)
    return band_ok | meta_ok, distf


def _alibi_slope(h):
    return float(2.0 ** (-8.0 * (h + 1) / N_HEADS))


def _attn_bias(n, bias_s):
    ok, distf = _attn_masks(n)
    for h in range(N_HEADS):
        bias_s[h] = jnp.where(ok, -_alibi_slope(h) * distf, NEG_INF)


def _attn_fwd(qkv, sinks, n_ex, nb):
    n_rows = qkv.shape[0]
    kvb = N_HEADS * HEAD_DIM // KV_DIM

    def body(sink_ref, q_ref, kvm_ref, kvp_ref, kvc_ref, o_ref, lse_ref, k_s, v_s, q_s, bias_s):
        n = pl.program_id(1)

        @pl.when(n <= 2)
        def _():
            _attn_bias(n, bias_s)

        v_s[...] = jnp.ones_like(v_s)
        for part, ref in enumerate((kvm_ref, kvp_ref, kvc_ref)):
            rows = slice(part * BLOCK, (part + 1) * BLOCK)
            k_s[rows, :] = ref[:, 0:N_KV * HEAD_DIM]
            for kv in range(N_KV):
                v_s[rows, kv * 2 * HEAD_DIM:kv * 2 * HEAD_DIM + HEAD_DIM] = \
                    ref[:, (N_KV + kv) * HEAD_DIM:(N_KV + kv + 1) * HEAD_DIM]
        def scores(kv):
            for g in range(GQA):
                h = kv * GQA + g
                q_s[kv, g * BLOCK:(g + 1) * BLOCK, :] = q_ref[:, h * HEAD_DIM:(h + 1) * HEAD_DIM] * (HEAD_DIM ** -0.5)
            return lax.dot_general(q_s[kv], k_s[:, kv * HEAD_DIM:(kv + 1) * HEAD_DIM], (((1,), (1,)), ((), ())),
                                   preferred_element_type=F32)

        ahead = scores(0)
        for kv in range(N_KV):
            s4 = ahead
            if kv + 1 < N_KV:
                ahead = scores(kv + 1)
            es, ms, sink_es = [], [], []
            for g in range(GQA):
                h = kv * GQA + g
                s = s4[g * BLOCK:(g + 1) * BLOCK] + bias_s[h]
                sink = sink_ref[0, h]
                m = jnp.maximum(jnp.max(s, axis=-1, keepdims=True), sink)
                es.append(jnp.exp(s - m).astype(BF16))
                ms.append(m)
                sink_es.append(jnp.exp(sink - m))
            pv = jnp.dot(jnp.concatenate(es, axis=0), v_s[:, kv * 2 * HEAD_DIM:(kv + 1) * 2 * HEAD_DIM],
                         preferred_element_type=F32)
            for g in range(GQA):
                h = kv * GQA + g
                pg = pv[g * BLOCK:(g + 1) * BLOCK]
                l = pg[:, HEAD_DIM:HEAD_DIM + 1] + sink_es[g]
                o_ref[:, h * HEAD_DIM:(h + 1) * HEAD_DIM] = (pg[:, 0:HEAD_DIM] * (1.0 / l)).astype(BF16)
                lse_ref[:, h:h + 1] = ms[g] + jnp.log(l)

    return pl.pallas_call(
        body, name="attn_fwd", grid=(n_ex, nb),
        in_specs=[pl.BlockSpec(memory_space=pltpu.SMEM),
                  pl.BlockSpec((BLOCK, N_HEADS * HEAD_DIM), lambda b, n: (b * nb + n, 0)),
                  pl.BlockSpec((BLOCK, KV_DIM), lambda b, n: (b * nb, kvb)),
                  pl.BlockSpec((BLOCK, KV_DIM), lambda b, n: (b * nb + jnp.maximum(n - 1, 0), kvb)),
                  pl.BlockSpec((BLOCK, KV_DIM), lambda b, n: (b * nb + n, kvb))],
        out_specs=[pl.BlockSpec((BLOCK, N_HEADS * HEAD_DIM), lambda b, n: (b * nb + n, 0)),
                   pl.BlockSpec((BLOCK, N_HEADS), lambda b, n: (b * nb + n, 0))],
        out_shape=[SDS((n_rows, N_HEADS * HEAD_DIM), BF16), SDS((n_rows, N_HEADS), F32)],
        scratch_shapes=[pltpu.VMEM((3 * BLOCK, N_KV * HEAD_DIM), BF16), pltpu.VMEM((3 * BLOCK, 2 * N_KV * HEAD_DIM), BF16),
                        pltpu.VMEM((N_KV, GQA * BLOCK, HEAD_DIM), BF16), pltpu.VMEM((N_HEADS, BLOCK, 3 * BLOCK), F32)],
        compiler_params=_params(2),
    )(sinks, qkv, qkv, qkv, qkv)


def _attn_bwd(qkv, sinks, o, lse, do, n_ex, nb):
    n_rows = qkv.shape[0]
    kvb = N_HEADS * HEAD_DIM // KV_DIM
    scale = HEAD_DIM ** -0.5
    nq = lambda r: nb - 1 - r

    def body(sink_ref, q_ref, kvm_ref, kvp_ref, kvc_ref, o_ref, lse_ref, do_ref, dqkv_ref, dsink_ref,
             k_s, v_s, dkv_s, carry_s, meta_s, q_s, do_s, bias_s):
        b, r = pl.program_id(0), pl.program_id(1)
        n = nq(r)

        @pl.when((r == 0) | (n <= 1))
        def _():
            _attn_bias(n, bias_s)

        @pl.when((b == 0) & (r == 0))
        def _():
            dsink_ref[...] = jnp.zeros_like(dsink_ref)

        @pl.when(r == 0)
        def _():
            carry_s[...] = jnp.zeros_like(carry_s)
            meta_s[...] = jnp.zeros_like(meta_s)

        for part, ref in enumerate((kvm_ref, kvp_ref, kvc_ref)):
            k_s[part * BLOCK:(part + 1) * BLOCK, :] = ref[:, 0:N_KV * HEAD_DIM]
            v_s[part * BLOCK:(part + 1) * BLOCK, :] = ref[:, N_KV * HEAD_DIM:KV_DIM]
        nt = (((1,), (1,)), ((), ()))
        tn = (((0,), (0,)), ((), ()))
        for kv in range(N_KV):
            kcols = slice(kv * HEAD_DIM, (kv + 1) * HEAD_DIM)
            vcols = slice(N_KV * HEAD_DIM + kv * HEAD_DIM, N_KV * HEAD_DIM + (kv + 1) * HEAD_DIM)
            for g in range(GQA):
                cols = slice((kv * GQA + g) * HEAD_DIM, (kv * GQA + g + 1) * HEAD_DIM)
                q_s[kv, g * BLOCK:(g + 1) * BLOCK, :] = q_ref[:, cols] * scale
                do_s[kv, g * BLOCK:(g + 1) * BLOCK, :] = do_ref[:, cols]
            kh, vh = k_s[:, kcols], v_s[:, kcols]
            s4 = lax.dot_general(q_s[kv], kh, nt, preferred_element_type=F32)
            dp4 = lax.dot_general(do_s[kv], vh, nt, preferred_element_type=F32)
            ps, dss = [], []
            for g in range(GQA):
                h = kv * GQA + g
                cols = slice(h * HEAD_DIM, (h + 1) * HEAD_DIM)
                rows = slice(g * BLOCK, (g + 1) * BLOCK)
                s = s4[rows] + bias_s[h]
                lse_h = lse_ref[:, h:h + 1]
                p = jnp.exp(s - lse_h)
                delta = jnp.sum(do_ref[:, cols].astype(F32) * o_ref[:, cols].astype(F32), axis=-1, keepdims=True)
                dsink_ref[:, h:h + 1] += -jnp.exp(sink_ref[0, h] - lse_h) * delta
                ps.append(p.astype(BF16))
                dss.append((p * (dp4[rows] - delta)).astype(BF16))
            p4, ds4 = jnp.concatenate(ps, axis=0), jnp.concatenate(dss, axis=0)
            dq4 = jnp.dot(ds4, kh, preferred_element_type=F32) * scale
            for g in range(GQA):
                cols = slice((kv * GQA + g) * HEAD_DIM, (kv * GQA + g + 1) * HEAD_DIM)
                dqkv_ref[:, cols] = dq4[g * BLOCK:(g + 1) * BLOCK].astype(BF16)
            dkv_s[:, kcols] = lax.dot_general(ds4, q_s[kv], tn, preferred_element_type=F32)
            dkv_s[:, vcols] = lax.dot_general(p4, do_s[kv], tn, preferred_element_type=F32)

        meta_s[...] += dkv_s[0:BLOCK, :]
        cur = dkv_s[2 * BLOCK:3 * BLOCK, :] + carry_s[...]
        carry_s[...] = dkv_s[BLOCK:2 * BLOCK, :]

        @pl.when(n > 0)
        def _():
            dqkv_ref[:, N_HEADS * HEAD_DIM:QKV_DIM] = cur.astype(BF16)

        @pl.when(n == 0)
        def _():
            dqkv_ref[:, N_HEADS * HEAD_DIM:QKV_DIM] = (cur + meta_s[...]).astype(BF16)

    blk = lambda b, r: (b * nb + nq(r), 0)
    return pl.pallas_call(
        body, name="attn_bwd", grid=(n_ex, nb),
        in_specs=[pl.BlockSpec(memory_space=pltpu.SMEM),
                  pl.BlockSpec((BLOCK, N_HEADS * HEAD_DIM), blk),
                  pl.BlockSpec((BLOCK, KV_DIM), lambda b, r: (b * nb, kvb)),
                  pl.BlockSpec((BLOCK, KV_DIM), lambda b, r: (b * nb + jnp.maximum(nq(r) - 1, 0), kvb)),
                  pl.BlockSpec((BLOCK, KV_DIM), lambda b, r: (b * nb + nq(r), kvb)),
                  pl.BlockSpec((BLOCK, N_HEADS * HEAD_DIM), blk),
                  pl.BlockSpec((BLOCK, N_HEADS), blk),
                  pl.BlockSpec((BLOCK, N_HEADS * HEAD_DIM), blk)],
        out_specs=[pl.BlockSpec((BLOCK, QKV_DIM), blk),
                   pl.BlockSpec((BLOCK, N_HEADS), lambda b, r: (0, 0))],
        out_shape=[SDS((n_rows, QKV_DIM), BF16), SDS((BLOCK, N_HEADS), F32)],
        scratch_shapes=[pltpu.VMEM((3 * BLOCK, N_KV * HEAD_DIM), BF16), pltpu.VMEM((3 * BLOCK, N_KV * HEAD_DIM), BF16),
                        pltpu.VMEM((3 * BLOCK, KV_DIM), F32), pltpu.VMEM((BLOCK, KV_DIM), F32),
                        pltpu.VMEM((BLOCK, KV_DIM), F32), pltpu.VMEM((N_KV, GQA * BLOCK, HEAD_DIM), BF16),
                        pltpu.VMEM((N_KV, GQA * BLOCK, HEAD_DIM), BF16), pltpu.VMEM((N_HEADS, BLOCK, 3 * BLOCK), F32)],
        compiler_params=_params(2),
    )(sinks, qkv, qkv, qkv, qkv, o, lse, do)


SSM_TILES = (1408, 384)
XW = 256 * PAIRS_PER_CHUNK


def _cmul_add(xr, xi, mr, mi, sr, si):
    return xr + mr * sr - mi * si, xi + mr * si + mi * sr


def _to_segments(src_ref, dst, seg):
    for s in range(seg):
        dst[s * 8:(s + 1) * 8, :] = src_ref[pl.ds(s, 8, stride=seg), :]


def _from_segments(src, i, seg):
    return src[pl.ds(i, seg, stride=8), :]


def _scan_segments(buf, tab_ref, carry_s, seg, reverse):
    shifts = (7, 6, 4) if reverse else (1, 2, 4)
    row_id = lax.broadcasted_iota(jnp.int32, (8, 128), 0)
    a_tiles = [tab_ref[j, c] for j in range(PAIRS_PER_CHUNK) for c in (0, 1)]

    def local(si, prev):
        s = (seg - 1 - si) if reverse else si
        row = pl.multiple_of(s * 8, 8)
        out = []
        for j in range(PAIRS_PER_CHUNK):
            re, im = slice(256 * j, 256 * j + 128), slice(256 * j + 128, 256 * j + 256)
            xr, xi = _cmul_add(buf[pl.ds(row, 8), re], buf[pl.ds(row, 8), im],
                               a_tiles[2 * j], a_tiles[2 * j + 1], prev[2 * j], prev[2 * j + 1])
            buf[pl.ds(row, 8), re] = xr
            buf[pl.ds(row, 8), im] = xi
            out += [xr, xi]
        return tuple(out)

    zero = jnp.zeros((8, 128), F32)
    edge = lax.fori_loop(0, seg, local, (zero,) * (2 * PAIRS_PER_CHUNK))

    entering = []
    for j in range(PAIRS_PER_CHUNK):
        er, ei = edge[2 * j], edge[2 * j + 1]
        if reverse:
            sr = jnp.where(row_id == 7, carry_s[2 * j], pltpu.roll(er, 7, 0))
            si_ = jnp.where(row_id == 7, carry_s[2 * j + 1], pltpu.roll(ei, 7, 0))
        else:
            sr = jnp.where(row_id == 0, carry_s[2 * j], pltpu.roll(er, 1, 0))
            si_ = jnp.where(row_id == 0, carry_s[2 * j + 1], pltpu.roll(ei, 1, 0))
        for lvl, sh in enumerate(shifts):
            sr, si_ = _cmul_add(sr, si_, tab_ref[j, 2 + 2 * lvl], tab_ref[j, 3 + 2 * lvl],
                                pltpu.roll(sr, sh, 0), pltpu.roll(si_, sh, 0))
        entering += [sr, si_]
        tr, ti = _cmul_add(er, ei, tab_ref[j, 2], tab_ref[j, 3], sr, si_)
        out_row = slice(0, 1) if reverse else slice(7, 8)
        carry_s[2 * j] = jnp.broadcast_to(tr[out_row], (8, 128))
        carry_s[2 * j + 1] = jnp.broadcast_to(ti[out_row], (8, 128))

    def fix(si, carried):
        s = (seg - 1 - si) if reverse else si
        row = pl.multiple_of(s * 8, 8)
        out = []
        for j in range(PAIRS_PER_CHUNK):
            re, im = slice(256 * j, 256 * j + 128), slice(256 * j + 128, 256 * j + 256)
            ar, ai, fr, fi = a_tiles[2 * j], a_tiles[2 * j + 1], carried[2 * j], carried[2 * j + 1]
            fr, fi = ar * fr - ai * fi, ar * fi + ai * fr
            buf[pl.ds(row, 8), re] += fr
            buf[pl.ds(row, 8), im] += fi
            out += [fr, fi]
        return tuple(out)

    lax.fori_loop(0, seg, fix, tuple(entering))


def _ssm_fwd(u, b_pad, c_pad, tab, d_skip, n_ex, lp):
    n_rows = u.shape[0]
    TM = _row_tile(lp, SSM_TILES)
    SEG = TM // 8
    n_t = lp // TM
    n_chunk = D_MODEL // 128

    def body(u_ref, bp_ref, cp_ref, tab_ref, d_ref, yg_ref, y_ref, xs_ref, buf, carry_s, us, ys):
        @pl.when(pl.program_id(2) == 0)
        def _():
            carry_s[...] = jnp.zeros_like(carry_s)

        _to_segments(u_ref, us, SEG)
        ub = us[...]
        u16 = ub.astype(BF16)
        buf[...] = jnp.dot(u16, bp_ref[...], preferred_element_type=F32)
        _scan_segments(buf, tab_ref, carry_s, SEG, reverse=False)
        xb = buf[...].astype(BF16)
        xs_ref[...] = xb
        ys[...] = d_ref[...] * ub + jnp.dot(xb, cp_ref[...], preferred_element_type=F32)
        for i in range(8):
            yi = _from_segments(ys, i, SEG)
            y_ref[i * SEG:(i + 1) * SEG, :] = yi
            yg_ref[i * SEG:(i + 1) * SEG, :] = _gelu(yi).astype(BF16)

    rows = lambda b, q, t: (b * n_t + t, q)
    return pl.pallas_call(
        body, name="ssm_fwd", grid=(n_ex, n_chunk, n_t),
        in_specs=[pl.BlockSpec((TM, 128), rows),
                  pl.BlockSpec((None, 128, XW), lambda b, q, t: (q, 0, 0)),
                  pl.BlockSpec((None, XW, 128), lambda b, q, t: (q, 0, 0)),
                  pl.BlockSpec((PAIRS_PER_CHUNK, 8, 8, 128), lambda b, q, t: (q, 0, 0, 0)),
                  pl.BlockSpec((1, 128), lambda b, q, t: (0, q))],
        out_specs=[pl.BlockSpec((TM, 128), rows), pl.BlockSpec((TM, 128), rows),
                   pl.BlockSpec((None, TM, XW), lambda b, q, t: (q, b * n_t + t, 0))],
        out_shape=[SDS((n_rows, D_MODEL), BF16), SDS((n_rows, D_MODEL), F32), SDS((n_chunk, n_rows, XW), BF16)],
        scratch_shapes=[pltpu.VMEM((TM, XW), F32), pltpu.VMEM((2 * PAIRS_PER_CHUNK, 8, 128), F32),
                        pltpu.VMEM((TM, 128), F32), pltpu.VMEM((TM, 128), F32)],
        compiler_params=_params(3),
    )(u, b_pad, c_pad, tab, d_skip)


def _ssm_bwd(dyg, y, u, xs, ct_pad, bt_pad, tab_rev, d_skip, n_ex, lp):
    n_rows = u.shape[0]
    TM = _row_tile(lp, SSM_TILES)
    SEG = TM // 8
    n_t = lp // TM
    n_chunk = D_MODEL // 128
    tile = lambda q, b, t: (b * n_t + (n_t - 1 - t), q)

    def body(dyg_ref, y_ref, u_ref, xs_ref, xp_ref, ct_ref, bt_ref, tab_ref, d_ref,
             du_ref, db_ref, dc_ref, da_ref, dd_ref, buf, xf, carry_s, us, dys, dyp):
        b, t = pl.program_id(1), pl.program_id(2)

        @pl.when((b == 0) & (t == 0))
        def _():
            db_ref[...] = jnp.zeros_like(db_ref)
            dc_ref[...] = jnp.zeros_like(dc_ref)
            da_ref[...] = jnp.zeros_like(da_ref)
            dd_ref[...] = jnp.zeros_like(dd_ref)

        @pl.when(t == 0)
        def _():
            carry_s[...] = jnp.zeros_like(carry_s)

        dys[...] = dyg_ref[...].astype(F32) * _gelu_grad(y_ref[...])
        dd_ref[...] += _fold8(dys[...] * u_ref[...])
        _to_segments(dys, dyp, SEG)
        dy = dyp[...]
        _to_segments(u_ref, us, SEG)
        dy16 = dy.astype(BF16)
        first_tile = t == n_t - 1
        tn = (((0,), (0,)), ((), ()))
        buf[...] = jnp.dot(dy16, ct_ref[...], preferred_element_type=F32)
        dc_ref[...] += lax.dot_general(dy16, xs_ref[...], tn, preferred_element_type=F32)
        xf[16:16 + TM, :] = xs_ref[...].astype(F32)
        xf[0:16, :] = jnp.where(first_tile, 0.0, xp_ref[...].astype(F32))
        _scan_segments(buf, tab_ref, carry_s, SEG, reverse=True)
        g16 = buf[...].astype(BF16)
        dys[...] = d_ref[...] * dy + jnp.dot(g16, bt_ref[...], preferred_element_type=F32)
        db_ref[...] += lax.dot_general(us[...].astype(BF16), g16, tn, preferred_element_type=F32)
        row_id = lax.broadcasted_iota(jnp.int32, (8, 128), 0)
        for j in range(PAIRS_PER_CHUNK):
            re, im = slice(256 * j, 256 * j + 128), slice(256 * j + 128, 256 * j + 256)
            first = [jnp.where(row_id == 0, jnp.broadcast_to(xf[15:16, c], (8, 128)),
                               pltpu.roll(xf[8 + TM:16 + TM, c], 1, 0)) for c in (re, im)]
            for rows, pr, pi in ((slice(0, 8), first[0], first[1]),
                                 (slice(8, TM), xf[16:8 + TM, re], xf[16:8 + TM, im])):
                gr, gi = buf[rows, re], buf[rows, im]
                da_ref[j, 0] += _fold8(gr * pr + gi * pi)
                da_ref[j, 1] += _fold8(gi * pr - gr * pi)
        for i in range(8):
            du_ref[i * SEG:(i + 1) * SEG, :] = _from_segments(dys, i, SEG)

    prev16 = lambda q, b, t: (q, jnp.maximum((b * n_t + (n_t - 1 - t)) * (TM // 16) - 1, 0), 0)
    return pl.pallas_call(
        body, name="ssm_bwd", grid=(n_chunk, n_ex, n_t),
        in_specs=[pl.BlockSpec((TM, 128), tile), pl.BlockSpec((TM, 128), tile), pl.BlockSpec((TM, 128), tile),
                  pl.BlockSpec((None, TM, XW), lambda q, b, t: (q, b * n_t + (n_t - 1 - t), 0)),
                  pl.BlockSpec((None, 16, XW), prev16),
                  pl.BlockSpec((None, 128, XW), lambda q, b, t: (q, 0, 0)),
                  pl.BlockSpec((None, XW, 128), lambda q, b, t: (q, 0, 0)),
                  pl.BlockSpec((PAIRS_PER_CHUNK, 8, 8, 128), lambda q, b, t: (q, 0, 0, 0)),
                  pl.BlockSpec((1, 128), lambda q, b, t: (0, q))],
        out_specs=[pl.BlockSpec((TM, 128), tile),
                   pl.BlockSpec((None, 128, XW), lambda q, b, t: (q, 0, 0)),
                   pl.BlockSpec((None, 128, XW), lambda q, b, t: (q, 0, 0)),
                   pl.BlockSpec((PAIRS_PER_CHUNK, 2, 8, 128), lambda q, b, t: (q, 0, 0, 0)),
                   pl.BlockSpec((8, 128), lambda q, b, t: (0, q))],
        out_shape=[SDS((n_rows, D_MODEL), F32), SDS((n_chunk, 128, XW), F32), SDS((n_chunk, 128, XW), F32),
                   SDS((N_PAIR, 2, 8, 128), F32), SDS((8, D_MODEL), F32)],
        scratch_shapes=[pltpu.VMEM((TM, XW), F32), pltpu.VMEM((TM + 16, XW), F32),
                        pltpu.VMEM((2 * PAIRS_PER_CHUNK, 8, 128), F32), pltpu.VMEM((TM, 128), F32),
                        pltpu.VMEM((TM, 128), F32), pltpu.VMEM((TM, 128), F32)],
        compiler_params=_params(3),
    )(dyg, y, u, xs, xs, ct_pad, bt_pad, tab_rev, d_skip)


def _rms_bwd_call(dhn, h, wn, dres, name):
    n_rows = h.shape[0]

    def body(dhn_ref, h_ref, wn_ref, dres_ref, o_ref, dw_ref):
        @pl.when(pl.program_id(0) == 0)
        def _():
            dw_ref[...] = jnp.zeros_like(dw_ref)

        dh, dw_rows = _rms_bwd(dhn_ref[...], h_ref[...], wn_ref[...])
        o_ref[...] = dres_ref[...] + dh
        dw_ref[...] += _fold8(dw_rows)

    row = lambda i: (i, 0)
    return pl.pallas_call(
        body, name=name, grid=(n_rows // TM,),
        in_specs=[pl.BlockSpec((TM, D_MODEL), row), pl.BlockSpec((TM, D_MODEL), row),
                  pl.BlockSpec((1, D_MODEL), lambda i: (0, 0)), pl.BlockSpec((TM, D_MODEL), row)],
        out_specs=[pl.BlockSpec((TM, D_MODEL), row), pl.BlockSpec((8, D_MODEL), lambda i: (0, 0))],
        out_shape=[SDS((n_rows, D_MODEL), F32), SDS((8, D_MODEL), F32)], compiler_params=_params(1),
    )(dhn, h, wn, dres)


def _glu_bwd(dh, z, w4):
    n_rows = dh.shape[0]
    tm = _row_tile(n_rows, MM_TILES)
    n_sh, _, k, n = w4.shape

    def body(dh_ref, z_ref, w_ref, dz_ref, dyg_ref):
        sg = jax.nn.sigmoid(z_ref[:, D_MODEL:2 * D_MODEL].astype(F32))
        d = dh_ref[...]
        dz_ref[:, 0:D_MODEL] = (d * sg).astype(BF16)
        dz_ref[:, D_MODEL:2 * D_MODEL] = (d * z_ref[:, 0:D_MODEL].astype(F32) * sg * (1.0 - sg)).astype(BF16)
        acc = None
        for s in range(n_sh):
            part = lax.dot_general(dz_ref[:, s * n:(s + 1) * n], w_ref[s], (((1,), (1,)), ((), ())),
                                   preferred_element_type=F32)
            acc = part if acc is None else acc + part
        dyg_ref[...] = acc.astype(BF16)

    row = lambda i: (i, 0)
    return pl.pallas_call(
        body, name="glu_bwd", grid=(n_rows // tm,),
        in_specs=[pl.BlockSpec((tm, D_MODEL), row), pl.BlockSpec((tm, 2 * D_MODEL), row), _w4_spec(w4)],
        out_specs=[pl.BlockSpec((tm, 2 * D_MODEL), row), pl.BlockSpec((tm, k), row)],
        out_shape=[SDS((n_rows, 2 * D_MODEL), BF16), SDS((n_rows, k), BF16)], compiler_params=_params(1),
    )(dh, z, w4)


def _loss_head(h, wn, target, n_ex, nb):
    n_rows = h.shape[0]
    per_tile = TM // BLOCK
    n_tiles = nb // per_tile

    def body(h_ref, wn_ref, *rest):
        t_refs, (dh_ref, loss_ref, dw_ref) = rest[:per_tile], rest[per_tile:]
        b, j = pl.program_id(0), pl.program_id(1)

        @pl.when((b == 0) & (j == 0))
        def _():
            loss_ref[...] = jnp.zeros_like(loss_ref)
            dw_ref[...] = jnp.zeros_like(dw_ref)

        def block(k):
            rows = slice(k * BLOCK, (k + 1) * BLOCK)
            hh = h_ref[rows, :]
            diff = _rms(hh, wn_ref[...]) - t_refs[k][...]
            loss_ref[...] += 0.5 * jnp.sum(diff * diff) * (1.0 / D_MODEL)
            dh, dw_rows = _rms_bwd(diff * (1.0 / D_MODEL), hh, wn_ref[...])
            dh_ref[rows, :] = dh
            dw_ref[...] += _fold8(dw_rows)

        @pl.when(j == 0)
        def _():
            dh_ref[0:BLOCK, :] = jnp.zeros((BLOCK, D_MODEL), F32)

        pl.when(j > 0)(lambda: block(0))
        for k in range(1, per_tile):
            block(k)

    def t_spec(k):
        return pl.BlockSpec((BLOCK, D_MODEL), lambda b, j: (b * (nb - 1) + jnp.maximum(per_tile * j + k - 1, 0), 0))

    tile = pl.BlockSpec((TM, D_MODEL), lambda b, j: (b * n_tiles + j, 0))
    return pl.pallas_call(
        body, name="loss_head", grid=(n_ex, n_tiles),
        in_specs=[tile, pl.BlockSpec((1, D_MODEL), lambda b, j: (0, 0))] + [t_spec(k) for k in range(per_tile)],
        out_specs=[tile, pl.BlockSpec((8, 128), lambda b, j: (0, 0)), pl.BlockSpec((8, D_MODEL), lambda b, j: (0, 0))],
        out_shape=[SDS((n_rows, D_MODEL), F32), SDS((8, 128), F32), SDS((8, D_MODEL), F32)],
        compiler_params=_params(2),
    )(h, wn, *([target] * per_tile))


def _adamw(pieces, w, m, v, name):
    n_layers = len(pieces)
    rows, cols = pieces[0].shape[1:]
    rb = rows
    for cand in (256, 136, 128, 64, 32, 16, 8):
        if rows % cand == 0 and rows > cand:
            rb = cand
            break
    n_blk = rows // rb
    c1 = 1.0 / (1.0 - ADAM_B1 ** ADAM_STEP)
    c2 = 1.0 / (1.0 - ADAM_B2 ** ADAM_STEP)

    def body(*refs):
        p_refs = refs[:n_layers]
        w_ref, m_ref, v_ref, g_out, d_out, m_out, v_out = refs[n_layers:]
        layer = pl.program_id(0)
        g = None
        for l, p_ref in enumerate(p_refs):
            gl = p_ref[0].astype(F32)
            for k in range(1, N_DEV):
                gl = gl + p_ref[k].astype(F32)
            g = gl if g is None else jnp.where(layer == l, gl, g)
        m_new = ADAM_B1 * m_ref[...] + (1.0 - ADAM_B1) * g
        v_new = ADAM_B2 * v_ref[...] + (1.0 - ADAM_B2) * (g * g)
        g_out[...] = g
        m_out[...] = m_new
        v_out[...] = v_new
        d_out[...] = -ADAM_LR * ((m_new * c1) / (jnp.sqrt(v_new * c2) + ADAM_EPS) + ADAM_WD * w_ref[...])

    def piece_spec(l):
        return pl.BlockSpec((N_DEV, rb, cols), lambda ly, i: (0, jnp.where(ly == l, i, 0), 0))

    blk = pl.BlockSpec((rb, cols), lambda ly, i: (ly * n_blk + i, 0))
    return pl.pallas_call(
        body, name=name, grid=(n_layers, n_blk),
        in_specs=[piece_spec(l) for l in range(n_layers)] + [blk, blk, blk],
        out_specs=[blk, blk, blk, blk],
        out_shape=[SDS((n_layers * rows, cols), F32)] * 4, compiler_params=_params(2),
    )(*pieces, w, m, v)


_HBM = pl.BlockSpec(memory_space=pltpu.HBM)
_SEM = pl.BlockSpec(memory_space=pltpu.SEMAPHORE)
_EFFECT = pltpu.SideEffectType.DATAFLOW_SIDE_EFFECTING
N_GATHER_PEERS = N_CHIPS - 1
N_EXCHANGE_PEERS = N_DEV - 1


def _gather_copies(srcs, lands, send_sems, recv_sems):
    x, y, c = lax.axis_index("x"), lax.axis_index("y"), lax.axis_index("c")
    mine = 2 * x + y
    chips = [(1 - x, y), (x, 1 - y), (1 - x, 1 - y)]
    out, inc = [], []
    for a in range(len(srcs)):
        for k, (px, py) in enumerate(chips):
            j = a * N_GATHER_PEERS + k
            sems = dict(send_sem=send_sems.at[j], recv_sem=recv_sems.at[j], device_id=(px, py, c),
                        device_id_type=pl.DeviceIdType.MESH)
            out.append(pltpu.make_async_remote_copy(src_ref=srcs[a], dst_ref=lands[a].at[mine], **sems))
            inc.append(pltpu.make_async_remote_copy(src_ref=srcs[a], dst_ref=lands[a].at[2 * px + py], **sems))
    return out, inc


def _exchange_copies(n_scatter):
    def copies(srcs, lands, send_sems, recv_sems):
        x, y, c = lax.axis_index("x"), lax.axis_index("y"), lax.axis_index("c")
        me = 4 * x + 2 * y + c
        peers = [(x ^ (k >> 2), y ^ ((k >> 1) & 1), c ^ (k & 1)) for k in range(1, N_DEV)]
        out, inc = [], []
        for a in range(len(srcs)):
            for k, (px, py, pc) in enumerate(peers):
                j = a * N_EXCHANGE_PEERS + k
                sems = dict(send_sem=send_sems.at[j], recv_sem=recv_sems.at[j], device_id=(px, py, pc),
                            device_id_type=pl.DeviceIdType.MESH)
                theirs = srcs[a].at[2 * px + py] if a < n_scatter else srcs[a]
                mine = srcs[a].at[2 * x + y] if a < n_scatter else srcs[a]
                out.append(pltpu.make_async_remote_copy(src_ref=theirs, dst_ref=lands[a].at[me], **sems))
                inc.append(pltpu.make_async_remote_copy(src_ref=mine, dst_ref=lands[a].at[4 * px + 2 * py + pc], **sems))
        return out, inc

    return copies


def _split_start(groups, copies_fn, n_peers, name):
    sizes = [len(srcs) for srcs, _ in groups]
    flat = [a for srcs, lands in groups for a in list(srcs) + list(lands)]
    n_flat, n_grp = len(flat), len(groups)

    def body(*refs):
        sems = refs[2 * n_flat:2 * n_flat + 2 * n_grp]
        token = refs[-1]
        at = 0
        for gi, n in enumerate(sizes):
            out, _ = copies_fn(refs[at:at + n], refs[at + n:at + 2 * n], sems[2 * gi], sems[2 * gi + 1])
            for cp in out:
                cp.start()
            at += 2 * n
        token[...] = jnp.zeros_like(token)

    sem_shapes = []
    for n in sizes:
        sem_shapes += [pltpu.SemaphoreType.DMA((n * n_peers,)), pltpu.SemaphoreType.DMA((n * n_peers,))]
    res = pl.pallas_call(
        body, name=name,
        out_shape=(*[pltpu.HBM(a.shape, a.dtype) for a in flat], *sem_shapes, SDS((8, 128), F32)),
        in_specs=[_HBM] * n_flat,
        out_specs=(*[_HBM] * n_flat, *[_SEM] * (2 * n_grp), pl.BlockSpec(memory_space=pltpu.VMEM)),
        input_output_aliases={i: i for i in range(n_flat)},
        compiler_params=pltpu.CompilerParams(has_side_effects=_EFFECT),
    )(*[pltpu.with_memory_space_constraint(a, pltpu.HBM) for a in flat])
    handles, at = [], 0
    for gi, n in enumerate(sizes):
        handles.append((res[n_flat + 2 * gi], res[n_flat + 2 * gi + 1], list(res[at:at + n]), list(res[at + n:at + 2 * n])))
        at += 2 * n
    return handles, res[-1]


def _split_wait(handle, after, copies_fn, name):
    send_sems, recv_sems, srcs, lands = handle
    n = len(srcs)
    after = list(after) if isinstance(after, (list, tuple)) else [after]

    def body(*refs):
        out, inc = copies_fn(refs[:n], refs[n:2 * n], refs[2 * n], refs[2 * n + 1])
        for cp in out:
            cp.wait_send()
        for cp in inc:
            cp.wait_recv()

    flat = list(srcs) + list(lands)
    res = pl.pallas_call(
        body, name=name,
        out_shape=tuple(pltpu.HBM(a.shape, a.dtype) for a in flat),
        in_specs=[_HBM] * (2 * n) + [_SEM, _SEM] + [pl.BlockSpec(memory_space=pl.ANY)] * len(after),
        out_specs=tuple([_HBM] * (2 * n)),
        input_output_aliases={i: i for i in range(2 * n)},
        compiler_params=pltpu.CompilerParams(has_side_effects=_EFFECT),
    )(*flat, send_sems, recv_sems, *after)
    return list(res[n:])


def _landing(own, slot, n_slots):
    return lax.dynamic_update_index_in_dim(lax.empty((n_slots,) + own.shape, own.dtype), own, slot, 0)


def _ssm_discretize(lam_re, lam_im, log_dt, b_re, b_im):
    lr = jnp.minimum(lam_re, LAMBDA_RE_MAX)
    li = lam_im
    dt = jnp.exp(log_dt)[:, None]
    mag = jnp.exp(lr * dt)
    ar, ai = mag * jnp.cos(li * dt), mag * jnp.sin(li * dt)
    den = lr * lr + li * li
    nr, ni = ar - 1.0, ai
    gr, gi = (nr * lr + ni * li) / den, (ni * lr - nr * li) / den
    bbr = gr[:, :, None] * b_re - gi[:, :, None] * b_im
    bbi = gr[:, :, None] * b_im + gi[:, :, None] * b_re
    return ar, ai, bbr, bbi


def _pair_lanes(t):
    return t.reshape(N_PAIR, 2 * SSM_STATE)


def _chan_state_blocks(t_gcp):
    t = t_gcp.reshape(N_PAIR, 2, SSM_GROUP, SSM_STATE)
    eye2 = jnp.eye(2, dtype=t.dtype)
    blk = jnp.einsum("rgcp,gh->rgchp", t, eye2).reshape(N_PAIR, 2 * SSM_GROUP, 2 * SSM_STATE)
    place = jax.nn.one_hot(jnp.arange(N_PAIR) % PAIRS_PER_CHUNK, PAIRS_PER_CHUNK, dtype=t.dtype)
    return jnp.einsum("rcl,rj->rjcl", blk, place).reshape(N_PAIR, 128, 2 * SSM_STATE)


def _chan_state_unblock(t):
    t = t.reshape(N_PAIR, PAIRS_PER_CHUNK, 2, SSM_GROUP, 2, SSM_STATE)
    place = jax.nn.one_hot(jnp.arange(N_PAIR) % PAIRS_PER_CHUNK, PAIRS_PER_CHUNK, dtype=t.dtype)
    t = jnp.einsum("rjgchp,rj->rgchp", t, place)
    t = jnp.einsum("rgchp,gh->rgcp", t, jnp.eye(2, dtype=t.dtype))
    return t.reshape(SSM_NG, SSM_GROUP, SSM_STATE)


def _scan_tables(zr, zi, reverse, seg):
    zr, zi = _pair_lanes(zr), _pair_lanes(-zi if reverse else zi)
    a = (jnp.exp(zr) * jnp.cos(zi), jnp.exp(zr) * jnp.sin(zi))
    cmul = lambda p, q: (p[0] * q[0] - p[1] * q[1], p[0] * q[1] + p[1] * q[0])
    big, square, bits = None, a, seg
    while bits:
        if bits & 1:
            big = square if big is None else cmul(big, square)
        square, bits = cmul(square, square), bits >> 1
    powers = [a, big]
    for _ in range(2):
        powers.append(cmul(powers[-1], powers[-1]))
    rows = jnp.arange(8)[None, :, None]
    tiles = [jnp.broadcast_to(part[:, None, :], (N_PAIR, 8, 128)) for part in powers[0]]
    for lvl, step in enumerate((1, 2, 4)):
        keep = (rows <= 7 - step) if reverse else (rows >= step)
        for part in powers[1 + lvl]:
            tiles.append(jnp.where(keep, part[:, None, :], 0.0))
    return jnp.stack(tiles, axis=1)


def _pairs_to_chunks(t):
    n_chunk = N_PAIR // PAIRS_PER_CHUNK
    return jnp.swapaxes(t.reshape(n_chunk, PAIRS_PER_CHUNK, 128, 256), 1, 2).reshape(n_chunk, 128, XW)


def _chunks_to_pairs(t):
    n_chunk = N_PAIR // PAIRS_PER_CHUNK
    return jnp.swapaxes(t.reshape(n_chunk, 128, PAIRS_PER_CHUNK, 256), 1, 2).reshape(N_PAIR, 128, 256)


def _ssm_operands(w, lp):
    seg = _row_tile(lp, SSM_TILES) // 8
    ar, ai, bbr, bbi = _ssm_discretize(w["ssm_lambda_re"], w["ssm_lambda_im"], w["ssm_log_dt"], w["ssm_b_re"], w["ssm_b_im"])
    b_blk = jnp.concatenate([_chan_state_blocks(jnp.swapaxes(bbr, 1, 2)), _chan_state_blocks(jnp.swapaxes(bbi, 1, 2))], axis=2)
    c_blk = jnp.concatenate([_chan_state_blocks(w["ssm_c_re"]), -_chan_state_blocks(w["ssm_c_im"])], axis=2)
    dt = jnp.exp(w["ssm_log_dt"])[:, None]
    zr, zi = jnp.minimum(w["ssm_lambda_re"], LAMBDA_RE_MAX) * dt, w["ssm_lambda_im"] * dt
    b_cat, c_cat = _pairs_to_chunks(b_blk).astype(BF16), _pairs_to_chunks(c_blk).astype(BF16)
    return (b_cat, jnp.swapaxes(b_cat, 1, 2), c_cat, jnp.swapaxes(c_cat, 1, 2),
            _scan_tables(zr, zi, False, seg), _scan_tables(zr, zi, True, seg))


def _local_step(x, target, w, late_weights, on_grads):
    n_ex, seq, _ = x.shape
    lp = seq + BLOCK
    nb = lp // BLOCK
    n_rows = n_ex * lp
    g = {}

    head = jnp.concatenate([jnp.zeros((PAD, D_MODEL), F32), w["meta_tokens"]], axis=0)
    h0 = jnp.concatenate([jnp.broadcast_to(head[None], (n_ex, BLOCK, D_MODEL)), x], axis=1).reshape(n_rows, D_MODEL)

    qkv, hn_a = _rms_mm_cols(h0, w["attn_norm_w"], w["attn_w_qkv"], "qkv_fwd")
    att, lse = _attn_fwd(qkv, w["attn_sinks"], n_ex, nb)
    h1 = _mm_acc(att, w["attn_w_o"], False, "attn_out_fwd", res=h0)
    w = {**w, **late_weights(0, att)}
    h2, a0, hn_m0, u = _mlp_fwd(h1, w["mlp_norm_w"][0:1], w["mlp_w_up"][0], w["mlp_w_down"][0], "mlp0_fwd",
                                next_norm=w["ssm_norm_w"])
    late = late_weights(1, h2)
    w["ssm_w_glu"] = late["ssm_w_glu"]
    w["mlp_w_up"], w["mlp_w_down"] = w["mlp_w_up"] + late["mlp_w_up"], w["mlp_w_down"] + late["mlp_w_down"]

    ops = w["ssm_operands"] if "ssm_operands" in w else _ssm_operands(w, lp)
    b_pad, bt_pad, ct_pad, c_pad, tab_fwd, tab_rev = ops
    yg, y, xs = _ssm_fwd(u, b_pad, c_pad, tab_fwd, w["ssm_d"], n_ex, lp)
    z = _mm_cols(yg, w["ssm_w_glu"], False, "glu_mm_fwd")
    h4, a1, hn_m1, h3 = _mlp_fwd(h2, w["mlp_norm_w"][1:2], w["mlp_w_up"][1], w["mlp_w_down"][1], "mlp1_fwd", glu_z=z)

    dh4, loss_tile, dnorm_f = _loss_head(h4, w["final_norm_w"], target.reshape(n_ex * seq, D_MODEL), n_ex, nb)

    def mlp_bwd(dh_out, h_in, a, hn, layer, tag, norm_w):
        dhn, dw_up, dw_down = None, None, None
        for s in range(N_CHIPS):
            final = s == N_CHIPS - 1
            res = _mlp_bwd_shard(s, dh_out, a, hn, dhn, h_in if final else None, norm_w,
                                 w["mlp_w_up"][layer], w["mlp_w_down"][layer], dw_up, dw_down, f"{tag}_bwd{s}")
            dhn, dw_up, dw_down = res[:3]
        return dhn, res[3], dw_up, dw_down

    dh3, dnorm_m1, dwu1, dwd1 = mlp_bwd(dh4, h3, a1, hn_m1, 1, "mlp1", w["mlp_norm_w"][1:2])
    tok = on_grads("mlp1", {"mlp_w_up": dwu1, "mlp_w_down": dwd1})
    dz, dyg = _glu_bwd(dh3, z, w["ssm_w_glu"])
    g["ssm_w_glu"] = _mm_tn(yg, dz, N_CHIPS, False, "glu_mm_dw")
    du, db_blk, dc_blk, da_t, dd_t = _ssm_bwd(dyg, y, u, xs, ct_pad, bt_pad, tab_rev, w["ssm_d"] + tok, n_ex, lp)
    dh2, dnorm_s = _rms_bwd_call(du, h2, w["ssm_norm_w"], dh3, "ssm_norm_bwd")
    db_blk, dc_blk = _chunks_to_pairs(db_blk), _chunks_to_pairs(dc_blk)
    g["ssm_c_re"] = _chan_state_unblock(dc_blk[:, :, 0:128])
    g["ssm_c_im"] = -_chan_state_unblock(dc_blk[:, :, 128:256])
    g_bbr = jnp.swapaxes(_chan_state_unblock(db_blk[:, :, 0:128]), 1, 2)
    g_bbi = jnp.swapaxes(_chan_state_unblock(db_blk[:, :, 128:256]), 1, 2)
    g_a = jnp.sum(da_t, axis=2).reshape(N_PAIR, 2, 2, SSM_STATE)
    g_ar, g_ai = g_a[:, 0].reshape(SSM_NG, SSM_STATE), g_a[:, 1].reshape(SSM_NG, SSM_STATE)
    _, vjp = jax.vjp(_ssm_discretize, w["ssm_lambda_re"], w["ssm_lambda_im"], w["ssm_log_dt"], w["ssm_b_re"], w["ssm_b_im"])
    g["ssm_lambda_re"], g["ssm_lambda_im"], g["ssm_log_dt"], g["ssm_b_re"], g["ssm_b_im"] = vjp((g_ar, g_ai, g_bbr, g_bbi))
    tok = on_grads("ssm", g)
    g = {}
    dh1, dnorm_m0, dwu0, dwd0 = mlp_bwd(dh2, h1, a0, hn_m0, 0, "mlp0", w["mlp_norm_w"][0:1] + tok)
    datt = _mm_cols(dh1, w["attn_w_o"], True, "attn_out_dx")
    dw_o = _mm_tn(att, dh1, N_CHIPS, True, "attn_out_dw")
    tok = on_grads("mlp0", {"mlp_w_up": dwu0, "mlp_w_down": dwd0, "attn_w_o": dw_o})
    dqkv, dsink_rows = _attn_bwd(qkv, w["attn_sinks"] + tok, att, lse, datt, n_ex, nb)
    tok = on_grads("qkv", {"attn_w_qkv": _mm_tn(hn_a, dqkv, N_CHIPS, False, "qkv_dw")})
    dh0, dnorm_a = _mm_acc(dqkv, w["attn_w_qkv"], True, "qkv_dx", rms_bwd=(h0, w["attn_norm_w"] + tok, dh1))

    dh0 = dh0.reshape(n_ex, lp, D_MODEL)
    on_grads("rest", {
        "mlp_norm_w": jnp.stack([jnp.sum(dnorm_m0, axis=0), jnp.sum(dnorm_m1, axis=0)]),
        "final_norm_w": jnp.sum(dnorm_f, axis=0),
        "attn_norm_w": jnp.sum(dnorm_a, axis=0)[None],
        "ssm_norm_w": jnp.sum(dnorm_s, axis=0)[None],
        "attn_sinks": jnp.sum(dsink_rows, axis=0)[None],
        "ssm_d": jnp.sum(dd_t, axis=0)[None],
        "meta_tokens": jnp.sum(dh0[:, PAD:BLOCK], axis=0),
        "loss": loss_tile[0, 0:1]})
    return loss_tile, dh0[:, BLOCK:]


_SHARDED_SMALL = ("meta_tokens", "ssm_norm_w", "ssm_d")
_REP_SSM = ("ssm_lambda_re", "ssm_lambda_im", "ssm_log_dt", "ssm_b_re", "ssm_b_im", "ssm_c_re", "ssm_c_im")
_REP_MISC = ("attn_norm_w", "attn_sinks", "mlp_norm_w", "final_norm_w")
_BIG = ("attn_w_qkv", "attn_w_o", "ssm_w_glu", "mlp_w_up", "mlp_w_down")


def _pack(parts, cols):
    flat = jnp.concatenate([p.reshape(-1) for p in parts])
    rows = -(-flat.shape[0] // (8 * cols)) * 8
    return jnp.pad(flat, (0, rows * cols - flat.shape[0])).reshape(rows, cols)


def _unpack(packed, like):
    flat = packed.reshape(-1)
    out, at = [], 0
    for p in like:
        out.append(flat[at:at + p.size].reshape(p.shape))
        at += p.size
    return out


def kernel(x, meta_tokens, attn_norm_w, attn_w_qkv, attn_sinks, attn_w_o, ssm_norm_w, ssm_lambda_re, ssm_lambda_im, ssm_log_dt, ssm_b_re, ssm_b_im, ssm_c_re, ssm_c_im, ssm_d, ssm_w_glu, mlp_norm_w, mlp_w_up, mlp_w_down, final_norm_w, loss_target, m_meta_tokens, m_attn_norm_w, m_attn_w_qkv, m_attn_sinks, m_attn_w_o, m_ssm_norm_w, m_ssm_lambda_re, m_ssm_lambda_im, m_ssm_log_dt, m_ssm_b_re, m_ssm_b_im, m_ssm_c_re, m_ssm_c_im, m_ssm_d, m_ssm_w_glu, m_mlp_norm_w, m_mlp_w_up, m_mlp_w_down, m_final_norm_w, v_meta_tokens, v_attn_norm_w, v_attn_w_qkv, v_attn_sinks, v_attn_w_o, v_ssm_norm_w, v_ssm_lambda_re, v_ssm_lambda_im, v_ssm_log_dt, v_ssm_b_re, v_ssm_b_im, v_ssm_c_re, v_ssm_c_im, v_ssm_d, v_ssm_w_glu, v_mlp_norm_w, v_mlp_w_up, v_mlp_w_down, v_final_norm_w):
    names = ("meta_tokens", "attn_norm_w", "attn_w_qkv", "attn_sinks", "attn_w_o", "ssm_norm_w", "ssm_lambda_re",
             "ssm_lambda_im", "ssm_log_dt", "ssm_b_re", "ssm_b_im", "ssm_c_re", "ssm_c_im", "ssm_d", "ssm_w_glu",
             "mlp_norm_w", "mlp_w_up", "mlp_w_down", "final_norm_w")
    wts = dict(zip(names, (meta_tokens, attn_norm_w, attn_w_qkv, attn_sinks, attn_w_o, ssm_norm_w, ssm_lambda_re,
                           ssm_lambda_im, ssm_log_dt, ssm_b_re, ssm_b_im, ssm_c_re, ssm_c_im, ssm_d, ssm_w_glu,
                           mlp_norm_w, mlp_w_up, mlp_w_down, final_norm_w)))
    mom = dict(zip(names, (m_meta_tokens, m_attn_norm_w, m_attn_w_qkv, m_attn_sinks, m_attn_w_o, m_ssm_norm_w,
                           m_ssm_lambda_re, m_ssm_lambda_im, m_ssm_log_dt, m_ssm_b_re, m_ssm_b_im, m_ssm_c_re,
                           m_ssm_c_im, m_ssm_d, m_ssm_w_glu, m_mlp_norm_w, m_mlp_w_up, m_mlp_w_down, m_final_norm_w)))
    var = dict(zip(names, (v_meta_tokens, v_attn_norm_w, v_attn_w_qkv, v_attn_sinks, v_attn_w_o, v_ssm_norm_w,
                           v_ssm_lambda_re, v_ssm_lambda_im, v_ssm_log_dt, v_ssm_b_re, v_ssm_b_im, v_ssm_c_re,
                           v_ssm_c_im, v_ssm_d, v_ssm_w_glu, v_mlp_norm_w, v_mlp_w_up, v_mlp_w_down, v_final_norm_w)))

    my_chip = 2 * lax.axis_index("x") + lax.axis_index("y")
    my_dev = 2 * my_chip + lax.axis_index("c")
    small_mine = _pack([wts[n] for n in _SHARDED_SMALL], 128)
    first = [attn_w_qkv.astype(BF16), attn_w_o.astype(BF16), small_mine]
    with_landing = lambda srcs: (srcs, [_landing(a, my_chip, N_CHIPS) for a in srcs])
    handles, token = _split_start([with_landing(first)], _gather_copies, N_GATHER_PEERS, "gather_start_first")
    up16, down16 = (mlp_w_up + token[0, 0]).astype(BF16), (mlp_w_down + token[0, 0]).astype(BF16)
    mlp0 = [up16[0:1], down16[0:1]]
    rest = [(ssm_w_glu + token[0, 0]).astype(BF16), up16[1:2], down16[1:2]]
    later, _ = _split_start([with_landing(mlp0), with_landing(rest)], _gather_copies, N_GATHER_PEERS, "gather_start_later")
    handles = handles + later
    full = {n: wts[n] for n in _REP_MISC}
    full["final_norm_w"] = final_norm_w[None]
    for n in _REP_SSM:
        full[n] = wts[n][0]
    full["ssm_operands"] = _ssm_operands(full, x.shape[1] + BLOCK)
    got = _split_wait(handles[0], full["ssm_operands"], _gather_copies, "gather_wait_first")
    full["attn_w_qkv"], full["attn_w_o"] = got[0], got[1]
    smalls = [_unpack(got[2][s], [wts[n] for n in _SHARDED_SMALL]) for s in range(N_CHIPS)]
    for k, n in enumerate(_SHARDED_SMALL):
        full[n] = jnp.concatenate([smalls[s][k] for s in range(N_CHIPS)], axis=1)

    def late_weights(stage, after):
        if stage == 0:
            up, down = _split_wait(handles[1], after, _gather_copies, "gather_wait_mlp0")
            return {"mlp_w_up": [up], "mlp_w_down": [down]}
        glu, up, down = _split_wait(handles[2], after, _gather_copies, "gather_wait_rest")
        return {"ssm_w_glu": glu, "mlp_w_up": [up], "mlp_w_down": [down]}

    def shard_cols(t):
        return jnp.swapaxes(t.reshape(t.shape[0], N_CHIPS, t.shape[1] // N_CHIPS), 0, 1)

    pending = {}

    def on_grads(tag, g):
        scatter = [g[n] for n in _BIG if n in g]
        whole = []
        if tag == "ssm":
            whole = [_pack([g[n] for n in _REP_SSM], D_MODEL)]
        if tag == "rest":
            parts = [shard_cols(g[n]) for n in _SHARDED_SMALL]
            scatter = [jnp.stack([_pack([p[s] for p in parts], 128) for s in range(N_CHIPS)])]
            whole = [_pack([g[n] for n in _REP_MISC] + [g["loss"]], D_MODEL)]
        srcs = scatter + whole
        lands = [_landing(lax.dynamic_index_in_dim(a, my_chip, 0, keepdims=False), my_dev, N_DEV) for a in scatter]
        lands += [_landing(a, my_dev, N_DEV) for a in whole]
        hs, token = _split_start([(srcs, lands)], _exchange_copies(len(scatter)), N_EXCHANGE_PEERS, "exchange_start_" + tag)
        pending[tag] = (hs[0], len(scatter))
        return token[0, 0]

    _, grad_x = _local_step(x, loss_target, full, late_weights, on_grads)

    recv = {}
    for tag, (handle, n_scatter) in pending.items():
        recv[tag] = _split_wait(handle, grad_x, _exchange_copies(n_scatter), "exchange_wait_" + tag)
    loss = jnp.sum(recv["rest"][1].reshape(N_DEV, -1)[:, sum(wts[n].size for n in _REP_MISC)])

    out = {}

    def update(tag, pieces, w2, m2, v2):
        return _adamw(pieces, w2, m2, v2, "adamw_" + tag)

    def update_weight(n, pieces):
        shp = wts[n].shape
        r2 = (math.prod(shp[:-1]), shp[-1])
        res = update(n, pieces, wts[n].reshape(r2), mom[n].reshape(r2), var[n].reshape(r2))
        out[n] = [t.reshape(shp) for t in res]

    update_weight("mlp_w_up", [recv["mlp0"][1], recv["mlp1"][0]])
    update_weight("mlp_w_down", [recv["mlp0"][2], recv["mlp1"][1]])
    update_weight("attn_w_o", [recv["mlp0"][0]])
    update_weight("ssm_w_glu", [recv["ssm"][0]])
    update_weight("attn_w_qkv", [recv["qkv"][0]])
    for tag, group, pieces, cols in (("small", _SHARDED_SMALL, recv["rest"][0], 128),
                                     ("rep_ssm", _REP_SSM, recv["ssm"][1], D_MODEL),
                                     ("rep_misc", _REP_MISC, recv["rest"][1], D_MODEL)):
        like = [wts[n] for n in group]
        res = update(tag, [pieces], _pack(like, cols), _pack([mom[n] for n in group], cols),
                     _pack([var[n] for n in group], cols))
        for k, n in enumerate(group):
            out[n] = [_unpack(t, like)[k] for t in res]

    return (loss, grad_x, *[out[n][0] for n in names], *[out[n][1] for n in names],
            *[out[n][2] for n in names], *[out[n][3] for n in names])
```

```python
import functools
import math

import jax
import jax.numpy as jnp
from jax import lax
from jax.experimental import pallas as pl
from jax.experimental.pallas import tpu as pltpu

F32 = jnp.float32
BF16 = jnp.bfloat16
SDS = jax.ShapeDtypeStruct

D_MODEL = 1024
N_HEADS = 16
N_KV = 4
GQA = N_HEADS // N_KV
HEAD_DIM = 64
BLOCK = 128
N_META = 16
PAD = BLOCK - N_META
QKV_DIM = (N_HEADS + 2 * N_KV) * HEAD_DIM
KV_DIM = 2 * N_KV * HEAD_DIM
D_FF = 4 * D_MODEL
N_CHIPS = 4
N_DEV = 8
SSM_GROUP = 16
SSM_NG = D_MODEL // SSM_GROUP
SSM_STATE = 64
N_PAIR = SSM_NG // 2
PAIRS_PER_CHUNK = 4
RMS_EPS = 1e-6
NEG_INF = -1e30
LAMBDA_RE_MAX = -1e-4
ADAM_LR, ADAM_B1, ADAM_B2, ADAM_EPS, ADAM_WD, ADAM_STEP = 0.001, 0.9, 0.999, 1e-08, 0.01, 10

TM = 384
MM_TILES = (1056, 768, 384)
MLP_FWD_TILES = (384,)
MLP_BWD_TILES = (768, 384)
TN_TILES = (1408, 768, 384)
VMEM_LIMIT = 56 * 1024 * 1024


def _params(n_grid):
    return pltpu.CompilerParams(dimension_semantics=("arbitrary",) * n_grid, vmem_limit_bytes=VMEM_LIMIT)


def _row_tile(n_rows, tiles):
    return next(t for t in tiles if n_rows % t == 0)


def _rms(h, w):
    r = lax.rsqrt(jnp.mean(h * h, axis=-1, keepdims=True) + RMS_EPS)
    return h * r * w


def _rms_bwd(dhn, h, w):
    r = lax.rsqrt(jnp.mean(h * h, axis=-1, keepdims=True) + RMS_EPS)
    g = dhn * w
    proj = jnp.sum(g * h, axis=-1, keepdims=True) * (1.0 / D_MODEL)
    return r * g - h * (r * r * r) * proj, dhn * h * r


def _fold8(t):
    return jnp.sum(t.reshape(t.shape[0] // 8, 8, t.shape[1]), axis=0)


def _gelu(y):
    return 0.5 * y * (1.0 + jnp.tanh(0.7978845608028654 * (y + 0.044715 * y * y * y)))


def _gelu_grad(y):
    t = jnp.tanh(0.7978845608028654 * (y + 0.044715 * y * y * y))
    return 0.5 * (1.0 + t) + 0.5 * y * (1.0 - t * t) * 0.7978845608028654 * (1.0 + 3.0 * 0.044715 * y * y)


def _w4_spec(w4):
    n_sh, _, k, n = w4.shape
    return pl.BlockSpec((n_sh, None, k, n), lambda i: (0, 0, 0, 0))


def _rms_mm_cols(h, wn, w4, name):
    n_rows = h.shape[0]
    n_sh, _, k, n = w4.shape
    tm = _row_tile(n_rows, MM_TILES)

    def body(h_ref, wn_ref, w_ref, o_ref, hn_ref):
        hn = _rms(h_ref[...], wn_ref[...]).astype(BF16)
        hn_ref[...] = hn
        for s in range(n_sh):
            o_ref[:, s * n:(s + 1) * n] = jnp.dot(hn, w_ref[s], preferred_element_type=F32).astype(o_ref.dtype)

    return pl.pallas_call(
        body, name=name, grid=(n_rows // tm,),
        in_specs=[pl.BlockSpec((tm, k), lambda i: (i, 0)), pl.BlockSpec((1, k), lambda i: (0, 0)), _w4_spec(w4)],
        out_specs=[pl.BlockSpec((tm, n_sh * n), lambda i: (i, 0)), pl.BlockSpec((tm, k), lambda i: (i, 0))],
        out_shape=[SDS((n_rows, n_sh * n), BF16), SDS((n_rows, k), BF16)],
        compiler_params=_params(1),
    )(h, wn, w4)


def _mm_cols(x, w4, trans_w, name):
    n_rows, kx = x.shape
    tm = _row_tile(n_rows, MM_TILES)
    n_sh, _, k, n = w4.shape
    n_out = k if trans_w else n
    dims = (((1,), (1,)), ((), ())) if trans_w else (((1,), (0,)), ((), ()))

    def body(x_ref, w_ref, o_ref):
        x16 = x_ref[...].astype(BF16)
        for s in range(n_sh):
            o_ref[:, s * n_out:(s + 1) * n_out] = lax.dot_general(
                x16, w_ref[s], dims, preferred_element_type=F32).astype(o_ref.dtype)

    return pl.pallas_call(
        body, name=name, grid=(n_rows // tm,),
        in_specs=[pl.BlockSpec((tm, kx), lambda i: (i, 0)), _w4_spec(w4)],
        out_specs=pl.BlockSpec((tm, n_sh * n_out), lambda i: (i, 0)),
        out_shape=SDS((n_rows, n_sh * n_out), BF16),
        compiler_params=_params(1),
    )(x, w4)


def _mm_acc(x, w4, trans_w, name, res=None, rms_bwd=None, out_dtype=F32):
    n_rows = x.shape[0]
    tm = _row_tile(n_rows, MM_TILES)
    n_sh, _, k, n = w4.shape
    kx, n_out = (n, k) if trans_w else (k, n)
    dims = (((1,), (1,)), ((), ())) if trans_w else (((1,), (0,)), ((), ()))

    def body(*refs):
        if rms_bwd is not None:
            x_ref, w_ref, h_ref, wn_ref, dres_ref, o_ref, dw_ref = refs
        elif res is not None:
            x_ref, w_ref, res_ref, o_ref = refs
        else:
            x_ref, w_ref, o_ref = refs
        acc = None
        for s in range(n_sh):
            part = lax.dot_general(x_ref[:, s * kx:(s + 1) * kx].astype(BF16), w_ref[s], dims, preferred_element_type=F32)
            acc = part if acc is None else acc + part
        if rms_bwd is not None:
            dh, dw_rows = _rms_bwd(acc, h_ref[...], wn_ref[...])
            o_ref[...] = (dres_ref[...] + dh).astype(o_ref.dtype)

            @pl.when(pl.program_id(0) == 0)
            def _():
                dw_ref[...] = jnp.zeros_like(dw_ref)

            dw_ref[...] += _fold8(dw_rows)
        elif res is not None:
            o_ref[...] = (res_ref[...] + acc).astype(o_ref.dtype)
        else:
            o_ref[...] = acc.astype(o_ref.dtype)

    row = lambda i: (i, 0)
    in_specs = [pl.BlockSpec((tm, n_sh * kx), row), _w4_spec(w4)]
    args = [x, w4]
    out_specs = pl.BlockSpec((tm, n_out), row)
    out_shape = SDS((n_rows, n_out), out_dtype)
    if rms_bwd is not None:
        h, wn, dres = rms_bwd
        in_specs += [pl.BlockSpec((tm, n_out), row), pl.BlockSpec((1, n_out), lambda i: (0, 0)),
                     pl.BlockSpec((tm, n_out), row)]
        args += [h, wn, dres]
        out_specs = [out_specs, pl.BlockSpec((8, n_out), lambda i: (0, 0))]
        out_shape = [out_shape, SDS((8, n_out), F32)]
    elif res is not None:
        in_specs.append(pl.BlockSpec((tm, n_out), row))
        args.append(res)
    return pl.pallas_call(
        body, name=name, grid=(n_rows // tm,), in_specs=in_specs, out_specs=out_specs, out_shape=out_shape,
        compiler_params=_params(1),
    )(*args)


def _mm_tn(a, b, n_sh, a_sharded, name):
    n_rows = a.shape[0]
    tm = _row_tile(n_rows, TN_TILES)
    ka = a.shape[1] // n_sh if a_sharded else a.shape[1]
    nb = b.shape[1] if a_sharded else b.shape[1] // n_sh
    n_i = n_rows // tm

    def body(a_ref, b_ref, o_ref, acc):
        i = pl.program_id(0)

        @pl.when(i == 0)
        def _():
            acc[...] = jnp.zeros_like(acc)

        for s in range(n_sh):
            a_s = a_ref[:, s * ka:(s + 1) * ka] if a_sharded else a_ref[...]
            b_s = b_ref[...] if a_sharded else b_ref[:, s * nb:(s + 1) * nb]
            acc[s] += lax.dot_general(a_s.astype(BF16), b_s.astype(BF16), (((0,), (0,)), ((), ())),
                                      preferred_element_type=F32)

        @pl.when(i == n_i - 1)
        def _():
            o_ref[...] = acc[...].astype(o_ref.dtype)

    return pl.pallas_call(
        body, name=name, grid=(n_i,),
        in_specs=[pl.BlockSpec((tm, a.shape[1]), lambda i: (i, 0)), pl.BlockSpec((tm, b.shape[1]), lambda i: (i, 0))],
        out_specs=pl.BlockSpec((n_sh, ka, nb), lambda i: (0, 0, 0)),
        out_shape=SDS((n_sh, ka, nb), BF16),
        scratch_shapes=[pltpu.VMEM((n_sh, ka, nb), F32)], compiler_params=_params(1),
    )(a, b)


def _mlp_fwd(h, wn, w_up4, w_down4, name, next_norm=None, glu_z=None):
    n_rows = h.shape[0]
    tm = _row_tile(n_rows, MLP_FWD_TILES)
    n_sh = w_up4.shape[0]
    f_sh = D_FF // n_sh
    w_down = w_down4.reshape(D_FF, D_MODEL)

    def body(*refs):
        refs = list(refs)
        h_ref, wn_ref, wu_ref, wd_ref = refs[:4]
        at = 4
        if next_norm is not None:
            nn_ref = refs[at]
            at += 1
        if glu_z is not None:
            z_ref = refs[at]
            at += 1
        o_ref, a_ref, hn_ref = refs[at:at + 3]
        at += 3
        if next_norm is not None:
            u_ref = refs[at]
            at += 1
        if glu_z is not None:
            hin_ref = refs[at]
            at += 1
        act_s = refs[at]
        h_in = h_ref[...]
        if glu_z is not None:
            h_in = h_in + z_ref[:, 0:D_MODEL].astype(F32) * jax.nn.sigmoid(z_ref[:, D_MODEL:2 * D_MODEL].astype(F32))
            hin_ref[...] = h_in
        hn = _rms(h_in, wn_ref[...]).astype(BF16)
        hn_ref[...] = hn
        for s in range(n_sh):
            cols = slice(s * f_sh, (s + 1) * f_sh)
            a = jnp.dot(hn, wu_ref[s], preferred_element_type=F32)
            a_ref[:, cols] = a.astype(BF16)
            act = jnp.maximum(a, 0.0)
            act_s[:, cols] = (act * act).astype(BF16)
        out = h_in + jnp.dot(act_s[...], wd_ref[...], preferred_element_type=F32)
        o_ref[...] = out
        if next_norm is not None:
            u_ref[...] = _rms(out, nn_ref[...])

    row = lambda i: (i, 0)
    vec = pl.BlockSpec((1, D_MODEL), lambda i: (0, 0))
    in_specs = [pl.BlockSpec((tm, D_MODEL), row), vec,
                pl.BlockSpec((n_sh, None, D_MODEL, f_sh), lambda i: (0, 0, 0, 0), pipeline_mode=pl.Buffered(1)),
                pl.BlockSpec((D_FF, D_MODEL), lambda i: (0, 0), pipeline_mode=pl.Buffered(1))]
    out_specs = [pl.BlockSpec((tm, D_MODEL), row), pl.BlockSpec((tm, D_FF), row), pl.BlockSpec((tm, D_MODEL), row)]
    out_shape = [SDS((n_rows, D_MODEL), F32), SDS((n_rows, D_FF), BF16), SDS((n_rows, D_MODEL), BF16)]
    args = [h, wn, w_up4, w_down]
    if next_norm is not None:
        in_specs.append(vec)
        args.append(next_norm)
    if glu_z is not None:
        in_specs.append(pl.BlockSpec((tm, 2 * D_MODEL), row))
        args.append(glu_z)
    for extra in (next_norm, glu_z):
        if extra is not None:
            out_specs.append(pl.BlockSpec((tm, D_MODEL), row))
            out_shape.append(SDS((n_rows, D_MODEL), F32))
    return pl.pallas_call(
        body, name=name, grid=(n_rows // tm,), in_specs=in_specs, out_specs=out_specs, out_shape=out_shape,
        scratch_shapes=[pltpu.VMEM((tm, D_FF), BF16)],
        compiler_params=_params(1),
    )(*args)


def _mlp_bwd_shard(s, dh, a, hn, dhn_prev, h, wn, w_up4, w_down4, dw_up_buf, dw_down_buf, name):
    n_rows = dh.shape[0]
    n_sh = w_up4.shape[0]
    f_sh = D_FF // n_sh
    tm = _row_tile(n_rows, MLP_BWD_TILES)
    n_i = n_rows // tm
    last = h is not None
    nt = (((1,), (1,)), ((), ()))
    tn = (((0,), (0,)), ((), ()))

    def body(*refs):
        refs = list(refs)
        dh_ref, a_ref, hn_ref, wu_ref, wd_ref = refs[:5]
        at = 5
        prev_ref = None
        if dhn_prev is not None:
            prev_ref = refs[at]
            at += 1
        if last:
            h_ref, wn_ref = refs[at:at + 2]
            at += 2
        if dw_up_buf is not None:
            at += 2
        o_ref, dwu_ref, dwd_ref = refs[at:at + 3]
        at += 3
        if last:
            dnorm_ref = refs[at]
            at += 1
        acc_u, acc_d = refs[at:at + 2]
        i = pl.program_id(0)

        @pl.when(i == 0)
        def _():
            acc_u[...] = jnp.zeros_like(acc_u)
            acc_d[...] = jnp.zeros_like(acc_d)
            if last:
                dnorm_ref[...] = jnp.zeros_like(dnorm_ref)

        dh16 = dh_ref[...].astype(BF16)
        r = jnp.maximum(a_ref[...].astype(F32), 0.0)
        dact = lax.dot_general(dh16, wd_ref[...], nt, preferred_element_type=F32)
        da16 = (dact * (2.0 * r)).astype(BF16)
        acc_d[...] += lax.dot_general((r * r).astype(BF16), dh16, tn, preferred_element_type=F32)
        acc_u[...] += lax.dot_general(hn_ref[...], da16, tn, preferred_element_type=F32)
        dhn = lax.dot_general(da16, wu_ref[...], nt, preferred_element_type=F32)
        if prev_ref is not None:
            dhn = dhn + prev_ref[...]
        if last:
            d_rms, dw_rows = _rms_bwd(dhn, h_ref[...], wn_ref[...])
            o_ref[...] = dh_ref[...] + d_rms
            dnorm_ref[...] += _fold8(dw_rows)
        else:
            o_ref[...] = dhn

        @pl.when(i == n_i - 1)
        def _():
            dwu_ref[...] = acc_u[...].astype(BF16)
            dwd_ref[...] = acc_d[...].astype(BF16)

    row = lambda i: (i, 0)
    tile = pl.BlockSpec((tm, D_MODEL), row)
    in_specs = [tile, pl.BlockSpec((tm, f_sh), lambda i: (i, s)), tile,
                pl.BlockSpec((None, None, D_MODEL, f_sh), lambda i: (s, 0, 0, 0)),
                pl.BlockSpec((None, None, f_sh, D_MODEL), lambda i: (s, 0, 0, 0))]
    args = [dh, a, hn, w_up4, w_down4]
    if dhn_prev is not None:
        in_specs.append(tile)
        args.append(dhn_prev)
    if last:
        in_specs += [tile, pl.BlockSpec((1, D_MODEL), lambda i: (0, 0))]
        args += [h, wn]
    aliases = {}
    if dw_up_buf is not None:
        aliases = {len(args): 1, len(args) + 1: 2}
        in_specs += [pl.BlockSpec(memory_space=pl.ANY)] * 2
        args += [dw_up_buf, dw_down_buf]
    out_specs = [tile, pl.BlockSpec((None, D_MODEL, f_sh), lambda i: (s, 0, 0)),
                 pl.BlockSpec((None, f_sh, D_MODEL), lambda i: (s, 0, 0))]
    out_shape = [SDS((n_rows, D_MODEL), F32), SDS((n_sh, D_MODEL, f_sh), BF16), SDS((n_sh, f_sh, D_MODEL), BF16)]
    if last:
        out_specs.append(pl.BlockSpec((8, D_MODEL), lambda i: (0, 0)))
        out_shape.append(SDS((8, D_MODEL), F32))
    return pl.pallas_call(
        body, name=name, grid=(n_i,), in_specs=in_specs, out_specs=out_specs, out_shape=out_shape,
        input_output_aliases=aliases,
        scratch_shapes=[pltpu.VMEM((D_MODEL, f_sh), F32), pltpu.VMEM((f_sh, D_MODEL), F32)],
        compiler_params=_params(1),
    )(*args)


def _attn_masks(n):
    qi = lax.broadcasted_iota(jnp.int32, (BLOCK, 3 * BLOCK), 0)
    col = lax.broadcasted_iota(jnp.int32, (BLOCK, 3 * BLOCK), 1)
    kj = col - BLOCK
    dist = BLOCK + qi - kj
    kmin = jnp.where(n == 0, 2 * BLOCK, jnp.where(n == 1, BLOCK, 0))
    band_ok = (col >= BLOCK) & (dist >= 0) & (dist < BLOCK) & (kj >= kmin)
    q_pos = n * BLOCK + qi - PAD
    meta_ok = (col >= PAD) & (col < BLOCK) & (col - PAD <= q_pos)
    distf = jnp.where(col >= BLOCK, dist, 0).astype(F32)
    return band_ok | meta_ok, distf


def _alibi_slope(h):
    return float(2.0 ** (-8.0 * (h + 1) / N_HEADS))


def _attn_bias(n, bias_s):
    ok, distf = _attn_masks(n)
    for h in range(N_HEADS):
        bias_s[h] = jnp.where(ok, -_alibi_slope(h) * distf, NEG_INF)


def _attn_fwd(qkv, sinks, n_ex, nb):
    n_rows = qkv.shape[0]
    kvb = N_HEADS * HEAD_DIM // KV_DIM

    def body(sink_ref, q_ref, kvm_ref, kvp_ref, kvc_ref, o_ref, lse_ref, k_s, v_s, q_s, bias_s):
        n = pl.program_id(1)

        @pl.when(n <= 2)
        def _():
            _attn_bias(n, bias_s)

        v_s[...] = jnp.ones_like(v_s)
        for part, ref in enumerate((kvm_ref, kvp_ref, kvc_ref)):
            rows = slice(part * BLOCK, (part + 1) * BLOCK)
            k_s[rows, :] = ref[:, 0:N_KV * HEAD_DIM]
            for kv in range(N_KV):
                v_s[rows, kv * 2 * HEAD_DIM:kv * 2 * HEAD_DIM + HEAD_DIM] = \
                    ref[:, (N_KV + kv) * HEAD_DIM:(N_KV + kv + 1) * HEAD_DIM]
        def scores(kv):
            for g in range(GQA):
                h = kv * GQA + g
                q_s[kv, g * BLOCK:(g + 1) * BLOCK, :] = q_ref[:, h * HEAD_DIM:(h + 1) * HEAD_DIM] * (HEAD_DIM ** -0.5)
            return lax.dot_general(q_s[kv], k_s[:, kv * HEAD_DIM:(kv + 1) * HEAD_DIM], (((1,), (1,)), ((), ())),
                                   preferred_element_type=F32)

        ahead = scores(0)
        for kv in range(N_KV):
            s4 = ahead
            if kv + 1 < N_KV:
                ahead = scores(kv + 1)
            es, ms, sink_es = [], [], []
            for g in range(GQA):
                h = kv * GQA + g
                s = s4[g * BLOCK:(g + 1) * BLOCK] + bias_s[h]
                sink = sink_ref[0, h]
                m = jnp.maximum(jnp.max(s, axis=-1, keepdims=True), sink)
                es.append(jnp.exp(s - m).astype(BF16))
                ms.append(m)
                sink_es.append(jnp.exp(sink - m))
            pv = jnp.dot(jnp.concatenate(es, axis=0), v_s[:, kv * 2 * HEAD_DIM:(kv + 1) * 2 * HEAD_DIM],
                         preferred_element_type=F32)
            for g in range(GQA):
                h = kv * GQA + g
                pg = pv[g * BLOCK:(g + 1) * BLOCK]
                l = pg[:, HEAD_DIM:HEAD_DIM + 1] + sink_es[g]
                o_ref[:, h * HEAD_DIM:(h + 1) * HEAD_DIM] = (pg[:, 0:HEAD_DIM] * (1.0 / l)).astype(BF16)
                lse_ref[:, h:h + 1] = ms[g] + jnp.log(l)

    return pl.pallas_call(
        body, name="attn_fwd", grid=(n_ex, nb),
        in_specs=[pl.BlockSpec(memory_space=pltpu.SMEM),
                  pl.BlockSpec((BLOCK, N_HEADS * HEAD_DIM), lambda b, n: (b * nb + n, 0)),
                  pl.BlockSpec((BLOCK, KV_DIM), lambda b, n: (b * nb, kvb)),
                  pl.BlockSpec((BLOCK, KV_DIM), lambda b, n: (b * nb + jnp.maximum(n - 1, 0), kvb)),
                  pl.BlockSpec((BLOCK, KV_DIM), lambda b, n: (b * nb + n, kvb))],
        out_specs=[pl.BlockSpec((BLOCK, N_HEADS * HEAD_DIM), lambda b, n: (b * nb + n, 0)),
                   pl.BlockSpec((BLOCK, N_HEADS), lambda b, n: (b * nb + n, 0))],
        out_shape=[SDS((n_rows, N_HEADS * HEAD_DIM), BF16), SDS((n_rows, N_HEADS), F32)],
        scratch_shapes=[pltpu.VMEM((3 * BLOCK, N_KV * HEAD_DIM), BF16), pltpu.VMEM((3 * BLOCK, 2 * N_KV * HEAD_DIM), BF16),
                        pltpu.VMEM((N_KV, GQA * BLOCK, HEAD_DIM), BF16), pltpu.VMEM((N_HEADS, BLOCK, 3 * BLOCK), F32)],
        compiler_params=_params(2),
    )(sinks, qkv, qkv, qkv, qkv)


def _attn_bwd(qkv, sinks, o, lse, do, n_ex, nb):
    n_rows = qkv.shape[0]
    kvb = N_HEADS * HEAD_DIM // KV_DIM
    scale = HEAD_DIM ** -0.5
    nq = lambda r: nb - 1 - r

    def body(sink_ref, q_ref, kvm_ref, kvp_ref, kvc_ref, o_ref, lse_ref, do_ref, dqkv_ref, dsink_ref,
             k_s, v_s, dkv_s, carry_s, meta_s, q_s, do_s, bias_s):
        b, r = pl.program_id(0), pl.program_id(1)
        n = nq(r)

        @pl.when((r == 0) | (n <= 1))
        def _():
            _attn_bias(n, bias_s)

        @pl.when((b == 0) & (r == 0))
        def _():
            dsink_ref[...] = jnp.zeros_like(dsink_ref)

        @pl.when(r == 0)
        def _():
            carry_s[...] = jnp.zeros_like(carry_s)
            meta_s[...] = jnp.zeros_like(meta_s)

        for part, ref in enumerate((kvm_ref, kvp_ref, kvc_ref)):
            k_s[part * BLOCK:(part + 1) * BLOCK, :] = ref[:, 0:N_KV * HEAD_DIM]
            v_s[part * BLOCK:(part + 1) * BLOCK, :] = ref[:, N_KV * HEAD_DIM:KV_DIM]
        nt = (((1,), (1,)), ((), ()))
        tn = (((0,), (0,)), ((), ()))
        for kv in range(N_KV):
            kcols = slice(kv * HEAD_DIM, (kv + 1) * HEAD_DIM)
            vcols = slice(N_KV * HEAD_DIM + kv * HEAD_DIM, N_KV * HEAD_DIM + (kv + 1) * HEAD_DIM)
            for g in range(GQA):
                cols = slice((kv * GQA + g) * HEAD_DIM, (kv * GQA + g + 1) * HEAD_DIM)
                q_s[kv, g * BLOCK:(g + 1) * BLOCK, :] = q_ref[:, cols] * scale
                do_s[kv, g * BLOCK:(g + 1) * BLOCK, :] = do_ref[:, cols]
            kh, vh = k_s[:, kcols], v_s[:, kcols]
            s4 = lax.dot_general(q_s[kv], kh, nt, preferred_element_type=F32)
            dp4 = lax.dot_general(do_s[kv], vh, nt, preferred_element_type=F32)
            ps, dss = [], []
            for g in range(GQA):
                h = kv * GQA + g
                cols = slice(h * HEAD_DIM, (h + 1) * HEAD_DIM)
                rows = slice(g * BLOCK, (g + 1) * BLOCK)
                s = s4[rows] + bias_s[h]
                lse_h = lse_ref[:, h:h + 1]
                p = jnp.exp(s - lse_h)
                delta = jnp.sum(do_ref[:, cols].astype(F32) * o_ref[:, cols].astype(F32), axis=-1, keepdims=True)
                dsink_ref[:, h:h + 1] += -jnp.exp(sink_ref[0, h] - lse_h) * delta
                ps.append(p.astype(BF16))
                dss.append((p * (dp4[rows] - delta)).astype(BF16))
            p4, ds4 = jnp.concatenate(ps, axis=0), jnp.concatenate(dss, axis=0)
            dq4 = jnp.dot(ds4, kh, preferred_element_type=F32) * scale
            for g in range(GQA):
                cols = slice((kv * GQA + g) * HEAD_DIM, (kv * GQA + g + 1) * HEAD_DIM)
                dqkv_ref[:, cols] = dq4[g * BLOCK:(g + 1) * BLOCK].astype(BF16)
            dkv_s[:, kcols] = lax.dot_general(ds4, q_s[kv], tn, preferred_element_type=F32)
            dkv_s[:, vcols] = lax.dot_general(p4, do_s[kv], tn, preferred_element_type=F32)

        meta_s[...] += dkv_s[0:BLOCK, :]
        cur = dkv_s[2 * BLOCK:3 * BLOCK, :] + carry_s[...]
        carry_s[...] = dkv_s[BLOCK:2 * BLOCK, :]

        @pl.when(n > 0)
        def _():
            dqkv_ref[:, N_HEADS * HEAD_DIM:QKV_DIM] = cur.astype(BF16)

        @pl.when(n == 0)
        def _():
            dqkv_ref[:, N_HEADS * HEAD_DIM:QKV_DIM] = (cur + meta_s[...]).astype(BF16)

    blk = lambda b, r: (b * nb + nq(r), 0)
    return pl.pallas_call(
        body, name="attn_bwd", grid=(n_ex, nb),
        in_specs=[pl.BlockSpec(memory_space=pltpu.SMEM),
                  pl.BlockSpec((BLOCK, N_HEADS * HEAD_DIM), blk),
                  pl.BlockSpec((BLOCK, KV_DIM), lambda b, r: (b * nb, kvb)),
                  pl.BlockSpec((BLOCK, KV_DIM), lambda b, r: (b * nb + jnp.maximum(nq(r) - 1, 0), kvb)),
                  pl.BlockSpec((BLOCK, KV_DIM), lambda b, r: (b * nb + nq(r), kvb)),
                  pl.BlockSpec((BLOCK, N_HEADS * HEAD_DIM), blk),
                  pl.BlockSpec((BLOCK, N_HEADS), blk),
                  pl.BlockSpec((BLOCK, N_HEADS * HEAD_DIM), blk)],
        out_specs=[pl.BlockSpec((BLOCK, QKV_DIM), blk),
                   pl.BlockSpec((BLOCK, N_HEADS), lambda b, r: (0, 0))],
        out_shape=[SDS((n_rows, QKV_DIM), BF16), SDS((BLOCK, N_HEADS), F32)],
        scratch_shapes=[pltpu.VMEM((3 * BLOCK, N_KV * HEAD_DIM), BF16), pltpu.VMEM((3 * BLOCK, N_KV * HEAD_DIM), BF16),
                        pltpu.VMEM((3 * BLOCK, KV_DIM), F32), pltpu.VMEM((BLOCK, KV_DIM), F32),
                        pltpu.VMEM((BLOCK, KV_DIM), F32), pltpu.VMEM((N_KV, GQA * BLOCK, HEAD_DIM), BF16),
                        pltpu.VMEM((N_KV, GQA * BLOCK, HEAD_DIM), BF16), pltpu.VMEM((N_HEADS, BLOCK, 3 * BLOCK), F32)],
        compiler_params=_params(2),
    )(sinks, qkv, qkv, qkv, qkv, o, lse, do)


SSM_TILES = (1408, 384)
XW = 256 * PAIRS_PER_CHUNK


def _cmul_add(xr, xi, mr, mi, sr, si):
    return xr + mr * sr - mi * si, xi + mr * si + mi * sr


def _to_segments(src_ref, dst, seg):
    for s in range(seg):
        dst[s * 8:(s + 1) * 8, :] = src_ref[pl.ds(s, 8, stride=seg), :]


def _from_segments(src, i, seg):
    return src[pl.ds(i, seg, stride=8), :]


def _scan_segments(buf, tab_ref, carry_s, seg, reverse):
    shifts = (7, 6, 4) if reverse else (1, 2, 4)
    row_id = lax.broadcasted_iota(jnp.int32, (8, 128), 0)
    a_tiles = [tab_ref[j, c] for j in range(PAIRS_PER_CHUNK) for c in (0, 1)]

    def local(si, prev):
        s = (seg - 1 - si) if reverse else si
        row = pl.multiple_of(s * 8, 8)
        out = []
        for j in range(PAIRS_PER_CHUNK):
            re, im = slice(256 * j, 256 * j + 128), slice(256 * j + 128, 256 * j + 256)
            xr, xi = _cmul_add(buf[pl.ds(row, 8), re], buf[pl.ds(row, 8), im],
                               a_tiles[2 * j], a_tiles[2 * j + 1], prev[2 * j], prev[2 * j + 1])
            buf[pl.ds(row, 8), re] = xr
            buf[pl.ds(row, 8), im] = xi
            out += [xr, xi]
        return tuple(out)

    zero = jnp.zeros((8, 128), F32)
    edge = lax.fori_loop(0, seg, local, (zero,) * (2 * PAIRS_PER_CHUNK))

    entering = []
    for j in range(PAIRS_PER_CHUNK):
        er, ei = edge[2 * j], edge[2 * j + 1]
        if reverse:
            sr = jnp.where(row_id == 7, carry_s[2 * j], pltpu.roll(er, 7, 0))
            si_ = jnp.where(row_id == 7, carry_s[2 * j + 1], pltpu.roll(ei, 7, 0))
        else:
            sr = jnp.where(row_id == 0, carry_s[2 * j], pltpu.roll(er, 1, 0))
            si_ = jnp.where(row_id == 0, carry_s[2 * j + 1], pltpu.roll(ei, 1, 0))
        for lvl, sh in enumerate(shifts):
            sr, si_ = _cmul_add(sr, si_, tab_ref[j, 2 + 2 * lvl], tab_ref[j, 3 + 2 * lvl],
                                pltpu.roll(sr, sh, 0), pltpu.roll(si_, sh, 0))
        entering += [sr, si_]
        tr, ti = _cmul_add(er, ei, tab_ref[j, 2], tab_ref[j, 3], sr, si_)
        out_row = slice(0, 1) if reverse else slice(7, 8)
        carry_s[2 * j] = jnp.broadcast_to(tr[out_row], (8, 128))
        carry_s[2 * j + 1] = jnp.broadcast_to(ti[out_row], (8, 128))

    def fix(si, carried):
        s = (seg - 1 - si) if reverse else si
        row = pl.multiple_of(s * 8, 8)
        out = []
        for j in range(PAIRS_PER_CHUNK):
            re, im = slice(256 * j, 256 * j + 128), slice(256 * j + 128, 256 * j + 256)
            ar, ai, fr, fi = a_tiles[2 * j], a_tiles[2 * j + 1], carried[2 * j], carried[2 * j + 1]
            fr, fi = ar * fr - ai * fi, ar * fi + ai * fr
            buf[pl.ds(row, 8), re] += fr
            buf[pl.ds(row, 8), im] += fi
            out += [fr, fi]
        return tuple(out)

    lax.fori_loop(0, seg, fix, tuple(entering))


def _ssm_fwd(u, b_pad, c_pad, tab, d_skip, n_ex, lp):
    n_rows = u.shape[0]
    TM = _row_tile(lp, SSM_TILES)
    SEG = TM // 8
    n_t = lp // TM
    n_chunk = D_MODEL // 128

    def body(u_ref, bp_ref, cp_ref, tab_ref, d_ref, yg_ref, y_ref, xs_ref, up_ref, buf, carry_s, us, ys):
        @pl.when(pl.program_id(2) == 0)
        def _():
            carry_s[...] = jnp.zeros_like(carry_s)

        _to_segments(u_ref, us, SEG)
        ub = us[...]
        u16 = ub.astype(BF16)
        up_ref[...] = u16
        buf[...] = jnp.dot(u16, bp_ref[...], preferred_element_type=F32)
        _scan_segments(buf, tab_ref, carry_s, SEG, reverse=False)
        xb = buf[...].astype(BF16)
        xs_ref[...] = xb
        ys[...] = d_ref[...] * ub + jnp.dot(xb, cp_ref[...], preferred_element_type=F32)
        for i in range(8):
            yi = _from_segments(ys, i, SEG)
            y_ref[i * SEG:(i + 1) * SEG, :] = yi
            yg_ref[i * SEG:(i + 1) * SEG, :] = _gelu(yi).astype(BF16)

    rows = lambda b, q, t: (b * n_t + t, q)
    return pl.pallas_call(
        body, name="ssm_fwd", grid=(n_ex, n_chunk, n_t),
        in_specs=[pl.BlockSpec((TM, 128), rows),
                  pl.BlockSpec((None, 128, XW), lambda b, q, t: (q, 0, 0)),
                  pl.BlockSpec((None, XW, 128), lambda b, q, t: (q, 0, 0)),
                  pl.BlockSpec((PAIRS_PER_CHUNK, 8, 8, 128), lambda b, q, t: (q, 0, 0, 0)),
                  pl.BlockSpec((1, 128), lambda b, q, t: (0, q))],
        out_specs=[pl.BlockSpec((TM, 128), rows), pl.BlockSpec((TM, 128), rows),
                   pl.BlockSpec((None, TM, XW), lambda b, q, t: (q, b * n_t + t, 0)), pl.BlockSpec((TM, 128), rows)],
        out_shape=[SDS((n_rows, D_MODEL), BF16), SDS((n_rows, D_MODEL), F32), SDS((n_chunk, n_rows, XW), BF16),
                   SDS((n_rows, D_MODEL), BF16)],
        scratch_shapes=[pltpu.VMEM((TM, XW), F32), pltpu.VMEM((2 * PAIRS_PER_CHUNK, 8, 128), F32),
                        pltpu.VMEM((TM, 128), F32), pltpu.VMEM((TM, 128), F32)],
        compiler_params=_params(3),
    )(u, b_pad, c_pad, tab, d_skip)


def _ssm_bwd(dyg, y, u_seg, xs, ct_pad, bt_pad, tab_rev, d_skip, n_ex, lp):
    n_rows = u_seg.shape[0]
    TM = _row_tile(lp, SSM_TILES)
    SEG = TM // 8
    n_t = lp // TM
    n_chunk = D_MODEL // 128
    tile = lambda q, b, t: (b * n_t + (n_t - 1 - t), q)

    def body(dyg_ref, y_ref, up_ref, xs_ref, xp_ref, ct_ref, bt_ref, tab_ref, d_ref,
             du_ref, db_ref, dc_ref, da_ref, dd_ref, buf, xf, carry_s, dys, dyp):
        b, t = pl.program_id(1), pl.program_id(2)

        @pl.when((b == 0) & (t == 0))
        def _():
            db_ref[...] = jnp.zeros_like(db_ref)
            dc_ref[...] = jnp.zeros_like(dc_ref)
            da_ref[...] = jnp.zeros_like(da_ref)
            dd_ref[...] = jnp.zeros_like(dd_ref)

        @pl.when(t == 0)
        def _():
            carry_s[...] = jnp.zeros_like(carry_s)

        dys[...] = dyg_ref[...].astype(F32) * _gelu_grad(y_ref[...])
        _to_segments(dys, dyp, SEG)
        dy = dyp[...]
        u16 = up_ref[...]
        dd_ref[...] += _fold8(dy * u16.astype(F32))
        dy16 = dy.astype(BF16)
        first_tile = t == n_t - 1
        tn = (((0,), (0,)), ((), ()))
        buf[...] = jnp.dot(dy16, ct_ref[...], preferred_element_type=F32)
        dc_ref[...] += lax.dot_general(dy16, xs_ref[...], tn, preferred_element_type=F32)
        xf[16:16 + TM, :] = xs_ref[...].astype(F32)
        xf[0:16, :] = jnp.where(first_tile, 0.0, xp_ref[...].astype(F32))
        _scan_segments(buf, tab_ref, carry_s, SEG, reverse=True)
        g16 = buf[...].astype(BF16)
        dys[...] = d_ref[...] * dy + jnp.dot(g16, bt_ref[...], preferred_element_type=F32)
        db_ref[...] += lax.dot_general(u16, g16, tn, preferred_element_type=F32)
        row_id = lax.broadcasted_iota(jnp.int32, (8, 128), 0)
        for j in range(PAIRS_PER_CHUNK):
            re, im = slice(256 * j, 256 * j + 128), slice(256 * j + 128, 256 * j + 256)
            first = [jnp.where(row_id == 0, jnp.broadcast_to(xf[15:16, c], (8, 128)),
                               pltpu.roll(xf[8 + TM:16 + TM, c], 1, 0)) for c in (re, im)]
            for rows, pr, pi in ((slice(0, 8), first[0], first[1]),
                                 (slice(8, TM), xf[16:8 + TM, re], xf[16:8 + TM, im])):
                gr, gi = buf[rows, re], buf[rows, im]
                da_ref[j, 0] += _fold8(gr * pr + gi * pi)
                da_ref[j, 1] += _fold8(gi * pr - gr * pi)
        for i in range(8):
            du_ref[i * SEG:(i + 1) * SEG, :] = _from_segments(dys, i, SEG)

    prev16 = lambda q, b, t: (q, jnp.maximum((b * n_t + (n_t - 1 - t)) * (TM // 16) - 1, 0), 0)
    return pl.pallas_call(
        body, name="ssm_bwd", grid=(n_chunk, n_ex, n_t),
        in_specs=[pl.BlockSpec((TM, 128), tile), pl.BlockSpec((TM, 128), tile), pl.BlockSpec((TM, 128), tile),
                  pl.BlockSpec((None, TM, XW), lambda q, b, t: (q, b * n_t + (n_t - 1 - t), 0)),
                  pl.BlockSpec((None, 16, XW), prev16),
                  pl.BlockSpec((None, 128, XW), lambda q, b, t: (q, 0, 0)),
                  pl.BlockSpec((None, XW, 128), lambda q, b, t: (q, 0, 0)),
                  pl.BlockSpec((PAIRS_PER_CHUNK, 8, 8, 128), lambda q, b, t: (q, 0, 0, 0)),
                  pl.BlockSpec((1, 128), lambda q, b, t: (0, q))],
        out_specs=[pl.BlockSpec((TM, 128), tile),
                   pl.BlockSpec((None, 128, XW), lambda q, b, t: (q, 0, 0)),
                   pl.BlockSpec((None, 128, XW), lambda q, b, t: (q, 0, 0)),
                   pl.BlockSpec((PAIRS_PER_CHUNK, 2, 8, 128), lambda q, b, t: (q, 0, 0, 0)),
                   pl.BlockSpec((8, 128), lambda q, b, t: (0, q))],
        out_shape=[SDS((n_rows, D_MODEL), F32), SDS((n_chunk, 128, XW), F32), SDS((n_chunk, 128, XW), F32),
                   SDS((N_PAIR, 2, 8, 128), F32), SDS((8, D_MODEL), F32)],
        scratch_shapes=[pltpu.VMEM((TM, XW), F32), pltpu.VMEM((TM + 16, XW), F32),
                        pltpu.VMEM((2 * PAIRS_PER_CHUNK, 8, 128), F32), pltpu.VMEM((TM, 128), F32),
                        pltpu.VMEM((TM, 128), F32)],
        compiler_params=_params(3),
    )(dyg, y, u_seg, xs, xs, ct_pad, bt_pad, tab_rev, d_skip)


def _rms_bwd_call(dhn, h, wn, dres, name):
    n_rows = h.shape[0]

    def body(dhn_ref, h_ref, wn_ref, dres_ref, o_ref, dw_ref):
        @pl.when(pl.program_id(0) == 0)
        def _():
            dw_ref[...] = jnp.zeros_like(dw_ref)

        dh, dw_rows = _rms_bwd(dhn_ref[...], h_ref[...], wn_ref[...])
        o_ref[...] = dres_ref[...] + dh
        dw_ref[...] += _fold8(dw_rows)

    row = lambda i: (i, 0)
    return pl.pallas_call(
        body, name=name, grid=(n_rows // TM,),
        in_specs=[pl.BlockSpec((TM, D_MODEL), row), pl.BlockSpec((TM, D_MODEL), row),
                  pl.BlockSpec((1, D_MODEL), lambda i: (0, 0)), pl.BlockSpec((TM, D_MODEL), row)],
        out_specs=[pl.BlockSpec((TM, D_MODEL), row), pl.BlockSpec((8, D_MODEL), lambda i: (0, 0))],
        out_shape=[SDS((n_rows, D_MODEL), F32), SDS((8, D_MODEL), F32)], compiler_params=_params(1),
    )(dhn, h, wn, dres)


def _glu_bwd(dh, z, w4):
    n_rows = dh.shape[0]
    tm = _row_tile(n_rows, MM_TILES)
    n_sh, _, k, n = w4.shape

    def body(dh_ref, z_ref, w_ref, dz_ref, dyg_ref):
        sg = jax.nn.sigmoid(z_ref[:, D_MODEL:2 * D_MODEL].astype(F32))
        d = dh_ref[...]
        dz_ref[:, 0:D_MODEL] = (d * sg).astype(BF16)
        dz_ref[:, D_MODEL:2 * D_MODEL] = (d * z_ref[:, 0:D_MODEL].astype(F32) * sg * (1.0 - sg)).astype(BF16)
        acc = None
        for s in range(n_sh):
            part = lax.dot_general(dz_ref[:, s * n:(s + 1) * n], w_ref[s], (((1,), (1,)), ((), ())),
                                   preferred_element_type=F32)
            acc = part if acc is None else acc + part
        dyg_ref[...] = acc.astype(BF16)

    row = lambda i: (i, 0)
    return pl.pallas_call(
        body, name="glu_bwd", grid=(n_rows // tm,),
        in_specs=[pl.BlockSpec((tm, D_MODEL), row), pl.BlockSpec((tm, 2 * D_MODEL), row), _w4_spec(w4)],
        out_specs=[pl.BlockSpec((tm, 2 * D_MODEL), row), pl.BlockSpec((tm, k), row)],
        out_shape=[SDS((n_rows, 2 * D_MODEL), BF16), SDS((n_rows, k), BF16)], compiler_params=_params(1),
    )(dh, z, w4)


def _loss_head(h, wn, target, n_ex, nb):
    n_rows = h.shape[0]
    per_tile = TM // BLOCK
    n_tiles = nb // per_tile

    def body(h_ref, wn_ref, *rest):
        t_refs, (dh_ref, loss_ref, dw_ref) = rest[:per_tile], rest[per_tile:]
        b, j = pl.program_id(0), pl.program_id(1)

        @pl.when((b == 0) & (j == 0))
        def _():
            loss_ref[...] = jnp.zeros_like(loss_ref)
            dw_ref[...] = jnp.zeros_like(dw_ref)

        def block(k):
            rows = slice(k * BLOCK, (k + 1) * BLOCK)
            hh = h_ref[rows, :]
            diff = _rms(hh, wn_ref[...]) - t_refs[k][...]
            loss_ref[...] += 0.5 * jnp.sum(diff * diff) * (1.0 / D_MODEL)
            dh, dw_rows = _rms_bwd(diff * (1.0 / D_MODEL), hh, wn_ref[...])
            dh_ref[rows, :] = dh
            dw_ref[...] += _fold8(dw_rows)

        @pl.when(j == 0)
        def _():
            dh_ref[0:BLOCK, :] = jnp.zeros((BLOCK, D_MODEL), F32)

        pl.when(j > 0)(lambda: block(0))
        for k in range(1, per_tile):
            block(k)

    def t_spec(k):
        return pl.BlockSpec((BLOCK, D_MODEL), lambda b, j: (b * (nb - 1) + jnp.maximum(per_tile * j + k - 1, 0), 0))

    tile = pl.BlockSpec((TM, D_MODEL), lambda b, j: (b * n_tiles + j, 0))
    return pl.pallas_call(
        body, name="loss_head", grid=(n_ex, n_tiles),
        in_specs=[tile, pl.BlockSpec((1, D_MODEL), lambda b, j: (0, 0))] + [t_spec(k) for k in range(per_tile)],
        out_specs=[tile, pl.BlockSpec((8, 128), lambda b, j: (0, 0)), pl.BlockSpec((8, D_MODEL), lambda b, j: (0, 0))],
        out_shape=[SDS((n_rows, D_MODEL), F32), SDS((8, 128), F32), SDS((8, D_MODEL), F32)],
        compiler_params=_params(2),
    )(h, wn, *([target] * per_tile))


def _adamw(pieces, w, m, v, name):
    n_layers = len(pieces)
    rows, cols = pieces[0].shape[1:]
    rb = rows
    for cand in (256, 136, 128, 64, 32, 16, 8):
        if rows % cand == 0 and rows > cand:
            rb = cand
            break
    n_blk = rows // rb
    c1 = 1.0 / (1.0 - ADAM_B1 ** ADAM_STEP)
    c2 = 1.0 / (1.0 - ADAM_B2 ** ADAM_STEP)

    def body(*refs):
        p_refs = refs[:n_layers]
        w_ref, m_ref, v_ref, g_out, d_out, m_out, v_out = refs[n_layers:]
        layer = pl.program_id(0)
        g = None
        for l, p_ref in enumerate(p_refs):
            gl = p_ref[0].astype(F32)
            for k in range(1, N_DEV):
                gl = gl + p_ref[k].astype(F32)
            g = gl if g is None else jnp.where(layer == l, gl, g)
        m_new = ADAM_B1 * m_ref[...] + (1.0 - ADAM_B1) * g
        v_new = ADAM_B2 * v_ref[...] + (1.0 - ADAM_B2) * (g * g)
        g_out[...] = g
        m_out[...] = m_new
        v_out[...] = v_new
        d_out[...] = -ADAM_LR * ((m_new * c1) / (jnp.sqrt(v_new * c2) + ADAM_EPS) + ADAM_WD * w_ref[...])

    def piece_spec(l):
        return pl.BlockSpec((N_DEV, rb, cols), lambda ly, i: (0, jnp.where(ly == l, i, 0), 0))

    blk = pl.BlockSpec((rb, cols), lambda ly, i: (ly * n_blk + i, 0))
    return pl.pallas_call(
        body, name=name, grid=(n_layers, n_blk),
        in_specs=[piece_spec(l) for l in range(n_layers)] + [blk, blk, blk],
        out_specs=[blk, blk, blk, blk],
        out_shape=[SDS((n_layers * rows, cols), F32)] * 4, compiler_params=_params(2),
    )(*pieces, w, m, v)


_HBM = pl.BlockSpec(memory_space=pltpu.HBM)
_SEM = pl.BlockSpec(memory_space=pltpu.SEMAPHORE)
_EFFECT = pltpu.SideEffectType.DATAFLOW_SIDE_EFFECTING
N_GATHER_PEERS = N_CHIPS - 1
N_EXCHANGE_PEERS = N_DEV - 1


def _gather_copies(srcs, lands, send_sems, recv_sems):
    x, y, c = lax.axis_index("x"), lax.axis_index("y"), lax.axis_index("c")
    mine = 2 * x + y
    chips = [(1 - x, y), (x, 1 - y), (1 - x, 1 - y)]
    out, inc = [], []
    for a in range(len(srcs)):
        for k, (px, py) in enumerate(chips):
            j = a * N_GATHER_PEERS + k
            sems = dict(send_sem=send_sems.at[j], recv_sem=recv_sems.at[j], device_id=(px, py, c),
                        device_id_type=pl.DeviceIdType.MESH)
            out.append(pltpu.make_async_remote_copy(src_ref=srcs[a], dst_ref=lands[a].at[mine], **sems))
            inc.append(pltpu.make_async_remote_copy(src_ref=srcs[a], dst_ref=lands[a].at[2 * px + py], **sems))
    return out, inc


def _exchange_copies(n_scatter):
    def copies(srcs, lands, send_sems, recv_sems):
        x, y, c = lax.axis_index("x"), lax.axis_index("y"), lax.axis_index("c")
        me = 4 * x + 2 * y + c
        peers = [(x ^ (k >> 2), y ^ ((k >> 1) & 1), c ^ (k & 1)) for k in range(1, N_DEV)]
        out, inc = [], []
        for a in range(len(srcs)):
            for k, (px, py, pc) in enumerate(peers):
                j = a * N_EXCHANGE_PEERS + k
                sems = dict(send_sem=send_sems.at[j], recv_sem=recv_sems.at[j], device_id=(px, py, pc),
                            device_id_type=pl.DeviceIdType.MESH)
                theirs = srcs[a].at[2 * px + py] if a < n_scatter else srcs[a]
                mine = srcs[a].at[2 * x + y] if a < n_scatter else srcs[a]
                out.append(pltpu.make_async_remote_copy(src_ref=theirs, dst_ref=lands[a].at[me], **sems))
                inc.append(pltpu.make_async_remote_copy(src_ref=mine, dst_ref=lands[a].at[4 * px + 2 * py + pc], **sems))
        return out, inc

    return copies


def _split_start(groups, copies_fn, n_peers, name):
    sizes = [len(srcs) for srcs, _ in groups]
    flat = [a for srcs, lands in groups for a in list(srcs) + list(lands)]
    n_flat, n_grp = len(flat), len(groups)

    def body(*refs):
        sems = refs[2 * n_flat:2 * n_flat + 2 * n_grp]
        token = refs[-1]
        at = 0
        for gi, n in enumerate(sizes):
            out, _ = copies_fn(refs[at:at + n], refs[at + n:at + 2 * n], sems[2 * gi], sems[2 * gi + 1])
            for cp in out:
                cp.start()
            at += 2 * n
        token[...] = jnp.zeros_like(token)

    sem_shapes = []
    for n in sizes:
        sem_shapes += [pltpu.SemaphoreType.DMA((n * n_peers,)), pltpu.SemaphoreType.DMA((n * n_peers,))]
    res = pl.pallas_call(
        body, name=name,
        out_shape=(*[pltpu.HBM(a.shape, a.dtype) for a in flat], *sem_shapes, SDS((8, 128), F32)),
        in_specs=[_HBM] * n_flat,
        out_specs=(*[_HBM] * n_flat, *[_SEM] * (2 * n_grp), pl.BlockSpec(memory_space=pltpu.VMEM)),
        input_output_aliases={i: i for i in range(n_flat)},
        compiler_params=pltpu.CompilerParams(has_side_effects=_EFFECT),
    )(*[pltpu.with_memory_space_constraint(a, pltpu.HBM) for a in flat])
    handles, at = [], 0
    for gi, n in enumerate(sizes):
        handles.append((res[n_flat + 2 * gi], res[n_flat + 2 * gi + 1], list(res[at:at + n]), list(res[at + n:at + 2 * n])))
        at += 2 * n
    return handles, res[-1]


def _split_wait(handle, after, copies_fn, name):
    send_sems, recv_sems, srcs, lands = handle
    n = len(srcs)
    after = list(after) if isinstance(after, (list, tuple)) else [after]

    def body(*refs):
        out, inc = copies_fn(refs[:n], refs[n:2 * n], refs[2 * n], refs[2 * n + 1])
        for cp in out:
            cp.wait_send()
        for cp in inc:
            cp.wait_recv()

    flat = list(srcs) + list(lands)
    res = pl.pallas_call(
        body, name=name,
        out_shape=tuple(pltpu.HBM(a.shape, a.dtype) for a in flat),
        in_specs=[_HBM] * (2 * n) + [_SEM, _SEM] + [pl.BlockSpec(memory_space=pl.ANY)] * len(after),
        out_specs=tuple([_HBM] * (2 * n)),
        input_output_aliases={i: i for i in range(2 * n)},
        compiler_params=pltpu.CompilerParams(has_side_effects=_EFFECT),
    )(*flat, send_sems, recv_sems, *after)
    return list(res[n:])


def _landing(own, slot, n_slots):
    return lax.dynamic_update_index_in_dim(lax.empty((n_slots,) + own.shape, own.dtype), own, slot, 0)


def _ssm_discretize(lam_re, lam_im, log_dt, b_re, b_im):
    lr = jnp.minimum(lam_re, LAMBDA_RE_MAX)
    li = lam_im
    dt = jnp.exp(log_dt)[:, None]
    mag = jnp.exp(lr * dt)
    ar, ai = mag * jnp.cos(li * dt), mag * jnp.sin(li * dt)
    den = lr * lr + li * li
    nr, ni = ar - 1.0, ai
    gr, gi = (nr * lr + ni * li) / den, (ni * lr - nr * li) / den
    bbr = gr[:, :, None] * b_re - gi[:, :, None] * b_im
    bbi = gr[:, :, None] * b_im + gi[:, :, None] * b_re
    return ar, ai, bbr, bbi


def _pair_lanes(t):
    return t.reshape(N_PAIR, 2 * SSM_STATE)


def _chan_state_blocks(t_gcp):
    t = t_gcp.reshape(N_PAIR, 2, SSM_GROUP, SSM_STATE)
    eye2 = jnp.eye(2, dtype=t.dtype)
    blk = jnp.einsum("rgcp,gh->rgchp", t, eye2).reshape(N_PAIR, 2 * SSM_GROUP, 2 * SSM_STATE)
    place = jax.nn.one_hot(jnp.arange(N_PAIR) % PAIRS_PER_CHUNK, PAIRS_PER_CHUNK, dtype=t.dtype)
    return jnp.einsum("rcl,rj->rjcl", blk, place).reshape(N_PAIR, 128, 2 * SSM_STATE)


def _chan_state_unblock(t):
    t = t.reshape(N_PAIR, PAIRS_PER_CHUNK, 2, SSM_GROUP, 2, SSM_STATE)
    place = jax.nn.one_hot(jnp.arange(N_PAIR) % PAIRS_PER_CHUNK, PAIRS_PER_CHUNK, dtype=t.dtype)
    t = jnp.einsum("rjgchp,rj->rgchp", t, place)
    t = jnp.einsum("rgchp,gh->rgcp", t, jnp.eye(2, dtype=t.dtype))
    return t.reshape(SSM_NG, SSM_GROUP, SSM_STATE)


def _scan_tables(zr, zi, reverse, seg):
    zr, zi = _pair_lanes(zr), _pair_lanes(-zi if reverse else zi)
    a = (jnp.exp(zr) * jnp.cos(zi), jnp.exp(zr) * jnp.sin(zi))
    cmul = lambda p, q: (p[0] * q[0] - p[1] * q[1], p[0] * q[1] + p[1] * q[0])
    big, square, bits = None, a, seg
    while bits:
        if bits & 1:
            big = square if big is None else cmul(big, square)
        square, bits = cmul(square, square), bits >> 1
    powers = [a, big]
    for _ in range(2):
        powers.append(cmul(powers[-1], powers[-1]))
    rows = jnp.arange(8)[None, :, None]
    tiles = [jnp.broadcast_to(part[:, None, :], (N_PAIR, 8, 128)) for part in powers[0]]
    for lvl, step in enumerate((1, 2, 4)):
        keep = (rows <= 7 - step) if reverse else (rows >= step)
        for part in powers[1 + lvl]:
            tiles.append(jnp.where(keep, part[:, None, :], 0.0))
    return jnp.stack(tiles, axis=1)


def _pairs_to_chunks(t):
    n_chunk = N_PAIR // PAIRS_PER_CHUNK
    return jnp.swapaxes(t.reshape(n_chunk, PAIRS_PER_CHUNK, 128, 256), 1, 2).reshape(n_chunk, 128, XW)


def _chunks_to_pairs(t):
    n_chunk = N_PAIR // PAIRS_PER_CHUNK
    return jnp.swapaxes(t.reshape(n_chunk, 128, PAIRS_PER_CHUNK, 256), 1, 2).reshape(N_PAIR, 128, 256)


def _ssm_operands(w, lp):
    seg = _row_tile(lp, SSM_TILES) // 8
    ar, ai, bbr, bbi = _ssm_discretize(w["ssm_lambda_re"], w["ssm_lambda_im"], w["ssm_log_dt"], w["ssm_b_re"], w["ssm_b_im"])
    b_blk = jnp.concatenate([_chan_state_blocks(jnp.swapaxes(bbr, 1, 2)), _chan_state_blocks(jnp.swapaxes(bbi, 1, 2))], axis=2)
    c_blk = jnp.concatenate([_chan_state_blocks(w["ssm_c_re"]), -_chan_state_blocks(w["ssm_c_im"])], axis=2)
    dt = jnp.exp(w["ssm_log_dt"])[:, None]
    zr, zi = jnp.minimum(w["ssm_lambda_re"], LAMBDA_RE_MAX) * dt, w["ssm_lambda_im"] * dt
    b_cat, c_cat = _pairs_to_chunks(b_blk).astype(BF16), _pairs_to_chunks(c_blk).astype(BF16)
    return (b_cat, jnp.swapaxes(b_cat, 1, 2), c_cat, jnp.swapaxes(c_cat, 1, 2),
            _scan_tables(zr, zi, False, seg), _scan_tables(zr, zi, True, seg))


def _local_step(x, target, w, late_weights, on_grads):
    n_ex, seq, _ = x.shape
    lp = seq + BLOCK
    nb = lp // BLOCK
    n_rows = n_ex * lp
    g = {}

    head = jnp.concatenate([jnp.zeros((PAD, D_MODEL), F32), w["meta_tokens"]], axis=0)
    h0 = jnp.concatenate([jnp.broadcast_to(head[None], (n_ex, BLOCK, D_MODEL)), x], axis=1).reshape(n_rows, D_MODEL)

    qkv, hn_a = _rms_mm_cols(h0, w["attn_norm_w"], w["attn_w_qkv"], "qkv_fwd")
    att, lse = _attn_fwd(qkv, w["attn_sinks"], n_ex, nb)
    h1 = _mm_acc(att, w["attn_w_o"], False, "attn_out_fwd", res=h0)
    w = {**w, **late_weights(0, att)}
    h2, a0, hn_m0, u = _mlp_fwd(h1, w["mlp_norm_w"][0:1], w["mlp_w_up"][0], w["mlp_w_down"][0], "mlp0_fwd",
                                next_norm=w["ssm_norm_w"])
    late = late_weights(1, h2)
    w["ssm_w_glu"] = late["ssm_w_glu"]
    w["mlp_w_up"], w["mlp_w_down"] = w["mlp_w_up"] + late["mlp_w_up"], w["mlp_w_down"] + late["mlp_w_down"]

    ops = w["ssm_operands"] if "ssm_operands" in w else _ssm_operands(w, lp)
    b_pad, bt_pad, ct_pad, c_pad, tab_fwd, tab_rev = ops
    yg, y, xs, u_seg = _ssm_fwd(u, b_pad, c_pad, tab_fwd, w["ssm_d"], n_ex, lp)
    z = _mm_cols(yg, w["ssm_w_glu"], False, "glu_mm_fwd")
    h4, a1, hn_m1, h3 = _mlp_fwd(h2, w["mlp_norm_w"][1:2], w["mlp_w_up"][1], w["mlp_w_down"][1], "mlp1_fwd", glu_z=z)

    dh4, loss_tile, dnorm_f = _loss_head(h4, w["final_norm_w"], target.reshape(n_ex * seq, D_MODEL), n_ex, nb)

    def mlp_bwd(dh_out, h_in, a, hn, layer, tag, norm_w):
        dhn, dw_up, dw_down = None, None, None
        for s in range(N_CHIPS):
            final = s == N_CHIPS - 1
            res = _mlp_bwd_shard(s, dh_out, a, hn, dhn, h_in if final else None, norm_w,
                                 w["mlp_w_up"][layer], w["mlp_w_down"][layer], dw_up, dw_down, f"{tag}_bwd{s}")
            dhn, dw_up, dw_down = res[:3]
        return dhn, res[3], dw_up, dw_down

    dh3, dnorm_m1, dwu1, dwd1 = mlp_bwd(dh4, h3, a1, hn_m1, 1, "mlp1", w["mlp_norm_w"][1:2])
    tok = on_grads("mlp1", {"mlp_w_up": dwu1, "mlp_w_down": dwd1})
    dz, dyg = _glu_bwd(dh3, z, w["ssm_w_glu"])
    g["ssm_w_glu"] = _mm_tn(yg, dz, N_CHIPS, False, "glu_mm_dw")
    du, db_blk, dc_blk, da_t, dd_t = _ssm_bwd(dyg, y, u_seg, xs, ct_pad, bt_pad, tab_rev, w["ssm_d"] + tok, n_ex, lp)
    dh2, dnorm_s = _rms_bwd_call(du, h2, w["ssm_norm_w"], dh3, "ssm_norm_bwd")
    db_blk, dc_blk = _chunks_to_pairs(db_blk), _chunks_to_pairs(dc_blk)
    g["ssm_c_re"] = _chan_state_unblock(dc_blk[:, :, 0:128])
    g["ssm_c_im"] = -_chan_state_unblock(dc_blk[:, :, 128:256])
    g_bbr = jnp.swapaxes(_chan_state_unblock(db_blk[:, :, 0:128]), 1, 2)
    g_bbi = jnp.swapaxes(_chan_state_unblock(db_blk[:, :, 128:256]), 1, 2)
    g_a = jnp.sum(da_t, axis=2).reshape(N_PAIR, 2, 2, SSM_STATE)
    g_ar, g_ai = g_a[:, 0].reshape(SSM_NG, SSM_STATE), g_a[:, 1].reshape(SSM_NG, SSM_STATE)
    _, vjp = jax.vjp(_ssm_discretize, w["ssm_lambda_re"], w["ssm_lambda_im"], w["ssm_log_dt"], w["ssm_b_re"], w["ssm_b_im"])
    g["ssm_lambda_re"], g["ssm_lambda_im"], g["ssm_log_dt"], g["ssm_b_re"], g["ssm_b_im"] = vjp((g_ar, g_ai, g_bbr, g_bbi))
    tok = on_grads("ssm", g)
    g = {}
    dh1, dnorm_m0, dwu0, dwd0 = mlp_bwd(dh2, h1, a0, hn_m0, 0, "mlp0", w["mlp_norm_w"][0:1] + tok)
    datt = _mm_cols(dh1, w["attn_w_o"], True, "attn_out_dx")
    dw_o = _mm_tn(att, dh1, N_CHIPS, True, "attn_out_dw")
    tok = on_grads("mlp0", {"mlp_w_up": dwu0, "mlp_w_down": dwd0, "attn_w_o": dw_o})
    dqkv, dsink_rows = _attn_bwd(qkv, w["attn_sinks"] + tok, att, lse, datt, n_ex, nb)
    tok = on_grads("qkv", {"attn_w_qkv": _mm_tn(hn_a, dqkv, N_CHIPS, False, "qkv_dw")})
    dh0, dnorm_a = _mm_acc(dqkv, w["attn_w_qkv"], True, "qkv_dx", rms_bwd=(h0, w["attn_norm_w"] + tok, dh1))

    dh0 = dh0.reshape(n_ex, lp, D_MODEL)
    on_grads("rest", {
        "mlp_norm_w": jnp.stack([jnp.sum(dnorm_m0, axis=0), jnp.sum(dnorm_m1, axis=0)]),
        "final_norm_w": jnp.sum(dnorm_f, axis=0),
        "attn_norm_w": jnp.sum(dnorm_a, axis=0)[None],
        "ssm_norm_w": jnp.sum(dnorm_s, axis=0)[None],
        "attn_sinks": jnp.sum(dsink_rows, axis=0)[None],
        "ssm_d": jnp.sum(dd_t, axis=0)[None],
        "meta_tokens": jnp.sum(dh0[:, PAD:BLOCK], axis=0),
        "loss": loss_tile[0, 0:1]})
    return loss_tile, dh0[:, BLOCK:]


_SHARDED_SMALL = ("meta_tokens", "ssm_norm_w", "ssm_d")
_REP_SSM = ("ssm_lambda_re", "ssm_lambda_im", "ssm_log_dt", "ssm_b_re", "ssm_b_im", "ssm_c_re", "ssm_c_im")
_REP_MISC = ("attn_norm_w", "attn_sinks", "mlp_norm_w", "final_norm_w")
_BIG = ("attn_w_qkv", "attn_w_o", "ssm_w_glu", "mlp_w_up", "mlp_w_down")


def _pack(parts, cols):
    flat = jnp.concatenate([p.reshape(-1) for p in parts])
    rows = -(-flat.shape[0] // (8 * cols)) * 8
    return jnp.pad(flat, (0, rows * cols - flat.shape[0])).reshape(rows, cols)


def _unpack(packed, like):
    flat = packed.reshape(-1)
    out, at = [], 0
    for p in like:
        out.append(flat[at:at + p.size].reshape(p.shape))
        at += p.size
    return out


def kernel(x, meta_tokens, attn_norm_w, attn_w_qkv, attn_sinks, attn_w_o, ssm_norm_w, ssm_lambda_re, ssm_lambda_im, ssm_log_dt, ssm_b_re, ssm_b_im, ssm_c_re, ssm_c_im, ssm_d, ssm_w_glu, mlp_norm_w, mlp_w_up, mlp_w_down, final_norm_w, loss_target, m_meta_tokens, m_attn_norm_w, m_attn_w_qkv, m_attn_sinks, m_attn_w_o, m_ssm_norm_w, m_ssm_lambda_re, m_ssm_lambda_im, m_ssm_log_dt, m_ssm_b_re, m_ssm_b_im, m_ssm_c_re, m_ssm_c_im, m_ssm_d, m_ssm_w_glu, m_mlp_norm_w, m_mlp_w_up, m_mlp_w_down, m_final_norm_w, v_meta_tokens, v_attn_norm_w, v_attn_w_qkv, v_attn_sinks, v_attn_w_o, v_ssm_norm_w, v_ssm_lambda_re, v_ssm_lambda_im, v_ssm_log_dt, v_ssm_b_re, v_ssm_b_im, v_ssm_c_re, v_ssm_c_im, v_ssm_d, v_ssm_w_glu, v_mlp_norm_w, v_mlp_w_up, v_mlp_w_down, v_final_norm_w):
    names = ("meta_tokens", "attn_norm_w", "attn_w_qkv", "attn_sinks", "attn_w_o", "ssm_norm_w", "ssm_lambda_re",
             "ssm_lambda_im", "ssm_log_dt", "ssm_b_re", "ssm_b_im", "ssm_c_re", "ssm_c_im", "ssm_d", "ssm_w_glu",
             "mlp_norm_w", "mlp_w_up", "mlp_w_down", "final_norm_w")
    wts = dict(zip(names, (meta_tokens, attn_norm_w, attn_w_qkv, attn_sinks, attn_w_o, ssm_norm_w, ssm_lambda_re,
                           ssm_lambda_im, ssm_log_dt, ssm_b_re, ssm_b_im, ssm_c_re, ssm_c_im, ssm_d, ssm_w_glu,
                           mlp_norm_w, mlp_w_up, mlp_w_down, final_norm_w)))
    mom = dict(zip(names, (m_meta_tokens, m_attn_norm_w, m_attn_w_qkv, m_attn_sinks, m_attn_w_o, m_ssm_norm_w,
                           m_ssm_lambda_re, m_ssm_lambda_im, m_ssm_log_dt, m_ssm_b_re, m_ssm_b_im, m_ssm_c_re,
                           m_ssm_c_im, m_ssm_d, m_ssm_w_glu, m_mlp_norm_w, m_mlp_w_up, m_mlp_w_down, m_final_norm_w)))
    var = dict(zip(names, (v_meta_tokens, v_attn_norm_w, v_attn_w_qkv, v_attn_sinks, v_attn_w_o, v_ssm_norm_w,
                           v_ssm_lambda_re, v_ssm_lambda_im, v_ssm_log_dt, v_ssm_b_re, v_ssm_b_im, v_ssm_c_re,
                           v_ssm_c_im, v_ssm_d, v_ssm_w_glu, v_mlp_norm_w, v_mlp_w_up, v_mlp_w_down, v_final_norm_w)))

    my_chip = 2 * lax.axis_index("x") + lax.axis_index("y")
    my_dev = 2 * my_chip + lax.axis_index("c")
    small_mine = _pack([wts[n] for n in _SHARDED_SMALL], 128)
    first = [attn_w_qkv.astype(BF16), attn_w_o.astype(BF16), small_mine]
    with_landing = lambda srcs: (srcs, [_landing(a, my_chip, N_CHIPS) for a in srcs])
    handles, token = _split_start([with_landing(first)], _gather_copies, N_GATHER_PEERS, "gather_start_first")
    up16, down16 = (mlp_w_up + token[0, 0]).astype(BF16), (mlp_w_down + token[0, 0]).astype(BF16)
    mlp0 = [up16[0:1], down16[0:1]]
    rest = [(ssm_w_glu + token[0, 0]).astype(BF16), up16[1:2], down16[1:2]]
    later, _ = _split_start([with_landing(mlp0), with_landing(rest)], _gather_copies, N_GATHER_PEERS, "gather_start_later")
    handles = handles + later
    full = {n: wts[n] for n in _REP_MISC}
    full["final_norm_w"] = final_norm_w[None]
    for n in _REP_SSM:
        full[n] = wts[n][0]
    full["ssm_operands"] = _ssm_operands(full, x.shape[1] + BLOCK)
    got = _split_wait(handles[0], full["ssm_operands"], _gather_copies, "gather_wait_first")
    full["attn_w_qkv"], full["attn_w_o"] = got[0], got[1]
    smalls = [_unpack(got[2][s], [wts[n] for n in _SHARDED_SMALL]) for s in range(N_CHIPS)]
    for k, n in enumerate(_SHARDED_SMALL):
        full[n] = jnp.concatenate([smalls[s][k] for s in range(N_CHIPS)], axis=1)

    def late_weights(stage, after):
        if stage == 0:
            up, down = _split_wait(handles[1], after, _gather_copies, "gather_wait_mlp0")
            return {"mlp_w_up": [up], "mlp_w_down": [down]}
        glu, up, down = _split_wait(handles[2], after, _gather_copies, "gather_wait_rest")
        return {"ssm_w_glu": glu, "mlp_w_up": [up], "mlp_w_down": [down]}

    def shard_cols(t):
        return jnp.swapaxes(t.reshape(t.shape[0], N_CHIPS, t.shape[1] // N_CHIPS), 0, 1)

    pending = {}

    def on_grads(tag, g):
        scatter = [g[n] for n in _BIG if n in g]
        whole = []
        if tag == "ssm":
            whole = [_pack([g[n] for n in _REP_SSM], D_MODEL)]
        if tag == "rest":
            parts = [shard_cols(g[n]) for n in _SHARDED_SMALL]
            scatter = [jnp.stack([_pack([p[s] for p in parts], 128) for s in range(N_CHIPS)])]
            whole = [_pack([g[n] for n in _REP_MISC] + [g["loss"]], D_MODEL)]
        srcs = scatter + whole
        lands = [_landing(lax.dynamic_index_in_dim(a, my_chip, 0, keepdims=False), my_dev, N_DEV) for a in scatter]
        lands += [_landing(a, my_dev, N_DEV) for a in whole]
        hs, token = _split_start([(srcs, lands)], _exchange_copies(len(scatter)), N_EXCHANGE_PEERS, "exchange_start_" + tag)
        pending[tag] = (hs[0], len(scatter))
        return token[0, 0]

    _, grad_x = _local_step(x, loss_target, full, late_weights, on_grads)

    recv = {}
    for tag, (handle, n_scatter) in pending.items():
        recv[tag] = _split_wait(handle, grad_x, _exchange_copies(n_scatter), "exchange_wait_" + tag)
    loss = jnp.sum(recv["rest"][1].reshape(N_DEV, -1)[:, sum(wts[n].size for n in _REP_MISC)])

    out = {}

    def update(tag, pieces, w2, m2, v2):
        return _adamw(pieces, w2, m2, v2, "adamw_" + tag)

    def update_weight(n, pieces):
        shp = wts[n].shape
        r2 = (math.prod(shp[:-1]), shp[-1])
        res = update(n, pieces, wts[n].reshape(r2), mom[n].reshape(r2), var[n].reshape(r2))
        out[n] = [t.reshape(shp) for t in res]

    update_weight("mlp_w_up", [recv["mlp0"][1], recv["mlp1"][0]])
    update_weight("mlp_w_down", [recv["mlp0"][2], recv["mlp1"][1]])
    update_weight("attn_w_o", [recv["mlp0"][0]])
    update_weight("ssm_w_glu", [recv["ssm"][0]])
    update_weight("attn_w_qkv", [recv["qkv"][0]])
    for tag, group, pieces, cols in (("small", _SHARDED_SMALL, recv["rest"][0], 128),
                                     ("rep_ssm", _REP_SSM, recv["ssm"][1], D_MODEL),
                                     ("rep_misc", _REP_MISC, recv["rest"][1], D_MODEL)):
        like = [wts[n] for n in group]
        res = update(tag, [pieces], _pack(like, cols), _pack([mom[n] for n in group], cols),
                     _pack([var[n] for n in group], cols))
        for k, n in enumerate(group):
            out[n] = [_unpack(t, like)[k] for t in res]

    return (loss, grad_x, *[out[n][0] for n in names], *[out[n][1] for n in names],
            *[out[n][2] for n in names], *[out[n][3] for n in names])
```

```python
import functools
import math

import jax
import jax.numpy as jnp
from jax import lax
from jax.experimental import pallas as pl
from jax.experimental.pallas import tpu as pltpu

F32 = jnp.float32
BF16 = jnp.bfloat16
SDS = jax.ShapeDtypeStruct

D_MODEL = 1024
N_HEADS = 16
N_KV = 4
GQA = N_HEADS // N_KV
HEAD_DIM = 64
BLOCK = 128
N_META = 16
PAD = BLOCK - N_META
QKV_DIM = (N_HEADS + 2 * N_KV) * HEAD_DIM
KV_DIM = 2 * N_KV * HEAD_DIM
D_FF = 4 * D_MODEL
N_CHIPS = 4
N_DEV = 8
SSM_GROUP = 16
SSM_NG = D_MODEL // SSM_GROUP
SSM_STATE = 64
N_PAIR = SSM_NG // 2
PAIRS_PER_CHUNK = 4
RMS_EPS = 1e-6
NEG_INF = -1e30
LAMBDA_RE_MAX = -1e-4
ADAM_LR, ADAM_B1, ADAM_B2, ADAM_EPS, ADAM_WD, ADAM_STEP = 0.001, 0.9, 0.999, 1e-08, 0.01, 10

TM = 384
MM_TILES = (1056, 768, 384)
MLP_FWD_TILES = (384,)
MLP_BWD_TILES = (768, 384)
TN_TILES = (1408, 768, 384)
VMEM_LIMIT = 56 * 1024 * 1024


def _params(n_grid):
    return pltpu.CompilerParams(dimension_semantics=("arbitrary",) * n_grid, vmem_limit_bytes=VMEM_LIMIT)


def _row_tile(n_rows, tiles):
    return next(t for t in tiles if n_rows % t == 0)


def _rms(h, w):
    r = lax.rsqrt(jnp.mean(h * h, axis=-1, keepdims=True) + RMS_EPS)
    return h * r * w


def _rms_bwd(dhn, h, w):
    r = lax.rsqrt(jnp.mean(h * h, axis=-1, keepdims=True) + RMS_EPS)
    g = dhn * w
    proj = jnp.sum(g * h, axis=-1, keepdims=True) * (1.0 / D_MODEL)
    return r * g - h * (r * r * r) * proj, dhn * h * r


def _fold8(t):
    return jnp.sum(t.reshape(t.shape[0] // 8, 8, t.shape[1]), axis=0)


def _gelu(y):
    return 0.5 * y * (1.0 + jnp.tanh(0.7978845608028654 * (y + 0.044715 * y * y * y)))


def _gelu_grad(y):
    t = jnp.tanh(0.7978845608028654 * (y + 0.044715 * y * y * y))
    return 0.5 * (1.0 + t) + 0.5 * y * (1.0 - t * t) * 0.7978845608028654 * (1.0 + 3.0 * 0.044715 * y * y)


def _w4_spec(w4):
    n_sh, _, k, n = w4.shape
    return pl.BlockSpec((n_sh, None, k, n), lambda i: (0, 0, 0, 0))


def _rms_mm_cols(h, wn, w4, name):
    n_rows = h.shape[0]
    n_sh, _, k, n = w4.shape
    tm = _row_tile(n_rows, MM_TILES)

    def body(h_ref, wn_ref, w_ref, o_ref, hn_ref):
        hn = _rms(h_ref[...], wn_ref[...]).astype(BF16)
        hn_ref[...] = hn
        for s in range(n_sh):
            o_ref[:, s * n:(s + 1) * n] = jnp.dot(hn, w_ref[s], preferred_element_type=F32).astype(o_ref.dtype)

    return pl.pallas_call(
        body, name=name, grid=(n_rows // tm,),
        in_specs=[pl.BlockSpec((tm, k), lambda i: (i, 0)), pl.BlockSpec((1, k), lambda i: (0, 0)), _w4_spec(w4)],
        out_specs=[pl.BlockSpec((tm, n_sh * n), lambda i: (i, 0)), pl.BlockSpec((tm, k), lambda i: (i, 0))],
        out_shape=[SDS((n_rows, n_sh * n), BF16), SDS((n_rows, k), BF16)],
        compiler_params=_params(1),
    )(h, wn, w4)


def _mm_cols(x, w4, trans_w, name):
    n_rows, kx = x.shape
    tm = _row_tile(n_rows, MM_TILES)
    n_sh, _, k, n = w4.shape
    n_out = k if trans_w else n
    dims = (((1,), (1,)), ((), ())) if trans_w else (((1,), (0,)), ((), ()))

    def body(x_ref, w_ref, o_ref):
        x16 = x_ref[...].astype(BF16)
        for s in range(n_sh):
            o_ref[:, s * n_out:(s + 1) * n_out] = lax.dot_general(
                x16, w_ref[s], dims, preferred_element_type=F32).astype(o_ref.dtype)

    return pl.pallas_call(
        body, name=name, grid=(n_rows // tm,),
        in_specs=[pl.BlockSpec((tm, kx), lambda i: (i, 0)), _w4_spec(w4)],
        out_specs=pl.BlockSpec((tm, n_sh * n_out), lambda i: (i, 0)),
        out_shape=SDS((n_rows, n_sh * n_out), BF16),
        compiler_params=_params(1),
    )(x, w4)


def _mm_acc(x, w4, trans_w, name, res=None, rms_bwd=None, out_dtype=F32):
    n_rows = x.shape[0]
    tm = _row_tile(n_rows, MM_TILES)
    n_sh, _, k, n = w4.shape
    kx, n_out = (n, k) if trans_w else (k, n)
    dims = (((1,), (1,)), ((), ())) if trans_w else (((1,), (0,)), ((), ()))

    def body(*refs):
        if rms_bwd is not None:
            x_ref, w_ref, h_ref, wn_ref, dres_ref, o_ref, dw_ref = refs
        elif res is not None:
            x_ref, w_ref, res_ref, o_ref = refs
        else:
            x_ref, w_ref, o_ref = refs
        acc = None
        for s in range(n_sh):
            part = lax.dot_general(x_ref[:, s * kx:(s + 1) * kx].astype(BF16), w_ref[s], dims, preferred_element_type=F32)
            acc = part if acc is None else acc + part
        if rms_bwd is not None:
            dh, dw_rows = _rms_bwd(acc, h_ref[...], wn_ref[...])
            o_ref[...] = (dres_ref[...] + dh).astype(o_ref.dtype)

            @pl.when(pl.program_id(0) == 0)
            def _():
                dw_ref[...] = jnp.zeros_like(dw_ref)

            dw_ref[...] += _fold8(dw_rows)
        elif res is not None:
            o_ref[...] = (res_ref[...] + acc).astype(o_ref.dtype)
        else:
            o_ref[...] = acc.astype(o_ref.dtype)

    row = lambda i: (i, 0)
    in_specs = [pl.BlockSpec((tm, n_sh * kx), row), _w4_spec(w4)]
    args = [x, w4]
    out_specs = pl.BlockSpec((tm, n_out), row)
    out_shape = SDS((n_rows, n_out), out_dtype)
    if rms_bwd is not None:
        h, wn, dres = rms_bwd
        in_specs += [pl.BlockSpec((tm, n_out), row), pl.BlockSpec((1, n_out), lambda i: (0, 0)),
                     pl.BlockSpec((tm, n_out), row)]
        args += [h, wn, dres]
        out_specs = [out_specs, pl.BlockSpec((8, n_out), lambda i: (0, 0))]
        out_shape = [out_shape, SDS((8, n_out), F32)]
    elif res is not None:
        in_specs.append(pl.BlockSpec((tm, n_out), row))
        args.append(res)
    return pl.pallas_call(
        body, name=name, grid=(n_rows // tm,), in_specs=in_specs, out_specs=out_specs, out_shape=out_shape,
        compiler_params=_params(1),
    )(*args)


def _mm_tn(a, b, n_sh, a_sharded, name):
    n_rows = a.shape[0]
    tm = _row_tile(n_rows, TN_TILES)
    ka = a.shape[1] // n_sh if a_sharded else a.shape[1]
    nb = b.shape[1] if a_sharded else b.shape[1] // n_sh
    n_i = n_rows // tm

    def body(a_ref, b_ref, o_ref, acc):
        i = pl.program_id(0)

        @pl.when(i == 0)
        def _():
            acc[...] = jnp.zeros_like(acc)

        for s in range(n_sh):
            a_s = a_ref[:, s * ka:(s + 1) * ka] if a_sharded else a_ref[...]
            b_s = b_ref[...] if a_sharded else b_ref[:, s * nb:(s + 1) * nb]
            acc[s] += lax.dot_general(a_s.astype(BF16), b_s.astype(BF16), (((0,), (0,)), ((), ())),
                                      preferred_element_type=F32)

        @pl.when(i == n_i - 1)
        def _():
            o_ref[...] = acc[...].astype(o_ref.dtype)

    return pl.pallas_call(
        body, name=name, grid=(n_i,),
        in_specs=[pl.BlockSpec((tm, a.shape[1]), lambda i: (i, 0)), pl.BlockSpec((tm, b.shape[1]), lambda i: (i, 0))],
        out_specs=pl.BlockSpec((n_sh, ka, nb), lambda i: (0, 0, 0)),
        out_shape=SDS((n_sh, ka, nb), BF16),
        scratch_shapes=[pltpu.VMEM((n_sh, ka, nb), F32)], compiler_params=_params(1),
    )(a, b)


def _mlp_fwd(h, wn, w_up4, w_down4, name, next_norm=None, glu_z=None):
    n_rows = h.shape[0]
    tm = _row_tile(n_rows, MLP_FWD_TILES)
    n_sh = w_up4.shape[0]
    f_sh = D_FF // n_sh
    w_down = w_down4.reshape(D_FF, D_MODEL)

    def body(*refs):
        refs = list(refs)
        h_ref, wn_ref, wu_ref, wd_ref = refs[:4]
        at = 4
        if next_norm is not None:
            nn_ref = refs[at]
            at += 1
        if glu_z is not None:
            z_ref = refs[at]
            at += 1
        o_ref, a_ref, hn_ref = refs[at:at + 3]
        at += 3
        if next_norm is not None:
            u_ref = refs[at]
            at += 1
        if glu_z is not None:
            hin_ref = refs[at]
            at += 1
        act_s = refs[at]
        h_in = h_ref[...]
        if glu_z is not None:
            h_in = h_in + z_ref[:, 0:D_MODEL].astype(F32) * jax.nn.sigmoid(z_ref[:, D_MODEL:2 * D_MODEL].astype(F32))
            hin_ref[...] = h_in
        hn = _rms(h_in, wn_ref[...]).astype(BF16)
        hn_ref[...] = hn
        for s in range(n_sh):
            cols = slice(s * f_sh, (s + 1) * f_sh)
            a = jnp.dot(hn, wu_ref[s], preferred_element_type=F32)
            a_ref[:, cols] = a.astype(BF16)
            act = jnp.maximum(a, 0.0)
            act_s[:, cols] = (act * act).astype(BF16)
        out = h_in + jnp.dot(act_s[...], wd_ref[...], preferred_element_type=F32)
        o_ref[...] = out
        if next_norm is not None:
            u_ref[...] = _rms(out, nn_ref[...])

    row = lambda i: (i, 0)
    vec = pl.BlockSpec((1, D_MODEL), lambda i: (0, 0))
    in_specs = [pl.BlockSpec((tm, D_MODEL), row), vec,
                pl.BlockSpec((n_sh, None, D_MODEL, f_sh), lambda i: (0, 0, 0, 0), pipeline_mode=pl.Buffered(1)),
                pl.BlockSpec((D_FF, D_MODEL), lambda i: (0, 0), pipeline_mode=pl.Buffered(1))]
    out_specs = [pl.BlockSpec((tm, D_MODEL), row), pl.BlockSpec((tm, D_FF), row), pl.BlockSpec((tm, D_MODEL), row)]
    out_shape = [SDS((n_rows, D_MODEL), F32), SDS((n_rows, D_FF), BF16), SDS((n_rows, D_MODEL), BF16)]
    args = [h, wn, w_up4, w_down]
    if next_norm is not None:
        in_specs.append(vec)
        args.append(next_norm)
    if glu_z is not None:
        in_specs.append(pl.BlockSpec((tm, 2 * D_MODEL), row))
        args.append(glu_z)
    for extra in (next_norm, glu_z):
        if extra is not None:
            out_specs.append(pl.BlockSpec((tm, D_MODEL), row))
            out_shape.append(SDS((n_rows, D_MODEL), F32))
    return pl.pallas_call(
        body, name=name, grid=(n_rows // tm,), in_specs=in_specs, out_specs=out_specs, out_shape=out_shape,
        scratch_shapes=[pltpu.VMEM((tm, D_FF), BF16)],
        compiler_params=_params(1),
    )(*args)


def _mlp_bwd_shard(s, dh, a, hn, dhn_prev, h, wn, w_up4, w_down4, dw_up_buf, dw_down_buf, name):
    n_rows = dh.shape[0]
    n_sh = w_up4.shape[0]
    f_sh = D_FF // n_sh
    tm = _row_tile(n_rows, MLP_BWD_TILES)
    n_i = n_rows // tm
    last = h is not None
    nt = (((1,), (1,)), ((), ()))
    tn = (((0,), (0,)), ((), ()))

    def body(*refs):
        refs = list(refs)
        dh_ref, a_ref, hn_ref, wu_ref, wd_ref = refs[:5]
        at = 5
        prev_ref = None
        if dhn_prev is not None:
            prev_ref = refs[at]
            at += 1
        if last:
            h_ref, wn_ref = refs[at:at + 2]
            at += 2
        if dw_up_buf is not None:
            at += 2
        o_ref, dwu_ref, dwd_ref = refs[at:at + 3]
        at += 3
        if last:
            dnorm_ref = refs[at]
            at += 1
        acc_u, acc_d = refs[at:at + 2]
        i = pl.program_id(0)

        @pl.when(i == 0)
        def _():
            acc_u[...] = jnp.zeros_like(acc_u)
            acc_d[...] = jnp.zeros_like(acc_d)
            if last:
                dnorm_ref[...] = jnp.zeros_like(dnorm_ref)

        dh16 = dh_ref[...].astype(BF16)
        r = jnp.maximum(a_ref[...].astype(F32), 0.0)
        dact = lax.dot_general(dh16, wd_ref[...], nt, preferred_element_type=F32)
        da16 = (dact * (2.0 * r)).astype(BF16)
        acc_d[...] += lax.dot_general((r * r).astype(BF16), dh16, tn, preferred_element_type=F32)
        acc_u[...] += lax.dot_general(hn_ref[...], da16, tn, preferred_element_type=F32)
        dhn = lax.dot_general(da16, wu_ref[...], nt, preferred_element_type=F32)
        if prev_ref is not None:
            dhn = dhn + prev_ref[...]
        if last:
            d_rms, dw_rows = _rms_bwd(dhn, h_ref[...], wn_ref[...])
            o_ref[...] = dh_ref[...] + d_rms
            dnorm_ref[...] += _fold8(dw_rows)
        else:
            o_ref[...] = dhn

        @pl.when(i == n_i - 1)
        def _():
            dwu_ref[...] = acc_u[...].astype(BF16)
            dwd_ref[...] = acc_d[...].astype(BF16)

    row = lambda i: (i, 0)
    tile = pl.BlockSpec((tm, D_MODEL), row)
    in_specs = [tile, pl.BlockSpec((tm, f_sh), lambda i: (i, s)), tile,
                pl.BlockSpec((None, None, D_MODEL, f_sh), lambda i: (s, 0, 0, 0)),
                pl.BlockSpec((None, None, f_sh, D_MODEL), lambda i: (s, 0, 0, 0))]
    args = [dh, a, hn, w_up4, w_down4]
    if dhn_prev is not None:
        in_specs.append(tile)
        args.append(dhn_prev)
    if last:
        in_specs += [tile, pl.BlockSpec((1, D_MODEL), lambda i: (0, 0))]
        args += [h, wn]
    aliases = {}
    if dw_up_buf is not None:
        aliases = {len(args): 1, len(args) + 1: 2}
        in_specs += [pl.BlockSpec(memory_space=pl.ANY)] * 2
        args += [dw_up_buf, dw_down_buf]
    out_specs = [tile, pl.BlockSpec((None, D_MODEL, f_sh), lambda i: (s, 0, 0)),
                 pl.BlockSpec((None, f_sh, D_MODEL), lambda i: (s, 0, 0))]
    out_shape = [SDS((n_rows, D_MODEL), F32), SDS((n_sh, D_MODEL, f_sh), BF16), SDS((n_sh, f_sh, D_MODEL), BF16)]
    if last:
        out_specs.append(pl.BlockSpec((8, D_MODEL), lambda i: (0, 0)))
        out_shape.append(SDS((8, D_MODEL), F32))
    return pl.pallas_call(
        body, name=name, grid=(n_i,), in_specs=in_specs, out_specs=out_specs, out_shape=out_shape,
        input_output_aliases=aliases,
        scratch_shapes=[pltpu.VMEM((D_MODEL, f_sh), F32), pltpu.VMEM((f_sh, D_MODEL), F32)],
        compiler_params=_params(1),
    )(*args)


def _attn_masks(n):
    qi = lax.broadcasted_iota(jnp.int32, (BLOCK, 3 * BLOCK), 0)
    col = lax.broadcasted_iota(jnp.int32, (BLOCK, 3 * BLOCK), 1)
    kj = col - BLOCK
    dist = BLOCK + qi - kj
    kmin = jnp.where(n == 0, 2 * BLOCK, jnp.where(n == 1, BLOCK, 0))
    band_ok = (col >= BLOCK) & (dist >= 0) & (dist < BLOCK) & (kj >= kmin)
    q_pos = n * BLOCK + qi - PAD
    meta_ok = (col >= PAD) & (col < BLOCK) & (col - PAD <= q_pos)
    distf = jnp.where(col >= BLOCK, dist, 0).astype(F32)
    return band_ok | meta_ok, distf


def _alibi_slope(h):
    return float(2.0 ** (-8.0 * (h + 1) / N_HEADS))


def _attn_bias(n, bias_s):
    ok, distf = _attn_masks(n)
    for h in range(N_HEADS):
        bias_s[h] = jnp.where(ok, -_alibi_slope(h) * distf, NEG_INF)


def _attn_fwd(qkv, sinks, n_ex, nb):
    n_rows = qkv.shape[0]
    kvb = N_HEADS * HEAD_DIM // KV_DIM

    def body(sink_ref, q_ref, kvm_ref, kvp_ref, kvc_ref, o_ref, lse_ref, k_s, v_s, q_s, bias_s):
        n = pl.program_id(1)

        @pl.when(n <= 2)
        def _():
            _attn_bias(n, bias_s)

        v_s[...] = jnp.ones_like(v_s)
        for part, ref in enumerate((kvm_ref, kvp_ref, kvc_ref)):
            rows = slice(part * BLOCK, (part + 1) * BLOCK)
            k_s[rows, :] = ref[:, 0:N_KV * HEAD_DIM]
            for kv in range(N_KV):
                v_s[rows, kv * 2 * HEAD_DIM:kv * 2 * HEAD_DIM + HEAD_DIM] = \
                    ref[:, (N_KV + kv) * HEAD_DIM:(N_KV + kv + 1) * HEAD_DIM]
        for h in range(N_HEADS):
            q_s[h // GQA, (h % GQA) * BLOCK:(h % GQA + 1) * BLOCK, :] = \
                q_ref[:, h * HEAD_DIM:(h + 1) * HEAD_DIM] * (HEAD_DIM ** -0.5)

        def scores(kv):
            return lax.dot_general(q_s[kv], k_s[:, kv * HEAD_DIM:(kv + 1) * HEAD_DIM], (((1,), (1,)), ((), ())),
                                   preferred_element_type=F32)

        ahead = scores(0)
        for kv in range(N_KV):
            s4 = ahead
            if kv + 1 < N_KV:
                ahead = scores(kv + 1)
            es, ms, sink_es = [], [], []
            for g in range(GQA):
                h = kv * GQA + g
                s = s4[g * BLOCK:(g + 1) * BLOCK] + bias_s[h]
                sink = sink_ref[0, h]
                m = jnp.maximum(jnp.max(s, axis=-1, keepdims=True), sink)
                es.append(jnp.exp(s - m).astype(BF16))
                ms.append(m)
                sink_es.append(jnp.exp(sink - m))
            pv = jnp.dot(jnp.concatenate(es, axis=0), v_s[:, kv * 2 * HEAD_DIM:(kv + 1) * 2 * HEAD_DIM],
                         preferred_element_type=F32)
            for g in range(GQA):
                h = kv * GQA + g
                pg = pv[g * BLOCK:(g + 1) * BLOCK]
                l = pg[:, HEAD_DIM:HEAD_DIM + 1] + sink_es[g]
                o_ref[:, h * HEAD_DIM:(h + 1) * HEAD_DIM] = (pg[:, 0:HEAD_DIM] * (1.0 / l)).astype(BF16)
                lse_ref[:, h:h + 1] = ms[g] + jnp.log(l)

    return pl.pallas_call(
        body, name="attn_fwd", grid=(n_ex, nb),
        in_specs=[pl.BlockSpec(memory_space=pltpu.SMEM),
                  pl.BlockSpec((BLOCK, N_HEADS * HEAD_DIM), lambda b, n: (b * nb + n, 0)),
                  pl.BlockSpec((BLOCK, KV_DIM), lambda b, n: (b * nb, kvb)),
                  pl.BlockSpec((BLOCK, KV_DIM), lambda b, n: (b * nb + jnp.maximum(n - 1, 0), kvb)),
                  pl.BlockSpec((BLOCK, KV_DIM), lambda b, n: (b * nb + n, kvb))],
        out_specs=[pl.BlockSpec((BLOCK, N_HEADS * HEAD_DIM), lambda b, n: (b * nb + n, 0)),
                   pl.BlockSpec((BLOCK, N_HEADS), lambda b, n: (b * nb + n, 0))],
        out_shape=[SDS((n_rows, N_HEADS * HEAD_DIM), BF16), SDS((n_rows, N_HEADS), F32)],
        scratch_shapes=[pltpu.VMEM((3 * BLOCK, N_KV * HEAD_DIM), BF16), pltpu.VMEM((3 * BLOCK, 2 * N_KV * HEAD_DIM), BF16),
                        pltpu.VMEM((N_KV, GQA * BLOCK, HEAD_DIM), BF16), pltpu.VMEM((N_HEADS, BLOCK, 3 * BLOCK), F32)],
        compiler_params=_params(2),
    )(sinks, qkv, qkv, qkv, qkv)


def _attn_bwd(qkv, sinks, o, lse, do, n_ex, nb):
    n_rows = qkv.shape[0]
    kvb = N_HEADS * HEAD_DIM // KV_DIM
    scale = HEAD_DIM ** -0.5
    nq = lambda r: nb - 1 - r

    def body(sink_ref, q_ref, kvm_ref, kvp_ref, kvc_ref, o_ref, lse_ref, do_ref, dqkv_ref, dsink_ref,
             k_s, v_s, dkv_s, carry_s, meta_s, q_s, do_s, bias_s):
        b, r = pl.program_id(0), pl.program_id(1)
        n = nq(r)

        @pl.when((r == 0) | (n <= 1))
        def _():
            _attn_bias(n, bias_s)

        @pl.when((b == 0) & (r == 0))
        def _():
            dsink_ref[...] = jnp.zeros_like(dsink_ref)

        @pl.when(r == 0)
        def _():
            carry_s[...] = jnp.zeros_like(carry_s)
            meta_s[...] = jnp.zeros_like(meta_s)

        for part, ref in enumerate((kvm_ref, kvp_ref, kvc_ref)):
            k_s[part * BLOCK:(part + 1) * BLOCK, :] = ref[:, 0:N_KV * HEAD_DIM]
            v_s[part * BLOCK:(part + 1) * BLOCK, :] = ref[:, N_KV * HEAD_DIM:KV_DIM]
        nt = (((1,), (1,)), ((), ()))
        tn = (((0,), (0,)), ((), ()))
        deltas = []
        for h in range(N_HEADS):
            rows = slice((h % GQA) * BLOCK, (h % GQA + 1) * BLOCK)
            cols = slice(h * HEAD_DIM, (h + 1) * HEAD_DIM)
            q_s[h // GQA, rows, :] = q_ref[:, cols] * scale
            do_s[h // GQA, rows, :] = do_ref[:, cols]
            deltas.append(jnp.sum(do_ref[:, cols].astype(F32) * o_ref[:, cols].astype(F32), axis=-1, keepdims=True))
        for kv in range(N_KV):
            kcols = slice(kv * HEAD_DIM, (kv + 1) * HEAD_DIM)
            vcols = slice(N_KV * HEAD_DIM + kv * HEAD_DIM, N_KV * HEAD_DIM + (kv + 1) * HEAD_DIM)
            kh, vh = k_s[:, kcols], v_s[:, kcols]
            s4 = lax.dot_general(q_s[kv], kh, nt, preferred_element_type=F32)
            dp4 = lax.dot_general(do_s[kv], vh, nt, preferred_element_type=F32)
            ps, dss = [], []
            for g in range(GQA):
                h = kv * GQA + g
                cols = slice(h * HEAD_DIM, (h + 1) * HEAD_DIM)
                rows = slice(g * BLOCK, (g + 1) * BLOCK)
                s = s4[rows] + bias_s[h]
                lse_h = lse_ref[:, h:h + 1]
                p = jnp.exp(s - lse_h)
                delta = deltas[h]
                dsink_ref[:, h:h + 1] += -jnp.exp(sink_ref[0, h] - lse_h) * delta
                ps.append(p.astype(BF16))
                dss.append((p * (dp4[rows] - delta)).astype(BF16))
            p4, ds4 = jnp.concatenate(ps, axis=0), jnp.concatenate(dss, axis=0)
            dq4 = jnp.dot(ds4, kh, preferred_element_type=F32) * scale
            for g in range(GQA):
                cols = slice((kv * GQA + g) * HEAD_DIM, (kv * GQA + g + 1) * HEAD_DIM)
                dqkv_ref[:, cols] = dq4[g * BLOCK:(g + 1) * BLOCK].astype(BF16)
            dkv_s[:, kcols] = lax.dot_general(ds4, q_s[kv], tn, preferred_element_type=F32)
            dkv_s[:, vcols] = lax.dot_general(p4, do_s[kv], tn, preferred_element_type=F32)

        meta_s[...] += dkv_s[0:BLOCK, :]
        cur = dkv_s[2 * BLOCK:3 * BLOCK, :] + carry_s[...]
        carry_s[...] = dkv_s[BLOCK:2 * BLOCK, :]

        @pl.when(n > 0)
        def _():
            dqkv_ref[:, N_HEADS * HEAD_DIM:QKV_DIM] = cur.astype(BF16)

        @pl.when(n == 0)
        def _():
            dqkv_ref[:, N_HEADS * HEAD_DIM:QKV_DIM] = (cur + meta_s[...]).astype(BF16)

    blk = lambda b, r: (b * nb + nq(r), 0)
    return pl.pallas_call(
        body, name="attn_bwd", grid=(n_ex, nb),
        in_specs=[pl.BlockSpec(memory_space=pltpu.SMEM),
                  pl.BlockSpec((BLOCK, N_HEADS * HEAD_DIM), blk),
                  pl.BlockSpec((BLOCK, KV_DIM), lambda b, r: (b * nb, kvb)),
                  pl.BlockSpec((BLOCK, KV_DIM), lambda b, r: (b * nb + jnp.maximum(nq(r) - 1, 0), kvb)),
                  pl.BlockSpec((BLOCK, KV_DIM), lambda b, r: (b * nb + nq(r), kvb)),
                  pl.BlockSpec((BLOCK, N_HEADS * HEAD_DIM), blk),
                  pl.BlockSpec((BLOCK, N_HEADS), blk),
                  pl.BlockSpec((BLOCK, N_HEADS * HEAD_DIM), blk)],
        out_specs=[pl.BlockSpec((BLOCK, QKV_DIM), blk),
                   pl.BlockSpec((BLOCK, N_HEADS), lambda b, r: (0, 0))],
        out_shape=[SDS((n_rows, QKV_DIM), BF16), SDS((BLOCK, N_HEADS), F32)],
        scratch_shapes=[pltpu.VMEM((3 * BLOCK, N_KV * HEAD_DIM), BF16), pltpu.VMEM((3 * BLOCK, N_KV * HEAD_DIM), BF16),
                        pltpu.VMEM((3 * BLOCK, KV_DIM), F32), pltpu.VMEM((BLOCK, KV_DIM), F32),
                        pltpu.VMEM((BLOCK, KV_DIM), F32), pltpu.VMEM((N_KV, GQA * BLOCK, HEAD_DIM), BF16),
                        pltpu.VMEM((N_KV, GQA * BLOCK, HEAD_DIM), BF16), pltpu.VMEM((N_HEADS, BLOCK, 3 * BLOCK), F32)],
        compiler_params=_params(2),
    )(sinks, qkv, qkv, qkv, qkv, o, lse, do)


SSM_TILES = (1408, 384)
XW = 256 * PAIRS_PER_CHUNK


def _cmul_add(xr, xi, mr, mi, sr, si):
    return xr + mr * sr - mi * si, xi + mr * si + mi * sr


def _to_segments(src_ref, dst, seg):
    for s in range(seg):
        dst[s * 8:(s + 1) * 8, :] = src_ref[pl.ds(s, 8, stride=seg), :]


def _from_segments(src, i, seg):
    return src[pl.ds(i, seg, stride=8), :]


def _scan_segments(buf, tab_ref, carry_s, seg, reverse):
    shifts = (7, 6, 4) if reverse else (1, 2, 4)
    row_id = lax.broadcasted_iota(jnp.int32, (8, 128), 0)
    a_tiles = [tab_ref[j, c] for j in range(PAIRS_PER_CHUNK) for c in (0, 1)]

    def local(si, prev):
        s = (seg - 1 - si) if reverse else si
        row = pl.multiple_of(s * 8, 8)
        out = []
        for j in range(PAIRS_PER_CHUNK):
            re, im = slice(256 * j, 256 * j + 128), slice(256 * j + 128, 256 * j + 256)
            xr, xi = _cmul_add(buf[pl.ds(row, 8), re], buf[pl.ds(row, 8), im],
                               a_tiles[2 * j], a_tiles[2 * j + 1], prev[2 * j], prev[2 * j + 1])
            buf[pl.ds(row, 8), re] = xr
            buf[pl.ds(row, 8), im] = xi
            out += [xr, xi]
        return tuple(out)

    zero = jnp.zeros((8, 128), F32)
    edge = lax.fori_loop(0, seg, local, (zero,) * (2 * PAIRS_PER_CHUNK))

    entering = []
    for j in range(PAIRS_PER_CHUNK):
        er, ei = edge[2 * j], edge[2 * j + 1]
        if reverse:
            sr = jnp.where(row_id == 7, carry_s[2 * j], pltpu.roll(er, 7, 0))
            si_ = jnp.where(row_id == 7, carry_s[2 * j + 1], pltpu.roll(ei, 7, 0))
        else:
            sr = jnp.where(row_id == 0, carry_s[2 * j], pltpu.roll(er, 1, 0))
            si_ = jnp.where(row_id == 0, carry_s[2 * j + 1], pltpu.roll(ei, 1, 0))
        for lvl, sh in enumerate(shifts):
            sr, si_ = _cmul_add(sr, si_, tab_ref[j, 2 + 2 * lvl], tab_ref[j, 3 + 2 * lvl],
                                pltpu.roll(sr, sh, 0), pltpu.roll(si_, sh, 0))
        entering += [sr, si_]
        tr, ti = _cmul_add(er, ei, tab_ref[j, 2], tab_ref[j, 3], sr, si_)
        out_row = slice(0, 1) if reverse else slice(7, 8)
        carry_s[2 * j] = jnp.broadcast_to(tr[out_row], (8, 128))
        carry_s[2 * j + 1] = jnp.broadcast_to(ti[out_row], (8, 128))

    def fix(si, carried):
        s = (seg - 1 - si) if reverse else si
        row = pl.multiple_of(s * 8, 8)
        out = []
        for j in range(PAIRS_PER_CHUNK):
            re, im = slice(256 * j, 256 * j + 128), slice(256 * j + 128, 256 * j + 256)
            ar, ai, fr, fi = a_tiles[2 * j], a_tiles[2 * j + 1], carried[2 * j], carried[2 * j + 1]
            fr, fi = ar * fr - ai * fi, ar * fi + ai * fr
            buf[pl.ds(row, 8), re] += fr
            buf[pl.ds(row, 8), im] += fi
            out += [fr, fi]
        return tuple(out)

    lax.fori_loop(0, seg, fix, tuple(entering))


def _ssm_fwd(u, b_pad, c_pad, tab, d_skip, n_ex, lp):
    n_rows = u.shape[0]
    TM = _row_tile(lp, SSM_TILES)
    SEG = TM // 8
    n_t = lp // TM
    n_chunk = D_MODEL // 128

    def body(u_ref, bp_ref, cp_ref, tab_ref, d_ref, yg_ref, y_ref, xs_ref, up_ref, buf, carry_s, us, ys):
        @pl.when(pl.program_id(2) == 0)
        def _():
            carry_s[...] = jnp.zeros_like(carry_s)

        _to_segments(u_ref, us, SEG)
        ub = us[...]
        u16 = ub.astype(BF16)
        up_ref[...] = u16
        buf[...] = jnp.dot(u16, bp_ref[...], preferred_element_type=F32)
        _scan_segments(buf, tab_ref, carry_s, SEG, reverse=False)
        xb = buf[...].astype(BF16)
        xs_ref[...] = xb
        ys[...] = d_ref[...] * ub + jnp.dot(xb, cp_ref[...], preferred_element_type=F32)
        for i in range(8):
            yi = _from_segments(ys, i, SEG)
            y_ref[i * SEG:(i + 1) * SEG, :] = yi
            yg_ref[i * SEG:(i + 1) * SEG, :] = _gelu(yi).astype(BF16)

    rows = lambda b, q, t: (b * n_t + t, q)
    return pl.pallas_call(
        body, name="ssm_fwd", grid=(n_ex, n_chunk, n_t),
        in_specs=[pl.BlockSpec((TM, 128), rows),
                  pl.BlockSpec((None, 128, XW), lambda b, q, t: (q, 0, 0)),
                  pl.BlockSpec((None, XW, 128), lambda b, q, t: (q, 0, 0)),
                  pl.BlockSpec((PAIRS_PER_CHUNK, 8, 8, 128), lambda b, q, t: (q, 0, 0, 0)),
                  pl.BlockSpec((1, 128), lambda b, q, t: (0, q))],
        out_specs=[pl.BlockSpec((TM, 128), rows), pl.BlockSpec((TM, 128), rows),
                   pl.BlockSpec((None, TM, XW), lambda b, q, t: (q, b * n_t + t, 0)), pl.BlockSpec((TM, 128), rows)],
        out_shape=[SDS((n_rows, D_MODEL), BF16), SDS((n_rows, D_MODEL), F32), SDS((n_chunk, n_rows, XW), BF16),
                   SDS((n_rows, D_MODEL), BF16)],
        scratch_shapes=[pltpu.VMEM((TM, XW), F32), pltpu.VMEM((2 * PAIRS_PER_CHUNK, 8, 128), F32),
                        pltpu.VMEM((TM, 128), F32), pltpu.VMEM((TM, 128), F32)],
        compiler_params=_params(3),
    )(u, b_pad, c_pad, tab, d_skip)


def _ssm_bwd(dyg, y, u_seg, xs, ct_pad, bt_pad, tab_rev, d_skip, n_ex, lp):
    n_rows = u_seg.shape[0]
    TM = _row_tile(lp, SSM_TILES)
    SEG = TM // 8
    n_t = lp // TM
    n_chunk = D_MODEL // 128
    tile = lambda q, b, t: (b * n_t + (n_t - 1 - t), q)

    def body(dyg_ref, y_ref, up_ref, xs_ref, xp_ref, ct_ref, bt_ref, tab_ref, d_ref,
             du_ref, db_ref, dc_ref, da_ref, dd_ref, buf, xf, carry_s, dys, dyp):
        b, t = pl.program_id(1), pl.program_id(2)

        @pl.when((b == 0) & (t == 0))
        def _():
            db_ref[...] = jnp.zeros_like(db_ref)
            dc_ref[...] = jnp.zeros_like(dc_ref)
            da_ref[...] = jnp.zeros_like(da_ref)
            dd_ref[...] = jnp.zeros_like(dd_ref)

        @pl.when(t == 0)
        def _():
            carry_s[...] = jnp.zeros_like(carry_s)

        dys[...] = dyg_ref[...].astype(F32) * _gelu_grad(y_ref[...])
        _to_segments(dys, dyp, SEG)
        dy = dyp[...]
        u16 = up_ref[...]
        dd_ref[...] += _fold8(dy * u16.astype(F32))
        dy16 = dy.astype(BF16)
        first_tile = t == n_t - 1
        tn = (((0,), (0,)), ((), ()))
        buf[...] = jnp.dot(dy16, ct_ref[...], preferred_element_type=F32)
        dc_ref[...] += lax.dot_general(dy16, xs_ref[...], tn, preferred_element_type=F32)
        xf[16:16 + TM, :] = xs_ref[...].astype(F32)
        xf[0:16, :] = jnp.where(first_tile, 0.0, xp_ref[...].astype(F32))
        _scan_segments(buf, tab_ref, carry_s, SEG, reverse=True)
        g16 = buf[...].astype(BF16)
        dys[...] = d_ref[...] * dy + jnp.dot(g16, bt_ref[...], preferred_element_type=F32)
        db_ref[...] += lax.dot_general(u16, g16, tn, preferred_element_type=F32)
        row_id = lax.broadcasted_iota(jnp.int32, (8, 128), 0)
        for j in range(PAIRS_PER_CHUNK):
            re, im = slice(256 * j, 256 * j + 128), slice(256 * j + 128, 256 * j + 256)
            first = [jnp.where(row_id == 0, jnp.broadcast_to(xf[15:16, c], (8, 128)),
                               pltpu.roll(xf[8 + TM:16 + TM, c], 1, 0)) for c in (re, im)]
            for rows, pr, pi in ((slice(0, 8), first[0], first[1]),
                                 (slice(8, TM), xf[16:8 + TM, re], xf[16:8 + TM, im])):
                gr, gi = buf[rows, re], buf[rows, im]
                da_ref[j, 0] += _fold8(gr * pr + gi * pi)
                da_ref[j, 1] += _fold8(gi * pr - gr * pi)
        for i in range(8):
            du_ref[i * SEG:(i + 1) * SEG, :] = _from_segments(dys, i, SEG)

    prev16 = lambda q, b, t: (q, jnp.maximum((b * n_t + (n_t - 1 - t)) * (TM // 16) - 1, 0), 0)
    return pl.pallas_call(
        body, name="ssm_bwd", grid=(n_chunk, n_ex, n_t),
        in_specs=[pl.BlockSpec((TM, 128), tile), pl.BlockSpec((TM, 128), tile), pl.BlockSpec((TM, 128), tile),
                  pl.BlockSpec((None, TM, XW), lambda q, b, t: (q, b * n_t + (n_t - 1 - t), 0)),
                  pl.BlockSpec((None, 16, XW), prev16),
                  pl.BlockSpec((None, 128, XW), lambda q, b, t: (q, 0, 0)),
                  pl.BlockSpec((None, XW, 128), lambda q, b, t: (q, 0, 0)),
                  pl.BlockSpec((PAIRS_PER_CHUNK, 8, 8, 128), lambda q, b, t: (q, 0, 0, 0)),
                  pl.BlockSpec((1, 128), lambda q, b, t: (0, q))],
        out_specs=[pl.BlockSpec((TM, 128), tile),
                   pl.BlockSpec((None, 128, XW), lambda q, b, t: (q, 0, 0)),
                   pl.BlockSpec((None, 128, XW), lambda q, b, t: (q, 0, 0)),
                   pl.BlockSpec((PAIRS_PER_CHUNK, 2, 8, 128), lambda q, b, t: (q, 0, 0, 0)),
                   pl.BlockSpec((8, 128), lambda q, b, t: (0, q))],
        out_shape=[SDS((n_rows, D_MODEL), F32), SDS((n_chunk, 128, XW), F32), SDS((n_chunk, 128, XW), F32),
                   SDS((N_PAIR, 2, 8, 128), F32), SDS((8, D_MODEL), F32)],
        scratch_shapes=[pltpu.VMEM((TM, XW), F32), pltpu.VMEM((TM + 16, XW), F32),
                        pltpu.VMEM((2 * PAIRS_PER_CHUNK, 8, 128), F32), pltpu.VMEM((TM, 128), F32),
                        pltpu.VMEM((TM, 128), F32)],
        compiler_params=_params(3),
    )(dyg, y, u_seg, xs, xs, ct_pad, bt_pad, tab_rev, d_skip)


def _rms_bwd_call(dhn, h, wn, dres, name):
    n_rows = h.shape[0]

    def body(dhn_ref, h_ref, wn_ref, dres_ref, o_ref, dw_ref):
        @pl.when(pl.program_id(0) == 0)
        def _():
            dw_ref[...] = jnp.zeros_like(dw_ref)

        dh, dw_rows = _rms_bwd(dhn_ref[...], h_ref[...], wn_ref[...])
        o_ref[...] = dres_ref[...] + dh
        dw_ref[...] += _fold8(dw_rows)

    row = lambda i: (i, 0)
    return pl.pallas_call(
        body, name=name, grid=(n_rows // TM,),
        in_specs=[pl.BlockSpec((TM, D_MODEL), row), pl.BlockSpec((TM, D_MODEL), row),
                  pl.BlockSpec((1, D_MODEL), lambda i: (0, 0)), pl.BlockSpec((TM, D_MODEL), row)],
        out_specs=[pl.BlockSpec((TM, D_MODEL), row), pl.BlockSpec((8, D_MODEL), lambda i: (0, 0))],
        out_shape=[SDS((n_rows, D_MODEL), F32), SDS((8, D_MODEL), F32)], compiler_params=_params(1),
    )(dhn, h, wn, dres)


def _glu_bwd(dh, z, w4):
    n_rows = dh.shape[0]
    tm = _row_tile(n_rows, MM_TILES)
    n_sh, _, k, n = w4.shape

    def body(dh_ref, z_ref, w_ref, dz_ref, dyg_ref):
        sg = jax.nn.sigmoid(z_ref[:, D_MODEL:2 * D_MODEL].astype(F32))
        d = dh_ref[...]
        dz_ref[:, 0:D_MODEL] = (d * sg).astype(BF16)
        dz_ref[:, D_MODEL:2 * D_MODEL] = (d * z_ref[:, 0:D_MODEL].astype(F32) * sg * (1.0 - sg)).astype(BF16)
        acc = None
        for s in range(n_sh):
            part = lax.dot_general(dz_ref[:, s * n:(s + 1) * n], w_ref[s], (((1,), (1,)), ((), ())),
                                   preferred_element_type=F32)
            acc = part if acc is None else acc + part
        dyg_ref[...] = acc.astype(BF16)

    row = lambda i: (i, 0)
    return pl.pallas_call(
        body, name="glu_bwd", grid=(n_rows // tm,),
        in_specs=[pl.BlockSpec((tm, D_MODEL), row), pl.BlockSpec((tm, 2 * D_MODEL), row), _w4_spec(w4)],
        out_specs=[pl.BlockSpec((tm, 2 * D_MODEL), row), pl.BlockSpec((tm, k), row)],
        out_shape=[SDS((n_rows, 2 * D_MODEL), BF16), SDS((n_rows, k), BF16)], compiler_params=_params(1),
    )(dh, z, w4)


def _loss_head(h, wn, target, n_ex, nb):
    n_rows = h.shape[0]
    per_tile = TM // BLOCK
    n_tiles = nb // per_tile

    def body(h_ref, wn_ref, *rest):
        t_refs, (dh_ref, loss_ref, dw_ref) = rest[:per_tile], rest[per_tile:]
        b, j = pl.program_id(0), pl.program_id(1)

        @pl.when((b == 0) & (j == 0))
        def _():
            loss_ref[...] = jnp.zeros_like(loss_ref)
            dw_ref[...] = jnp.zeros_like(dw_ref)

        def block(k):
            rows = slice(k * BLOCK, (k + 1) * BLOCK)
            hh = h_ref[rows, :]
            diff = _rms(hh, wn_ref[...]) - t_refs[k][...]
            loss_ref[...] += 0.5 * jnp.sum(diff * diff) * (1.0 / D_MODEL)
            dh, dw_rows = _rms_bwd(diff * (1.0 / D_MODEL), hh, wn_ref[...])
            dh_ref[rows, :] = dh
            dw_ref[...] += _fold8(dw_rows)

        @pl.when(j == 0)
        def _():
            dh_ref[0:BLOCK, :] = jnp.zeros((BLOCK, D_MODEL), F32)

        pl.when(j > 0)(lambda: block(0))
        for k in range(1, per_tile):
            block(k)

    def t_spec(k):
        return pl.BlockSpec((BLOCK, D_MODEL), lambda b, j: (b * (nb - 1) + jnp.maximum(per_tile * j + k - 1, 0), 0))

    tile = pl.BlockSpec((TM, D_MODEL), lambda b, j: (b * n_tiles + j, 0))
    return pl.pallas_call(
        body, name="loss_head", grid=(n_ex, n_tiles),
        in_specs=[tile, pl.BlockSpec((1, D_MODEL), lambda b, j: (0, 0))] + [t_spec(k) for k in range(per_tile)],
        out_specs=[tile, pl.BlockSpec((8, 128), lambda b, j: (0, 0)), pl.BlockSpec((8, D_MODEL), lambda b, j: (0, 0))],
        out_shape=[SDS((n_rows, D_MODEL), F32), SDS((8, 128), F32), SDS((8, D_MODEL), F32)],
        compiler_params=_params(2),
    )(h, wn, *([target] * per_tile))


def _adamw(pieces, w, m, v, name):
    n_layers = len(pieces)
    rows, cols = pieces[0].shape[1:]
    rb = rows
    for cand in (256, 136, 128, 64, 32, 16, 8):
        if rows % cand == 0 and rows > cand:
            rb = cand
            break
    n_blk = rows // rb
    c1 = 1.0 / (1.0 - ADAM_B1 ** ADAM_STEP)
    c2 = 1.0 / (1.0 - ADAM_B2 ** ADAM_STEP)

    def body(*refs):
        p_refs = refs[:n_layers]
        w_ref, m_ref, v_ref, g_out, d_out, m_out, v_out = refs[n_layers:]
        layer = pl.program_id(0)
        g = None
        for l, p_ref in enumerate(p_refs):
            gl = p_ref[0].astype(F32)
            for k in range(1, N_DEV):
                gl = gl + p_ref[k].astype(F32)
            g = gl if g is None else jnp.where(layer == l, gl, g)
        m_new = ADAM_B1 * m_ref[...] + (1.0 - ADAM_B1) * g
        v_new = ADAM_B2 * v_ref[...] + (1.0 - ADAM_B2) * (g * g)
        g_out[...] = g
        m_out[...] = m_new
        v_out[...] = v_new
        d_out[...] = -ADAM_LR * ((m_new * c1) / (jnp.sqrt(v_new * c2) + ADAM_EPS) + ADAM_WD * w_ref[...])

    def piece_spec(l):
        return pl.BlockSpec((N_DEV, rb, cols), lambda ly, i: (0, jnp.where(ly == l, i, 0), 0))

    blk = pl.BlockSpec((rb, cols), lambda ly, i: (ly * n_blk + i, 0))
    return pl.pallas_call(
        body, name=name, grid=(n_layers, n_blk),
        in_specs=[piece_spec(l) for l in range(n_layers)] + [blk, blk, blk],
        out_specs=[blk, blk, blk, blk],
        out_shape=[SDS((n_layers * rows, cols), F32)] * 4, compiler_params=_params(2),
    )(*pieces, w, m, v)


_HBM = pl.BlockSpec(memory_space=pltpu.HBM)
_SEM = pl.BlockSpec(memory_space=pltpu.SEMAPHORE)
_EFFECT = pltpu.SideEffectType.DATAFLOW_SIDE_EFFECTING
N_GATHER_PEERS = N_CHIPS - 1
N_EXCHANGE_PEERS = N_DEV - 1


def _gather_copies(srcs, lands, send_sems, recv_sems):
    x, y, c = lax.axis_index("x"), lax.axis_index("y"), lax.axis_index("c")
    mine = 2 * x + y
    chips = [(1 - x, y), (x, 1 - y), (1 - x, 1 - y)]
    out, inc = [], []
    for a in range(len(srcs)):
        for k, (px, py) in enumerate(chips):
            j = a * N_GATHER_PEERS + k
            sems = dict(send_sem=send_sems.at[j], recv_sem=recv_sems.at[j], device_id=(px, py, c),
                        device_id_type=pl.DeviceIdType.MESH)
            out.append(pltpu.make_async_remote_copy(src_ref=srcs[a], dst_ref=lands[a].at[mine], **sems))
            inc.append(pltpu.make_async_remote_copy(src_ref=srcs[a], dst_ref=lands[a].at[2 * px + py], **sems))
    return out, inc


def _exchange_copies(n_scatter):
    def copies(srcs, lands, send_sems, recv_sems):
        x, y, c = lax.axis_index("x"), lax.axis_index("y"), lax.axis_index("c")
        me = 4 * x + 2 * y + c
        peers = [(x ^ (k >> 2), y ^ ((k >> 1) & 1), c ^ (k & 1)) for k in range(1, N_DEV)]
        out, inc = [], []
        for a in range(len(srcs)):
            for k, (px, py, pc) in enumerate(peers):
                j = a * N_EXCHANGE_PEERS + k
                sems = dict(send_sem=send_sems.at[j], recv_sem=recv_sems.at[j], device_id=(px, py, pc),
                            device_id_type=pl.DeviceIdType.MESH)
                theirs = srcs[a].at[2 * px + py] if a < n_scatter else srcs[a]
                mine = srcs[a].at[2 * x + y] if a < n_scatter else srcs[a]
                out.append(pltpu.make_async_remote_copy(src_ref=theirs, dst_ref=lands[a].at[me], **sems))
                inc.append(pltpu.make_async_remote_copy(src_ref=mine, dst_ref=lands[a].at[4 * px + 2 * py + pc], **sems))
        return out, inc

    return copies


def _split_start(groups, copies_fn, n_peers, name):
    sizes = [len(srcs) for srcs, _ in groups]
    flat = [a for srcs, lands in groups for a in list(srcs) + list(lands)]
    n_flat, n_grp = len(flat), len(groups)

    def body(*refs):
        sems = refs[2 * n_flat:2 * n_flat + 2 * n_grp]
        token = refs[-1]
        at = 0
        for gi, n in enumerate(sizes):
            out, _ = copies_fn(refs[at:at + n], refs[at + n:at + 2 * n], sems[2 * gi], sems[2 * gi + 1])
            for cp in out:
                cp.start()
            at += 2 * n
        token[...] = jnp.zeros_like(token)

    sem_shapes = []
    for n in sizes:
        sem_shapes += [pltpu.SemaphoreType.DMA((n * n_peers,)), pltpu.SemaphoreType.DMA((n * n_peers,))]
    res = pl.pallas_call(
        body, name=name,
        out_shape=(*[pltpu.HBM(a.shape, a.dtype) for a in flat], *sem_shapes, SDS((8, 128), F32)),
        in_specs=[_HBM] * n_flat,
        out_specs=(*[_HBM] * n_flat, *[_SEM] * (2 * n_grp), pl.BlockSpec(memory_space=pltpu.VMEM)),
        input_output_aliases={i: i for i in range(n_flat)},
        compiler_params=pltpu.CompilerParams(has_side_effects=_EFFECT),
    )(*[pltpu.with_memory_space_constraint(a, pltpu.HBM) for a in flat])
    handles, at = [], 0
    for gi, n in enumerate(sizes):
        handles.append((res[n_flat + 2 * gi], res[n_flat + 2 * gi + 1], list(res[at:at + n]), list(res[at + n:at + 2 * n])))
        at += 2 * n
    return handles, res[-1]


def _split_wait(handle, after, copies_fn, name):
    send_sems, recv_sems, srcs, lands = handle
    n = len(srcs)
    after = list(after) if isinstance(after, (list, tuple)) else [after]

    def body(*refs):
        out, inc = copies_fn(refs[:n], refs[n:2 * n], refs[2 * n], refs[2 * n + 1])
        for cp in out:
            cp.wait_send()
        for cp in inc:
            cp.wait_recv()

    flat = list(srcs) + list(lands)
    res = pl.pallas_call(
        body, name=name,
        out_shape=tuple(pltpu.HBM(a.shape, a.dtype) for a in flat),
        in_specs=[_HBM] * (2 * n) + [_SEM, _SEM] + [pl.BlockSpec(memory_space=pl.ANY)] * len(after),
        out_specs=tuple([_HBM] * (2 * n)),
        input_output_aliases={i: i for i in range(2 * n)},
        compiler_params=pltpu.CompilerParams(has_side_effects=_EFFECT),
    )(*flat, send_sems, recv_sems, *after)
    return list(res[n:])


def _landing(own, slot, n_slots):
    return lax.dynamic_update_index_in_dim(lax.empty((n_slots,) + own.shape, own.dtype), own, slot, 0)


def _ssm_discretize(lam_re, lam_im, log_dt, b_re, b_im):
    lr = jnp.minimum(lam_re, LAMBDA_RE_MAX)
    li = lam_im
    dt = jnp.exp(log_dt)[:, None]
    mag = jnp.exp(lr * dt)
    ar, ai = mag * jnp.cos(li * dt), mag * jnp.sin(li * dt)
    den = lr * lr + li * li
    nr, ni = ar - 1.0, ai
    gr, gi = (nr * lr + ni * li) / den, (ni * lr - nr * li) / den
    bbr = gr[:, :, None] * b_re - gi[:, :, None] * b_im
    bbi = gr[:, :, None] * b_im + gi[:, :, None] * b_re
    return ar, ai, bbr, bbi


def _pair_lanes(t):
    return t.reshape(N_PAIR, 2 * SSM_STATE)


def _chan_state_blocks(t_gcp):
    t = t_gcp.reshape(N_PAIR, 2, SSM_GROUP, SSM_STATE)
    eye2 = jnp.eye(2, dtype=t.dtype)
    blk = jnp.einsum("rgcp,gh->rgchp", t, eye2).reshape(N_PAIR, 2 * SSM_GROUP, 2 * SSM_STATE)
    place = jax.nn.one_hot(jnp.arange(N_PAIR) % PAIRS_PER_CHUNK, PAIRS_PER_CHUNK, dtype=t.dtype)
    return jnp.einsum("rcl,rj->rjcl", blk, place).reshape(N_PAIR, 128, 2 * SSM_STATE)


def _chan_state_unblock(t):
    t = t.reshape(N_PAIR, PAIRS_PER_CHUNK, 2, SSM_GROUP, 2, SSM_STATE)
    place = jax.nn.one_hot(jnp.arange(N_PAIR) % PAIRS_PER_CHUNK, PAIRS_PER_CHUNK, dtype=t.dtype)
    t = jnp.einsum("rjgchp,rj->rgchp", t, place)
    t = jnp.einsum("rgchp,gh->rgcp", t, jnp.eye(2, dtype=t.dtype))
    return t.reshape(SSM_NG, SSM_GROUP, SSM_STATE)


def _scan_tables(zr, zi, reverse, seg):
    zr, zi = _pair_lanes(zr), _pair_lanes(-zi if reverse else zi)
    a = (jnp.exp(zr) * jnp.cos(zi), jnp.exp(zr) * jnp.sin(zi))
    cmul = lambda p, q: (p[0] * q[0] - p[1] * q[1], p[0] * q[1] + p[1] * q[0])
    big, square, bits = None, a, seg
    while bits:
        if bits & 1:
            big = square if big is None else cmul(big, square)
        square, bits = cmul(square, square), bits >> 1
    powers = [a, big]
    for _ in range(2):
        powers.append(cmul(powers[-1], powers[-1]))
    rows = jnp.arange(8)[None, :, None]
    tiles = [jnp.broadcast_to(part[:, None, :], (N_PAIR, 8, 128)) for part in powers[0]]
    for lvl, step in enumerate((1, 2, 4)):
        keep = (rows <= 7 - step) if reverse else (rows >= step)
        for part in powers[1 + lvl]:
            tiles.append(jnp.where(keep, part[:, None, :], 0.0))
    return jnp.stack(tiles, axis=1)


def _pairs_to_chunks(t):
    n_chunk = N_PAIR // PAIRS_PER_CHUNK
    return jnp.swapaxes(t.reshape(n_chunk, PAIRS_PER_CHUNK, 128, 256), 1, 2).reshape(n_chunk, 128, XW)


def _chunks_to_pairs(t):
    n_chunk = N_PAIR // PAIRS_PER_CHUNK
    return jnp.swapaxes(t.reshape(n_chunk, 128, PAIRS_PER_CHUNK, 256), 1, 2).reshape(N_PAIR, 128, 256)


def _ssm_operands(w, lp):
    seg = _row_tile(lp, SSM_TILES) // 8
    ar, ai, bbr, bbi = _ssm_discretize(w["ssm_lambda_re"], w["ssm_lambda_im"], w["ssm_log_dt"], w["ssm_b_re"], w["ssm_b_im"])
    b_blk = jnp.concatenate([_chan_state_blocks(jnp.swapaxes(bbr, 1, 2)), _chan_state_blocks(jnp.swapaxes(bbi, 1, 2))], axis=2)
    c_blk = jnp.concatenate([_chan_state_blocks(w["ssm_c_re"]), -_chan_state_blocks(w["ssm_c_im"])], axis=2)
    dt = jnp.exp(w["ssm_log_dt"])[:, None]
    zr, zi = jnp.minimum(w["ssm_lambda_re"], LAMBDA_RE_MAX) * dt, w["ssm_lambda_im"] * dt
    b_cat, c_cat = _pairs_to_chunks(b_blk).astype(BF16), _pairs_to_chunks(c_blk).astype(BF16)
    return (b_cat, jnp.swapaxes(b_cat, 1, 2), c_cat, jnp.swapaxes(c_cat, 1, 2),
            _scan_tables(zr, zi, False, seg), _scan_tables(zr, zi, True, seg))


def _local_step(x, target, w, late_weights, on_grads):
    n_ex, seq, _ = x.shape
    lp = seq + BLOCK
    nb = lp // BLOCK
    n_rows = n_ex * lp
    g = {}

    head = jnp.concatenate([jnp.zeros((PAD, D_MODEL), F32), w["meta_tokens"]], axis=0)
    h0 = jnp.concatenate([jnp.broadcast_to(head[None], (n_ex, BLOCK, D_MODEL)), x], axis=1).reshape(n_rows, D_MODEL)

    qkv, hn_a = _rms_mm_cols(h0, w["attn_norm_w"], w["attn_w_qkv"], "qkv_fwd")
    att, lse = _attn_fwd(qkv, w["attn_sinks"], n_ex, nb)
    h1 = _mm_acc(att, w["attn_w_o"], False, "attn_out_fwd", res=h0)
    w = {**w, **late_weights(0, att)}
    h2, a0, hn_m0, u = _mlp_fwd(h1, w["mlp_norm_w"][0:1], w["mlp_w_up"][0], w["mlp_w_down"][0], "mlp0_fwd",
                                next_norm=w["ssm_norm_w"])
    late = late_weights(1, h2)
    w["ssm_w_glu"] = late["ssm_w_glu"]
    w["mlp_w_up"], w["mlp_w_down"] = w["mlp_w_up"] + late["mlp_w_up"], w["mlp_w_down"] + late["mlp_w_down"]

    ops = w["ssm_operands"] if "ssm_operands" in w else _ssm_operands(w, lp)
    b_pad, bt_pad, ct_pad, c_pad, tab_fwd, tab_rev = ops
    yg, y, xs, u_seg = _ssm_fwd(u, b_pad, c_pad, tab_fwd, w["ssm_d"], n_ex, lp)
    z = _mm_cols(yg, w["ssm_w_glu"], False, "glu_mm_fwd")
    h4, a1, hn_m1, h3 = _mlp_fwd(h2, w["mlp_norm_w"][1:2], w["mlp_w_up"][1], w["mlp_w_down"][1], "mlp1_fwd", glu_z=z)

    dh4, loss_tile, dnorm_f = _loss_head(h4, w["final_norm_w"], target.reshape(n_ex * seq, D_MODEL), n_ex, nb)

    def mlp_bwd(dh_out, h_in, a, hn, layer, tag, norm_w):
        dhn, dw_up, dw_down = None, None, None
        for s in range(N_CHIPS):
            final = s == N_CHIPS - 1
            res = _mlp_bwd_shard(s, dh_out, a, hn, dhn, h_in if final else None, norm_w,
                                 w["mlp_w_up"][layer], w["mlp_w_down"][layer], dw_up, dw_down, f"{tag}_bwd{s}")
            dhn, dw_up, dw_down = res[:3]
        return dhn, res[3], dw_up, dw_down

    dh3, dnorm_m1, dwu1, dwd1 = mlp_bwd(dh4, h3, a1, hn_m1, 1, "mlp1", w["mlp_norm_w"][1:2])
    tok = on_grads("mlp1", {"mlp_w_up": dwu1, "mlp_w_down": dwd1})
    dz, dyg = _glu_bwd(dh3, z, w["ssm_w_glu"])
    g["ssm_w_glu"] = _mm_tn(yg, dz, N_CHIPS, False, "glu_mm_dw")
    du, db_blk, dc_blk, da_t, dd_t = _ssm_bwd(dyg, y, u_seg, xs, ct_pad, bt_pad, tab_rev, w["ssm_d"] + tok, n_ex, lp)
    dh2, dnorm_s = _rms_bwd_call(du, h2, w["ssm_norm_w"], dh3, "ssm_norm_bwd")
    db_blk, dc_blk = _chunks_to_pairs(db_blk), _chunks_to_pairs(dc_blk)
    g["ssm_c_re"] = _chan_state_unblock(dc_blk[:, :, 0:128])
    g["ssm_c_im"] = -_chan_state_unblock(dc_blk[:, :, 128:256])
    g_bbr = jnp.swapaxes(_chan_state_unblock(db_blk[:, :, 0:128]), 1, 2)
    g_bbi = jnp.swapaxes(_chan_state_unblock(db_blk[:, :, 128:256]), 1, 2)
    g_a = jnp.sum(da_t, axis=2).reshape(N_PAIR, 2, 2, SSM_STATE)
    g_ar, g_ai = g_a[:, 0].reshape(SSM_NG, SSM_STATE), g_a[:, 1].reshape(SSM_NG, SSM_STATE)
    _, vjp = jax.vjp(_ssm_discretize, w["ssm_lambda_re"], w["ssm_lambda_im"], w["ssm_log_dt"], w["ssm_b_re"], w["ssm_b_im"])
    g["ssm_lambda_re"], g["ssm_lambda_im"], g["ssm_log_dt"], g["ssm_b_re"], g["ssm_b_im"] = vjp((g_ar, g_ai, g_bbr, g_bbi))
    tok = on_grads("ssm", g)
    g = {}
    dh1, dnorm_m0, dwu0, dwd0 = mlp_bwd(dh2, h1, a0, hn_m0, 0, "mlp0", w["mlp_norm_w"][0:1] + tok)
    datt = _mm_cols(dh1, w["attn_w_o"], True, "attn_out_dx")
    dw_o = _mm_tn(att, dh1, N_CHIPS, True, "attn_out_dw")
    tok = on_grads("mlp0", {"mlp_w_up": dwu0, "mlp_w_down": dwd0, "attn_w_o": dw_o})
    dqkv, dsink_rows = _attn_bwd(qkv, w["attn_sinks"] + tok, att, lse, datt, n_ex, nb)
    tok = on_grads("qkv", {"attn_w_qkv": _mm_tn(hn_a, dqkv, N_CHIPS, False, "qkv_dw")})
    dh0, dnorm_a = _mm_acc(dqkv, w["attn_w_qkv"], True, "qkv_dx", rms_bwd=(h0, w["attn_norm_w"] + tok, dh1))

    dh0 = dh0.reshape(n_ex, lp, D_MODEL)
    on_grads("rest", {
        "mlp_norm_w": jnp.stack([jnp.sum(dnorm_m0, axis=0), jnp.sum(dnorm_m1, axis=0)]),
        "final_norm_w": jnp.sum(dnorm_f, axis=0),
        "attn_norm_w": jnp.sum(dnorm_a, axis=0)[None],
        "ssm_norm_w": jnp.sum(dnorm_s, axis=0)[None],
        "attn_sinks": jnp.sum(dsink_rows, axis=0)[None],
        "ssm_d": jnp.sum(dd_t, axis=0)[None],
        "meta_tokens": jnp.sum(dh0[:, PAD:BLOCK], axis=0),
        "loss": loss_tile[0, 0:1]})
    return loss_tile, dh0[:, BLOCK:]


_SHARDED_SMALL = ("meta_tokens", "ssm_norm_w", "ssm_d")
_REP_SSM = ("ssm_lambda_re", "ssm_lambda_im", "ssm_log_dt", "ssm_b_re", "ssm_b_im", "ssm_c_re", "ssm_c_im")
_REP_MISC = ("attn_norm_w", "attn_sinks", "mlp_norm_w", "final_norm_w")
_BIG = ("attn_w_qkv", "attn_w_o", "ssm_w_glu", "mlp_w_up", "mlp_w_down")


def _pack(parts, cols):
    flat = jnp.concatenate([p.reshape(-1) for p in parts])
    rows = -(-flat.shape[0] // (8 * cols)) * 8
    return jnp.pad(flat, (0, rows * cols - flat.shape[0])).reshape(rows, cols)


def _unpack(packed, like):
    flat = packed.reshape(-1)
    out, at = [], 0
    for p in like:
        out.append(flat[at:at + p.size].reshape(p.shape))
        at += p.size
    return out


def kernel(x, meta_tokens, attn_norm_w, attn_w_qkv, attn_sinks, attn_w_o, ssm_norm_w, ssm_lambda_re, ssm_lambda_im, ssm_log_dt, ssm_b_re, ssm_b_im, ssm_c_re, ssm_c_im, ssm_d, ssm_w_glu, mlp_norm_w, mlp_w_up, mlp_w_down, final_norm_w, loss_target, m_meta_tokens, m_attn_norm_w, m_attn_w_qkv, m_attn_sinks, m_attn_w_o, m_ssm_norm_w, m_ssm_lambda_re, m_ssm_lambda_im, m_ssm_log_dt, m_ssm_b_re, m_ssm_b_im, m_ssm_c_re, m_ssm_c_im, m_ssm_d, m_ssm_w_glu, m_mlp_norm_w, m_mlp_w_up, m_mlp_w_down, m_final_norm_w, v_meta_tokens, v_attn_norm_w, v_attn_w_qkv, v_attn_sinks, v_attn_w_o, v_ssm_norm_w, v_ssm_lambda_re, v_ssm_lambda_im, v_ssm_log_dt, v_ssm_b_re, v_ssm_b_im, v_ssm_c_re, v_ssm_c_im, v_ssm_d, v_ssm_w_glu, v_mlp_norm_w, v_mlp_w_up, v_mlp_w_down, v_final_norm_w):
    names = ("meta_tokens", "attn_norm_w", "attn_w_qkv", "attn_sinks", "attn_w_o", "ssm_norm_w", "ssm_lambda_re",
             "ssm_lambda_im", "ssm_log_dt", "ssm_b_re", "ssm_b_im", "ssm_c_re", "ssm_c_im", "ssm_d", "ssm_w_glu",
             "mlp_norm_w", "mlp_w_up", "mlp_w_down", "final_norm_w")
    wts = dict(zip(names, (meta_tokens, attn_norm_w, attn_w_qkv, attn_sinks, attn_w_o, ssm_norm_w, ssm_lambda_re,
                           ssm_lambda_im, ssm_log_dt, ssm_b_re, ssm_b_im, ssm_c_re, ssm_c_im, ssm_d, ssm_w_glu,
                           mlp_norm_w, mlp_w_up, mlp_w_down, final_norm_w)))
    mom = dict(zip(names, (m_meta_tokens, m_attn_norm_w, m_attn_w_qkv, m_attn_sinks, m_attn_w_o, m_ssm_norm_w,
                           m_ssm_lambda_re, m_ssm_lambda_im, m_ssm_log_dt, m_ssm_b_re, m_ssm_b_im, m_ssm_c_re,
                           m_ssm_c_im, m_ssm_d, m_ssm_w_glu, m_mlp_norm_w, m_mlp_w_up, m_mlp_w_down, m_final_norm_w)))
    var = dict(zip(names, (v_meta_tokens, v_attn_norm_w, v_attn_w_qkv, v_attn_sinks, v_attn_w_o, v_ssm_norm_w,
                           v_ssm_lambda_re, v_ssm_lambda_im, v_ssm_log_dt, v_ssm_b_re, v_ssm_b_im, v_ssm_c_re,
                           v_ssm_c_im, v_ssm_d, v_ssm_w_glu, v_mlp_norm_w, v_mlp_w_up, v_mlp_w_down, v_final_norm_w)))

    my_chip = 2 * lax.axis_index("x") + lax.axis_index("y")
    my_dev = 2 * my_chip + lax.axis_index("c")
    small_mine = _pack([wts[n] for n in _SHARDED_SMALL], 128)
    first = [attn_w_qkv.astype(BF16), attn_w_o.astype(BF16), small_mine]
    with_landing = lambda srcs: (srcs, [_landing(a, my_chip, N_CHIPS) for a in srcs])
    handles, token = _split_start([with_landing(first)], _gather_copies, N_GATHER_PEERS, "gather_start_first")
    up16, down16 = (mlp_w_up + token[0, 0]).astype(BF16), (mlp_w_down + token[0, 0]).astype(BF16)
    mlp0 = [up16[0:1], down16[0:1]]
    rest = [(ssm_w_glu + token[0, 0]).astype(BF16), up16[1:2], down16[1:2]]
    later, _ = _split_start([with_landing(mlp0), with_landing(rest)], _gather_copies, N_GATHER_PEERS, "gather_start_later")
    handles = handles + later
    full = {n: wts[n] for n in _REP_MISC}
    full["final_norm_w"] = final_norm_w[None]
    for n in _REP_SSM:
        full[n] = wts[n][0]
    full["ssm_operands"] = _ssm_operands(full, x.shape[1] + BLOCK)
    got = _split_wait(handles[0], full["ssm_operands"], _gather_copies, "gather_wait_first")
    full["attn_w_qkv"], full["attn_w_o"] = got[0], got[1]
    smalls = [_unpack(got[2][s], [wts[n] for n in _SHARDED_SMALL]) for s in range(N_CHIPS)]
    for k, n in enumerate(_SHARDED_SMALL):
        full[n] = jnp.concatenate([smalls[s][k] for s in range(N_CHIPS)], axis=1)

    def late_weights(stage, after):
        if stage == 0:
            up, down = _split_wait(handles[1], after, _gather_copies, "gather_wait_mlp0")
            return {"mlp_w_up": [up], "mlp_w_down": [down]}
        glu, up, down = _split_wait(handles[2], after, _gather_copies, "gather_wait_rest")
        return {"ssm_w_glu": glu, "mlp_w_up": [up], "mlp_w_down": [down]}

    def shard_cols(t):
        return jnp.swapaxes(t.reshape(t.shape[0], N_CHIPS, t.shape[1] // N_CHIPS), 0, 1)

    pending = {}

    def on_grads(tag, g):
        scatter = [g[n] for n in _BIG if n in g]
        whole = []
        if tag == "ssm":
            whole = [_pack([g[n] for n in _REP_SSM], D_MODEL)]
        if tag == "rest":
            parts = [shard_cols(g[n]) for n in _SHARDED_SMALL]
            scatter = [jnp.stack([_pack([p[s] for p in parts], 128) for s in range(N_CHIPS)])]
            whole = [_pack([g[n] for n in _REP_MISC] + [g["loss"]], D_MODEL)]
        srcs = scatter + whole
        lands = [_landing(lax.dynamic_index_in_dim(a, my_chip, 0, keepdims=False), my_dev, N_DEV) for a in scatter]
        lands += [_landing(a, my_dev, N_DEV) for a in whole]
        hs, token = _split_start([(srcs, lands)], _exchange_copies(len(scatter)), N_EXCHANGE_PEERS, "exchange_start_" + tag)
        pending[tag] = (hs[0], len(scatter))
        return token[0, 0]

    _, grad_x = _local_step(x, loss_target, full, late_weights, on_grads)

    recv = {}
    for tag, (handle, n_scatter) in pending.items():
        recv[tag] = _split_wait(handle, grad_x, _exchange_copies(n_scatter), "exchange_wait_" + tag)
    loss = jnp.sum(recv["rest"][1].reshape(N_DEV, -1)[:, sum(wts[n].size for n in _REP_MISC)])

    out = {}

    def update(tag, pieces, w2, m2, v2):
        return _adamw(pieces, w2, m2, v2, "adamw_" + tag)

    def update_weight(n, pieces):
        shp = wts[n].shape
        r2 = (math.prod(shp[:-1]), shp[-1])
        res = update(n, pieces, wts[n].reshape(r2), mom[n].reshape(r2), var[n].reshape(r2))
        out[n] = [t.reshape(shp) for t in res]

    update_weight("mlp_w_up", [recv["mlp0"][1], recv["mlp1"][0]])
    update_weight("mlp_w_down", [recv["mlp0"][2], recv["mlp1"][1]])
    update_weight("attn_w_o", [recv["mlp0"][0]])
    update_weight("ssm_w_glu", [recv["ssm"][0]])
    update_weight("attn_w_qkv", [recv["qkv"][0]])
    for tag, group, pieces, cols in (("small", _SHARDED_SMALL, recv["rest"][0], 128),
                                     ("rep_ssm", _REP_SSM, recv["ssm"][1], D_MODEL),
                                     ("rep_misc", _REP_MISC, recv["rest"][1], D_MODEL)):
        like = [wts[n] for n in group]
        res = update(tag, [pieces], _pack(like, cols), _pack([mom[n] for n in group], cols),
                     _pack([var[n] for n in group], cols))
        for k, n in enumerate(group):
            out[n] = [_unpack(t, like)[k] for t in res]

    return (loss, grad_x, *[out[n][0] for n in names], *[out[n][1] for n in names],
            *[out[n][2] for n in names], *[out[n][3] for n in names])
```

```python
import functools
import math

import jax
import jax.numpy as jnp
from jax import lax
from jax.experimental import pallas as pl
from jax.experimental.pallas import tpu as pltpu

F32 = jnp.float32
BF16 = jnp.bfloat16
SDS = jax.ShapeDtypeStruct

D_MODEL = 1024
N_HEADS = 16
N_KV = 4
GQA = N_HEADS // N_KV
HEAD_DIM = 64
BLOCK = 128
N_META = 16
PAD = BLOCK - N_META
QKV_DIM = (N_HEADS + 2 * N_KV) * HEAD_DIM
KV_DIM = 2 * N_KV * HEAD_DIM
D_FF = 4 * D_MODEL
N_CHIPS = 4
N_DEV = 8
SSM_GROUP = 16
SSM_NG = D_MODEL // SSM_GROUP
SSM_STATE = 64
N_PAIR = SSM_NG // 2
PAIRS_PER_CHUNK = 4
RMS_EPS = 1e-6
NEG_INF = -1e30
LAMBDA_RE_MAX = -1e-4
ADAM_LR, ADAM_B1, ADAM_B2, ADAM_EPS, ADAM_WD, ADAM_STEP = 0.001, 0.9, 0.999, 1e-08, 0.01, 10

TM = 384
MM_TILES = (1056, 768, 384)
MLP_FWD_TILES = (384,)
MLP_BWD_TILES = (768, 384)
TN_TILES = (1408, 768, 384)
VMEM_LIMIT = 56 * 1024 * 1024


def _params(n_grid):
    return pltpu.CompilerParams(dimension_semantics=("arbitrary",) * n_grid, vmem_limit_bytes=VMEM_LIMIT)


def _row_tile(n_rows, tiles):
    return next(t for t in tiles if n_rows % t == 0)


def _rms(h, w):
    r = lax.rsqrt(jnp.mean(h * h, axis=-1, keepdims=True) + RMS_EPS)
    return h * r * w


def _rms_bwd(dhn, h, w):
    r = lax.rsqrt(jnp.mean(h * h, axis=-1, keepdims=True) + RMS_EPS)
    g = dhn * w
    proj = jnp.sum(g * h, axis=-1, keepdims=True) * (1.0 / D_MODEL)
    return r * g - h * (r * r * r) * proj, dhn * h * r


def _fold8(t):
    return jnp.sum(t.reshape(t.shape[0] // 8, 8, t.shape[1]), axis=0)


def _gelu(y):
    return 0.5 * y * (1.0 + jnp.tanh(0.7978845608028654 * (y + 0.044715 * y * y * y)))


def _gelu_grad(y):
    t = jnp.tanh(0.7978845608028654 * (y + 0.044715 * y * y * y))
    return 0.5 * (1.0 + t) + 0.5 * y * (1.0 - t * t) * 0.7978845608028654 * (1.0 + 3.0 * 0.044715 * y * y)


def _w4_spec(w4):
    n_sh, _, k, n = w4.shape
    return pl.BlockSpec((n_sh, None, k, n), lambda i: (0, 0, 0, 0))


def _rms_mm_cols(h, wn, w4, name):
    n_rows = h.shape[0]
    n_sh, _, k, n = w4.shape
    tm = _row_tile(n_rows, MM_TILES)

    def body(h_ref, wn_ref, w_ref, o_ref, hn_ref):
        hn = _rms(h_ref[...], wn_ref[...]).astype(BF16)
        hn_ref[...] = hn
        for s in range(n_sh):
            o_ref[:, s * n:(s + 1) * n] = jnp.dot(hn, w_ref[s], preferred_element_type=F32).astype(o_ref.dtype)

    return pl.pallas_call(
        body, name=name, grid=(n_rows // tm,),
        in_specs=[pl.BlockSpec((tm, k), lambda i: (i, 0)), pl.BlockSpec((1, k), lambda i: (0, 0)), _w4_spec(w4)],
        out_specs=[pl.BlockSpec((tm, n_sh * n), lambda i: (i, 0)), pl.BlockSpec((tm, k), lambda i: (i, 0))],
        out_shape=[SDS((n_rows, n_sh * n), BF16), SDS((n_rows, k), BF16)],
        compiler_params=_params(1),
    )(h, wn, w4)


def _mm_cols(x, w4, trans_w, name):
    n_rows, kx = x.shape
    tm = _row_tile(n_rows, MM_TILES)
    n_sh, _, k, n = w4.shape
    n_out = k if trans_w else n
    dims = (((1,), (1,)), ((), ())) if trans_w else (((1,), (0,)), ((), ()))

    def body(x_ref, w_ref, o_ref):
        x16 = x_ref[...].astype(BF16)
        for s in range(n_sh):
            o_ref[:, s * n_out:(s + 1) * n_out] = lax.dot_general(
                x16, w_ref[s], dims, preferred_element_type=F32).astype(o_ref.dtype)

    return pl.pallas_call(
        body, name=name, grid=(n_rows // tm,),
        in_specs=[pl.BlockSpec((tm, kx), lambda i: (i, 0)), _w4_spec(w4)],
        out_specs=pl.BlockSpec((tm, n_sh * n_out), lambda i: (i, 0)),
        out_shape=SDS((n_rows, n_sh * n_out), BF16),
        compiler_params=_params(1),
    )(x, w4)


def _mm_acc(x, w4, trans_w, name, res=None, rms_bwd=None):
    n_rows = x.shape[0]
    tm = _row_tile(n_rows, MM_TILES)
    n_sh, _, k, n = w4.shape
    kx, n_out = (n, k) if trans_w else (k, n)
    dims = (((1,), (1,)), ((), ())) if trans_w else (((1,), (0,)), ((), ()))

    def body(*refs):
        if rms_bwd is not None:
            x_ref, w_ref, h_ref, wn_ref, dres_ref, o_ref, dw_ref = refs
        elif res is not None:
            x_ref, w_ref, res_ref, o_ref = refs
        else:
            x_ref, w_ref, o_ref = refs
        acc = None
        for s in range(n_sh):
            part = lax.dot_general(x_ref[:, s * kx:(s + 1) * kx].astype(BF16), w_ref[s], dims, preferred_element_type=F32)
            acc = part if acc is None else acc + part
        if rms_bwd is not None:
            dh, dw_rows = _rms_bwd(acc, h_ref[...], wn_ref[...])
            o_ref[...] = (dres_ref[...] + dh).astype(o_ref.dtype)

            @pl.when(pl.program_id(0) == 0)
            def _():
                dw_ref[...] = jnp.zeros_like(dw_ref)

            dw_ref[...] += _fold8(dw_rows)
        elif res is not None:
            o_ref[...] = (res_ref[...] + acc).astype(o_ref.dtype)
        else:
            o_ref[...] = acc.astype(o_ref.dtype)

    row = lambda i: (i, 0)
    in_specs = [pl.BlockSpec((tm, n_sh * kx), row), _w4_spec(w4)]
    args = [x, w4]
    out_specs = pl.BlockSpec((tm, n_out), row)
    out_shape = SDS((n_rows, n_out), F32)
    if rms_bwd is not None:
        h, wn, dres = rms_bwd
        in_specs += [pl.BlockSpec((tm, n_out), row), pl.BlockSpec((1, n_out), lambda i: (0, 0)),
                     pl.BlockSpec((tm, n_out), row)]
        args += [h, wn, dres]
        out_specs = [out_specs, pl.BlockSpec((8, n_out), lambda i: (0, 0))]
        out_shape = [out_shape, SDS((8, n_out), F32)]
    elif res is not None:
        in_specs.append(pl.BlockSpec((tm, n_out), row))
        args.append(res)
    return pl.pallas_call(
        body, name=name, grid=(n_rows // tm,), in_specs=in_specs, out_specs=out_specs, out_shape=out_shape,
        compiler_params=_params(1),
    )(*args)


def _mm_tn(a, b, n_sh, a_sharded, name):
    n_rows = a.shape[0]
    tm = _row_tile(n_rows, TN_TILES)
    ka = a.shape[1] // n_sh if a_sharded else a.shape[1]
    nb = b.shape[1] if a_sharded else b.shape[1] // n_sh
    n_i = n_rows // tm

    def body(a_ref, b_ref, o_ref, acc):
        i = pl.program_id(0)

        @pl.when(i == 0)
        def _():
            acc[...] = jnp.zeros_like(acc)

        for s in range(n_sh):
            a_s = a_ref[:, s * ka:(s + 1) * ka] if a_sharded else a_ref[...]
            b_s = b_ref[...] if a_sharded else b_ref[:, s * nb:(s + 1) * nb]
            acc[s] += lax.dot_general(a_s.astype(BF16), b_s.astype(BF16), (((0,), (0,)), ((), ())),
                                      preferred_element_type=F32)

        @pl.when(i == n_i - 1)
        def _():
            o_ref[...] = acc[...].astype(o_ref.dtype)

    return pl.pallas_call(
        body, name=name, grid=(n_i,),
        in_specs=[pl.BlockSpec((tm, a.shape[1]), lambda i: (i, 0)), pl.BlockSpec((tm, b.shape[1]), lambda i: (i, 0))],
        out_specs=pl.BlockSpec((n_sh, ka, nb), lambda i: (0, 0, 0)),
        out_shape=SDS((n_sh, ka, nb), BF16),
        scratch_shapes=[pltpu.VMEM((n_sh, ka, nb), F32)], compiler_params=_params(1),
    )(a, b)


def _mlp_fwd(h, wn, w_up4, w_down4, name, next_norm=None, glu_z=None, loss=None):
    n_rows = h.shape[0]
    tm = _row_tile(n_rows, MLP_FWD_TILES)
    n_sh = w_up4.shape[0]
    f_sh = D_FF // n_sh
    w_down = w_down4.reshape(D_FF, D_MODEL)
    per_tile = tm // BLOCK

    def body(*refs):
        refs = list(refs)
        h_ref, wn_ref, wu_ref, wd_ref = refs[:4]
        at = 4
        if next_norm is not None:
            nn_ref = refs[at]
            at += 1
        if glu_z is not None:
            z_ref = refs[at]
            at += 1
        if loss is not None:
            fw_ref, t_refs = refs[at], refs[at + 1:at + 1 + per_tile]
            at += 1 + per_tile
        o_ref, a_ref, hn_ref = refs[at:at + 3]
        at += 3
        if next_norm is not None:
            u_ref = refs[at]
            at += 1
        if glu_z is not None:
            hin_ref = refs[at]
            at += 1
        if loss is not None:
            loss_ref, dfw_ref = refs[at:at + 2]
            at += 2
        act_s = refs[at]
        h_in = h_ref[...]
        if glu_z is not None:
            h_in = h_in + z_ref[:, 0:D_MODEL].astype(F32) * jax.nn.sigmoid(z_ref[:, D_MODEL:2 * D_MODEL].astype(F32))
            hin_ref[...] = h_in
        hn = _rms(h_in, wn_ref[...]).astype(BF16)
        hn_ref[...] = hn
        for s in range(n_sh):
            cols = slice(s * f_sh, (s + 1) * f_sh)
            a = jnp.dot(hn, wu_ref[s], preferred_element_type=F32)
            a_ref[:, cols] = a.astype(BF16)
            act = jnp.maximum(a, 0.0)
            act_s[:, cols] = (act * act).astype(BF16)
        out = h_in + jnp.dot(act_s[...], wd_ref[...], preferred_element_type=F32)
        if next_norm is not None:
            u_ref[...] = _rms(out, nn_ref[...])
        if loss is None:
            o_ref[...] = out
        else:
            i = pl.program_id(0)
            first_of_example = i % tiles_per_example == 0

            @pl.when(i == 0)
            def _():
                loss_ref[...] = jnp.zeros_like(loss_ref)
                dfw_ref[...] = jnp.zeros_like(dfw_ref)

            def block(k):
                rows = slice(k * BLOCK, (k + 1) * BLOCK)
                diff = _rms(out[rows], fw_ref[...]) - t_refs[k][...]
                loss_ref[...] += 0.5 * jnp.sum(diff * diff) * (1.0 / D_MODEL)
                dh, dw_rows = _rms_bwd(diff * (1.0 / D_MODEL), out[rows], fw_ref[...])
                o_ref[rows, :] = dh
                dfw_ref[...] += _fold8(dw_rows)

            @pl.when(first_of_example)
            def _():
                o_ref[0:BLOCK, :] = jnp.zeros((BLOCK, D_MODEL), F32)

            pl.when(jnp.logical_not(first_of_example))(lambda: block(0))
            for k in range(1, per_tile):
                block(k)

    row = lambda i: (i, 0)
    vec = pl.BlockSpec((1, D_MODEL), lambda i: (0, 0))
    in_specs = [pl.BlockSpec((tm, D_MODEL), row), vec,
                pl.BlockSpec((n_sh, None, D_MODEL, f_sh), lambda i: (0, 0, 0, 0), pipeline_mode=pl.Buffered(1)),
                pl.BlockSpec((D_FF, D_MODEL), lambda i: (0, 0), pipeline_mode=pl.Buffered(1))]
    out_specs = [pl.BlockSpec((tm, D_MODEL), row), pl.BlockSpec((tm, D_FF), row), pl.BlockSpec((tm, D_MODEL), row)]
    out_shape = [SDS((n_rows, D_MODEL), F32), SDS((n_rows, D_FF), BF16), SDS((n_rows, D_MODEL), BF16)]
    args = [h, wn, w_up4, w_down]
    if next_norm is not None:
        in_specs.append(vec)
        args.append(next_norm)
    if glu_z is not None:
        in_specs.append(pl.BlockSpec((tm, 2 * D_MODEL), row))
        args.append(glu_z)
    for extra in (next_norm, glu_z):
        if extra is not None:
            out_specs.append(pl.BlockSpec((tm, D_MODEL), row))
            out_shape.append(SDS((n_rows, D_MODEL), F32))
    if loss is not None:
        final_wn, target, lp = loss
        tiles_per_example = lp // tm
        real_blocks = lp // BLOCK - 1

        def t_spec(k):
            return pl.BlockSpec((BLOCK, D_MODEL), lambda i: (
                (i // tiles_per_example) * real_blocks + jnp.maximum(per_tile * (i % tiles_per_example) + k - 1, 0), 0))

        in_specs += [vec] + [t_spec(k) for k in range(per_tile)]
        args += [final_wn] + [target] * per_tile
        out_specs += [pl.BlockSpec((8, 128), lambda i: (0, 0)), pl.BlockSpec((8, D_MODEL), lambda i: (0, 0))]
        out_shape += [SDS((8, 128), F32), SDS((8, D_MODEL), F32)]
    return pl.pallas_call(
        body, name=name, grid=(n_rows // tm,), in_specs=in_specs, out_specs=out_specs, out_shape=out_shape,
        scratch_shapes=[pltpu.VMEM((tm, D_FF), BF16)],
        compiler_params=_params(1),
    )(*args)


def _mlp_bwd_shard(s, dh, a, hn, dhn_prev, h, wn, w_up4, w_down4, dw_up_buf, dw_down_buf, name):
    n_rows = dh.shape[0]
    n_sh = w_up4.shape[0]
    f_sh = D_FF // n_sh
    tm = _row_tile(n_rows, MLP_BWD_TILES)
    n_i = n_rows // tm
    last = h is not None
    nt = (((1,), (1,)), ((), ()))
    tn = (((0,), (0,)), ((), ()))

    def body(*refs):
        refs = list(refs)
        dh_ref, a_ref, hn_ref, wu_ref, wd_ref = refs[:5]
        at = 5
        prev_ref = None
        if dhn_prev is not None:
            prev_ref = refs[at]
            at += 1
        if last:
            h_ref, wn_ref = refs[at:at + 2]
            at += 2
        if dw_up_buf is not None:
            at += 2
        o_ref, dwu_ref, dwd_ref = refs[at:at + 3]
        at += 3
        if last:
            dnorm_ref = refs[at]
            at += 1
        acc_u, acc_d = refs[at:at + 2]
        i = pl.program_id(0)

        @pl.when(i == 0)
        def _():
            acc_u[...] = jnp.zeros_like(acc_u)
            acc_d[...] = jnp.zeros_like(acc_d)
            if last:
                dnorm_ref[...] = jnp.zeros_like(dnorm_ref)

        dh16 = dh_ref[...].astype(BF16)
        r = jnp.maximum(a_ref[...].astype(F32), 0.0)
        dact = lax.dot_general(dh16, wd_ref[...], nt, preferred_element_type=F32)
        da16 = (dact * (2.0 * r)).astype(BF16)
        acc_d[...] += lax.dot_general((r * r).astype(BF16), dh16, tn, preferred_element_type=F32)
        acc_u[...] += lax.dot_general(hn_ref[...], da16, tn, preferred_element_type=F32)
        dhn = lax.dot_general(da16, wu_ref[...], nt, preferred_element_type=F32)
        if prev_ref is not None:
            dhn = dhn + prev_ref[...]
        if last:
            d_rms, dw_rows = _rms_bwd(dhn, h_ref[...], wn_ref[...])
            o_ref[...] = dh_ref[...] + d_rms
            dnorm_ref[...] += _fold8(dw_rows)
        else:
            o_ref[...] = dhn

        @pl.when(i == n_i - 1)
        def _():
            dwu_ref[...] = acc_u[...].astype(BF16)
            dwd_ref[...] = acc_d[...].astype(BF16)

    row = lambda i: (i, 0)
    tile = pl.BlockSpec((tm, D_MODEL), row)
    in_specs = [tile, pl.BlockSpec((tm, f_sh), lambda i: (i, s)), tile,
                pl.BlockSpec((None, None, D_MODEL, f_sh), lambda i: (s, 0, 0, 0)),
                pl.BlockSpec((None, None, f_sh, D_MODEL), lambda i: (s, 0, 0, 0))]
    args = [dh, a, hn, w_up4, w_down4]
    if dhn_prev is not None:
        in_specs.append(tile)
        args.append(dhn_prev)
    if last:
        in_specs += [tile, pl.BlockSpec((1, D_MODEL), lambda i: (0, 0))]
        args += [h, wn]
    aliases = {}
    if dw_up_buf is not None:
        aliases = {len(args): 1, len(args) + 1: 2}
        in_specs += [pl.BlockSpec(memory_space=pl.ANY)] * 2
        args += [dw_up_buf, dw_down_buf]
    out_specs = [tile, pl.BlockSpec((None, D_MODEL, f_sh), lambda i: (s, 0, 0)),
                 pl.BlockSpec((None, f_sh, D_MODEL), lambda i: (s, 0, 0))]
    out_shape = [SDS((n_rows, D_MODEL), F32), SDS((n_sh, D_MODEL, f_sh), BF16), SDS((n_sh, f_sh, D_MODEL), BF16)]
    if last:
        out_specs.append(pl.BlockSpec((8, D_MODEL), lambda i: (0, 0)))
        out_shape.append(SDS((8, D_MODEL), F32))
    return pl.pallas_call(
        body, name=name, grid=(n_i,), in_specs=in_specs, out_specs=out_specs, out_shape=out_shape,
        input_output_aliases=aliases,
        scratch_shapes=[pltpu.VMEM((D_MODEL, f_sh), F32), pltpu.VMEM((f_sh, D_MODEL), F32)],
        compiler_params=_params(1),
    )(*args)


def _attn_masks(n):
    qi = lax.broadcasted_iota(jnp.int32, (BLOCK, 3 * BLOCK), 0)
    col = lax.broadcasted_iota(jnp.int32, (BLOCK, 3 * BLOCK), 1)
    kj = col - BLOCK
    dist = BLOCK + qi - kj
    kmin = jnp.where(n == 0, 2 * BLOCK, jnp.where(n == 1, BLOCK, 0))
    band_ok = (col >= BLOCK) & (dist >= 0) & (dist < BLOCK) & (kj >= kmin)
    q_pos = n * BLOCK + qi - PAD
    meta_ok = (col >= PAD) & (col < BLOCK) & (col - PAD <= q_pos)
    distf = jnp.where(col >= BLOCK, dist, 0).astype(F32)
    return band_ok | meta_ok, distf


def _alibi_slope(h):
    return float(2.0 ** (-8.0 * (h + 1) / N_HEADS))


def _attn_bias(n, bias_s):
    ok, distf = _attn_masks(n)
    for h in range(N_HEADS):
        bias_s[h] = jnp.where(ok, -_alibi_slope(h) * distf, NEG_INF)


def _attn_fwd(qkv, sinks, n_ex, nb):
    n_rows = qkv.shape[0]
    kvb = N_HEADS * HEAD_DIM // KV_DIM

    def body(sink_ref, q_ref, kvm_ref, kvp_ref, kvc_ref, o_ref, lse_ref, k_s, v_s, q_s, bias_s):
        n = pl.program_id(1)

        @pl.when(n <= 2)
        def _():
            _attn_bias(n, bias_s)

        v_s[...] = jnp.ones_like(v_s)
        for part, ref in enumerate((kvm_ref, kvp_ref, kvc_ref)):
            rows = slice(part * BLOCK, (part + 1) * BLOCK)
            k_s[rows, :] = ref[:, 0:N_KV * HEAD_DIM]
            for kv in range(N_KV):
                v_s[rows, kv * 2 * HEAD_DIM:kv * 2 * HEAD_DIM + HEAD_DIM] = \
                    ref[:, (N_KV + kv) * HEAD_DIM:(N_KV + kv + 1) * HEAD_DIM]
        for h in range(N_HEADS):
            q_s[h // GQA, (h % GQA) * BLOCK:(h % GQA + 1) * BLOCK, :] = \
                q_ref[:, h * HEAD_DIM:(h + 1) * HEAD_DIM] * (HEAD_DIM ** -0.5)

        def scores(kv):
            return lax.dot_general(q_s[kv], k_s[:, kv * HEAD_DIM:(kv + 1) * HEAD_DIM], (((1,), (1,)), ((), ())),
                                   preferred_element_type=F32)

        ahead = scores(0)
        for kv in range(N_KV):
            s4 = ahead
            if kv + 1 < N_KV:
                ahead = scores(kv + 1)
            es, ms, sink_es = [], [], []
            for g in range(GQA):
                h = kv * GQA + g
                s = s4[g * BLOCK:(g + 1) * BLOCK] + bias_s[h]
                sink = sink_ref[0, h]
                m = jnp.maximum(jnp.max(s, axis=-1, keepdims=True), sink)
                es.append(jnp.exp(s - m).astype(BF16))
                ms.append(m)
                sink_es.append(jnp.exp(sink - m))
            pv = jnp.dot(jnp.concatenate(es, axis=0), v_s[:, kv * 2 * HEAD_DIM:(kv + 1) * 2 * HEAD_DIM],
                         preferred_element_type=F32)
            for g in range(GQA):
                h = kv * GQA + g
                pg = pv[g * BLOCK:(g + 1) * BLOCK]
                l = pg[:, HEAD_DIM:HEAD_DIM + 1] + sink_es[g]
                o_ref[:, h * HEAD_DIM:(h + 1) * HEAD_DIM] = (pg[:, 0:HEAD_DIM] * (1.0 / l)).astype(BF16)
                lse_ref[:, h:h + 1] = ms[g] + jnp.log(l)

    return pl.pallas_call(
        body, name="attn_fwd", grid=(n_ex, nb),
        in_specs=[pl.BlockSpec(memory_space=pltpu.SMEM),
                  pl.BlockSpec((BLOCK, N_HEADS * HEAD_DIM), lambda b, n: (b * nb + n, 0)),
                  pl.BlockSpec((BLOCK, KV_DIM), lambda b, n: (b * nb, kvb)),
                  pl.BlockSpec((BLOCK, KV_DIM), lambda b, n: (b * nb + jnp.maximum(n - 1, 0), kvb)),
                  pl.BlockSpec((BLOCK, KV_DIM), lambda b, n: (b * nb + n, kvb))],
        out_specs=[pl.BlockSpec((BLOCK, N_HEADS * HEAD_DIM), lambda b, n: (b * nb + n, 0)),
                   pl.BlockSpec((BLOCK, N_HEADS), lambda b, n: (b * nb + n, 0))],
        out_shape=[SDS((n_rows, N_HEADS * HEAD_DIM), BF16), SDS((n_rows, N_HEADS), F32)],
        scratch_shapes=[pltpu.VMEM((3 * BLOCK, N_KV * HEAD_DIM), BF16), pltpu.VMEM((3 * BLOCK, 2 * N_KV * HEAD_DIM), BF16),
                        pltpu.VMEM((N_KV, GQA * BLOCK, HEAD_DIM), BF16), pltpu.VMEM((N_HEADS, BLOCK, 3 * BLOCK), F32)],
        compiler_params=_params(2),
    )(sinks, qkv, qkv, qkv, qkv)


def _attn_bwd(qkv, sinks, o, lse, do, n_ex, nb):
    n_rows = qkv.shape[0]
    kvb = N_HEADS * HEAD_DIM // KV_DIM
    scale = HEAD_DIM ** -0.5
    nq = lambda r: nb - 1 - r

    def body(sink_ref, q_ref, kvm_ref, kvp_ref, kvc_ref, o_ref, lse_ref, do_ref, dqkv_ref, dsink_ref,
             k_s, v_s, dkv_s, carry_s, meta_s, q_s, do_s, bias_s):
        b, r = pl.program_id(0), pl.program_id(1)
        n = nq(r)

        @pl.when((r == 0) | (n <= 1))
        def _():
            _attn_bias(n, bias_s)

        @pl.when((b == 0) & (r == 0))
        def _():
            dsink_ref[...] = jnp.zeros_like(dsink_ref)

        @pl.when(r == 0)
        def _():
            carry_s[...] = jnp.zeros_like(carry_s)
            meta_s[...] = jnp.zeros_like(meta_s)

        for part, ref in enumerate((kvm_ref, kvp_ref, kvc_ref)):
            k_s[part * BLOCK:(part + 1) * BLOCK, :] = ref[:, 0:N_KV * HEAD_DIM]
            v_s[part * BLOCK:(part + 1) * BLOCK, :] = ref[:, N_KV * HEAD_DIM:KV_DIM]
        nt = (((1,), (1,)), ((), ()))
        tn = (((0,), (0,)), ((), ()))
        deltas = []
        for h in range(N_HEADS):
            rows = slice((h % GQA) * BLOCK, (h % GQA + 1) * BLOCK)
            cols = slice(h * HEAD_DIM, (h + 1) * HEAD_DIM)
            q_s[h // GQA, rows, :] = q_ref[:, cols] * scale
            do_s[h // GQA, rows, :] = do_ref[:, cols]
            deltas.append(jnp.sum(do_ref[:, cols].astype(F32) * o_ref[:, cols].astype(F32), axis=-1, keepdims=True))
        for kv in range(N_KV):
            kcols = slice(kv * HEAD_DIM, (kv + 1) * HEAD_DIM)
            vcols = slice(N_KV * HEAD_DIM + kv * HEAD_DIM, N_KV * HEAD_DIM + (kv + 1) * HEAD_DIM)
            kh, vh = k_s[:, kcols], v_s[:, kcols]
            s4 = lax.dot_general(q_s[kv], kh, nt, preferred_element_type=F32)
            dp4 = lax.dot_general(do_s[kv], vh, nt, preferred_element_type=F32)
            ps, dss = [], []
            for g in range(GQA):
                h = kv * GQA + g
                cols = slice(h * HEAD_DIM, (h + 1) * HEAD_DIM)
                rows = slice(g * BLOCK, (g + 1) * BLOCK)
                s = s4[rows] + bias_s[h]
                lse_h = lse_ref[:, h:h + 1]
                p = jnp.exp(s - lse_h)
                delta = deltas[h]
                dsink_ref[:, h:h + 1] += -jnp.exp(sink_ref[0, h] - lse_h) * delta
                ps.append(p.astype(BF16))
                dss.append((p * (dp4[rows] - delta)).astype(BF16))
            p4, ds4 = jnp.concatenate(ps, axis=0), jnp.concatenate(dss, axis=0)
            dq4 = jnp.dot(ds4, kh, preferred_element_type=F32) * scale
            for g in range(GQA):
                cols = slice((kv * GQA + g) * HEAD_DIM, (kv * GQA + g + 1) * HEAD_DIM)
                dqkv_ref[:, cols] = dq4[g * BLOCK:(g + 1) * BLOCK].astype(BF16)
            dkv_s[:, kcols] = lax.dot_general(ds4, q_s[kv], tn, preferred_element_type=F32)
            dkv_s[:, vcols] = lax.dot_general(p4, do_s[kv], tn, preferred_element_type=F32)

        meta_s[...] += dkv_s[0:BLOCK, :]
        cur = dkv_s[2 * BLOCK:3 * BLOCK, :] + carry_s[...]
        carry_s[...] = dkv_s[BLOCK:2 * BLOCK, :]

        @pl.when(n > 0)
        def _():
            dqkv_ref[:, N_HEADS * HEAD_DIM:QKV_DIM] = cur.astype(BF16)

        @pl.when(n == 0)
        def _():
            dqkv_ref[:, N_HEADS * HEAD_DIM:QKV_DIM] = (cur + meta_s[...]).astype(BF16)

    blk = lambda b, r: (b * nb + nq(r), 0)
    return pl.pallas_call(
        body, name="attn_bwd", grid=(n_ex, nb),
        in_specs=[pl.BlockSpec(memory_space=pltpu.SMEM),
                  pl.BlockSpec((BLOCK, N_HEADS * HEAD_DIM), blk),
                  pl.BlockSpec((BLOCK, KV_DIM), lambda b, r: (b * nb, kvb)),
                  pl.BlockSpec((BLOCK, KV_DIM), lambda b, r: (b * nb + jnp.maximum(nq(r) - 1, 0), kvb)),
                  pl.BlockSpec((BLOCK, KV_DIM), lambda b, r: (b * nb + nq(r), kvb)),
                  pl.BlockSpec((BLOCK, N_HEADS * HEAD_DIM), blk),
                  pl.BlockSpec((BLOCK, N_HEADS), blk),
                  pl.BlockSpec((BLOCK, N_HEADS * HEAD_DIM), blk)],
        out_specs=[pl.BlockSpec((BLOCK, QKV_DIM), blk),
                   pl.BlockSpec((BLOCK, N_HEADS), lambda b, r: (0, 0))],
        out_shape=[SDS((n_rows, QKV_DIM), BF16), SDS((BLOCK, N_HEADS), F32)],
        scratch_shapes=[pltpu.VMEM((3 * BLOCK, N_KV * HEAD_DIM), BF16), pltpu.VMEM((3 * BLOCK, N_KV * HEAD_DIM), BF16),
                        pltpu.VMEM((3 * BLOCK, KV_DIM), F32), pltpu.VMEM((BLOCK, KV_DIM), F32),
                        pltpu.VMEM((BLOCK, KV_DIM), F32), pltpu.VMEM((N_KV, GQA * BLOCK, HEAD_DIM), BF16),
                        pltpu.VMEM((N_KV, GQA * BLOCK, HEAD_DIM), BF16), pltpu.VMEM((N_HEADS, BLOCK, 3 * BLOCK), F32)],
        compiler_params=_params(2),
    )(sinks, qkv, qkv, qkv, qkv, o, lse, do)


SSM_TILES = (1408, 384)
XW = 256 * PAIRS_PER_CHUNK


def _cmul_add(xr, xi, mr, mi, sr, si):
    return xr + mr * sr - mi * si, xi + mr * si + mi * sr


def _to_segments(src_ref, dst, seg):
    for s in range(seg):
        dst[s * 8:(s + 1) * 8, :] = src_ref[pl.ds(s, 8, stride=seg), :]


def _from_segments(src, i, seg):
    return src[pl.ds(i, seg, stride=8), :]


def _scan_segments(buf, tab_ref, carry_s, seg, reverse):
    shifts = (7, 6, 4) if reverse else (1, 2, 4)
    row_id = lax.broadcasted_iota(jnp.int32, (8, 128), 0)
    a_tiles = [tab_ref[j, c] for j in range(PAIRS_PER_CHUNK) for c in (0, 1)]

    def local(si, prev):
        s = (seg - 1 - si) if reverse else si
        row = pl.multiple_of(s * 8, 8)
        out = []
        for j in range(PAIRS_PER_CHUNK):
            re, im = slice(256 * j, 256 * j + 128), slice(256 * j + 128, 256 * j + 256)
            xr, xi = _cmul_add(buf[pl.ds(row, 8), re], buf[pl.ds(row, 8), im],
                               a_tiles[2 * j], a_tiles[2 * j + 1], prev[2 * j], prev[2 * j + 1])
            buf[pl.ds(row, 8), re] = xr
            buf[pl.ds(row, 8), im] = xi
            out += [xr, xi]
        return tuple(out)

    zero = jnp.zeros((8, 128), F32)
    edge = lax.fori_loop(0, seg, local, (zero,) * (2 * PAIRS_PER_CHUNK))

    entering = []
    for j in range(PAIRS_PER_CHUNK):
        er, ei = edge[2 * j], edge[2 * j + 1]
        if reverse:
            sr = jnp.where(row_id == 7, carry_s[2 * j], pltpu.roll(er, 7, 0))
            si_ = jnp.where(row_id == 7, carry_s[2 * j + 1], pltpu.roll(ei, 7, 0))
        else:
            sr = jnp.where(row_id == 0, carry_s[2 * j], pltpu.roll(er, 1, 0))
            si_ = jnp.where(row_id == 0, carry_s[2 * j + 1], pltpu.roll(ei, 1, 0))
        for lvl, sh in enumerate(shifts):
            sr, si_ = _cmul_add(sr, si_, tab_ref[j, 2 + 2 * lvl], tab_ref[j, 3 + 2 * lvl],
                                pltpu.roll(sr, sh, 0), pltpu.roll(si_, sh, 0))
        entering += [sr, si_]
        tr, ti = _cmul_add(er, ei, tab_ref[j, 2], tab_ref[j, 3], sr, si_)
        out_row = slice(0, 1) if reverse else slice(7, 8)
        carry_s[2 * j] = jnp.broadcast_to(tr[out_row], (8, 128))
        carry_s[2 * j + 1] = jnp.broadcast_to(ti[out_row], (8, 128))

    def fix(si, carried):
        s = (seg - 1 - si) if reverse else si
        row = pl.multiple_of(s * 8, 8)
        out = []
        for j in range(PAIRS_PER_CHUNK):
            re, im = slice(256 * j, 256 * j + 128), slice(256 * j + 128, 256 * j + 256)
            ar, ai, fr, fi = a_tiles[2 * j], a_tiles[2 * j + 1], carried[2 * j], carried[2 * j + 1]
            fr, fi = ar * fr - ai * fi, ar * fi + ai * fr
            buf[pl.ds(row, 8), re] += fr
            buf[pl.ds(row, 8), im] += fi
            out += [fr, fi]
        return tuple(out)

    lax.fori_loop(0, seg, fix, tuple(entering))


def _ssm_fwd(u, b_pad, c_pad, tab, d_skip, n_ex, lp):
    n_rows = u.shape[0]
    TM = _row_tile(lp, SSM_TILES)
    SEG = TM // 8
    n_t = lp // TM
    n_chunk = D_MODEL // 128

    def body(u_ref, bp_ref, cp_ref, tab_ref, d_ref, yg_ref, y_ref, xs_ref, up_ref, buf, carry_s, us, ys):
        @pl.when(pl.program_id(2) == 0)
        def _():
            carry_s[...] = jnp.zeros_like(carry_s)

        _to_segments(u_ref, us, SEG)
        ub = us[...]
        u16 = ub.astype(BF16)
        up_ref[...] = u16
        buf[...] = jnp.dot(u16, bp_ref[...], preferred_element_type=F32)
        _scan_segments(buf, tab_ref, carry_s, SEG, reverse=False)
        xb = buf[...].astype(BF16)
        xs_ref[...] = xb
        ys[...] = d_ref[...] * ub + jnp.dot(xb, cp_ref[...], preferred_element_type=F32)
        for i in range(8):
            yi = _from_segments(ys, i, SEG)
            y_ref[i * SEG:(i + 1) * SEG, :] = yi
            yg_ref[i * SEG:(i + 1) * SEG, :] = _gelu(yi).astype(BF16)

    rows = lambda b, q, t: (b * n_t + t, q)
    return pl.pallas_call(
        body, name="ssm_fwd", grid=(n_ex, n_chunk, n_t),
        in_specs=[pl.BlockSpec((TM, 128), rows),
                  pl.BlockSpec((None, 128, XW), lambda b, q, t: (q, 0, 0)),
                  pl.BlockSpec((None, XW, 128), lambda b, q, t: (q, 0, 0)),
                  pl.BlockSpec((PAIRS_PER_CHUNK, 8, 8, 128), lambda b, q, t: (q, 0, 0, 0)),
                  pl.BlockSpec((1, 128), lambda b, q, t: (0, q))],
        out_specs=[pl.BlockSpec((TM, 128), rows), pl.BlockSpec((TM, 128), rows),
                   pl.BlockSpec((None, TM, XW), lambda b, q, t: (q, b * n_t + t, 0)), pl.BlockSpec((TM, 128), rows)],
        out_shape=[SDS((n_rows, D_MODEL), BF16), SDS((n_rows, D_MODEL), F32), SDS((n_chunk, n_rows, XW), BF16),
                   SDS((n_rows, D_MODEL), BF16)],
        scratch_shapes=[pltpu.VMEM((TM, XW), F32), pltpu.VMEM((2 * PAIRS_PER_CHUNK, 8, 128), F32),
                        pltpu.VMEM((TM, 128), F32), pltpu.VMEM((TM, 128), F32)],
        compiler_params=_params(3),
    )(u, b_pad, c_pad, tab, d_skip)


def _ssm_bwd(dyg, y, u_seg, xs, ct_pad, bt_pad, tab_rev, d_skip, n_ex, lp):
    n_rows = u_seg.shape[0]
    TM = _row_tile(lp, SSM_TILES)
    SEG = TM // 8
    n_t = lp // TM
    n_chunk = D_MODEL // 128
    tile = lambda q, b, t: (b * n_t + (n_t - 1 - t), q)

    def body(dyg_ref, y_ref, up_ref, xs_ref, xp_ref, ct_ref, bt_ref, tab_ref, d_ref,
             du_ref, db_ref, dc_ref, da_ref, dd_ref, buf, xf, carry_s, dys, dyp):
        b, t = pl.program_id(1), pl.program_id(2)

        @pl.when((b == 0) & (t == 0))
        def _():
            db_ref[...] = jnp.zeros_like(db_ref)
            dc_ref[...] = jnp.zeros_like(dc_ref)
            da_ref[...] = jnp.zeros_like(da_ref)
            dd_ref[...] = jnp.zeros_like(dd_ref)

        @pl.when(t == 0)
        def _():
            carry_s[...] = jnp.zeros_like(carry_s)

        dys[...] = dyg_ref[...].astype(F32) * _gelu_grad(y_ref[...])
        _to_segments(dys, dyp, SEG)
        dy = dyp[...]
        u16 = up_ref[...]
        dd_ref[...] += _fold8(dy * u16.astype(F32))
        dy16 = dy.astype(BF16)
        first_tile = t == n_t - 1
        tn = (((0,), (0,)), ((), ()))
        buf[...] = jnp.dot(dy16, ct_ref[...], preferred_element_type=F32)
        dc_ref[...] += lax.dot_general(dy16, xs_ref[...], tn, preferred_element_type=F32)
        xf[16:16 + TM, :] = xs_ref[...].astype(F32)
        xf[0:16, :] = jnp.where(first_tile, 0.0, xp_ref[...].astype(F32))
        _scan_segments(buf, tab_ref, carry_s, SEG, reverse=True)
        g16 = buf[...].astype(BF16)
        dys[...] = d_ref[...] * dy + jnp.dot(g16, bt_ref[...], preferred_element_type=F32)
        db_ref[...] += lax.dot_general(u16, g16, tn, preferred_element_type=F32)
        row_id = lax.broadcasted_iota(jnp.int32, (8, 128), 0)
        for j in range(PAIRS_PER_CHUNK):
            re, im = slice(256 * j, 256 * j + 128), slice(256 * j + 128, 256 * j + 256)
            first = [jnp.where(row_id == 0, jnp.broadcast_to(xf[15:16, c], (8, 128)),
                               pltpu.roll(xf[8 + TM:16 + TM, c], 1, 0)) for c in (re, im)]
            for rows, pr, pi in ((slice(0, 8), first[0], first[1]),
                                 (slice(8, TM), xf[16:8 + TM, re], xf[16:8 + TM, im])):
                gr, gi = buf[rows, re], buf[rows, im]
                da_ref[j, 0] += _fold8(gr * pr + gi * pi)
                da_ref[j, 1] += _fold8(gi * pr - gr * pi)
        for i in range(8):
            du_ref[i * SEG:(i + 1) * SEG, :] = _from_segments(dys, i, SEG)

    prev16 = lambda q, b, t: (q, jnp.maximum((b * n_t + (n_t - 1 - t)) * (TM // 16) - 1, 0), 0)
    return pl.pallas_call(
        body, name="ssm_bwd", grid=(n_chunk, n_ex, n_t),
        in_specs=[pl.BlockSpec((TM, 128), tile), pl.BlockSpec((TM, 128), tile), pl.BlockSpec((TM, 128), tile),
                  pl.BlockSpec((None, TM, XW), lambda q, b, t: (q, b * n_t + (n_t - 1 - t), 0)),
                  pl.BlockSpec((None, 16, XW), prev16),
                  pl.BlockSpec((None, 128, XW), lambda q, b, t: (q, 0, 0)),
                  pl.BlockSpec((None, XW, 128), lambda q, b, t: (q, 0, 0)),
                  pl.BlockSpec((PAIRS_PER_CHUNK, 8, 8, 128), lambda q, b, t: (q, 0, 0, 0)),
                  pl.BlockSpec((1, 128), lambda q, b, t: (0, q))],
        out_specs=[pl.BlockSpec((TM, 128), tile),
                   pl.BlockSpec((None, 128, XW), lambda q, b, t: (q, 0, 0)),
                   pl.BlockSpec((None, 128, XW), lambda q, b, t: (q, 0, 0)),
                   pl.BlockSpec((PAIRS_PER_CHUNK, 2, 8, 128), lambda q, b, t: (q, 0, 0, 0)),
                   pl.BlockSpec((8, 128), lambda q, b, t: (0, q))],
        out_shape=[SDS((n_rows, D_MODEL), F32), SDS((n_chunk, 128, XW), F32), SDS((n_chunk, 128, XW), F32),
                   SDS((N_PAIR, 2, 8, 128), F32), SDS((8, D_MODEL), F32)],
        scratch_shapes=[pltpu.VMEM((TM, XW), F32), pltpu.VMEM((TM + 16, XW), F32),
                        pltpu.VMEM((2 * PAIRS_PER_CHUNK, 8, 128), F32), pltpu.VMEM((TM, 128), F32),
                        pltpu.VMEM((TM, 128), F32)],
        compiler_params=_params(3),
    )(dyg, y, u_seg, xs, xs, ct_pad, bt_pad, tab_rev, d_skip)


def _rms_bwd_call(dhn, h, wn, dres, name):
    n_rows = h.shape[0]

    def body(dhn_ref, h_ref, wn_ref, dres_ref, o_ref, dw_ref):
        @pl.when(pl.program_id(0) == 0)
        def _():
            dw_ref[...] = jnp.zeros_like(dw_ref)

        dh, dw_rows = _rms_bwd(dhn_ref[...], h_ref[...], wn_ref[...])
        o_ref[...] = dres_ref[...] + dh
        dw_ref[...] += _fold8(dw_rows)

    row = lambda i: (i, 0)
    return pl.pallas_call(
        body, name=name, grid=(n_rows // TM,),
        in_specs=[pl.BlockSpec((TM, D_MODEL), row), pl.BlockSpec((TM, D_MODEL), row),
                  pl.BlockSpec((1, D_MODEL), lambda i: (0, 0)), pl.BlockSpec((TM, D_MODEL), row)],
        out_specs=[pl.BlockSpec((TM, D_MODEL), row), pl.BlockSpec((8, D_MODEL), lambda i: (0, 0))],
        out_shape=[SDS((n_rows, D_MODEL), F32), SDS((8, D_MODEL), F32)], compiler_params=_params(1),
    )(dhn, h, wn, dres)


def _glu_bwd(dh, z, w4):
    n_rows = dh.shape[0]
    tm = _row_tile(n_rows, MM_TILES)
    n_sh, _, k, n = w4.shape

    def body(dh_ref, z_ref, w_ref, dz_ref, dyg_ref):
        sg = jax.nn.sigmoid(z_ref[:, D_MODEL:2 * D_MODEL].astype(F32))
        d = dh_ref[...]
        dz_ref[:, 0:D_MODEL] = (d * sg).astype(BF16)
        dz_ref[:, D_MODEL:2 * D_MODEL] = (d * z_ref[:, 0:D_MODEL].astype(F32) * sg * (1.0 - sg)).astype(BF16)
        acc = None
        for s in range(n_sh):
            part = lax.dot_general(dz_ref[:, s * n:(s + 1) * n], w_ref[s], (((1,), (1,)), ((), ())),
                                   preferred_element_type=F32)
            acc = part if acc is None else acc + part
        dyg_ref[...] = acc.astype(BF16)

    row = lambda i: (i, 0)
    return pl.pallas_call(
        body, name="glu_bwd", grid=(n_rows // tm,),
        in_specs=[pl.BlockSpec((tm, D_MODEL), row), pl.BlockSpec((tm, 2 * D_MODEL), row), _w4_spec(w4)],
        out_specs=[pl.BlockSpec((tm, 2 * D_MODEL), row), pl.BlockSpec((tm, k), row)],
        out_shape=[SDS((n_rows, 2 * D_MODEL), BF16), SDS((n_rows, k), BF16)], compiler_params=_params(1),
    )(dh, z, w4)


def _adamw(pieces, w, m, v, name):
    n_layers = len(pieces)
    rows, cols = pieces[0].shape[1:]
    rb = rows
    for cand in (256, 136, 128, 64, 32, 16, 8):
        if rows % cand == 0 and rows > cand:
            rb = cand
            break
    n_blk = rows // rb
    c1 = 1.0 / (1.0 - ADAM_B1 ** ADAM_STEP)
    c2 = 1.0 / (1.0 - ADAM_B2 ** ADAM_STEP)

    def body(*refs):
        p_refs = refs[:n_layers]
        w_ref, m_ref, v_ref, g_out, d_out, m_out, v_out = refs[n_layers:]
        layer = pl.program_id(0)
        g = None
        for l, p_ref in enumerate(p_refs):
            gl = p_ref[0].astype(F32)
            for k in range(1, N_DEV):
                gl = gl + p_ref[k].astype(F32)
            g = gl if g is None else jnp.where(layer == l, gl, g)
        m_new = ADAM_B1 * m_ref[...] + (1.0 - ADAM_B1) * g
        v_new = ADAM_B2 * v_ref[...] + (1.0 - ADAM_B2) * (g * g)
        g_out[...] = g
        m_out[...] = m_new
        v_out[...] = v_new
        d_out[...] = -ADAM_LR * ((m_new * c1) / (jnp.sqrt(v_new * c2) + ADAM_EPS) + ADAM_WD * w_ref[...])

    def piece_spec(l):
        return pl.BlockSpec((N_DEV, rb, cols), lambda ly, i: (0, jnp.where(ly == l, i, 0), 0))

    blk = pl.BlockSpec((rb, cols), lambda ly, i: (ly * n_blk + i, 0))
    return pl.pallas_call(
        body, name=name, grid=(n_layers, n_blk),
        in_specs=[piece_spec(l) for l in range(n_layers)] + [blk, blk, blk],
        out_specs=[blk, blk, blk, blk],
        out_shape=[SDS((n_layers * rows, cols), F32)] * 4, compiler_params=_params(2),
    )(*pieces, w, m, v)


_HBM = pl.BlockSpec(memory_space=pltpu.HBM)
_SEM = pl.BlockSpec(memory_space=pltpu.SEMAPHORE)
_EFFECT = pltpu.SideEffectType.DATAFLOW_SIDE_EFFECTING
N_GATHER_PEERS = N_CHIPS - 1
N_EXCHANGE_PEERS = N_DEV - 1


def _gather_copies(srcs, lands, send_sems, recv_sems):
    x, y, c = lax.axis_index("x"), lax.axis_index("y"), lax.axis_index("c")
    mine = 2 * x + y
    chips = [(1 - x, y), (x, 1 - y), (1 - x, 1 - y)]
    out, inc = [], []
    for a in range(len(srcs)):
        for k, (px, py) in enumerate(chips):
            j = a * N_GATHER_PEERS + k
            sems = dict(send_sem=send_sems.at[j], recv_sem=recv_sems.at[j], device_id=(px, py, c),
                        device_id_type=pl.DeviceIdType.MESH)
            out.append(pltpu.make_async_remote_copy(src_ref=srcs[a], dst_ref=lands[a].at[mine], **sems))
            inc.append(pltpu.make_async_remote_copy(src_ref=srcs[a], dst_ref=lands[a].at[2 * px + py], **sems))
    return out, inc


def _exchange_copies(n_scatter):
    def copies(srcs, lands, send_sems, recv_sems):
        x, y, c = lax.axis_index("x"), lax.axis_index("y"), lax.axis_index("c")
        me = 4 * x + 2 * y + c
        peers = [(x ^ (k >> 2), y ^ ((k >> 1) & 1), c ^ (k & 1)) for k in range(1, N_DEV)]
        out, inc = [], []
        for a in range(len(srcs)):
            for k, (px, py, pc) in enumerate(peers):
                j = a * N_EXCHANGE_PEERS + k
                sems = dict(send_sem=send_sems.at[j], recv_sem=recv_sems.at[j], device_id=(px, py, pc),
                            device_id_type=pl.DeviceIdType.MESH)
                theirs = srcs[a].at[2 * px + py] if a < n_scatter else srcs[a]
                mine = srcs[a].at[2 * x + y] if a < n_scatter else srcs[a]
                out.append(pltpu.make_async_remote_copy(src_ref=theirs, dst_ref=lands[a].at[me], **sems))
                inc.append(pltpu.make_async_remote_copy(src_ref=mine, dst_ref=lands[a].at[4 * px + 2 * py + pc], **sems))
        return out, inc

    return copies


def _split_start(groups, copies_fn, n_peers, name):
    sizes = [len(srcs) for srcs, _ in groups]
    flat = [a for srcs, lands in groups for a in list(srcs) + list(lands)]
    n_flat, n_grp = len(flat), len(groups)

    def body(*refs):
        sems = refs[2 * n_flat:2 * n_flat + 2 * n_grp]
        token = refs[-1]
        at = 0
        for gi, n in enumerate(sizes):
            out, _ = copies_fn(refs[at:at + n], refs[at + n:at + 2 * n], sems[2 * gi], sems[2 * gi + 1])
            for cp in out:
                cp.start()
            at += 2 * n
        token[...] = jnp.zeros_like(token)

    sem_shapes = []
    for n in sizes:
        sem_shapes += [pltpu.SemaphoreType.DMA((n * n_peers,)), pltpu.SemaphoreType.DMA((n * n_peers,))]
    res = pl.pallas_call(
        body, name=name,
        out_shape=(*[pltpu.HBM(a.shape, a.dtype) for a in flat], *sem_shapes, SDS((8, 128), F32)),
        in_specs=[_HBM] * n_flat,
        out_specs=(*[_HBM] * n_flat, *[_SEM] * (2 * n_grp), pl.BlockSpec(memory_space=pltpu.VMEM)),
        input_output_aliases={i: i for i in range(n_flat)},
        compiler_params=pltpu.CompilerParams(has_side_effects=_EFFECT),
    )(*[pltpu.with_memory_space_constraint(a, pltpu.HBM) for a in flat])
    handles, at = [], 0
    for gi, n in enumerate(sizes):
        handles.append((res[n_flat + 2 * gi], res[n_flat + 2 * gi + 1], list(res[at:at + n]), list(res[at + n:at + 2 * n])))
        at += 2 * n
    return handles, res[-1]


def _split_wait(handle, after, copies_fn, name):
    send_sems, recv_sems, srcs, lands = handle
    n = len(srcs)
    after = list(after) if isinstance(after, (list, tuple)) else [after]

    def body(*refs):
        out, inc = copies_fn(refs[:n], refs[n:2 * n], refs[2 * n], refs[2 * n + 1])
        for cp in out:
            cp.wait_send()
        for cp in inc:
            cp.wait_recv()

    flat = list(srcs) + list(lands)
    res = pl.pallas_call(
        body, name=name,
        out_shape=tuple(pltpu.HBM(a.shape, a.dtype) for a in flat),
        in_specs=[_HBM] * (2 * n) + [_SEM, _SEM] + [pl.BlockSpec(memory_space=pl.ANY)] * len(after),
        out_specs=tuple([_HBM] * (2 * n)),
        input_output_aliases={i: i for i in range(2 * n)},
        compiler_params=pltpu.CompilerParams(has_side_effects=_EFFECT),
    )(*flat, send_sems, recv_sems, *after)
    return list(res[n:])


def _landing(own, slot, n_slots):
    return lax.dynamic_update_index_in_dim(lax.empty((n_slots,) + own.shape, own.dtype), own, slot, 0)


def _ssm_discretize(lam_re, lam_im, log_dt, b_re, b_im):
    lr = jnp.minimum(lam_re, LAMBDA_RE_MAX)
    li = lam_im
    dt = jnp.exp(log_dt)[:, None]
    mag = jnp.exp(lr * dt)
    ar, ai = mag * jnp.cos(li * dt), mag * jnp.sin(li * dt)
    den = lr * lr + li * li
    nr, ni = ar - 1.0, ai
    gr, gi = (nr * lr + ni * li) / den, (ni * lr - nr * li) / den
    bbr = gr[:, :, None] * b_re - gi[:, :, None] * b_im
    bbi = gr[:, :, None] * b_im + gi[:, :, None] * b_re
    return ar, ai, bbr, bbi


def _pair_lanes(t):
    return t.reshape(N_PAIR, 2 * SSM_STATE)


def _chan_state_blocks(t_gcp):
    t = t_gcp.reshape(N_PAIR, 2, SSM_GROUP, SSM_STATE)
    eye2 = jnp.eye(2, dtype=t.dtype)
    blk = jnp.einsum("rgcp,gh->rgchp", t, eye2).reshape(N_PAIR, 2 * SSM_GROUP, 2 * SSM_STATE)
    place = jax.nn.one_hot(jnp.arange(N_PAIR) % PAIRS_PER_CHUNK, PAIRS_PER_CHUNK, dtype=t.dtype)
    return jnp.einsum("rcl,rj->rjcl", blk, place).reshape(N_PAIR, 128, 2 * SSM_STATE)


def _chan_state_unblock(t):
    t = t.reshape(N_PAIR, PAIRS_PER_CHUNK, 2, SSM_GROUP, 2, SSM_STATE)
    place = jax.nn.one_hot(jnp.arange(N_PAIR) % PAIRS_PER_CHUNK, PAIRS_PER_CHUNK, dtype=t.dtype)
    t = jnp.einsum("rjgchp,rj->rgchp", t, place)
    t = jnp.einsum("rgchp,gh->rgcp", t, jnp.eye(2, dtype=t.dtype))
    return t.reshape(SSM_NG, SSM_GROUP, SSM_STATE)


def _scan_tables(zr, zi, reverse, seg):
    zr, zi = _pair_lanes(zr), _pair_lanes(-zi if reverse else zi)
    a = (jnp.exp(zr) * jnp.cos(zi), jnp.exp(zr) * jnp.sin(zi))
    cmul = lambda p, q: (p[0] * q[0] - p[1] * q[1], p[0] * q[1] + p[1] * q[0])
    big, square, bits = None, a, seg
    while bits:
        if bits & 1:
            big = square if big is None else cmul(big, square)
        square, bits = cmul(square, square), bits >> 1
    powers = [a, big]
    for _ in range(2):
        powers.append(cmul(powers[-1], powers[-1]))
    rows = jnp.arange(8)[None, :, None]
    tiles = [jnp.broadcast_to(part[:, None, :], (N_PAIR, 8, 128)) for part in powers[0]]
    for lvl, step in enumerate((1, 2, 4)):
        keep = (rows <= 7 - step) if reverse else (rows >= step)
        for part in powers[1 + lvl]:
            tiles.append(jnp.where(keep, part[:, None, :], 0.0))
    return jnp.stack(tiles, axis=1)


def _pairs_to_chunks(t):
    n_chunk = N_PAIR // PAIRS_PER_CHUNK
    return jnp.swapaxes(t.reshape(n_chunk, PAIRS_PER_CHUNK, 128, 256), 1, 2).reshape(n_chunk, 128, XW)


def _chunks_to_pairs(t):
    n_chunk = N_PAIR // PAIRS_PER_CHUNK
    return jnp.swapaxes(t.reshape(n_chunk, 128, PAIRS_PER_CHUNK, 256), 1, 2).reshape(N_PAIR, 128, 256)


def _ssm_operands(w, lp):
    seg = _row_tile(lp, SSM_TILES) // 8
    ar, ai, bbr, bbi = _ssm_discretize(w["ssm_lambda_re"], w["ssm_lambda_im"], w["ssm_log_dt"], w["ssm_b_re"], w["ssm_b_im"])
    b_blk = jnp.concatenate([_chan_state_blocks(jnp.swapaxes(bbr, 1, 2)), _chan_state_blocks(jnp.swapaxes(bbi, 1, 2))], axis=2)
    c_blk = jnp.concatenate([_chan_state_blocks(w["ssm_c_re"]), -_chan_state_blocks(w["ssm_c_im"])], axis=2)
    dt = jnp.exp(w["ssm_log_dt"])[:, None]
    zr, zi = jnp.minimum(w["ssm_lambda_re"], LAMBDA_RE_MAX) * dt, w["ssm_lambda_im"] * dt
    b_cat, c_cat = _pairs_to_chunks(b_blk).astype(BF16), _pairs_to_chunks(c_blk).astype(BF16)
    return (b_cat, jnp.swapaxes(b_cat, 1, 2), c_cat, jnp.swapaxes(c_cat, 1, 2),
            _scan_tables(zr, zi, False, seg), _scan_tables(zr, zi, True, seg))


def _local_step(x, target, w, late_weights, on_grads):
    n_ex, seq, _ = x.shape
    lp = seq + BLOCK
    nb = lp // BLOCK
    n_rows = n_ex * lp
    g = {}

    head = jnp.concatenate([jnp.zeros((PAD, D_MODEL), F32), w["meta_tokens"]], axis=0)
    h0 = jnp.concatenate([jnp.broadcast_to(head[None], (n_ex, BLOCK, D_MODEL)), x], axis=1).reshape(n_rows, D_MODEL)

    qkv, hn_a = _rms_mm_cols(h0, w["attn_norm_w"], w["attn_w_qkv"], "qkv_fwd")
    att, lse = _attn_fwd(qkv, w["attn_sinks"], n_ex, nb)
    h1 = _mm_acc(att, w["attn_w_o"], False, "attn_out_fwd", res=h0)
    w = {**w, **late_weights(0, att)}
    h2, a0, hn_m0, u = _mlp_fwd(h1, w["mlp_norm_w"][0:1], w["mlp_w_up"][0], w["mlp_w_down"][0], "mlp0_fwd",
                                next_norm=w["ssm_norm_w"])
    late = late_weights(1, h2)
    w["ssm_w_glu"] = late["ssm_w_glu"]
    w["mlp_w_up"], w["mlp_w_down"] = w["mlp_w_up"] + late["mlp_w_up"], w["mlp_w_down"] + late["mlp_w_down"]

    ops = w["ssm_operands"] if "ssm_operands" in w else _ssm_operands(w, lp)
    b_pad, bt_pad, ct_pad, c_pad, tab_fwd, tab_rev = ops
    yg, y, xs, u_seg = _ssm_fwd(u, b_pad, c_pad, tab_fwd, w["ssm_d"], n_ex, lp)
    z = _mm_cols(yg, w["ssm_w_glu"], False, "glu_mm_fwd")
    dh4, a1, hn_m1, h3, loss_tile, dnorm_f = _mlp_fwd(
        h2, w["mlp_norm_w"][1:2], w["mlp_w_up"][1], w["mlp_w_down"][1], "mlp1_fwd", glu_z=z,
        loss=(w["final_norm_w"], target.reshape(n_ex * seq, D_MODEL), lp))

    def mlp_bwd(dh_out, h_in, a, hn, layer, tag, norm_w):
        dhn, dw_up, dw_down = None, None, None
        for s in range(N_CHIPS):
            final = s == N_CHIPS - 1
            res = _mlp_bwd_shard(s, dh_out, a, hn, dhn, h_in if final else None, norm_w,
                                 w["mlp_w_up"][layer], w["mlp_w_down"][layer], dw_up, dw_down, f"{tag}_bwd{s}")
            dhn, dw_up, dw_down = res[:3]
        return dhn, res[3], dw_up, dw_down

    dh3, dnorm_m1, dwu1, dwd1 = mlp_bwd(dh4, h3, a1, hn_m1, 1, "mlp1", w["mlp_norm_w"][1:2])
    tok = on_grads("mlp1", {"mlp_w_up": dwu1, "mlp_w_down": dwd1})
    dz, dyg = _glu_bwd(dh3, z, w["ssm_w_glu"])
    g["ssm_w_glu"] = _mm_tn(yg, dz, N_CHIPS, False, "glu_mm_dw")
    du, db_blk, dc_blk, da_t, dd_t = _ssm_bwd(dyg, y, u_seg, xs, ct_pad, bt_pad, tab_rev, w["ssm_d"] + tok, n_ex, lp)
    dh2, dnorm_s = _rms_bwd_call(du, h2, w["ssm_norm_w"], dh3, "ssm_norm_bwd")
    db_blk, dc_blk = _chunks_to_pairs(db_blk), _chunks_to_pairs(dc_blk)
    g["ssm_c_re"] = _chan_state_unblock(dc_blk[:, :, 0:128])
    g["ssm_c_im"] = -_chan_state_unblock(dc_blk[:, :, 128:256])
    g_bbr = jnp.swapaxes(_chan_state_unblock(db_blk[:, :, 0:128]), 1, 2)
    g_bbi = jnp.swapaxes(_chan_state_unblock(db_blk[:, :, 128:256]), 1, 2)
    g_a = jnp.sum(da_t, axis=2).reshape(N_PAIR, 2, 2, SSM_STATE)
    g_ar, g_ai = g_a[:, 0].reshape(SSM_NG, SSM_STATE), g_a[:, 1].reshape(SSM_NG, SSM_STATE)
    _, vjp = jax.vjp(_ssm_discretize, w["ssm_lambda_re"], w["ssm_lambda_im"], w["ssm_log_dt"], w["ssm_b_re"], w["ssm_b_im"])
    g["ssm_lambda_re"], g["ssm_lambda_im"], g["ssm_log_dt"], g["ssm_b_re"], g["ssm_b_im"] = vjp((g_ar, g_ai, g_bbr, g_bbi))
    tok = on_grads("ssm", g)
    g = {}
    dh1, dnorm_m0, dwu0, dwd0 = mlp_bwd(dh2, h1, a0, hn_m0, 0, "mlp0", w["mlp_norm_w"][0:1] + tok)
    datt = _mm_cols(dh1, w["attn_w_o"], True, "attn_out_dx")
    dw_o = _mm_tn(att, dh1, N_CHIPS, True, "attn_out_dw")
    tok = on_grads("mlp0", {"mlp_w_up": dwu0, "mlp_w_down": dwd0, "attn_w_o": dw_o})
    dqkv, dsink_rows = _attn_bwd(qkv, w["attn_sinks"] + tok, att, lse, datt, n_ex, nb)
    tok = on_grads("qkv", {"attn_w_qkv": _mm_tn(hn_a, dqkv, N_CHIPS, False, "qkv_dw")})
    dh0, dnorm_a = _mm_acc(dqkv, w["attn_w_qkv"], True, "qkv_dx", rms_bwd=(h0, w["attn_norm_w"] + tok, dh1))

    dh0 = dh0.reshape(n_ex, lp, D_MODEL)
    on_grads("rest", {
        "mlp_norm_w": jnp.stack([jnp.sum(dnorm_m0, axis=0), jnp.sum(dnorm_m1, axis=0)]),
        "final_norm_w": jnp.sum(dnorm_f, axis=0),
        "attn_norm_w": jnp.sum(dnorm_a, axis=0)[None],
        "ssm_norm_w": jnp.sum(dnorm_s, axis=0)[None],
        "attn_sinks": jnp.sum(dsink_rows, axis=0)[None],
        "ssm_d": jnp.sum(dd_t, axis=0)[None],
        "meta_tokens": jnp.sum(dh0[:, PAD:BLOCK], axis=0),
        "loss": loss_tile[0, 0:1]})
    return loss_tile, dh0[:, BLOCK:]


_SHARDED_SMALL = ("meta_tokens", "ssm_norm_w", "ssm_d")
_REP_SSM = ("ssm_lambda_re", "ssm_lambda_im", "ssm_log_dt", "ssm_b_re", "ssm_b_im", "ssm_c_re", "ssm_c_im")
_REP_MISC = ("attn_norm_w", "attn_sinks", "mlp_norm_w", "final_norm_w")
_BIG = ("attn_w_qkv", "attn_w_o", "ssm_w_glu", "mlp_w_up", "mlp_w_down")


def _pack(parts, cols):
    flat = jnp.concatenate([p.reshape(-1) for p in parts])
    rows = -(-flat.shape[0] // (8 * cols)) * 8
    return jnp.pad(flat, (0, rows * cols - flat.shape[0])).reshape(rows, cols)


def _unpack(packed, like):
    flat = packed.reshape(-1)
    out, at = [], 0
    for p in like:
        out.append(flat[at:at + p.size].reshape(p.shape))
        at += p.size
    return out


def kernel(x, meta_tokens, attn_norm_w, attn_w_qkv, attn_sinks, attn_w_o, ssm_norm_w, ssm_lambda_re, ssm_lambda_im, ssm_log_dt, ssm_b_re, ssm_b_im, ssm_c_re, ssm_c_im, ssm_d, ssm_w_glu, mlp_norm_w, mlp_w_up, mlp_w_down, final_norm_w, loss_target, m_meta_tokens, m_attn_norm_w, m_attn_w_qkv, m_attn_sinks, m_attn_w_o, m_ssm_norm_w, m_ssm_lambda_re, m_ssm_lambda_im, m_ssm_log_dt, m_ssm_b_re, m_ssm_b_im, m_ssm_c_re, m_ssm_c_im, m_ssm_d, m_ssm_w_glu, m_mlp_norm_w, m_mlp_w_up, m_mlp_w_down, m_final_norm_w, v_meta_tokens, v_attn_norm_w, v_attn_w_qkv, v_attn_sinks, v_attn_w_o, v_ssm_norm_w, v_ssm_lambda_re, v_ssm_lambda_im, v_ssm_log_dt, v_ssm_b_re, v_ssm_b_im, v_ssm_c_re, v_ssm_c_im, v_ssm_d, v_ssm_w_glu, v_mlp_norm_w, v_mlp_w_up, v_mlp_w_down, v_final_norm_w):
    names = ("meta_tokens", "attn_norm_w", "attn_w_qkv", "attn_sinks", "attn_w_o", "ssm_norm_w", "ssm_lambda_re",
             "ssm_lambda_im", "ssm_log_dt", "ssm_b_re", "ssm_b_im", "ssm_c_re", "ssm_c_im", "ssm_d", "ssm_w_glu",
             "mlp_norm_w", "mlp_w_up", "mlp_w_down", "final_norm_w")
    wts = dict(zip(names, (meta_tokens, attn_norm_w, attn_w_qkv, attn_sinks, attn_w_o, ssm_norm_w, ssm_lambda_re,
                           ssm_lambda_im, ssm_log_dt, ssm_b_re, ssm_b_im, ssm_c_re, ssm_c_im, ssm_d, ssm_w_glu,
                           mlp_norm_w, mlp_w_up, mlp_w_down, final_norm_w)))
    mom = dict(zip(names, (m_meta_tokens, m_attn_norm_w, m_attn_w_qkv, m_attn_sinks, m_attn_w_o, m_ssm_norm_w,
                           m_ssm_lambda_re, m_ssm_lambda_im, m_ssm_log_dt, m_ssm_b_re, m_ssm_b_im, m_ssm_c_re,
                           m_ssm_c_im, m_ssm_d, m_ssm_w_glu, m_mlp_norm_w, m_mlp_w_up, m_mlp_w_down, m_final_norm_w)))
    var = dict(zip(names, (v_meta_tokens, v_attn_norm_w, v_attn_w_qkv, v_attn_sinks, v_attn_w_o, v_ssm_norm_w,
                           v_ssm_lambda_re, v_ssm_lambda_im, v_ssm_log_dt, v_ssm_b_re, v_ssm_b_im, v_ssm_c_re,
                           v_ssm_c_im, v_ssm_d, v_ssm_w_glu, v_mlp_norm_w, v_mlp_w_up, v_mlp_w_down, v_final_norm_w)))

    my_chip = 2 * lax.axis_index("x") + lax.axis_index("y")
    my_dev = 2 * my_chip + lax.axis_index("c")
    small_mine = _pack([wts[n] for n in _SHARDED_SMALL], 128)
    first = [attn_w_qkv.astype(BF16), attn_w_o.astype(BF16), small_mine]
    with_landing = lambda srcs: (srcs, [_landing(a, my_chip, N_CHIPS) for a in srcs])
    handles, token = _split_start([with_landing(first)], _gather_copies, N_GATHER_PEERS, "gather_start_first")
    up16, down16 = (mlp_w_up + token[0, 0]).astype(BF16), (mlp_w_down + token[0, 0]).astype(BF16)
    mlp0 = [up16[0:1], down16[0:1]]
    rest = [(ssm_w_glu + token[0, 0]).astype(BF16), up16[1:2], down16[1:2]]
    later, _ = _split_start([with_landing(mlp0), with_landing(rest)], _gather_copies, N_GATHER_PEERS, "gather_start_later")
    handles = handles + later
    full = {n: wts[n] for n in _REP_MISC}
    full["final_norm_w"] = final_norm_w[None]
    for n in _REP_SSM:
        full[n] = wts[n][0]
    full["ssm_operands"] = _ssm_operands(full, x.shape[1] + BLOCK)
    got = _split_wait(handles[0], full["ssm_operands"], _gather_copies, "gather_wait_first")
    full["attn_w_qkv"], full["attn_w_o"] = got[0], got[1]
    smalls = [_unpack(got[2][s], [wts[n] for n in _SHARDED_SMALL]) for s in range(N_CHIPS)]
    for k, n in enumerate(_SHARDED_SMALL):
        full[n] = jnp.concatenate([smalls[s][k] for s in range(N_CHIPS)], axis=1)

    def late_weights(stage, after):
        if stage == 0:
            up, down = _split_wait(handles[1], after, _gather_copies, "gather_wait_mlp0")
            return {"mlp_w_up": [up], "mlp_w_down": [down]}
        glu, up, down = _split_wait(handles[2], after, _gather_copies, "gather_wait_rest")
        return {"ssm_w_glu": glu, "mlp_w_up": [up], "mlp_w_down": [down]}

    def shard_cols(t):
        return jnp.swapaxes(t.reshape(t.shape[0], N_CHIPS, t.shape[1] // N_CHIPS), 0, 1)

    pending = {}

    def on_grads(tag, g):
        scatter = [g[n] for n in _BIG if n in g]
        whole = []
        if tag == "ssm":
            whole = [_pack([g[n] for n in _REP_SSM], D_MODEL)]
        if tag == "rest":
            parts = [shard_cols(g[n]) for n in _SHARDED_SMALL]
            scatter = [jnp.stack([_pack([p[s] for p in parts], 128) for s in range(N_CHIPS)])]
            whole = [_pack([g[n] for n in _REP_MISC] + [g["loss"]], D_MODEL)]
        srcs = scatter + whole
        lands = [_landing(lax.dynamic_index_in_dim(a, my_chip, 0, keepdims=False), my_dev, N_DEV) for a in scatter]
        lands += [_landing(a, my_dev, N_DEV) for a in whole]
        hs, token = _split_start([(srcs, lands)], _exchange_copies(len(scatter)), N_EXCHANGE_PEERS, "exchange_start_" + tag)
        pending[tag] = (hs[0], len(scatter))
        return token[0, 0]

    _, grad_x = _local_step(x, loss_target, full, late_weights, on_grads)

    recv = {}
    for tag, (handle, n_scatter) in pending.items():
        recv[tag] = _split_wait(handle, grad_x, _exchange_copies(n_scatter), "exchange_wait_" + tag)
    loss = jnp.sum(recv["rest"][1].reshape(N_DEV, -1)[:, sum(wts[n].size for n in _REP_MISC)])

    out = {}

    def update(tag, pieces, w2, m2, v2):
        return _adamw(pieces, w2, m2, v2, "adamw_" + tag)

    def update_weight(n, pieces):
        shp = wts[n].shape
        r2 = (math.prod(shp[:-1]), shp[-1])
        res = update(n, pieces, wts[n].reshape(r2), mom[n].reshape(r2), var[n].reshape(r2))
        out[n] = [t.reshape(shp) for t in res]

    update_weight("mlp_w_up", [recv["mlp0"][1], recv["mlp1"][0]])
    update_weight("mlp_w_down", [recv["mlp0"][2], recv["mlp1"][1]])
    update_weight("attn_w_o", [recv["mlp0"][0]])
    update_weight("ssm_w_glu", [recv["ssm"][0]])
    update_weight("attn_w_qkv", [recv["qkv"][0]])
    for tag, group, pieces, cols in (("small", _SHARDED_SMALL, recv["rest"][0], 128),
                                     ("rep_ssm", _REP_SSM, recv["ssm"][1], D_MODEL),
                                     ("rep_misc", _REP_MISC, recv["rest"][1], D_MODEL)):
        like = [wts[n] for n in group]
        res = update(tag, [pieces], _pack(like, cols), _pack([mom[n] for n in group], cols),
                     _pack([var[n] for n in group], cols))
        for k, n in enumerate(group):
            out[n] = [_unpack(t, like)[k] for t in res]

    return (loss, grad_x, *[out[n][0] for n in names], *[out[n][1] for n in names],
            *[out[n][2] for n in names], *[out[n][3] for n in names])
```

```python
import functools
import math

import jax
import jax.numpy as jnp
from jax import lax
from jax.experimental import pallas as pl
from jax.experimental.pallas import tpu as pltpu

F32 = jnp.float32
BF16 = jnp.bfloat16
SDS = jax.ShapeDtypeStruct

D_MODEL = 1024
N_HEADS = 16
N_KV = 4
GQA = N_HEADS // N_KV
HEAD_DIM = 64
BLOCK = 128
N_META = 16
PAD = BLOCK - N_META
QKV_DIM = (N_HEADS + 2 * N_KV) * HEAD_DIM
KV_DIM = 2 * N_KV * HEAD_DIM
D_FF = 4 * D_MODEL
N_CHIPS = 4
N_DEV = 8
SSM_GROUP = 16
SSM_NG = D_MODEL // SSM_GROUP
SSM_STATE = 64
N_PAIR = SSM_NG // 2
PAIRS_PER_CHUNK = 4
RMS_EPS = 1e-6
NEG_INF = -1e30
LAMBDA_RE_MAX = -1e-4
ADAM_LR, ADAM_B1, ADAM_B2, ADAM_EPS, ADAM_WD, ADAM_STEP = 0.001, 0.9, 0.999, 1e-08, 0.01, 10

TM = 384
MM_TILES = (1056, 768, 384)
MLP_FWD_TILES = (384,)
MLP_BWD_TILES = (768, 384)
TN_TILES = (1408, 768, 384)
VMEM_LIMIT = 56 * 1024 * 1024


def _params(n_grid):
    return pltpu.CompilerParams(dimension_semantics=("arbitrary",) * n_grid, vmem_limit_bytes=VMEM_LIMIT)


def _row_tile(n_rows, tiles):
    return next(t for t in tiles if n_rows % t == 0)


def _rms(h, w):
    r = lax.rsqrt(jnp.mean(h * h, axis=-1, keepdims=True) + RMS_EPS)
    return h * r * w


def _rms_bwd(dhn, h, w):
    r = lax.rsqrt(jnp.mean(h * h, axis=-1, keepdims=True) + RMS_EPS)
    g = dhn * w
    proj = jnp.sum(g * h, axis=-1, keepdims=True) * (1.0 / D_MODEL)
    return r * g - h * (r * r * r) * proj, dhn * h * r


def _fold8(t):
    return jnp.sum(t.reshape(t.shape[0] // 8, 8, t.shape[1]), axis=0)


def _gelu(y):
    return 0.5 * y * (1.0 + jnp.tanh(0.7978845608028654 * (y + 0.044715 * y * y * y)))


def _gelu_grad(y):
    t = jnp.tanh(0.7978845608028654 * (y + 0.044715 * y * y * y))
    return 0.5 * (1.0 + t) + 0.5 * y * (1.0 - t * t) * 0.7978845608028654 * (1.0 + 3.0 * 0.044715 * y * y)


def _w4_spec(w4):
    n_sh, _, k, n = w4.shape
    return pl.BlockSpec((n_sh, None, k, n), lambda i: (0, 0, 0, 0))


def _rms_mm_cols(h, wn, w4, name):
    n_rows = h.shape[0]
    n_sh, _, k, n = w4.shape
    tm = _row_tile(n_rows, MM_TILES)

    def body(h_ref, wn_ref, w_ref, o_ref, hn_ref):
        hn = _rms(h_ref[...], wn_ref[...]).astype(BF16)
        hn_ref[...] = hn
        for s in range(n_sh):
            o_ref[:, s * n:(s + 1) * n] = jnp.dot(hn, w_ref[s], preferred_element_type=F32).astype(o_ref.dtype)

    return pl.pallas_call(
        body, name=name, grid=(n_rows // tm,),
        in_specs=[pl.BlockSpec((tm, k), lambda i: (i, 0)), pl.BlockSpec((1, k), lambda i: (0, 0)), _w4_spec(w4)],
        out_specs=[pl.BlockSpec((tm, n_sh * n), lambda i: (i, 0)), pl.BlockSpec((tm, k), lambda i: (i, 0))],
        out_shape=[SDS((n_rows, n_sh * n), BF16), SDS((n_rows, k), BF16)],
        compiler_params=_params(1),
    )(h, wn, w4)


def _mm_cols(x, w4, trans_w, name):
    n_rows, kx = x.shape
    tm = _row_tile(n_rows, MM_TILES)
    n_sh, _, k, n = w4.shape
    n_out = k if trans_w else n
    dims = (((1,), (1,)), ((), ())) if trans_w else (((1,), (0,)), ((), ()))

    def body(x_ref, w_ref, o_ref):
        x16 = x_ref[...].astype(BF16)
        for s in range(n_sh):
            o_ref[:, s * n_out:(s + 1) * n_out] = lax.dot_general(
                x16, w_ref[s], dims, preferred_element_type=F32).astype(o_ref.dtype)

    return pl.pallas_call(
        body, name=name, grid=(n_rows // tm,),
        in_specs=[pl.BlockSpec((tm, kx), lambda i: (i, 0)), _w4_spec(w4)],
        out_specs=pl.BlockSpec((tm, n_sh * n_out), lambda i: (i, 0)),
        out_shape=SDS((n_rows, n_sh * n_out), BF16),
        compiler_params=_params(1),
    )(x, w4)


def _mm_acc(x, w4, trans_w, name, res=None, rms_bwd=None):
    n_rows = x.shape[0]
    tm = _row_tile(n_rows, MM_TILES)
    n_sh, _, k, n = w4.shape
    kx, n_out = (n, k) if trans_w else (k, n)
    dims = (((1,), (1,)), ((), ())) if trans_w else (((1,), (0,)), ((), ()))

    def body(*refs):
        if rms_bwd is not None:
            x_ref, w_ref, h_ref, wn_ref, dres_ref, o_ref, dw_ref = refs
        elif res is not None:
            x_ref, w_ref, res_ref, o_ref = refs
        else:
            x_ref, w_ref, o_ref = refs
        acc = None
        for s in range(n_sh):
            part = lax.dot_general(x_ref[:, s * kx:(s + 1) * kx].astype(BF16), w_ref[s], dims, preferred_element_type=F32)
            acc = part if acc is None else acc + part
        if rms_bwd is not None:
            dh, dw_rows = _rms_bwd(acc, h_ref[...], wn_ref[...])
            o_ref[...] = (dres_ref[...] + dh).astype(o_ref.dtype)

            @pl.when(pl.program_id(0) == 0)
            def _():
                dw_ref[...] = jnp.zeros_like(dw_ref)

            dw_ref[...] += _fold8(dw_rows)
        elif res is not None:
            o_ref[...] = (res_ref[...] + acc).astype(o_ref.dtype)
        else:
            o_ref[...] = acc.astype(o_ref.dtype)

    row = lambda i: (i, 0)
    in_specs = [pl.BlockSpec((tm, n_sh * kx), row), _w4_spec(w4)]
    args = [x, w4]
    out_specs = pl.BlockSpec((tm, n_out), row)
    out_shape = SDS((n_rows, n_out), F32)
    if rms_bwd is not None:
        h, wn, dres = rms_bwd
        in_specs += [pl.BlockSpec((tm, n_out), row), pl.BlockSpec((1, n_out), lambda i: (0, 0)),
                     pl.BlockSpec((tm, n_out), row)]
        args += [h, wn, dres]
        out_specs = [out_specs, pl.BlockSpec((8, n_out), lambda i: (0, 0))]
        out_shape = [out_shape, SDS((8, n_out), F32)]
    elif res is not None:
        in_specs.append(pl.BlockSpec((tm, n_out), row))
        args.append(res)
    return pl.pallas_call(
        body, name=name, grid=(n_rows // tm,), in_specs=in_specs, out_specs=out_specs, out_shape=out_shape,
        compiler_params=_params(1),
    )(*args)


def _mm_tn(a, b, n_sh, a_sharded, name):
    n_rows = a.shape[0]
    tm = _row_tile(n_rows, TN_TILES)
    ka = a.shape[1] // n_sh if a_sharded else a.shape[1]
    nb = b.shape[1] if a_sharded else b.shape[1] // n_sh
    n_i = n_rows // tm

    def body(a_ref, b_ref, o_ref, acc):
        i = pl.program_id(0)

        @pl.when(i == 0)
        def _():
            acc[...] = jnp.zeros_like(acc)

        for s in range(n_sh):
            a_s = a_ref[:, s * ka:(s + 1) * ka] if a_sharded else a_ref[...]
            b_s = b_ref[...] if a_sharded else b_ref[:, s * nb:(s + 1) * nb]
            acc[s] += lax.dot_general(a_s.astype(BF16), b_s.astype(BF16), (((0,), (0,)), ((), ())),
                                      preferred_element_type=F32)

        @pl.when(i == n_i - 1)
        def _():
            o_ref[...] = acc[...].astype(o_ref.dtype)

    return pl.pallas_call(
        body, name=name, grid=(n_i,),
        in_specs=[pl.BlockSpec((tm, a.shape[1]), lambda i: (i, 0)), pl.BlockSpec((tm, b.shape[1]), lambda i: (i, 0))],
        out_specs=pl.BlockSpec((n_sh, ka, nb), lambda i: (0, 0, 0)),
        out_shape=SDS((n_sh, ka, nb), BF16),
        scratch_shapes=[pltpu.VMEM((n_sh, ka, nb), F32)], compiler_params=_params(1),
    )(a, b)


def _mlp_fwd(h, wn, w_up4, w_down4, name, next_norm=None, glu_z=None, loss=None):
    n_rows = h.shape[0]
    tm = _row_tile(n_rows, MLP_FWD_TILES)
    n_sh = w_up4.shape[0]
    f_sh = D_FF // n_sh
    w_down = w_down4.reshape(D_FF, D_MODEL)
    per_tile = tm // BLOCK

    def body(*refs):
        refs = list(refs)
        h_ref, wn_ref, wu_ref, wd_ref = refs[:4]
        at = 4
        if next_norm is not None:
            nn_ref = refs[at]
            at += 1
        if glu_z is not None:
            z_ref = refs[at]
            at += 1
        if loss is not None:
            fw_ref, t_refs = refs[at], refs[at + 1:at + 1 + per_tile]
            at += 1 + per_tile
        o_ref, a_ref, hn_ref = refs[at:at + 3]
        at += 3
        if next_norm is not None:
            u_ref = refs[at]
            at += 1
        if glu_z is not None:
            hin_ref = refs[at]
            at += 1
        if loss is not None:
            loss_ref, dfw_ref = refs[at:at + 2]
            at += 2
        act_s = refs[at]
        h_in = h_ref[...]
        if glu_z is not None:
            h_in = h_in + z_ref[:, 0:D_MODEL].astype(F32) * jax.nn.sigmoid(z_ref[:, D_MODEL:2 * D_MODEL].astype(F32))
            hin_ref[...] = h_in
        hn = _rms(h_in, wn_ref[...]).astype(BF16)
        hn_ref[...] = hn
        for s in range(n_sh):
            cols = slice(s * f_sh, (s + 1) * f_sh)
            a = jnp.dot(hn, wu_ref[s], preferred_element_type=F32)
            a_ref[:, cols] = a.astype(BF16)
            act = jnp.maximum(a, 0.0)
            act_s[:, cols] = (act * act).astype(BF16)
        out = h_in + jnp.dot(act_s[...], wd_ref[...], preferred_element_type=F32)
        if next_norm is not None:
            u_ref[...] = _rms(out, nn_ref[...])
        if loss is None:
            o_ref[...] = out
        else:
            i = pl.program_id(0)
            first_of_example = i % tiles_per_example == 0

            @pl.when(i == 0)
            def _():
                loss_ref[...] = jnp.zeros_like(loss_ref)
                dfw_ref[...] = jnp.zeros_like(dfw_ref)

            def block(k):
                rows = slice(k * BLOCK, (k + 1) * BLOCK)
                diff = _rms(out[rows], fw_ref[...]) - t_refs[k][...]
                loss_ref[...] += 0.5 * jnp.sum(diff * diff) * (1.0 / D_MODEL)
                dh, dw_rows = _rms_bwd(diff * (1.0 / D_MODEL), out[rows], fw_ref[...])
                o_ref[rows, :] = dh
                dfw_ref[...] += _fold8(dw_rows)

            @pl.when(first_of_example)
            def _():
                o_ref[0:BLOCK, :] = jnp.zeros((BLOCK, D_MODEL), F32)

            pl.when(jnp.logical_not(first_of_example))(lambda: block(0))
            for k in range(1, per_tile):
                block(k)

    row = lambda i: (i, 0)
    vec = pl.BlockSpec((1, D_MODEL), lambda i: (0, 0))
    in_specs = [pl.BlockSpec((tm, D_MODEL), row), vec,
                pl.BlockSpec((n_sh, None, D_MODEL, f_sh), lambda i: (0, 0, 0, 0), pipeline_mode=pl.Buffered(1)),
                pl.BlockSpec((D_FF, D_MODEL), lambda i: (0, 0), pipeline_mode=pl.Buffered(1))]
    out_specs = [pl.BlockSpec((tm, D_MODEL), row), pl.BlockSpec((tm, D_FF), row), pl.BlockSpec((tm, D_MODEL), row)]
    out_shape = [SDS((n_rows, D_MODEL), F32), SDS((n_rows, D_FF), BF16), SDS((n_rows, D_MODEL), BF16)]
    args = [h, wn, w_up4, w_down]
    if next_norm is not None:
        in_specs.append(vec)
        args.append(next_norm)
    if glu_z is not None:
        in_specs.append(pl.BlockSpec((tm, 2 * D_MODEL), row))
        args.append(glu_z)
    for extra in (next_norm, glu_z):
        if extra is not None:
            out_specs.append(pl.BlockSpec((tm, D_MODEL), row))
            out_shape.append(SDS((n_rows, D_MODEL), F32))
    if loss is not None:
        final_wn, target, lp = loss
        tiles_per_example = lp // tm
        real_blocks = lp // BLOCK - 1

        def t_spec(k):
            return pl.BlockSpec((BLOCK, D_MODEL), lambda i: (
                (i // tiles_per_example) * real_blocks + jnp.maximum(per_tile * (i % tiles_per_example) + k - 1, 0), 0))

        in_specs += [vec] + [t_spec(k) for k in range(per_tile)]
        args += [final_wn] + [target] * per_tile
        out_specs += [pl.BlockSpec((8, 128), lambda i: (0, 0)), pl.BlockSpec((8, D_MODEL), lambda i: (0, 0))]
        out_shape += [SDS((8, 128), F32), SDS((8, D_MODEL), F32)]
    return pl.pallas_call(
        body, name=name, grid=(n_rows // tm,), in_specs=in_specs, out_specs=out_specs, out_shape=out_shape,
        scratch_shapes=[pltpu.VMEM((tm, D_FF), BF16)],
        compiler_params=_params(1),
    )(*args)


def _mlp_bwd_shard(s, dh, a, hn, dhn_prev, h, wn, w_up4, w_down4, dw_up_buf, dw_down_buf, name):
    n_rows = dh.shape[0]
    n_sh = w_up4.shape[0]
    f_sh = D_FF // n_sh
    tm = _row_tile(n_rows, MLP_BWD_TILES)
    n_i = n_rows // tm
    last = h is not None
    nt = (((1,), (1,)), ((), ()))
    tn = (((0,), (0,)), ((), ()))

    def body(*refs):
        refs = list(refs)
        dh_ref, a_ref, hn_ref, wu_ref, wd_ref = refs[:5]
        at = 5
        prev_ref = None
        if dhn_prev is not None:
            prev_ref = refs[at]
            at += 1
        if last:
            h_ref, wn_ref = refs[at:at + 2]
            at += 2
        if dw_up_buf is not None:
            at += 2
        o_ref, dwu_ref, dwd_ref = refs[at:at + 3]
        at += 3
        if last:
            dnorm_ref = refs[at]
            at += 1
        acc_u, acc_d = refs[at:at + 2]
        i = pl.program_id(0)

        @pl.when(i == 0)
        def _():
            acc_u[...] = jnp.zeros_like(acc_u)
            acc_d[...] = jnp.zeros_like(acc_d)
            if last:
                dnorm_ref[...] = jnp.zeros_like(dnorm_ref)

        dh16 = dh_ref[...].astype(BF16)
        r = jnp.maximum(a_ref[...].astype(F32), 0.0)
        dact = lax.dot_general(dh16, wd_ref[...], nt, preferred_element_type=F32)
        da16 = (dact * (2.0 * r)).astype(BF16)
        acc_d[...] += lax.dot_general((r * r).astype(BF16), dh16, tn, preferred_element_type=F32)
        acc_u[...] += lax.dot_general(hn_ref[...], da16, tn, preferred_element_type=F32)
        dhn = lax.dot_general(da16, wu_ref[...], nt, preferred_element_type=F32)
        if prev_ref is not None:
            dhn = dhn + prev_ref[...]
        if last:
            d_rms, dw_rows = _rms_bwd(dhn, h_ref[...], wn_ref[...])
            o_ref[...] = dh_ref[...] + d_rms
            dnorm_ref[...] += _fold8(dw_rows)
        else:
            o_ref[...] = dhn

        @pl.when(i == n_i - 1)
        def _():
            dwu_ref[...] = acc_u[...].astype(BF16)
            dwd_ref[...] = acc_d[...].astype(BF16)

    row = lambda i: (i, 0)
    tile = pl.BlockSpec((tm, D_MODEL), row)
    in_specs = [tile, pl.BlockSpec((tm, f_sh), lambda i: (i, s)), tile,
                pl.BlockSpec((None, None, D_MODEL, f_sh), lambda i: (s, 0, 0, 0)),
                pl.BlockSpec((None, None, f_sh, D_MODEL), lambda i: (s, 0, 0, 0))]
    args = [dh, a, hn, w_up4, w_down4]
    if dhn_prev is not None:
        in_specs.append(tile)
        args.append(dhn_prev)
    if last:
        in_specs += [tile, pl.BlockSpec((1, D_MODEL), lambda i: (0, 0))]
        args += [h, wn]
    aliases = {}
    if dw_up_buf is not None:
        aliases = {len(args): 1, len(args) + 1: 2}
        in_specs += [pl.BlockSpec(memory_space=pl.ANY)] * 2
        args += [dw_up_buf, dw_down_buf]
    out_specs = [tile, pl.BlockSpec((None, D_MODEL, f_sh), lambda i: (s, 0, 0)),
                 pl.BlockSpec((None, f_sh, D_MODEL), lambda i: (s, 0, 0))]
    out_shape = [SDS((n_rows, D_MODEL), F32), SDS((n_sh, D_MODEL, f_sh), BF16), SDS((n_sh, f_sh, D_MODEL), BF16)]
    if last:
        out_specs.append(pl.BlockSpec((8, D_MODEL), lambda i: (0, 0)))
        out_shape.append(SDS((8, D_MODEL), F32))
    return pl.pallas_call(
        body, name=name, grid=(n_i,), in_specs=in_specs, out_specs=out_specs, out_shape=out_shape,
        input_output_aliases=aliases,
        scratch_shapes=[pltpu.VMEM((D_MODEL, f_sh), F32), pltpu.VMEM((f_sh, D_MODEL), F32)],
        compiler_params=_params(1),
    )(*args)


def _attn_masks(n):
    qi = lax.broadcasted_iota(jnp.int32, (BLOCK, 3 * BLOCK), 0)
    col = lax.broadcasted_iota(jnp.int32, (BLOCK, 3 * BLOCK), 1)
    kj = col - BLOCK
    dist = BLOCK + qi - kj
    kmin = jnp.where(n == 0, 2 * BLOCK, jnp.where(n == 1, BLOCK, 0))
    band_ok = (col >= BLOCK) & (dist >= 0) & (dist < BLOCK) & (kj >= kmin)
    q_pos = n * BLOCK + qi - PAD
    meta_ok = (col >= PAD) & (col < BLOCK) & (col - PAD <= q_pos)
    distf = jnp.where(col >= BLOCK, dist, 0).astype(F32)
    return band_ok | meta_ok, distf


def _alibi_slope(h):
    return float(2.0 ** (-8.0 * (h + 1) / N_HEADS))


def _attn_bias(n, bias_s):
    ok, distf = _attn_masks(n)
    for h in range(N_HEADS):
        bias_s[h] = jnp.where(ok, -_alibi_slope(h) * distf, NEG_INF)


def _attn_fwd(qkv, sinks, n_ex, nb):
    n_rows = qkv.shape[0]
    kvb = N_HEADS * HEAD_DIM // KV_DIM

    def body(sink_ref, q_ref, kvm_ref, kvp_ref, kvc_ref, o_ref, lse_ref, k_s, v_s, q_s, bias_s):
        n = pl.program_id(1)

        @pl.when(n <= 2)
        def _():
            _attn_bias(n, bias_s)

        v_s[...] = jnp.ones_like(v_s)
        for part, ref in enumerate((kvm_ref, kvp_ref, kvc_ref)):
            rows = slice(part * BLOCK, (part + 1) * BLOCK)
            k_s[rows, :] = ref[:, 0:N_KV * HEAD_DIM]
            for kv in range(N_KV):
                v_s[rows, kv * 2 * HEAD_DIM:kv * 2 * HEAD_DIM + HEAD_DIM] = \
                    ref[:, (N_KV + kv) * HEAD_DIM:(N_KV + kv + 1) * HEAD_DIM]
        for h in range(N_HEADS):
            q_s[h // GQA, (h % GQA) * BLOCK:(h % GQA + 1) * BLOCK, :] = \
                q_ref[:, h * HEAD_DIM:(h + 1) * HEAD_DIM] * (HEAD_DIM ** -0.5)

        def scores(kv):
            return lax.dot_general(q_s[kv], k_s[:, kv * HEAD_DIM:(kv + 1) * HEAD_DIM], (((1,), (1,)), ((), ())),
                                   preferred_element_type=F32)

        ahead = scores(0)
        for kv in range(N_KV):
            s4 = ahead
            if kv + 1 < N_KV:
                ahead = scores(kv + 1)
            es, ms, sink_es = [], [], []
            for g in range(GQA):
                h = kv * GQA + g
                s = s4[g * BLOCK:(g + 1) * BLOCK] + bias_s[h]
                sink = sink_ref[0, h]
                m = jnp.maximum(jnp.max(s, axis=-1, keepdims=True), sink)
                es.append(jnp.exp(s - m).astype(BF16))
                ms.append(m)
                sink_es.append(jnp.exp(sink - m))
            pv = jnp.dot(jnp.concatenate(es, axis=0), v_s[:, kv * 2 * HEAD_DIM:(kv + 1) * 2 * HEAD_DIM],
                         preferred_element_type=F32)
            for g in range(GQA):
                h = kv * GQA + g
                pg = pv[g * BLOCK:(g + 1) * BLOCK]
                l = pg[:, HEAD_DIM:HEAD_DIM + 1] + sink_es[g]
                o_ref[:, h * HEAD_DIM:(h + 1) * HEAD_DIM] = (pg[:, 0:HEAD_DIM] * (1.0 / l)).astype(BF16)
                lse_ref[:, h:h + 1] = ms[g] + jnp.log(l)

    return pl.pallas_call(
        body, name="attn_fwd", grid=(n_ex, nb),
        in_specs=[pl.BlockSpec(memory_space=pltpu.SMEM),
                  pl.BlockSpec((BLOCK, N_HEADS * HEAD_DIM), lambda b, n: (b * nb + n, 0)),
                  pl.BlockSpec((BLOCK, KV_DIM), lambda b, n: (b * nb, kvb)),
                  pl.BlockSpec((BLOCK, KV_DIM), lambda b, n: (b * nb + jnp.maximum(n - 1, 0), kvb)),
                  pl.BlockSpec((BLOCK, KV_DIM), lambda b, n: (b * nb + n, kvb))],
        out_specs=[pl.BlockSpec((BLOCK, N_HEADS * HEAD_DIM), lambda b, n: (b * nb + n, 0)),
                   pl.BlockSpec((BLOCK, N_HEADS), lambda b, n: (b * nb + n, 0))],
        out_shape=[SDS((n_rows, N_HEADS * HEAD_DIM), BF16), SDS((n_rows, N_HEADS), F32)],
        scratch_shapes=[pltpu.VMEM((3 * BLOCK, N_KV * HEAD_DIM), BF16), pltpu.VMEM((3 * BLOCK, 2 * N_KV * HEAD_DIM), BF16),
                        pltpu.VMEM((N_KV, GQA * BLOCK, HEAD_DIM), BF16), pltpu.VMEM((N_HEADS, BLOCK, 3 * BLOCK), F32)],
        compiler_params=_params(2),
    )(sinks, qkv, qkv, qkv, qkv)


def _attn_bwd(qkv, sinks, o, lse, do, n_ex, nb):
    n_rows = qkv.shape[0]
    kvb = N_HEADS * HEAD_DIM // KV_DIM
    scale = HEAD_DIM ** -0.5
    nq = lambda r: nb - 1 - r

    def body(sink_ref, q_ref, kvm_ref, kvp_ref, kvc_ref, o_ref, lse_ref, do_ref, dqkv_ref, dsink_ref,
             k_s, v_s, dkv_s, carry_s, meta_s, q_s, do_s, bias_s):
        b, r = pl.program_id(0), pl.program_id(1)
        n = nq(r)

        @pl.when((r == 0) | (n <= 1))
        def _():
            _attn_bias(n, bias_s)

        @pl.when((b == 0) & (r == 0))
        def _():
            dsink_ref[...] = jnp.zeros_like(dsink_ref)

        @pl.when(r == 0)
        def _():
            carry_s[...] = jnp.zeros_like(carry_s)
            meta_s[...] = jnp.zeros_like(meta_s)

        for part, ref in enumerate((kvm_ref, kvp_ref, kvc_ref)):
            k_s[part * BLOCK:(part + 1) * BLOCK, :] = ref[:, 0:N_KV * HEAD_DIM]
            v_s[part * BLOCK:(part + 1) * BLOCK, :] = ref[:, N_KV * HEAD_DIM:KV_DIM]
        nt = (((1,), (1,)), ((), ()))
        tn = (((0,), (0,)), ((), ()))
        deltas = []
        for h in range(N_HEADS):
            rows = slice((h % GQA) * BLOCK, (h % GQA + 1) * BLOCK)
            cols = slice(h * HEAD_DIM, (h + 1) * HEAD_DIM)
            q_s[h // GQA, rows, :] = q_ref[:, cols] * scale
            do_s[h // GQA, rows, :] = do_ref[:, cols]
            deltas.append(jnp.sum(do_ref[:, cols].astype(F32) * o_ref[:, cols].astype(F32), axis=-1, keepdims=True))
        for kv in range(N_KV):
            kcols = slice(kv * HEAD_DIM, (kv + 1) * HEAD_DIM)
            vcols = slice(N_KV * HEAD_DIM + kv * HEAD_DIM, N_KV * HEAD_DIM + (kv + 1) * HEAD_DIM)
            kh, vh = k_s[:, kcols], v_s[:, kcols]
            s4 = lax.dot_general(q_s[kv], kh, nt, preferred_element_type=F32)
            dp4 = lax.dot_general(do_s[kv], vh, nt, preferred_element_type=F32)
            ps, dss = [], []
            for g in range(GQA):
                h = kv * GQA + g
                cols = slice(h * HEAD_DIM, (h + 1) * HEAD_DIM)
                rows = slice(g * BLOCK, (g + 1) * BLOCK)
                s = s4[rows] + bias_s[h]
                lse_h = lse_ref[:, h:h + 1]
                p = jnp.exp(s - lse_h)
                delta = deltas[h]
                dsink_ref[:, h:h + 1] += -jnp.exp(sink_ref[0, h] - lse_h) * delta
                ps.append(p.astype(BF16))
                dss.append((p * (dp4[rows] - delta)).astype(BF16))
            p4, ds4 = jnp.concatenate(ps, axis=0), jnp.concatenate(dss, axis=0)
            dq4 = jnp.dot(ds4, kh, preferred_element_type=F32) * scale
            for g in range(GQA):
                cols = slice((kv * GQA + g) * HEAD_DIM, (kv * GQA + g + 1) * HEAD_DIM)
                dqkv_ref[:, cols] = dq4[g * BLOCK:(g + 1) * BLOCK].astype(BF16)
            dkv_s[:, kcols] = lax.dot_general(ds4, q_s[kv], tn, preferred_element_type=F32)
            dkv_s[:, vcols] = lax.dot_general(p4, do_s[kv], tn, preferred_element_type=F32)

        meta_s[...] += dkv_s[0:BLOCK, :]
        cur = dkv_s[2 * BLOCK:3 * BLOCK, :] + carry_s[...]
        carry_s[...] = dkv_s[BLOCK:2 * BLOCK, :]

        @pl.when(n > 0)
        def _():
            dqkv_ref[:, N_HEADS * HEAD_DIM:QKV_DIM] = cur.astype(BF16)

        @pl.when(n == 0)
        def _():
            dqkv_ref[:, N_HEADS * HEAD_DIM:QKV_DIM] = (cur + meta_s[...]).astype(BF16)

    blk = lambda b, r: (b * nb + nq(r), 0)
    return pl.pallas_call(
        body, name="attn_bwd", grid=(n_ex, nb),
        in_specs=[pl.BlockSpec(memory_space=pltpu.SMEM),
                  pl.BlockSpec((BLOCK, N_HEADS * HEAD_DIM), blk),
                  pl.BlockSpec((BLOCK, KV_DIM), lambda b, r: (b * nb, kvb)),
                  pl.BlockSpec((BLOCK, KV_DIM), lambda b, r: (b * nb + jnp.maximum(nq(r) - 1, 0), kvb)),
                  pl.BlockSpec((BLOCK, KV_DIM), lambda b, r: (b * nb + nq(r), kvb)),
                  pl.BlockSpec((BLOCK, N_HEADS * HEAD_DIM), blk),
                  pl.BlockSpec((BLOCK, N_HEADS), blk),
                  pl.BlockSpec((BLOCK, N_HEADS * HEAD_DIM), blk)],
        out_specs=[pl.BlockSpec((BLOCK, QKV_DIM), blk),
                   pl.BlockSpec((BLOCK, N_HEADS), lambda b, r: (0, 0))],
        out_shape=[SDS((n_rows, QKV_DIM), BF16), SDS((BLOCK, N_HEADS), F32)],
        scratch_shapes=[pltpu.VMEM((3 * BLOCK, N_KV * HEAD_DIM), BF16), pltpu.VMEM((3 * BLOCK, N_KV * HEAD_DIM), BF16),
                        pltpu.VMEM((3 * BLOCK, KV_DIM), F32), pltpu.VMEM((BLOCK, KV_DIM), F32),
                        pltpu.VMEM((BLOCK, KV_DIM), F32), pltpu.VMEM((N_KV, GQA * BLOCK, HEAD_DIM), BF16),
                        pltpu.VMEM((N_KV, GQA * BLOCK, HEAD_DIM), BF16), pltpu.VMEM((N_HEADS, BLOCK, 3 * BLOCK), F32)],
        compiler_params=_params(2),
    )(sinks, qkv, qkv, qkv, qkv, o, lse, do)


SSM_TILES = (2112, 1408, 384)
XW = 256 * PAIRS_PER_CHUNK


def _cmul_add(xr, xi, mr, mi, sr, si):
    return xr + mr * sr - mi * si, xi + mr * si + mi * sr


def _to_segments(src_ref, dst, seg):
    for s in range(seg):
        dst[s * 8:(s + 1) * 8, :] = src_ref[pl.ds(s, 8, stride=seg), :]


def _from_segments(src, i, seg):
    return src[pl.ds(i, seg, stride=8), :]


def _scan_segments(buf, tab_ref, carry_s, seg, reverse):
    shifts = (7, 6, 4) if reverse else (1, 2, 4)
    row_id = lax.broadcasted_iota(jnp.int32, (8, 128), 0)
    a_tiles = [tab_ref[j, c] for j in range(PAIRS_PER_CHUNK) for c in (0, 1)]

    def local(si, prev):
        s = (seg - 1 - si) if reverse else si
        row = pl.multiple_of(s * 8, 8)
        out = []
        for j in range(PAIRS_PER_CHUNK):
            re, im = slice(256 * j, 256 * j + 128), slice(256 * j + 128, 256 * j + 256)
            xr, xi = _cmul_add(buf[pl.ds(row, 8), re], buf[pl.ds(row, 8), im],
                               a_tiles[2 * j], a_tiles[2 * j + 1], prev[2 * j], prev[2 * j + 1])
            buf[pl.ds(row, 8), re] = xr
            buf[pl.ds(row, 8), im] = xi
            out += [xr, xi]
        return tuple(out)

    zero = jnp.zeros((8, 128), F32)
    edge = lax.fori_loop(0, seg, local, (zero,) * (2 * PAIRS_PER_CHUNK))

    entering = []
    for j in range(PAIRS_PER_CHUNK):
        er, ei = edge[2 * j], edge[2 * j + 1]
        if reverse:
            sr = jnp.where(row_id == 7, carry_s[2 * j], pltpu.roll(er, 7, 0))
            si_ = jnp.where(row_id == 7, carry_s[2 * j + 1], pltpu.roll(ei, 7, 0))
        else:
            sr = jnp.where(row_id == 0, carry_s[2 * j], pltpu.roll(er, 1, 0))
            si_ = jnp.where(row_id == 0, carry_s[2 * j + 1], pltpu.roll(ei, 1, 0))
        for lvl, sh in enumerate(shifts):
            sr, si_ = _cmul_add(sr, si_, tab_ref[j, 2 + 2 * lvl], tab_ref[j, 3 + 2 * lvl],
                                pltpu.roll(sr, sh, 0), pltpu.roll(si_, sh, 0))
        entering += [sr, si_]
        tr, ti = _cmul_add(er, ei, tab_ref[j, 2], tab_ref[j, 3], sr, si_)
        out_row = slice(0, 1) if reverse else slice(7, 8)
        carry_s[2 * j] = jnp.broadcast_to(tr[out_row], (8, 128))
        carry_s[2 * j + 1] = jnp.broadcast_to(ti[out_row], (8, 128))

    def fix(si, carried):
        s = (seg - 1 - si) if reverse else si
        row = pl.multiple_of(s * 8, 8)
        out = []
        for j in range(PAIRS_PER_CHUNK):
            re, im = slice(256 * j, 256 * j + 128), slice(256 * j + 128, 256 * j + 256)
            ar, ai, fr, fi = a_tiles[2 * j], a_tiles[2 * j + 1], carried[2 * j], carried[2 * j + 1]
            fr, fi = ar * fr - ai * fi, ar * fi + ai * fr
            buf[pl.ds(row, 8), re] += fr
            buf[pl.ds(row, 8), im] += fi
            out += [fr, fi]
        return tuple(out)

    lax.fori_loop(0, seg, fix, tuple(entering))


def _ssm_fwd(u, b_pad, c_pad, tab, d_skip, n_ex, lp):
    n_rows = u.shape[0]
    TM = _row_tile(lp, SSM_TILES)
    SEG = TM // 8
    n_t = lp // TM
    n_chunk = D_MODEL // 128

    def body(u_ref, bp_ref, cp_ref, tab_ref, d_ref, yg_ref, y_ref, xs_ref, up_ref, buf, carry_s, us, ys):
        @pl.when(pl.program_id(2) == 0)
        def _():
            carry_s[...] = jnp.zeros_like(carry_s)

        _to_segments(u_ref, us, SEG)
        ub = us[...]
        u16 = ub.astype(BF16)
        up_ref[...] = u16
        buf[...] = jnp.dot(u16, bp_ref[...], preferred_element_type=F32)
        _scan_segments(buf, tab_ref, carry_s, SEG, reverse=False)
        xb = buf[...].astype(BF16)
        xs_ref[...] = xb
        ys[...] = d_ref[...] * ub + jnp.dot(xb, cp_ref[...], preferred_element_type=F32)
        for i in range(8):
            yi = _from_segments(ys, i, SEG)
            y_ref[i * SEG:(i + 1) * SEG, :] = yi
            yg_ref[i * SEG:(i + 1) * SEG, :] = _gelu(yi).astype(BF16)

    rows = lambda b, q, t: (b * n_t + t, q)
    return pl.pallas_call(
        body, name="ssm_fwd", grid=(n_ex, n_chunk, n_t),
        in_specs=[pl.BlockSpec((TM, 128), rows),
                  pl.BlockSpec((None, 128, XW), lambda b, q, t: (q, 0, 0)),
                  pl.BlockSpec((None, XW, 128), lambda b, q, t: (q, 0, 0)),
                  pl.BlockSpec((PAIRS_PER_CHUNK, 8, 8, 128), lambda b, q, t: (q, 0, 0, 0)),
                  pl.BlockSpec((1, 128), lambda b, q, t: (0, q))],
        out_specs=[pl.BlockSpec((TM, 128), rows), pl.BlockSpec((TM, 128), rows),
                   pl.BlockSpec((None, TM, XW), lambda b, q, t: (q, b * n_t + t, 0)), pl.BlockSpec((TM, 128), rows)],
        out_shape=[SDS((n_rows, D_MODEL), BF16), SDS((n_rows, D_MODEL), F32), SDS((n_chunk, n_rows, XW), BF16),
                   SDS((n_rows, D_MODEL), BF16)],
        scratch_shapes=[pltpu.VMEM((TM, XW), F32), pltpu.VMEM((2 * PAIRS_PER_CHUNK, 8, 128), F32),
                        pltpu.VMEM((TM, 128), F32), pltpu.VMEM((TM, 128), F32)],
        compiler_params=_params(3),
    )(u, b_pad, c_pad, tab, d_skip)


def _ssm_bwd(dyg, y, u_seg, xs, ct_pad, bt_pad, tab_rev, d_skip, n_ex, lp):
    n_rows = u_seg.shape[0]
    TM = _row_tile(lp, SSM_TILES)
    SEG = TM // 8
    n_t = lp // TM
    n_chunk = D_MODEL // 128
    tile = lambda q, b, t: (b * n_t + (n_t - 1 - t), q)

    def body(dyg_ref, y_ref, up_ref, xs_ref, xp_ref, ct_ref, bt_ref, tab_ref, d_ref,
             du_ref, db_ref, dc_ref, da_ref, dd_ref, buf, xf, carry_s, dys, dyp):
        b, t = pl.program_id(1), pl.program_id(2)

        @pl.when((b == 0) & (t == 0))
        def _():
            db_ref[...] = jnp.zeros_like(db_ref)
            dc_ref[...] = jnp.zeros_like(dc_ref)
            da_ref[...] = jnp.zeros_like(da_ref)
            dd_ref[...] = jnp.zeros_like(dd_ref)

        @pl.when(t == 0)
        def _():
            carry_s[...] = jnp.zeros_like(carry_s)

        dys[...] = dyg_ref[...].astype(F32) * _gelu_grad(y_ref[...])
        _to_segments(dys, dyp, SEG)
        dy = dyp[...]
        u16 = up_ref[...]
        dd_ref[...] += _fold8(dy * u16.astype(F32))
        dy16 = dy.astype(BF16)
        first_tile = t == n_t - 1
        tn = (((0,), (0,)), ((), ()))
        buf[...] = jnp.dot(dy16, ct_ref[...], preferred_element_type=F32)
        dc_ref[...] += lax.dot_general(dy16, xs_ref[...], tn, preferred_element_type=F32)
        xf[16:16 + TM, :] = xs_ref[...].astype(F32)
        xf[0:16, :] = jnp.where(first_tile, 0.0, xp_ref[...].astype(F32))
        _scan_segments(buf, tab_ref, carry_s, SEG, reverse=True)
        g16 = buf[...].astype(BF16)
        dys[...] = d_ref[...] * dy + jnp.dot(g16, bt_ref[...], preferred_element_type=F32)
        db_ref[...] += lax.dot_general(u16, g16, tn, preferred_element_type=F32)
        row_id = lax.broadcasted_iota(jnp.int32, (8, 128), 0)
        for j in range(PAIRS_PER_CHUNK):
            re, im = slice(256 * j, 256 * j + 128), slice(256 * j + 128, 256 * j + 256)
            first = [jnp.where(row_id == 0, jnp.broadcast_to(xf[15:16, c], (8, 128)),
                               pltpu.roll(xf[8 + TM:16 + TM, c], 1, 0)) for c in (re, im)]
            for rows, pr, pi in ((slice(0, 8), first[0], first[1]),
                                 (slice(8, TM), xf[16:8 + TM, re], xf[16:8 + TM, im])):
                gr, gi = buf[rows, re], buf[rows, im]
                da_ref[j, 0] += _fold8(gr * pr + gi * pi)
                da_ref[j, 1] += _fold8(gi * pr - gr * pi)
        for i in range(8):
            du_ref[i * SEG:(i + 1) * SEG, :] = _from_segments(dys, i, SEG)

    prev16 = lambda q, b, t: (q, jnp.maximum((b * n_t + (n_t - 1 - t)) * (TM // 16) - 1, 0), 0)
    return pl.pallas_call(
        body, name="ssm_bwd", grid=(n_chunk, n_ex, n_t),
        in_specs=[pl.BlockSpec((TM, 128), tile), pl.BlockSpec((TM, 128), tile), pl.BlockSpec((TM, 128), tile),
                  pl.BlockSpec((None, TM, XW), lambda q, b, t: (q, b * n_t + (n_t - 1 - t), 0)),
                  pl.BlockSpec((None, 16, XW), prev16),
                  pl.BlockSpec((None, 128, XW), lambda q, b, t: (q, 0, 0)),
                  pl.BlockSpec((None, XW, 128), lambda q, b, t: (q, 0, 0)),
                  pl.BlockSpec((PAIRS_PER_CHUNK, 8, 8, 128), lambda q, b, t: (q, 0, 0, 0)),
                  pl.BlockSpec((1, 128), lambda q, b, t: (0, q))],
        out_specs=[pl.BlockSpec((TM, 128), tile),
                   pl.BlockSpec((None, 128, XW), lambda q, b, t: (q, 0, 0)),
                   pl.BlockSpec((None, 128, XW), lambda q, b, t: (q, 0, 0)),
                   pl.BlockSpec((PAIRS_PER_CHUNK, 2, 8, 128), lambda q, b, t: (q, 0, 0, 0)),
                   pl.BlockSpec((8, 128), lambda q, b, t: (0, q))],
        out_shape=[SDS((n_rows, D_MODEL), F32), SDS((n_chunk, 128, XW), F32), SDS((n_chunk, 128, XW), F32),
                   SDS((N_PAIR, 2, 8, 128), F32), SDS((8, D_MODEL), F32)],
        scratch_shapes=[pltpu.VMEM((TM, XW), F32), pltpu.VMEM((TM + 16, XW), F32),
                        pltpu.VMEM((2 * PAIRS_PER_CHUNK, 8, 128), F32), pltpu.VMEM((TM, 128), F32),
                        pltpu.VMEM((TM, 128), F32)],
        compiler_params=_params(3),
    )(dyg, y, u_seg, xs, xs, ct_pad, bt_pad, tab_rev, d_skip)


def _rms_bwd_call(dhn, h, wn, dres, name):
    n_rows = h.shape[0]

    def body(dhn_ref, h_ref, wn_ref, dres_ref, o_ref, dw_ref):
        @pl.when(pl.program_id(0) == 0)
        def _():
            dw_ref[...] = jnp.zeros_like(dw_ref)

        dh, dw_rows = _rms_bwd(dhn_ref[...], h_ref[...], wn_ref[...])
        o_ref[...] = dres_ref[...] + dh
        dw_ref[...] += _fold8(dw_rows)

    row = lambda i: (i, 0)
    return pl.pallas_call(
        body, name=name, grid=(n_rows // TM,),
        in_specs=[pl.BlockSpec((TM, D_MODEL), row), pl.BlockSpec((TM, D_MODEL), row),
                  pl.BlockSpec((1, D_MODEL), lambda i: (0, 0)), pl.BlockSpec((TM, D_MODEL), row)],
        out_specs=[pl.BlockSpec((TM, D_MODEL), row), pl.BlockSpec((8, D_MODEL), lambda i: (0, 0))],
        out_shape=[SDS((n_rows, D_MODEL), F32), SDS((8, D_MODEL), F32)], compiler_params=_params(1),
    )(dhn, h, wn, dres)


def _glu_bwd(dh, z, w4):
    n_rows = dh.shape[0]
    tm = _row_tile(n_rows, MM_TILES)
    n_sh, _, k, n = w4.shape

    def body(dh_ref, z_ref, w_ref, dz_ref, dyg_ref):
        sg = jax.nn.sigmoid(z_ref[:, D_MODEL:2 * D_MODEL].astype(F32))
        d = dh_ref[...]
        dz_ref[:, 0:D_MODEL] = (d * sg).astype(BF16)
        dz_ref[:, D_MODEL:2 * D_MODEL] = (d * z_ref[:, 0:D_MODEL].astype(F32) * sg * (1.0 - sg)).astype(BF16)
        acc = None
        for s in range(n_sh):
            part = lax.dot_general(dz_ref[:, s * n:(s + 1) * n], w_ref[s], (((1,), (1,)), ((), ())),
                                   preferred_element_type=F32)
            acc = part if acc is None else acc + part
        dyg_ref[...] = acc.astype(BF16)

    row = lambda i: (i, 0)
    return pl.pallas_call(
        body, name="glu_bwd", grid=(n_rows // tm,),
        in_specs=[pl.BlockSpec((tm, D_MODEL), row), pl.BlockSpec((tm, 2 * D_MODEL), row), _w4_spec(w4)],
        out_specs=[pl.BlockSpec((tm, 2 * D_MODEL), row), pl.BlockSpec((tm, k), row)],
        out_shape=[SDS((n_rows, 2 * D_MODEL), BF16), SDS((n_rows, k), BF16)], compiler_params=_params(1),
    )(dh, z, w4)


def _adamw(pieces, w, m, v, name):
    n_layers = len(pieces)
    rows, cols = pieces[0].shape[1:]
    rb = rows
    for cand in (256, 136, 128, 64, 32, 16, 8):
        if rows % cand == 0 and rows > cand:
            rb = cand
            break
    n_blk = rows // rb
    c1 = 1.0 / (1.0 - ADAM_B1 ** ADAM_STEP)
    c2 = 1.0 / (1.0 - ADAM_B2 ** ADAM_STEP)

    def body(*refs):
        p_refs = refs[:n_layers]
        w_ref, m_ref, v_ref, g_out, d_out, m_out, v_out = refs[n_layers:]
        layer = pl.program_id(0)
        g = None
        for l, p_ref in enumerate(p_refs):
            gl = p_ref[0].astype(F32)
            for k in range(1, N_DEV):
                gl = gl + p_ref[k].astype(F32)
            g = gl if g is None else jnp.where(layer == l, gl, g)
        m_new = ADAM_B1 * m_ref[...] + (1.0 - ADAM_B1) * g
        v_new = ADAM_B2 * v_ref[...] + (1.0 - ADAM_B2) * (g * g)
        g_out[...] = g
        m_out[...] = m_new
        v_out[...] = v_new
        d_out[...] = -ADAM_LR * ((m_new * c1) / (jnp.sqrt(v_new * c2) + ADAM_EPS) + ADAM_WD * w_ref[...])

    def piece_spec(l):
        return pl.BlockSpec((N_DEV, rb, cols), lambda ly, i: (0, jnp.where(ly == l, i, 0), 0))

    blk = pl.BlockSpec((rb, cols), lambda ly, i: (ly * n_blk + i, 0))
    return pl.pallas_call(
        body, name=name, grid=(n_layers, n_blk),
        in_specs=[piece_spec(l) for l in range(n_layers)] + [blk, blk, blk],
        out_specs=[blk, blk, blk, blk],
        out_shape=[SDS((n_layers * rows, cols), F32)] * 4, compiler_params=_params(2),
    )(*pieces, w, m, v)


_HBM = pl.BlockSpec(memory_space=pltpu.HBM)
_SEM = pl.BlockSpec(memory_space=pltpu.SEMAPHORE)
_EFFECT = pltpu.SideEffectType.DATAFLOW_SIDE_EFFECTING
N_GATHER_PEERS = N_CHIPS - 1
N_EXCHANGE_PEERS = N_DEV - 1


def _gather_copies(srcs, lands, send_sems, recv_sems):
    x, y, c = lax.axis_index("x"), lax.axis_index("y"), lax.axis_index("c")
    mine = 2 * x + y
    chips = [(1 - x, y), (x, 1 - y), (1 - x, 1 - y)]
    out, inc = [], []
    for a in range(len(srcs)):
        for k, (px, py) in enumerate(chips):
            j = a * N_GATHER_PEERS + k
            sems = dict(send_sem=send_sems.at[j], recv_sem=recv_sems.at[j], device_id=(px, py, c),
                        device_id_type=pl.DeviceIdType.MESH)
            out.append(pltpu.make_async_remote_copy(src_ref=srcs[a], dst_ref=lands[a].at[mine], **sems))
            inc.append(pltpu.make_async_remote_copy(src_ref=srcs[a], dst_ref=lands[a].at[2 * px + py], **sems))
    return out, inc


def _exchange_copies(n_scatter):
    def copies(srcs, lands, send_sems, recv_sems):
        x, y, c = lax.axis_index("x"), lax.axis_index("y"), lax.axis_index("c")
        me = 4 * x + 2 * y + c
        peers = [(x ^ (k >> 2), y ^ ((k >> 1) & 1), c ^ (k & 1)) for k in range(1, N_DEV)]
        out, inc = [], []
        for a in range(len(srcs)):
            for k, (px, py, pc) in enumerate(peers):
                j = a * N_EXCHANGE_PEERS + k
                sems = dict(send_sem=send_sems.at[j], recv_sem=recv_sems.at[j], device_id=(px, py, pc),
                            device_id_type=pl.DeviceIdType.MESH)
                theirs = srcs[a].at[2 * px + py] if a < n_scatter else srcs[a]
                mine = srcs[a].at[2 * x + y] if a < n_scatter else srcs[a]
                out.append(pltpu.make_async_remote_copy(src_ref=theirs, dst_ref=lands[a].at[me], **sems))
                inc.append(pltpu.make_async_remote_copy(src_ref=mine, dst_ref=lands[a].at[4 * px + 2 * py + pc], **sems))
        return out, inc

    return copies


def _split_start(groups, copies_fn, n_peers, name):
    sizes = [len(srcs) for srcs, _ in groups]
    flat = [a for srcs, lands in groups for a in list(srcs) + list(lands)]
    n_flat, n_grp = len(flat), len(groups)

    def body(*refs):
        sems = refs[2 * n_flat:2 * n_flat + 2 * n_grp]
        token = refs[-1]
        at = 0
        for gi, n in enumerate(sizes):
            out, _ = copies_fn(refs[at:at + n], refs[at + n:at + 2 * n], sems[2 * gi], sems[2 * gi + 1])
            for cp in out:
                cp.start()
            at += 2 * n
        token[...] = jnp.zeros_like(token)

    sem_shapes = []
    for n in sizes:
        sem_shapes += [pltpu.SemaphoreType.DMA((n * n_peers,)), pltpu.SemaphoreType.DMA((n * n_peers,))]
    res = pl.pallas_call(
        body, name=name,
        out_shape=(*[pltpu.HBM(a.shape, a.dtype) for a in flat], *sem_shapes, SDS((8, 128), F32)),
        in_specs=[_HBM] * n_flat,
        out_specs=(*[_HBM] * n_flat, *[_SEM] * (2 * n_grp), pl.BlockSpec(memory_space=pltpu.VMEM)),
        input_output_aliases={i: i for i in range(n_flat)},
        compiler_params=pltpu.CompilerParams(has_side_effects=_EFFECT),
    )(*[pltpu.with_memory_space_constraint(a, pltpu.HBM) for a in flat])
    handles, at = [], 0
    for gi, n in enumerate(sizes):
        handles.append((res[n_flat + 2 * gi], res[n_flat + 2 * gi + 1], list(res[at:at + n]), list(res[at + n:at + 2 * n])))
        at += 2 * n
    return handles, res[-1]


def _split_wait(handle, after, copies_fn, name):
    send_sems, recv_sems, srcs, lands = handle
    n = len(srcs)
    after = list(after) if isinstance(after, (list, tuple)) else [after]

    def body(*refs):
        out, inc = copies_fn(refs[:n], refs[n:2 * n], refs[2 * n], refs[2 * n + 1])
        for cp in out:
            cp.wait_send()
        for cp in inc:
            cp.wait_recv()

    flat = list(srcs) + list(lands)
    res = pl.pallas_call(
        body, name=name,
        out_shape=tuple(pltpu.HBM(a.shape, a.dtype) for a in flat),
        in_specs=[_HBM] * (2 * n) + [_SEM, _SEM] + [pl.BlockSpec(memory_space=pl.ANY)] * len(after),
        out_specs=tuple([_HBM] * (2 * n)),
        input_output_aliases={i: i for i in range(2 * n)},
        compiler_params=pltpu.CompilerParams(has_side_effects=_EFFECT),
    )(*flat, send_sems, recv_sems, *after)
    return list(res[n:])


def _landing(own, slot, n_slots):
    return lax.dynamic_update_index_in_dim(lax.empty((n_slots,) + own.shape, own.dtype), own, slot, 0)


def _ssm_discretize(lam_re, lam_im, log_dt, b_re, b_im):
    lr = jnp.minimum(lam_re, LAMBDA_RE_MAX)
    li = lam_im
    dt = jnp.exp(log_dt)[:, None]
    mag = jnp.exp(lr * dt)
    ar, ai = mag * jnp.cos(li * dt), mag * jnp.sin(li * dt)
    den = lr * lr + li * li
    nr, ni = ar - 1.0, ai
    gr, gi = (nr * lr + ni * li) / den, (ni * lr - nr * li) / den
    bbr = gr[:, :, None] * b_re - gi[:, :, None] * b_im
    bbi = gr[:, :, None] * b_im + gi[:, :, None] * b_re
    return ar, ai, bbr, bbi


def _pair_lanes(t):
    return t.reshape(N_PAIR, 2 * SSM_STATE)


def _chan_state_blocks(t_gcp):
    t = t_gcp.reshape(N_PAIR, 2, SSM_GROUP, SSM_STATE)
    eye2 = jnp.eye(2, dtype=t.dtype)
    blk = jnp.einsum("rgcp,gh->rgchp", t, eye2).reshape(N_PAIR, 2 * SSM_GROUP, 2 * SSM_STATE)
    place = jax.nn.one_hot(jnp.arange(N_PAIR) % PAIRS_PER_CHUNK, PAIRS_PER_CHUNK, dtype=t.dtype)
    return jnp.einsum("rcl,rj->rjcl", blk, place).reshape(N_PAIR, 128, 2 * SSM_STATE)


def _chan_state_unblock(t):
    t = t.reshape(N_PAIR, PAIRS_PER_CHUNK, 2, SSM_GROUP, 2, SSM_STATE)
    place = jax.nn.one_hot(jnp.arange(N_PAIR) % PAIRS_PER_CHUNK, PAIRS_PER_CHUNK, dtype=t.dtype)
    t = jnp.einsum("rjgchp,rj->rgchp", t, place)
    t = jnp.einsum("rgchp,gh->rgcp", t, jnp.eye(2, dtype=t.dtype))
    return t.reshape(SSM_NG, SSM_GROUP, SSM_STATE)


def _scan_tables(zr, zi, reverse, seg):
    zr, zi = _pair_lanes(zr), _pair_lanes(-zi if reverse else zi)
    a = (jnp.exp(zr) * jnp.cos(zi), jnp.exp(zr) * jnp.sin(zi))
    cmul = lambda p, q: (p[0] * q[0] - p[1] * q[1], p[0] * q[1] + p[1] * q[0])
    big, square, bits = None, a, seg
    while bits:
        if bits & 1:
            big = square if big is None else cmul(big, square)
        square, bits = cmul(square, square), bits >> 1
    powers = [a, big]
    for _ in range(2):
        powers.append(cmul(powers[-1], powers[-1]))
    rows = jnp.arange(8)[None, :, None]
    tiles = [jnp.broadcast_to(part[:, None, :], (N_PAIR, 8, 128)) for part in powers[0]]
    for lvl, step in enumerate((1, 2, 4)):
        keep = (rows <= 7 - step) if reverse else (rows >= step)
        for part in powers[1 + lvl]:
            tiles.append(jnp.where(keep, part[:, None, :], 0.0))
    return jnp.stack(tiles, axis=1)


def _pairs_to_chunks(t):
    n_chunk = N_PAIR // PAIRS_PER_CHUNK
    return jnp.swapaxes(t.reshape(n_chunk, PAIRS_PER_CHUNK, 128, 256), 1, 2).reshape(n_chunk, 128, XW)


def _chunks_to_pairs(t):
    n_chunk = N_PAIR // PAIRS_PER_CHUNK
    return jnp.swapaxes(t.reshape(n_chunk, 128, PAIRS_PER_CHUNK, 256), 1, 2).reshape(N_PAIR, 128, 256)


def _ssm_operands(w, lp):
    seg = _row_tile(lp, SSM_TILES) // 8
    ar, ai, bbr, bbi = _ssm_discretize(w["ssm_lambda_re"], w["ssm_lambda_im"], w["ssm_log_dt"], w["ssm_b_re"], w["ssm_b_im"])
    b_blk = jnp.concatenate([_chan_state_blocks(jnp.swapaxes(bbr, 1, 2)), _chan_state_blocks(jnp.swapaxes(bbi, 1, 2))], axis=2)
    c_blk = jnp.concatenate([_chan_state_blocks(w["ssm_c_re"]), -_chan_state_blocks(w["ssm_c_im"])], axis=2)
    dt = jnp.exp(w["ssm_log_dt"])[:, None]
    zr, zi = jnp.minimum(w["ssm_lambda_re"], LAMBDA_RE_MAX) * dt, w["ssm_lambda_im"] * dt
    b_cat, c_cat = _pairs_to_chunks(b_blk).astype(BF16), _pairs_to_chunks(c_blk).astype(BF16)
    return (b_cat, jnp.swapaxes(b_cat, 1, 2), c_cat, jnp.swapaxes(c_cat, 1, 2),
            _scan_tables(zr, zi, False, seg), _scan_tables(zr, zi, True, seg))


def _local_step(x, target, w, late_weights, on_grads):
    n_ex, seq, _ = x.shape
    lp = seq + BLOCK
    nb = lp // BLOCK
    n_rows = n_ex * lp
    g = {}

    head = jnp.concatenate([jnp.zeros((PAD, D_MODEL), F32), w["meta_tokens"]], axis=0)
    h0 = jnp.concatenate([jnp.broadcast_to(head[None], (n_ex, BLOCK, D_MODEL)), x], axis=1).reshape(n_rows, D_MODEL)

    qkv, hn_a = _rms_mm_cols(h0, w["attn_norm_w"], w["attn_w_qkv"], "qkv_fwd")
    att, lse = _attn_fwd(qkv, w["attn_sinks"], n_ex, nb)
    h1 = _mm_acc(att, w["attn_w_o"], False, "attn_out_fwd", res=h0)
    w = {**w, **late_weights(0, att)}
    h2, a0, hn_m0, u = _mlp_fwd(h1, w["mlp_norm_w"][0:1], w["mlp_w_up"][0], w["mlp_w_down"][0], "mlp0_fwd",
                                next_norm=w["ssm_norm_w"])
    late = late_weights(1, h2)
    w["ssm_w_glu"] = late["ssm_w_glu"]
    w["mlp_w_up"], w["mlp_w_down"] = w["mlp_w_up"] + late["mlp_w_up"], w["mlp_w_down"] + late["mlp_w_down"]

    ops = w["ssm_operands"] if "ssm_operands" in w else _ssm_operands(w, lp)
    b_pad, bt_pad, ct_pad, c_pad, tab_fwd, tab_rev = ops
    yg, y, xs, u_seg = _ssm_fwd(u, b_pad, c_pad, tab_fwd, w["ssm_d"], n_ex, lp)
    z = _mm_cols(yg, w["ssm_w_glu"], False, "glu_mm_fwd")
    dh4, a1, hn_m1, h3, loss_tile, dnorm_f = _mlp_fwd(
        h2, w["mlp_norm_w"][1:2], w["mlp_w_up"][1], w["mlp_w_down"][1], "mlp1_fwd", glu_z=z,
        loss=(w["final_norm_w"], target.reshape(n_ex * seq, D_MODEL), lp))

    def mlp_bwd(dh_out, h_in, a, hn, layer, tag, norm_w):
        dhn, dw_up, dw_down = None, None, None
        for s in range(N_CHIPS):
            final = s == N_CHIPS - 1
            res = _mlp_bwd_shard(s, dh_out, a, hn, dhn, h_in if final else None, norm_w,
                                 w["mlp_w_up"][layer], w["mlp_w_down"][layer], dw_up, dw_down, f"{tag}_bwd{s}")
            dhn, dw_up, dw_down = res[:3]
        return dhn, res[3], dw_up, dw_down

    dh3, dnorm_m1, dwu1, dwd1 = mlp_bwd(dh4, h3, a1, hn_m1, 1, "mlp1", w["mlp_norm_w"][1:2])
    tok = on_grads("mlp1", {"mlp_w_up": dwu1, "mlp_w_down": dwd1})
    dz, dyg = _glu_bwd(dh3, z, w["ssm_w_glu"])
    g["ssm_w_glu"] = _mm_tn(yg, dz, N_CHIPS, False, "glu_mm_dw")
    du, db_blk, dc_blk, da_t, dd_t = _ssm_bwd(dyg, y, u_seg, xs, ct_pad, bt_pad, tab_rev, w["ssm_d"] + tok, n_ex, lp)
    dh2, dnorm_s = _rms_bwd_call(du, h2, w["ssm_norm_w"], dh3, "ssm_norm_bwd")
    db_blk, dc_blk = _chunks_to_pairs(db_blk), _chunks_to_pairs(dc_blk)
    g["ssm_c_re"] = _chan_state_unblock(dc_blk[:, :, 0:128])
    g["ssm_c_im"] = -_chan_state_unblock(dc_blk[:, :, 128:256])
    g_bbr = jnp.swapaxes(_chan_state_unblock(db_blk[:, :, 0:128]), 1, 2)
    g_bbi = jnp.swapaxes(_chan_state_unblock(db_blk[:, :, 128:256]), 1, 2)
    g_a = jnp.sum(da_t, axis=2).reshape(N_PAIR, 2, 2, SSM_STATE)
    g_ar, g_ai = g_a[:, 0].reshape(SSM_NG, SSM_STATE), g_a[:, 1].reshape(SSM_NG, SSM_STATE)
    _, vjp = jax.vjp(_ssm_discretize, w["ssm_lambda_re"], w["ssm_lambda_im"], w["ssm_log_dt"], w["ssm_b_re"], w["ssm_b_im"])
    g["ssm_lambda_re"], g["ssm_lambda_im"], g["ssm_log_dt"], g["ssm_b_re"], g["ssm_b_im"] = vjp((g_ar, g_ai, g_bbr, g_bbi))
    tok = on_grads("ssm", g)
    g = {}
    dh1, dnorm_m0, dwu0, dwd0 = mlp_bwd(dh2, h1, a0, hn_m0, 0, "mlp0", w["mlp_norm_w"][0:1] + tok)
    datt = _mm_cols(dh1, w["attn_w_o"], True, "attn_out_dx")
    dw_o = _mm_tn(att, dh1, N_CHIPS, True, "attn_out_dw")
    tok = on_grads("mlp0", {"mlp_w_up": dwu0, "mlp_w_down": dwd0, "attn_w_o": dw_o})
    dqkv, dsink_rows = _attn_bwd(qkv, w["attn_sinks"] + tok, att, lse, datt, n_ex, nb)
    tok = on_grads("qkv", {"attn_w_qkv": _mm_tn(hn_a, dqkv, N_CHIPS, False, "qkv_dw")})
    dh0, dnorm_a = _mm_acc(dqkv, w["attn_w_qkv"], True, "qkv_dx", rms_bwd=(h0, w["attn_norm_w"] + tok, dh1))

    dh0 = dh0.reshape(n_ex, lp, D_MODEL)
    on_grads("rest", {
        "mlp_norm_w": jnp.stack([jnp.sum(dnorm_m0, axis=0), jnp.sum(dnorm_m1, axis=0)]),
        "final_norm_w": jnp.sum(dnorm_f, axis=0),
        "attn_norm_w": jnp.sum(dnorm_a, axis=0)[None],
        "ssm_norm_w": jnp.sum(dnorm_s, axis=0)[None],
        "attn_sinks": jnp.sum(dsink_rows, axis=0)[None],
        "ssm_d": jnp.sum(dd_t, axis=0)[None],
        "meta_tokens": jnp.sum(dh0[:, PAD:BLOCK], axis=0),
        "loss": loss_tile[0, 0:1]})
    return loss_tile, dh0[:, BLOCK:]


_SHARDED_SMALL = ("meta_tokens", "ssm_norm_w", "ssm_d")
_REP_SSM = ("ssm_lambda_re", "ssm_lambda_im", "ssm_log_dt", "ssm_b_re", "ssm_b_im", "ssm_c_re", "ssm_c_im")
_REP_MISC = ("attn_norm_w", "attn_sinks", "mlp_norm_w", "final_norm_w")
_BIG = ("attn_w_qkv", "attn_w_o", "ssm_w_glu", "mlp_w_up", "mlp_w_down")


def _pack(parts, cols):
    flat = jnp.concatenate([p.reshape(-1) for p in parts])
    rows = -(-flat.shape[0] // (8 * cols)) * 8
    return jnp.pad(flat, (0, rows * cols - flat.shape[0])).reshape(rows, cols)


def _unpack(packed, like):
    flat = packed.reshape(-1)
    out, at = [], 0
    for p in like:
        out.append(flat[at:at + p.size].reshape(p.shape))
        at += p.size
    return out


def kernel(x, meta_tokens, attn_norm_w, attn_w_qkv, attn_sinks, attn_w_o, ssm_norm_w, ssm_lambda_re, ssm_lambda_im, ssm_log_dt, ssm_b_re, ssm_b_im, ssm_c_re, ssm_c_im, ssm_d, ssm_w_glu, mlp_norm_w, mlp_w_up, mlp_w_down, final_norm_w, loss_target, m_meta_tokens, m_attn_norm_w, m_attn_w_qkv, m_attn_sinks, m_attn_w_o, m_ssm_norm_w, m_ssm_lambda_re, m_ssm_lambda_im, m_ssm_log_dt, m_ssm_b_re, m_ssm_b_im, m_ssm_c_re, m_ssm_c_im, m_ssm_d, m_ssm_w_glu, m_mlp_norm_w, m_mlp_w_up, m_mlp_w_down, m_final_norm_w, v_meta_tokens, v_attn_norm_w, v_attn_w_qkv, v_attn_sinks, v_attn_w_o, v_ssm_norm_w, v_ssm_lambda_re, v_ssm_lambda_im, v_ssm_log_dt, v_ssm_b_re, v_ssm_b_im, v_ssm_c_re, v_ssm_c_im, v_ssm_d, v_ssm_w_glu, v_mlp_norm_w, v_mlp_w_up, v_mlp_w_down, v_final_norm_w):
    names = ("meta_tokens", "attn_norm_w", "attn_w_qkv", "attn_sinks", "attn_w_o", "ssm_norm_w", "ssm_lambda_re",
             "ssm_lambda_im", "ssm_log_dt", "ssm_b_re", "ssm_b_im", "ssm_c_re", "ssm_c_im", "ssm_d", "ssm_w_glu",
             "mlp_norm_w", "mlp_w_up", "mlp_w_down", "final_norm_w")
    wts = dict(zip(names, (meta_tokens, attn_norm_w, attn_w_qkv, attn_sinks, attn_w_o, ssm_norm_w, ssm_lambda_re,
                           ssm_lambda_im, ssm_log_dt, ssm_b_re, ssm_b_im, ssm_c_re, ssm_c_im, ssm_d, ssm_w_glu,
                           mlp_norm_w, mlp_w_up, mlp_w_down, final_norm_w)))
    mom = dict(zip(names, (m_meta_tokens, m_attn_norm_w, m_attn_w_qkv, m_attn_sinks, m_attn_w_o, m_ssm_norm_w,
                           m_ssm_lambda_re, m_ssm_lambda_im, m_ssm_log_dt, m_ssm_b_re, m_ssm_b_im, m_ssm_c_re,
                           m_ssm_c_im, m_ssm_d, m_ssm_w_glu, m_mlp_norm_w, m_mlp_w_up, m_mlp_w_down, m_final_norm_w)))
    var = dict(zip(names, (v_meta_tokens, v_attn_norm_w, v_attn_w_qkv, v_attn_sinks, v_attn_w_o, v_ssm_norm_w,
                           v_ssm_lambda_re, v_ssm_lambda_im, v_ssm_log_dt, v_ssm_b_re, v_ssm_b_im, v_ssm_c_re,
                           v_ssm_c_im, v_ssm_d, v_ssm_w_glu, v_mlp_norm_w, v_mlp_w_up, v_mlp_w_down, v_final_norm_w)))

    my_chip = 2 * lax.axis_index("x") + lax.axis_index("y")
    my_dev = 2 * my_chip + lax.axis_index("c")
    small_mine = _pack([wts[n] for n in _SHARDED_SMALL], 128)
    first = [attn_w_qkv.astype(BF16), attn_w_o.astype(BF16), small_mine]
    with_landing = lambda srcs: (srcs, [_landing(a, my_chip, N_CHIPS) for a in srcs])
    handles, token = _split_start([with_landing(first)], _gather_copies, N_GATHER_PEERS, "gather_start_first")
    up16, down16 = (mlp_w_up + token[0, 0]).astype(BF16), (mlp_w_down + token[0, 0]).astype(BF16)
    mlp0 = [up16[0:1], down16[0:1]]
    rest = [(ssm_w_glu + token[0, 0]).astype(BF16), up16[1:2], down16[1:2]]
    later, _ = _split_start([with_landing(mlp0), with_landing(rest)], _gather_copies, N_GATHER_PEERS, "gather_start_later")
    handles = handles + later
    full = {n: wts[n] for n in _REP_MISC}
    full["final_norm_w"] = final_norm_w[None]
    for n in _REP_SSM:
        full[n] = wts[n][0]
    full["ssm_operands"] = _ssm_operands(full, x.shape[1] + BLOCK)
    got = _split_wait(handles[0], full["ssm_operands"], _gather_copies, "gather_wait_first")
    full["attn_w_qkv"], full["attn_w_o"] = got[0], got[1]
    smalls = [_unpack(got[2][s], [wts[n] for n in _SHARDED_SMALL]) for s in range(N_CHIPS)]
    for k, n in enumerate(_SHARDED_SMALL):
        full[n] = jnp.concatenate([smalls[s][k] for s in range(N_CHIPS)], axis=1)

    def late_weights(stage, after):
        if stage == 0:
            up, down = _split_wait(handles[1], after, _gather_copies, "gather_wait_mlp0")
            return {"mlp_w_up": [up], "mlp_w_down": [down]}
        glu, up, down = _split_wait(handles[2], after, _gather_copies, "gather_wait_rest")
        return {"ssm_w_glu": glu, "mlp_w_up": [up], "mlp_w_down": [down]}

    def shard_cols(t):
        return jnp.swapaxes(t.reshape(t.shape[0], N_CHIPS, t.shape[1] // N_CHIPS), 0, 1)

    pending = {}

    def on_grads(tag, g):
        scatter = [g[n] for n in _BIG if n in g]
        whole = []
        if tag == "ssm":
            whole = [_pack([g[n] for n in _REP_SSM], D_MODEL)]
        if tag == "rest":
            parts = [shard_cols(g[n]) for n in _SHARDED_SMALL]
            scatter = [jnp.stack([_pack([p[s] for p in parts], 128) for s in range(N_CHIPS)])]
            whole = [_pack([g[n] for n in _REP_MISC] + [g["loss"]], D_MODEL)]
        srcs = scatter + whole
        lands = [_landing(lax.dynamic_index_in_dim(a, my_chip, 0, keepdims=False), my_dev, N_DEV) for a in scatter]
        lands += [_landing(a, my_dev, N_DEV) for a in whole]
        hs, token = _split_start([(srcs, lands)], _exchange_copies(len(scatter)), N_EXCHANGE_PEERS, "exchange_start_" + tag)
        pending[tag] = (hs[0], len(scatter))
        return token[0, 0]

    _, grad_x = _local_step(x, loss_target, full, late_weights, on_grads)

    recv = {}
    for tag, (handle, n_scatter) in pending.items():
        recv[tag] = _split_wait(handle, grad_x, _exchange_copies(n_scatter), "exchange_wait_" + tag)
    loss = jnp.sum(recv["rest"][1].reshape(N_DEV, -1)[:, sum(wts[n].size for n in _REP_MISC)])

    out = {}

    def update(tag, pieces, w2, m2, v2):
        return _adamw(pieces, w2, m2, v2, "adamw_" + tag)

    def update_weight(n, pieces):
        shp = wts[n].shape
        r2 = (math.prod(shp[:-1]), shp[-1])
        res = update(n, pieces, wts[n].reshape(r2), mom[n].reshape(r2), var[n].reshape(r2))
        out[n] = [t.reshape(shp) for t in res]

    update_weight("mlp_w_up", [recv["mlp0"][1], recv["mlp1"][0]])
    update_weight("mlp_w_down", [recv["mlp0"][2], recv["mlp1"][1]])
    update_weight("attn_w_o", [recv["mlp0"][0]])
    update_weight("ssm_w_glu", [recv["ssm"][0]])
    update_weight("attn_w_qkv", [recv["qkv"][0]])
    for tag, group, pieces, cols in (("small", _SHARDED_SMALL, recv["rest"][0], 128),
                                     ("rep_ssm", _REP_SSM, recv["ssm"][1], D_MODEL),
                                     ("rep_misc", _REP_MISC, recv["rest"][1], D_MODEL)):
        like = [wts[n] for n in group]
        res = update(tag, [pieces], _pack(like, cols), _pack([mom[n] for n in group], cols),
                     _pack([var[n] for n in group], cols))
        for k, n in enumerate(group):
            out[n] = [_unpack(t, like)[k] for t in res]

    return (loss, grad_x, *[out[n][0] for n in names], *[out[n][1] for n in names],
            *[out[n][2] for n in names], *[out[n][3] for n in names])
```

```python
import functools
import math

import jax
import jax.numpy as jnp
from jax import lax
from jax.experimental import pallas as pl
from jax.experimental.pallas import tpu as pltpu

F32 = jnp.float32
BF16 = jnp.bfloat16
SDS = jax.ShapeDtypeStruct

D_MODEL = 1024
N_HEADS = 16
N_KV = 4
GQA = N_HEADS // N_KV
HEAD_DIM = 64
BLOCK = 128
N_META = 16
PAD = BLOCK - N_META
QKV_DIM = (N_HEADS + 2 * N_KV) * HEAD_DIM
KV_DIM = 2 * N_KV * HEAD_DIM
D_FF = 4 * D_MODEL
N_CHIPS = 4
N_DEV = 8
SSM_GROUP = 16
SSM_NG = D_MODEL // SSM_GROUP
SSM_STATE = 64
N_PAIR = SSM_NG // 2
PAIRS_PER_CHUNK = 4
RMS_EPS = 1e-6
NEG_INF = -1e30
LAMBDA_RE_MAX = -1e-4
ADAM_LR, ADAM_B1, ADAM_B2, ADAM_EPS, ADAM_WD, ADAM_STEP = 0.001, 0.9, 0.999, 1e-08, 0.01, 10

TM = 384
MM_TILES = (1056, 768, 384)
MLP_FWD_TILES = (384,)
MLP_BWD_TILES = (768, 384)
TN_TILES = (1408, 768, 384)
VMEM_LIMIT = 56 * 1024 * 1024


def _params(n_grid):
    return pltpu.CompilerParams(dimension_semantics=("arbitrary",) * n_grid, vmem_limit_bytes=VMEM_LIMIT)


def _row_tile(n_rows, tiles):
    return next(t for t in tiles if n_rows % t == 0)


def _rms(h, w):
    r = lax.rsqrt(jnp.mean(h * h, axis=-1, keepdims=True) + RMS_EPS)
    return h * r * w


def _rms_bwd(dhn, h, w):
    r = lax.rsqrt(jnp.mean(h * h, axis=-1, keepdims=True) + RMS_EPS)
    g = dhn * w
    proj = jnp.sum(g * h, axis=-1, keepdims=True) * (1.0 / D_MODEL)
    return r * g - h * (r * r * r) * proj, dhn * h * r


def _fold8(t):
    return jnp.sum(t.reshape(t.shape[0] // 8, 8, t.shape[1]), axis=0)


def _gelu(y):
    return 0.5 * y * (1.0 + jnp.tanh(0.7978845608028654 * (y + 0.044715 * y * y * y)))


def _gelu_grad(y):
    t = jnp.tanh(0.7978845608028654 * (y + 0.044715 * y * y * y))
    return 0.5 * (1.0 + t) + 0.5 * y * (1.0 - t * t) * 0.7978845608028654 * (1.0 + 3.0 * 0.044715 * y * y)


def _w4_spec(w4):
    n_sh, _, k, n = w4.shape
    return pl.BlockSpec((n_sh, None, k, n), lambda i: (0, 0, 0, 0))


def _rms_mm_cols(h, wn, w4, name):
    n_rows = h.shape[0]
    n_sh, _, k, n = w4.shape
    tm = _row_tile(n_rows, MM_TILES)

    def body(h_ref, wn_ref, w_ref, o_ref, hn_ref):
        hn = _rms(h_ref[...], wn_ref[...]).astype(BF16)
        hn_ref[...] = hn
        for s in range(n_sh):
            o_ref[:, s * n:(s + 1) * n] = jnp.dot(hn, w_ref[s], preferred_element_type=F32).astype(o_ref.dtype)

    return pl.pallas_call(
        body, name=name, grid=(n_rows // tm,),
        in_specs=[pl.BlockSpec((tm, k), lambda i: (i, 0)), pl.BlockSpec((1, k), lambda i: (0, 0)), _w4_spec(w4)],
        out_specs=[pl.BlockSpec((tm, n_sh * n), lambda i: (i, 0)), pl.BlockSpec((tm, k), lambda i: (i, 0))],
        out_shape=[SDS((n_rows, n_sh * n), BF16), SDS((n_rows, k), BF16)],
        compiler_params=_params(1),
    )(h, wn, w4)


def _mm_cols(x, w4, trans_w, name):
    n_rows, kx = x.shape
    tm = _row_tile(n_rows, MM_TILES)
    n_sh, _, k, n = w4.shape
    n_out = k if trans_w else n
    dims = (((1,), (1,)), ((), ())) if trans_w else (((1,), (0,)), ((), ()))

    def body(x_ref, w_ref, o_ref):
        x16 = x_ref[...].astype(BF16)
        for s in range(n_sh):
            o_ref[:, s * n_out:(s + 1) * n_out] = lax.dot_general(
                x16, w_ref[s], dims, preferred_element_type=F32).astype(o_ref.dtype)

    return pl.pallas_call(
        body, name=name, grid=(n_rows // tm,),
        in_specs=[pl.BlockSpec((tm, kx), lambda i: (i, 0)), _w4_spec(w4)],
        out_specs=pl.BlockSpec((tm, n_sh * n_out), lambda i: (i, 0)),
        out_shape=SDS((n_rows, n_sh * n_out), BF16),
        compiler_params=_params(1),
    )(x, w4)


def _mm_acc(x, w4, trans_w, name, res=None, rms_bwd=None):
    n_rows = x.shape[0]
    tm = _row_tile(n_rows, MM_TILES)
    n_sh, _, k, n = w4.shape
    kx, n_out = (n, k) if trans_w else (k, n)
    dims = (((1,), (1,)), ((), ())) if trans_w else (((1,), (0,)), ((), ()))

    def body(*refs):
        if rms_bwd is not None:
            x_ref, w_ref, h_ref, wn_ref, dres_ref, o_ref, dw_ref = refs
        elif res is not None:
            x_ref, w_ref, res_ref, o_ref = refs
        else:
            x_ref, w_ref, o_ref = refs
        acc = None
        for s in range(n_sh):
            part = lax.dot_general(x_ref[:, s * kx:(s + 1) * kx].astype(BF16), w_ref[s], dims, preferred_element_type=F32)
            acc = part if acc is None else acc + part
        if rms_bwd is not None:
            dh, dw_rows = _rms_bwd(acc, h_ref[...], wn_ref[...])
            o_ref[...] = (dres_ref[...] + dh).astype(o_ref.dtype)

            @pl.when(pl.program_id(0) == 0)
            def _():
                dw_ref[...] = jnp.zeros_like(dw_ref)

            dw_ref[...] += _fold8(dw_rows)
        elif res is not None:
            o_ref[...] = (res_ref[...] + acc).astype(o_ref.dtype)
        else:
            o_ref[...] = acc.astype(o_ref.dtype)

    row = lambda i: (i, 0)
    in_specs = [pl.BlockSpec((tm, n_sh * kx), row), _w4_spec(w4)]
    args = [x, w4]
    out_specs = pl.BlockSpec((tm, n_out), row)
    out_shape = SDS((n_rows, n_out), F32)
    if rms_bwd is not None:
        h, wn, dres = rms_bwd
        in_specs += [pl.BlockSpec((tm, n_out), row), pl.BlockSpec((1, n_out), lambda i: (0, 0)),
                     pl.BlockSpec((tm, n_out), row)]
        args += [h, wn, dres]
        out_specs = [out_specs, pl.BlockSpec((8, n_out), lambda i: (0, 0))]
        out_shape = [out_shape, SDS((8, n_out), F32)]
    elif res is not None:
        in_specs.append(pl.BlockSpec((tm, n_out), row))
        args.append(res)
    return pl.pallas_call(
        body, name=name, grid=(n_rows // tm,), in_specs=in_specs, out_specs=out_specs, out_shape=out_shape,
        compiler_params=_params(1),
    )(*args)


def _mm_tn(a, b, n_sh, a_sharded, name):
    n_rows = a.shape[0]
    tm = _row_tile(n_rows, TN_TILES)
    ka = a.shape[1] // n_sh if a_sharded else a.shape[1]
    nb = b.shape[1] if a_sharded else b.shape[1] // n_sh
    n_i = n_rows // tm

    def body(a_ref, b_ref, o_ref, acc):
        i = pl.program_id(0)

        @pl.when(i == 0)
        def _():
            acc[...] = jnp.zeros_like(acc)

        for s in range(n_sh):
            a_s = a_ref[:, s * ka:(s + 1) * ka] if a_sharded else a_ref[...]
            b_s = b_ref[...] if a_sharded else b_ref[:, s * nb:(s + 1) * nb]
            acc[s] += lax.dot_general(a_s.astype(BF16), b_s.astype(BF16), (((0,), (0,)), ((), ())),
                                      preferred_element_type=F32)

        @pl.when(i == n_i - 1)
        def _():
            o_ref[...] = acc[...].astype(o_ref.dtype)

    return pl.pallas_call(
        body, name=name, grid=(n_i,),
        in_specs=[pl.BlockSpec((tm, a.shape[1]), lambda i: (i, 0)), pl.BlockSpec((tm, b.shape[1]), lambda i: (i, 0))],
        out_specs=pl.BlockSpec((n_sh, ka, nb), lambda i: (0, 0, 0)),
        out_shape=SDS((n_sh, ka, nb), BF16),
        scratch_shapes=[pltpu.VMEM((n_sh, ka, nb), F32)], compiler_params=_params(1),
    )(a, b)


def _mlp_fwd(h, wn, w_up4, w_down4, name, next_norm=None, glu_z=None, loss=None):
    n_rows = h.shape[0]
    tm = _row_tile(n_rows, MLP_FWD_TILES)
    n_sh = w_up4.shape[0]
    f_sh = D_FF // n_sh
    w_down = w_down4.reshape(D_FF, D_MODEL)
    per_tile = tm // BLOCK

    def body(*refs):
        refs = list(refs)
        h_ref, wn_ref, wu_ref, wd_ref = refs[:4]
        at = 4
        if next_norm is not None:
            nn_ref = refs[at]
            at += 1
        if glu_z is not None:
            z_ref = refs[at]
            at += 1
        if loss is not None:
            fw_ref, t_refs = refs[at], refs[at + 1:at + 1 + per_tile]
            at += 1 + per_tile
        o_ref, a_ref, hn_ref = refs[at:at + 3]
        at += 3
        if next_norm is not None:
            u_ref = refs[at]
            at += 1
        if glu_z is not None:
            hin_ref = refs[at]
            at += 1
        if loss is not None:
            loss_ref, dfw_ref = refs[at:at + 2]
            at += 2
        act_s = refs[at]
        h_in = h_ref[...]
        if glu_z is not None:
            h_in = h_in + z_ref[:, 0:D_MODEL].astype(F32) * jax.nn.sigmoid(z_ref[:, D_MODEL:2 * D_MODEL].astype(F32))
            hin_ref[...] = h_in
        hn = _rms(h_in, wn_ref[...]).astype(BF16)
        hn_ref[...] = hn
        for s in range(n_sh):
            cols = slice(s * f_sh, (s + 1) * f_sh)
            a = jnp.dot(hn, wu_ref[s], preferred_element_type=F32)
            a_ref[:, cols] = a.astype(BF16)
            act = jnp.maximum(a, 0.0)
            act_s[:, cols] = (act * act).astype(BF16)
        out = h_in + jnp.dot(act_s[...], wd_ref[...], preferred_element_type=F32)
        if next_norm is not None:
            u_ref[...] = _rms(out, nn_ref[...])
        if loss is None:
            o_ref[...] = out
        else:
            i = pl.program_id(0)
            first_of_example = i % tiles_per_example == 0

            @pl.when(i == 0)
            def _():
                loss_ref[...] = jnp.zeros_like(loss_ref)
                dfw_ref[...] = jnp.zeros_like(dfw_ref)

            def block(k):
                rows = slice(k * BLOCK, (k + 1) * BLOCK)
                diff = _rms(out[rows], fw_ref[...]) - t_refs[k][...]
                loss_ref[...] += 0.5 * jnp.sum(diff * diff) * (1.0 / D_MODEL)
                dh, dw_rows = _rms_bwd(diff * (1.0 / D_MODEL), out[rows], fw_ref[...])
                o_ref[rows, :] = dh
                dfw_ref[...] += _fold8(dw_rows)

            @pl.when(first_of_example)
            def _():
                o_ref[0:BLOCK, :] = jnp.zeros((BLOCK, D_MODEL), F32)

            pl.when(jnp.logical_not(first_of_example))(lambda: block(0))
            for k in range(1, per_tile):
                block(k)

    row = lambda i: (i, 0)
    vec = pl.BlockSpec((1, D_MODEL), lambda i: (0, 0))
    in_specs = [pl.BlockSpec((tm, D_MODEL), row), vec,
                pl.BlockSpec((n_sh, None, D_MODEL, f_sh), lambda i: (0, 0, 0, 0), pipeline_mode=pl.Buffered(1)),
                pl.BlockSpec((D_FF, D_MODEL), lambda i: (0, 0), pipeline_mode=pl.Buffered(1))]
    out_specs = [pl.BlockSpec((tm, D_MODEL), row), pl.BlockSpec((tm, D_FF), row), pl.BlockSpec((tm, D_MODEL), row)]
    out_shape = [SDS((n_rows, D_MODEL), F32), SDS((n_rows, D_FF), BF16), SDS((n_rows, D_MODEL), BF16)]
    args = [h, wn, w_up4, w_down]
    if next_norm is not None:
        in_specs.append(vec)
        args.append(next_norm)
    if glu_z is not None:
        in_specs.append(pl.BlockSpec((tm, 2 * D_MODEL), row))
        args.append(glu_z)
    for extra in (next_norm, glu_z):
        if extra is not None:
            out_specs.append(pl.BlockSpec((tm, D_MODEL), row))
            out_shape.append(SDS((n_rows, D_MODEL), F32))
    if loss is not None:
        final_wn, target, lp = loss
        tiles_per_example = lp // tm
        real_blocks = lp // BLOCK - 1

        def t_spec(k):
            return pl.BlockSpec((BLOCK, D_MODEL), lambda i: (
                (i // tiles_per_example) * real_blocks + jnp.maximum(per_tile * (i % tiles_per_example) + k - 1, 0), 0))

        in_specs += [vec] + [t_spec(k) for k in range(per_tile)]
        args += [final_wn] + [target] * per_tile
        out_specs += [pl.BlockSpec((8, 128), lambda i: (0, 0)), pl.BlockSpec((8, D_MODEL), lambda i: (0, 0))]
        out_shape += [SDS((8, 128), F32), SDS((8, D_MODEL), F32)]
    return pl.pallas_call(
        body, name=name, grid=(n_rows // tm,), in_specs=in_specs, out_specs=out_specs, out_shape=out_shape,
        scratch_shapes=[pltpu.VMEM((tm, D_FF), BF16)],
        compiler_params=_params(1),
    )(*args)


def _mlp_bwd_shard(s, dh, a, hn, dhn_prev, h, wn, w_up4, w_down4, dw_up_buf, dw_down_buf, name):
    n_rows = dh.shape[0]
    n_sh = w_up4.shape[0]
    f_sh = D_FF // n_sh
    tm = _row_tile(n_rows, MLP_BWD_TILES)
    n_i = n_rows // tm
    last = h is not None
    nt = (((1,), (1,)), ((), ()))
    tn = (((0,), (0,)), ((), ()))

    def body(*refs):
        refs = list(refs)
        dh_ref, a_ref, hn_ref, wu_ref, wd_ref = refs[:5]
        at = 5
        prev_ref = None
        if dhn_prev is not None:
            prev_ref = refs[at]
            at += 1
        if last:
            h_ref, wn_ref = refs[at:at + 2]
            at += 2
        if dw_up_buf is not None:
            at += 2
        o_ref, dwu_ref, dwd_ref = refs[at:at + 3]
        at += 3
        if last:
            dnorm_ref = refs[at]
            at += 1
        acc_u, acc_d = refs[at:at + 2]
        i = pl.program_id(0)

        @pl.when(i == 0)
        def _():
            acc_u[...] = jnp.zeros_like(acc_u)
            acc_d[...] = jnp.zeros_like(acc_d)
            if last:
                dnorm_ref[...] = jnp.zeros_like(dnorm_ref)

        dh16 = dh_ref[...].astype(BF16)
        r = jnp.maximum(a_ref[...].astype(F32), 0.0)
        dact = lax.dot_general(dh16, wd_ref[...], nt, preferred_element_type=F32)
        da16 = (dact * (2.0 * r)).astype(BF16)
        acc_d[...] += lax.dot_general((r * r).astype(BF16), dh16, tn, preferred_element_type=F32)
        acc_u[...] += lax.dot_general(hn_ref[...], da16, tn, preferred_element_type=F32)
        dhn = lax.dot_general(da16, wu_ref[...], nt, preferred_element_type=F32)
        if prev_ref is not None:
            dhn = dhn + prev_ref[...]
        if last:
            d_rms, dw_rows = _rms_bwd(dhn, h_ref[...], wn_ref[...])
            o_ref[...] = dh_ref[...] + d_rms
            dnorm_ref[...] += _fold8(dw_rows)
        else:
            o_ref[...] = dhn

        @pl.when(i == n_i - 1)
        def _():
            dwu_ref[...] = acc_u[...].astype(BF16)
            dwd_ref[...] = acc_d[...].astype(BF16)

    row = lambda i: (i, 0)
    tile = pl.BlockSpec((tm, D_MODEL), row)
    in_specs = [tile, pl.BlockSpec((tm, f_sh), lambda i: (i, s)), tile,
                pl.BlockSpec((None, None, D_MODEL, f_sh), lambda i: (s, 0, 0, 0)),
                pl.BlockSpec((None, None, f_sh, D_MODEL), lambda i: (s, 0, 0, 0))]
    args = [dh, a, hn, w_up4, w_down4]
    if dhn_prev is not None:
        in_specs.append(tile)
        args.append(dhn_prev)
    if last:
        in_specs += [tile, pl.BlockSpec((1, D_MODEL), lambda i: (0, 0))]
        args += [h, wn]
    aliases = {}
    if dw_up_buf is not None:
        aliases = {len(args): 1, len(args) + 1: 2}
        in_specs += [pl.BlockSpec(memory_space=pl.ANY)] * 2
        args += [dw_up_buf, dw_down_buf]
    out_specs = [tile, pl.BlockSpec((None, D_MODEL, f_sh), lambda i: (s, 0, 0)),
                 pl.BlockSpec((None, f_sh, D_MODEL), lambda i: (s, 0, 0))]
    out_shape = [SDS((n_rows, D_MODEL), F32), SDS((n_sh, D_MODEL, f_sh), BF16), SDS((n_sh, f_sh, D_MODEL), BF16)]
    if last:
        out_specs.append(pl.BlockSpec((8, D_MODEL), lambda i: (0, 0)))
        out_shape.append(SDS((8, D_MODEL), F32))
    return pl.pallas_call(
        body, name=name, grid=(n_i,), in_specs=in_specs, out_specs=out_specs, out_shape=out_shape,
        input_output_aliases=aliases,
        scratch_shapes=[pltpu.VMEM((D_MODEL, f_sh), F32), pltpu.VMEM((f_sh, D_MODEL), F32)],
        compiler_params=_params(1),
    )(*args)


def _attn_masks(n):
    qi = lax.broadcasted_iota(jnp.int32, (BLOCK, 3 * BLOCK), 0)
    col = lax.broadcasted_iota(jnp.int32, (BLOCK, 3 * BLOCK), 1)
    kj = col - BLOCK
    dist = BLOCK + qi - kj
    kmin = jnp.where(n == 0, 2 * BLOCK, jnp.where(n == 1, BLOCK, 0))
    band_ok = (col >= BLOCK) & (dist >= 0) & (dist < BLOCK) & (kj >= kmin)
    q_pos = n * BLOCK + qi - PAD
    meta_ok = (col >= PAD) & (col < BLOCK) & (col - PAD <= q_pos)
    distf = jnp.where(col >= BLOCK, dist, 0).astype(F32)
    return band_ok | meta_ok, distf


def _alibi_slope(h):
    return float(2.0 ** (-8.0 * (h + 1) / N_HEADS))


def _attn_bias(n, bias_s):
    ok, distf = _attn_masks(n)
    for h in range(N_HEADS):
        bias_s[h] = jnp.where(ok, -_alibi_slope(h) * distf, NEG_INF)


def _attn_fwd(qkv, sinks, n_ex, nb):
    n_rows = qkv.shape[0]
    kvb = N_HEADS * HEAD_DIM // KV_DIM

    def body(sink_ref, q_ref, kvm_ref, kvp_ref, kvc_ref, o_ref, lse_ref, k_s, v_s, q_s, bias_s):
        n = pl.program_id(1)

        @pl.when(n <= 2)
        def _():
            _attn_bias(n, bias_s)

        v_s[...] = jnp.ones_like(v_s)
        for part, ref in enumerate((kvm_ref, kvp_ref, kvc_ref)):
            rows = slice(part * BLOCK, (part + 1) * BLOCK)
            k_s[rows, :] = ref[:, 0:N_KV * HEAD_DIM]
            for kv in range(N_KV):
                v_s[rows, kv * 2 * HEAD_DIM:kv * 2 * HEAD_DIM + HEAD_DIM] = \
                    ref[:, (N_KV + kv) * HEAD_DIM:(N_KV + kv + 1) * HEAD_DIM]
        for h in range(N_HEADS):
            q_s[h // GQA, (h % GQA) * BLOCK:(h % GQA + 1) * BLOCK, :] = \
                q_ref[:, h * HEAD_DIM:(h + 1) * HEAD_DIM] * (HEAD_DIM ** -0.5)

        def scores(kv):
            return lax.dot_general(q_s[kv], k_s[:, kv * HEAD_DIM:(kv + 1) * HEAD_DIM], (((1,), (1,)), ((), ())),
                                   preferred_element_type=F32)

        ahead = scores(0)
        for kv in range(N_KV):
            s4 = ahead
            if kv + 1 < N_KV:
                ahead = scores(kv + 1)
            es, ms, sink_es = [], [], []
            for g in range(GQA):
                h = kv * GQA + g
                s = s4[g * BLOCK:(g + 1) * BLOCK] + bias_s[h]
                sink = sink_ref[0, h]
                m = jnp.maximum(jnp.max(s, axis=-1, keepdims=True), sink)
                es.append(jnp.exp(s - m).astype(BF16))
                ms.append(m)
                sink_es.append(jnp.exp(sink - m))
            pv = jnp.dot(jnp.concatenate(es, axis=0), v_s[:, kv * 2 * HEAD_DIM:(kv + 1) * 2 * HEAD_DIM],
                         preferred_element_type=F32)
            for g in range(GQA):
                h = kv * GQA + g
                pg = pv[g * BLOCK:(g + 1) * BLOCK]
                l = pg[:, HEAD_DIM:HEAD_DIM + 1] + sink_es[g]
                o_ref[:, h * HEAD_DIM:(h + 1) * HEAD_DIM] = (pg[:, 0:HEAD_DIM] * (1.0 / l)).astype(BF16)
                lse_ref[:, h:h + 1] = ms[g] + jnp.log(l)

    return pl.pallas_call(
        body, name="attn_fwd", grid=(n_ex, nb),
        in_specs=[pl.BlockSpec(memory_space=pltpu.SMEM),
                  pl.BlockSpec((BLOCK, N_HEADS * HEAD_DIM), lambda b, n: (b * nb + n, 0)),
                  pl.BlockSpec((BLOCK, KV_DIM), lambda b, n: (b * nb, kvb)),
                  pl.BlockSpec((BLOCK, KV_DIM), lambda b, n: (b * nb + jnp.maximum(n - 1, 0), kvb)),
                  pl.BlockSpec((BLOCK, KV_DIM), lambda b, n: (b * nb + n, kvb))],
        out_specs=[pl.BlockSpec((BLOCK, N_HEADS * HEAD_DIM), lambda b, n: (b * nb + n, 0)),
                   pl.BlockSpec((BLOCK, N_HEADS), lambda b, n: (b * nb + n, 0))],
        out_shape=[SDS((n_rows, N_HEADS * HEAD_DIM), BF16), SDS((n_rows, N_HEADS), F32)],
        scratch_shapes=[pltpu.VMEM((3 * BLOCK, N_KV * HEAD_DIM), BF16), pltpu.VMEM((3 * BLOCK, 2 * N_KV * HEAD_DIM), BF16),
                        pltpu.VMEM((N_KV, GQA * BLOCK, HEAD_DIM), BF16), pltpu.VMEM((N_HEADS, BLOCK, 3 * BLOCK), F32)],
        compiler_params=_params(2),
    )(sinks, qkv, qkv, qkv, qkv)


def _attn_bwd(qkv, sinks, o, lse, do, n_ex, nb):
    n_rows = qkv.shape[0]
    kvb = N_HEADS * HEAD_DIM // KV_DIM
    scale = HEAD_DIM ** -0.5
    nq = lambda r: nb - 1 - r

    def body(sink_ref, q_ref, kvm_ref, kvp_ref, kvc_ref, o_ref, lse_ref, do_ref, dqkv_ref, dsink_ref,
             k_s, v_s, dkv_s, carry_s, meta_s, q_s, do_s, bias_s):
        b, r = pl.program_id(0), pl.program_id(1)
        n = nq(r)

        @pl.when((r == 0) | (n <= 1))
        def _():
            _attn_bias(n, bias_s)

        @pl.when((b == 0) & (r == 0))
        def _():
            dsink_ref[...] = jnp.zeros_like(dsink_ref)

        @pl.when(r == 0)
        def _():
            carry_s[...] = jnp.zeros_like(carry_s)
            meta_s[...] = jnp.zeros_like(meta_s)

        for part, ref in enumerate((kvm_ref, kvp_ref, kvc_ref)):
            k_s[part * BLOCK:(part + 1) * BLOCK, :] = ref[:, 0:N_KV * HEAD_DIM]
            v_s[part * BLOCK:(part + 1) * BLOCK, :] = ref[:, N_KV * HEAD_DIM:KV_DIM]
        nt = (((1,), (1,)), ((), ()))
        tn = (((0,), (0,)), ((), ()))
        deltas = []
        for h in range(N_HEADS):
            rows = slice((h % GQA) * BLOCK, (h % GQA + 1) * BLOCK)
            cols = slice(h * HEAD_DIM, (h + 1) * HEAD_DIM)
            q_s[h // GQA, rows, :] = q_ref[:, cols] * scale
            do_s[h // GQA, rows, :] = do_ref[:, cols]
            deltas.append(jnp.sum(do_ref[:, cols].astype(F32) * o_ref[:, cols].astype(F32), axis=-1, keepdims=True))
        for kv in range(N_KV):
            kcols = slice(kv * HEAD_DIM, (kv + 1) * HEAD_DIM)
            vcols = slice(N_KV * HEAD_DIM + kv * HEAD_DIM, N_KV * HEAD_DIM + (kv + 1) * HEAD_DIM)
            kh, vh = k_s[:, kcols], v_s[:, kcols]
            s4 = lax.dot_general(q_s[kv], kh, nt, preferred_element_type=F32)
            dp4 = lax.dot_general(do_s[kv], vh, nt, preferred_element_type=F32)
            ps, dss = [], []
            for g in range(GQA):
                h = kv * GQA + g
                cols = slice(h * HEAD_DIM, (h + 1) * HEAD_DIM)
                rows = slice(g * BLOCK, (g + 1) * BLOCK)
                s = s4[rows] + bias_s[h]
                lse_h = lse_ref[:, h:h + 1]
                p = jnp.exp(s - lse_h)
                delta = deltas[h]
                dsink_ref[:, h:h + 1] += -jnp.exp(sink_ref[0, h] - lse_h) * delta
                ps.append(p.astype(BF16))
                dss.append((p * (dp4[rows] - delta)).astype(BF16))
            p4, ds4 = jnp.concatenate(ps, axis=0), jnp.concatenate(dss, axis=0)
            dq4 = jnp.dot(ds4, kh, preferred_element_type=F32) * scale
            for g in range(GQA):
                cols = slice((kv * GQA + g) * HEAD_DIM, (kv * GQA + g + 1) * HEAD_DIM)
                dqkv_ref[:, cols] = dq4[g * BLOCK:(g + 1) * BLOCK].astype(BF16)
            dkv_s[:, kcols] = lax.dot_general(ds4, q_s[kv], tn, preferred_element_type=F32)
            dkv_s[:, vcols] = lax.dot_general(p4, do_s[kv], tn, preferred_element_type=F32)

        meta_s[...] += dkv_s[0:BLOCK, :]
        cur = dkv_s[2 * BLOCK:3 * BLOCK, :] + carry_s[...]
        carry_s[...] = dkv_s[BLOCK:2 * BLOCK, :]

        @pl.when(n > 0)
        def _():
            dqkv_ref[:, N_HEADS * HEAD_DIM:QKV_DIM] = cur.astype(BF16)

        @pl.when(n == 0)
        def _():
            dqkv_ref[:, N_HEADS * HEAD_DIM:QKV_DIM] = (cur + meta_s[...]).astype(BF16)

    blk = lambda b, r: (b * nb + nq(r), 0)
    return pl.pallas_call(
        body, name="attn_bwd", grid=(n_ex, nb),
        in_specs=[pl.BlockSpec(memory_space=pltpu.SMEM),
                  pl.BlockSpec((BLOCK, N_HEADS * HEAD_DIM), blk),
                  pl.BlockSpec((BLOCK, KV_DIM), lambda b, r: (b * nb, kvb)),
                  pl.BlockSpec((BLOCK, KV_DIM), lambda b, r: (b * nb + jnp.maximum(nq(r) - 1, 0), kvb)),
                  pl.BlockSpec((BLOCK, KV_DIM), lambda b, r: (b * nb + nq(r), kvb)),
                  pl.BlockSpec((BLOCK, N_HEADS * HEAD_DIM), blk),
                  pl.BlockSpec((BLOCK, N_HEADS), blk),
                  pl.BlockSpec((BLOCK, N_HEADS * HEAD_DIM), blk)],
        out_specs=[pl.BlockSpec((BLOCK, QKV_DIM), blk),
                   pl.BlockSpec((BLOCK, N_HEADS), lambda b, r: (0, 0))],
        out_shape=[SDS((n_rows, QKV_DIM), BF16), SDS((BLOCK, N_HEADS), F32)],
        scratch_shapes=[pltpu.VMEM((3 * BLOCK, N_KV * HEAD_DIM), BF16), pltpu.VMEM((3 * BLOCK, N_KV * HEAD_DIM), BF16),
                        pltpu.VMEM((3 * BLOCK, KV_DIM), F32), pltpu.VMEM((BLOCK, KV_DIM), F32),
                        pltpu.VMEM((BLOCK, KV_DIM), F32), pltpu.VMEM((N_KV, GQA * BLOCK, HEAD_DIM), BF16),
                        pltpu.VMEM((N_KV, GQA * BLOCK, HEAD_DIM), BF16), pltpu.VMEM((N_HEADS, BLOCK, 3 * BLOCK), F32)],
        compiler_params=_params(2),
    )(sinks, qkv, qkv, qkv, qkv, o, lse, do)


SSM_TILES = (2112, 1408, 384)
XW = 256 * PAIRS_PER_CHUNK


def _cmul_add(xr, xi, mr, mi, sr, si):
    return xr + mr * sr - mi * si, xi + mr * si + mi * sr


def _to_segments(src_ref, dst, seg):
    for s in range(seg):
        dst[s * 8:(s + 1) * 8, :] = src_ref[pl.ds(s, 8, stride=seg), :]


def _from_segments(src, i, seg):
    return src[pl.ds(i, seg, stride=8), :]


def _scan_segments(buf, tab_ref, carry_s, seg, reverse):
    shifts = (7, 6, 4) if reverse else (1, 2, 4)
    row_id = lax.broadcasted_iota(jnp.int32, (8, 128), 0)
    a_tiles = [tab_ref[j, c] for j in range(PAIRS_PER_CHUNK) for c in (0, 1)]

    def local(si, prev):
        s = (seg - 1 - si) if reverse else si
        row = pl.multiple_of(s * 8, 8)
        out = []
        for j in range(PAIRS_PER_CHUNK):
            re, im = slice(256 * j, 256 * j + 128), slice(256 * j + 128, 256 * j + 256)
            xr, xi = _cmul_add(buf[pl.ds(row, 8), re], buf[pl.ds(row, 8), im],
                               a_tiles[2 * j], a_tiles[2 * j + 1], prev[2 * j], prev[2 * j + 1])
            buf[pl.ds(row, 8), re] = xr
            buf[pl.ds(row, 8), im] = xi
            out += [xr, xi]
        return tuple(out)

    zero = jnp.zeros((8, 128), F32)
    edge = lax.fori_loop(0, seg, local, (zero,) * (2 * PAIRS_PER_CHUNK))

    entering = []
    for j in range(PAIRS_PER_CHUNK):
        er, ei = edge[2 * j], edge[2 * j + 1]
        if reverse:
            sr = jnp.where(row_id == 7, carry_s[2 * j], pltpu.roll(er, 7, 0))
            si_ = jnp.where(row_id == 7, carry_s[2 * j + 1], pltpu.roll(ei, 7, 0))
        else:
            sr = jnp.where(row_id == 0, carry_s[2 * j], pltpu.roll(er, 1, 0))
            si_ = jnp.where(row_id == 0, carry_s[2 * j + 1], pltpu.roll(ei, 1, 0))
        for lvl, sh in enumerate(shifts):
            sr, si_ = _cmul_add(sr, si_, tab_ref[j, 2 + 2 * lvl], tab_ref[j, 3 + 2 * lvl],
                                pltpu.roll(sr, sh, 0), pltpu.roll(si_, sh, 0))
        entering += [sr, si_]
        tr, ti = _cmul_add(er, ei, tab_ref[j, 2], tab_ref[j, 3], sr, si_)
        out_row = slice(0, 1) if reverse else slice(7, 8)
        carry_s[2 * j] = jnp.broadcast_to(tr[out_row], (8, 128))
        carry_s[2 * j + 1] = jnp.broadcast_to(ti[out_row], (8, 128))

    def fix(si, carried):
        s = (seg - 1 - si) if reverse else si
        row = pl.multiple_of(s * 8, 8)
        out = []
        for j in range(PAIRS_PER_CHUNK):
            re, im = slice(256 * j, 256 * j + 128), slice(256 * j + 128, 256 * j + 256)
            ar, ai, fr, fi = a_tiles[2 * j], a_tiles[2 * j + 1], carried[2 * j], carried[2 * j + 1]
            fr, fi = ar * fr - ai * fi, ar * fi + ai * fr
            buf[pl.ds(row, 8), re] += fr
            buf[pl.ds(row, 8), im] += fi
            out += [fr, fi]
        return tuple(out)

    lax.fori_loop(0, seg, fix, tuple(entering))


def _ssm_fwd(u, b_pad, c_pad, tab, d_skip, n_ex, lp):
    n_rows = u.shape[0]
    TM = _row_tile(lp, SSM_TILES)
    SEG = TM // 8
    n_t = lp // TM
    n_chunk = D_MODEL // 128

    def body(u_ref, bp_ref, cp_ref, tab_ref, d_ref, yg_ref, y_ref, xs_ref, up_ref, buf, carry_s, us, ys):
        @pl.when(pl.program_id(2) == 0)
        def _():
            carry_s[...] = jnp.zeros_like(carry_s)

        _to_segments(u_ref, us, SEG)
        ub = us[...]
        u16 = ub.astype(BF16)
        up_ref[...] = u16
        buf[...] = jnp.dot(u16, bp_ref[...], preferred_element_type=F32)
        _scan_segments(buf, tab_ref, carry_s, SEG, reverse=False)
        xb = buf[...].astype(BF16)
        xs_ref[...] = xb
        ys[...] = d_ref[...] * ub + jnp.dot(xb, cp_ref[...], preferred_element_type=F32)
        for i in range(8):
            yi = _from_segments(ys, i, SEG)
            y_ref[i * SEG:(i + 1) * SEG, :] = yi
            yg_ref[i * SEG:(i + 1) * SEG, :] = _gelu(yi).astype(BF16)

    rows = lambda b, q, t: (b * n_t + t, q)
    return pl.pallas_call(
        body, name="ssm_fwd", grid=(n_ex, n_chunk, n_t),
        in_specs=[pl.BlockSpec((TM, 128), rows),
                  pl.BlockSpec((None, 128, XW), lambda b, q, t: (q, 0, 0)),
                  pl.BlockSpec((None, XW, 128), lambda b, q, t: (q, 0, 0)),
                  pl.BlockSpec((PAIRS_PER_CHUNK, 8, 8, 128), lambda b, q, t: (q, 0, 0, 0)),
                  pl.BlockSpec((1, 128), lambda b, q, t: (0, q))],
        out_specs=[pl.BlockSpec((TM, 128), rows), pl.BlockSpec((TM, 128), rows),
                   pl.BlockSpec((None, TM, XW), lambda b, q, t: (q, b * n_t + t, 0)), pl.BlockSpec((TM, 128), rows)],
        out_shape=[SDS((n_rows, D_MODEL), BF16), SDS((n_rows, D_MODEL), F32), SDS((n_chunk, n_rows, XW), BF16),
                   SDS((n_rows, D_MODEL), BF16)],
        scratch_shapes=[pltpu.VMEM((TM, XW), F32), pltpu.VMEM((2 * PAIRS_PER_CHUNK, 8, 128), F32),
                        pltpu.VMEM((TM, 128), F32), pltpu.VMEM((TM, 128), F32)],
        compiler_params=_params(3),
    )(u, b_pad, c_pad, tab, d_skip)


def _ssm_bwd(dyg, y, u_seg, xs, ct_pad, bt_pad, tab_rev, d_skip, n_ex, lp):
    n_rows = u_seg.shape[0]
    TM = _row_tile(lp, SSM_TILES)
    SEG = TM // 8
    n_t = lp // TM
    n_chunk = D_MODEL // 128
    tile = lambda q, b, t: (b * n_t + (n_t - 1 - t), q)

    def body(dyg_ref, y_ref, up_ref, xs_ref, xp_ref, ct_ref, bt_ref, tab_ref, d_ref,
             du_ref, db_ref, dc_ref, da_ref, dd_ref, buf, xf, carry_s, dys, dyp):
        b, t = pl.program_id(1), pl.program_id(2)

        @pl.when((b == 0) & (t == 0))
        def _():
            db_ref[...] = jnp.zeros_like(db_ref)
            dc_ref[...] = jnp.zeros_like(dc_ref)
            da_ref[...] = jnp.zeros_like(da_ref)
            dd_ref[...] = jnp.zeros_like(dd_ref)

        @pl.when(t == 0)
        def _():
            carry_s[...] = jnp.zeros_like(carry_s)

        dys[...] = dyg_ref[...].astype(F32) * _gelu_grad(y_ref[...])
        _to_segments(dys, dyp, SEG)
        dy = dyp[...]
        u16 = up_ref[...]
        dd_ref[...] += _fold8(dy * u16.astype(F32))
        dy16 = dy.astype(BF16)
        first_tile = t == n_t - 1
        tn = (((0,), (0,)), ((), ()))
        buf[...] = jnp.dot(dy16, ct_ref[...], preferred_element_type=F32)
        dc_ref[...] += lax.dot_general(dy16, xs_ref[...], tn, preferred_element_type=F32)
        xf[16:16 + TM, :] = xs_ref[...].astype(F32)
        xf[0:16, :] = jnp.where(first_tile, 0.0, xp_ref[...].astype(F32))
        _scan_segments(buf, tab_ref, carry_s, SEG, reverse=True)
        g16 = buf[...].astype(BF16)
        dys[...] = d_ref[...] * dy + jnp.dot(g16, bt_ref[...], preferred_element_type=F32)
        db_ref[...] += lax.dot_general(u16, g16, tn, preferred_element_type=F32)
        row_id = lax.broadcasted_iota(jnp.int32, (8, 128), 0)
        for j in range(PAIRS_PER_CHUNK):
            re, im = slice(256 * j, 256 * j + 128), slice(256 * j + 128, 256 * j + 256)
            first = [jnp.where(row_id == 0, jnp.broadcast_to(xf[15:16, c], (8, 128)),
                               pltpu.roll(xf[8 + TM:16 + TM, c], 1, 0)) for c in (re, im)]
            for rows, pr, pi in ((slice(0, 8), first[0], first[1]),
                                 (slice(8, TM), xf[16:8 + TM, re], xf[16:8 + TM, im])):
                gr, gi = buf[rows, re], buf[rows, im]
                da_ref[j, 0] += _fold8(gr * pr + gi * pi)
                da_ref[j, 1] += _fold8(gi * pr - gr * pi)
        for i in range(8):
            du_ref[i * SEG:(i + 1) * SEG, :] = _from_segments(dys, i, SEG)

    prev16 = lambda q, b, t: (q, jnp.maximum((b * n_t + (n_t - 1 - t)) * (TM // 16) - 1, 0), 0)
    return pl.pallas_call(
        body, name="ssm_bwd", grid=(n_chunk, n_ex, n_t),
        in_specs=[pl.BlockSpec((TM, 128), tile), pl.BlockSpec((TM, 128), tile), pl.BlockSpec((TM, 128), tile),
                  pl.BlockSpec((None, TM, XW), lambda q, b, t: (q, b * n_t + (n_t - 1 - t), 0)),
                  pl.BlockSpec((None, 16, XW), prev16),
                  pl.BlockSpec((None, 128, XW), lambda q, b, t: (q, 0, 0)),
                  pl.BlockSpec((None, XW, 128), lambda q, b, t: (q, 0, 0)),
                  pl.BlockSpec((PAIRS_PER_CHUNK, 8, 8, 128), lambda q, b, t: (q, 0, 0, 0)),
                  pl.BlockSpec((1, 128), lambda q, b, t: (0, q))],
        out_specs=[pl.BlockSpec((TM, 128), tile),
                   pl.BlockSpec((None, 128, XW), lambda q, b, t: (q, 0, 0)),
                   pl.BlockSpec((None, 128, XW), lambda q, b, t: (q, 0, 0)),
                   pl.BlockSpec((PAIRS_PER_CHUNK, 2, 8, 128), lambda q, b, t: (q, 0, 0, 0)),
                   pl.BlockSpec((8, 128), lambda q, b, t: (0, q))],
        out_shape=[SDS((n_rows, D_MODEL), F32), SDS((n_chunk, 128, XW), F32), SDS((n_chunk, 128, XW), F32),
                   SDS((N_PAIR, 2, 8, 128), F32), SDS((8, D_MODEL), F32)],
        scratch_shapes=[pltpu.VMEM((TM, XW), F32), pltpu.VMEM((TM + 16, XW), F32),
                        pltpu.VMEM((2 * PAIRS_PER_CHUNK, 8, 128), F32), pltpu.VMEM((TM, 128), F32),
                        pltpu.VMEM((TM, 128), F32)],
        compiler_params=_params(3),
    )(dyg, y, u_seg, xs, xs, ct_pad, bt_pad, tab_rev, d_skip)


def _rms_bwd_call(dhn, h, wn, dres, name):
    n_rows = h.shape[0]

    def body(dhn_ref, h_ref, wn_ref, dres_ref, o_ref, dw_ref):
        @pl.when(pl.program_id(0) == 0)
        def _():
            dw_ref[...] = jnp.zeros_like(dw_ref)

        dh, dw_rows = _rms_bwd(dhn_ref[...], h_ref[...], wn_ref[...])
        o_ref[...] = dres_ref[...] + dh
        dw_ref[...] += _fold8(dw_rows)

    row = lambda i: (i, 0)
    return pl.pallas_call(
        body, name=name, grid=(n_rows // TM,),
        in_specs=[pl.BlockSpec((TM, D_MODEL), row), pl.BlockSpec((TM, D_MODEL), row),
                  pl.BlockSpec((1, D_MODEL), lambda i: (0, 0)), pl.BlockSpec((TM, D_MODEL), row)],
        out_specs=[pl.BlockSpec((TM, D_MODEL), row), pl.BlockSpec((8, D_MODEL), lambda i: (0, 0))],
        out_shape=[SDS((n_rows, D_MODEL), F32), SDS((8, D_MODEL), F32)], compiler_params=_params(1),
    )(dhn, h, wn, dres)


def _glu_bwd(dh, z, w4):
    n_rows = dh.shape[0]
    tm = _row_tile(n_rows, MM_TILES)
    n_sh, _, k, n = w4.shape

    def body(dh_ref, z_ref, w_ref, dz_ref, dyg_ref):
        sg = jax.nn.sigmoid(z_ref[:, D_MODEL:2 * D_MODEL].astype(F32))
        d = dh_ref[...]
        dz_ref[:, 0:D_MODEL] = (d * sg).astype(BF16)
        dz_ref[:, D_MODEL:2 * D_MODEL] = (d * z_ref[:, 0:D_MODEL].astype(F32) * sg * (1.0 - sg)).astype(BF16)
        acc = None
        for s in range(n_sh):
            part = lax.dot_general(dz_ref[:, s * n:(s + 1) * n], w_ref[s], (((1,), (1,)), ((), ())),
                                   preferred_element_type=F32)
            acc = part if acc is None else acc + part
        dyg_ref[...] = acc.astype(BF16)

    row = lambda i: (i, 0)
    return pl.pallas_call(
        body, name="glu_bwd", grid=(n_rows // tm,),
        in_specs=[pl.BlockSpec((tm, D_MODEL), row), pl.BlockSpec((tm, 2 * D_MODEL), row), _w4_spec(w4)],
        out_specs=[pl.BlockSpec((tm, 2 * D_MODEL), row), pl.BlockSpec((tm, k), row)],
        out_shape=[SDS((n_rows, 2 * D_MODEL), BF16), SDS((n_rows, k), BF16)], compiler_params=_params(1),
    )(dh, z, w4)


def _adamw(pieces, w, m, v, name):
    n_layers = len(pieces)
    rows, cols = pieces[0].shape[1:]
    rb = rows
    for cand in (256, 136, 128, 64, 32, 16, 8):
        if rows % cand == 0 and rows > cand:
            rb = cand
            break
    n_blk = rows // rb
    c1 = 1.0 / (1.0 - ADAM_B1 ** ADAM_STEP)
    c2 = 1.0 / (1.0 - ADAM_B2 ** ADAM_STEP)

    def body(*refs):
        p_refs = refs[:n_layers]
        w_ref, m_ref, v_ref, g_out, d_out, m_out, v_out = refs[n_layers:]
        layer = pl.program_id(0)
        g = None
        for l, p_ref in enumerate(p_refs):
            gl = p_ref[0].astype(F32)
            for k in range(1, N_DEV):
                gl = gl + p_ref[k].astype(F32)
            g = gl if g is None else jnp.where(layer == l, gl, g)
        m_new = ADAM_B1 * m_ref[...] + (1.0 - ADAM_B1) * g
        v_new = ADAM_B2 * v_ref[...] + (1.0 - ADAM_B2) * (g * g)
        g_out[...] = g
        m_out[...] = m_new
        v_out[...] = v_new
        d_out[...] = -ADAM_LR * ((m_new * c1) / (jnp.sqrt(v_new * c2) + ADAM_EPS) + ADAM_WD * w_ref[...])

    def piece_spec(l):
        return pl.BlockSpec((N_DEV, rb, cols), lambda ly, i: (0, jnp.where(ly == l, i, 0), 0))

    blk = pl.BlockSpec((rb, cols), lambda ly, i: (ly * n_blk + i, 0))
    return pl.pallas_call(
        body, name=name, grid=(n_layers, n_blk),
        in_specs=[piece_spec(l) for l in range(n_layers)] + [blk, blk, blk],
        out_specs=[blk, blk, blk, blk],
        out_shape=[SDS((n_layers * rows, cols), F32)] * 4, compiler_params=_params(2),
    )(*pieces, w, m, v)


_HBM = pl.BlockSpec(memory_space=pltpu.HBM)
_SEM = pl.BlockSpec(memory_space=pltpu.SEMAPHORE)
_EFFECT = pltpu.SideEffectType.DATAFLOW_SIDE_EFFECTING
N_GATHER_PEERS = N_CHIPS - 1
N_EXCHANGE_PEERS = N_DEV - 1


def _gather_copies(srcs, lands, send_sems, recv_sems):
    x, y, c = lax.axis_index("x"), lax.axis_index("y"), lax.axis_index("c")
    mine = 2 * x + y
    chips = [(1 - x, y), (x, 1 - y), (1 - x, 1 - y)]
    out, inc = [], []
    for a in range(len(srcs)):
        for k, (px, py) in enumerate(chips):
            j = a * N_GATHER_PEERS + k
            sems = dict(send_sem=send_sems.at[j], recv_sem=recv_sems.at[j], device_id=(px, py, c),
                        device_id_type=pl.DeviceIdType.MESH)
            out.append(pltpu.make_async_remote_copy(src_ref=srcs[a], dst_ref=lands[a].at[mine], **sems))
            inc.append(pltpu.make_async_remote_copy(src_ref=srcs[a], dst_ref=lands[a].at[2 * px + py], **sems))
    return out, inc


def _exchange_copies(n_scatter):
    def copies(srcs, lands, send_sems, recv_sems):
        x, y, c = lax.axis_index("x"), lax.axis_index("y"), lax.axis_index("c")
        me = 4 * x + 2 * y + c
        peers = [(x ^ (k >> 2), y ^ ((k >> 1) & 1), c ^ (k & 1)) for k in range(1, N_DEV)]
        out, inc = [], []
        for a in range(len(srcs)):
            for k, (px, py, pc) in enumerate(peers):
                j = a * N_EXCHANGE_PEERS + k
                sems = dict(send_sem=send_sems.at[j], recv_sem=recv_sems.at[j], device_id=(px, py, pc),
                            device_id_type=pl.DeviceIdType.MESH)
                theirs = srcs[a].at[2 * px + py] if a < n_scatter else srcs[a]
                mine = srcs[a].at[2 * x + y] if a < n_scatter else srcs[a]
                out.append(pltpu.make_async_remote_copy(src_ref=theirs, dst_ref=lands[a].at[me], **sems))
                inc.append(pltpu.make_async_remote_copy(src_ref=mine, dst_ref=lands[a].at[4 * px + 2 * py + pc], **sems))
        return out, inc

    return copies


def _split_start(groups, copies_fn, n_peers, name):
    sizes = [len(srcs) for srcs, _ in groups]
    flat = [a for srcs, lands in groups for a in list(srcs) + list(lands)]
    n_flat, n_grp = len(flat), len(groups)

    def body(*refs):
        sems = refs[2 * n_flat:2 * n_flat + 2 * n_grp]
        token = refs[-1]
        at = 0
        for gi, n in enumerate(sizes):
            out, _ = copies_fn(refs[at:at + n], refs[at + n:at + 2 * n], sems[2 * gi], sems[2 * gi + 1])
            for cp in out:
                cp.start()
            at += 2 * n
        token[...] = jnp.zeros_like(token)

    sem_shapes = []
    for n in sizes:
        sem_shapes += [pltpu.SemaphoreType.DMA((n * n_peers,)), pltpu.SemaphoreType.DMA((n * n_peers,))]
    res = pl.pallas_call(
        body, name=name,
        out_shape=(*[pltpu.HBM(a.shape, a.dtype) for a in flat], *sem_shapes, SDS((8, 128), F32)),
        in_specs=[_HBM] * n_flat,
        out_specs=(*[_HBM] * n_flat, *[_SEM] * (2 * n_grp), pl.BlockSpec(memory_space=pltpu.VMEM)),
        input_output_aliases={i: i for i in range(n_flat)},
        compiler_params=pltpu.CompilerParams(has_side_effects=_EFFECT),
    )(*[pltpu.with_memory_space_constraint(a, pltpu.HBM) for a in flat])
    handles, at = [], 0
    for gi, n in enumerate(sizes):
        handles.append((res[n_flat + 2 * gi], res[n_flat + 2 * gi + 1], list(res[at:at + n]), list(res[at + n:at + 2 * n])))
        at += 2 * n
    return handles, res[-1]


def _split_wait(handle, after, copies_fn, name):
    send_sems, recv_sems, srcs, lands = handle
    n = len(srcs)
    after = list(after) if isinstance(after, (list, tuple)) else [after]

    def body(*refs):
        out, inc = copies_fn(refs[:n], refs[n:2 * n], refs[2 * n], refs[2 * n + 1])
        for cp in out:
            cp.wait_send()
        for cp in inc:
            cp.wait_recv()

    flat = list(srcs) + list(lands)
    res = pl.pallas_call(
        body, name=name,
        out_shape=tuple(pltpu.HBM(a.shape, a.dtype) for a in flat),
        in_specs=[_HBM] * (2 * n) + [_SEM, _SEM] + [pl.BlockSpec(memory_space=pl.ANY)] * len(after),
        out_specs=tuple([_HBM] * (2 * n)),
        input_output_aliases={i: i for i in range(2 * n)},
        compiler_params=pltpu.CompilerParams(has_side_effects=_EFFECT),
    )(*flat, send_sems, recv_sems, *after)
    return list(res[n:])


def _landing(own, slot, n_slots):
    return lax.dynamic_update_index_in_dim(lax.empty((n_slots,) + own.shape, own.dtype), own, slot, 0)


def _ssm_discretize(lam_re, lam_im, log_dt, b_re, b_im):
    lr = jnp.minimum(lam_re, LAMBDA_RE_MAX)
    li = lam_im
    dt = jnp.exp(log_dt)[:, None]
    mag = jnp.exp(lr * dt)
    ar, ai = mag * jnp.cos(li * dt), mag * jnp.sin(li * dt)
    den = lr * lr + li * li
    nr, ni = ar - 1.0, ai
    gr, gi = (nr * lr + ni * li) / den, (ni * lr - nr * li) / den
    bbr = gr[:, :, None] * b_re - gi[:, :, None] * b_im
    bbi = gr[:, :, None] * b_im + gi[:, :, None] * b_re
    return ar, ai, bbr, bbi


def _pair_lanes(t):
    return t.reshape(N_PAIR, 2 * SSM_STATE)


def _chan_state_blocks(t_gcp):
    t = t_gcp.reshape(N_PAIR, 2, SSM_GROUP, SSM_STATE)
    eye2 = jnp.eye(2, dtype=t.dtype)
    blk = jnp.einsum("rgcp,gh->rgchp", t, eye2).reshape(N_PAIR, 2 * SSM_GROUP, 2 * SSM_STATE)
    place = jax.nn.one_hot(jnp.arange(N_PAIR) % PAIRS_PER_CHUNK, PAIRS_PER_CHUNK, dtype=t.dtype)
    return jnp.einsum("rcl,rj->rjcl", blk, place).reshape(N_PAIR, 128, 2 * SSM_STATE)


def _chan_state_unblock(t):
    t = t.reshape(N_PAIR, PAIRS_PER_CHUNK, 2, SSM_GROUP, 2, SSM_STATE)
    place = jax.nn.one_hot(jnp.arange(N_PAIR) % PAIRS_PER_CHUNK, PAIRS_PER_CHUNK, dtype=t.dtype)
    t = jnp.einsum("rjgchp,rj->rgchp", t, place)
    t = jnp.einsum("rgchp,gh->rgcp", t, jnp.eye(2, dtype=t.dtype))
    return t.reshape(SSM_NG, SSM_GROUP, SSM_STATE)


def _scan_tables(zr, zi, reverse, seg):
    zr, zi = _pair_lanes(zr), _pair_lanes(-zi if reverse else zi)
    a = (jnp.exp(zr) * jnp.cos(zi), jnp.exp(zr) * jnp.sin(zi))
    cmul = lambda p, q: (p[0] * q[0] - p[1] * q[1], p[0] * q[1] + p[1] * q[0])
    big, square, bits = None, a, seg
    while bits:
        if bits & 1:
            big = square if big is None else cmul(big, square)
        square, bits = cmul(square, square), bits >> 1
    powers = [a, big]
    for _ in range(2):
        powers.append(cmul(powers[-1], powers[-1]))
    rows = jnp.arange(8)[None, :, None]
    tiles = [jnp.broadcast_to(part[:, None, :], (N_PAIR, 8, 128)) for part in powers[0]]
    for lvl, step in enumerate((1, 2, 4)):
        keep = (rows <= 7 - step) if reverse else (rows >= step)
        for part in powers[1 + lvl]:
            tiles.append(jnp.where(keep, part[:, None, :], 0.0))
    return jnp.stack(tiles, axis=1)


def _pairs_to_chunks(t):
    n_chunk = N_PAIR // PAIRS_PER_CHUNK
    return jnp.swapaxes(t.reshape(n_chunk, PAIRS_PER_CHUNK, 128, 256), 1, 2).reshape(n_chunk, 128, XW)


def _chunks_to_pairs(t):
    n_chunk = N_PAIR // PAIRS_PER_CHUNK
    return jnp.swapaxes(t.reshape(n_chunk, 128, PAIRS_PER_CHUNK, 256), 1, 2).reshape(N_PAIR, 128, 256)


def _ssm_operands(w, lp):
    seg = _row_tile(lp, SSM_TILES) // 8
    ar, ai, bbr, bbi = _ssm_discretize(w["ssm_lambda_re"], w["ssm_lambda_im"], w["ssm_log_dt"], w["ssm_b_re"], w["ssm_b_im"])
    b_blk = jnp.concatenate([_chan_state_blocks(jnp.swapaxes(bbr, 1, 2)), _chan_state_blocks(jnp.swapaxes(bbi, 1, 2))], axis=2)
    c_blk = jnp.concatenate([_chan_state_blocks(w["ssm_c_re"]), -_chan_state_blocks(w["ssm_c_im"])], axis=2)
    dt = jnp.exp(w["ssm_log_dt"])[:, None]
    zr, zi = jnp.minimum(w["ssm_lambda_re"], LAMBDA_RE_MAX) * dt, w["ssm_lambda_im"] * dt
    b_cat, c_cat = _pairs_to_chunks(b_blk).astype(BF16), _pairs_to_chunks(c_blk).astype(BF16)
    return (b_cat, jnp.swapaxes(b_cat, 1, 2), c_cat, jnp.swapaxes(c_cat, 1, 2),
            _scan_tables(zr, zi, False, seg), _scan_tables(zr, zi, True, seg))


def _local_step(x, target, w, late_weights, on_grads):
    n_ex, seq, _ = x.shape
    lp = seq + BLOCK
    nb = lp // BLOCK
    n_rows = n_ex * lp
    g = {}

    head = jnp.concatenate([jnp.zeros((PAD, D_MODEL), F32), w["meta_tokens"]], axis=0)
    h0 = jnp.concatenate([jnp.broadcast_to(head[None], (n_ex, BLOCK, D_MODEL)), x], axis=1).reshape(n_rows, D_MODEL)

    qkv, hn_a = _rms_mm_cols(h0, w["attn_norm_w"], w["attn_w_qkv"], "qkv_fwd")
    att, lse = _attn_fwd(qkv, w["attn_sinks"], n_ex, nb)
    h1 = _mm_acc(att, w["attn_w_o"], False, "attn_out_fwd", res=h0)
    w = {**w, **late_weights(0, att)}
    h2, a0, hn_m0, u = _mlp_fwd(h1, w["mlp_norm_w"][0:1], w["mlp_w_up"][0], w["mlp_w_down"][0], "mlp0_fwd",
                                next_norm=w["ssm_norm_w"])
    late = late_weights(1, h2)
    w["ssm_w_glu"] = late["ssm_w_glu"]
    w["mlp_w_up"], w["mlp_w_down"] = w["mlp_w_up"] + late["mlp_w_up"], w["mlp_w_down"] + late["mlp_w_down"]

    ops = w["ssm_operands"] if "ssm_operands" in w else _ssm_operands(w, lp)
    b_pad, bt_pad, ct_pad, c_pad, tab_fwd, tab_rev = ops
    yg, y, xs, u_seg = _ssm_fwd(u, b_pad, c_pad, tab_fwd, w["ssm_d"], n_ex, lp)
    z = _mm_cols(yg, w["ssm_w_glu"], False, "glu_mm_fwd")
    dh4, a1, hn_m1, h3, loss_tile, dnorm_f = _mlp_fwd(
        h2, w["mlp_norm_w"][1:2], w["mlp_w_up"][1], w["mlp_w_down"][1], "mlp1_fwd", glu_z=z,
        loss=(w["final_norm_w"], target.reshape(n_ex * seq, D_MODEL), lp))

    def mlp_bwd(dh_out, h_in, a, hn, layer, tag, norm_w):
        dhn, dw_up, dw_down = None, None, None
        for s in range(N_CHIPS):
            final = s == N_CHIPS - 1
            res = _mlp_bwd_shard(s, dh_out, a, hn, dhn, h_in if final else None, norm_w,
                                 w["mlp_w_up"][layer], w["mlp_w_down"][layer], dw_up, dw_down, f"{tag}_bwd{s}")
            dhn, dw_up, dw_down = res[:3]
        return dhn, res[3], dw_up, dw_down

    dh3, dnorm_m1, dwu1, dwd1 = mlp_bwd(dh4, h3, a1, hn_m1, 1, "mlp1", w["mlp_norm_w"][1:2])
    tok = on_grads("mlp1", {"mlp_w_up": dwu1, "mlp_w_down": dwd1})
    dz, dyg = _glu_bwd(dh3, z, w["ssm_w_glu"])
    g["ssm_w_glu"] = _mm_tn(yg, dz, N_CHIPS, False, "glu_mm_dw")
    du, db_blk, dc_blk, da_t, dd_t = _ssm_bwd(dyg, y, u_seg, xs, ct_pad, bt_pad, tab_rev, w["ssm_d"] + tok, n_ex, lp)
    dh2, dnorm_s = _rms_bwd_call(du, h2, w["ssm_norm_w"], dh3, "ssm_norm_bwd")
    db_blk, dc_blk = _chunks_to_pairs(db_blk), _chunks_to_pairs(dc_blk)
    g["ssm_c_re"] = _chan_state_unblock(dc_blk[:, :, 0:128])
    g["ssm_c_im"] = -_chan_state_unblock(dc_blk[:, :, 128:256])
    g_bbr = jnp.swapaxes(_chan_state_unblock(db_blk[:, :, 0:128]), 1, 2)
    g_bbi = jnp.swapaxes(_chan_state_unblock(db_blk[:, :, 128:256]), 1, 2)
    g_a = jnp.sum(da_t, axis=2).reshape(N_PAIR, 2, 2, SSM_STATE)
    g_ar, g_ai = g_a[:, 0].reshape(SSM_NG, SSM_STATE), g_a[:, 1].reshape(SSM_NG, SSM_STATE)
    _, vjp = jax.vjp(_ssm_discretize, w["ssm_lambda_re"], w["ssm_lambda_im"], w["ssm_log_dt"], w["ssm_b_re"], w["ssm_b_im"])
    g["ssm_lambda_re"], g["ssm_lambda_im"], g["ssm_log_dt"], g["ssm_b_re"], g["ssm_b_im"] = vjp((g_ar, g_ai, g_bbr, g_bbi))
    tok = on_grads("ssm", g)
    g = {}
    dh1, dnorm_m0, dwu0, dwd0 = mlp_bwd(dh2, h1, a0, hn_m0, 0, "mlp0", w["mlp_norm_w"][0:1] + tok)
    datt = _mm_cols(dh1, w["attn_w_o"], True, "attn_out_dx")
    dw_o = _mm_tn(att, dh1, N_CHIPS, True, "attn_out_dw")
    tok = on_grads("mlp0", {"mlp_w_up": dwu0, "mlp_w_down": dwd0, "attn_w_o": dw_o})
    dqkv, dsink_rows = _attn_bwd(qkv, w["attn_sinks"] + tok, att, lse, datt, n_ex, nb)
    tok = on_grads("qkv", {"attn_w_qkv": _mm_tn(hn_a, dqkv, N_CHIPS, False, "qkv_dw")})
    dh0, dnorm_a = _mm_acc(dqkv, w["attn_w_qkv"], True, "qkv_dx", rms_bwd=(h0, w["attn_norm_w"] + tok, dh1))

    dh0 = dh0.reshape(n_ex, lp, D_MODEL)
    on_grads("rest", {
        "mlp_norm_w": jnp.stack([jnp.sum(dnorm_m0, axis=0), jnp.sum(dnorm_m1, axis=0)]),
        "final_norm_w": jnp.sum(dnorm_f, axis=0),
        "attn_norm_w": jnp.sum(dnorm_a, axis=0)[None],
        "ssm_norm_w": jnp.sum(dnorm_s, axis=0)[None],
        "attn_sinks": jnp.sum(dsink_rows, axis=0)[None],
        "ssm_d": jnp.sum(dd_t, axis=0)[None],
        "meta_tokens": jnp.sum(dh0[:, PAD:BLOCK], axis=0),
        "loss": loss_tile[0, 0:1]})
    return loss_tile, dh0[:, BLOCK:]


_SHARDED_SMALL = ("meta_tokens", "ssm_norm_w", "ssm_d")
_REP_SSM = ("ssm_lambda_re", "ssm_lambda_im", "ssm_log_dt", "ssm_b_re", "ssm_b_im", "ssm_c_re", "ssm_c_im")
_REP_MISC = ("attn_norm_w", "attn_sinks", "mlp_norm_w", "final_norm_w")
_BIG = ("attn_w_qkv", "attn_w_o", "ssm_w_glu", "mlp_w_up", "mlp_w_down")


def _pack(parts, cols):
    flat = jnp.concatenate([p.reshape(-1) for p in parts])
    rows = -(-flat.shape[0] // (8 * cols)) * 8
    return jnp.pad(flat, (0, rows * cols - flat.shape[0])).reshape(rows, cols)


def _unpack(packed, like):
    flat = packed.reshape(-1)
    out, at = [], 0
    for p in like:
        out.append(flat[at:at + p.size].reshape(p.shape))
        at += p.size
    return out


def kernel(x, meta_tokens, attn_norm_w, attn_w_qkv, attn_sinks, attn_w_o, ssm_norm_w, ssm_lambda_re, ssm_lambda_im, ssm_log_dt, ssm_b_re, ssm_b_im, ssm_c_re, ssm_c_im, ssm_d, ssm_w_glu, mlp_norm_w, mlp_w_up, mlp_w_down, final_norm_w, loss_target, m_meta_tokens, m_attn_norm_w, m_attn_w_qkv, m_attn_sinks, m_attn_w_o, m_ssm_norm_w, m_ssm_lambda_re, m_ssm_lambda_im, m_ssm_log_dt, m_ssm_b_re, m_ssm_b_im, m_ssm_c_re, m_ssm_c_im, m_ssm_d, m_ssm_w_glu, m_mlp_norm_w, m_mlp_w_up, m_mlp_w_down, m_final_norm_w, v_meta_tokens, v_attn_norm_w, v_attn_w_qkv, v_attn_sinks, v_attn_w_o, v_ssm_norm_w, v_ssm_lambda_re, v_ssm_lambda_im, v_ssm_log_dt, v_ssm_b_re, v_ssm_b_im, v_ssm_c_re, v_ssm_c_im, v_ssm_d, v_ssm_w_glu, v_mlp_norm_w, v_mlp_w_up, v_mlp_w_down, v_final_norm_w):
    names = ("meta_tokens", "attn_norm_w", "attn_w_qkv", "attn_sinks", "attn_w_o", "ssm_norm_w", "ssm_lambda_re",
             "ssm_lambda_im", "ssm_log_dt", "ssm_b_re", "ssm_b_im", "ssm_c_re", "ssm_c_im", "ssm_d", "ssm_w_glu",
             "mlp_norm_w", "mlp_w_up", "mlp_w_down", "final_norm_w")
    wts = dict(zip(names, (meta_tokens, attn_norm_w, attn_w_qkv, attn_sinks, attn_w_o, ssm_norm_w, ssm_lambda_re,
                           ssm_lambda_im, ssm_log_dt, ssm_b_re, ssm_b_im, ssm_c_re, ssm_c_im, ssm_d, ssm_w_glu,
                           mlp_norm_w, mlp_w_up, mlp_w_down, final_norm_w)))
    mom = dict(zip(names, (m_meta_tokens, m_attn_norm_w, m_attn_w_qkv, m_attn_sinks, m_attn_w_o, m_ssm_norm_w,
                           m_ssm_lambda_re, m_ssm_lambda_im, m_ssm_log_dt, m_ssm_b_re, m_ssm_b_im, m_ssm_c_re,
                           m_ssm_c_im, m_ssm_d, m_ssm_w_glu, m_mlp_norm_w, m_mlp_w_up, m_mlp_w_down, m_final_norm_w)))
    var = dict(zip(names, (v_meta_tokens, v_attn_norm_w, v_attn_w_qkv, v_attn_sinks, v_attn_w_o, v_ssm_norm_w,
                           v_ssm_lambda_re, v_ssm_lambda_im, v_ssm_log_dt, v_ssm_b_re, v_ssm_b_im, v_ssm_c_re,
                           v_ssm_c_im, v_ssm_d, v_ssm_w_glu, v_mlp_norm_w, v_mlp_w_up, v_mlp_w_down, v_final_norm_w)))

    my_chip = 2 * lax.axis_index("x") + lax.axis_index("y")
    my_dev = 2 * my_chip + lax.axis_index("c")
    small_mine = _pack([wts[n] for n in _SHARDED_SMALL], 128)
    first = [attn_w_qkv.astype(BF16), attn_w_o.astype(BF16), small_mine]
    with_landing = lambda srcs: (srcs, [_landing(a, my_chip, N_CHIPS) for a in srcs])
    handles, token = _split_start([with_landing(first)], _gather_copies, N_GATHER_PEERS, "gather_start_first")
    up16, down16 = (mlp_w_up + token[0, 0]).astype(BF16), (mlp_w_down + token[0, 0]).astype(BF16)
    mlp0 = [up16[0:1], down16[0:1]]
    rest = [(ssm_w_glu + token[0, 0]).astype(BF16), up16[1:2], down16[1:2]]
    later, _ = _split_start([with_landing(mlp0), with_landing(rest)], _gather_copies, N_GATHER_PEERS, "gather_start_later")
    handles = handles + later
    full = {n: wts[n] for n in _REP_MISC}
    full["final_norm_w"] = final_norm_w[None]
    for n in _REP_SSM:
        full[n] = wts[n][0]
    full["ssm_operands"] = _ssm_operands(full, x.shape[1] + BLOCK)
    got = _split_wait(handles[0], full["ssm_operands"], _gather_copies, "gather_wait_first")
    full["attn_w_qkv"], full["attn_w_o"] = got[0], got[1]
    smalls = [_unpack(got[2][s], [wts[n] for n in _SHARDED_SMALL]) for s in range(N_CHIPS)]
    for k, n in enumerate(_SHARDED_SMALL):
        full[n] = jnp.concatenate([smalls[s][k] for s in range(N_CHIPS)], axis=1)

    def late_weights(stage, after):
        if stage == 0:
            up, down = _split_wait(handles[1], after, _gather_copies, "gather_wait_mlp0")
            return {"mlp_w_up": [up], "mlp_w_down": [down]}
        glu, up, down = _split_wait(handles[2], after, _gather_copies, "gather_wait_rest")
        return {"ssm_w_glu": glu, "mlp_w_up": [up], "mlp_w_down": [down]}

    def shard_cols(t):
        return jnp.swapaxes(t.reshape(t.shape[0], N_CHIPS, t.shape[1] // N_CHIPS), 0, 1)

    pending = {}

    def on_grads(tag, g):
        scatter = [g[n] for n in _BIG if n in g]
        whole = []
        if tag == "ssm":
            whole = [_pack([g[n] for n in _REP_SSM], D_MODEL)]
        if tag == "rest":
            parts = [shard_cols(g[n]) for n in _SHARDED_SMALL]
            scatter = [jnp.stack([_pack([p[s] for p in parts], 128) for s in range(N_CHIPS)])]
            whole = [_pack([g[n] for n in _REP_MISC] + [g["loss"]], D_MODEL)]
        srcs = scatter + whole
        lands = [_landing(lax.dynamic_index_in_dim(a, my_chip, 0, keepdims=False), my_dev, N_DEV) for a in scatter]
        lands += [_landing(a, my_dev, N_DEV) for a in whole]
        hs, token = _split_start([(srcs, lands)], _exchange_copies(len(scatter)), N_EXCHANGE_PEERS, "exchange_start_" + tag)
        pending[tag] = (hs[0], len(scatter))
        return token[0, 0]

    _, grad_x = _local_step(x, loss_target, full, late_weights, on_grads)

    recv = {}

    def wait_for(tag, after):
        handle, n_scatter = pending[tag]
        recv[tag] = _split_wait(handle, after, _exchange_copies(n_scatter), "exchange_wait_" + tag)

    for tag in ("mlp1", "ssm", "mlp0"):
        wait_for(tag, grad_x)

    out = {}

    def update(tag, pieces, w2, m2, v2):
        return _adamw(pieces, w2, m2, v2, "adamw_" + tag)

    def update_weight(n, pieces):
        shp = wts[n].shape
        r2 = (math.prod(shp[:-1]), shp[-1])
        res = update(n, pieces, wts[n].reshape(r2), mom[n].reshape(r2), var[n].reshape(r2))
        out[n] = [t.reshape(shp) for t in res]

    update_weight("mlp_w_up", [recv["mlp0"][1], recv["mlp1"][0]])
    update_weight("mlp_w_down", [recv["mlp0"][2], recv["mlp1"][1]])
    update_weight("attn_w_o", [recv["mlp0"][0]])
    update_weight("ssm_w_glu", [recv["ssm"][0]])
    for tag in ("qkv", "rest"):
        wait_for(tag, [out[n][1] for n in ("mlp_w_up", "mlp_w_down", "attn_w_o", "ssm_w_glu")])
    loss = jnp.sum(recv["rest"][1].reshape(N_DEV, -1)[:, sum(wts[n].size for n in _REP_MISC)])
    update_weight("attn_w_qkv", [recv["qkv"][0]])
    for tag, group, pieces, cols in (("small", _SHARDED_SMALL, recv["rest"][0], 128),
                                     ("rep_ssm", _REP_SSM, recv["ssm"][1], D_MODEL),
                                     ("rep_misc", _REP_MISC, recv["rest"][1], D_MODEL)):
        like = [wts[n] for n in group]
        res = update(tag, [pieces], _pack(like, cols), _pack([mom[n] for n in group], cols),
                     _pack([var[n] for n in group], cols))
        for k, n in enumerate(group):
            out[n] = [_unpack(t, like)[k] for t in res]

    return (loss, grad_x, *[out[n][0] for n in names], *[out[n][1] for n in names],
            *[out[n][2] for n in names], *[out[n][3] for n in names])
```

```python
import functools
import math

import jax
import jax.numpy as jnp
from jax import lax
from jax.experimental import pallas as pl
from jax.experimental.pallas import tpu as pltpu

F32 = jnp.float32
BF16 = jnp.bfloat16
SDS = jax.ShapeDtypeStruct

D_MODEL = 1024
N_HEADS = 16
N_KV = 4
GQA = N_HEADS // N_KV
HEAD_DIM = 64
BLOCK = 128
N_META = 16
PAD = BLOCK - N_META
QKV_DIM = (N_HEADS + 2 * N_KV) * HEAD_DIM
KV_DIM = 2 * N_KV * HEAD_DIM
D_FF = 4 * D_MODEL
N_CHIPS = 4
N_DEV = 8
SSM_GROUP = 16
SSM_NG = D_MODEL // SSM_GROUP
SSM_STATE = 64
N_PAIR = SSM_NG // 2
PAIRS_PER_CHUNK = 4
RMS_EPS = 1e-6
NEG_INF = -1e30
LAMBDA_RE_MAX = -1e-4
ADAM_LR, ADAM_B1, ADAM_B2, ADAM_EPS, ADAM_WD, ADAM_STEP = 0.001, 0.9, 0.999, 1e-08, 0.01, 10

TM = 768
MM_TILES = (1056, 768, 384)
MLP_FWD_TILES = (384,)
MLP_BWD_TILES = (768, 384)
TN_TILES = (1408, 768, 384)
VMEM_LIMIT = 56 * 1024 * 1024


def _params(n_grid):
    return pltpu.CompilerParams(dimension_semantics=("arbitrary",) * n_grid, vmem_limit_bytes=VMEM_LIMIT)


def _row_tile(n_rows, tiles):
    return next(t for t in tiles if n_rows % t == 0)


def _rms(h, w):
    r = lax.rsqrt(jnp.mean(h * h, axis=-1, keepdims=True) + RMS_EPS)
    return h * r * w


def _rms_bwd(dhn, h, w):
    r = lax.rsqrt(jnp.mean(h * h, axis=-1, keepdims=True) + RMS_EPS)
    g = dhn * w
    proj = jnp.sum(g * h, axis=-1, keepdims=True) * (1.0 / D_MODEL)
    return r * g - h * (r * r * r) * proj, dhn * h * r


def _fold8(t):
    return jnp.sum(t.reshape(t.shape[0] // 8, 8, t.shape[1]), axis=0)


def _gelu(y):
    return 0.5 * y * (1.0 + jnp.tanh(0.7978845608028654 * (y + 0.044715 * y * y * y)))


def _gelu_grad(y):
    t = jnp.tanh(0.7978845608028654 * (y + 0.044715 * y * y * y))
    return 0.5 * (1.0 + t) + 0.5 * y * (1.0 - t * t) * 0.7978845608028654 * (1.0 + 3.0 * 0.044715 * y * y)


def _w4_spec(w4):
    n_sh, _, k, n = w4.shape
    return pl.BlockSpec((n_sh, None, k, n), lambda i: (0, 0, 0, 0))


def _rms_mm_cols(h, wn, w4, name):
    n_rows = h.shape[0]
    n_sh, _, k, n = w4.shape
    tm = _row_tile(n_rows, MM_TILES)

    def body(h_ref, wn_ref, w_ref, o_ref, hn_ref):
        hn = _rms(h_ref[...], wn_ref[...]).astype(BF16)
        hn_ref[...] = hn
        for s in range(n_sh):
            o_ref[:, s * n:(s + 1) * n] = jnp.dot(hn, w_ref[s], preferred_element_type=F32).astype(o_ref.dtype)

    return pl.pallas_call(
        body, name=name, grid=(n_rows // tm,),
        in_specs=[pl.BlockSpec((tm, k), lambda i: (i, 0)), pl.BlockSpec((1, k), lambda i: (0, 0)), _w4_spec(w4)],
        out_specs=[pl.BlockSpec((tm, n_sh * n), lambda i: (i, 0)), pl.BlockSpec((tm, k), lambda i: (i, 0))],
        out_shape=[SDS((n_rows, n_sh * n), BF16), SDS((n_rows, k), BF16)],
        compiler_params=_params(1),
    )(h, wn, w4)


def _mm_cols(x, w4, trans_w, name):
    n_rows, kx = x.shape
    tm = _row_tile(n_rows, MM_TILES)
    n_sh, _, k, n = w4.shape
    n_out = k if trans_w else n
    dims = (((1,), (1,)), ((), ())) if trans_w else (((1,), (0,)), ((), ()))

    def body(x_ref, w_ref, o_ref):
        x16 = x_ref[...].astype(BF16)
        for s in range(n_sh):
            o_ref[:, s * n_out:(s + 1) * n_out] = lax.dot_general(
                x16, w_ref[s], dims, preferred_element_type=F32).astype(o_ref.dtype)

    return pl.pallas_call(
        body, name=name, grid=(n_rows // tm,),
        in_specs=[pl.BlockSpec((tm, kx), lambda i: (i, 0)), _w4_spec(w4)],
        out_specs=pl.BlockSpec((tm, n_sh * n_out), lambda i: (i, 0)),
        out_shape=SDS((n_rows, n_sh * n_out), BF16),
        compiler_params=_params(1),
    )(x, w4)


def _mm_acc(x, w4, trans_w, name, res=None, rms_bwd=None):
    n_rows = x.shape[0]
    tm = _row_tile(n_rows, MM_TILES)
    n_sh, _, k, n = w4.shape
    kx, n_out = (n, k) if trans_w else (k, n)
    dims = (((1,), (1,)), ((), ())) if trans_w else (((1,), (0,)), ((), ()))

    def body(*refs):
        if rms_bwd is not None:
            x_ref, w_ref, h_ref, wn_ref, dres_ref, o_ref, dw_ref = refs
        elif res is not None:
            x_ref, w_ref, res_ref, o_ref = refs
        else:
            x_ref, w_ref, o_ref = refs
        acc = None
        for s in range(n_sh):
            part = lax.dot_general(x_ref[:, s * kx:(s + 1) * kx].astype(BF16), w_ref[s], dims, preferred_element_type=F32)
            acc = part if acc is None else acc + part
        if rms_bwd is not None:
            dh, dw_rows = _rms_bwd(acc, h_ref[...], wn_ref[...])
            o_ref[...] = (dres_ref[...] + dh).astype(o_ref.dtype)

            @pl.when(pl.program_id(0) == 0)
            def _():
                dw_ref[...] = jnp.zeros_like(dw_ref)

            dw_ref[...] += _fold8(dw_rows)
        elif res is not None:
            o_ref[...] = (res_ref[...] + acc).astype(o_ref.dtype)
        else:
            o_ref[...] = acc.astype(o_ref.dtype)

    row = lambda i: (i, 0)
    in_specs = [pl.BlockSpec((tm, n_sh * kx), row), _w4_spec(w4)]
    args = [x, w4]
    out_specs = pl.BlockSpec((tm, n_out), row)
    out_shape = SDS((n_rows, n_out), F32)
    if rms_bwd is not None:
        h, wn, dres = rms_bwd
        in_specs += [pl.BlockSpec((tm, n_out), row), pl.BlockSpec((1, n_out), lambda i: (0, 0)),
                     pl.BlockSpec((tm, n_out), row)]
        args += [h, wn, dres]
        out_specs = [out_specs, pl.BlockSpec((8, n_out), lambda i: (0, 0))]
        out_shape = [out_shape, SDS((8, n_out), F32)]
    elif res is not None:
        in_specs.append(pl.BlockSpec((tm, n_out), row))
        args.append(res)
    return pl.pallas_call(
        body, name=name, grid=(n_rows // tm,), in_specs=in_specs, out_specs=out_specs, out_shape=out_shape,
        compiler_params=_params(1),
    )(*args)


def _mm_tn(a, b, n_sh, a_sharded, name):
    n_rows = a.shape[0]
    tm = _row_tile(n_rows, TN_TILES)
    ka = a.shape[1] // n_sh if a_sharded else a.shape[1]
    nb = b.shape[1] if a_sharded else b.shape[1] // n_sh
    n_i = n_rows // tm

    def body(a_ref, b_ref, o_ref, acc):
        i = pl.program_id(0)

        @pl.when(i == 0)
        def _():
            acc[...] = jnp.zeros_like(acc)

        for s in range(n_sh):
            a_s = a_ref[:, s * ka:(s + 1) * ka] if a_sharded else a_ref[...]
            b_s = b_ref[...] if a_sharded else b_ref[:, s * nb:(s + 1) * nb]
            acc[s] += lax.dot_general(a_s.astype(BF16), b_s.astype(BF16), (((0,), (0,)), ((), ())),
                                      preferred_element_type=F32)

        @pl.when(i == n_i - 1)
        def _():
            o_ref[...] = acc[...].astype(o_ref.dtype)

    return pl.pallas_call(
        body, name=name, grid=(n_i,),
        in_specs=[pl.BlockSpec((tm, a.shape[1]), lambda i: (i, 0)), pl.BlockSpec((tm, b.shape[1]), lambda i: (i, 0))],
        out_specs=pl.BlockSpec((n_sh, ka, nb), lambda i: (0, 0, 0)),
        out_shape=SDS((n_sh, ka, nb), BF16),
        scratch_shapes=[pltpu.VMEM((n_sh, ka, nb), F32)], compiler_params=_params(1),
    )(a, b)


def _mlp_fwd(h, wn, w_up4, w_down4, name, next_norm=None, glu_z=None, loss=None):
    n_rows = h.shape[0]
    tm = _row_tile(n_rows, MLP_FWD_TILES)
    n_sh = w_up4.shape[0]
    f_sh = D_FF // n_sh
    w_down = w_down4.reshape(D_FF, D_MODEL)
    per_tile = tm // BLOCK

    def body(*refs):
        refs = list(refs)
        h_ref, wn_ref, wu_ref, wd_ref = refs[:4]
        at = 4
        if next_norm is not None:
            nn_ref = refs[at]
            at += 1
        if glu_z is not None:
            z_ref = refs[at]
            at += 1
        if loss is not None:
            fw_ref, t_refs = refs[at], refs[at + 1:at + 1 + per_tile]
            at += 1 + per_tile
        o_ref, a_ref, hn_ref = refs[at:at + 3]
        at += 3
        if next_norm is not None:
            u_ref = refs[at]
            at += 1
        if glu_z is not None:
            hin_ref = refs[at]
            at += 1
        if loss is not None:
            loss_ref, dfw_ref = refs[at:at + 2]
            at += 2
        act_s = refs[at]
        h_in = h_ref[...]
        if glu_z is not None:
            h_in = h_in + z_ref[:, 0:D_MODEL].astype(F32) * jax.nn.sigmoid(z_ref[:, D_MODEL:2 * D_MODEL].astype(F32))
            hin_ref[...] = h_in
        hn = _rms(h_in, wn_ref[...]).astype(BF16)
        hn_ref[...] = hn
        for s in range(n_sh):
            cols = slice(s * f_sh, (s + 1) * f_sh)
            a = jnp.dot(hn, wu_ref[s], preferred_element_type=F32)
            a_ref[:, cols] = a.astype(BF16)
            act = jnp.maximum(a, 0.0)
            act_s[:, cols] = (act * act).astype(BF16)
        out = h_in + jnp.dot(act_s[...], wd_ref[...], preferred_element_type=F32)
        if next_norm is not None:
            u_ref[...] = _rms(out, nn_ref[...])
        if loss is None:
            o_ref[...] = out
        else:
            i = pl.program_id(0)
            first_of_example = i % tiles_per_example == 0

            @pl.when(i == 0)
            def _():
                loss_ref[...] = jnp.zeros_like(loss_ref)
                dfw_ref[...] = jnp.zeros_like(dfw_ref)

            def block(k):
                rows = slice(k * BLOCK, (k + 1) * BLOCK)
                diff = _rms(out[rows], fw_ref[...]) - t_refs[k][...]
                loss_ref[...] += 0.5 * jnp.sum(diff * diff) * (1.0 / D_MODEL)
                dh, dw_rows = _rms_bwd(diff * (1.0 / D_MODEL), out[rows], fw_ref[...])
                o_ref[rows, :] = dh
                dfw_ref[...] += _fold8(dw_rows)

            @pl.when(first_of_example)
            def _():
                o_ref[0:BLOCK, :] = jnp.zeros((BLOCK, D_MODEL), F32)

            pl.when(jnp.logical_not(first_of_example))(lambda: block(0))
            for k in range(1, per_tile):
                block(k)

    row = lambda i: (i, 0)
    vec = pl.BlockSpec((1, D_MODEL), lambda i: (0, 0))
    in_specs = [pl.BlockSpec((tm, D_MODEL), row), vec,
                pl.BlockSpec((n_sh, None, D_MODEL, f_sh), lambda i: (0, 0, 0, 0), pipeline_mode=pl.Buffered(1)),
                pl.BlockSpec((D_FF, D_MODEL), lambda i: (0, 0), pipeline_mode=pl.Buffered(1))]
    out_specs = [pl.BlockSpec((tm, D_MODEL), row), pl.BlockSpec((tm, D_FF), row), pl.BlockSpec((tm, D_MODEL), row)]
    out_shape = [SDS((n_rows, D_MODEL), F32), SDS((n_rows, D_FF), BF16), SDS((n_rows, D_MODEL), BF16)]
    args = [h, wn, w_up4, w_down]
    if next_norm is not None:
        in_specs.append(vec)
        args.append(next_norm)
    if glu_z is not None:
        in_specs.append(pl.BlockSpec((tm, 2 * D_MODEL), row))
        args.append(glu_z)
    for extra in (next_norm, glu_z):
        if extra is not None:
            out_specs.append(pl.BlockSpec((tm, D_MODEL), row))
            out_shape.append(SDS((n_rows, D_MODEL), F32))
    if loss is not None:
        final_wn, target, lp = loss
        tiles_per_example = lp // tm
        real_blocks = lp // BLOCK - 1

        def t_spec(k):
            return pl.BlockSpec((BLOCK, D_MODEL), lambda i: (
                (i // tiles_per_example) * real_blocks + jnp.maximum(per_tile * (i % tiles_per_example) + k - 1, 0), 0))

        in_specs += [vec] + [t_spec(k) for k in range(per_tile)]
        args += [final_wn] + [target] * per_tile
        out_specs += [pl.BlockSpec((8, 128), lambda i: (0, 0)), pl.BlockSpec((8, D_MODEL), lambda i: (0, 0))]
        out_shape += [SDS((8, 128), F32), SDS((8, D_MODEL), F32)]
    return pl.pallas_call(
        body, name=name, grid=(n_rows // tm,), in_specs=in_specs, out_specs=out_specs, out_shape=out_shape,
        scratch_shapes=[pltpu.VMEM((tm, D_FF), BF16)],
        compiler_params=_params(1),
    )(*args)


def _mlp_bwd_shard(s, dh, a, hn, dhn_prev, h, wn, w_up4, w_down4, dw_up_buf, dw_down_buf, name):
    n_rows = dh.shape[0]
    n_sh = w_up4.shape[0]
    f_sh = D_FF // n_sh
    tm = _row_tile(n_rows, MLP_BWD_TILES)
    n_i = n_rows // tm
    last = h is not None
    nt = (((1,), (1,)), ((), ()))
    tn = (((0,), (0,)), ((), ()))

    def body(*refs):
        refs = list(refs)
        dh_ref, a_ref, hn_ref, wu_ref, wd_ref = refs[:5]
        at = 5
        prev_ref = None
        if dhn_prev is not None:
            prev_ref = refs[at]
            at += 1
        if last:
            h_ref, wn_ref = refs[at:at + 2]
            at += 2
        if dw_up_buf is not None:
            at += 2
        o_ref, dwu_ref, dwd_ref = refs[at:at + 3]
        at += 3
        if last:
            dnorm_ref = refs[at]
            at += 1
        acc_u, acc_d = refs[at:at + 2]
        i = pl.program_id(0)

        @pl.when(i == 0)
        def _():
            acc_u[...] = jnp.zeros_like(acc_u)
            acc_d[...] = jnp.zeros_like(acc_d)
            if last:
                dnorm_ref[...] = jnp.zeros_like(dnorm_ref)

        dh16 = dh_ref[...].astype(BF16)
        r = jnp.maximum(a_ref[...].astype(F32), 0.0)
        dact = lax.dot_general(dh16, wd_ref[...], nt, preferred_element_type=F32)
        da16 = (dact * (2.0 * r)).astype(BF16)
        acc_d[...] += lax.dot_general((r * r).astype(BF16), dh16, tn, preferred_element_type=F32)
        acc_u[...] += lax.dot_general(hn_ref[...], da16, tn, preferred_element_type=F32)
        dhn = lax.dot_general(da16, wu_ref[...], nt, preferred_element_type=F32)
        if prev_ref is not None:
            dhn = dhn + prev_ref[...]
        if last:
            d_rms, dw_rows = _rms_bwd(dhn, h_ref[...], wn_ref[...])
            o_ref[...] = dh_ref[...] + d_rms
            dnorm_ref[...] += _fold8(dw_rows)
        else:
            o_ref[...] = dhn

        @pl.when(i == n_i - 1)
        def _():
            dwu_ref[...] = acc_u[...].astype(BF16)
            dwd_ref[...] = acc_d[...].astype(BF16)

    row = lambda i: (i, 0)
    tile = pl.BlockSpec((tm, D_MODEL), row)
    in_specs = [tile, pl.BlockSpec((tm, f_sh), lambda i: (i, s)), tile,
                pl.BlockSpec((None, None, D_MODEL, f_sh), lambda i: (s, 0, 0, 0)),
                pl.BlockSpec((None, None, f_sh, D_MODEL), lambda i: (s, 0, 0, 0))]
    args = [dh, a, hn, w_up4, w_down4]
    if dhn_prev is not None:
        in_specs.append(tile)
        args.append(dhn_prev)
    if last:
        in_specs += [tile, pl.BlockSpec((1, D_MODEL), lambda i: (0, 0))]
        args += [h, wn]
    aliases = {}
    if dw_up_buf is not None:
        aliases = {len(args): 1, len(args) + 1: 2}
        in_specs += [pl.BlockSpec(memory_space=pl.ANY)] * 2
        args += [dw_up_buf, dw_down_buf]
    out_specs = [tile, pl.BlockSpec((None, D_MODEL, f_sh), lambda i: (s, 0, 0)),
                 pl.BlockSpec((None, f_sh, D_MODEL), lambda i: (s, 0, 0))]
    out_shape = [SDS((n_rows, D_MODEL), F32), SDS((n_sh, D_MODEL, f_sh), BF16), SDS((n_sh, f_sh, D_MODEL), BF16)]
    if last:
        out_specs.append(pl.BlockSpec((8, D_MODEL), lambda i: (0, 0)))
        out_shape.append(SDS((8, D_MODEL), F32))
    return pl.pallas_call(
        body, name=name, grid=(n_i,), in_specs=in_specs, out_specs=out_specs, out_shape=out_shape,
        input_output_aliases=aliases,
        scratch_shapes=[pltpu.VMEM((D_MODEL, f_sh), F32), pltpu.VMEM((f_sh, D_MODEL), F32)],
        compiler_params=_params(1),
    )(*args)


def _attn_masks(n):
    qi = lax.broadcasted_iota(jnp.int32, (BLOCK, 3 * BLOCK), 0)
    col = lax.broadcasted_iota(jnp.int32, (BLOCK, 3 * BLOCK), 1)
    kj = col - BLOCK
    dist = BLOCK + qi - kj
    kmin = jnp.where(n == 0, 2 * BLOCK, jnp.where(n == 1, BLOCK, 0))
    band_ok = (col >= BLOCK) & (dist >= 0) & (dist < BLOCK) & (kj >= kmin)
    q_pos = n * BLOCK + qi - PAD
    meta_ok = (col >= PAD) & (col < BLOCK) & (col - PAD <= q_pos)
    distf = jnp.where(col >= BLOCK, dist, 0).astype(F32)
    return band_ok | meta_ok, distf


def _alibi_slope(h):
    return float(2.0 ** (-8.0 * (h + 1) / N_HEADS))


def _attn_bias(n, bias_s):
    ok, distf = _attn_masks(n)
    for h in range(N_HEADS):
        bias_s[h] = jnp.where(ok, -_alibi_slope(h) * distf, NEG_INF)


def _attn_fwd(qkv, sinks, n_ex, nb):
    n_rows = qkv.shape[0]
    kvb = N_HEADS * HEAD_DIM // KV_DIM

    def body(sink_ref, q_ref, kvm_ref, kvp_ref, kvc_ref, o_ref, lse_ref, k_s, v_s, q_s, bias_s):
        n = pl.program_id(1)

        @pl.when(n <= 2)
        def _():
            _attn_bias(n, bias_s)

        v_s[...] = jnp.ones_like(v_s)
        for part, ref in enumerate((kvm_ref, kvp_ref, kvc_ref)):
            rows = slice(part * BLOCK, (part + 1) * BLOCK)
            k_s[rows, :] = ref[:, 0:N_KV * HEAD_DIM]
            for kv in range(N_KV):
                v_s[rows, kv * 2 * HEAD_DIM:kv * 2 * HEAD_DIM + HEAD_DIM] = \
                    ref[:, (N_KV + kv) * HEAD_DIM:(N_KV + kv + 1) * HEAD_DIM]
        for h in range(N_HEADS):
            q_s[h // GQA, (h % GQA) * BLOCK:(h % GQA + 1) * BLOCK, :] = \
                q_ref[:, h * HEAD_DIM:(h + 1) * HEAD_DIM] * (HEAD_DIM ** -0.5)

        def scores(kv):
            return lax.dot_general(q_s[kv], k_s[:, kv * HEAD_DIM:(kv + 1) * HEAD_DIM], (((1,), (1,)), ((), ())),
                                   preferred_element_type=F32)

        ahead = scores(0)
        for kv in range(N_KV):
            s4 = ahead
            if kv + 1 < N_KV:
                ahead = scores(kv + 1)
            es, ms, sink_es = [], [], []
            for g in range(GQA):
                h = kv * GQA + g
                s = s4[g * BLOCK:(g + 1) * BLOCK] + bias_s[h]
                sink = sink_ref[0, h]
                m = jnp.maximum(jnp.max(s, axis=-1, keepdims=True), sink)
                es.append(jnp.exp(s - m).astype(BF16))
                ms.append(m)
                sink_es.append(jnp.exp(sink - m))
            pv = jnp.dot(jnp.concatenate(es, axis=0), v_s[:, kv * 2 * HEAD_DIM:(kv + 1) * 2 * HEAD_DIM],
                         preferred_element_type=F32)
            for g in range(GQA):
                h = kv * GQA + g
                pg = pv[g * BLOCK:(g + 1) * BLOCK]
                l = pg[:, HEAD_DIM:HEAD_DIM + 1] + sink_es[g]
                o_ref[:, h * HEAD_DIM:(h + 1) * HEAD_DIM] = (pg[:, 0:HEAD_DIM] * (1.0 / l)).astype(BF16)
                lse_ref[:, h:h + 1] = ms[g] + jnp.log(l)

    return pl.pallas_call(
        body, name="attn_fwd", grid=(n_ex, nb),
        in_specs=[pl.BlockSpec(memory_space=pltpu.SMEM),
                  pl.BlockSpec((BLOCK, N_HEADS * HEAD_DIM), lambda b, n: (b * nb + n, 0)),
                  pl.BlockSpec((BLOCK, KV_DIM), lambda b, n: (b * nb, kvb)),
                  pl.BlockSpec((BLOCK, KV_DIM), lambda b, n: (b * nb + jnp.maximum(n - 1, 0), kvb)),
                  pl.BlockSpec((BLOCK, KV_DIM), lambda b, n: (b * nb + n, kvb))],
        out_specs=[pl.BlockSpec((BLOCK, N_HEADS * HEAD_DIM), lambda b, n: (b * nb + n, 0)),
                   pl.BlockSpec((BLOCK, N_HEADS), lambda b, n: (b * nb + n, 0))],
        out_shape=[SDS((n_rows, N_HEADS * HEAD_DIM), BF16), SDS((n_rows, N_HEADS), F32)],
        scratch_shapes=[pltpu.VMEM((3 * BLOCK, N_KV * HEAD_DIM), BF16), pltpu.VMEM((3 * BLOCK, 2 * N_KV * HEAD_DIM), BF16),
                        pltpu.VMEM((N_KV, GQA * BLOCK, HEAD_DIM), BF16), pltpu.VMEM((N_HEADS, BLOCK, 3 * BLOCK), F32)],
        compiler_params=_params(2),
    )(sinks, qkv, qkv, qkv, qkv)


def _attn_bwd(qkv, sinks, o, lse, do, n_ex, nb):
    n_rows = qkv.shape[0]
    kvb = N_HEADS * HEAD_DIM // KV_DIM
    scale = HEAD_DIM ** -0.5
    nq = lambda r: nb - 1 - r

    def body(sink_ref, q_ref, kvm_ref, kvp_ref, kvc_ref, o_ref, lse_ref, do_ref, dqkv_ref, dsink_ref,
             k_s, v_s, dkv_s, carry_s, meta_s, q_s, do_s, bias_s):
        b, r = pl.program_id(0), pl.program_id(1)
        n = nq(r)

        @pl.when((r == 0) | (n <= 1))
        def _():
            _attn_bias(n, bias_s)

        @pl.when((b == 0) & (r == 0))
        def _():
            dsink_ref[...] = jnp.zeros_like(dsink_ref)

        @pl.when(r == 0)
        def _():
            carry_s[...] = jnp.zeros_like(carry_s)
            meta_s[...] = jnp.zeros_like(meta_s)

        for part, ref in enumerate((kvm_ref, kvp_ref, kvc_ref)):
            k_s[part * BLOCK:(part + 1) * BLOCK, :] = ref[:, 0:N_KV * HEAD_DIM]
            v_s[part * BLOCK:(part + 1) * BLOCK, :] = ref[:, N_KV * HEAD_DIM:KV_DIM]
        nt = (((1,), (1,)), ((), ()))
        tn = (((0,), (0,)), ((), ()))
        deltas = []
        for h in range(N_HEADS):
            rows = slice((h % GQA) * BLOCK, (h % GQA + 1) * BLOCK)
            cols = slice(h * HEAD_DIM, (h + 1) * HEAD_DIM)
            q_s[h // GQA, rows, :] = q_ref[:, cols] * scale
            do_s[h // GQA, rows, :] = do_ref[:, cols]
            deltas.append(jnp.sum(do_ref[:, cols].astype(F32) * o_ref[:, cols].astype(F32), axis=-1, keepdims=True))
        for kv in range(N_KV):
            kcols = slice(kv * HEAD_DIM, (kv + 1) * HEAD_DIM)
            vcols = slice(N_KV * HEAD_DIM + kv * HEAD_DIM, N_KV * HEAD_DIM + (kv + 1) * HEAD_DIM)
            kh, vh = k_s[:, kcols], v_s[:, kcols]
            s4 = lax.dot_general(q_s[kv], kh, nt, preferred_element_type=F32)
            dp4 = lax.dot_general(do_s[kv], vh, nt, preferred_element_type=F32)
            ps, dss = [], []
            for g in range(GQA):
                h = kv * GQA + g
                cols = slice(h * HEAD_DIM, (h + 1) * HEAD_DIM)
                rows = slice(g * BLOCK, (g + 1) * BLOCK)
                s = s4[rows] + bias_s[h]
                lse_h = lse_ref[:, h:h + 1]
                p = jnp.exp(s - lse_h)
                delta = deltas[h]
                dsink_ref[:, h:h + 1] += -jnp.exp(sink_ref[0, h] - lse_h) * delta
                ps.append(p.astype(BF16))
                dss.append((p * (dp4[rows] - delta)).astype(BF16))
            p4, ds4 = jnp.concatenate(ps, axis=0), jnp.concatenate(dss, axis=0)
            dq4 = jnp.dot(ds4, kh, preferred_element_type=F32) * scale
            for g in range(GQA):
                cols = slice((kv * GQA + g) * HEAD_DIM, (kv * GQA + g + 1) * HEAD_DIM)
                dqkv_ref[:, cols] = dq4[g * BLOCK:(g + 1) * BLOCK].astype(BF16)
            dkv_s[:, kcols] = lax.dot_general(ds4, q_s[kv], tn, preferred_element_type=F32)
            dkv_s[:, vcols] = lax.dot_general(p4, do_s[kv], tn, preferred_element_type=F32)

        meta_s[...] += dkv_s[0:BLOCK, :]
        cur = dkv_s[2 * BLOCK:3 * BLOCK, :] + carry_s[...]
        carry_s[...] = dkv_s[BLOCK:2 * BLOCK, :]

        @pl.when(n > 0)
        def _():
            dqkv_ref[:, N_HEADS * HEAD_DIM:QKV_DIM] = cur.astype(BF16)

        @pl.when(n == 0)
        def _():
            dqkv_ref[:, N_HEADS * HEAD_DIM:QKV_DIM] = (cur + meta_s[...]).astype(BF16)

    blk = lambda b, r: (b * nb + nq(r), 0)
    return pl.pallas_call(
        body, name="attn_bwd", grid=(n_ex, nb),
        in_specs=[pl.BlockSpec(memory_space=pltpu.SMEM),
                  pl.BlockSpec((BLOCK, N_HEADS * HEAD_DIM), blk),
                  pl.BlockSpec((BLOCK, KV_DIM), lambda b, r: (b * nb, kvb)),
                  pl.BlockSpec((BLOCK, KV_DIM), lambda b, r: (b * nb + jnp.maximum(nq(r) - 1, 0), kvb)),
                  pl.BlockSpec((BLOCK, KV_DIM), lambda b, r: (b * nb + nq(r), kvb)),
                  pl.BlockSpec((BLOCK, N_HEADS * HEAD_DIM), blk),
                  pl.BlockSpec((BLOCK, N_HEADS), blk),
                  pl.BlockSpec((BLOCK, N_HEADS * HEAD_DIM), blk)],
        out_specs=[pl.BlockSpec((BLOCK, QKV_DIM), blk),
                   pl.BlockSpec((BLOCK, N_HEADS), lambda b, r: (0, 0))],
        out_shape=[SDS((n_rows, QKV_DIM), BF16), SDS((BLOCK, N_HEADS), F32)],
        scratch_shapes=[pltpu.VMEM((3 * BLOCK, N_KV * HEAD_DIM), BF16), pltpu.VMEM((3 * BLOCK, N_KV * HEAD_DIM), BF16),
                        pltpu.VMEM((3 * BLOCK, KV_DIM), F32), pltpu.VMEM((BLOCK, KV_DIM), F32),
                        pltpu.VMEM((BLOCK, KV_DIM), F32), pltpu.VMEM((N_KV, GQA * BLOCK, HEAD_DIM), BF16),
                        pltpu.VMEM((N_KV, GQA * BLOCK, HEAD_DIM), BF16), pltpu.VMEM((N_HEADS, BLOCK, 3 * BLOCK), F32)],
        compiler_params=_params(2),
    )(sinks, qkv, qkv, qkv, qkv, o, lse, do)


SSM_TILES = (2112, 1408, 384)
XW = 256 * PAIRS_PER_CHUNK


def _cmul_add(xr, xi, mr, mi, sr, si):
    return xr + mr * sr - mi * si, xi + mr * si + mi * sr


def _to_segments(src_ref, dst, seg):
    for s in range(seg):
        dst[s * 8:(s + 1) * 8, :] = src_ref[pl.ds(s, 8, stride=seg), :]


def _from_segments(src, i, seg):
    return src[pl.ds(i, seg, stride=8), :]


def _scan_segments(buf, tab_ref, carry_s, seg, reverse):
    shifts = (7, 6, 4) if reverse else (1, 2, 4)
    row_id = lax.broadcasted_iota(jnp.int32, (8, 128), 0)
    a_tiles = [tab_ref[j, c] for j in range(PAIRS_PER_CHUNK) for c in (0, 1)]

    def local(si, prev):
        s = (seg - 1 - si) if reverse else si
        row = pl.multiple_of(s * 8, 8)
        out = []
        for j in range(PAIRS_PER_CHUNK):
            re, im = slice(256 * j, 256 * j + 128), slice(256 * j + 128, 256 * j + 256)
            xr, xi = _cmul_add(buf[pl.ds(row, 8), re], buf[pl.ds(row, 8), im],
                               a_tiles[2 * j], a_tiles[2 * j + 1], prev[2 * j], prev[2 * j + 1])
            buf[pl.ds(row, 8), re] = xr
            buf[pl.ds(row, 8), im] = xi
            out += [xr, xi]
        return tuple(out)

    zero = jnp.zeros((8, 128), F32)
    edge = lax.fori_loop(0, seg, local, (zero,) * (2 * PAIRS_PER_CHUNK))

    entering = []
    for j in range(PAIRS_PER_CHUNK):
        er, ei = edge[2 * j], edge[2 * j + 1]
        if reverse:
            sr = jnp.where(row_id == 7, carry_s[2 * j], pltpu.roll(er, 7, 0))
            si_ = jnp.where(row_id == 7, carry_s[2 * j + 1], pltpu.roll(ei, 7, 0))
        else:
            sr = jnp.where(row_id == 0, carry_s[2 * j], pltpu.roll(er, 1, 0))
            si_ = jnp.where(row_id == 0, carry_s[2 * j + 1], pltpu.roll(ei, 1, 0))
        for lvl, sh in enumerate(shifts):
            sr, si_ = _cmul_add(sr, si_, tab_ref[j, 2 + 2 * lvl], tab_ref[j, 3 + 2 * lvl],
                                pltpu.roll(sr, sh, 0), pltpu.roll(si_, sh, 0))
        entering += [sr, si_]
        tr, ti = _cmul_add(er, ei, tab_ref[j, 2], tab_ref[j, 3], sr, si_)
        out_row = slice(0, 1) if reverse else slice(7, 8)
        carry_s[2 * j] = jnp.broadcast_to(tr[out_row], (8, 128))
        carry_s[2 * j + 1] = jnp.broadcast_to(ti[out_row], (8, 128))

    def fix(si, carried):
        s = (seg - 1 - si) if reverse else si
        row = pl.multiple_of(s * 8, 8)
        out = []
        for j in range(PAIRS_PER_CHUNK):
            re, im = slice(256 * j, 256 * j + 128), slice(256 * j + 128, 256 * j + 256)
            ar, ai, fr, fi = a_tiles[2 * j], a_tiles[2 * j + 1], carried[2 * j], carried[2 * j + 1]
            fr, fi = ar * fr - ai * fi, ar * fi + ai * fr
            buf[pl.ds(row, 8), re] += fr
            buf[pl.ds(row, 8), im] += fi
            out += [fr, fi]
        return tuple(out)

    lax.fori_loop(0, seg, fix, tuple(entering))


def _ssm_fwd(u, b_pad, c_pad, tab, d_skip, n_ex, lp):
    n_rows = u.shape[0]
    TM = _row_tile(lp, SSM_TILES)
    SEG = TM // 8
    n_t = lp // TM
    n_chunk = D_MODEL // 128

    def body(u_ref, bp_ref, cp_ref, tab_ref, d_ref, yg_ref, y_ref, xs_ref, up_ref, buf, carry_s, us, ys):
        @pl.when(pl.program_id(2) == 0)
        def _():
            carry_s[...] = jnp.zeros_like(carry_s)

        _to_segments(u_ref, us, SEG)
        ub = us[...]
        u16 = ub.astype(BF16)
        up_ref[...] = u16
        buf[...] = jnp.dot(u16, bp_ref[...], preferred_element_type=F32)
        _scan_segments(buf, tab_ref, carry_s, SEG, reverse=False)
        xb = buf[...].astype(BF16)
        xs_ref[...] = xb
        ys[...] = d_ref[...] * ub + jnp.dot(xb, cp_ref[...], preferred_element_type=F32)
        for i in range(8):
            yi = _from_segments(ys, i, SEG)
            y_ref[i * SEG:(i + 1) * SEG, :] = yi
            yg_ref[i * SEG:(i + 1) * SEG, :] = _gelu(yi).astype(BF16)

    rows = lambda b, q, t: (b * n_t + t, q)
    return pl.pallas_call(
        body, name="ssm_fwd", grid=(n_ex, n_chunk, n_t),
        in_specs=[pl.BlockSpec((TM, 128), rows),
                  pl.BlockSpec((None, 128, XW), lambda b, q, t: (q, 0, 0)),
                  pl.BlockSpec((None, XW, 128), lambda b, q, t: (q, 0, 0)),
                  pl.BlockSpec((PAIRS_PER_CHUNK, 8, 8, 128), lambda b, q, t: (q, 0, 0, 0)),
                  pl.BlockSpec((1, 128), lambda b, q, t: (0, q))],
        out_specs=[pl.BlockSpec((TM, 128), rows), pl.BlockSpec((TM, 128), rows),
                   pl.BlockSpec((None, TM, XW), lambda b, q, t: (q, b * n_t + t, 0)), pl.BlockSpec((TM, 128), rows)],
        out_shape=[SDS((n_rows, D_MODEL), BF16), SDS((n_rows, D_MODEL), F32), SDS((n_chunk, n_rows, XW), BF16),
                   SDS((n_rows, D_MODEL), BF16)],
        scratch_shapes=[pltpu.VMEM((TM, XW), F32), pltpu.VMEM((2 * PAIRS_PER_CHUNK, 8, 128), F32),
                        pltpu.VMEM((TM, 128), F32), pltpu.VMEM((TM, 128), F32)],
        compiler_params=_params(3),
    )(u, b_pad, c_pad, tab, d_skip)


def _ssm_bwd(dyg, y, u_seg, xs, ct_pad, bt_pad, tab_rev, d_skip, n_ex, lp):
    n_rows = u_seg.shape[0]
    TM = _row_tile(lp, SSM_TILES)
    SEG = TM // 8
    n_t = lp // TM
    n_chunk = D_MODEL // 128
    tile = lambda q, b, t: (b * n_t + (n_t - 1 - t), q)

    def body(dyg_ref, y_ref, up_ref, xs_ref, xp_ref, ct_ref, bt_ref, tab_ref, d_ref,
             du_ref, db_ref, dc_ref, da_ref, dd_ref, buf, xf, carry_s, dys, dyp):
        b, t = pl.program_id(1), pl.program_id(2)

        @pl.when((b == 0) & (t == 0))
        def _():
            db_ref[...] = jnp.zeros_like(db_ref)
            dc_ref[...] = jnp.zeros_like(dc_ref)
            da_ref[...] = jnp.zeros_like(da_ref)
            dd_ref[...] = jnp.zeros_like(dd_ref)

        @pl.when(t == 0)
        def _():
            carry_s[...] = jnp.zeros_like(carry_s)

        dys[...] = dyg_ref[...].astype(F32) * _gelu_grad(y_ref[...])
        _to_segments(dys, dyp, SEG)
        dy = dyp[...]
        u16 = up_ref[...]
        dd_ref[...] += _fold8(dy * u16.astype(F32))
        dy16 = dy.astype(BF16)
        first_tile = t == n_t - 1
        tn = (((0,), (0,)), ((), ()))
        buf[...] = jnp.dot(dy16, ct_ref[...], preferred_element_type=F32)
        dc_ref[...] += lax.dot_general(dy16, xs_ref[...], tn, preferred_element_type=F32)
        xf[16:16 + TM, :] = xs_ref[...].astype(F32)
        xf[0:16, :] = jnp.where(first_tile, 0.0, xp_ref[...].astype(F32))
        _scan_segments(buf, tab_ref, carry_s, SEG, reverse=True)
        g16 = buf[...].astype(BF16)
        dys[...] = d_ref[...] * dy + jnp.dot(g16, bt_ref[...], preferred_element_type=F32)
        db_ref[...] += lax.dot_general(u16, g16, tn, preferred_element_type=F32)
        row_id = lax.broadcasted_iota(jnp.int32, (8, 128), 0)
        for j in range(PAIRS_PER_CHUNK):
            re, im = slice(256 * j, 256 * j + 128), slice(256 * j + 128, 256 * j + 256)
            first = [jnp.where(row_id == 0, jnp.broadcast_to(xf[15:16, c], (8, 128)),
                               pltpu.roll(xf[8 + TM:16 + TM, c], 1, 0)) for c in (re, im)]
            for rows, pr, pi in ((slice(0, 8), first[0], first[1]),
                                 (slice(8, TM), xf[16:8 + TM, re], xf[16:8 + TM, im])):
                gr, gi = buf[rows, re], buf[rows, im]
                da_ref[j, 0] += _fold8(gr * pr + gi * pi)
                da_ref[j, 1] += _fold8(gi * pr - gr * pi)
        for i in range(8):
            du_ref[i * SEG:(i + 1) * SEG, :] = _from_segments(dys, i, SEG)

    prev16 = lambda q, b, t: (q, jnp.maximum((b * n_t + (n_t - 1 - t)) * (TM // 16) - 1, 0), 0)
    return pl.pallas_call(
        body, name="ssm_bwd", grid=(n_chunk, n_ex, n_t),
        in_specs=[pl.BlockSpec((TM, 128), tile), pl.BlockSpec((TM, 128), tile), pl.BlockSpec((TM, 128), tile),
                  pl.BlockSpec((None, TM, XW), lambda q, b, t: (q, b * n_t + (n_t - 1 - t), 0)),
                  pl.BlockSpec((None, 16, XW), prev16),
                  pl.BlockSpec((None, 128, XW), lambda q, b, t: (q, 0, 0)),
                  pl.BlockSpec((None, XW, 128), lambda q, b, t: (q, 0, 0)),
                  pl.BlockSpec((PAIRS_PER_CHUNK, 8, 8, 128), lambda q, b, t: (q, 0, 0, 0)),
                  pl.BlockSpec((1, 128), lambda q, b, t: (0, q))],
        out_specs=[pl.BlockSpec((TM, 128), tile),
                   pl.BlockSpec((None, 128, XW), lambda q, b, t: (q, 0, 0)),
                   pl.BlockSpec((None, 128, XW), lambda q, b, t: (q, 0, 0)),
                   pl.BlockSpec((PAIRS_PER_CHUNK, 2, 8, 128), lambda q, b, t: (q, 0, 0, 0)),
                   pl.BlockSpec((8, 128), lambda q, b, t: (0, q))],
        out_shape=[SDS((n_rows, D_MODEL), F32), SDS((n_chunk, 128, XW), F32), SDS((n_chunk, 128, XW), F32),
                   SDS((N_PAIR, 2, 8, 128), F32), SDS((8, D_MODEL), F32)],
        scratch_shapes=[pltpu.VMEM((TM, XW), F32), pltpu.VMEM((TM + 16, XW), F32),
                        pltpu.VMEM((2 * PAIRS_PER_CHUNK, 8, 128), F32), pltpu.VMEM((TM, 128), F32),
                        pltpu.VMEM((TM, 128), F32)],
        compiler_params=_params(3),
    )(dyg, y, u_seg, xs, xs, ct_pad, bt_pad, tab_rev, d_skip)


def _rms_bwd_call(dhn, h, wn, dres, name):
    n_rows = h.shape[0]

    def body(dhn_ref, h_ref, wn_ref, dres_ref, o_ref, dw_ref):
        @pl.when(pl.program_id(0) == 0)
        def _():
            dw_ref[...] = jnp.zeros_like(dw_ref)

        dh, dw_rows = _rms_bwd(dhn_ref[...], h_ref[...], wn_ref[...])
        o_ref[...] = dres_ref[...] + dh
        dw_ref[...] += _fold8(dw_rows)

    row = lambda i: (i, 0)
    return pl.pallas_call(
        body, name=name, grid=(n_rows // TM,),
        in_specs=[pl.BlockSpec((TM, D_MODEL), row), pl.BlockSpec((TM, D_MODEL), row),
                  pl.BlockSpec((1, D_MODEL), lambda i: (0, 0)), pl.BlockSpec((TM, D_MODEL), row)],
        out_specs=[pl.BlockSpec((TM, D_MODEL), row), pl.BlockSpec((8, D_MODEL), lambda i: (0, 0))],
        out_shape=[SDS((n_rows, D_MODEL), F32), SDS((8, D_MODEL), F32)], compiler_params=_params(1),
    )(dhn, h, wn, dres)


def _glu_bwd(dh, z, w4):
    n_rows = dh.shape[0]
    tm = _row_tile(n_rows, MM_TILES)
    n_sh, _, k, n = w4.shape

    def body(dh_ref, z_ref, w_ref, dz_ref, dyg_ref):
        sg = jax.nn.sigmoid(z_ref[:, D_MODEL:2 * D_MODEL].astype(F32))
        d = dh_ref[...]
        dz_ref[:, 0:D_MODEL] = (d * sg).astype(BF16)
        dz_ref[:, D_MODEL:2 * D_MODEL] = (d * z_ref[:, 0:D_MODEL].astype(F32) * sg * (1.0 - sg)).astype(BF16)
        acc = None
        for s in range(n_sh):
            part = lax.dot_general(dz_ref[:, s * n:(s + 1) * n], w_ref[s], (((1,), (1,)), ((), ())),
                                   preferred_element_type=F32)
            acc = part if acc is None else acc + part
        dyg_ref[...] = acc.astype(BF16)

    row = lambda i: (i, 0)
    return pl.pallas_call(
        body, name="glu_bwd", grid=(n_rows // tm,),
        in_specs=[pl.BlockSpec((tm, D_MODEL), row), pl.BlockSpec((tm, 2 * D_MODEL), row), _w4_spec(w4)],
        out_specs=[pl.BlockSpec((tm, 2 * D_MODEL), row), pl.BlockSpec((tm, k), row)],
        out_shape=[SDS((n_rows, 2 * D_MODEL), BF16), SDS((n_rows, k), BF16)], compiler_params=_params(1),
    )(dh, z, w4)


def _adamw(pieces, w, m, v, name):
    n_layers = len(pieces)
    rows, cols = pieces[0].shape[1:]
    rb = rows
    for cand in (256, 136, 128, 64, 32, 16, 8):
        if rows % cand == 0 and rows > cand:
            rb = cand
            break
    n_blk = rows // rb
    c1 = 1.0 / (1.0 - ADAM_B1 ** ADAM_STEP)
    c2 = 1.0 / (1.0 - ADAM_B2 ** ADAM_STEP)

    def body(*refs):
        p_refs = refs[:n_layers]
        w_ref, m_ref, v_ref, g_out, d_out, m_out, v_out = refs[n_layers:]
        layer = pl.program_id(0)
        g = None
        for l, p_ref in enumerate(p_refs):
            gl = p_ref[0].astype(F32)
            for k in range(1, N_DEV):
                gl = gl + p_ref[k].astype(F32)
            g = gl if g is None else jnp.where(layer == l, gl, g)
        m_new = ADAM_B1 * m_ref[...] + (1.0 - ADAM_B1) * g
        v_new = ADAM_B2 * v_ref[...] + (1.0 - ADAM_B2) * (g * g)
        g_out[...] = g
        m_out[...] = m_new
        v_out[...] = v_new
        d_out[...] = -ADAM_LR * ((m_new * c1) / (jnp.sqrt(v_new * c2) + ADAM_EPS) + ADAM_WD * w_ref[...])

    def piece_spec(l):
        return pl.BlockSpec((N_DEV, rb, cols), lambda ly, i: (0, jnp.where(ly == l, i, 0), 0))

    blk = pl.BlockSpec((rb, cols), lambda ly, i: (ly * n_blk + i, 0))
    return pl.pallas_call(
        body, name=name, grid=(n_layers, n_blk),
        in_specs=[piece_spec(l) for l in range(n_layers)] + [blk, blk, blk],
        out_specs=[blk, blk, blk, blk],
        out_shape=[SDS((n_layers * rows, cols), F32)] * 4, compiler_params=_params(2),
    )(*pieces, w, m, v)


_HBM = pl.BlockSpec(memory_space=pltpu.HBM)
_SEM = pl.BlockSpec(memory_space=pltpu.SEMAPHORE)
_EFFECT = pltpu.SideEffectType.DATAFLOW_SIDE_EFFECTING
N_GATHER_PEERS = N_CHIPS - 1
N_EXCHANGE_PEERS = N_DEV - 1


def _gather_copies(srcs, lands, send_sems, recv_sems):
    x, y, c = lax.axis_index("x"), lax.axis_index("y"), lax.axis_index("c")
    mine = 2 * x + y
    chips = [(1 - x, y), (x, 1 - y), (1 - x, 1 - y)]
    out, inc = [], []
    for a in range(len(srcs)):
        for k, (px, py) in enumerate(chips):
            j = a * N_GATHER_PEERS + k
            sems = dict(send_sem=send_sems.at[j], recv_sem=recv_sems.at[j], device_id=(px, py, c),
                        device_id_type=pl.DeviceIdType.MESH)
            out.append(pltpu.make_async_remote_copy(src_ref=srcs[a], dst_ref=lands[a].at[mine], **sems))
            inc.append(pltpu.make_async_remote_copy(src_ref=srcs[a], dst_ref=lands[a].at[2 * px + py], **sems))
    return out, inc


def _exchange_copies(n_scatter):
    def copies(srcs, lands, send_sems, recv_sems):
        x, y, c = lax.axis_index("x"), lax.axis_index("y"), lax.axis_index("c")
        me = 4 * x + 2 * y + c
        peers = [(x ^ (k >> 2), y ^ ((k >> 1) & 1), c ^ (k & 1)) for k in range(1, N_DEV)]
        out, inc = [], []
        for a in range(len(srcs)):
            for k, (px, py, pc) in enumerate(peers):
                j = a * N_EXCHANGE_PEERS + k
                sems = dict(send_sem=send_sems.at[j], recv_sem=recv_sems.at[j], device_id=(px, py, pc),
                            device_id_type=pl.DeviceIdType.MESH)
                theirs = srcs[a].at[2 * px + py] if a < n_scatter else srcs[a]
                mine = srcs[a].at[2 * x + y] if a < n_scatter else srcs[a]
                out.append(pltpu.make_async_remote_copy(src_ref=theirs, dst_ref=lands[a].at[me], **sems))
                inc.append(pltpu.make_async_remote_copy(src_ref=mine, dst_ref=lands[a].at[4 * px + 2 * py + pc], **sems))
        return out, inc

    return copies


def _split_start(groups, copies_fn, n_peers, name):
    sizes = [len(srcs) for srcs, _ in groups]
    flat = [a for srcs, lands in groups for a in list(srcs) + list(lands)]
    n_flat, n_grp = len(flat), len(groups)

    def body(*refs):
        sems = refs[2 * n_flat:2 * n_flat + 2 * n_grp]
        token = refs[-1]
        at = 0
        for gi, n in enumerate(sizes):
            out, _ = copies_fn(refs[at:at + n], refs[at + n:at + 2 * n], sems[2 * gi], sems[2 * gi + 1])
            for cp in out:
                cp.start()
            at += 2 * n
        token[...] = jnp.zeros_like(token)

    sem_shapes = []
    for n in sizes:
        sem_shapes += [pltpu.SemaphoreType.DMA((n * n_peers,)), pltpu.SemaphoreType.DMA((n * n_peers,))]
    res = pl.pallas_call(
        body, name=name,
        out_shape=(*[pltpu.HBM(a.shape, a.dtype) for a in flat], *sem_shapes, SDS((8, 128), F32)),
        in_specs=[_HBM] * n_flat,
        out_specs=(*[_HBM] * n_flat, *[_SEM] * (2 * n_grp), pl.BlockSpec(memory_space=pltpu.VMEM)),
        input_output_aliases={i: i for i in range(n_flat)},
        compiler_params=pltpu.CompilerParams(has_side_effects=_EFFECT),
    )(*[pltpu.with_memory_space_constraint(a, pltpu.HBM) for a in flat])
    handles, at = [], 0
    for gi, n in enumerate(sizes):
        handles.append((res[n_flat + 2 * gi], res[n_flat + 2 * gi + 1], list(res[at:at + n]), list(res[at + n:at + 2 * n])))
        at += 2 * n
    return handles, res[-1]


def _split_wait(handle, after, copies_fn, name):
    send_sems, recv_sems, srcs, lands = handle
    n = len(srcs)
    after = list(after) if isinstance(after, (list, tuple)) else [after]

    def body(*refs):
        out, inc = copies_fn(refs[:n], refs[n:2 * n], refs[2 * n], refs[2 * n + 1])
        for cp in out:
            cp.wait_send()
        for cp in inc:
            cp.wait_recv()

    flat = list(srcs) + list(lands)
    res = pl.pallas_call(
        body, name=name,
        out_shape=tuple(pltpu.HBM(a.shape, a.dtype) for a in flat),
        in_specs=[_HBM] * (2 * n) + [_SEM, _SEM] + [pl.BlockSpec(memory_space=pl.ANY)] * len(after),
        out_specs=tuple([_HBM] * (2 * n)),
        input_output_aliases={i: i for i in range(2 * n)},
        compiler_params=pltpu.CompilerParams(has_side_effects=_EFFECT),
    )(*flat, send_sems, recv_sems, *after)
    return list(res[n:])


def _landing(own, slot, n_slots):
    return lax.dynamic_update_index_in_dim(lax.empty((n_slots,) + own.shape, own.dtype), own, slot, 0)


def _ssm_discretize(lam_re, lam_im, log_dt, b_re, b_im):
    lr = jnp.minimum(lam_re, LAMBDA_RE_MAX)
    li = lam_im
    dt = jnp.exp(log_dt)[:, None]
    mag = jnp.exp(lr * dt)
    ar, ai = mag * jnp.cos(li * dt), mag * jnp.sin(li * dt)
    den = lr * lr + li * li
    nr, ni = ar - 1.0, ai
    gr, gi = (nr * lr + ni * li) / den, (ni * lr - nr * li) / den
    bbr = gr[:, :, None] * b_re - gi[:, :, None] * b_im
    bbi = gr[:, :, None] * b_im + gi[:, :, None] * b_re
    return ar, ai, bbr, bbi


def _pair_lanes(t):
    return t.reshape(N_PAIR, 2 * SSM_STATE)


def _chan_state_blocks(t_gcp):
    t = t_gcp.reshape(N_PAIR, 2, SSM_GROUP, SSM_STATE)
    eye2 = jnp.eye(2, dtype=t.dtype)
    blk = jnp.einsum("rgcp,gh->rgchp", t, eye2).reshape(N_PAIR, 2 * SSM_GROUP, 2 * SSM_STATE)
    place = jax.nn.one_hot(jnp.arange(N_PAIR) % PAIRS_PER_CHUNK, PAIRS_PER_CHUNK, dtype=t.dtype)
    return jnp.einsum("rcl,rj->rjcl", blk, place).reshape(N_PAIR, 128, 2 * SSM_STATE)


def _chan_state_unblock(t):
    t = t.reshape(N_PAIR, PAIRS_PER_CHUNK, 2, SSM_GROUP, 2, SSM_STATE)
    place = jax.nn.one_hot(jnp.arange(N_PAIR) % PAIRS_PER_CHUNK, PAIRS_PER_CHUNK, dtype=t.dtype)
    t = jnp.einsum("rjgchp,rj->rgchp", t, place)
    t = jnp.einsum("rgchp,gh->rgcp", t, jnp.eye(2, dtype=t.dtype))
    return t.reshape(SSM_NG, SSM_GROUP, SSM_STATE)


def _scan_tables(zr, zi, reverse, seg):
    zr, zi = _pair_lanes(zr), _pair_lanes(-zi if reverse else zi)
    a = (jnp.exp(zr) * jnp.cos(zi), jnp.exp(zr) * jnp.sin(zi))
    cmul = lambda p, q: (p[0] * q[0] - p[1] * q[1], p[0] * q[1] + p[1] * q[0])
    big, square, bits = None, a, seg
    while bits:
        if bits & 1:
            big = square if big is None else cmul(big, square)
        square, bits = cmul(square, square), bits >> 1
    powers = [a, big]
    for _ in range(2):
        powers.append(cmul(powers[-1], powers[-1]))
    rows = jnp.arange(8)[None, :, None]
    tiles = [jnp.broadcast_to(part[:, None, :], (N_PAIR, 8, 128)) for part in powers[0]]
    for lvl, step in enumerate((1, 2, 4)):
        keep = (rows <= 7 - step) if reverse else (rows >= step)
        for part in powers[1 + lvl]:
            tiles.append(jnp.where(keep, part[:, None, :], 0.0))
    return jnp.stack(tiles, axis=1)


def _pairs_to_chunks(t):
    n_chunk = N_PAIR // PAIRS_PER_CHUNK
    return jnp.swapaxes(t.reshape(n_chunk, PAIRS_PER_CHUNK, 128, 256), 1, 2).reshape(n_chunk, 128, XW)


def _chunks_to_pairs(t):
    n_chunk = N_PAIR // PAIRS_PER_CHUNK
    return jnp.swapaxes(t.reshape(n_chunk, 128, PAIRS_PER_CHUNK, 256), 1, 2).reshape(N_PAIR, 128, 256)


def _ssm_operands(w, lp):
    seg = _row_tile(lp, SSM_TILES) // 8
    ar, ai, bbr, bbi = _ssm_discretize(w["ssm_lambda_re"], w["ssm_lambda_im"], w["ssm_log_dt"], w["ssm_b_re"], w["ssm_b_im"])
    b_blk = jnp.concatenate([_chan_state_blocks(jnp.swapaxes(bbr, 1, 2)), _chan_state_blocks(jnp.swapaxes(bbi, 1, 2))], axis=2)
    c_blk = jnp.concatenate([_chan_state_blocks(w["ssm_c_re"]), -_chan_state_blocks(w["ssm_c_im"])], axis=2)
    dt = jnp.exp(w["ssm_log_dt"])[:, None]
    zr, zi = jnp.minimum(w["ssm_lambda_re"], LAMBDA_RE_MAX) * dt, w["ssm_lambda_im"] * dt
    b_cat, c_cat = _pairs_to_chunks(b_blk).astype(BF16), _pairs_to_chunks(c_blk).astype(BF16)
    return (b_cat, jnp.swapaxes(b_cat, 1, 2), c_cat, jnp.swapaxes(c_cat, 1, 2),
            _scan_tables(zr, zi, False, seg), _scan_tables(zr, zi, True, seg))


def _local_step(x, target, w, late_weights, on_grads):
    n_ex, seq, _ = x.shape
    lp = seq + BLOCK
    nb = lp // BLOCK
    n_rows = n_ex * lp
    g = {}

    head = jnp.concatenate([jnp.zeros((PAD, D_MODEL), F32), w["meta_tokens"]], axis=0)
    h0 = jnp.concatenate([jnp.broadcast_to(head[None], (n_ex, BLOCK, D_MODEL)), x], axis=1).reshape(n_rows, D_MODEL)

    qkv, hn_a = _rms_mm_cols(h0, w["attn_norm_w"], w["attn_w_qkv"], "qkv_fwd")
    att, lse = _attn_fwd(qkv, w["attn_sinks"], n_ex, nb)
    h1 = _mm_acc(att, w["attn_w_o"], False, "attn_out_fwd", res=h0)
    w = {**w, **late_weights(0, att)}
    h2, a0, hn_m0, u = _mlp_fwd(h1, w["mlp_norm_w"][0:1], w["mlp_w_up"][0], w["mlp_w_down"][0], "mlp0_fwd",
                                next_norm=w["ssm_norm_w"])
    late = late_weights(1, h2)
    w["ssm_w_glu"] = late["ssm_w_glu"]
    w["mlp_w_up"], w["mlp_w_down"] = w["mlp_w_up"] + late["mlp_w_up"], w["mlp_w_down"] + late["mlp_w_down"]

    ops = w["ssm_operands"] if "ssm_operands" in w else _ssm_operands(w, lp)
    b_pad, bt_pad, ct_pad, c_pad, tab_fwd, tab_rev = ops
    yg, y, xs, u_seg = _ssm_fwd(u, b_pad, c_pad, tab_fwd, w["ssm_d"], n_ex, lp)
    z = _mm_cols(yg, w["ssm_w_glu"], False, "glu_mm_fwd")
    dh4, a1, hn_m1, h3, loss_tile, dnorm_f = _mlp_fwd(
        h2, w["mlp_norm_w"][1:2], w["mlp_w_up"][1], w["mlp_w_down"][1], "mlp1_fwd", glu_z=z,
        loss=(w["final_norm_w"], target.reshape(n_ex * seq, D_MODEL), lp))

    def mlp_bwd(dh_out, h_in, a, hn, layer, tag, norm_w):
        dhn, dw_up, dw_down = None, None, None
        for s in range(N_CHIPS):
            final = s == N_CHIPS - 1
            res = _mlp_bwd_shard(s, dh_out, a, hn, dhn, h_in if final else None, norm_w,
                                 w["mlp_w_up"][layer], w["mlp_w_down"][layer], dw_up, dw_down, f"{tag}_bwd{s}")
            dhn, dw_up, dw_down = res[:3]
        return dhn, res[3], dw_up, dw_down

    dh3, dnorm_m1, dwu1, dwd1 = mlp_bwd(dh4, h3, a1, hn_m1, 1, "mlp1", w["mlp_norm_w"][1:2])
    tok = on_grads("mlp1", {"mlp_w_up": dwu1, "mlp_w_down": dwd1})
    dz, dyg = _glu_bwd(dh3, z, w["ssm_w_glu"])
    g["ssm_w_glu"] = _mm_tn(yg, dz, N_CHIPS, False, "glu_mm_dw")
    du, db_blk, dc_blk, da_t, dd_t = _ssm_bwd(dyg, y, u_seg, xs, ct_pad, bt_pad, tab_rev, w["ssm_d"] + tok, n_ex, lp)
    dh2, dnorm_s = _rms_bwd_call(du, h2, w["ssm_norm_w"], dh3, "ssm_norm_bwd")
    db_blk, dc_blk = _chunks_to_pairs(db_blk), _chunks_to_pairs(dc_blk)
    g["ssm_c_re"] = _chan_state_unblock(dc_blk[:, :, 0:128])
    g["ssm_c_im"] = -_chan_state_unblock(dc_blk[:, :, 128:256])
    g_bbr = jnp.swapaxes(_chan_state_unblock(db_blk[:, :, 0:128]), 1, 2)
    g_bbi = jnp.swapaxes(_chan_state_unblock(db_blk[:, :, 128:256]), 1, 2)
    g_a = jnp.sum(da_t, axis=2).reshape(N_PAIR, 2, 2, SSM_STATE)
    g_ar, g_ai = g_a[:, 0].reshape(SSM_NG, SSM_STATE), g_a[:, 1].reshape(SSM_NG, SSM_STATE)
    _, vjp = jax.vjp(_ssm_discretize, w["ssm_lambda_re"], w["ssm_lambda_im"], w["ssm_log_dt"], w["ssm_b_re"], w["ssm_b_im"])
    g["ssm_lambda_re"], g["ssm_lambda_im"], g["ssm_log_dt"], g["ssm_b_re"], g["ssm_b_im"] = vjp((g_ar, g_ai, g_bbr, g_bbi))
    tok = on_grads("ssm", g)
    g = {}
    dh1, dnorm_m0, dwu0, dwd0 = mlp_bwd(dh2, h1, a0, hn_m0, 0, "mlp0", w["mlp_norm_w"][0:1] + tok)
    datt = _mm_cols(dh1, w["attn_w_o"], True, "attn_out_dx")
    dw_o = _mm_tn(att, dh1, N_CHIPS, True, "attn_out_dw")
    tok = on_grads("mlp0", {"mlp_w_up": dwu0, "mlp_w_down": dwd0, "attn_w_o": dw_o})
    dqkv, dsink_rows = _attn_bwd(qkv, w["attn_sinks"] + tok, att, lse, datt, n_ex, nb)
    tok = on_grads("qkv", {"attn_w_qkv": _mm_tn(hn_a, dqkv, N_CHIPS, False, "qkv_dw")})
    dh0, dnorm_a = _mm_acc(dqkv, w["attn_w_qkv"], True, "qkv_dx", rms_bwd=(h0, w["attn_norm_w"] + tok, dh1))

    dh0 = dh0.reshape(n_ex, lp, D_MODEL)
    on_grads("rest", {
        "mlp_norm_w": jnp.stack([jnp.sum(dnorm_m0, axis=0), jnp.sum(dnorm_m1, axis=0)]),
        "final_norm_w": jnp.sum(dnorm_f, axis=0),
        "attn_norm_w": jnp.sum(dnorm_a, axis=0)[None],
        "ssm_norm_w": jnp.sum(dnorm_s, axis=0)[None],
        "attn_sinks": jnp.sum(dsink_rows, axis=0)[None],
        "ssm_d": jnp.sum(dd_t, axis=0)[None],
        "meta_tokens": jnp.sum(dh0[:, PAD:BLOCK], axis=0),
        "loss": loss_tile[0, 0:1]})
    return loss_tile, dh0[:, BLOCK:]


_SHARDED_SMALL = ("meta_tokens", "ssm_norm_w", "ssm_d")
_REP_SSM = ("ssm_lambda_re", "ssm_lambda_im", "ssm_log_dt", "ssm_b_re", "ssm_b_im", "ssm_c_re", "ssm_c_im")
_REP_MISC = ("attn_norm_w", "attn_sinks", "mlp_norm_w", "final_norm_w")
_BIG = ("attn_w_qkv", "attn_w_o", "ssm_w_glu", "mlp_w_up", "mlp_w_down")


def _pack(parts, cols):
    flat = jnp.concatenate([p.reshape(-1) for p in parts])
    rows = -(-flat.shape[0] // (8 * cols)) * 8
    return jnp.pad(flat, (0, rows * cols - flat.shape[0])).reshape(rows, cols)


def _unpack(packed, like):
    flat = packed.reshape(-1)
    out, at = [], 0
    for p in like:
        out.append(flat[at:at + p.size].reshape(p.shape))
        at += p.size
    return out


def kernel(x, meta_tokens, attn_norm_w, attn_w_qkv, attn_sinks, attn_w_o, ssm_norm_w, ssm_lambda_re, ssm_lambda_im, ssm_log_dt, ssm_b_re, ssm_b_im, ssm_c_re, ssm_c_im, ssm_d, ssm_w_glu, mlp_norm_w, mlp_w_up, mlp_w_down, final_norm_w, loss_target, m_meta_tokens, m_attn_norm_w, m_attn_w_qkv, m_attn_sinks, m_attn_w_o, m_ssm_norm_w, m_ssm_lambda_re, m_ssm_lambda_im, m_ssm_log_dt, m_ssm_b_re, m_ssm_b_im, m_ssm_c_re, m_ssm_c_im, m_ssm_d, m_ssm_w_glu, m_mlp_norm_w, m_mlp_w_up, m_mlp_w_down, m_final_norm_w, v_meta_tokens, v_attn_norm_w, v_attn_w_qkv, v_attn_sinks, v_attn_w_o, v_ssm_norm_w, v_ssm_lambda_re, v_ssm_lambda_im, v_ssm_log_dt, v_ssm_b_re, v_ssm_b_im, v_ssm_c_re, v_ssm_c_im, v_ssm_d, v_ssm_w_glu, v_mlp_norm_w, v_mlp_w_up, v_mlp_w_down, v_final_norm_w):
    names = ("meta_tokens", "attn_norm_w", "attn_w_qkv", "attn_sinks", "attn_w_o", "ssm_norm_w", "ssm_lambda_re",
             "ssm_lambda_im", "ssm_log_dt", "ssm_b_re", "ssm_b_im", "ssm_c_re", "ssm_c_im", "ssm_d", "ssm_w_glu",
             "mlp_norm_w", "mlp_w_up", "mlp_w_down", "final_norm_w")
    wts = dict(zip(names, (meta_tokens, attn_norm_w, attn_w_qkv, attn_sinks, attn_w_o, ssm_norm_w, ssm_lambda_re,
                           ssm_lambda_im, ssm_log_dt, ssm_b_re, ssm_b_im, ssm_c_re, ssm_c_im, ssm_d, ssm_w_glu,
                           mlp_norm_w, mlp_w_up, mlp_w_down, final_norm_w)))
    mom = dict(zip(names, (m_meta_tokens, m_attn_norm_w, m_attn_w_qkv, m_attn_sinks, m_attn_w_o, m_ssm_norm_w,
                           m_ssm_lambda_re, m_ssm_lambda_im, m_ssm_log_dt, m_ssm_b_re, m_ssm_b_im, m_ssm_c_re,
                           m_ssm_c_im, m_ssm_d, m_ssm_w_glu, m_mlp_norm_w, m_mlp_w_up, m_mlp_w_down, m_final_norm_w)))
    var = dict(zip(names, (v_meta_tokens, v_attn_norm_w, v_attn_w_qkv, v_attn_sinks, v_attn_w_o, v_ssm_norm_w,
                           v_ssm_lambda_re, v_ssm_lambda_im, v_ssm_log_dt, v_ssm_b_re, v_ssm_b_im, v_ssm_c_re,
                           v_ssm_c_im, v_ssm_d, v_ssm_w_glu, v_mlp_norm_w, v_mlp_w_up, v_mlp_w_down, v_final_norm_w)))

    my_chip = 2 * lax.axis_index("x") + lax.axis_index("y")
    my_dev = 2 * my_chip + lax.axis_index("c")
    small_mine = _pack([wts[n] for n in _SHARDED_SMALL], 128)
    first = [attn_w_qkv.astype(BF16), attn_w_o.astype(BF16), small_mine]
    with_landing = lambda srcs: (srcs, [_landing(a, my_chip, N_CHIPS) for a in srcs])
    handles, token = _split_start([with_landing(first)], _gather_copies, N_GATHER_PEERS, "gather_start_first")
    up16, down16 = (mlp_w_up + token[0, 0]).astype(BF16), (mlp_w_down + token[0, 0]).astype(BF16)
    mlp0 = [up16[0:1], down16[0:1]]
    rest = [(ssm_w_glu + token[0, 0]).astype(BF16), up16[1:2], down16[1:2]]
    later, _ = _split_start([with_landing(mlp0), with_landing(rest)], _gather_copies, N_GATHER_PEERS, "gather_start_later")
    handles = handles + later
    full = {n: wts[n] for n in _REP_MISC}
    full["final_norm_w"] = final_norm_w[None]
    for n in _REP_SSM:
        full[n] = wts[n][0]
    full["ssm_operands"] = _ssm_operands(full, x.shape[1] + BLOCK)
    got = _split_wait(handles[0], full["ssm_operands"], _gather_copies, "gather_wait_first")
    full["attn_w_qkv"], full["attn_w_o"] = got[0], got[1]
    smalls = [_unpack(got[2][s], [wts[n] for n in _SHARDED_SMALL]) for s in range(N_CHIPS)]
    for k, n in enumerate(_SHARDED_SMALL):
        full[n] = jnp.concatenate([smalls[s][k] for s in range(N_CHIPS)], axis=1)

    def late_weights(stage, after):
        if stage == 0:
            up, down = _split_wait(handles[1], after, _gather_copies, "gather_wait_mlp0")
            return {"mlp_w_up": [up], "mlp_w_down": [down]}
        glu, up, down = _split_wait(handles[2], after, _gather_copies, "gather_wait_rest")
        return {"ssm_w_glu": glu, "mlp_w_up": [up], "mlp_w_down": [down]}

    def shard_cols(t):
        return jnp.swapaxes(t.reshape(t.shape[0], N_CHIPS, t.shape[1] // N_CHIPS), 0, 1)

    pending = {}

    def on_grads(tag, g):
        scatter = [g[n] for n in _BIG if n in g]
        whole = []
        if tag == "ssm":
            whole = [_pack([g[n] for n in _REP_SSM], D_MODEL)]
        if tag == "rest":
            parts = [shard_cols(g[n]) for n in _SHARDED_SMALL]
            scatter = [jnp.stack([_pack([p[s] for p in parts], 128) for s in range(N_CHIPS)])]
            whole = [_pack([g[n] for n in _REP_MISC] + [g["loss"]], D_MODEL)]
        srcs = scatter + whole
        lands = [_landing(lax.dynamic_index_in_dim(a, my_chip, 0, keepdims=False), my_dev, N_DEV) for a in scatter]
        lands += [_landing(a, my_dev, N_DEV) for a in whole]
        hs, token = _split_start([(srcs, lands)], _exchange_copies(len(scatter)), N_EXCHANGE_PEERS, "exchange_start_" + tag)
        pending[tag] = (hs[0], len(scatter))
        return token[0, 0]

    _, grad_x = _local_step(x, loss_target, full, late_weights, on_grads)

    recv = {}

    def wait_for(tag, after):
        handle, n_scatter = pending[tag]
        recv[tag] = _split_wait(handle, after, _exchange_copies(n_scatter), "exchange_wait_" + tag)

    for tag in ("mlp1", "ssm", "mlp0"):
        wait_for(tag, grad_x)

    out = {}

    def update(tag, pieces, w2, m2, v2):
        return _adamw(pieces, w2, m2, v2, "adamw_" + tag)

    def update_weight(n, pieces):
        shp = wts[n].shape
        r2 = (math.prod(shp[:-1]), shp[-1])
        res = update(n, pieces, wts[n].reshape(r2), mom[n].reshape(r2), var[n].reshape(r2))
        out[n] = [t.reshape(shp) for t in res]

    update_weight("mlp_w_up", [recv["mlp0"][1], recv["mlp1"][0]])
    update_weight("mlp_w_down", [recv["mlp0"][2], recv["mlp1"][1]])
    update_weight("attn_w_o", [recv["mlp0"][0]])
    update_weight("ssm_w_glu", [recv["ssm"][0]])
    for tag in ("qkv", "rest"):
        wait_for(tag, [out[n][1] for n in ("mlp_w_up", "mlp_w_down", "attn_w_o", "ssm_w_glu")])
    loss = jnp.sum(recv["rest"][1].reshape(N_DEV, -1)[:, sum(wts[n].size for n in _REP_MISC)])
    update_weight("attn_w_qkv", [recv["qkv"][0]])
    for tag, group, pieces, cols in (("small", _SHARDED_SMALL, recv["rest"][0], 128),
                                     ("rep_ssm", _REP_SSM, recv["ssm"][1], D_MODEL),
                                     ("rep_misc", _REP_MISC, recv["rest"][1], D_MODEL)):
        like = [wts[n] for n in group]
        res = update(tag, [pieces], _pack(like, cols), _pack([mom[n] for n in group], cols),
                     _pack([var[n] for n in group], cols))
        for k, n in enumerate(group):
            out[n] = [_unpack(t, like)[k] for t in res]

    return (loss, grad_x, *[out[n][0] for n in names], *[out[n][1] for n in names],
            *[out[n][2] for n in names], *[out[n][3] for n in names])
```
